```python
import jax, jax.numpy as jnp
from jax import lax
import numpy as np

D_MODEL = 1024
BATCH = 4
SEQ = 8192
DEPTH = 1
DEC_BATCH = 16
DEC_SEQ = 16
PAST_LEN = 4096

CHUNK = 64
CONV_CH = 512
CONV_W = 31
GLA_HEADS = 4
GLA_DK = 64
GLA_DV = 128
GLA_K = GLA_HEADS * GLA_DK
GLA_V = GLA_HEADS * GLA_DV
GATE_RANK = 16
GATE_NORM = 16.0
MIX_W = CONV_CH + GLA_V
OFF_UV = 0
OFF_UG = OFF_UV + CONV_CH
OFF_Q = OFF_UG + CONV_CH
OFF_K = OFF_Q + GLA_K
OFF_V = OFF_K + GLA_K
OFF_G = OFF_V + GLA_V
OFF_A = OFF_G + GLA_V
IN_COLS = OFF_A + GATE_RANK
N_EXPERTS = 32
TOP_K = 4
D_FF = 1024
SWIGLU_LIMIT = 7.0
SWIGLU_ALPHA = 1.702
EXPERT_BLOCK = 256
LN_EPS = 1e-5
DN_ALPHA = (2 * DEPTH) ** 0.25
DN_BETA = (8 * DEPTH) ** -0.25

kernel_name = "hybrid_conformer_gla_moe_stream_step"


def layer_norm(x, g, b):
    xf = x.astype(jnp.float32)
    mu = jnp.mean(xf, axis=-1, keepdims=True)
    var = jnp.mean(jnp.square(xf - mu), axis=-1, keepdims=True)
    y = (xf - mu) * lax.rsqrt(var + LN_EPS) * g.astype(jnp.float32) + b.astype(jnp.float32)
    return y.astype(x.dtype)


def gla_chunk(S, q, k, v, log_a):
    C = q.shape[2]
    b = jnp.cumsum(log_a, axis=2)
    causal = jnp.tril(jnp.ones((C, C), dtype=bool))[None, None, :, :, None]
    diff = b[:, :, :, None, :] - b[:, :, None, :, :]
    decay = jnp.exp(jnp.where(causal, diff, -jnp.inf))
    attn = jnp.einsum('bhik,bhjk,bhijk->bhij', q, k, decay)
    o = jnp.einsum('bhij,bhjv->bhiv', attn, v) + jnp.einsum('bhik,bhkv->bhiv', q * jnp.exp(b), S)
    b_last = b[:, :, -1:, :]
    S_new = jnp.exp(b_last[:, :, 0, :])[..., None] * S + jnp.einsum(
        'bhjk,bhjv->bhkv', k * jnp.exp(b_last - b), v)
    return S_new, o


def gla_scan(S0, q, k, v, log_a):
    B, H, T, _ = q.shape
    c = min(CHUNK, T)
    n = T // c

    def to_chunks(a):
        return a.reshape(B, H, n, c, a.shape[-1]).transpose(2, 0, 1, 3, 4)

    S, o = lax.scan(lambda s, xs: gla_chunk(s, *xs), S0,
                    (to_chunks(q), to_chunks(k), to_chunks(v), to_chunks(log_a)))
    o = o.transpose(1, 2, 0, 3, 4).reshape(B, H, T, GLA_DV)
    return S, o


def mixer(x, conv_state, gla_state, w_in, w_dw, b_dw, ln_conv_g, ln_conv_b,
          w_gate_lr, b_gate, gla_norm_g, w_o):
    B, T, _ = x.shape
    h = x @ w_in
    u = h[..., OFF_UV:OFF_UG] * jax.nn.sigmoid(h[..., OFF_UG:OFF_Q])
    ctx = jnp.concatenate([conv_state.astype(u.dtype), u], axis=1)
    new_conv = ctx[:, -(CONV_W - 1):]
    c = lax.conv_general_dilated(ctx, w_dw[:, None, :].astype(u.dtype), (1,), 'VALID',
                                 dimension_numbers=('NWC', 'WIO', 'NWC'),
                                 feature_group_count=CONV_CH) + b_dw
    c = jax.nn.silu(layer_norm(c, ln_conv_g, ln_conv_b))
    f32 = jnp.float32
    q = h[..., OFF_Q:OFF_K].astype(f32) * (GLA_DK ** -0.5)
    k = h[..., OFF_K:OFF_V].astype(f32)
    v = h[..., OFF_V:OFF_G].astype(f32)
    g = h[..., OFF_G:OFF_A]
    a = h[..., OFF_A:IN_COLS] @ w_gate_lr + b_gate
    log_a = jax.nn.log_sigmoid(a.astype(f32)) / GATE_NORM

    def heads(t, d):
        return t.reshape(B, T, GLA_HEADS, d).transpose(0, 2, 1, 3)

    S_new, o = gla_scan(gla_state.astype(f32), heads(q, GLA_DK), heads(k, GLA_DK),
                        heads(v, GLA_DV), heads(log_a, GLA_DK))
    o = o.transpose(0, 2, 1, 3)
    o = o * lax.rsqrt(jnp.mean(jnp.square(o), axis=-1, keepdims=True) + LN_EPS)
    o = o * gla_norm_g.astype(f32).reshape(GLA_HEADS, GLA_DV)
    o = o.reshape(B, T, GLA_V).astype(x.dtype) * jax.nn.silu(g)
    y = jnp.concatenate([c, o], axis=-1) @ w_o
    return y, new_conv, S_new.astype(x.dtype)


def moe(x, w_router, b_router, w_gu, b_gu, w_down, b_down):
    T, D = x.shape
    f32 = jnp.float32
    logits = x.astype(f32) @ w_router.astype(f32) + b_router.astype(f32)
    top_val, top_idx = lax.top_k(logits, TOP_K)
    gates = jax.nn.softmax(top_val, axis=-1)
    N = T * TOP_K
    e_flat = top_idx.reshape(N)
    tok_flat = jnp.arange(N, dtype=jnp.int32) // TOP_K
    g_flat = gates.reshape(N)
    order = jnp.argsort(e_flat)
    e_sorted = e_flat[order]
    counts = jnp.bincount(e_flat, length=N_EXPERTS)
    padded = (counts + EXPERT_BLOCK - 1) // EXPERT_BLOCK * EXPERT_BLOCK
    pad_end = jnp.cumsum(padded)
    pad_start = pad_end - padded
    start = jnp.cumsum(counts) - counts
    dest = pad_start[e_sorted] + jnp.arange(N) - start[e_sorted]
    NB = -(-N // EXPERT_BLOCK) + N_EXPERTS
    slot_tok = jnp.full((NB * EXPERT_BLOCK,), T, jnp.int32).at[dest].set(tok_flat[order])
    slot_gate = jnp.zeros((NB * EXPERT_BLOCK,), f32).at[dest].set(g_flat[order])
    block_e = jnp.minimum(jnp.searchsorted(pad_end, jnp.arange(NB) * EXPERT_BLOCK, side='right'),
                          N_EXPERTS - 1)
    x_pad = jnp.concatenate([x, jnp.zeros((1, D), x.dtype)], axis=0)

    def expert_block(args):
        toks, e = args
        hgu = x_pad[toks] @ w_gu[e] + b_gu[e]
        gate = jnp.minimum(hgu[:, :D_FF], SWIGLU_LIMIT)
        up = jnp.clip(hgu[:, D_FF:], -SWIGLU_LIMIT, SWIGLU_LIMIT)
        act = (up + 1) * gate * jax.nn.sigmoid(SWIGLU_ALPHA * gate)
        return act @ w_down[e] + b_down[e]

    yb = lax.map(expert_block, (slot_tok.reshape(NB, EXPERT_BLOCK), block_e))
    y = jnp.zeros((T + 1, D), f32).at[slot_tok].add(
        yb.reshape(NB * EXPERT_BLOCK, D).astype(f32) * slot_gate[:, None])
    return y[:T].astype(x.dtype)


def layer(x, conv_state, gla_state, w_in, w_dw, b_dw, ln_conv_g, ln_conv_b, w_gate_lr, b_gate,
          gla_norm_g, w_o, ln1_g, ln1_b, w_router, b_router, w_gu, b_gu, w_down, b_down,
          ln2_g, ln2_b):
    B, T, D = x.shape
    h, new_conv, new_gla = mixer(x, conv_state, gla_state, w_in, w_dw, b_dw, ln_conv_g, ln_conv_b,
                                 w_gate_lr, b_gate, gla_norm_g, w_o)
    x = layer_norm(DN_ALPHA * x + h, ln1_g, ln1_b)
    m = moe(x.reshape(B * T, D), w_router, b_router, w_gu, b_gu, w_down, b_down).reshape(B, T, D)
    x = layer_norm(DN_ALPHA * x + m, ln2_g, ln2_b)
    return x, new_conv, new_gla


def setup_inputs(seed: int = 0) -> dict:
    key = jax.random.key(seed)
    ks = jax.random.split(key, 24)
    nrm = lambda k, s: jax.random.normal(k, s, jnp.float32)
    return {
        "x_prompt": nrm(ks[0], (BATCH, SEQ, D_MODEL)),
        "x_sample": nrm(ks[1], (DEC_BATCH, DEC_SEQ, D_MODEL)),
        "state_conv": 0.5 * nrm(ks[2], (DEPTH, DEC_BATCH, CONV_W - 1, CONV_CH)),
        "state_gla": 0.5 * nrm(ks[3], (DEPTH, DEC_BATCH, GLA_HEADS, GLA_DK, GLA_DV)),
        "w_in": nrm(ks[4], (DEPTH, D_MODEL, IN_COLS)) * D_MODEL ** -0.5,
        "w_dw": nrm(ks[5], (DEPTH, CONV_W, CONV_CH)) * CONV_W ** -0.5,
        "b_dw": 0.01 * nrm(ks[6], (DEPTH, CONV_CH)),
        "ln_conv_g": 1.0 + 0.01 * nrm(ks[7], (DEPTH, CONV_CH)),
        "ln_conv_b": 0.01 * nrm(ks[8], (DEPTH, CONV_CH)),
        "w_gate_lr": nrm(ks[9], (DEPTH, GATE_RANK, GLA_K)) * GATE_RANK ** -0.5,
        "b_gate": 0.01 * nrm(ks[10], (DEPTH, GLA_K)),
        "gla_norm_g": 1.0 + 0.01 * nrm(ks[11], (DEPTH, GLA_V)),
        "w_o": nrm(ks[12], (DEPTH, MIX_W, D_MODEL)) * MIX_W ** -0.5 * DN_BETA,
        "ln1_g": 1.0 + 0.01 * nrm(ks[13], (DEPTH, D_MODEL)),
        "ln1_b": 0.01 * nrm(ks[14], (DEPTH, D_MODEL)),
        "w_router": nrm(ks[15], (DEPTH, D_MODEL, N_EXPERTS)) * D_MODEL ** -0.5,
        "b_router": 0.01 * nrm(ks[16], (DEPTH, N_EXPERTS)),
        "w_gu": nrm(ks[17], (DEPTH, N_EXPERTS, D_MODEL, 2 * D_FF)) * D_MODEL ** -0.5,
        "b_gu": 0.01 * nrm(ks[18], (DEPTH, N_EXPERTS, 2 * D_FF)),
        "w_down": nrm(ks[19], (DEPTH, N_EXPERTS, D_FF, D_MODEL)) * D_FF ** -0.5 * DN_BETA,
        "b_down": 0.01 * nrm(ks[20], (DEPTH, N_EXPERTS, D_MODEL)),
        "ln2_g": 1.0 + 0.01 * nrm(ks[21], (DEPTH, D_MODEL)),
        "ln2_b": 0.01 * nrm(ks[22], (DEPTH, D_MODEL)),
    }


def reference(x_prompt, x_sample, state_conv, state_gla, w_in, w_dw, b_dw, ln_conv_g, ln_conv_b,
              w_gate_lr, b_gate, gla_norm_g, w_o, ln1_g, ln1_b, w_router, b_router, w_gu, b_gu,
              w_down, b_down, ln2_g, ln2_b):
    xp = x_prompt
    xs = x_sample
    conv_p, gla_p, conv_s, gla_s = [], [], [], []
    for l in range(DEPTH):
        params = (w_in[l], w_dw[l], b_dw[l], ln_conv_g[l], ln_conv_b[l], w_gate_lr[l], b_gate[l],
                  gla_norm_g[l], w_o[l], ln1_g[l], ln1_b[l], w_router[l], b_router[l], w_gu[l],
                  b_gu[l], w_down[l], b_down[l], ln2_g[l], ln2_b[l])
        zc = jnp.zeros((xp.shape[0], CONV_W - 1, CONV_CH), xp.dtype)
        zs = jnp.zeros((xp.shape[0], GLA_HEADS, GLA_DK, GLA_DV), jnp.float32)
        xp, cp, sp = layer(xp, zc, zs, *params)
        xs, cs, ss = layer(xs, state_conv[l], state_gla[l], *params)
        conv_p.append(cp)
        gla_p.append(sp)
        conv_s.append(cs)
        gla_s.append(ss)
    return (xp, xs, jnp.stack(conv_p), jnp.stack(gla_p), jnp.stack(conv_s), jnp.stack(gla_s))
```

```python
import functools

import jax
import jax.numpy as jnp
from jax import lax
from jax.experimental import pallas as pl
from jax.experimental.pallas import tpu as pltpu

F32 = jnp.float32
BF16 = jnp.bfloat16

D_MODEL = 1024
CHUNK = 64
SUB = 16
CONV_CH = 512
CONV_W = 31
HALO = 32
GLA_HEADS = 4
GLA_DK = 64
GLA_DV = 128
GLA_K = GLA_HEADS * GLA_DK
GLA_V = GLA_HEADS * GLA_DV
GATE_RANK = 16
GATE_PAD = 128
GATE_NORM = 16.0
MIX_W = CONV_CH + GLA_V
OFF_UV = 0
OFF_UG = OFF_UV + CONV_CH
OFF_Q = OFF_UG + CONV_CH
OFF_K = OFF_Q + GLA_K
OFF_V = OFF_K + GLA_K
OFF_G = OFF_V + GLA_V
OFF_A = OFF_G + GLA_V
IN_COLS_PAD = OFF_A + GATE_PAD
N_EXPERTS = 32
TOP_K = 4
D_FF = 1024
SWIGLU_LIMIT = 7.0
SWIGLU_ALPHA = 1.702
EXPERT_BLOCK = 256
LN_EPS = 1e-5
DEPTH = 1
DN_ALPHA = (2 * DEPTH) ** 0.25
VMEM_LIMIT = 56 * 1024 * 1024


def _dot(a, b):
    return jnp.dot(a, b, preferred_element_type=F32)


def _dot_nt(a, b):
    return lax.dot_general(a, b, (((1,), (1,)), ((), ())), preferred_element_type=F32)


def _dot_tn(a, b):
    return lax.dot_general(a, b, (((0,), (0,)), ((), ())), preferred_element_type=F32)


def _layer_norm(x, g, b):
    mu = jnp.mean(x, axis=-1, keepdims=True)
    xc = x - mu
    var = jnp.mean(xc * xc, axis=-1, keepdims=True)
    return xc * lax.rsqrt(var + LN_EPS) * g + b


def _sigmoid(x):
    return 1.0 / (1.0 + jnp.exp(-x))


def _split_bf16(x):
    hi = x.astype(BF16)
    lo = (x - hi.astype(F32)).astype(BF16)
    return hi, lo


def _mixer_kernel(x_ref, cs_ref, gs_ref, w_in_ref, w_dw_ref, b_dw_ref, lncg_ref, lncb_ref,
                  wg_ref, bg_ref, gng_ref, w_o_ref, ln1g_ref, ln1b_ref, wr_ref, br_ref,
                  x1_ref, logit_ref, conv_out_ref, gla_out_ref,
                  ubuf, qbuf, kbuf, bbuf, labuf, vbuf, gbuf, mixbuf, st_ref, *, tm, chunk):
    t = pl.program_id(1)
    nt = pl.num_programs(1)
    n_chunks = tm // chunk
    n_sub = chunk // SUB

    @pl.when(t == 0)
    def _init():
        ubuf[0:HALO - (CONV_W - 1), :] = jnp.zeros((HALO - (CONV_W - 1), CONV_CH), F32)
        ubuf[HALO - (CONV_W - 1):HALO, :] = cs_ref[0]
        kbuf[0:SUB, :] = jnp.zeros((SUB, GLA_K), F32)
        bbuf[0:SUB, :] = jnp.zeros((SUB, GLA_K), F32)
        vbuf[0:SUB, :] = jnp.zeros((SUB, GLA_V), F32)
        for h in range(GLA_HEADS):
            st_ref[:, h * GLA_DK:(h + 1) * GLA_DK] = gs_ref[0, h].T

    x = x_ref[0]
    xb = x.astype(BF16)

    hv = _dot(xb, w_in_ref[:, OFF_UV:OFF_UG])
    hg = _dot(xb, w_in_ref[:, OFF_UG:OFF_Q])
    u = hv * _sigmoid(hg)
    ubuf[HALO:HALO + tm, :] = u
    acc = jnp.zeros((tm, CONV_CH), F32) + b_dw_ref[...]
    for w in range(CONV_W):
        acc = acc + w_dw_ref[w:w + 1, :] * ubuf[HALO - (CONV_W - 1) + w:HALO - (CONV_W - 1) + w + tm, :]
    cact = _layer_norm(acc, lncg_ref[...], lncb_ref[...])
    cact = cact * _sigmoid(cact)
    mixbuf[:, 0:CONV_CH] = cact.astype(BF16)
    tail = ubuf[tm:tm + HALO, :]
    ubuf[0:HALO, :] = tail

    @pl.when(t == nt - 1)
    def _conv_out():
        conv_out_ref[0] = tail[HALO - (CONV_W - 1):, :]

    qbuf[...] = _dot(xb, w_in_ref[:, OFF_Q:OFF_K]) * (GLA_DK ** -0.5)
    kbuf[SUB:SUB + tm, :] = _dot(xb, w_in_ref[:, OFF_K:OFF_V])
    vbuf[SUB:SUB + tm, :] = _dot(xb, w_in_ref[:, OFF_V:OFF_G])
    gbuf[...] = _dot(xb, w_in_ref[:, OFF_G:OFF_A])
    ha = _dot(xb, w_in_ref[:, OFF_A:IN_COLS_PAD])
    a = _dot(ha.astype(BF16), wg_ref[...]) + bg_ref[...]
    labuf[...] = (jnp.minimum(a, 0.0) - jnp.log(1.0 + jnp.exp(-jnp.abs(a)))) * (1.0 / GATE_NORM)

    lane_k = lax.broadcasted_iota(jnp.int32, (1, GLA_K), 1) // GLA_DK
    row_c = lax.broadcasted_iota(jnp.int32, (chunk, 1), 0)
    tri = (lax.broadcasted_iota(jnp.int32, (chunk, chunk), 0)
           >= lax.broadcasted_iota(jnp.int32, (chunk, chunk), 1)).astype(BF16)
    e2 = (lax.broadcasted_iota(jnp.int32, (GLA_K, GLA_V), 0) // GLA_DK
          == lax.broadcasted_iota(jnp.int32, (GLA_K, GLA_V), 1) // GLA_DV).astype(BF16)
    row_s = lax.broadcasted_iota(jnp.int32, (GLA_HEADS * chunk, 1), 0)
    blk_s = (row_s % chunk) // SUB

    def head_stack(m):
        return jnp.concatenate(
            [jnp.where(lane_k == h, m, 0.0) for h in range(GLA_HEADS)], axis=0).astype(BF16)

    def unstack(r, width):
        return jnp.concatenate(
            [r[h * chunk:(h + 1) * chunk, h * width:(h + 1) * width] for h in range(GLA_HEADS)],
            axis=1)

    def chunk_step(c, carry):
        base = pl.multiple_of(c * chunk, chunk)
        q_c = qbuf[pl.ds(base, chunk), :]
        k_c = kbuf[pl.ds(base + SUB, chunk), :]
        v_c = vbuf[pl.ds(base + SUB, chunk), :]
        la_c = labuf[pl.ds(base, chunk), :]
        la_hi, la_lo = _split_bf16(la_c)
        b = _dot(tri, la_hi) + _dot(tri, la_lo)
        bbuf[pl.ds(base + SUB, chunk), :] = b
        b_last = b[chunk - 1:chunk, :]
        st = st_ref[...]

        qe = q_c * jnp.exp(b)
        o2 = _dot_nt(head_stack(qe), st.astype(BF16))
        o = jnp.concatenate([o2[h * chunk:(h + 1) * chunk, :] for h in range(GLA_HEADS)], axis=1)

        if n_sub > 1:
            r_rows = [b[0:SUB, :]] + [jnp.broadcast_to(b[SUB * i - 1:SUB * i, :], (SUB, GLA_K))
                                      for i in range(1, n_sub)]
            r_q = jnp.concatenate(r_rows, axis=0)
            q_t = jnp.where(row_c >= SUB, q_c * jnp.exp(jnp.minimum(b - r_q, 0.0)), 0.0)
            k_parts = []
            for i in range(1, n_sub):
                r_i = b[SUB * i - 1:SUB * i, :]
                k_t = jnp.where(row_c < SUB * i, k_c * jnp.exp(jnp.minimum(r_i - b, 0.0)), 0.0)
                k_parts.append(k_t)
                k_parts.append(jnp.zeros((128 - chunk, GLA_K), F32))
            k_cat = jnp.concatenate(k_parts, axis=0).astype(BF16)
            r = _dot_nt(head_stack(q_t), k_cat)
            a_off = r[:, 0:128]
            for i in range(2, n_sub):
                a_off = jnp.where(blk_s == i, r[:, (i - 1) * 128:i * 128], a_off)
            o1 = _dot(a_off[:, 0:chunk].astype(BF16), v_c.astype(BF16))
            o = o + unstack(o1, GLA_DV)

        k_w = kbuf[pl.ds(base, SUB + chunk), :]
        b_w = bbuf[pl.ds(base, SUB + chunk), :]
        v_w = vbuf[pl.ds(base, SUB + chunk), :]
        for d in range(SUB):
            k_s = k_w[SUB - d:SUB - d + chunk, :]
            b_s = b_w[SUB - d:SUB - d + chunk, :]
            v_s = v_w[SUB - d:SUB - d + chunk, :]
            p = jnp.where(row_c % SUB >= d, q_c * k_s * jnp.exp(jnp.minimum(b - b_s, 0.0)), 0.0)
            o = o + _dot(p.astype(BF16), e2) * v_s

        ke = k_c * jnp.exp(b_last - b)
        r2 = _dot_tn(v_c.astype(BF16), ke.astype(BF16))
        upd = jnp.zeros((GLA_DV, GLA_K), F32)
        for h in range(GLA_HEADS):
            upd = upd + jnp.where(lane_k == h, r2[h * GLA_DV:(h + 1) * GLA_DV, :], 0.0)
        st_ref[...] = st * jnp.exp(b_last) + upd

        g_c = gbuf[pl.ds(base, chunk), :]
        outs = []
        for h in range(GLA_HEADS):
            oh = o[:, h * GLA_DV:(h + 1) * GLA_DV]
            ms = jnp.mean(oh * oh, axis=-1, keepdims=True)
            outs.append(oh * lax.rsqrt(ms + LN_EPS))
        on = jnp.concatenate(outs, axis=1) * gng_ref[...]
        on = on * (g_c * _sigmoid(g_c))
        mixbuf[pl.ds(base, chunk), CONV_CH:MIX_W] = on.astype(BF16)
        return carry

    lax.fori_loop(0, n_chunks, chunk_step, 0)


    @pl.when(t == nt - 1)
    def _gla_out():
        st = st_ref[...]
        for h in range(GLA_HEADS):
            gla_out_ref[0, h] = st[:, h * GLA_DK:(h + 1) * GLA_DK].T

    y = _dot(mixbuf[...], w_o_ref[...])
    x1 = _layer_norm(DN_ALPHA * x + y, ln1g_ref[...], ln1b_ref[...])
    x1_ref[0] = x1
    x_hi, x_lo = _split_bf16(x1)
    lg = _dot(x_hi, wr_ref[...])
    logit_ref[0] = lg[:, 0:128] + lg[:, 128:256] + _dot(x_lo, wr_ref[:, 0:128]) + br_ref[...]


def _mixer(x, conv_state, gla_state, p, *, tm):
    bsz, seq, _ = x.shape
    chunk = min(CHUNK, seq)
    assert seq % tm == 0 and tm % chunk == 0 and chunk % SUB == 0
    nt = seq // tm
    full = lambda shape: pl.BlockSpec(shape, lambda b, t: (0,) * len(shape))
    kern = functools.partial(_mixer_kernel, tm=tm, chunk=chunk)
    return pl.pallas_call(
        kern,
        grid=(bsz, nt),
        in_specs=[
            pl.BlockSpec((1, tm, D_MODEL), lambda b, t: (b, t, 0)),
            pl.BlockSpec((1, CONV_W - 1, CONV_CH), lambda b, t: (b, 0, 0)),
            pl.BlockSpec((1, GLA_HEADS, GLA_DK, GLA_DV), lambda b, t: (b, 0, 0, 0)),
            full((D_MODEL, IN_COLS_PAD)),
            full((CONV_W, CONV_CH)),
            full((1, CONV_CH)),
            full((1, CONV_CH)),
            full((1, CONV_CH)),
            full((GATE_PAD, GLA_K)),
            full((1, GLA_K)),
            full((1, GLA_V)),
            full((MIX_W, D_MODEL)),
            full((1, D_MODEL)),
            full((1, D_MODEL)),
            full((D_MODEL, 256)),
            full((1, 128)),
        ],
        out_specs=[
            pl.BlockSpec((1, tm, D_MODEL), lambda b, t: (b, t, 0)),
            pl.BlockSpec((1, tm, 128), lambda b, t: (b, t, 0)),
            pl.BlockSpec((1, CONV_W - 1, CONV_CH), lambda b, t: (b, 0, 0)),
            pl.BlockSpec((1, GLA_HEADS, GLA_DK, GLA_DV), lambda b, t: (b, 0, 0, 0)),
        ],
        out_shape=[
            jax.ShapeDtypeStruct((bsz, seq, D_MODEL), F32),
            jax.ShapeDtypeStruct((bsz, seq, 128), F32),
            jax.ShapeDtypeStruct((bsz, CONV_W - 1, CONV_CH), F32),
            jax.ShapeDtypeStruct((bsz, GLA_HEADS, GLA_DK, GLA_DV), F32),
        ],
        scratch_shapes=[
            pltpu.VMEM((HALO + tm, CONV_CH), F32),
            pltpu.VMEM((tm, GLA_K), F32),
            pltpu.VMEM((SUB + tm, GLA_K), F32),
            pltpu.VMEM((SUB + tm, GLA_K), F32),
            pltpu.VMEM((tm, GLA_K), F32),
            pltpu.VMEM((SUB + tm, GLA_V), F32),
            pltpu.VMEM((tm, GLA_V), F32),
            pltpu.VMEM((tm, MIX_W), BF16),
            pltpu.VMEM((GLA_DV, GLA_K), F32),
        ],
        compiler_params=pltpu.CompilerParams(
            dimension_semantics=("arbitrary", "arbitrary"),
            vmem_limit_bytes=VMEM_LIMIT),
        name="mixer",
    )(x, conv_state, gla_state, p["w_in"], p["w_dw"], p["b_dw"], p["ln_conv_g"], p["ln_conv_b"],
      p["w_gate"], p["b_gate"], p["gla_norm_g"], p["w_o"], p["ln1_g"], p["ln1_b"],
      p["w_router"], p["b_router"])


def _route_kernel(lg_ref, ri_ref, rg_ref, cnt_ref, carry_ref, *, tr):
    i = pl.program_id(0)

    @pl.when(i == 0)
    def _init():
        carry_ref[...] = jnp.zeros((1, 128), F32)

    lane = lax.broadcasted_iota(jnp.int32, (tr, 128), 1)
    l = jnp.where(lane < N_EXPERTS, lg_ref[...], -jnp.inf)
    hots, vals, idxs = [], [], []
    for _ in range(TOP_K):
        m = jnp.max(l, axis=-1, keepdims=True)
        ik = jnp.min(jnp.where(l == m, lane, 128), axis=-1, keepdims=True)
        hot = lane == ik
        hots.append(hot)
        vals.append(m)
        idxs.append(ik)
        l = jnp.where(hot, -jnp.inf, l)
    es = [jnp.exp(v - vals[0]) for v in vals]
    denom = es[0] + es[1] + es[2] + es[3]
    member = jnp.zeros((tr, 128), F32)
    for hot in hots:
        member = member + hot.astype(F32)
    strict = (lax.broadcasted_iota(jnp.int32, (tr, tr), 0)
              > lax.broadcasted_iota(jnp.int32, (tr, tr), 1)).astype(BF16)
    cum = _dot(strict, member.astype(BF16)) + carry_ref[...]
    ri = jnp.zeros((tr, 128), jnp.int32)
    rg = jnp.zeros((tr, 128), F32)
    for k in range(TOP_K):
        rank = jnp.sum(jnp.where(hots[k], cum, 0.0), axis=-1, keepdims=True).astype(jnp.int32)
        ri = jnp.where(lane == k, idxs[k], ri)
        ri = jnp.where(lane == TOP_K + k, rank, ri)
        rg = jnp.where(lane == k, es[k] / denom, rg)
    ri_ref[...] = ri
    rg_ref[...] = rg
    carry_ref[...] = carry_ref[...] + jnp.sum(member, axis=0, keepdims=True)

    @pl.when(i == pl.num_programs(0) - 1)
    def _fin():
        cnt_ref[...] = carry_ref[...]


def _route(logits, *, tr):
    n_tok = logits.shape[0]
    assert n_tok % tr == 0
    return pl.pallas_call(
        functools.partial(_route_kernel, tr=tr),
        grid=(n_tok // tr,),
        in_specs=[pl.BlockSpec((tr, 128), lambda i: (i, 0))],
        out_specs=[pl.BlockSpec((tr, 128), lambda i: (i, 0)),
                   pl.BlockSpec((tr, 128), lambda i: (i, 0)),
                   pl.BlockSpec((1, 128), lambda i: (0, 0))],
        out_shape=[jax.ShapeDtypeStruct((n_tok, 128), jnp.int32),
                   jax.ShapeDtypeStruct((n_tok, 128), F32),
                   jax.ShapeDtypeStruct((1, 128), F32)],
        scratch_shapes=[pltpu.VMEM((1, 128), F32)],
        compiler_params=pltpu.CompilerParams(dimension_semantics=("arbitrary",)),
        name="route",
    )(logits)


def _ffn_kernel(be_ref, nu_ref, xs_ref, wgu_ref, bgu_ref, wdn_ref, bdn_ref, yb_ref, wgu_bf, wdn_bf):
    i = pl.program_id(0)
    e = be_ref[i]
    prev = be_ref[jnp.maximum(i - 1, 0)]

    @pl.when((i == 0) | (e != prev))
    def _cast():
        wgu_bf[...] = wgu_ref[0].astype(BF16)
        wdn_bf[...] = wdn_ref[0].astype(BF16)

    @pl.when(i < nu_ref[0])
    def _compute():
        hgu = _dot(xs_ref[...].astype(BF16), wgu_bf[...]) + bgu_ref[0]
        gate = jnp.minimum(hgu[:, :D_FF], SWIGLU_LIMIT)
        up = jnp.clip(hgu[:, D_FF:], -SWIGLU_LIMIT, SWIGLU_LIMIT)
        act = (up + 1.0) * gate * _sigmoid(SWIGLU_ALPHA * gate)
        yb_ref[...] = _dot(act.astype(BF16), wdn_bf[...]) + bdn_ref[0]


def _ffn(block_e, n_used, xs, w_gu, b_gu, w_down, b_down):
    n_rows = xs.shape[0]
    nb = n_rows // EXPERT_BLOCK
    grid_spec = pltpu.PrefetchScalarGridSpec(
        num_scalar_prefetch=2,
        grid=(nb,),
        in_specs=[
            pl.BlockSpec((EXPERT_BLOCK, D_MODEL), lambda i, be, nu: (i, 0)),
            pl.BlockSpec((1, D_MODEL, 2 * D_FF), lambda i, be, nu: (be[i], 0, 0)),
            pl.BlockSpec((1, 1, 2 * D_FF), lambda i, be, nu: (be[i], 0, 0)),
            pl.BlockSpec((1, D_FF, D_MODEL), lambda i, be, nu: (be[i], 0, 0)),
            pl.BlockSpec((1, 1, D_MODEL), lambda i, be, nu: (be[i], 0, 0)),
        ],
        out_specs=pl.BlockSpec((EXPERT_BLOCK, D_MODEL), lambda i, be, nu: (i, 0)),
        scratch_shapes=[pltpu.VMEM((D_MODEL, 2 * D_FF), BF16),
                        pltpu.VMEM((D_FF, D_MODEL), BF16)],
    )
    return pl.pallas_call(
        _ffn_kernel,
        grid_spec=grid_spec,
        out_shape=jax.ShapeDtypeStruct((n_rows, D_MODEL), F32),
        compiler_params=pltpu.CompilerParams(
            dimension_semantics=("arbitrary",), vmem_limit_bytes=VMEM_LIMIT),
        name="expert_ffn",
    )(block_e, n_used, xs, w_gu, b_gu.reshape(N_EXPERTS, 1, 2 * D_FF), w_down,
      b_down.reshape(N_EXPERTS, 1, D_MODEL))


def _combine_kernel(x1_ref, g_ref, rg_ref, ln2g_ref, ln2b_ref, yp_ref, ys_ref, *, n_prompt_blocks):
    i = pl.program_id(0)
    m = jnp.zeros(x1_ref.shape, F32)
    for k in range(TOP_K):
        m = m + rg_ref[:, k:k + 1] * g_ref[k]
    y = _layer_norm(DN_ALPHA * x1_ref[...] + m, ln2g_ref[...], ln2b_ref[...])

    @pl.when(i < n_prompt_blocks)
    def _p():
        yp_ref[...] = y

    @pl.when(i >= n_prompt_blocks)
    def _s():
        ys_ref[...] = y


def _combine(x1, g, rg, ln2_g, ln2_b, *, n_prompt, tf):
    n_tok = x1.shape[0]
    n_sample = n_tok - n_prompt
    assert n_prompt % tf == 0 and n_sample % tf == 0
    npb = n_prompt // tf
    return pl.pallas_call(
        functools.partial(_combine_kernel, n_prompt_blocks=npb),
        grid=(n_tok // tf,),
        in_specs=[pl.BlockSpec((tf, D_MODEL), lambda i: (i, 0)),
                  pl.BlockSpec((TOP_K, tf, D_MODEL), lambda i: (0, i, 0)),
                  pl.BlockSpec((tf, 128), lambda i: (i, 0)),
                  pl.BlockSpec((1, D_MODEL), lambda i: (0, 0)),
                  pl.BlockSpec((1, D_MODEL), lambda i: (0, 0))],
        out_specs=[pl.BlockSpec((tf, D_MODEL), lambda i: (jnp.minimum(i, npb - 1), 0)),
                   pl.BlockSpec((tf, D_MODEL), lambda i: (jnp.maximum(i - npb, 0), 0))],
        out_shape=[jax.ShapeDtypeStruct((n_prompt, D_MODEL), F32),
                   jax.ShapeDtypeStruct((n_sample, D_MODEL), F32)],
        compiler_params=pltpu.CompilerParams(
            dimension_semantics=("arbitrary",), vmem_limit_bytes=VMEM_LIMIT),
        name="combine",
    )(x1, g, rg, ln2_g, ln2_b)


def _prep_params(w_in, w_dw, b_dw, ln_conv_g, ln_conv_b, w_gate_lr, b_gate, gla_norm_g, w_o,
                 ln1_g, ln1_b, w_router, b_router):
    row = lambda v: v.reshape(1, -1).astype(F32)
    w_in_p = jnp.pad(w_in, ((0, 0), (0, IN_COLS_PAD - w_in.shape[1]))).astype(BF16)
    w_gate = jnp.pad(w_gate_lr, ((0, GATE_PAD - GATE_RANK), (0, 0))).astype(BF16)
    wr_hi = w_router.astype(BF16)
    wr_lo = (w_router - wr_hi.astype(F32)).astype(BF16)
    padr = lambda m: jnp.pad(m, ((0, 0), (0, 128 - N_EXPERTS)))
    return {
        "w_in": w_in_p, "w_dw": w_dw, "b_dw": row(b_dw), "ln_conv_g": row(ln_conv_g),
        "ln_conv_b": row(ln_conv_b), "w_gate": w_gate, "b_gate": row(b_gate),
        "gla_norm_g": row(gla_norm_g), "w_o": w_o.astype(BF16), "ln1_g": row(ln1_g),
        "ln1_b": row(ln1_b), "w_router": jnp.concatenate([padr(wr_hi), padr(wr_lo)], axis=1),
        "b_router": jnp.pad(row(b_router), ((0, 0), (0, 128 - N_EXPERTS))),
    }


def _moe_layout(ri, counts):
    n_tok = ri.shape[0]
    idx = ri[:, 0:TOP_K]
    rank = ri[:, TOP_K:2 * TOP_K]
    cnt = counts[0, :N_EXPERTS].astype(jnp.int32)
    padded = (cnt + EXPERT_BLOCK - 1) // EXPERT_BLOCK * EXPERT_BLOCK
    pad_end = jnp.cumsum(padded)
    pad_start = pad_end - padded
    nb = -(-(n_tok * TOP_K) // EXPERT_BLOCK) + N_EXPERTS
    onehot = idx[:, :, None] == jnp.arange(N_EXPERTS, dtype=jnp.int32)[None, None, :]
    dest = rank + jnp.sum(jnp.where(onehot, pad_start[None, None, :], 0), axis=-1)
    block_e = jnp.minimum(
        jnp.searchsorted(pad_end, jnp.arange(nb, dtype=jnp.int32) * EXPERT_BLOCK, side="right"),
        N_EXPERTS - 1).astype(jnp.int32)
    n_used = (pad_end[-1:] // EXPERT_BLOCK).astype(jnp.int32)
    return dest, block_e, n_used, nb


def kernel(x_prompt, x_sample, state_conv, state_gla, w_in, w_dw, b_dw, ln_conv_g, ln_conv_b,
           w_gate_lr, b_gate, gla_norm_g, w_o, ln1_g, ln1_b, w_router, b_router, w_gu, b_gu,
           w_down, b_down, ln2_g, ln2_b):
    assert w_in.shape[0] == DEPTH
    l = 0
    p = _prep_params(w_in[l], w_dw[l], b_dw[l], ln_conv_g[l], ln_conv_b[l], w_gate_lr[l],
                     b_gate[l], gla_norm_g[l], w_o[l], ln1_g[l], ln1_b[l], w_router[l],
                     b_router[l])
    bp, tp, _ = x_prompt.shape
    bs, ts, _ = x_sample.shape
    zc = jnp.zeros((bp, CONV_W - 1, CONV_CH), F32)
    zs = jnp.zeros((bp, GLA_HEADS, GLA_DK, GLA_DV), F32)
    x1p, lgp, conv_p, gla_p = _mixer(x_prompt, zc, zs, p, tm=512)
    x1s, lgs, conv_s, gla_s = _mixer(x_sample, state_conv[l], state_gla[l], p, tm=ts)
    n_prompt = bp * tp
    x1 = jnp.concatenate([x1p.reshape(n_prompt, D_MODEL), x1s.reshape(bs * ts, D_MODEL)], axis=0)
    lg = jnp.concatenate([lgp.reshape(n_prompt, 128), lgs.reshape(bs * ts, 128)], axis=0)
    n_tok = x1.shape[0]

    ri, rg, counts = _route(lg, tr=256)
    dest, block_e, n_used, nb = _moe_layout(ri, counts)
    tok = jnp.broadcast_to(jnp.arange(n_tok, dtype=jnp.int32)[:, None], (n_tok, TOP_K))
    slot_tok = jnp.zeros((nb * EXPERT_BLOCK,), jnp.int32).at[dest.reshape(-1)].set(tok.reshape(-1))
    xs = x1[slot_tok]
    yb = _ffn(block_e, n_used, xs, w_gu[l], b_gu[l], w_down[l], b_down[l])
    g = yb[dest.T]
    yp, ys = _combine(x1, g, rg, ln2_g[l].reshape(1, -1), ln2_b[l].reshape(1, -1),
                      n_prompt=n_prompt, tf=256)
    return (yp.reshape(bp, tp, D_MODEL), ys.reshape(bs, ts, D_MODEL),
            conv_p[None], gla_p[None], conv_s[None], gla_s[None])
```

```python
import functools

import jax
import jax.numpy as jnp
from jax import lax
from jax.experimental import pallas as pl
from jax.experimental.pallas import tpu as pltpu
from jax.experimental.pallas import tpu_sc as plsc

F32 = jnp.float32
BF16 = jnp.bfloat16

D_MODEL = 1024
CHUNK = 64
SUB = 16
CONV_CH = 512
CONV_W = 31
HALO = 32
GLA_HEADS = 4
GLA_DK = 64
GLA_DV = 128
GLA_K = GLA_HEADS * GLA_DK
GLA_V = GLA_HEADS * GLA_DV
GATE_RANK = 16
GATE_PAD = 128
GATE_NORM = 16.0
MIX_W = CONV_CH + GLA_V
OFF_UV = 0
OFF_UG = OFF_UV + CONV_CH
OFF_Q = OFF_UG + CONV_CH
OFF_K = OFF_Q + GLA_K
OFF_V = OFF_K + GLA_K
OFF_G = OFF_V + GLA_V
OFF_A = OFF_G + GLA_V
IN_COLS_PAD = OFF_A + GATE_PAD
N_EXPERTS = 32
TOP_K = 4
D_FF = 1024
SWIGLU_LIMIT = 7.0
SWIGLU_ALPHA = 1.702
EXPERT_BLOCK = 256
LN_EPS = 1e-5
DEPTH = 1
DN_ALPHA = (2 * DEPTH) ** 0.25
VMEM_LIMIT = 56 * 1024 * 1024
SC_SEG = 256
SC_WINDOW = 128
N_SEG = D_MODEL // SC_SEG


def _dot(a, b):
    return jnp.dot(a, b, preferred_element_type=F32)


def _dot_nt(a, b):
    return lax.dot_general(a, b, (((1,), (1,)), ((), ())), preferred_element_type=F32)


def _dot_tn(a, b):
    return lax.dot_general(a, b, (((0,), (0,)), ((), ())), preferred_element_type=F32)


def _layer_norm(x, g, b):
    mu = jnp.mean(x, axis=-1, keepdims=True)
    xc = x - mu
    var = jnp.mean(xc * xc, axis=-1, keepdims=True)
    return xc * lax.rsqrt(var + LN_EPS) * g + b


def _sigmoid(x):
    return 1.0 / (1.0 + jnp.exp(-x))


def _split_bf16(x):
    hi = x.astype(BF16)
    lo = (x - hi.astype(F32)).astype(BF16)
    return hi, lo


def _mixer_kernel(x_ref, cs_ref, gs_ref, w_in_ref, w_dw_ref, b_dw_ref, lncg_ref, lncb_ref,
                  wg_ref, bg_ref, gng_ref, w_o_ref, ln1g_ref, ln1b_ref, wr_ref, br_ref, *rest,
                  tm, chunk, n_alias):
    (x1_ref, logit_ref, conv_out_ref, gla_out_ref,
     ubuf, qbuf, kbuf, bbuf, labuf, vbuf, gbuf, mixbuf, st_ref) = rest[n_alias:]
    t = pl.program_id(1)
    nt = pl.num_programs(1)
    n_chunks = tm // chunk
    n_sub = chunk // SUB

    @pl.when(t == 0)
    def _init():
        ubuf[0:HALO - (CONV_W - 1), :] = jnp.zeros((HALO - (CONV_W - 1), CONV_CH), F32)
        ubuf[HALO - (CONV_W - 1):HALO, :] = cs_ref[0]
        kbuf[0:SUB, :] = jnp.zeros((SUB, GLA_K), F32)
        bbuf[0:SUB, :] = jnp.zeros((SUB, GLA_K), F32)
        vbuf[0:SUB, :] = jnp.zeros((SUB, GLA_V), F32)
        for h in range(GLA_HEADS):
            st_ref[:, h * GLA_DK:(h + 1) * GLA_DK] = gs_ref[0, h].T

    x = x_ref[0]
    xb = x.astype(BF16)

    hv = _dot(xb, w_in_ref[:, OFF_UV:OFF_UG])
    hg = _dot(xb, w_in_ref[:, OFF_UG:OFF_Q])
    u = hv * _sigmoid(hg)
    ubuf[HALO:HALO + tm, :] = u
    acc = jnp.zeros((tm, CONV_CH), F32) + b_dw_ref[...]
    for w in range(CONV_W):
        acc = acc + w_dw_ref[w:w + 1, :] * ubuf[HALO - (CONV_W - 1) + w:HALO - (CONV_W - 1) + w + tm, :]
    cact = _layer_norm(acc, lncg_ref[...], lncb_ref[...])
    cact = cact * _sigmoid(cact)
    mixbuf[:, 0:CONV_CH] = cact.astype(BF16)
    tail = ubuf[tm:tm + HALO, :]
    ubuf[0:HALO, :] = tail

    @pl.when(t == nt - 1)
    def _conv_out():
        conv_out_ref[0] = tail[HALO - (CONV_W - 1):, :]

    qbuf[...] = _dot(xb, w_in_ref[:, OFF_Q:OFF_K]) * (GLA_DK ** -0.5)
    kbuf[SUB:SUB + tm, :] = _dot(xb, w_in_ref[:, OFF_K:OFF_V])
    vbuf[SUB:SUB + tm, :] = _dot(xb, w_in_ref[:, OFF_V:OFF_G])
    gbuf[...] = _dot(xb, w_in_ref[:, OFF_G:OFF_A])
    ha = _dot(xb, w_in_ref[:, OFF_A:IN_COLS_PAD])
    a = _dot(ha.astype(BF16), wg_ref[...]) + bg_ref[...]
    labuf[...] = (jnp.minimum(a, 0.0) - jnp.log(1.0 + jnp.exp(-jnp.abs(a)))) * (1.0 / GATE_NORM)

    lane_k = lax.broadcasted_iota(jnp.int32, (1, GLA_K), 1) // GLA_DK
    row_c = lax.broadcasted_iota(jnp.int32, (chunk, 1), 0)
    tri = (lax.broadcasted_iota(jnp.int32, (chunk, chunk), 0)
           >= lax.broadcasted_iota(jnp.int32, (chunk, chunk), 1)).astype(BF16)
    e2 = (lax.broadcasted_iota(jnp.int32, (GLA_K, GLA_V), 0) // GLA_DK
          == lax.broadcasted_iota(jnp.int32, (GLA_K, GLA_V), 1) // GLA_DV).astype(BF16)
    row_s = lax.broadcasted_iota(jnp.int32, (GLA_HEADS * chunk, 1), 0)
    blk_s = (row_s % chunk) // SUB

    def head_stack(m):
        return jnp.concatenate(
            [jnp.where(lane_k == h, m, 0.0) for h in range(GLA_HEADS)], axis=0).astype(BF16)

    def unstack(r, width):
        return jnp.concatenate(
            [r[h * chunk:(h + 1) * chunk, h * width:(h + 1) * width] for h in range(GLA_HEADS)],
            axis=1)

    def chunk_step(c, carry):
        base = pl.multiple_of(c * chunk, chunk)
        q_c = qbuf[pl.ds(base, chunk), :]
        k_c = kbuf[pl.ds(base + SUB, chunk), :]
        v_c = vbuf[pl.ds(base + SUB, chunk), :]
        la_c = labuf[pl.ds(base, chunk), :]
        la_hi, la_lo = _split_bf16(la_c)
        b = _dot(tri, la_hi) + _dot(tri, la_lo)
        bbuf[pl.ds(base + SUB, chunk), :] = b
        b_last = b[chunk - 1:chunk, :]
        st = st_ref[...]

        qe = q_c * jnp.exp(b)
        o2 = _dot_nt(head_stack(qe), st.astype(BF16))
        o = jnp.concatenate([o2[h * chunk:(h + 1) * chunk, :] for h in range(GLA_HEADS)], axis=1)

        if n_sub > 1:
            r_rows = [b[0:SUB, :]] + [jnp.broadcast_to(b[SUB * i - 1:SUB * i, :], (SUB, GLA_K))
                                      for i in range(1, n_sub)]
            r_q = jnp.concatenate(r_rows, axis=0)
            q_t = jnp.where(row_c >= SUB, q_c * jnp.exp(jnp.minimum(b - r_q, 0.0)), 0.0)
            k_parts = []
            for i in range(1, n_sub):
                r_i = b[SUB * i - 1:SUB * i, :]
                k_t = jnp.where(row_c < SUB * i, k_c * jnp.exp(jnp.minimum(r_i - b, 0.0)), 0.0)
                k_parts.append(k_t)
                k_parts.append(jnp.zeros((128 - chunk, GLA_K), F32))
            k_cat = jnp.concatenate(k_parts, axis=0).astype(BF16)
            r = _dot_nt(head_stack(q_t), k_cat)
            a_off = r[:, 0:128]
            for i in range(2, n_sub):
                a_off = jnp.where(blk_s == i, r[:, (i - 1) * 128:i * 128], a_off)
            o1 = _dot(a_off[:, 0:chunk].astype(BF16), v_c.astype(BF16))
            o = o + unstack(o1, GLA_DV)

        k_w = kbuf[pl.ds(base, SUB + chunk), :]
        b_w = bbuf[pl.ds(base, SUB + chunk), :]
        v_w = vbuf[pl.ds(base, SUB + chunk), :]
        for d in range(SUB):
            k_s = k_w[SUB - d:SUB - d + chunk, :]
            b_s = b_w[SUB - d:SUB - d + chunk, :]
            v_s = v_w[SUB - d:SUB - d + chunk, :]
            p = jnp.where(row_c % SUB >= d, q_c * k_s * jnp.exp(jnp.minimum(b - b_s, 0.0)), 0.0)
            o = o + _dot(p.astype(BF16), e2) * v_s

        ke = k_c * jnp.exp(b_last - b)
        r2 = _dot_tn(v_c.astype(BF16), ke.astype(BF16))
        upd = jnp.zeros((GLA_DV, GLA_K), F32)
        for h in range(GLA_HEADS):
            upd = upd + jnp.where(lane_k == h, r2[h * GLA_DV:(h + 1) * GLA_DV, :], 0.0)
        st_ref[...] = st * jnp.exp(b_last) + upd

        g_c = gbuf[pl.ds(base, chunk), :]
        outs = []
        for h in range(GLA_HEADS):
            oh = o[:, h * GLA_DV:(h + 1) * GLA_DV]
            ms = jnp.mean(oh * oh, axis=-1, keepdims=True)
            outs.append(oh * lax.rsqrt(ms + LN_EPS))
        on = jnp.concatenate(outs, axis=1) * gng_ref[...]
        on = on * (g_c * _sigmoid(g_c))
        mixbuf[pl.ds(base, chunk), CONV_CH:MIX_W] = on.astype(BF16)
        return carry

    lax.fori_loop(0, n_chunks, chunk_step, 0)


    @pl.when(t == nt - 1)
    def _gla_out():
        st = st_ref[...]
        for h in range(GLA_HEADS):
            gla_out_ref[0, h] = st[:, h * GLA_DK:(h + 1) * GLA_DK].T

    y = _dot(mixbuf[...], w_o_ref[...])
    x1 = _layer_norm(DN_ALPHA * x + y, ln1g_ref[...], ln1b_ref[...])
    for j in range(N_SEG):
        x1_ref[j] = x1[:, j * SC_SEG:(j + 1) * SC_SEG]
    x_hi, x_lo = _split_bf16(x1)
    lg = _dot(x_hi, wr_ref[...])
    logit_ref[...] = lg[:, 0:128] + lg[:, 128:256] + _dot(x_lo, wr_ref[:, 0:128]) + br_ref[...]


def _mixer(x, conv_state, gla_state, p, *, tm, n_tok_total, tok_offset, token_bufs=None):
    bsz, seq, _ = x.shape
    chunk = min(CHUNK, seq)
    assert seq % tm == 0 and tm % chunk == 0 and chunk % SUB == 0 and tok_offset % tm == 0
    nt = seq // tm
    blk0 = tok_offset // tm
    full = lambda shape: pl.BlockSpec(shape, lambda b, t: (0,) * len(shape))
    alias = () if token_bufs is None else tuple(token_bufs)
    kern = functools.partial(_mixer_kernel, tm=tm, chunk=chunk, n_alias=len(alias))
    n_in = 16
    return pl.pallas_call(
        kern,
        grid=(bsz, nt),
        input_output_aliases={n_in + i: i for i in range(len(alias))},
        in_specs=[
            pl.BlockSpec((1, tm, D_MODEL), lambda b, t: (b, t, 0)),
            pl.BlockSpec((1, CONV_W - 1, CONV_CH), lambda b, t: (b, 0, 0)),
            pl.BlockSpec((1, GLA_HEADS, GLA_DK, GLA_DV), lambda b, t: (b, 0, 0, 0)),
            full((D_MODEL, IN_COLS_PAD)),
            full((CONV_W, CONV_CH)),
            full((1, CONV_CH)),
            full((1, CONV_CH)),
            full((1, CONV_CH)),
            full((GATE_PAD, GLA_K)),
            full((1, GLA_K)),
            full((1, GLA_V)),
            full((MIX_W, D_MODEL)),
            full((1, D_MODEL)),
            full((1, D_MODEL)),
            full((D_MODEL, 256)),
            full((1, 128)),
        ] + [pl.BlockSpec(memory_space=pl.ANY)] * len(alias),
        out_specs=[
            pl.BlockSpec((N_SEG, tm, SC_SEG), lambda b, t: (0, blk0 + b * nt + t, 0)),
            pl.BlockSpec((tm, 128), lambda b, t: (blk0 + b * nt + t, 0)),
            pl.BlockSpec((1, CONV_W - 1, CONV_CH), lambda b, t: (b, 0, 0)),
            pl.BlockSpec((1, GLA_HEADS, GLA_DK, GLA_DV), lambda b, t: (b, 0, 0, 0)),
        ],
        out_shape=[
            jax.ShapeDtypeStruct((N_SEG, n_tok_total, SC_SEG), F32),
            jax.ShapeDtypeStruct((n_tok_total, 128), F32),
            jax.ShapeDtypeStruct((bsz, CONV_W - 1, CONV_CH), F32),
            jax.ShapeDtypeStruct((bsz, GLA_HEADS, GLA_DK, GLA_DV), F32),
        ],
        scratch_shapes=[
            pltpu.VMEM((HALO + tm, CONV_CH), F32),
            pltpu.VMEM((tm, GLA_K), F32),
            pltpu.VMEM((SUB + tm, GLA_K), F32),
            pltpu.VMEM((SUB + tm, GLA_K), F32),
            pltpu.VMEM((tm, GLA_K), F32),
            pltpu.VMEM((SUB + tm, GLA_V), F32),
            pltpu.VMEM((tm, GLA_V), F32),
            pltpu.VMEM((tm, MIX_W), BF16),
            pltpu.VMEM((GLA_DV, GLA_K), F32),
        ],
        compiler_params=pltpu.CompilerParams(
            dimension_semantics=("arbitrary", "arbitrary"),
            vmem_limit_bytes=VMEM_LIMIT),
        name="mixer",
    )(x, conv_state, gla_state, p["w_in"], p["w_dw"], p["b_dw"], p["ln_conv_g"], p["ln_conv_b"],
      p["w_gate"], p["b_gate"], p["gla_norm_g"], p["w_o"], p["ln1_g"], p["ln1_b"],
      p["w_router"], p["b_router"], *alias)


def _route_kernel(lg_ref, ri_ref, rg_ref, cnt_ref, carry_ref, *, tr):
    i = pl.program_id(0)

    @pl.when(i == 0)
    def _init():
        carry_ref[...] = jnp.zeros((1, 128), F32)

    lane = lax.broadcasted_iota(jnp.int32, (tr, 128), 1)
    l = jnp.where(lane < N_EXPERTS, lg_ref[...], -jnp.inf)
    hots, vals, idxs = [], [], []
    for _ in range(TOP_K):
        m = jnp.max(l, axis=-1, keepdims=True)
        ik = jnp.min(jnp.where(l == m, lane, 128), axis=-1, keepdims=True)
        hot = lane == ik
        hots.append(hot)
        vals.append(m)
        idxs.append(ik)
        l = jnp.where(hot, -jnp.inf, l)
    es = [jnp.exp(v - vals[0]) for v in vals]
    denom = es[0] + es[1] + es[2] + es[3]
    member = jnp.zeros((tr, 128), F32)
    for hot in hots:
        member = member + hot.astype(F32)
    strict = (lax.broadcasted_iota(jnp.int32, (tr, tr), 0)
              > lax.broadcasted_iota(jnp.int32, (tr, tr), 1)).astype(BF16)
    cum = _dot(strict, member.astype(BF16)) + carry_ref[...]
    ri = jnp.zeros((tr, 128), jnp.int32)
    rg = jnp.zeros((tr, 128), F32)
    for k in range(TOP_K):
        rank = jnp.sum(jnp.where(hots[k], cum, 0.0), axis=-1, keepdims=True).astype(jnp.int32)
        ri = jnp.where(lane == k, idxs[k], ri)
        ri = jnp.where(lane == TOP_K + k, rank, ri)
        rg = jnp.where(lane == k, es[k] / denom, rg)
    ri_ref[...] = ri
    rg_ref[...] = rg
    carry_ref[...] = carry_ref[...] + jnp.sum(member, axis=0, keepdims=True)

    @pl.when(i == pl.num_programs(0) - 1)
    def _fin():
        cnt_ref[...] = carry_ref[...]


def _route(logits, *, tr):
    n_tok = logits.shape[0]
    assert n_tok % tr == 0
    return pl.pallas_call(
        functools.partial(_route_kernel, tr=tr),
        grid=(n_tok // tr,),
        in_specs=[pl.BlockSpec((tr, 128), lambda i: (i, 0))],
        out_specs=[pl.BlockSpec((tr, 128), lambda i: (i, 0)),
                   pl.BlockSpec((tr, 128), lambda i: (i, 0)),
                   pl.BlockSpec((1, 128), lambda i: (0, 0))],
        out_shape=[jax.ShapeDtypeStruct((n_tok, 128), jnp.int32),
                   jax.ShapeDtypeStruct((n_tok, 128), F32),
                   jax.ShapeDtypeStruct((1, 128), F32)],
        scratch_shapes=[pltpu.VMEM((1, 128), F32)],
        compiler_params=pltpu.CompilerParams(dimension_semantics=("arbitrary",)),
        name="route",
    )(logits)


def _ffn_kernel(be_ref, nu_ref, xs_ref, wgu_ref, bgu_ref, wdn_ref, bdn_ref, yb_ref, wgu_bf, wdn_bf):
    i = pl.program_id(0)
    e = be_ref[i]
    prev = be_ref[jnp.maximum(i - 1, 0)]

    @pl.when((i == 0) | (e != prev))
    def _cast():
        wgu_bf[...] = wgu_ref[0].astype(BF16)
        wdn_bf[...] = wdn_ref[0].astype(BF16)

    @pl.when(i < nu_ref[0])
    def _compute():
        xs = jnp.concatenate([xs_ref[j] for j in range(N_SEG)], axis=1).astype(BF16)
        hgu = _dot(xs, wgu_bf[...]) + bgu_ref[0]
        gate = jnp.minimum(hgu[:, :D_FF], SWIGLU_LIMIT)
        up = jnp.clip(hgu[:, D_FF:], -SWIGLU_LIMIT, SWIGLU_LIMIT)
        act = (up + 1.0) * gate * _sigmoid(SWIGLU_ALPHA * gate)
        yb = _dot(act.astype(BF16), wdn_bf[...]) + bdn_ref[0]
        for j in range(N_SEG):
            yb_ref[j] = yb[:, j * SC_SEG:(j + 1) * SC_SEG]


def _ffn(block_e, n_used, xs, w_gu, b_gu, w_down, b_down):
    n_rows = xs.shape[1]
    nb = n_rows // EXPERT_BLOCK
    grid_spec = pltpu.PrefetchScalarGridSpec(
        num_scalar_prefetch=2,
        grid=(nb,),
        in_specs=[
            pl.BlockSpec((N_SEG, EXPERT_BLOCK, SC_SEG), lambda i, be, nu: (0, i, 0)),
            pl.BlockSpec((1, D_MODEL, 2 * D_FF), lambda i, be, nu: (be[i], 0, 0)),
            pl.BlockSpec((1, 1, 2 * D_FF), lambda i, be, nu: (be[i], 0, 0)),
            pl.BlockSpec((1, D_FF, D_MODEL), lambda i, be, nu: (be[i], 0, 0)),
            pl.BlockSpec((1, 1, D_MODEL), lambda i, be, nu: (be[i], 0, 0)),
        ],
        out_specs=pl.BlockSpec((N_SEG, EXPERT_BLOCK, SC_SEG), lambda i, be, nu: (0, i, 0)),
        scratch_shapes=[pltpu.VMEM((D_MODEL, 2 * D_FF), BF16),
                        pltpu.VMEM((D_FF, D_MODEL), BF16)],
    )
    return pl.pallas_call(
        _ffn_kernel,
        grid_spec=grid_spec,
        out_shape=jax.ShapeDtypeStruct((N_SEG, n_rows, SC_SEG), F32),
        compiler_params=pltpu.CompilerParams(
            dimension_semantics=("arbitrary",), vmem_limit_bytes=VMEM_LIMIT),
        name="expert_ffn",
    )(block_e, n_used, xs, w_gu, b_gu.reshape(N_EXPERTS, 1, 2 * D_FF), w_down,
      b_down.reshape(N_EXPERTS, 1, D_MODEL))


def _combine_kernel(x1_ref, g_ref, rg_ref, ln2g_ref, ln2b_ref, yp_ref, ys_ref, *, n_prompt_blocks):
    i = pl.program_id(0)
    segs = []
    for j in range(N_SEG):
        z = DN_ALPHA * x1_ref[j]
        for k in range(TOP_K):
            z = z + rg_ref[:, k:k + 1] * g_ref[k, j]
        segs.append(z)
    y = _layer_norm(jnp.concatenate(segs, axis=1), ln2g_ref[...], ln2b_ref[...])

    @pl.when(i < n_prompt_blocks)
    def _p():
        yp_ref[...] = y

    @pl.when(i >= n_prompt_blocks)
    def _s():
        ys_ref[...] = y


def _combine(x1, g, rg, ln2_g, ln2_b, *, n_prompt, tf):
    n_tok = x1.shape[1]
    n_sample = n_tok - n_prompt
    assert n_prompt % tf == 0 and n_sample % tf == 0
    npb = n_prompt // tf
    return pl.pallas_call(
        functools.partial(_combine_kernel, n_prompt_blocks=npb),
        grid=(n_tok // tf,),
        in_specs=[pl.BlockSpec((N_SEG, tf, SC_SEG), lambda i: (0, i, 0)),
                  pl.BlockSpec((TOP_K, N_SEG, tf, SC_SEG), lambda i: (0, 0, i, 0)),
                  pl.BlockSpec((tf, 128), lambda i: (i, 0)),
                  pl.BlockSpec((1, D_MODEL), lambda i: (0, 0)),
                  pl.BlockSpec((1, D_MODEL), lambda i: (0, 0))],
        out_specs=[pl.BlockSpec((tf, D_MODEL), lambda i: (jnp.minimum(i, npb - 1), 0)),
                   pl.BlockSpec((tf, D_MODEL), lambda i: (jnp.maximum(i - npb, 0), 0))],
        out_shape=[jax.ShapeDtypeStruct((n_prompt, D_MODEL), F32),
                   jax.ShapeDtypeStruct((n_sample, D_MODEL), F32)],
        compiler_params=pltpu.CompilerParams(
            dimension_semantics=("arbitrary",), vmem_limit_bytes=VMEM_LIMIT),
        name="combine",
    )(x1, g, rg, ln2_g, ln2_b)


def _sc_mesh():
    return plsc.VectorSubcoreMesh(core_axis_name="c", subcore_axis_name="s")


def _sc_scatter_rows(src, dest, n_out):
    n_src, width = src.shape
    n_k = dest.shape[0]
    assert width == SC_SEG and n_src % SC_WINDOW == 0 and dest.shape[1] == n_src

    @pl.kernel(out_type=jax.ShapeDtypeStruct((n_out, width), src.dtype), mesh=_sc_mesh(),
               scratch_types=[])
    def scatter_kernel(src_hbm, idx_hbm, out_hbm):
        def body(src_vmem, idx_vmem):
            for k in range(n_k):
                pltpu.sync_copy(src_vmem, out_hbm.at[idx_vmem.at[k]])

        pltpu.emit_pipeline(
            body, grid=(n_src // SC_WINDOW,),
            in_specs=[pl.BlockSpec((SC_WINDOW, width), lambda i: (i, 0)),
                      pl.BlockSpec((n_k, SC_WINDOW), lambda i: (0, i))],
            out_specs=[],
            core_axis_name=("c", "s"),
            dimension_semantics=(pltpu.PARALLEL,),
        )(src_hbm, idx_hbm)

    return scatter_kernel(src, dest)


def _sc_gather_rows(src, idx):
    n_out = idx.shape[1]
    width = src.shape[1]
    assert width == SC_SEG and n_out % SC_WINDOW == 0

    @pl.kernel(out_type=jax.ShapeDtypeStruct((n_out, width), src.dtype), mesh=_sc_mesh(),
               scratch_types=[])
    def gather_kernel(src_hbm, idx_hbm, out_hbm):
        def body(idx_vmem, out_vmem):
            pltpu.sync_copy(src_hbm.at[idx_vmem.at[0]], out_vmem)

        pltpu.emit_pipeline(
            body, grid=(n_out // SC_WINDOW,),
            in_specs=[pl.BlockSpec((1, SC_WINDOW), lambda i: (0, i))],
            out_specs=[pl.BlockSpec((SC_WINDOW, width), lambda i: (i, 0))],
            core_axis_name=("c", "s"),
            dimension_semantics=(pltpu.PARALLEL,),
        )(idx_hbm, out_hbm)

    return gather_kernel(src, idx)


def _prep_params(w_in, w_dw, b_dw, ln_conv_g, ln_conv_b, w_gate_lr, b_gate, gla_norm_g, w_o,
                 ln1_g, ln1_b, w_router, b_router):
    row = lambda v: v.reshape(1, -1).astype(F32)
    w_in_p = jnp.pad(w_in, ((0, 0), (0, IN_COLS_PAD - w_in.shape[1]))).astype(BF16)
    w_gate = jnp.pad(w_gate_lr, ((0, GATE_PAD - GATE_RANK), (0, 0))).astype(BF16)
    wr_hi = w_router.astype(BF16)
    wr_lo = (w_router - wr_hi.astype(F32)).astype(BF16)
    padr = lambda m: jnp.pad(m, ((0, 0), (0, 128 - N_EXPERTS)))
    return {
        "w_in": w_in_p, "w_dw": w_dw, "b_dw": row(b_dw), "ln_conv_g": row(ln_conv_g),
        "ln_conv_b": row(ln_conv_b), "w_gate": w_gate, "b_gate": row(b_gate),
        "gla_norm_g": row(gla_norm_g), "w_o": w_o.astype(BF16), "ln1_g": row(ln1_g),
        "ln1_b": row(ln1_b), "w_router": jnp.concatenate([padr(wr_hi), padr(wr_lo)], axis=1),
        "b_router": jnp.pad(row(b_router), ((0, 0), (0, 128 - N_EXPERTS))),
    }


def _moe_layout(ri, counts):
    n_tok = ri.shape[0]
    idx = ri[:, 0:TOP_K]
    rank = ri[:, TOP_K:2 * TOP_K]
    cnt = counts[0, :N_EXPERTS].astype(jnp.int32)
    padded = (cnt + EXPERT_BLOCK - 1) // EXPERT_BLOCK * EXPERT_BLOCK
    pad_end = jnp.cumsum(padded)
    pad_start = pad_end - padded
    nb = -(-(n_tok * TOP_K) // EXPERT_BLOCK) + N_EXPERTS
    onehot = idx[:, :, None] == jnp.arange(N_EXPERTS, dtype=jnp.int32)[None, None, :]
    dest = rank + jnp.sum(jnp.where(onehot, pad_start[None, None, :], 0), axis=-1)
    block_row0 = jnp.arange(nb, dtype=jnp.int32) * EXPERT_BLOCK
    block_e = jnp.minimum(
        jnp.sum((pad_end[None, :] <= block_row0[:, None]).astype(jnp.int32), axis=1),
        N_EXPERTS - 1)
    n_used = (pad_end[-1:] // EXPERT_BLOCK).astype(jnp.int32)
    return dest, block_e, n_used, nb


def kernel(x_prompt, x_sample, state_conv, state_gla, w_in, w_dw, b_dw, ln_conv_g, ln_conv_b,
           w_gate_lr, b_gate, gla_norm_g, w_o, ln1_g, ln1_b, w_router, b_router, w_gu, b_gu,
           w_down, b_down, ln2_g, ln2_b):
    assert w_in.shape[0] == DEPTH
    l = 0
    p = _prep_params(w_in[l], w_dw[l], b_dw[l], ln_conv_g[l], ln_conv_b[l], w_gate_lr[l],
                     b_gate[l], gla_norm_g[l], w_o[l], ln1_g[l], ln1_b[l], w_router[l],
                     b_router[l])
    bp, tp, _ = x_prompt.shape
    bs, ts, _ = x_sample.shape
    zc = jnp.zeros((bp, CONV_W - 1, CONV_CH), F32)
    zs = jnp.zeros((bp, GLA_HEADS, GLA_DK, GLA_DV), F32)
    n_prompt = bp * tp
    n_tok = n_prompt + bs * ts
    x1, lg, conv_p, gla_p = _mixer(x_prompt, zc, zs, p, tm=512, n_tok_total=n_tok, tok_offset=0)
    x1, lg, conv_s, gla_s = _mixer(x_sample, state_conv[l], state_gla[l], p, tm=ts,
                                   n_tok_total=n_tok, tok_offset=n_prompt, token_bufs=(x1, lg))

    ri, rg, counts = _route(lg, tr=256)
    dest, block_e, n_used, nb = _moe_layout(ri, counts)
    n_slot = nb * EXPERT_BLOCK
    seg0 = jnp.arange(N_SEG, dtype=jnp.int32) * n_slot
    dest_seg = (dest.T[:, None, :] + seg0[None, :, None]).reshape(TOP_K, N_SEG * n_tok)
    xs = _sc_scatter_rows(x1.reshape(N_SEG * n_tok, SC_SEG), dest_seg, N_SEG * n_slot)
    yb = _ffn(block_e, n_used, xs.reshape(N_SEG, n_slot, SC_SEG), w_gu[l], b_gu[l], w_down[l],
              b_down[l])
    g = _sc_gather_rows(yb.reshape(N_SEG * n_slot, SC_SEG), dest_seg.reshape(1, -1))
    yp, ys = _combine(x1, g.reshape(TOP_K, N_SEG, n_tok, SC_SEG), rg, ln2_g[l].reshape(1, -1),
                      ln2_b[l].reshape(1, -1), n_prompt=n_prompt, tf=256)
    return (yp.reshape(bp, tp, D_MODEL), ys.reshape(bs, ts, D_MODEL),
            conv_p[None], gla_p[None], conv_s[None], gla_s[None])
```

```python
import functools

import jax
import jax.numpy as jnp
from jax import lax
from jax.experimental import pallas as pl
from jax.experimental.pallas import tpu as pltpu
from jax.experimental.pallas import tpu_sc as plsc

F32 = jnp.float32
BF16 = jnp.bfloat16

D_MODEL = 1024
CHUNK = 64
SUB = 16
DIAG = 8
CONV_CH = 512
CONV_W = 31
HALO = 32
GLA_HEADS = 4
GLA_DK = 64
GLA_DV = 128
GLA_K = GLA_HEADS * GLA_DK
GLA_V = GLA_HEADS * GLA_DV
GATE_RANK = 16
GATE_PAD = 128
GATE_NORM = 16.0
MIX_W = CONV_CH + GLA_V
OFF_UV = 0
OFF_UG = OFF_UV + CONV_CH
OFF_Q = OFF_UG + CONV_CH
OFF_K = OFF_Q + GLA_K
OFF_V = OFF_K + GLA_K
OFF_G = OFF_V + GLA_V
OFF_A = OFF_G + GLA_V
IN_COLS_PAD = OFF_A + GATE_PAD
N_EXPERTS = 32
TOP_K = 4
D_FF = 1024
SWIGLU_LIMIT = 7.0
SWIGLU_ALPHA = 1.702
EXPERT_BLOCK = 256
LN_EPS = 1e-5
DEPTH = 1
DN_ALPHA = (2 * DEPTH) ** 0.25
VMEM_LIMIT = 56 * 1024 * 1024
HALF = D_MODEL // 2
SC_SEG = 256
SC_WINDOW = 128
N_SEG = HALF // SC_SEG
U32 = jnp.uint32


def _dot(a, b):
    return jnp.dot(a, b, preferred_element_type=F32)


def _dot_nt(a, b):
    return lax.dot_general(a, b, (((1,), (1,)), ((), ())), preferred_element_type=F32)


def _dot_tn(a, b):
    return lax.dot_general(a, b, (((0,), (0,)), ((), ())), preferred_element_type=F32)


def _layer_norm(x, g, b):
    mu = jnp.mean(x, axis=-1, keepdims=True)
    xc = x - mu
    var = jnp.mean(xc * xc, axis=-1, keepdims=True)
    return xc * lax.rsqrt(var + LN_EPS) * g + b


def _sigmoid(x):
    return 1.0 / (1.0 + jnp.exp(-x))


def _split_bf16(x):
    hi = x.astype(BF16)
    lo = (x - hi.astype(F32)).astype(BF16)
    return hi, lo


def _pack_segments(x):
    bits = pltpu.bitcast(x.astype(BF16).astype(F32), U32)
    word = (bits[:, :HALF] >> 16) | bits[:, HALF:]
    return [word[:, j * SC_SEG:(j + 1) * SC_SEG] for j in range(N_SEG)]


def _unpack_segments(segs):
    word = jnp.concatenate(segs, axis=1)
    lo = pltpu.bitcast(word << 16, F32)
    hi = pltpu.bitcast(word & jnp.uint32(0xFFFF0000), F32)
    return lo, hi


def _mixer_kernel(x_ref, cs_ref, gs_ref, w_in_ref, w_dw_ref, b_dw_ref, lncg_ref, lncb_ref,
                  wg_ref, bg_ref, gng_ref, w_o_ref, ln1g_ref, ln1b_ref, wr_ref, br_ref, *rest,
                  tm, chunk, n_alias):
    (x1p_ref, x1_ref, logit_ref, conv_out_ref, gla_out_ref,
     ubuf, qbuf, kbuf, labuf, vbuf, gbuf, mixbuf, st_ref) = rest[n_alias:]
    t = pl.program_id(1)
    nt = pl.num_programs(1)
    n_chunks = tm // chunk
    n_sub = chunk // SUB

    @pl.when(t == 0)
    def _init():
        ubuf[0:HALO - (CONV_W - 1), :] = jnp.zeros((HALO - (CONV_W - 1), CONV_CH), F32)
        ubuf[HALO - (CONV_W - 1):HALO, :] = cs_ref[0]
        for h in range(GLA_HEADS):
            st_ref[:, h * GLA_DK:(h + 1) * GLA_DK] = gs_ref[0, h].T

    x = x_ref[0]
    xb = x.astype(BF16)

    hv = _dot(xb, w_in_ref[:, OFF_UV:OFF_UG])
    hg = _dot(xb, w_in_ref[:, OFF_UG:OFF_Q])
    u = hv * _sigmoid(hg)
    ubuf[HALO:HALO + tm, :] = u
    lead = HALO - (CONV_W - 1)

    def proj_q():
        qbuf[...] = _dot(xb, w_in_ref[:, OFF_Q:OFF_K]) * (GLA_DK ** -0.5)

    def proj_k():
        kbuf[...] = _dot(xb, w_in_ref[:, OFF_K:OFF_V])

    def proj_v(half):
        lo = half * (GLA_V // 2)
        vbuf[:, lo:lo + GLA_V // 2] = _dot(xb, w_in_ref[:, OFF_V + lo:OFF_V + lo + GLA_V // 2])

    def proj_g(half):
        lo = half * (GLA_V // 2)
        gbuf[:, lo:lo + GLA_V // 2] = _dot(xb, w_in_ref[:, OFF_G + lo:OFF_G + lo + GLA_V // 2])

    def proj_gate():
        ha = _dot(xb, w_in_ref[:, OFF_A:IN_COLS_PAD])
        a = _dot(ha.astype(BF16), wg_ref[...]) + bg_ref[...]
        labuf[...] = ((jnp.minimum(a, 0.0) - jnp.log(1.0 + jnp.exp(-jnp.abs(a))))
                      * (1.0 / GATE_NORM))

    proj_jobs = [proj_q, proj_k, functools.partial(proj_v, 0), functools.partial(proj_v, 1),
                 functools.partial(proj_g, 0), functools.partial(proj_g, 1), proj_gate]

    acc = jnp.zeros((tm, CONV_CH), F32) + b_dw_ref[...]
    for r in range(8):
        if r < len(proj_jobs):
            proj_jobs[r]()
        rows = tm if r == 0 else tm + 8
        part = None
        for s in range(r, lead + CONV_W, 8):
            if s < lead:
                continue
            term = w_dw_ref[s - lead:s - lead + 1, :] * ubuf[s - r:s - r + rows, :]
            part = term if part is None else part + term
        acc = acc + part[r:r + tm, :]
    cact = _layer_norm(acc, lncg_ref[...], lncb_ref[...])
    cact = cact * _sigmoid(cact)
    mixbuf[:, 0:CONV_CH] = cact.astype(BF16)
    tail = ubuf[tm:tm + HALO, :]
    ubuf[0:HALO, :] = tail

    @pl.when(t == nt - 1)
    def _conv_out():
        conv_out_ref[0] = tail[HALO - (CONV_W - 1):, :]

    lane_k = lax.broadcasted_iota(jnp.int32, (1, GLA_K), 1) // GLA_DK
    row_c = lax.broadcasted_iota(jnp.int32, (chunk, 1), 0)
    tri = (lax.broadcasted_iota(jnp.int32, (chunk, chunk), 0)
           >= lax.broadcasted_iota(jnp.int32, (chunk, chunk), 1)).astype(BF16)
    e2 = (lax.broadcasted_iota(jnp.int32, (GLA_K, GLA_V), 0) // GLA_DK
          == lax.broadcasted_iota(jnp.int32, (GLA_K, GLA_V), 1) // GLA_DV).astype(BF16)
    row_s = lax.broadcasted_iota(jnp.int32, (GLA_HEADS * chunk, 1), 0)
    blk_s = (row_s % chunk) // SUB
    same_blk = blk_s == lax.broadcasted_iota(jnp.int32, (1, 128), 1) // SUB
    second_half = row_c % SUB >= DIAG
    row_g = lax.broadcasted_iota(jnp.int32, (1, DIAG, 1), 1)
    lane_v = lax.broadcasted_iota(jnp.int32, (1, GLA_V), 1) % GLA_DV

    def head_stack(m):
        return jnp.concatenate(
            [jnp.where(lane_k == h, m, 0.0) for h in range(GLA_HEADS)], axis=0).astype(BF16)

    def unstack(r, width):
        return jnp.concatenate(
            [r[h * chunk:(h + 1) * chunk, h * width:(h + 1) * width] for h in range(GLA_HEADS)],
            axis=1)

    def chunk_step(c, carry):
        base = pl.multiple_of(c * chunk, chunk)
        q_c = qbuf[pl.ds(base, chunk), :]
        k_c = kbuf[pl.ds(base, chunk), :]
        v_c = vbuf[pl.ds(base, chunk), :]
        v_b = v_c.astype(BF16)
        la_c = labuf[pl.ds(base, chunk), :]
        la_hi, la_lo = _split_bf16(la_c)
        b = _dot(tri, la_hi) + _dot(tri, la_lo)
        b_last = b[chunk - 1:chunk, :]
        st = st_ref[...]

        qe = q_c * jnp.exp(b)
        o2 = _dot_nt(head_stack(qe), st.astype(BF16))
        o = jnp.concatenate([o2[h * chunk:(h + 1) * chunk, :] for h in range(GLA_HEADS)], axis=1)

        a_off = None
        if n_sub > 1:
            r_rows = [b[0:SUB, :]] + [jnp.broadcast_to(b[SUB * i - 1:SUB * i, :], (SUB, GLA_K))
                                      for i in range(1, n_sub)]
            r_q = jnp.concatenate(r_rows, axis=0)
            q_t = jnp.where(row_c >= SUB, q_c * jnp.exp(jnp.minimum(b - r_q, 0.0)), 0.0)
            k_parts = []
            for i in range(1, n_sub):
                r_i = b[SUB * i - 1:SUB * i, :]
                k_t = jnp.where(row_c < SUB * i, k_c * jnp.exp(jnp.minimum(r_i - b, 0.0)), 0.0)
                k_parts.append(k_t)
                k_parts.append(jnp.zeros((128 - chunk, GLA_K), F32))
            k_cat = jnp.concatenate(k_parts, axis=0).astype(BF16)
            r = _dot_nt(head_stack(q_t), k_cat)
            a_off = r[:, 0:128]
            for i in range(2, n_sub):
                a_off = jnp.where(blk_s == i, r[:, (i - 1) * 128:i * 128], a_off)

        r_b = jnp.concatenate(
            [jnp.broadcast_to(b[SUB * i + DIAG - 1:SUB * i + DIAG, :], (SUB, GLA_K))
             for i in range(n_sub)], axis=0)
        q_b = jnp.where(second_half, q_c * jnp.exp(jnp.minimum(b - r_b, 0.0)), 0.0)
        k_b = jnp.where(second_half, 0.0, k_c * jnp.exp(jnp.minimum(r_b - b, 0.0)))
        k_b = jnp.concatenate([k_b, jnp.zeros((128 - chunk, GLA_K), F32)], axis=0).astype(BF16)
        a_half = jnp.where(same_blk, _dot_nt(head_stack(q_b), k_b), 0.0)
        a_off = a_half if a_off is None else a_off + a_half
        o1 = _dot(a_off[:, 0:chunk].astype(BF16), v_b)
        o = o + unstack(o1, GLA_DV)

        q3 = q_c.reshape(chunk // DIAG, DIAG, GLA_K)
        k3 = k_c.reshape(chunk // DIAG, DIAG, GLA_K)
        b3 = b.reshape(chunk // DIAG, DIAG, GLA_K)
        a_d = jnp.zeros((chunk, GLA_V), F32)
        for d in range(DIAG):
            k_s = k3 if d == 0 else pltpu.roll(k3, d, axis=1)
            b_s = b3 if d == 0 else pltpu.roll(b3, d, axis=1)
            p = jnp.where(row_g >= d, q3 * k_s * jnp.exp(jnp.minimum(b3 - b_s, 0.0)), 0.0)
            w_d = _dot(p.reshape(chunk, GLA_K).astype(BF16), e2)
            a_d = jnp.where(lane_v == row_c - d, w_d, a_d)
        a_d = a_d.astype(BF16)
        o = o + jnp.concatenate(
            [_dot(a_d[:, h * GLA_DV:h * GLA_DV + chunk], v_b[:, h * GLA_DV:(h + 1) * GLA_DV])
             for h in range(GLA_HEADS)], axis=1)

        ke = k_c * jnp.exp(b_last - b)
        r2 = _dot_tn(v_b, ke.astype(BF16))
        upd = jnp.zeros((GLA_DV, GLA_K), F32)
        for h in range(GLA_HEADS):
            upd = upd + jnp.where(lane_k == h, r2[h * GLA_DV:(h + 1) * GLA_DV, :], 0.0)
        st_ref[...] = st * jnp.exp(b_last) + upd

        g_c = gbuf[pl.ds(base, chunk), :]
        outs = []
        for h in range(GLA_HEADS):
            oh = o[:, h * GLA_DV:(h + 1) * GLA_DV]
            ms = jnp.mean(oh * oh, axis=-1, keepdims=True)
            outs.append(oh * lax.rsqrt(ms + LN_EPS))
        on = jnp.concatenate(outs, axis=1) * gng_ref[...]
        on = on * (g_c * _sigmoid(g_c))
        mixbuf[pl.ds(base, chunk), CONV_CH:MIX_W] = on.astype(BF16)
        return carry

    per_trip = 4 if n_chunks % 4 == 0 else 1

    def trip(i, carry):
        for j in range(per_trip):
            carry = chunk_step(i * per_trip + j, carry)
        return carry

    lax.fori_loop(0, n_chunks // per_trip, trip, 0)

    @pl.when(t == nt - 1)
    def _gla_out():
        st = st_ref[...]
        for h in range(GLA_HEADS):
            gla_out_ref[0, h] = st[:, h * GLA_DK:(h + 1) * GLA_DK].T

    y = _dot(mixbuf[...], w_o_ref[...])
    x1 = _layer_norm(DN_ALPHA * x + y, ln1g_ref[...], ln1b_ref[...])
    x1_ref[...] = x1
    for j, seg in enumerate(_pack_segments(x1)):
        x1p_ref[j] = seg
    x_hi, x_lo = _split_bf16(x1)
    lg = _dot(x_hi, wr_ref[...])
    logit_ref[...] = lg[:, 0:128] + lg[:, 128:256] + _dot(x_lo, wr_ref[:, 0:128]) + br_ref[...]


def _mixer(x, conv_state, gla_state, p, *, tm, n_tok_total, tok_offset, token_bufs=None):
    bsz, seq, _ = x.shape
    chunk = min(CHUNK, seq)
    assert seq % tm == 0 and tm % chunk == 0 and chunk % SUB == 0 and tok_offset % tm == 0
    nt = seq // tm
    blk0 = tok_offset // tm
    full = lambda shape: pl.BlockSpec(shape, lambda b, t: (0,) * len(shape))
    alias = () if token_bufs is None else tuple(token_bufs)
    kern = functools.partial(_mixer_kernel, tm=tm, chunk=chunk, n_alias=len(alias))
    n_in = 16
    return pl.pallas_call(
        kern,
        grid=(bsz, nt),
        input_output_aliases={n_in + i: i for i in range(len(alias))},
        in_specs=[
            pl.BlockSpec((1, tm, D_MODEL), lambda b, t: (b, t, 0)),
            pl.BlockSpec((1, CONV_W - 1, CONV_CH), lambda b, t: (b, 0, 0)),
            pl.BlockSpec((1, GLA_HEADS, GLA_DK, GLA_DV), lambda b, t: (b, 0, 0, 0)),
            full((D_MODEL, IN_COLS_PAD)),
            full((CONV_W, CONV_CH)),
            full((1, CONV_CH)),
            full((1, CONV_CH)),
            full((1, CONV_CH)),
            full((GATE_PAD, GLA_K)),
            full((1, GLA_K)),
            full((1, GLA_V)),
            full((MIX_W, D_MODEL)),
            full((1, D_MODEL)),
            full((1, D_MODEL)),
            full((D_MODEL, 256)),
            full((1, 128)),
        ] + [pl.BlockSpec(memory_space=pl.ANY)] * len(alias),
        out_specs=[
            pl.BlockSpec((N_SEG, tm, SC_SEG), lambda b, t: (0, blk0 + b * nt + t, 0)),
            pl.BlockSpec((tm, D_MODEL), lambda b, t: (blk0 + b * nt + t, 0)),
            pl.BlockSpec((tm, 128), lambda b, t: (blk0 + b * nt + t, 0)),
            pl.BlockSpec((1, CONV_W - 1, CONV_CH), lambda b, t: (b, 0, 0)),
            pl.BlockSpec((1, GLA_HEADS, GLA_DK, GLA_DV), lambda b, t: (b, 0, 0, 0)),
        ],
        out_shape=[
            jax.ShapeDtypeStruct((N_SEG, n_tok_total, SC_SEG), U32),
            jax.ShapeDtypeStruct((n_tok_total, D_MODEL), F32),
            jax.ShapeDtypeStruct((n_tok_total, 128), F32),
            jax.ShapeDtypeStruct((bsz, CONV_W - 1, CONV_CH), F32),
            jax.ShapeDtypeStruct((bsz, GLA_HEADS, GLA_DK, GLA_DV), F32),
        ],
        scratch_shapes=[
            pltpu.VMEM((HALO + tm, CONV_CH), F32),
            pltpu.VMEM((tm, GLA_K), F32),
            pltpu.VMEM((tm, GLA_K), F32),
            pltpu.VMEM((tm, GLA_K), F32),
            pltpu.VMEM((tm, GLA_V), F32),
            pltpu.VMEM((tm, GLA_V), F32),
            pltpu.VMEM((tm, MIX_W), BF16),
            pltpu.VMEM((GLA_DV, GLA_K), F32),
        ],
        compiler_params=pltpu.CompilerParams(
            dimension_semantics=("arbitrary", "arbitrary"),
            vmem_limit_bytes=VMEM_LIMIT),
        name="mixer",
    )(x, conv_state, gla_state, p["w_in"], p["w_dw"], p["b_dw"], p["ln_conv_g"], p["ln_conv_b"],
      p["w_gate"], p["b_gate"], p["gla_norm_g"], p["w_o"], p["ln1_g"], p["ln1_b"],
      p["w_router"], p["b_router"], *alias)


def _route_kernel(lg_ref, ri_ref, rg_ref, cnt_ref, carry_ref, *, tr):
    i = pl.program_id(0)

    @pl.when(i == 0)
    def _init():
        carry_ref[...] = jnp.zeros((1, 128), F32)

    lane = lax.broadcasted_iota(jnp.int32, (tr, 128), 1)
    l = jnp.where(lane < N_EXPERTS, lg_ref[...], -jnp.inf)
    hots, vals, idxs = [], [], []
    for _ in range(TOP_K):
        m = jnp.max(l, axis=-1, keepdims=True)
        ik = jnp.min(jnp.where(l == m, lane, 128), axis=-1, keepdims=True)
        hot = lane == ik
        hots.append(hot)
        vals.append(m)
        idxs.append(ik)
        l = jnp.where(hot, -jnp.inf, l)
    es = [jnp.exp(v - vals[0]) for v in vals]
    denom = es[0] + es[1] + es[2] + es[3]
    member = jnp.zeros((tr, 128), F32)
    for hot in hots:
        member = member + hot.astype(F32)
    strict = (lax.broadcasted_iota(jnp.int32, (tr, tr), 0)
              > lax.broadcasted_iota(jnp.int32, (tr, tr), 1)).astype(BF16)
    cum = _dot(strict, member.astype(BF16)) + carry_ref[...]
    ri = jnp.zeros((tr, 128), jnp.int32)
    rg = jnp.zeros((tr, 128), F32)
    for k in range(TOP_K):
        rank = jnp.sum(jnp.where(hots[k], cum, 0.0), axis=-1, keepdims=True).astype(jnp.int32)
        ri = jnp.where(lane == k, idxs[k], ri)
        ri = jnp.where(lane == TOP_K + k, rank, ri)
        rg = jnp.where(lane == k, es[k] / denom, rg)
    ri_ref[...] = ri
    rg_ref[...] = rg
    carry_ref[...] = carry_ref[...] + jnp.sum(member, axis=0, keepdims=True)

    @pl.when(i == pl.num_programs(0) - 1)
    def _fin():
        cnt_ref[...] = carry_ref[...]


def _route(logits, *, tr):
    n_tok = logits.shape[0]
    assert n_tok % tr == 0
    return pl.pallas_call(
        functools.partial(_route_kernel, tr=tr),
        grid=(n_tok // tr,),
        in_specs=[pl.BlockSpec((tr, 128), lambda i: (i, 0))],
        out_specs=[pl.BlockSpec((tr, 128), lambda i: (i, 0)),
                   pl.BlockSpec((tr, 128), lambda i: (i, 0)),
                   pl.BlockSpec((1, 128), lambda i: (0, 0))],
        out_shape=[jax.ShapeDtypeStruct((n_tok, 128), jnp.int32),
                   jax.ShapeDtypeStruct((n_tok, 128), F32),
                   jax.ShapeDtypeStruct((1, 128), F32)],
        scratch_shapes=[pltpu.VMEM((1, 128), F32)],
        compiler_params=pltpu.CompilerParams(dimension_semantics=("arbitrary",)),
        name="route",
    )(logits)


def _ffn_kernel(be_ref, nu_ref, xs_ref, wgu_ref, bgu_ref, wdn_ref, bdn_ref, yb_ref, wgu_bf, wdn_bf):
    i = pl.program_id(0)
    e = be_ref[i]
    prev = be_ref[jnp.maximum(i - 1, 0)]

    @pl.when((i == 0) | (e != prev))
    def _cast():
        wgu_bf[...] = wgu_ref[0].astype(BF16)
        wdn_bf[...] = wdn_ref[0].astype(BF16)

    @pl.when(i < nu_ref[0])
    def _compute():
        x_lo, x_hi = _unpack_segments([xs_ref[j] for j in range(N_SEG)])
        hgu = (_dot(x_lo.astype(BF16), wgu_bf[0:HALF, :]) + _dot(x_hi.astype(BF16), wgu_bf[HALF:, :])
               + bgu_ref[0])
        gate = jnp.minimum(hgu[:, :D_FF], SWIGLU_LIMIT)
        up = jnp.clip(hgu[:, D_FF:], -SWIGLU_LIMIT, SWIGLU_LIMIT)
        act = (up + 1.0) * gate * _sigmoid(SWIGLU_ALPHA * gate)
        yb = _dot(act.astype(BF16), wdn_bf[...]) + bdn_ref[0]
        for j, seg in enumerate(_pack_segments(yb)):
            yb_ref[j] = seg


def _ffn(block_e, n_used, xs, w_gu, b_gu, w_down, b_down):
    n_rows = xs.shape[1]
    nb = n_rows // EXPERT_BLOCK
    grid_spec = pltpu.PrefetchScalarGridSpec(
        num_scalar_prefetch=2,
        grid=(nb,),
        in_specs=[
            pl.BlockSpec((N_SEG, EXPERT_BLOCK, SC_SEG), lambda i, be, nu: (0, i, 0)),
            pl.BlockSpec((1, D_MODEL, 2 * D_FF), lambda i, be, nu: (be[i], 0, 0)),
            pl.BlockSpec((1, 1, 2 * D_FF), lambda i, be, nu: (be[i], 0, 0)),
            pl.BlockSpec((1, D_FF, D_MODEL), lambda i, be, nu: (be[i], 0, 0)),
            pl.BlockSpec((1, 1, D_MODEL), lambda i, be, nu: (be[i], 0, 0)),
        ],
        out_specs=pl.BlockSpec((N_SEG, EXPERT_BLOCK, SC_SEG), lambda i, be, nu: (0, i, 0)),
        scratch_shapes=[pltpu.VMEM((D_MODEL, 2 * D_FF), BF16),
                        pltpu.VMEM((D_FF, D_MODEL), BF16)],
    )
    return pl.pallas_call(
        _ffn_kernel,
        grid_spec=grid_spec,
        out_shape=jax.ShapeDtypeStruct((N_SEG, n_rows, SC_SEG), U32),
        compiler_params=pltpu.CompilerParams(
            dimension_semantics=("arbitrary",), vmem_limit_bytes=VMEM_LIMIT),
        name="expert_ffn",
    )(block_e, n_used, xs, w_gu, b_gu.reshape(N_EXPERTS, 1, 2 * D_FF), w_down,
      b_down.reshape(N_EXPERTS, 1, D_MODEL))


def _combine_kernel(x1_ref, g_ref, rg_ref, ln2g_ref, ln2b_ref, yp_ref, ys_ref, *, n_prompt_blocks):
    i = pl.program_id(0)
    m_lo = jnp.zeros((x1_ref.shape[0], HALF), F32)
    m_hi = jnp.zeros((x1_ref.shape[0], HALF), F32)
    for k in range(TOP_K):
        g_lo, g_hi = _unpack_segments([g_ref[k, j] for j in range(N_SEG)])
        gate = rg_ref[:, k:k + 1]
        m_lo = m_lo + gate * g_lo
        m_hi = m_hi + gate * g_hi
    z = DN_ALPHA * x1_ref[...] + jnp.concatenate([m_lo, m_hi], axis=1)
    y = _layer_norm(z, ln2g_ref[...], ln2b_ref[...])

    @pl.when(i < n_prompt_blocks)
    def _p():
        yp_ref[...] = y

    @pl.when(i >= n_prompt_blocks)
    def _s():
        ys_ref[...] = y


def _combine(x1, g, rg, ln2_g, ln2_b, *, n_prompt, tf):
    n_tok = x1.shape[0]
    n_sample = n_tok - n_prompt
    assert n_prompt % tf == 0 and n_sample % tf == 0
    npb = n_prompt // tf
    return pl.pallas_call(
        functools.partial(_combine_kernel, n_prompt_blocks=npb),
        grid=(n_tok // tf,),
        in_specs=[pl.BlockSpec((tf, D_MODEL), lambda i: (i, 0)),
                  pl.BlockSpec((TOP_K, N_SEG, tf, SC_SEG), lambda i: (0, 0, i, 0)),
                  pl.BlockSpec((tf, 128), lambda i: (i, 0)),
                  pl.BlockSpec((1, D_MODEL), lambda i: (0, 0)),
                  pl.BlockSpec((1, D_MODEL), lambda i: (0, 0))],
        out_specs=[pl.BlockSpec((tf, D_MODEL), lambda i: (jnp.minimum(i, npb - 1), 0)),
                   pl.BlockSpec((tf, D_MODEL), lambda i: (jnp.maximum(i - npb, 0), 0))],
        out_shape=[jax.ShapeDtypeStruct((n_prompt, D_MODEL), F32),
                   jax.ShapeDtypeStruct((n_sample, D_MODEL), F32)],
        compiler_params=pltpu.CompilerParams(
            dimension_semantics=("arbitrary",), vmem_limit_bytes=VMEM_LIMIT),
        name="combine",
    )(x1, g, rg, ln2_g, ln2_b)


def _sc_mesh():
    return plsc.VectorSubcoreMesh(core_axis_name="c", subcore_axis_name="s")


def _sc_scatter_rows(src, dest, n_out):
    n_src, width = src.shape
    n_k = dest.shape[0]
    assert width == SC_SEG and n_src % SC_WINDOW == 0 and dest.shape[1] == n_src

    @pl.kernel(out_type=jax.ShapeDtypeStruct((n_out, width), src.dtype), mesh=_sc_mesh(),
               scratch_types=[])
    def scatter_kernel(src_hbm, idx_hbm, out_hbm):
        def body(src_vmem, idx_vmem):
            for k in range(n_k):
                pltpu.sync_copy(src_vmem, out_hbm.at[idx_vmem.at[k]])

        pltpu.emit_pipeline(
            body, grid=(n_src // SC_WINDOW,),
            in_specs=[pl.BlockSpec((SC_WINDOW, width), lambda i: (i, 0)),
                      pl.BlockSpec((n_k, SC_WINDOW), lambda i: (0, i))],
            out_specs=[],
            core_axis_name=("c", "s"),
            dimension_semantics=(pltpu.PARALLEL,),
        )(src_hbm, idx_hbm)

    return scatter_kernel(src, dest)


def _sc_gather_rows(src, idx):
    n_out = idx.shape[1]
    width = src.shape[1]
    assert width == SC_SEG and n_out % SC_WINDOW == 0

    @pl.kernel(out_type=jax.ShapeDtypeStruct((n_out, width), src.dtype), mesh=_sc_mesh(),
               scratch_types=[])
    def gather_kernel(src_hbm, idx_hbm, out_hbm):
        def body(idx_vmem, out_vmem):
            pltpu.sync_copy(src_hbm.at[idx_vmem.at[0]], out_vmem)

        pltpu.emit_pipeline(
            body, grid=(n_out // SC_WINDOW,),
            in_specs=[pl.BlockSpec((1, SC_WINDOW), lambda i: (0, i))],
            out_specs=[pl.BlockSpec((SC_WINDOW, width), lambda i: (i, 0))],
            core_axis_name=("c", "s"),
            dimension_semantics=(pltpu.PARALLEL,),
        )(idx_hbm, out_hbm)

    return gather_kernel(src, idx)


def _prep_params(w_in, w_dw, b_dw, ln_conv_g, ln_conv_b, w_gate_lr, b_gate, gla_norm_g, w_o,
                 ln1_g, ln1_b, w_router, b_router):
    row = lambda v: v.reshape(1, -1).astype(F32)
    w_in_p = jnp.pad(w_in, ((0, 0), (0, IN_COLS_PAD - w_in.shape[1]))).astype(BF16)
    w_gate = jnp.pad(w_gate_lr, ((0, GATE_PAD - GATE_RANK), (0, 0))).astype(BF16)
    wr_hi = w_router.astype(BF16)
    wr_lo = (w_router - wr_hi.astype(F32)).astype(BF16)
    padr = lambda m: jnp.pad(m, ((0, 0), (0, 128 - N_EXPERTS)))
    return {
        "w_in": w_in_p, "w_dw": w_dw, "b_dw": row(b_dw), "ln_conv_g": row(ln_conv_g),
        "ln_conv_b": row(ln_conv_b), "w_gate": w_gate, "b_gate": row(b_gate),
        "gla_norm_g": row(gla_norm_g), "w_o": w_o.astype(BF16), "ln1_g": row(ln1_g),
        "ln1_b": row(ln1_b), "w_router": jnp.concatenate([padr(wr_hi), padr(wr_lo)], axis=1),
        "b_router": jnp.pad(row(b_router), ((0, 0), (0, 128 - N_EXPERTS))),
    }


def _moe_layout(ri, counts):
    n_tok = ri.shape[0]
    idx = ri[:, 0:TOP_K]
    rank = ri[:, TOP_K:2 * TOP_K]
    cnt = counts[0, :N_EXPERTS].astype(jnp.int32)
    padded = (cnt + EXPERT_BLOCK - 1) // EXPERT_BLOCK * EXPERT_BLOCK
    pad_end = jnp.cumsum(padded)
    pad_start = pad_end - padded
    nb = -(-(n_tok * TOP_K) // EXPERT_BLOCK) + N_EXPERTS
    onehot = idx[:, :, None] == jnp.arange(N_EXPERTS, dtype=jnp.int32)[None, None, :]
    dest = rank + jnp.sum(jnp.where(onehot, pad_start[None, None, :], 0), axis=-1)
    block_row0 = jnp.arange(nb, dtype=jnp.int32) * EXPERT_BLOCK
    block_e = jnp.minimum(
        jnp.sum((pad_end[None, :] <= block_row0[:, None]).astype(jnp.int32), axis=1),
        N_EXPERTS - 1)
    n_used = (pad_end[-1:] // EXPERT_BLOCK).astype(jnp.int32)
    return dest, block_e, n_used, nb


def kernel(x_prompt, x_sample, state_conv, state_gla, w_in, w_dw, b_dw, ln_conv_g, ln_conv_b,
           w_gate_lr, b_gate, gla_norm_g, w_o, ln1_g, ln1_b, w_router, b_router, w_gu, b_gu,
           w_down, b_down, ln2_g, ln2_b):
    assert w_in.shape[0] == DEPTH
    l = 0
    p = _prep_params(w_in[l], w_dw[l], b_dw[l], ln_conv_g[l], ln_conv_b[l], w_gate_lr[l],
                     b_gate[l], gla_norm_g[l], w_o[l], ln1_g[l], ln1_b[l], w_router[l],
                     b_router[l])
    bp, tp, _ = x_prompt.shape
    bs, ts, _ = x_sample.shape
    zc = jnp.zeros((bp, CONV_W - 1, CONV_CH), F32)
    zs = jnp.zeros((bp, GLA_HEADS, GLA_DK, GLA_DV), F32)
    n_prompt = bp * tp
    n_tok = n_prompt + bs * ts
    x1p, x1, lg, conv_p, gla_p = _mixer(x_prompt, zc, zs, p, tm=512, n_tok_total=n_tok,
                                        tok_offset=0)
    x1p, x1, lg, conv_s, gla_s = _mixer(x_sample, state_conv[l], state_gla[l], p, tm=ts,
                                        n_tok_total=n_tok, tok_offset=n_prompt,
                                        token_bufs=(x1p, x1, lg))

    ri, rg, counts = _route(lg, tr=256)
    dest, block_e, n_used, nb = _moe_layout(ri, counts)
    n_slot = nb * EXPERT_BLOCK
    seg0 = jnp.arange(N_SEG, dtype=jnp.int32) * n_slot
    dest_seg = (dest.T[:, None, :] + seg0[None, :, None]).reshape(TOP_K, N_SEG * n_tok)
    xs = _sc_scatter_rows(x1p.reshape(N_SEG * n_tok, SC_SEG), dest_seg, N_SEG * n_slot)
    yb = _ffn(block_e, n_used, xs.reshape(N_SEG, n_slot, SC_SEG), w_gu[l], b_gu[l], w_down[l],
              b_down[l])
    g = _sc_gather_rows(yb.reshape(N_SEG * n_slot, SC_SEG), dest_seg.reshape(1, -1))
    yp, ys = _combine(x1, g.reshape(TOP_K, N_SEG, n_tok, SC_SEG), rg, ln2_g[l].reshape(1, -1),
                      ln2_b[l].reshape(1, -1), n_prompt=n_prompt, tf=256)
    return (yp.reshape(bp, tp, D_MODEL), ys.reshape(bs, ts, D_MODEL),
            conv_p[None], gla_p[None], conv_s[None], gla_s[None])
```

```python
import functools

import jax
import jax.numpy as jnp
from jax import lax
from jax.experimental import pallas as pl
from jax.experimental.pallas import tpu as pltpu
from jax.experimental.pallas import tpu_sc as plsc

F32 = jnp.float32
BF16 = jnp.bfloat16

D_MODEL = 1024
CHUNK = 64
SUB = 16
DIAG = 8
CONV_CH = 512
CONV_W = 31
HALO = 32
GLA_HEADS = 4
GLA_DK = 64
GLA_DV = 128
GLA_K = GLA_HEADS * GLA_DK
GLA_V = GLA_HEADS * GLA_DV
GATE_RANK = 16
GATE_PAD = 128
GATE_NORM = 16.0
MIX_W = CONV_CH + GLA_V
OFF_UV = 0
OFF_UG = OFF_UV + CONV_CH
OFF_Q = OFF_UG + CONV_CH
OFF_K = OFF_Q + GLA_K
OFF_V = OFF_K + GLA_K
OFF_G = OFF_V + GLA_V
OFF_A = OFF_G + GLA_V
IN_COLS_PAD = OFF_A + GATE_PAD
N_EXPERTS = 32
TOP_K = 4
D_FF = 1024
SWIGLU_LIMIT = 7.0
SWIGLU_ALPHA = 1.702
EXPERT_BLOCK = 256
LN_EPS = 1e-5
DEPTH = 1
DN_ALPHA = (2 * DEPTH) ** 0.25
VMEM_LIMIT = 56 * 1024 * 1024
HALF = D_MODEL // 2
SC_SEG = 256
SC_WINDOW = 128
N_SEG = HALF // SC_SEG
U32 = jnp.uint32


def _dot(a, b):
    return jnp.dot(a, b, preferred_element_type=F32)


def _dot_nt(a, b):
    return lax.dot_general(a, b, (((1,), (1,)), ((), ())), preferred_element_type=F32)


def _dot_tn(a, b):
    return lax.dot_general(a, b, (((0,), (0,)), ((), ())), preferred_element_type=F32)


def _layer_norm(x, g, b):
    mu = jnp.mean(x, axis=-1, keepdims=True)
    xc = x - mu
    var = jnp.mean(xc * xc, axis=-1, keepdims=True)
    return xc * lax.rsqrt(var + LN_EPS) * g + b


def _sigmoid(x):
    return 1.0 / (1.0 + jnp.exp(-x))


def _split_bf16(x):
    hi = x.astype(BF16)
    lo = (x - hi.astype(F32)).astype(BF16)
    return hi, lo


def _pack_segments(x):
    bits = pltpu.bitcast(x.astype(BF16).astype(F32), U32)
    word = (bits[:, :HALF] >> 16) | bits[:, HALF:]
    return [word[:, j * SC_SEG:(j + 1) * SC_SEG] for j in range(N_SEG)]


def _unpack_segments(segs):
    word = jnp.concatenate(segs, axis=1)
    lo = pltpu.bitcast(word << 16, F32)
    hi = pltpu.bitcast(word & jnp.uint32(0xFFFF0000), F32)
    return lo, hi


def _mixer_kernel(x_ref, cs_ref, gs_ref, w_in_ref, w_dw_ref, b_dw_ref, lncg_ref, lncb_ref,
                  wg_ref, bg_ref, gng_ref, w_o_ref, ln1g_ref, ln1b_ref, wr_ref, br_ref, *rest,
                  tm, chunk, n_alias):
    (x1p_ref, x1_ref, logit_ref, conv_out_ref, gla_out_ref,
     ubuf, qbuf, kbuf, labuf, vbuf, gbuf, mixbuf, st_ref) = rest[n_alias:]
    t = pl.program_id(1)
    nt = pl.num_programs(1)
    n_chunks = tm // chunk
    n_sub = chunk // SUB

    @pl.when(t == 0)
    def _init():
        ubuf[0:HALO - (CONV_W - 1), :] = jnp.zeros((HALO - (CONV_W - 1), CONV_CH), F32)
        ubuf[HALO - (CONV_W - 1):HALO, :] = cs_ref[0]
        for h in range(GLA_HEADS):
            st_ref[:, h * GLA_DK:(h + 1) * GLA_DK] = gs_ref[0, h].T

    x = x_ref[0]
    xb = x.astype(BF16)

    hv = _dot(xb, w_in_ref[:, OFF_UV:OFF_UG])
    hg = _dot(xb, w_in_ref[:, OFF_UG:OFF_Q])
    u = hv * _sigmoid(hg)
    ubuf[HALO:HALO + tm, :] = u
    lead = HALO - (CONV_W - 1)

    def proj_q():
        qbuf[...] = _dot(xb, w_in_ref[:, OFF_Q:OFF_K]) * (GLA_DK ** -0.5)

    def proj_k():
        kbuf[...] = _dot(xb, w_in_ref[:, OFF_K:OFF_V])

    def proj_v(half):
        lo = half * (GLA_V // 2)
        vbuf[:, lo:lo + GLA_V // 2] = _dot(xb, w_in_ref[:, OFF_V + lo:OFF_V + lo + GLA_V // 2])

    def proj_g(half):
        lo = half * (GLA_V // 2)
        gbuf[:, lo:lo + GLA_V // 2] = _dot(xb, w_in_ref[:, OFF_G + lo:OFF_G + lo + GLA_V // 2])

    def proj_gate():
        ha = _dot(xb, w_in_ref[:, OFF_A:IN_COLS_PAD])
        a = _dot(ha.astype(BF16), wg_ref[...]) + bg_ref[...]
        labuf[...] = ((jnp.minimum(a, 0.0) - jnp.log(1.0 + jnp.exp(-jnp.abs(a))))
                      * (1.0 / GATE_NORM))

    proj_jobs = [proj_q, proj_k, functools.partial(proj_v, 0), functools.partial(proj_v, 1),
                 functools.partial(proj_g, 0), functools.partial(proj_g, 1), proj_gate]

    acc = jnp.zeros((tm, CONV_CH), F32) + b_dw_ref[...]
    for r in range(8):
        if r < len(proj_jobs):
            proj_jobs[r]()
        rows = tm if r == 0 else tm + 8
        part = None
        for s in range(r, lead + CONV_W, 8):
            if s < lead:
                continue
            term = w_dw_ref[s - lead:s - lead + 1, :] * ubuf[s - r:s - r + rows, :]
            part = term if part is None else part + term
        acc = acc + part[r:r + tm, :]
    cact = _layer_norm(acc, lncg_ref[...], lncb_ref[...])
    cact = cact * _sigmoid(cact)
    mixbuf[:, 0:CONV_CH] = cact.astype(BF16)
    tail = ubuf[tm:tm + HALO, :]
    ubuf[0:HALO, :] = tail

    @pl.when(t == nt - 1)
    def _conv_out():
        conv_out_ref[0] = tail[HALO - (CONV_W - 1):, :]

    lane_k = lax.broadcasted_iota(jnp.int32, (1, GLA_K), 1) // GLA_DK
    row_c = lax.broadcasted_iota(jnp.int32, (chunk, 1), 0)
    tri = (lax.broadcasted_iota(jnp.int32, (chunk, chunk), 0)
           >= lax.broadcasted_iota(jnp.int32, (chunk, chunk), 1)).astype(BF16)
    e2 = (lax.broadcasted_iota(jnp.int32, (GLA_K, GLA_V), 0) // GLA_DK
          == lax.broadcasted_iota(jnp.int32, (GLA_K, GLA_V), 1) // GLA_DV).astype(BF16)
    row_s = lax.broadcasted_iota(jnp.int32, (GLA_HEADS * chunk, 1), 0)
    blk_s = (row_s % chunk) // SUB
    same_blk = blk_s == lax.broadcasted_iota(jnp.int32, (1, 128), 1) // SUB
    second_half = row_c % SUB >= DIAG
    row_g = lax.broadcasted_iota(jnp.int32, (1, DIAG, 1), 1)
    lane_v = lax.broadcasted_iota(jnp.int32, (1, GLA_V), 1) % GLA_DV

    def head_stack(m):
        return jnp.concatenate(
            [jnp.where(lane_k == h, m, 0.0) for h in range(GLA_HEADS)], axis=0).astype(BF16)

    def unstack(r, width):
        return jnp.concatenate(
            [r[h * chunk:(h + 1) * chunk, h * width:(h + 1) * width] for h in range(GLA_HEADS)],
            axis=1)

    def chunk_step(c, carry):
        base = pl.multiple_of(c * chunk, chunk)
        q_c = qbuf[pl.ds(base, chunk), :]
        k_c = kbuf[pl.ds(base, chunk), :]
        v_c = vbuf[pl.ds(base, chunk), :]
        v_b = v_c.astype(BF16)
        la_c = labuf[pl.ds(base, chunk), :]
        la_hi, la_lo = _split_bf16(la_c)
        b = _dot(tri, la_hi) + _dot(tri, la_lo)
        b_last = b[chunk - 1:chunk, :]
        st = st_ref[...]

        qe = q_c * jnp.exp(b)
        o2 = _dot_nt(head_stack(qe), st.astype(BF16))
        o = jnp.concatenate([o2[h * chunk:(h + 1) * chunk, :] for h in range(GLA_HEADS)], axis=1)

        a_off = None
        if n_sub > 1:
            r_rows = [b[0:SUB, :]] + [jnp.broadcast_to(b[SUB * i - 1:SUB * i, :], (SUB, GLA_K))
                                      for i in range(1, n_sub)]
            r_q = jnp.concatenate(r_rows, axis=0)
            q_t = jnp.where(row_c >= SUB, q_c * jnp.exp(jnp.minimum(b - r_q, 0.0)), 0.0)
            k_parts = []
            for i in range(1, n_sub):
                r_i = b[SUB * i - 1:SUB * i, :]
                k_t = jnp.where(row_c < SUB * i, k_c * jnp.exp(jnp.minimum(r_i - b, 0.0)), 0.0)
                k_parts.append(k_t)
                k_parts.append(jnp.zeros((128 - chunk, GLA_K), F32))
            k_cat = jnp.concatenate(k_parts, axis=0).astype(BF16)
            r = _dot_nt(head_stack(q_t), k_cat)
            a_off = r[:, 0:128]
            for i in range(2, n_sub):
                a_off = jnp.where(blk_s == i, r[:, (i - 1) * 128:i * 128], a_off)

        r_b = jnp.concatenate(
            [jnp.broadcast_to(b[SUB * i + DIAG - 1:SUB * i + DIAG, :], (SUB, GLA_K))
             for i in range(n_sub)], axis=0)
        q_b = jnp.where(second_half, q_c * jnp.exp(jnp.minimum(b - r_b, 0.0)), 0.0)
        k_b = jnp.where(second_half, 0.0, k_c * jnp.exp(jnp.minimum(r_b - b, 0.0)))
        k_b = jnp.concatenate([k_b, jnp.zeros((128 - chunk, GLA_K), F32)], axis=0).astype(BF16)
        a_half = jnp.where(same_blk, _dot_nt(head_stack(q_b), k_b), 0.0)
        a_off = a_half if a_off is None else a_off + a_half
        o1 = _dot(a_off[:, 0:chunk].astype(BF16), v_b)
        o = o + unstack(o1, GLA_DV)

        q3 = q_c.reshape(chunk // DIAG, DIAG, GLA_K)
        k3 = k_c.reshape(chunk // DIAG, DIAG, GLA_K)
        b3 = b.reshape(chunk // DIAG, DIAG, GLA_K)
        a_d = jnp.zeros((chunk, GLA_V), F32)
        for d in range(DIAG):
            k_s = k3 if d == 0 else pltpu.roll(k3, d, axis=1)
            b_s = b3 if d == 0 else pltpu.roll(b3, d, axis=1)
            p = jnp.where(row_g >= d, q3 * k_s * jnp.exp(jnp.minimum(b3 - b_s, 0.0)), 0.0)
            w_d = _dot(p.reshape(chunk, GLA_K).astype(BF16), e2)
            a_d = jnp.where(lane_v == row_c - d, w_d, a_d)
        a_d = a_d.astype(BF16)
        o = o + jnp.concatenate(
            [_dot(a_d[:, h * GLA_DV:h * GLA_DV + chunk], v_b[:, h * GLA_DV:(h + 1) * GLA_DV])
             for h in range(GLA_HEADS)], axis=1)

        ke = k_c * jnp.exp(b_last - b)
        r2 = _dot_tn(v_b, ke.astype(BF16))
        upd = jnp.zeros((GLA_DV, GLA_K), F32)
        for h in range(GLA_HEADS):
            upd = upd + jnp.where(lane_k == h, r2[h * GLA_DV:(h + 1) * GLA_DV, :], 0.0)
        st_ref[...] = st * jnp.exp(b_last) + upd

        g_c = gbuf[pl.ds(base, chunk), :]
        outs = []
        for h in range(GLA_HEADS):
            oh = o[:, h * GLA_DV:(h + 1) * GLA_DV]
            ms = jnp.mean(oh * oh, axis=-1, keepdims=True)
            outs.append(oh * lax.rsqrt(ms + LN_EPS))
        on = jnp.concatenate(outs, axis=1) * gng_ref[...]
        on = on * (g_c * _sigmoid(g_c))
        mixbuf[pl.ds(base, chunk), CONV_CH:MIX_W] = on.astype(BF16)
        return carry

    per_trip = 4 if n_chunks % 4 == 0 else 1

    def trip(i, carry):
        for j in range(per_trip):
            carry = chunk_step(i * per_trip + j, carry)
        return carry

    lax.fori_loop(0, n_chunks // per_trip, trip, 0)

    @pl.when(t == nt - 1)
    def _gla_out():
        st = st_ref[...]
        for h in range(GLA_HEADS):
            gla_out_ref[0, h] = st[:, h * GLA_DK:(h + 1) * GLA_DK].T

    y = _dot(mixbuf[...], w_o_ref[...])
    x1 = _layer_norm(DN_ALPHA * x + y, ln1g_ref[...], ln1b_ref[...])
    x1_ref[...] = x1
    for j, seg in enumerate(_pack_segments(x1)):
        x1p_ref[j] = seg
    x_hi, x_lo = _split_bf16(x1)
    lg = _dot(x_hi, wr_ref[...])
    logit_ref[...] = lg[:, 0:128] + lg[:, 128:256] + _dot(x_lo, wr_ref[:, 0:128]) + br_ref[...]


def _mixer(x, conv_state, gla_state, p, *, tm, n_tok_total, tok_offset, token_bufs=None):
    bsz, seq, _ = x.shape
    chunk = min(CHUNK, seq)
    assert seq % tm == 0 and tm % chunk == 0 and chunk % SUB == 0 and tok_offset % tm == 0
    nt = seq // tm
    blk0 = tok_offset // tm
    full = lambda shape: pl.BlockSpec(shape, lambda b, t: (0,) * len(shape))
    alias = () if token_bufs is None else tuple(token_bufs)
    kern = functools.partial(_mixer_kernel, tm=tm, chunk=chunk, n_alias=len(alias))
    n_in = 16
    return pl.pallas_call(
        kern,
        grid=(bsz, nt),
        input_output_aliases={n_in + i: i for i in range(len(alias))},
        in_specs=[
            pl.BlockSpec((1, tm, D_MODEL), lambda b, t: (b, t, 0)),
            pl.BlockSpec((1, CONV_W - 1, CONV_CH), lambda b, t: (b, 0, 0)),
            pl.BlockSpec((1, GLA_HEADS, GLA_DK, GLA_DV), lambda b, t: (b, 0, 0, 0)),
            full((D_MODEL, IN_COLS_PAD)),
            full((CONV_W, CONV_CH)),
            full((1, CONV_CH)),
            full((1, CONV_CH)),
            full((1, CONV_CH)),
            full((GATE_PAD, GLA_K)),
            full((1, GLA_K)),
            full((1, GLA_V)),
            full((MIX_W, D_MODEL)),
            full((1, D_MODEL)),
            full((1, D_MODEL)),
            full((D_MODEL, 256)),
            full((1, 128)),
        ] + [pl.BlockSpec(memory_space=pl.ANY)] * len(alias),
        out_specs=[
            pl.BlockSpec((N_SEG, tm, SC_SEG), lambda b, t: (0, blk0 + b * nt + t, 0)),
            pl.BlockSpec((tm, D_MODEL), lambda b, t: (blk0 + b * nt + t, 0)),
            pl.BlockSpec((tm, 128), lambda b, t: (blk0 + b * nt + t, 0)),
            pl.BlockSpec((1, CONV_W - 1, CONV_CH), lambda b, t: (b, 0, 0)),
            pl.BlockSpec((1, GLA_HEADS, GLA_DK, GLA_DV), lambda b, t: (b, 0, 0, 0)),
        ],
        out_shape=[
            jax.ShapeDtypeStruct((N_SEG, n_tok_total, SC_SEG), U32),
            jax.ShapeDtypeStruct((n_tok_total, D_MODEL), F32),
            jax.ShapeDtypeStruct((n_tok_total, 128), F32),
            jax.ShapeDtypeStruct((bsz, CONV_W - 1, CONV_CH), F32),
            jax.ShapeDtypeStruct((bsz, GLA_HEADS, GLA_DK, GLA_DV), F32),
        ],
        scratch_shapes=[
            pltpu.VMEM((HALO + tm, CONV_CH), F32),
            pltpu.VMEM((tm, GLA_K), F32),
            pltpu.VMEM((tm, GLA_K), F32),
            pltpu.VMEM((tm, GLA_K), F32),
            pltpu.VMEM((tm, GLA_V), F32),
            pltpu.VMEM((tm, GLA_V), F32),
            pltpu.VMEM((tm, MIX_W), BF16),
            pltpu.VMEM((GLA_DV, GLA_K), F32),
        ],
        compiler_params=pltpu.CompilerParams(
            dimension_semantics=("arbitrary", "arbitrary"),
            vmem_limit_bytes=VMEM_LIMIT),
        name="mixer",
    )(x, conv_state, gla_state, p["w_in"], p["w_dw"], p["b_dw"], p["ln_conv_g"], p["ln_conv_b"],
      p["w_gate"], p["b_gate"], p["gla_norm_g"], p["w_o"], p["ln1_g"], p["ln1_b"],
      p["w_router"], p["b_router"], *alias)


def _route_kernel(lg_ref, ri_ref, rg_ref, cnt_ref, carry_ref, *, tr):
    i = pl.program_id(0)

    @pl.when(i == 0)
    def _init():
        carry_ref[...] = jnp.zeros((N_EXPERTS, 128), F32)

    l = lg_ref[...].T[0:N_EXPERTS, :]
    row = lax.broadcasted_iota(jnp.int32, (N_EXPERTS, tr), 0)
    hots, vals, idxs = [], [], []
    for _ in range(TOP_K):
        m = jnp.max(l, axis=0, keepdims=True)
        ik = jnp.min(jnp.where(l == m, row, N_EXPERTS), axis=0, keepdims=True)
        hot = row == ik
        hots.append(hot)
        vals.append(m)
        idxs.append(ik)
        l = jnp.where(hot, -jnp.inf, l)
    es = [jnp.exp(v - vals[0]) for v in vals]
    denom = es[0] + es[1] + es[2] + es[3]
    member = jnp.zeros((N_EXPERTS, tr), F32)
    for hot in hots:
        member = member + hot.astype(F32)
    before = (lax.broadcasted_iota(jnp.int32, (tr, tr), 0)
              < lax.broadcasted_iota(jnp.int32, (tr, tr), 1)).astype(BF16)
    cum = _dot(member.astype(BF16), before) + carry_ref[:, 0:1]
    ranks = [jnp.sum(jnp.where(hot, cum, 0.0), axis=0, keepdims=True).astype(jnp.int32)
             for hot in hots]
    ri_ref[...] = jnp.concatenate(idxs + ranks, axis=0)
    gates = jnp.concatenate([e / denom for e in es] + [jnp.zeros((128 - TOP_K, tr), F32)], axis=0)
    rg_ref[...] = gates.T
    carry_ref[...] = carry_ref[...] + jnp.sum(member, axis=1, keepdims=True)

    @pl.when(i == pl.num_programs(0) - 1)
    def _fin():
        cnt_ref[...] = carry_ref[...]


def _route(logits, *, tr):
    n_tok = logits.shape[0]
    assert n_tok % tr == 0
    return pl.pallas_call(
        functools.partial(_route_kernel, tr=tr),
        grid=(n_tok // tr,),
        in_specs=[pl.BlockSpec((tr, 128), lambda i: (i, 0))],
        out_specs=[pl.BlockSpec((2 * TOP_K, tr), lambda i: (0, i)),
                   pl.BlockSpec((tr, 128), lambda i: (i, 0)),
                   pl.BlockSpec((N_EXPERTS, 128), lambda i: (0, 0))],
        out_shape=[jax.ShapeDtypeStruct((2 * TOP_K, n_tok), jnp.int32),
                   jax.ShapeDtypeStruct((n_tok, 128), F32),
                   jax.ShapeDtypeStruct((N_EXPERTS, 128), F32)],
        scratch_shapes=[pltpu.VMEM((N_EXPERTS, 128), F32)],
        compiler_params=pltpu.CompilerParams(dimension_semantics=("arbitrary",)),
        name="route",
    )(logits)


def _ffn_kernel(gs_ref, nu_ref, xs_hbm, wgu_ref, bgu_ref, wdn_ref, bdn_ref, yb_hbm,
                wgu_bf, wdn_bf, xin, yout, in_sem, out_sem):
    e = pl.program_id(0)
    n_used = nu_ref[0]
    g_first = gs_ref[e]
    g_end = gs_ref[e + 1]

    def in_copy(g, slot):
        rows = pl.ds(pl.multiple_of(g * EXPERT_BLOCK, EXPERT_BLOCK), EXPERT_BLOCK)
        return pltpu.make_async_copy(xs_hbm.at[:, rows, :], xin.at[slot], in_sem.at[slot])

    def out_copy(g, slot):
        rows = pl.ds(pl.multiple_of(g * EXPERT_BLOCK, EXPERT_BLOCK), EXPERT_BLOCK)
        return pltpu.make_async_copy(yout.at[slot], yb_hbm.at[:, rows, :], out_sem.at[slot])

    @pl.when((e == 0) & (n_used > 0))
    def _prime():
        in_copy(0, 0).start()

    @pl.when(g_end > g_first)
    def _cast():
        wgu_bf[...] = wgu_ref[0].astype(BF16)
        wdn_bf[...] = wdn_ref[0].astype(BF16)

    def block(g, carry):
        slot = g % 2
        in_copy(g, slot).wait()

        @pl.when(g + 1 < n_used)
        def _prefetch():
            in_copy(g + 1, 1 - slot).start()

        @pl.when(g >= 2)
        def _free_out():
            out_copy(g - 2, slot).wait()

        x_lo, x_hi = _unpack_segments([xin[slot, j] for j in range(N_SEG)])
        hgu = (_dot(x_lo.astype(BF16), wgu_bf[0:HALF, :]) + _dot(x_hi.astype(BF16), wgu_bf[HALF:, :])
               + bgu_ref[0])
        gate = jnp.minimum(hgu[:, :D_FF], SWIGLU_LIMIT)
        up = jnp.clip(hgu[:, D_FF:], -SWIGLU_LIMIT, SWIGLU_LIMIT)
        act = (up + 1.0) * gate * _sigmoid(SWIGLU_ALPHA * gate)
        yb = _dot(act.astype(BF16), wdn_bf[...]) + bdn_ref[0]
        for j, seg in enumerate(_pack_segments(yb)):
            yout[slot, j] = seg
        out_copy(g, slot).start()
        return carry

    lax.fori_loop(g_first, g_end, block, 0)

    @pl.when(e == pl.num_programs(0) - 1)
    def _drain():
        @pl.when(n_used >= 2)
        def _():
            out_copy(n_used - 2, n_used % 2).wait()

        @pl.when(n_used >= 1)
        def _():
            out_copy(n_used - 1, (n_used - 1) % 2).wait()


def _ffn(block_start, n_used, xs, w_gu, b_gu, w_down, b_down):
    n_rows = xs.shape[1]
    grid_spec = pltpu.PrefetchScalarGridSpec(
        num_scalar_prefetch=2,
        grid=(N_EXPERTS,),
        in_specs=[
            pl.BlockSpec(memory_space=pl.ANY),
            pl.BlockSpec((1, D_MODEL, 2 * D_FF), lambda e, gs, nu: (e, 0, 0)),
            pl.BlockSpec((1, 1, 2 * D_FF), lambda e, gs, nu: (e, 0, 0)),
            pl.BlockSpec((1, D_FF, D_MODEL), lambda e, gs, nu: (e, 0, 0)),
            pl.BlockSpec((1, 1, D_MODEL), lambda e, gs, nu: (e, 0, 0)),
        ],
        out_specs=pl.BlockSpec(memory_space=pl.ANY),
        scratch_shapes=[pltpu.VMEM((D_MODEL, 2 * D_FF), BF16),
                        pltpu.VMEM((D_FF, D_MODEL), BF16),
                        pltpu.VMEM((2, N_SEG, EXPERT_BLOCK, SC_SEG), U32),
                        pltpu.VMEM((2, N_SEG, EXPERT_BLOCK, SC_SEG), U32),
                        pltpu.SemaphoreType.DMA((2,)),
                        pltpu.SemaphoreType.DMA((2,))],
    )
    return pl.pallas_call(
        _ffn_kernel,
        grid_spec=grid_spec,
        out_shape=jax.ShapeDtypeStruct((N_SEG, n_rows, SC_SEG), U32),
        compiler_params=pltpu.CompilerParams(
            dimension_semantics=("arbitrary",), vmem_limit_bytes=VMEM_LIMIT),
        name="expert_ffn",
    )(block_start, n_used, xs, w_gu, b_gu.reshape(N_EXPERTS, 1, 2 * D_FF), w_down,
      b_down.reshape(N_EXPERTS, 1, D_MODEL))


def _combine_kernel(x1_ref, g_ref, rg_ref, ln2g_ref, ln2b_ref, yp_ref, ys_ref, *, n_prompt_blocks):
    i = pl.program_id(0)
    m_lo = jnp.zeros((x1_ref.shape[0], HALF), F32)
    m_hi = jnp.zeros((x1_ref.shape[0], HALF), F32)
    for k in range(TOP_K):
        g_lo, g_hi = _unpack_segments([g_ref[k, j] for j in range(N_SEG)])
        gate = rg_ref[:, k:k + 1]
        m_lo = m_lo + gate * g_lo
        m_hi = m_hi + gate * g_hi
    z = DN_ALPHA * x1_ref[...] + jnp.concatenate([m_lo, m_hi], axis=1)
    y = _layer_norm(z, ln2g_ref[...], ln2b_ref[...])

    @pl.when(i < n_prompt_blocks)
    def _p():
        yp_ref[...] = y

    @pl.when(i >= n_prompt_blocks)
    def _s():
        ys_ref[...] = y


def _combine(x1, g, rg, ln2_g, ln2_b, *, n_prompt, tf):
    n_tok = x1.shape[0]
    n_sample = n_tok - n_prompt
    assert n_prompt % tf == 0 and n_sample % tf == 0
    npb = n_prompt // tf
    return pl.pallas_call(
        functools.partial(_combine_kernel, n_prompt_blocks=npb),
        grid=(n_tok // tf,),
        in_specs=[pl.BlockSpec((tf, D_MODEL), lambda i: (i, 0)),
                  pl.BlockSpec((TOP_K, N_SEG, tf, SC_SEG), lambda i: (0, 0, i, 0)),
                  pl.BlockSpec((tf, 128), lambda i: (i, 0)),
                  pl.BlockSpec((1, D_MODEL), lambda i: (0, 0)),
                  pl.BlockSpec((1, D_MODEL), lambda i: (0, 0))],
        out_specs=[pl.BlockSpec((tf, D_MODEL), lambda i: (jnp.minimum(i, npb - 1), 0)),
                   pl.BlockSpec((tf, D_MODEL), lambda i: (jnp.maximum(i - npb, 0), 0))],
        out_shape=[jax.ShapeDtypeStruct((n_prompt, D_MODEL), F32),
                   jax.ShapeDtypeStruct((n_sample, D_MODEL), F32)],
        compiler_params=pltpu.CompilerParams(
            dimension_semantics=("arbitrary",), vmem_limit_bytes=VMEM_LIMIT),
        name="combine",
    )(x1, g, rg, ln2_g, ln2_b)


def _sc_mesh():
    return plsc.VectorSubcoreMesh(core_axis_name="c", subcore_axis_name="s")


def _sc_scatter_rows(src, dest, n_out):
    n_src, width = src.shape
    n_k = dest.shape[0]
    assert width == SC_SEG and n_src % SC_WINDOW == 0 and dest.shape[1] == n_src

    @pl.kernel(out_type=jax.ShapeDtypeStruct((n_out, width), src.dtype), mesh=_sc_mesh(),
               scratch_types=[])
    def scatter_kernel(src_hbm, idx_hbm, out_hbm):
        def body(src_vmem, idx_vmem):
            for k in range(n_k):
                pltpu.sync_copy(src_vmem, out_hbm.at[idx_vmem.at[k]])

        pltpu.emit_pipeline(
            body, grid=(n_src // SC_WINDOW,),
            in_specs=[pl.BlockSpec((SC_WINDOW, width), lambda i: (i, 0)),
                      pl.BlockSpec((n_k, SC_WINDOW), lambda i: (0, i))],
            out_specs=[],
            core_axis_name=("c", "s"),
            dimension_semantics=(pltpu.PARALLEL,),
        )(src_hbm, idx_hbm)

    return scatter_kernel(src, dest)


def _sc_gather_rows(src, idx):
    n_out = idx.shape[1]
    width = src.shape[1]
    assert width == SC_SEG and n_out % SC_WINDOW == 0

    @pl.kernel(out_type=jax.ShapeDtypeStruct((n_out, width), src.dtype), mesh=_sc_mesh(),
               scratch_types=[])
    def gather_kernel(src_hbm, idx_hbm, out_hbm):
        def body(idx_vmem, out_vmem):
            pltpu.sync_copy(src_hbm.at[idx_vmem.at[0]], out_vmem)

        pltpu.emit_pipeline(
            body, grid=(n_out // SC_WINDOW,),
            in_specs=[pl.BlockSpec((1, SC_WINDOW), lambda i: (0, i))],
            out_specs=[pl.BlockSpec((SC_WINDOW, width), lambda i: (i, 0))],
            core_axis_name=("c", "s"),
            dimension_semantics=(pltpu.PARALLEL,),
        )(idx_hbm, out_hbm)

    return gather_kernel(src, idx)


def _prep_params(w_in, w_dw, b_dw, ln_conv_g, ln_conv_b, w_gate_lr, b_gate, gla_norm_g, w_o,
                 ln1_g, ln1_b, w_router, b_router):
    row = lambda v: v.reshape(1, -1).astype(F32)
    w_in_p = jnp.pad(w_in, ((0, 0), (0, IN_COLS_PAD - w_in.shape[1]))).astype(BF16)
    w_gate = jnp.pad(w_gate_lr, ((0, GATE_PAD - GATE_RANK), (0, 0))).astype(BF16)
    wr_hi = w_router.astype(BF16)
    wr_lo = (w_router - wr_hi.astype(F32)).astype(BF16)
    padr = lambda m: jnp.pad(m, ((0, 0), (0, 128 - N_EXPERTS)))
    return {
        "w_in": w_in_p, "w_dw": w_dw, "b_dw": row(b_dw), "ln_conv_g": row(ln_conv_g),
        "ln_conv_b": row(ln_conv_b), "w_gate": w_gate, "b_gate": row(b_gate),
        "gla_norm_g": row(gla_norm_g), "w_o": w_o.astype(BF16), "ln1_g": row(ln1_g),
        "ln1_b": row(ln1_b), "w_router": jnp.concatenate([padr(wr_hi), padr(wr_lo)], axis=1),
        "b_router": jnp.pad(row(b_router), ((0, 0), (0, 128 - N_EXPERTS))),
    }


def _moe_layout(ri, counts):
    n_tok = ri.shape[1]
    idx = ri[0:TOP_K, :]
    rank = ri[TOP_K:2 * TOP_K, :]
    cnt = counts[:, 0].astype(jnp.int32)
    padded = (cnt + EXPERT_BLOCK - 1) // EXPERT_BLOCK * EXPERT_BLOCK
    pad_end = jnp.cumsum(padded)
    pad_start = pad_end - padded
    nb = -(-(n_tok * TOP_K) // EXPERT_BLOCK) + N_EXPERTS
    dest = rank
    for e in range(N_EXPERTS):
        dest = dest + jnp.where(idx == e, pad_start[e], 0)
    block_start = (jnp.concatenate([pad_start, pad_end[-1:]]) // EXPERT_BLOCK).astype(jnp.int32)
    n_used = block_start[-1:]
    return dest, block_start, n_used, nb


def kernel(x_prompt, x_sample, state_conv, state_gla, w_in, w_dw, b_dw, ln_conv_g, ln_conv_b,
           w_gate_lr, b_gate, gla_norm_g, w_o, ln1_g, ln1_b, w_router, b_router, w_gu, b_gu,
           w_down, b_down, ln2_g, ln2_b):
    assert w_in.shape[0] == DEPTH
    l = 0
    p = _prep_params(w_in[l], w_dw[l], b_dw[l], ln_conv_g[l], ln_conv_b[l], w_gate_lr[l],
                     b_gate[l], gla_norm_g[l], w_o[l], ln1_g[l], ln1_b[l], w_router[l],
                     b_router[l])
    bp, tp, _ = x_prompt.shape
    bs, ts, _ = x_sample.shape
    zc = jnp.zeros((bp, CONV_W - 1, CONV_CH), F32)
    zs = jnp.zeros((bp, GLA_HEADS, GLA_DK, GLA_DV), F32)
    n_prompt = bp * tp
    n_tok = n_prompt + bs * ts
    x1p, x1, lg, conv_p, gla_p = _mixer(x_prompt, zc, zs, p, tm=512, n_tok_total=n_tok,
                                        tok_offset=0)
    x1p, x1, lg, conv_s, gla_s = _mixer(x_sample, state_conv[l], state_gla[l], p, tm=ts,
                                        n_tok_total=n_tok, tok_offset=n_prompt,
                                        token_bufs=(x1p, x1, lg))

    ri, rg, counts = _route(lg, tr=768 if n_tok % 768 == 0 else 256)
    dest, block_start, n_used, nb = _moe_layout(ri, counts)
    n_slot = nb * EXPERT_BLOCK
    seg0 = jnp.arange(N_SEG, dtype=jnp.int32) * n_slot
    dest_seg = (dest[:, None, :] + seg0[None, :, None]).reshape(TOP_K, N_SEG * n_tok)
    xs = _sc_scatter_rows(x1p.reshape(N_SEG * n_tok, SC_SEG), dest_seg, N_SEG * n_slot)
    yb = _ffn(block_start, n_used, xs.reshape(N_SEG, n_slot, SC_SEG), w_gu[l], b_gu[l], w_down[l],
              b_down[l])
    g = _sc_gather_rows(yb.reshape(N_SEG * n_slot, SC_SEG), dest_seg.reshape(1, -1))
    yp, ys = _combine(x1, g.reshape(TOP_K, N_SEG, n_tok, SC_SEG), rg, ln2_g[l].reshape(1, -1),
                      ln2_b[l].reshape(1, -1), n_prompt=n_prompt, tf=256)
    return (yp.reshape(bp, tp, D_MODEL), ys.reshape(bs, ts, D_MODEL),
            conv_p[None], gla_p[None], conv_s[None], gla_s[None])
```

```python
import functools

import jax
import jax.numpy as jnp
from jax import lax
from jax.experimental import pallas as pl
from jax.experimental.pallas import tpu as pltpu
from jax.experimental.pallas import tpu_sc as plsc

F32 = jnp.float32
BF16 = jnp.bfloat16

D_MODEL = 1024
CHUNK = 64
SUB = 16
DIAG = 8
CONV_CH = 512
CONV_W = 31
HALO = 32
CONV_BLOCK = 32
COMBINE_PIECES = 4
GLA_HEADS = 4
GLA_DK = 64
GLA_DV = 128
GLA_K = GLA_HEADS * GLA_DK
GLA_V = GLA_HEADS * GLA_DV
GATE_RANK = 16
GATE_PAD = 128
GATE_NORM = 16.0
MIX_W = CONV_CH + GLA_V
OFF_UV = 0
OFF_UG = OFF_UV + CONV_CH
OFF_Q = OFF_UG + CONV_CH
OFF_K = OFF_Q + GLA_K
OFF_V = OFF_K + GLA_K
OFF_G = OFF_V + GLA_V
OFF_A = OFF_G + GLA_V
IN_COLS_PAD = OFF_A + GATE_PAD
N_EXPERTS = 32
TOP_K = 4
D_FF = 1024
SWIGLU_LIMIT = 7.0
SWIGLU_ALPHA = 1.702
EXPERT_BLOCK = 256
LN_EPS = 1e-5
DEPTH = 1
DN_ALPHA = (2 * DEPTH) ** 0.25
VMEM_LIMIT = 56 * 1024 * 1024
HALF = D_MODEL // 2
SC_SEG = 256
SC_WINDOW = 128
N_SEG = HALF // SC_SEG
U32 = jnp.uint32


def _dot(a, b):
    return jnp.dot(a, b, preferred_element_type=F32)


def _dot_nt(a, b):
    return lax.dot_general(a, b, (((1,), (1,)), ((), ())), preferred_element_type=F32)


def _dot_tn(a, b):
    return lax.dot_general(a, b, (((0,), (0,)), ((), ())), preferred_element_type=F32)


def _layer_norm(x, g, b):
    mu = jnp.mean(x, axis=-1, keepdims=True)
    xc = x - mu
    var = jnp.mean(xc * xc, axis=-1, keepdims=True)
    return xc * lax.rsqrt(var + LN_EPS) * g + b


def _sigmoid(x):
    return 1.0 / (1.0 + jnp.exp(-x))


def _split_bf16(x):
    hi = x.astype(BF16)
    lo = (x - hi.astype(F32)).astype(BF16)
    return hi, lo


def _pack_segments(x):
    bits = pltpu.bitcast(x.astype(BF16).astype(F32), U32)
    word = (bits[:, :HALF] >> 16) | bits[:, HALF:]
    return [word[:, j * SC_SEG:(j + 1) * SC_SEG] for j in range(N_SEG)]


def _unpack_segments(segs):
    word = jnp.concatenate(segs, axis=1)
    lo = pltpu.bitcast(word << 16, F32)
    hi = pltpu.bitcast(word & jnp.uint32(0xFFFF0000), F32)
    return lo, hi


def _mixer_kernel(x_ref, cs_ref, gs_ref, w_in_ref, w_dw_ref, b_dw_ref, lncg_ref, lncb_ref,
                  wg_ref, bg_ref, gng_ref, w_o_ref, ln1g_ref, ln1b_ref, wr_ref, br_ref, *rest,
                  tm, chunk, n_alias):
    (x1p_ref, x1_ref, logit_ref, conv_out_ref, gla_out_ref,
     ubuf, qbuf, kbuf, labuf, vbuf, gbuf, mixbuf, st_ref) = rest[n_alias:]
    t = pl.program_id(1)
    nt = pl.num_programs(1)
    n_chunks = tm // chunk
    n_sub = chunk // SUB

    @pl.when(t == 0)
    def _init():
        ubuf[0:HALO - (CONV_W - 1), :] = jnp.zeros((HALO - (CONV_W - 1), CONV_CH), F32)
        ubuf[HALO - (CONV_W - 1):HALO, :] = cs_ref[0]
        for h in range(GLA_HEADS):
            st_ref[:, h * GLA_DK:(h + 1) * GLA_DK] = gs_ref[0, h].T

    x = x_ref[0]
    xb = x.astype(BF16)

    hv = _dot(xb, w_in_ref[:, OFF_UV:OFF_UG])
    hg = _dot(xb, w_in_ref[:, OFF_UG:OFF_Q])
    u = hv * _sigmoid(hg)
    ubuf[HALO:HALO + tm, :] = u
    lead = HALO - (CONV_W - 1)

    def proj_q():
        qbuf[...] = _dot(xb, w_in_ref[:, OFF_Q:OFF_K]) * (GLA_DK ** -0.5)

    def proj_k():
        kbuf[...] = _dot(xb, w_in_ref[:, OFF_K:OFF_V])

    def proj_v(half):
        lo = half * (GLA_V // 2)
        vbuf[:, lo:lo + GLA_V // 2] = _dot(xb, w_in_ref[:, OFF_V + lo:OFF_V + lo + GLA_V // 2])

    def proj_g(half):
        lo = half * (GLA_V // 2)
        gbuf[:, lo:lo + GLA_V // 2] = _dot(xb, w_in_ref[:, OFF_G + lo:OFF_G + lo + GLA_V // 2])

    def proj_gate():
        ha = _dot(xb, w_in_ref[:, OFF_A:IN_COLS_PAD])
        a = _dot(ha.astype(BF16), wg_ref[...]) + bg_ref[...]
        labuf[...] = ((jnp.minimum(a, 0.0) - jnp.log(1.0 + jnp.exp(-jnp.abs(a))))
                      * (1.0 / GATE_NORM))

    proj_jobs = [proj_q, proj_k, functools.partial(proj_v, 0), functools.partial(proj_v, 1),
                 functools.partial(proj_g, 0), functools.partial(proj_g, 1), proj_gate]

    cb = min(CONV_BLOCK, tm)

    def conv_block(r0):
        acc = jnp.zeros((cb, CONV_CH), F32) + b_dw_ref[...]
        for r in range(8):
            rows = cb if r == 0 else cb + 8
            part = None
            for s in range(r, lead + CONV_W, 8):
                if s < lead:
                    continue
                term = w_dw_ref[s - lead:s - lead + 1, :] * ubuf[pl.ds(r0 + (s - r), rows), :]
                part = term if part is None else part + term
            acc = acc + part[r:r + cb, :]
        cact = _layer_norm(acc, lncg_ref[...], lncb_ref[...])
        cact = cact * _sigmoid(cact)
        mixbuf[pl.ds(r0, cb), 0:CONV_CH] = cact.astype(BF16)

    per_trip = 4 if n_chunks % 4 == 0 else 1
    n_trips = n_chunks // per_trip
    loop_blocks = (tm // 2) // (cb * n_trips) if n_chunks >= 4 else 0
    pre_blocks = tm // cb - loop_blocks * n_trips
    for blk in range(max(pre_blocks, len(proj_jobs))):
        if blk < len(proj_jobs):
            proj_jobs[blk]()
        if blk < pre_blocks:
            conv_block(blk * cb)

    lane_k = lax.broadcasted_iota(jnp.int32, (1, GLA_K), 1) // GLA_DK
    row_c = lax.broadcasted_iota(jnp.int32, (chunk, 1), 0)
    tri = (lax.broadcasted_iota(jnp.int32, (chunk, chunk), 0)
           >= lax.broadcasted_iota(jnp.int32, (chunk, chunk), 1)).astype(BF16)
    e2 = (lax.broadcasted_iota(jnp.int32, (GLA_K, GLA_V), 0) // GLA_DK
          == lax.broadcasted_iota(jnp.int32, (GLA_K, GLA_V), 1) // GLA_DV).astype(BF16)
    row_s = lax.broadcasted_iota(jnp.int32, (GLA_HEADS * chunk, 1), 0)
    blk_s = (row_s % chunk) // SUB
    same_blk = blk_s == lax.broadcasted_iota(jnp.int32, (1, 128), 1) // SUB
    second_half = row_c % SUB >= DIAG
    row_g = lax.broadcasted_iota(jnp.int32, (1, DIAG, 1), 1)
    lane_v = lax.broadcasted_iota(jnp.int32, (1, GLA_V), 1) % GLA_DV

    def head_stack(m):
        return jnp.concatenate(
            [jnp.where(lane_k == h, m, 0.0) for h in range(GLA_HEADS)], axis=0).astype(BF16)

    def unstack(r, width):
        return jnp.concatenate(
            [r[h * chunk:(h + 1) * chunk, h * width:(h + 1) * width] for h in range(GLA_HEADS)],
            axis=1)

    def chunk_step(c, carry):
        base = pl.multiple_of(c * chunk, chunk)
        q_c = qbuf[pl.ds(base, chunk), :]
        k_c = kbuf[pl.ds(base, chunk), :]
        v_c = vbuf[pl.ds(base, chunk), :]
        v_b = v_c.astype(BF16)
        la_c = labuf[pl.ds(base, chunk), :]
        la_hi, la_lo = _split_bf16(la_c)
        b = _dot(tri, la_hi) + _dot(tri, la_lo)
        b_last = b[chunk - 1:chunk, :]
        st = st_ref[...]

        qe = q_c * jnp.exp(b)
        o2 = _dot_nt(head_stack(qe), st.astype(BF16))
        o = jnp.concatenate([o2[h * chunk:(h + 1) * chunk, :] for h in range(GLA_HEADS)], axis=1)

        a_off = None
        if n_sub > 1:
            r_rows = [b[0:SUB, :]] + [jnp.broadcast_to(b[SUB * i - 1:SUB * i, :], (SUB, GLA_K))
                                      for i in range(1, n_sub)]
            r_q = jnp.concatenate(r_rows, axis=0)
            q_t = jnp.where(row_c >= SUB, q_c * jnp.exp(jnp.minimum(b - r_q, 0.0)), 0.0)
            k_parts = []
            for i in range(1, n_sub):
                r_i = b[SUB * i - 1:SUB * i, :]
                k_t = jnp.where(row_c < SUB * i, k_c * jnp.exp(jnp.minimum(r_i - b, 0.0)), 0.0)
                k_parts.append(k_t)
                k_parts.append(jnp.zeros((128 - chunk, GLA_K), F32))
            k_cat = jnp.concatenate(k_parts, axis=0).astype(BF16)
            r = _dot_nt(head_stack(q_t), k_cat)
            a_off = r[:, 0:128]
            for i in range(2, n_sub):
                a_off = jnp.where(blk_s == i, r[:, (i - 1) * 128:i * 128], a_off)

        r_b = jnp.concatenate(
            [jnp.broadcast_to(b[SUB * i + DIAG - 1:SUB * i + DIAG, :], (SUB, GLA_K))
             for i in range(n_sub)], axis=0)
        q_b = jnp.where(second_half, q_c * jnp.exp(jnp.minimum(b - r_b, 0.0)), 0.0)
        k_b = jnp.where(second_half, 0.0, k_c * jnp.exp(jnp.minimum(r_b - b, 0.0)))
        k_b = jnp.concatenate([k_b, jnp.zeros((128 - chunk, GLA_K), F32)], axis=0).astype(BF16)
        a_half = jnp.where(same_blk, _dot_nt(head_stack(q_b), k_b), 0.0)
        a_off = a_half if a_off is None else a_off + a_half

        q3 = q_c.reshape(chunk // DIAG, DIAG, GLA_K)
        k3 = k_c.reshape(chunk // DIAG, DIAG, GLA_K)
        b3 = b.reshape(chunk // DIAG, DIAG, GLA_K)
        ps = []
        for d in range(DIAG):
            k_s = k3 if d == 0 else pltpu.roll(k3, d, axis=1)
            b_s = b3 if d == 0 else pltpu.roll(b3, d, axis=1)
            p = jnp.where(row_g >= d, q3 * k_s * jnp.exp(jnp.minimum(b3 - b_s, 0.0)), 0.0)
            ps.append(p.reshape(chunk, GLA_K).astype(BF16))
        w_all = _dot(jnp.concatenate(ps, axis=0), e2)
        a_d = jnp.zeros((chunk, GLA_V), F32)
        for d in range(DIAG):
            a_d = jnp.where(lane_v == row_c - d, w_all[d * chunk:(d + 1) * chunk, :], a_d)
        a_off = a_off + jnp.concatenate(
            [a_d[:, h * GLA_DV:(h + 1) * GLA_DV] for h in range(GLA_HEADS)], axis=0)
        o1 = _dot(a_off[:, 0:chunk].astype(BF16), v_b)
        o = o + unstack(o1, GLA_DV)

        ke = k_c * jnp.exp(b_last - b)
        r2 = _dot_tn(v_b, ke.astype(BF16))
        upd = jnp.zeros((GLA_DV, GLA_K), F32)
        for h in range(GLA_HEADS):
            upd = upd + jnp.where(lane_k == h, r2[h * GLA_DV:(h + 1) * GLA_DV, :], 0.0)
        st_ref[...] = st * jnp.exp(b_last) + upd

        g_c = gbuf[pl.ds(base, chunk), :]
        outs = []
        for h in range(GLA_HEADS):
            oh = o[:, h * GLA_DV:(h + 1) * GLA_DV]
            ms = jnp.mean(oh * oh, axis=-1, keepdims=True)
            outs.append(oh * lax.rsqrt(ms + LN_EPS))
        on = jnp.concatenate(outs, axis=1) * gng_ref[...]
        on = on * (g_c * _sigmoid(g_c))
        mixbuf[pl.ds(base, chunk), CONV_CH:MIX_W] = on.astype(BF16)
        return carry

    def trip(i, carry):
        for j in range(max(per_trip, loop_blocks)):
            if j < loop_blocks:
                conv_block(pl.multiple_of((pre_blocks + i * loop_blocks + j) * cb, cb))
            if j < per_trip:
                carry = chunk_step(i * per_trip + j, carry)
        return carry

    lax.fori_loop(0, n_trips, trip, 0)

    tail = ubuf[tm:tm + HALO, :]
    ubuf[0:HALO, :] = tail

    @pl.when(t == nt - 1)
    def _conv_out():
        conv_out_ref[0] = tail[HALO - (CONV_W - 1):, :]

    @pl.when(t == nt - 1)
    def _gla_out():
        st = st_ref[...]
        for h in range(GLA_HEADS):
            gla_out_ref[0, h] = st[:, h * GLA_DK:(h + 1) * GLA_DK].T

    y = _dot(mixbuf[...], w_o_ref[...])
    x1 = _layer_norm(DN_ALPHA * x + y, ln1g_ref[...], ln1b_ref[...])
    x1_ref[...] = x1
    for j, seg in enumerate(_pack_segments(x1)):
        x1p_ref[j] = seg
    x_hi, x_lo = _split_bf16(x1)
    lg = _dot(x_hi, wr_ref[...])
    logit_ref[...] = lg[:, 0:128] + lg[:, 128:256] + _dot(x_lo, wr_ref[:, 0:128]) + br_ref[...]


def _mixer(x, conv_state, gla_state, p, *, tm, n_tok_total, tok_offset, token_bufs=None):
    bsz, seq, _ = x.shape
    chunk = min(CHUNK, seq)
    assert seq % tm == 0 and tm % chunk == 0 and chunk % SUB == 0 and tok_offset % tm == 0
    nt = seq // tm
    blk0 = tok_offset // tm
    full = lambda shape: pl.BlockSpec(shape, lambda b, t: (0,) * len(shape))
    alias = () if token_bufs is None else tuple(token_bufs)
    kern = functools.partial(_mixer_kernel, tm=tm, chunk=chunk, n_alias=len(alias))
    n_in = 16
    return pl.pallas_call(
        kern,
        grid=(bsz, nt),
        input_output_aliases={n_in + i: i for i in range(len(alias))},
        in_specs=[
            pl.BlockSpec((1, tm, D_MODEL), lambda b, t: (b, t, 0)),
            pl.BlockSpec((1, CONV_W - 1, CONV_CH), lambda b, t: (b, 0, 0)),
            pl.BlockSpec((1, GLA_HEADS, GLA_DK, GLA_DV), lambda b, t: (b, 0, 0, 0)),
            full((D_MODEL, IN_COLS_PAD)),
            full((CONV_W, CONV_CH)),
            full((1, CONV_CH)),
            full((1, CONV_CH)),
            full((1, CONV_CH)),
            full((GATE_PAD, GLA_K)),
            full((1, GLA_K)),
            full((1, GLA_V)),
            full((MIX_W, D_MODEL)),
            full((1, D_MODEL)),
            full((1, D_MODEL)),
            full((D_MODEL, 256)),
            full((1, 128)),
        ] + [pl.BlockSpec(memory_space=pl.ANY)] * len(alias),
        out_specs=[
            pl.BlockSpec((N_SEG, tm, SC_SEG), lambda b, t: (0, blk0 + b * nt + t, 0)),
            pl.BlockSpec((tm, D_MODEL), lambda b, t: (blk0 + b * nt + t, 0)),
            pl.BlockSpec((tm, 128), lambda b, t: (blk0 + b * nt + t, 0)),
            pl.BlockSpec((1, CONV_W - 1, CONV_CH), lambda b, t: (b, 0, 0)),
            pl.BlockSpec((1, GLA_HEADS, GLA_DK, GLA_DV), lambda b, t: (b, 0, 0, 0)),
        ],
        out_shape=[
            jax.ShapeDtypeStruct((N_SEG, n_tok_total, SC_SEG), U32),
            jax.ShapeDtypeStruct((n_tok_total, D_MODEL), F32),
            jax.ShapeDtypeStruct((n_tok_total, 128), F32),
            jax.ShapeDtypeStruct((bsz, CONV_W - 1, CONV_CH), F32),
            jax.ShapeDtypeStruct((bsz, GLA_HEADS, GLA_DK, GLA_DV), F32),
        ],
        scratch_shapes=[
            pltpu.VMEM((HALO + tm, CONV_CH), F32),
            pltpu.VMEM((tm, GLA_K), F32),
            pltpu.VMEM((tm, GLA_K), F32),
            pltpu.VMEM((tm, GLA_K), F32),
            pltpu.VMEM((tm, GLA_V), F32),
            pltpu.VMEM((tm, GLA_V), F32),
            pltpu.VMEM((tm, MIX_W), BF16),
            pltpu.VMEM((GLA_DV, GLA_K), F32),
        ],
        compiler_params=pltpu.CompilerParams(
            dimension_semantics=("arbitrary", "arbitrary"),
            vmem_limit_bytes=VMEM_LIMIT),
        name="mixer",
    )(x, conv_state, gla_state, p["w_in"], p["w_dw"], p["b_dw"], p["ln_conv_g"], p["ln_conv_b"],
      p["w_gate"], p["b_gate"], p["gla_norm_g"], p["w_o"], p["ln1_g"], p["ln1_b"],
      p["w_router"], p["b_router"], *alias)


def _route_kernel(lg_ref, ri_ref, rg_ref, cnt_ref, carry_ref, *, tr):
    i = pl.program_id(0)

    @pl.when(i == 0)
    def _init():
        carry_ref[...] = jnp.zeros((N_EXPERTS, 128), F32)

    l = lg_ref[...].T[0:N_EXPERTS, :]
    row = lax.broadcasted_iota(jnp.int32, (N_EXPERTS, tr), 0)
    hots, vals, idxs = [], [], []
    for _ in range(TOP_K):
        m = jnp.max(l, axis=0, keepdims=True)
        ik = jnp.min(jnp.where(l == m, row, N_EXPERTS), axis=0, keepdims=True)
        hot = row == ik
        hots.append(hot)
        vals.append(m)
        idxs.append(ik)
        l = jnp.where(hot, -jnp.inf, l)
    es = [jnp.exp(v - vals[0]) for v in vals]
    denom = es[0] + es[1] + es[2] + es[3]
    member = jnp.zeros((N_EXPERTS, tr), F32)
    for hot in hots:
        member = member + hot.astype(F32)
    before = (lax.broadcasted_iota(jnp.int32, (tr, tr), 0)
              < lax.broadcasted_iota(jnp.int32, (tr, tr), 1)).astype(BF16)
    cum = _dot(member.astype(BF16), before) + carry_ref[:, 0:1]
    ranks = [jnp.sum(jnp.where(hot, cum, 0.0), axis=0, keepdims=True).astype(jnp.int32)
             for hot in hots]
    ri_ref[...] = jnp.concatenate(idxs + ranks, axis=0)
    gates = jnp.concatenate([e / denom for e in es] + [jnp.zeros((128 - TOP_K, tr), F32)], axis=0)
    rg_ref[...] = gates.T
    carry_ref[...] = carry_ref[...] + jnp.sum(member, axis=1, keepdims=True)

    @pl.when(i == pl.num_programs(0) - 1)
    def _fin():
        cnt_ref[...] = carry_ref[...]


def _route(logits, *, tr):
    n_tok = logits.shape[0]
    assert n_tok % tr == 0
    return pl.pallas_call(
        functools.partial(_route_kernel, tr=tr),
        grid=(n_tok // tr,),
        in_specs=[pl.BlockSpec((tr, 128), lambda i: (i, 0))],
        out_specs=[pl.BlockSpec((2 * TOP_K, tr), lambda i: (0, i)),
                   pl.BlockSpec((tr, 128), lambda i: (i, 0)),
                   pl.BlockSpec((N_EXPERTS, 128), lambda i: (0, 0))],
        out_shape=[jax.ShapeDtypeStruct((2 * TOP_K, n_tok), jnp.int32),
                   jax.ShapeDtypeStruct((n_tok, 128), F32),
                   jax.ShapeDtypeStruct((N_EXPERTS, 128), F32)],
        scratch_shapes=[pltpu.VMEM((N_EXPERTS, 128), F32)],
        compiler_params=pltpu.CompilerParams(dimension_semantics=("arbitrary",)),
        name="route",
    )(logits)


def _ffn_kernel(gs_ref, nu_ref, xs_hbm, wgu_ref, bgu_ref, wdn_ref, bdn_ref, yb_hbm,
                wgu_bf, wdn_bf, xin, yout, in_sem, out_sem):
    e = pl.program_id(0)
    n_used = nu_ref[0]
    g_first = gs_ref[e]
    g_end = gs_ref[e + 1]

    def in_copy(g, slot):
        rows = pl.ds(pl.multiple_of(g * EXPERT_BLOCK, EXPERT_BLOCK), EXPERT_BLOCK)
        return pltpu.make_async_copy(xs_hbm.at[:, rows, :], xin.at[slot], in_sem.at[slot])

    def out_copy(g, slot):
        rows = pl.ds(pl.multiple_of(g * EXPERT_BLOCK, EXPERT_BLOCK), EXPERT_BLOCK)
        return pltpu.make_async_copy(yout.at[slot], yb_hbm.at[:, rows, :], out_sem.at[slot])

    @pl.when((e == 0) & (n_used > 0))
    def _prime():
        in_copy(0, 0).start()

    @pl.when(g_end > g_first)
    def _cast():
        wgu_bf[...] = wgu_ref[0].astype(BF16)
        wdn_bf[...] = wdn_ref[0].astype(BF16)

    def block(g, carry):
        slot = g % 2
        in_copy(g, slot).wait()

        @pl.when(g + 1 < n_used)
        def _prefetch():
            in_copy(g + 1, 1 - slot).start()

        @pl.when(g >= 2)
        def _free_out():
            out_copy(g - 2, slot).wait()

        x_lo, x_hi = _unpack_segments([xin[slot, j] for j in range(N_SEG)])
        hgu = (_dot(x_lo.astype(BF16), wgu_bf[0:HALF, :]) + _dot(x_hi.astype(BF16), wgu_bf[HALF:, :])
               + bgu_ref[0])
        gate = jnp.minimum(hgu[:, :D_FF], SWIGLU_LIMIT)
        up = jnp.clip(hgu[:, D_FF:], -SWIGLU_LIMIT, SWIGLU_LIMIT)
        act = (up + 1.0) * gate * _sigmoid(SWIGLU_ALPHA * gate)
        yb = _dot(act.astype(BF16), wdn_bf[...]) + bdn_ref[0]
        for j, seg in enumerate(_pack_segments(yb)):
            yout[slot, j] = seg
        out_copy(g, slot).start()
        return carry

    lax.fori_loop(g_first, g_end, block, 0)

    @pl.when(e == pl.num_programs(0) - 1)
    def _drain():
        @pl.when(n_used >= 2)
        def _():
            out_copy(n_used - 2, n_used % 2).wait()

        @pl.when(n_used >= 1)
        def _():
            out_copy(n_used - 1, (n_used - 1) % 2).wait()


def _ffn(block_start, n_used, xs, w_gu, b_gu, w_down, b_down):
    n_rows = xs.shape[1]
    grid_spec = pltpu.PrefetchScalarGridSpec(
        num_scalar_prefetch=2,
        grid=(N_EXPERTS,),
        in_specs=[
            pl.BlockSpec(memory_space=pl.ANY),
            pl.BlockSpec((1, D_MODEL, 2 * D_FF), lambda e, gs, nu: (e, 0, 0)),
            pl.BlockSpec((1, 1, 2 * D_FF), lambda e, gs, nu: (e, 0, 0)),
            pl.BlockSpec((1, D_FF, D_MODEL), lambda e, gs, nu: (e, 0, 0)),
            pl.BlockSpec((1, 1, D_MODEL), lambda e, gs, nu: (e, 0, 0)),
        ],
        out_specs=pl.BlockSpec(memory_space=pl.ANY),
        scratch_shapes=[pltpu.VMEM((D_MODEL, 2 * D_FF), BF16),
                        pltpu.VMEM((D_FF, D_MODEL), BF16),
                        pltpu.VMEM((2, N_SEG, EXPERT_BLOCK, SC_SEG), U32),
                        pltpu.VMEM((2, N_SEG, EXPERT_BLOCK, SC_SEG), U32),
                        pltpu.SemaphoreType.DMA((2,)),
                        pltpu.SemaphoreType.DMA((2,))],
    )
    return pl.pallas_call(
        _ffn_kernel,
        grid_spec=grid_spec,
        out_shape=jax.ShapeDtypeStruct((N_SEG, n_rows, SC_SEG), U32),
        compiler_params=pltpu.CompilerParams(
            dimension_semantics=("arbitrary",), vmem_limit_bytes=VMEM_LIMIT),
        name="expert_ffn",
    )(block_start, n_used, xs, w_gu, b_gu.reshape(N_EXPERTS, 1, 2 * D_FF), w_down,
      b_down.reshape(N_EXPERTS, 1, D_MODEL))


def _combine_kernel(x1_ref, g_ref, rg_ref, ln2g_ref, ln2b_ref, *rest, blk0, n_prompt_blocks,
                    n_alias, with_sample):
    yp_ref = rest[n_alias]
    i = blk0 + pl.program_id(0)
    m_lo = jnp.zeros((x1_ref.shape[0], HALF), F32)
    m_hi = jnp.zeros((x1_ref.shape[0], HALF), F32)
    for k in range(TOP_K):
        g_lo, g_hi = _unpack_segments([g_ref[k, j] for j in range(N_SEG)])
        gate = rg_ref[:, k:k + 1]
        m_lo = m_lo + gate * g_lo
        m_hi = m_hi + gate * g_hi
    z = DN_ALPHA * x1_ref[...] + jnp.concatenate([m_lo, m_hi], axis=1)
    y = _layer_norm(z, ln2g_ref[...], ln2b_ref[...])

    if with_sample:
        ys_ref = rest[n_alias + 1]

        @pl.when(i < n_prompt_blocks)
        def _p():
            yp_ref[...] = y

        @pl.when(i >= n_prompt_blocks)
        def _s():
            ys_ref[...] = y
    else:
        yp_ref[...] = y


def _combine(x1, g, rg, ln2_g, ln2_b, *, tok0, n_prompt, tf, yp_buf=None):
    n_tok = x1.shape[0]
    piece = g.shape[2]
    n_sample = n_tok - n_prompt
    assert n_prompt % tf == 0 and n_sample % tf == 0 and tok0 % tf == 0 and piece % tf == 0
    npb = n_prompt // tf
    blk0 = tok0 // tf
    with_sample = tok0 + piece > n_prompt
    assert not with_sample or tok0 + piece == n_tok
    alias = () if yp_buf is None else (yp_buf,)
    out_specs = [pl.BlockSpec((tf, D_MODEL), lambda i: (jnp.minimum(blk0 + i, npb - 1), 0))]
    out_shape = [jax.ShapeDtypeStruct((n_prompt, D_MODEL), F32)]
    if with_sample:
        out_specs.append(pl.BlockSpec((tf, D_MODEL), lambda i: (jnp.maximum(blk0 + i - npb, 0), 0)))
        out_shape.append(jax.ShapeDtypeStruct((n_sample, D_MODEL), F32))
    n_in = 5
    return pl.pallas_call(
        functools.partial(_combine_kernel, blk0=blk0, n_prompt_blocks=npb, n_alias=len(alias),
                          with_sample=with_sample),
        grid=(piece // tf,),
        input_output_aliases={n_in + i: i for i in range(len(alias))},
        in_specs=[pl.BlockSpec((tf, D_MODEL), lambda i: (blk0 + i, 0)),
                  pl.BlockSpec((TOP_K, N_SEG, tf, SC_SEG), lambda i: (0, 0, i, 0)),
                  pl.BlockSpec((tf, 128), lambda i: (blk0 + i, 0)),
                  pl.BlockSpec((1, D_MODEL), lambda i: (0, 0)),
                  pl.BlockSpec((1, D_MODEL), lambda i: (0, 0))]
        + [pl.BlockSpec(memory_space=pl.ANY)] * len(alias),
        out_specs=out_specs,
        out_shape=out_shape,
        compiler_params=pltpu.CompilerParams(
            dimension_semantics=("arbitrary",), vmem_limit_bytes=VMEM_LIMIT),
        name="combine",
    )(x1, g, rg, ln2_g, ln2_b, *alias)


def _sc_mesh():
    return plsc.VectorSubcoreMesh(core_axis_name="c", subcore_axis_name="s")


def _sc_scatter_rows(src, dest, n_out):
    n_src, width = src.shape
    n_k = dest.shape[0]
    assert width == SC_SEG and n_src % SC_WINDOW == 0 and dest.shape[1] == n_src

    @pl.kernel(out_type=jax.ShapeDtypeStruct((n_out, width), src.dtype), mesh=_sc_mesh(),
               scratch_types=[])
    def scatter_kernel(src_hbm, idx_hbm, out_hbm):
        def body(src_vmem, idx_vmem):
            for k in range(n_k):
                pltpu.sync_copy(src_vmem, out_hbm.at[idx_vmem.at[k]])

        pltpu.emit_pipeline(
            body, grid=(n_src // SC_WINDOW,),
            in_specs=[pl.BlockSpec((SC_WINDOW, width), lambda i: (i, 0)),
                      pl.BlockSpec((n_k, SC_WINDOW), lambda i: (0, i))],
            out_specs=[],
            core_axis_name=("c", "s"),
            dimension_semantics=(pltpu.PARALLEL,),
        )(src_hbm, idx_hbm)

    return scatter_kernel(src, dest)


def _sc_gather_rows(src, idx):
    n_out = idx.shape[1]
    width = src.shape[1]
    assert width == SC_SEG and n_out % SC_WINDOW == 0

    @pl.kernel(out_type=jax.ShapeDtypeStruct((n_out, width), src.dtype), mesh=_sc_mesh(),
               scratch_types=[])
    def gather_kernel(src_hbm, idx_hbm, out_hbm):
        def body(idx_vmem, out_vmem):
            pltpu.sync_copy(src_hbm.at[idx_vmem.at[0]], out_vmem)

        pltpu.emit_pipeline(
            body, grid=(n_out // SC_WINDOW,),
            in_specs=[pl.BlockSpec((1, SC_WINDOW), lambda i: (0, i))],
            out_specs=[pl.BlockSpec((SC_WINDOW, width), lambda i: (i, 0))],
            core_axis_name=("c", "s"),
            dimension_semantics=(pltpu.PARALLEL,),
        )(idx_hbm, out_hbm)

    return gather_kernel(src, idx)


def _prep_params(w_in, w_dw, b_dw, ln_conv_g, ln_conv_b, w_gate_lr, b_gate, gla_norm_g, w_o,
                 ln1_g, ln1_b, w_router, b_router):
    row = lambda v: v.reshape(1, -1).astype(F32)
    w_in_p = jnp.pad(w_in, ((0, 0), (0, IN_COLS_PAD - w_in.shape[1]))).astype(BF16)
    w_gate = jnp.pad(w_gate_lr, ((0, GATE_PAD - GATE_RANK), (0, 0))).astype(BF16)
    wr_hi = w_router.astype(BF16)
    wr_lo = (w_router - wr_hi.astype(F32)).astype(BF16)
    padr = lambda m: jnp.pad(m, ((0, 0), (0, 128 - N_EXPERTS)))
    return {
        "w_in": w_in_p, "w_dw": w_dw, "b_dw": row(b_dw), "ln_conv_g": row(ln_conv_g),
        "ln_conv_b": row(ln_conv_b), "w_gate": w_gate, "b_gate": row(b_gate),
        "gla_norm_g": row(gla_norm_g), "w_o": w_o.astype(BF16), "ln1_g": row(ln1_g),
        "ln1_b": row(ln1_b), "w_router": jnp.concatenate([padr(wr_hi), padr(wr_lo)], axis=1),
        "b_router": jnp.pad(row(b_router), ((0, 0), (0, 128 - N_EXPERTS))),
    }


def _moe_layout(ri, counts):
    n_tok = ri.shape[1]
    idx = ri[0:TOP_K, :]
    rank = ri[TOP_K:2 * TOP_K, :]
    cnt = counts[:, 0].astype(jnp.int32)
    padded = (cnt + EXPERT_BLOCK - 1) // EXPERT_BLOCK * EXPERT_BLOCK
    pad_end = jnp.cumsum(padded)
    pad_start = pad_end - padded
    nb = -(-(n_tok * TOP_K) // EXPERT_BLOCK) + N_EXPERTS
    dest = rank
    for e in range(N_EXPERTS):
        dest = dest + jnp.where(idx == e, pad_start[e], 0)
    block_start = (jnp.concatenate([pad_start, pad_end[-1:]]) // EXPERT_BLOCK).astype(jnp.int32)
    n_used = block_start[-1:]
    return dest, block_start, n_used, nb


def kernel(x_prompt, x_sample, state_conv, state_gla, w_in, w_dw, b_dw, ln_conv_g, ln_conv_b,
           w_gate_lr, b_gate, gla_norm_g, w_o, ln1_g, ln1_b, w_router, b_router, w_gu, b_gu,
           w_down, b_down, ln2_g, ln2_b):
    assert w_in.shape[0] == DEPTH
    l = 0
    p = _prep_params(w_in[l], w_dw[l], b_dw[l], ln_conv_g[l], ln_conv_b[l], w_gate_lr[l],
                     b_gate[l], gla_norm_g[l], w_o[l], ln1_g[l], ln1_b[l], w_router[l],
                     b_router[l])
    bp, tp, _ = x_prompt.shape
    bs, ts, _ = x_sample.shape
    zc = jnp.zeros((bp, CONV_W - 1, CONV_CH), F32)
    zs = jnp.zeros((bp, GLA_HEADS, GLA_DK, GLA_DV), F32)
    n_prompt = bp * tp
    n_tok = n_prompt + bs * ts
    x1p, x1, lg, conv_p, gla_p = _mixer(x_prompt, zc, zs, p, tm=512, n_tok_total=n_tok,
                                        tok_offset=0)
    x1p, x1, lg, conv_s, gla_s = _mixer(x_sample, state_conv[l], state_gla[l], p, tm=ts,
                                        n_tok_total=n_tok, tok_offset=n_prompt,
                                        token_bufs=(x1p, x1, lg))

    ri, rg, counts = _route(lg, tr=768 if n_tok % 768 == 0 else 256)
    dest, block_start, n_used, nb = _moe_layout(ri, counts)
    n_slot = nb * EXPERT_BLOCK
    seg0 = jnp.arange(N_SEG, dtype=jnp.int32) * n_slot
    dest_seg = (dest[:, None, :] + seg0[None, :, None]).reshape(TOP_K, N_SEG * n_tok)
    xs = _sc_scatter_rows(x1p.reshape(N_SEG * n_tok, SC_SEG), dest_seg, N_SEG * n_slot)
    yb = _ffn(block_start, n_used, xs.reshape(N_SEG, n_slot, SC_SEG), w_gu[l], b_gu[l], w_down[l],
              b_down[l])
    tf = 256
    piece = -(-n_prompt // (COMBINE_PIECES * tf)) * tf
    bounds = [min(p * piece, n_prompt) for p in range(COMBINE_PIECES)] + [n_tok]
    bounds = sorted(set(bounds))
    dest_kjt = dest_seg.reshape(TOP_K, N_SEG, n_tok)
    yp = None
    for t0, t1 in zip(bounds[:-1], bounds[1:]):
        g = _sc_gather_rows(yb.reshape(N_SEG * n_slot, SC_SEG),
                            dest_kjt[:, :, t0:t1].reshape(1, -1))
        outs = _combine(x1, g.reshape(TOP_K, N_SEG, t1 - t0, SC_SEG), rg,
                        ln2_g[l].reshape(1, -1), ln2_b[l].reshape(1, -1), tok0=t0,
                        n_prompt=n_prompt, tf=tf, yp_buf=yp)
        yp = outs[0]
    ys = outs[1]
    return (yp.reshape(bp, tp, D_MODEL), ys.reshape(bs, ts, D_MODEL),
            conv_p[None], gla_p[None], conv_s[None], gla_s[None])
```

```python
import functools

import jax
import jax.numpy as jnp
from jax import lax
from jax.experimental import pallas as pl
from jax.experimental.pallas import tpu as pltpu
from jax.experimental.pallas import tpu_sc as plsc

F32 = jnp.float32
BF16 = jnp.bfloat16

D_MODEL = 1024
CHUNK = 64
SUB = 16
DIAG = 8
CONV_CH = 512
CONV_W = 31
HALO = 32
CONV_BLOCK = 32
COMBINE_PIECES = 4
IN_AHEAD = 2
IN_RING = IN_AHEAD + 2
GLA_HEADS = 4
GLA_DK = 64
GLA_DV = 128
GLA_K = GLA_HEADS * GLA_DK
GLA_V = GLA_HEADS * GLA_DV
GATE_RANK = 16
GATE_PAD = 128
GATE_NORM = 16.0
MIX_W = CONV_CH + GLA_V
OFF_UV = 0
OFF_UG = OFF_UV + CONV_CH
OFF_Q = OFF_UG + CONV_CH
OFF_K = OFF_Q + GLA_K
OFF_V = OFF_K + GLA_K
OFF_G = OFF_V + GLA_V
OFF_A = OFF_G + GLA_V
IN_COLS_PAD = OFF_A + GATE_PAD
N_EXPERTS = 32
TOP_K = 4
D_FF = 1024
SWIGLU_LIMIT = 7.0
SWIGLU_ALPHA = 1.702
EXPERT_BLOCK = 256
LN_EPS = 1e-5
DEPTH = 1
DN_ALPHA = (2 * DEPTH) ** 0.25
VMEM_LIMIT = 56 * 1024 * 1024
HALF = D_MODEL // 2
SC_SEG = 256
SC_WINDOW = 128
N_SEG = HALF // SC_SEG
U32 = jnp.uint32


def _dot(a, b):
    return jnp.dot(a, b, preferred_element_type=F32)


def _dot_nt(a, b):
    return lax.dot_general(a, b, (((1,), (1,)), ((), ())), preferred_element_type=F32)


def _dot_tn(a, b):
    return lax.dot_general(a, b, (((0,), (0,)), ((), ())), preferred_element_type=F32)


def _layer_norm(x, g, b):
    mu = jnp.mean(x, axis=-1, keepdims=True)
    xc = x - mu
    var = jnp.mean(xc * xc, axis=-1, keepdims=True)
    return xc * lax.rsqrt(var + LN_EPS) * g + b


def _sigmoid(x):
    return 1.0 / (1.0 + jnp.exp(-x))


def _split_bf16(x):
    hi = x.astype(BF16)
    lo = (x - hi.astype(F32)).astype(BF16)
    return hi, lo


def _pack_segments(x):
    bits = pltpu.bitcast(x.astype(BF16).astype(F32), U32)
    word = (bits[:, :HALF] >> 16) | bits[:, HALF:]
    return [word[:, j * SC_SEG:(j + 1) * SC_SEG] for j in range(N_SEG)]


def _unpack_segments(segs):
    word = jnp.concatenate(segs, axis=1)
    lo = pltpu.bitcast(word << 16, F32)
    hi = pltpu.bitcast(word & jnp.uint32(0xFFFF0000), F32)
    return lo, hi


def _mixer_kernel(x_ref, cs_ref, gs_ref, w_in_ref, w_dw_ref, b_dw_ref, lncg_ref, lncb_ref,
                  wg_ref, bg_ref, gng_ref, w_o_ref, ln1g_ref, ln1b_ref, wr_ref, br_ref, *rest,
                  tm, chunk, n_alias):
    (x1p_ref, x1_ref, logit_ref, conv_out_ref, gla_out_ref,
     ubuf, qbuf, kbuf, labuf, vbuf, gbuf, mixbuf, st_ref) = rest[n_alias:]
    t = pl.program_id(1)
    nt = pl.num_programs(1)
    n_chunks = tm // chunk
    n_sub = chunk // SUB

    @pl.when(t == 0)
    def _init():
        ubuf[0:HALO - (CONV_W - 1), :] = jnp.zeros((HALO - (CONV_W - 1), CONV_CH), F32)
        ubuf[HALO - (CONV_W - 1):HALO, :] = cs_ref[0]
        for h in range(GLA_HEADS):
            st_ref[:, h * GLA_DK:(h + 1) * GLA_DK] = gs_ref[0, h].T

    x = x_ref[0]
    xb = x.astype(BF16)

    hv = _dot(xb, w_in_ref[:, OFF_UV:OFF_UG])
    hg = _dot(xb, w_in_ref[:, OFF_UG:OFF_Q])
    u = hv * _sigmoid(hg)
    ubuf[HALO:HALO + tm, :] = u
    lead = HALO - (CONV_W - 1)

    def proj_q():
        qbuf[...] = _dot(xb, w_in_ref[:, OFF_Q:OFF_K]) * (GLA_DK ** -0.5)

    def proj_k():
        kbuf[...] = _dot(xb, w_in_ref[:, OFF_K:OFF_V])

    def proj_v(half):
        lo = half * (GLA_V // 2)
        vbuf[:, lo:lo + GLA_V // 2] = _dot(xb, w_in_ref[:, OFF_V + lo:OFF_V + lo + GLA_V // 2])

    def proj_g(half):
        lo = half * (GLA_V // 2)
        gbuf[:, lo:lo + GLA_V // 2] = _dot(xb, w_in_ref[:, OFF_G + lo:OFF_G + lo + GLA_V // 2])

    def proj_gate():
        ha = _dot(xb, w_in_ref[:, OFF_A:IN_COLS_PAD])
        a = _dot(ha.astype(BF16), wg_ref[...]) + bg_ref[...]
        labuf[...] = ((jnp.minimum(a, 0.0) - jnp.log(1.0 + jnp.exp(-jnp.abs(a))))
                      * (1.0 / GATE_NORM))

    proj_jobs = [proj_q, proj_k, functools.partial(proj_v, 0), functools.partial(proj_v, 1),
                 functools.partial(proj_g, 0), functools.partial(proj_g, 1), proj_gate]

    cb = min(CONV_BLOCK, tm)

    def conv_block(r0):
        acc = jnp.zeros((cb, CONV_CH), F32) + b_dw_ref[...]
        for r in range(8):
            rows = cb if r == 0 else cb + 8
            part = None
            for s in range(r, lead + CONV_W, 8):
                if s < lead:
                    continue
                term = w_dw_ref[s - lead:s - lead + 1, :] * ubuf[pl.ds(r0 + (s - r), rows), :]
                part = term if part is None else part + term
            acc = acc + part[r:r + cb, :]
        cact = _layer_norm(acc, lncg_ref[...], lncb_ref[...])
        cact = cact * _sigmoid(cact)
        mixbuf[pl.ds(r0, cb), 0:CONV_CH] = cact.astype(BF16)

    per_trip = 4 if n_chunks % 4 == 0 else 1
    n_trips = n_chunks // per_trip
    loop_blocks = (tm // 2) // (cb * n_trips) if n_chunks >= 4 else 0
    pre_blocks = tm // cb - loop_blocks * n_trips
    for blk in range(max(pre_blocks, len(proj_jobs))):
        if blk < len(proj_jobs):
            proj_jobs[blk]()
        if blk < pre_blocks:
            conv_block(blk * cb)

    lane_k = lax.broadcasted_iota(jnp.int32, (1, GLA_K), 1) // GLA_DK
    row_c = lax.broadcasted_iota(jnp.int32, (chunk, 1), 0)
    tri = (lax.broadcasted_iota(jnp.int32, (chunk, chunk), 0)
           >= lax.broadcasted_iota(jnp.int32, (chunk, chunk), 1)).astype(BF16)
    e2 = (lax.broadcasted_iota(jnp.int32, (GLA_K, GLA_V), 0) // GLA_DK
          == lax.broadcasted_iota(jnp.int32, (GLA_K, GLA_V), 1) // GLA_DV).astype(BF16)
    row_s = lax.broadcasted_iota(jnp.int32, (GLA_HEADS * chunk, 1), 0)
    blk_s = (row_s % chunk) // SUB
    same_blk = blk_s == lax.broadcasted_iota(jnp.int32, (1, 128), 1) // SUB
    second_half = row_c % SUB >= DIAG
    row_g = lax.broadcasted_iota(jnp.int32, (1, DIAG, 1), 1)
    lane_v = lax.broadcasted_iota(jnp.int32, (1, GLA_V), 1) % GLA_DV

    def head_stack(m):
        return jnp.concatenate(
            [jnp.where(lane_k == h, m, 0.0) for h in range(GLA_HEADS)], axis=0).astype(BF16)

    def unstack(r, width):
        return jnp.concatenate(
            [r[h * chunk:(h + 1) * chunk, h * width:(h + 1) * width] for h in range(GLA_HEADS)],
            axis=1)

    def chunk_step(c, carry):
        base = pl.multiple_of(c * chunk, chunk)
        q_c = qbuf[pl.ds(base, chunk), :]
        k_c = kbuf[pl.ds(base, chunk), :]
        v_c = vbuf[pl.ds(base, chunk), :]
        v_b = v_c.astype(BF16)
        la_c = labuf[pl.ds(base, chunk), :]
        la_hi, la_lo = _split_bf16(la_c)
        b = _dot(tri, la_hi) + _dot(tri, la_lo)
        b_last = b[chunk - 1:chunk, :]
        st = st_ref[...]

        qe = q_c * jnp.exp(b)
        o2 = _dot_nt(head_stack(qe), st.astype(BF16))
        o = jnp.concatenate([o2[h * chunk:(h + 1) * chunk, :] for h in range(GLA_HEADS)], axis=1)

        a_off = None
        if n_sub > 1:
            r_rows = [b[0:SUB, :]] + [jnp.broadcast_to(b[SUB * i - 1:SUB * i, :], (SUB, GLA_K))
                                      for i in range(1, n_sub)]
            r_q = jnp.concatenate(r_rows, axis=0)
            q_t = jnp.where(row_c >= SUB, q_c * jnp.exp(jnp.minimum(b - r_q, 0.0)), 0.0)
            k_parts = []
            for i in range(1, n_sub):
                r_i = b[SUB * i - 1:SUB * i, :]
                k_t = jnp.where(row_c < SUB * i, k_c * jnp.exp(jnp.minimum(r_i - b, 0.0)), 0.0)
                k_parts.append(k_t)
                k_parts.append(jnp.zeros((128 - chunk, GLA_K), F32))
            k_cat = jnp.concatenate(k_parts, axis=0).astype(BF16)
            r = _dot_nt(head_stack(q_t), k_cat)
            a_off = r[:, 0:128]
            for i in range(2, n_sub):
                a_off = jnp.where(blk_s == i, r[:, (i - 1) * 128:i * 128], a_off)

        r_b = jnp.concatenate(
            [jnp.broadcast_to(b[SUB * i + DIAG - 1:SUB * i + DIAG, :], (SUB, GLA_K))
             for i in range(n_sub)], axis=0)
        q_b = jnp.where(second_half, q_c * jnp.exp(jnp.minimum(b - r_b, 0.0)), 0.0)
        k_b = jnp.where(second_half, 0.0, k_c * jnp.exp(jnp.minimum(r_b - b, 0.0)))
        k_b = jnp.concatenate([k_b, jnp.zeros((128 - chunk, GLA_K), F32)], axis=0).astype(BF16)
        a_half = jnp.where(same_blk, _dot_nt(head_stack(q_b), k_b), 0.0)
        a_off = a_half if a_off is None else a_off + a_half

        q3 = q_c.reshape(chunk // DIAG, DIAG, GLA_K)
        k3 = k_c.reshape(chunk // DIAG, DIAG, GLA_K)
        b3 = b.reshape(chunk // DIAG, DIAG, GLA_K)
        ps = []
        for d in range(DIAG):
            k_s = k3 if d == 0 else pltpu.roll(k3, d, axis=1)
            b_s = b3 if d == 0 else pltpu.roll(b3, d, axis=1)
            p = jnp.where(row_g >= d, q3 * k_s * jnp.exp(jnp.minimum(b3 - b_s, 0.0)), 0.0)
            ps.append(p.reshape(chunk, GLA_K).astype(BF16))
        w_all = _dot(jnp.concatenate(ps, axis=0), e2)
        a_d = jnp.zeros((chunk, GLA_V), F32)
        for d in range(DIAG):
            a_d = jnp.where(lane_v == row_c - d, w_all[d * chunk:(d + 1) * chunk, :], a_d)
        a_off = a_off + jnp.concatenate(
            [a_d[:, h * GLA_DV:(h + 1) * GLA_DV] for h in range(GLA_HEADS)], axis=0)
        o1 = _dot(a_off[:, 0:chunk].astype(BF16), v_b)
        o = o + unstack(o1, GLA_DV)

        ke = k_c * jnp.exp(b_last - b)
        r2 = _dot_tn(v_b, ke.astype(BF16))
        upd = jnp.zeros((GLA_DV, GLA_K), F32)
        for h in range(GLA_HEADS):
            upd = upd + jnp.where(lane_k == h, r2[h * GLA_DV:(h + 1) * GLA_DV, :], 0.0)
        st_ref[...] = st * jnp.exp(b_last) + upd

        g_c = gbuf[pl.ds(base, chunk), :]
        outs = []
        for h in range(GLA_HEADS):
            oh = o[:, h * GLA_DV:(h + 1) * GLA_DV]
            ms = jnp.mean(oh * oh, axis=-1, keepdims=True)
            outs.append(oh * lax.rsqrt(ms + LN_EPS))
        on = jnp.concatenate(outs, axis=1) * gng_ref[...]
        on = on * (g_c * _sigmoid(g_c))
        mixbuf[pl.ds(base, chunk), CONV_CH:MIX_W] = on.astype(BF16)
        return carry

    def trip(i, carry):
        for j in range(max(per_trip, loop_blocks)):
            if j < loop_blocks:
                conv_block(pl.multiple_of((pre_blocks + i * loop_blocks + j) * cb, cb))
            if j < per_trip:
                carry = chunk_step(i * per_trip + j, carry)
        return carry

    lax.fori_loop(0, n_trips, trip, 0)

    tail = ubuf[tm:tm + HALO, :]
    ubuf[0:HALO, :] = tail

    @pl.when(t == nt - 1)
    def _conv_out():
        conv_out_ref[0] = tail[HALO - (CONV_W - 1):, :]

    @pl.when(t == nt - 1)
    def _gla_out():
        st = st_ref[...]
        for h in range(GLA_HEADS):
            gla_out_ref[0, h] = st[:, h * GLA_DK:(h + 1) * GLA_DK].T

    y = _dot(mixbuf[...], w_o_ref[...])
    x1 = _layer_norm(DN_ALPHA * x + y, ln1g_ref[...], ln1b_ref[...])
    x1_ref[...] = x1
    for j, seg in enumerate(_pack_segments(x1)):
        x1p_ref[j] = seg
    x_hi, x_lo = _split_bf16(x1)
    lg = _dot(x_hi, wr_ref[...])
    logit_ref[...] = lg[:, 0:128] + lg[:, 128:256] + _dot(x_lo, wr_ref[:, 0:128]) + br_ref[...]


def _mixer(x, conv_state, gla_state, p, *, tm, n_tok_total, tok_offset, token_bufs=None):
    bsz, seq, _ = x.shape
    chunk = min(CHUNK, seq)
    assert seq % tm == 0 and tm % chunk == 0 and chunk % SUB == 0 and tok_offset % tm == 0
    nt = seq // tm
    blk0 = tok_offset // tm
    full = lambda shape: pl.BlockSpec(shape, lambda b, t: (0,) * len(shape))
    alias = () if token_bufs is None else tuple(token_bufs)
    kern = functools.partial(_mixer_kernel, tm=tm, chunk=chunk, n_alias=len(alias))
    n_in = 16
    return pl.pallas_call(
        kern,
        grid=(bsz, nt),
        input_output_aliases={n_in + i: i for i in range(len(alias))},
        in_specs=[
            pl.BlockSpec((1, tm, D_MODEL), lambda b, t: (b, t, 0)),
            pl.BlockSpec((1, CONV_W - 1, CONV_CH), lambda b, t: (b, 0, 0)),
            pl.BlockSpec((1, GLA_HEADS, GLA_DK, GLA_DV), lambda b, t: (b, 0, 0, 0)),
            full((D_MODEL, IN_COLS_PAD)),
            full((CONV_W, CONV_CH)),
            full((1, CONV_CH)),
            full((1, CONV_CH)),
            full((1, CONV_CH)),
            full((GATE_PAD, GLA_K)),
            full((1, GLA_K)),
            full((1, GLA_V)),
            full((MIX_W, D_MODEL)),
            full((1, D_MODEL)),
            full((1, D_MODEL)),
            full((D_MODEL, 256)),
            full((1, 128)),
        ] + [pl.BlockSpec(memory_space=pl.ANY)] * len(alias),
        out_specs=[
            pl.BlockSpec((N_SEG, tm, SC_SEG), lambda b, t: (0, blk0 + b * nt + t, 0)),
            pl.BlockSpec((tm, D_MODEL), lambda b, t: (blk0 + b * nt + t, 0)),
            pl.BlockSpec((tm, 128), lambda b, t: (blk0 + b * nt + t, 0)),
            pl.BlockSpec((1, CONV_W - 1, CONV_CH), lambda b, t: (b, 0, 0)),
            pl.BlockSpec((1, GLA_HEADS, GLA_DK, GLA_DV), lambda b, t: (b, 0, 0, 0)),
        ],
        out_shape=[
            jax.ShapeDtypeStruct((N_SEG, n_tok_total, SC_SEG), U32),
            jax.ShapeDtypeStruct((n_tok_total, D_MODEL), F32),
            jax.ShapeDtypeStruct((n_tok_total, 128), F32),
            jax.ShapeDtypeStruct((bsz, CONV_W - 1, CONV_CH), F32),
            jax.ShapeDtypeStruct((bsz, GLA_HEADS, GLA_DK, GLA_DV), F32),
        ],
        scratch_shapes=[
            pltpu.VMEM((HALO + tm, CONV_CH), F32),
            pltpu.VMEM((tm, GLA_K), F32),
            pltpu.VMEM((tm, GLA_K), F32),
            pltpu.VMEM((tm, GLA_K), F32),
            pltpu.VMEM((tm, GLA_V), F32),
            pltpu.VMEM((tm, GLA_V), F32),
            pltpu.VMEM((tm, MIX_W), BF16),
            pltpu.VMEM((GLA_DV, GLA_K), F32),
        ],
        compiler_params=pltpu.CompilerParams(
            dimension_semantics=("arbitrary", "arbitrary"),
            vmem_limit_bytes=VMEM_LIMIT),
        name="mixer",
    )(x, conv_state, gla_state, p["w_in"], p["w_dw"], p["b_dw"], p["ln_conv_g"], p["ln_conv_b"],
      p["w_gate"], p["b_gate"], p["gla_norm_g"], p["w_o"], p["ln1_g"], p["ln1_b"],
      p["w_router"], p["b_router"], *alias)


def _route_kernel(lg_ref, ri_ref, rg_ref, cnt_ref, carry_ref, *, tr):
    i = pl.program_id(0)

    @pl.when(i == 0)
    def _init():
        carry_ref[...] = jnp.zeros((N_EXPERTS, 128), F32)

    l = lg_ref[...].T[0:N_EXPERTS, :]
    row = lax.broadcasted_iota(jnp.int32, (N_EXPERTS, tr), 0)
    hots, vals, idxs = [], [], []
    for _ in range(TOP_K):
        m = jnp.max(l, axis=0, keepdims=True)
        ik = jnp.min(jnp.where(l == m, row, N_EXPERTS), axis=0, keepdims=True)
        hot = row == ik
        hots.append(hot)
        vals.append(m)
        idxs.append(ik)
        l = jnp.where(hot, -jnp.inf, l)
    es = [jnp.exp(v - vals[0]) for v in vals]
    denom = es[0] + es[1] + es[2] + es[3]
    member = jnp.zeros((N_EXPERTS, tr), F32)
    for hot in hots:
        member = member + hot.astype(F32)
    before = (lax.broadcasted_iota(jnp.int32, (tr, tr), 0)
              < lax.broadcasted_iota(jnp.int32, (tr, tr), 1)).astype(BF16)
    cum = _dot(member.astype(BF16), before) + carry_ref[:, 0:1]
    ranks = [jnp.sum(jnp.where(hot, cum, 0.0), axis=0, keepdims=True).astype(jnp.int32)
             for hot in hots]
    ri_ref[...] = jnp.concatenate(idxs + ranks, axis=0)
    gates = jnp.concatenate([e / denom for e in es] + [jnp.zeros((128 - TOP_K, tr), F32)], axis=0)
    rg_ref[...] = gates.T
    carry_ref[...] = carry_ref[...] + jnp.sum(member, axis=1, keepdims=True)

    @pl.when(i == pl.num_programs(0) - 1)
    def _fin():
        cnt_ref[...] = carry_ref[...]


def _route(logits, *, tr):
    n_tok = logits.shape[0]
    assert n_tok % tr == 0
    return pl.pallas_call(
        functools.partial(_route_kernel, tr=tr),
        grid=(n_tok // tr,),
        in_specs=[pl.BlockSpec((tr, 128), lambda i: (i, 0))],
        out_specs=[pl.BlockSpec((2 * TOP_K, tr), lambda i: (0, i)),
                   pl.BlockSpec((tr, 128), lambda i: (i, 0)),
                   pl.BlockSpec((N_EXPERTS, 128), lambda i: (0, 0))],
        out_shape=[jax.ShapeDtypeStruct((2 * TOP_K, n_tok), jnp.int32),
                   jax.ShapeDtypeStruct((n_tok, 128), F32),
                   jax.ShapeDtypeStruct((N_EXPERTS, 128), F32)],
        scratch_shapes=[pltpu.VMEM((N_EXPERTS, 128), F32)],
        compiler_params=pltpu.CompilerParams(dimension_semantics=("arbitrary",)),
        name="route",
    )(logits)


def _ffn_kernel(gs_ref, nu_ref, xs_hbm, wgu_ref, bgu_ref, wdn_ref, bdn_ref, yb_hbm,
                wgu_bf, wdn_bf, xin, yout, in_sem, out_sem):
    e = pl.program_id(0)
    n_used = nu_ref[0]
    g_first = gs_ref[e]
    g_end = gs_ref[e + 1]

    def in_copy(g, slot):
        rows = pl.ds(pl.multiple_of(g * EXPERT_BLOCK, EXPERT_BLOCK), EXPERT_BLOCK)
        return pltpu.make_async_copy(xs_hbm.at[:, rows, :], xin.at[slot], in_sem.at[slot])

    def out_copy(g, slot):
        rows = pl.ds(pl.multiple_of(g * EXPERT_BLOCK, EXPERT_BLOCK), EXPERT_BLOCK)
        return pltpu.make_async_copy(yout.at[slot], yb_hbm.at[:, rows, :], out_sem.at[slot])

    @pl.when(e == 0)
    def _prime():
        for g in range(IN_AHEAD):
            @pl.when(g < n_used)
            def _():
                in_copy(g, g).start()

    @pl.when(g_end > g_first)
    def _cast():
        wgu_bf[...] = wgu_ref[0].astype(BF16)
        wdn_bf[...] = wdn_ref[0].astype(BF16)

    def blocks(g0, n):
        for j in range(n):
            g = g0 + j
            in_copy(g, g % IN_RING).wait()
        for j in range(n):
            g = g0 + IN_AHEAD + j

            @pl.when(g < n_used)
            def _prefetch():
                in_copy(g, g % IN_RING).start()
        for j in range(n):
            g = g0 + j

            @pl.when(g >= 2)
            def _free_out():
                out_copy(g - 2, g % 2).wait()
        for j in range(n):
            g = g0 + j
            x_lo, x_hi = _unpack_segments([xin[g % IN_RING, s] for s in range(N_SEG)])
            hgu = (_dot(x_lo.astype(BF16), wgu_bf[0:HALF, :])
                   + _dot(x_hi.astype(BF16), wgu_bf[HALF:, :]) + bgu_ref[0])
            gate = jnp.minimum(hgu[:, :D_FF], SWIGLU_LIMIT)
            up = jnp.clip(hgu[:, D_FF:], -SWIGLU_LIMIT, SWIGLU_LIMIT)
            act = (up + 1.0) * gate * _sigmoid(SWIGLU_ALPHA * gate)
            yb = _dot(act.astype(BF16), wdn_bf[...]) + bdn_ref[0]
            for s, seg in enumerate(_pack_segments(yb)):
                yout[g % 2, s] = seg
        for j in range(n):
            g = g0 + j
            out_copy(g, g % 2).start()

    n_pairs = (g_end - g_first) // 2

    def pair(i, carry):
        blocks(g_first + 2 * i, 2)
        return carry

    lax.fori_loop(0, n_pairs, pair, 0)

    @pl.when((g_end - g_first) % 2 == 1)
    def _odd():
        blocks(g_end - 1, 1)

    @pl.when(e == pl.num_programs(0) - 1)
    def _drain():
        @pl.when(n_used >= 2)
        def _():
            out_copy(n_used - 2, n_used % 2).wait()

        @pl.when(n_used >= 1)
        def _():
            out_copy(n_used - 1, (n_used - 1) % 2).wait()


def _ffn(block_start, n_used, xs, w_gu, b_gu, w_down, b_down):
    n_rows = xs.shape[1]
    grid_spec = pltpu.PrefetchScalarGridSpec(
        num_scalar_prefetch=2,
        grid=(N_EXPERTS,),
        in_specs=[
            pl.BlockSpec(memory_space=pl.ANY),
            pl.BlockSpec((1, D_MODEL, 2 * D_FF), lambda e, gs, nu: (e, 0, 0)),
            pl.BlockSpec((1, 1, 2 * D_FF), lambda e, gs, nu: (e, 0, 0)),
            pl.BlockSpec((1, D_FF, D_MODEL), lambda e, gs, nu: (e, 0, 0)),
            pl.BlockSpec((1, 1, D_MODEL), lambda e, gs, nu: (e, 0, 0)),
        ],
        out_specs=pl.BlockSpec(memory_space=pl.ANY),
        scratch_shapes=[pltpu.VMEM((D_MODEL, 2 * D_FF), BF16),
                        pltpu.VMEM((D_FF, D_MODEL), BF16),
                        pltpu.VMEM((IN_RING, N_SEG, EXPERT_BLOCK, SC_SEG), U32),
                        pltpu.VMEM((2, N_SEG, EXPERT_BLOCK, SC_SEG), U32),
                        pltpu.SemaphoreType.DMA((IN_RING,)),
                        pltpu.SemaphoreType.DMA((2,))],
    )
    return pl.pallas_call(
        _ffn_kernel,
        grid_spec=grid_spec,
        out_shape=jax.ShapeDtypeStruct((N_SEG, n_rows, SC_SEG), U32),
        compiler_params=pltpu.CompilerParams(
            dimension_semantics=("arbitrary",), vmem_limit_bytes=VMEM_LIMIT),
        name="expert_ffn",
    )(block_start, n_used, xs, w_gu, b_gu.reshape(N_EXPERTS, 1, 2 * D_FF), w_down,
      b_down.reshape(N_EXPERTS, 1, D_MODEL))


def _combine_kernel(x1_ref, g_ref, rg_ref, ln2g_ref, ln2b_ref, *rest, blk0, n_prompt_blocks,
                    n_alias, with_sample):
    yp_ref = rest[n_alias]
    i = blk0 + pl.program_id(0)
    m_lo = jnp.zeros((x1_ref.shape[0], HALF), F32)
    m_hi = jnp.zeros((x1_ref.shape[0], HALF), F32)
    for k in range(TOP_K):
        g_lo, g_hi = _unpack_segments([g_ref[k, j] for j in range(N_SEG)])
        gate = rg_ref[:, k:k + 1]
        m_lo = m_lo + gate * g_lo
        m_hi = m_hi + gate * g_hi
    z = DN_ALPHA * x1_ref[...] + jnp.concatenate([m_lo, m_hi], axis=1)
    y = _layer_norm(z, ln2g_ref[...], ln2b_ref[...])

    if with_sample:
        ys_ref = rest[n_alias + 1]

        @pl.when(i < n_prompt_blocks)
        def _p():
            yp_ref[...] = y

        @pl.when(i >= n_prompt_blocks)
        def _s():
            ys_ref[...] = y
    else:
        yp_ref[...] = y


def _combine(x1, g, rg, ln2_g, ln2_b, *, tok0, n_prompt, tf, yp_buf=None):
    n_tok = x1.shape[0]
    piece = g.shape[2]
    n_sample = n_tok - n_prompt
    assert n_prompt % tf == 0 and n_sample % tf == 0 and tok0 % tf == 0 and piece % tf == 0
    npb = n_prompt // tf
    blk0 = tok0 // tf
    with_sample = tok0 + piece > n_prompt
    assert not with_sample or tok0 + piece == n_tok
    alias = () if yp_buf is None else (yp_buf,)
    out_specs = [pl.BlockSpec((tf, D_MODEL), lambda i: (jnp.minimum(blk0 + i, npb - 1), 0))]
    out_shape = [jax.ShapeDtypeStruct((n_prompt, D_MODEL), F32)]
    if with_sample:
        out_specs.append(pl.BlockSpec((tf, D_MODEL), lambda i: (jnp.maximum(blk0 + i - npb, 0), 0)))
        out_shape.append(jax.ShapeDtypeStruct((n_sample, D_MODEL), F32))
    n_in = 5
    return pl.pallas_call(
        functools.partial(_combine_kernel, blk0=blk0, n_prompt_blocks=npb, n_alias=len(alias),
                          with_sample=with_sample),
        grid=(piece // tf,),
        input_output_aliases={n_in + i: i for i in range(len(alias))},
        in_specs=[pl.BlockSpec((tf, D_MODEL), lambda i: (blk0 + i, 0)),
                  pl.BlockSpec((TOP_K, N_SEG, tf, SC_SEG), lambda i: (0, 0, i, 0)),
                  pl.BlockSpec((tf, 128), lambda i: (blk0 + i, 0)),
                  pl.BlockSpec((1, D_MODEL), lambda i: (0, 0)),
                  pl.BlockSpec((1, D_MODEL), lambda i: (0, 0))]
        + [pl.BlockSpec(memory_space=pl.ANY)] * len(alias),
        out_specs=out_specs,
        out_shape=out_shape,
        compiler_params=pltpu.CompilerParams(
            dimension_semantics=("arbitrary",), vmem_limit_bytes=VMEM_LIMIT),
        name="combine",
    )(x1, g, rg, ln2_g, ln2_b, *alias)


def _sc_mesh():
    return plsc.VectorSubcoreMesh(core_axis_name="c", subcore_axis_name="s")


def _sc_scatter_rows(src, dest, n_out):
    n_src, width = src.shape
    n_k = dest.shape[0]
    assert width == SC_SEG and n_src % SC_WINDOW == 0 and dest.shape[1] == n_src

    @pl.kernel(out_type=jax.ShapeDtypeStruct((n_out, width), src.dtype), mesh=_sc_mesh(),
               scratch_types=[])
    def scatter_kernel(src_hbm, idx_hbm, out_hbm):
        def body(src_vmem, idx_vmem):
            for k in range(n_k):
                pltpu.sync_copy(src_vmem, out_hbm.at[idx_vmem.at[k]])

        pltpu.emit_pipeline(
            body, grid=(n_src // SC_WINDOW,),
            in_specs=[pl.BlockSpec((SC_WINDOW, width), lambda i: (i, 0)),
                      pl.BlockSpec((n_k, SC_WINDOW), lambda i: (0, i))],
            out_specs=[],
            core_axis_name=("c", "s"),
            dimension_semantics=(pltpu.PARALLEL,),
        )(src_hbm, idx_hbm)

    return scatter_kernel(src, dest)


def _sc_gather_rows(src, idx):
    n_out = idx.shape[1]
    width = src.shape[1]
    assert width == SC_SEG and n_out % SC_WINDOW == 0

    @pl.kernel(out_type=jax.ShapeDtypeStruct((n_out, width), src.dtype), mesh=_sc_mesh(),
               scratch_types=[])
    def gather_kernel(src_hbm, idx_hbm, out_hbm):
        def body(idx_vmem, out_vmem):
            pltpu.sync_copy(src_hbm.at[idx_vmem.at[0]], out_vmem)

        pltpu.emit_pipeline(
            body, grid=(n_out // SC_WINDOW,),
            in_specs=[pl.BlockSpec((1, SC_WINDOW), lambda i: (0, i))],
            out_specs=[pl.BlockSpec((SC_WINDOW, width), lambda i: (i, 0))],
            core_axis_name=("c", "s"),
            dimension_semantics=(pltpu.PARALLEL,),
        )(idx_hbm, out_hbm)

    return gather_kernel(src, idx)


def _prep_params(w_in, w_dw, b_dw, ln_conv_g, ln_conv_b, w_gate_lr, b_gate, gla_norm_g, w_o,
                 ln1_g, ln1_b, w_router, b_router):
    row = lambda v: v.reshape(1, -1).astype(F32)
    w_in_p = jnp.pad(w_in, ((0, 0), (0, IN_COLS_PAD - w_in.shape[1]))).astype(BF16)
    w_gate = jnp.pad(w_gate_lr, ((0, GATE_PAD - GATE_RANK), (0, 0))).astype(BF16)
    wr_hi = w_router.astype(BF16)
    wr_lo = (w_router - wr_hi.astype(F32)).astype(BF16)
    padr = lambda m: jnp.pad(m, ((0, 0), (0, 128 - N_EXPERTS)))
    return {
        "w_in": w_in_p, "w_dw": w_dw, "b_dw": row(b_dw), "ln_conv_g": row(ln_conv_g),
        "ln_conv_b": row(ln_conv_b), "w_gate": w_gate, "b_gate": row(b_gate),
        "gla_norm_g": row(gla_norm_g), "w_o": w_o.astype(BF16), "ln1_g": row(ln1_g),
        "ln1_b": row(ln1_b), "w_router": jnp.concatenate([padr(wr_hi), padr(wr_lo)], axis=1),
        "b_router": jnp.pad(row(b_router), ((0, 0), (0, 128 - N_EXPERTS))),
    }


def _moe_layout(ri, counts):
    n_tok = ri.shape[1]
    idx = ri[0:TOP_K, :]
    rank = ri[TOP_K:2 * TOP_K, :]
    cnt = counts[:, 0].astype(jnp.int32)
    padded = (cnt + EXPERT_BLOCK - 1) // EXPERT_BLOCK * EXPERT_BLOCK
    pad_end = jnp.cumsum(padded)
    pad_start = pad_end - padded
    nb = -(-(n_tok * TOP_K) // EXPERT_BLOCK) + N_EXPERTS
    dest = rank
    for e in range(N_EXPERTS):
        dest = dest + jnp.where(idx == e, pad_start[e], 0)
    block_start = (jnp.concatenate([pad_start, pad_end[-1:]]) // EXPERT_BLOCK).astype(jnp.int32)
    n_used = block_start[-1:]
    return dest, block_start, n_used, nb


def kernel(x_prompt, x_sample, state_conv, state_gla, w_in, w_dw, b_dw, ln_conv_g, ln_conv_b,
           w_gate_lr, b_gate, gla_norm_g, w_o, ln1_g, ln1_b, w_router, b_router, w_gu, b_gu,
           w_down, b_down, ln2_g, ln2_b):
    assert w_in.shape[0] == DEPTH
    l = 0
    p = _prep_params(w_in[l], w_dw[l], b_dw[l], ln_conv_g[l], ln_conv_b[l], w_gate_lr[l],
                     b_gate[l], gla_norm_g[l], w_o[l], ln1_g[l], ln1_b[l], w_router[l],
                     b_router[l])
    bp, tp, _ = x_prompt.shape
    bs, ts, _ = x_sample.shape
    zc = jnp.zeros((bp, CONV_W - 1, CONV_CH), F32)
    zs = jnp.zeros((bp, GLA_HEADS, GLA_DK, GLA_DV), F32)
    n_prompt = bp * tp
    n_tok = n_prompt + bs * ts
    x1p, x1, lg, conv_p, gla_p = _mixer(x_prompt, zc, zs, p, tm=512, n_tok_total=n_tok,
                                        tok_offset=0)
    x1p, x1, lg, conv_s, gla_s = _mixer(x_sample, state_conv[l], state_gla[l], p, tm=ts,
                                        n_tok_total=n_tok, tok_offset=n_prompt,
                                        token_bufs=(x1p, x1, lg))

    ri, rg, counts = _route(lg, tr=768 if n_tok % 768 == 0 else 256)
    dest, block_start, n_used, nb = _moe_layout(ri, counts)
    n_slot = nb * EXPERT_BLOCK
    seg0 = jnp.arange(N_SEG, dtype=jnp.int32) * n_slot
    dest_seg = (dest[:, None, :] + seg0[None, :, None]).reshape(TOP_K, N_SEG * n_tok)
    xs = _sc_scatter_rows(x1p.reshape(N_SEG * n_tok, SC_SEG), dest_seg, N_SEG * n_slot)
    yb = _ffn(block_start, n_used, xs.reshape(N_SEG, n_slot, SC_SEG), w_gu[l], b_gu[l], w_down[l],
              b_down[l])
    tf = 256
    piece = -(-n_prompt // (COMBINE_PIECES * tf)) * tf
    bounds = [min(p * piece, n_prompt) for p in range(COMBINE_PIECES)] + [n_tok]
    bounds = sorted(set(bounds))
    dest_kjt = dest_seg.reshape(TOP_K, N_SEG, n_tok)
    yp = None
    for t0, t1 in zip(bounds[:-1], bounds[1:]):
        g = _sc_gather_rows(yb.reshape(N_SEG * n_slot, SC_SEG),
                            dest_kjt[:, :, t0:t1].reshape(1, -1))
        outs = _combine(x1, g.reshape(TOP_K, N_SEG, t1 - t0, SC_SEG), rg,
                        ln2_g[l].reshape(1, -1), ln2_b[l].reshape(1, -1), tok0=t0,
                        n_prompt=n_prompt, tf=tf, yp_buf=yp)
        yp = outs[0]
    ys = outs[1]
    return (yp.reshape(bp, tp, D_MODEL), ys.reshape(bs, ts, D_MODEL),
            conv_p[None], gla_p[None], conv_s[None], gla_s[None])
```

```python
import functools

import jax
import jax.numpy as jnp
from jax import lax
from jax.experimental import pallas as pl
from jax.experimental.pallas import tpu as pltpu
from jax.experimental.pallas import tpu_sc as plsc

F32 = jnp.float32
BF16 = jnp.bfloat16

D_MODEL = 1024
CHUNK = 64
SUB = 16
DIAG = 8
CONV_CH = 512
CONV_W = 31
HALO = 32
CONV_BLOCK = 32
COMBINE_PIECES = 4
IN_AHEAD = 2
IN_RING = IN_AHEAD + 2
OUT_RING = 4
GLA_HEADS = 4
GLA_DK = 64
GLA_DV = 128
GLA_K = GLA_HEADS * GLA_DK
GLA_V = GLA_HEADS * GLA_DV
GATE_RANK = 16
GATE_PAD = 128
GATE_NORM = 16.0
MIX_W = CONV_CH + GLA_V
OFF_UV = 0
OFF_UG = OFF_UV + CONV_CH
OFF_Q = OFF_UG + CONV_CH
OFF_K = OFF_Q + GLA_K
OFF_V = OFF_K + GLA_K
OFF_G = OFF_V + GLA_V
OFF_A = OFF_G + GLA_V
IN_COLS_PAD = OFF_A + GATE_PAD
N_EXPERTS = 32
TOP_K = 4
D_FF = 1024
SWIGLU_LIMIT = 7.0
SWIGLU_ALPHA = 1.702
EXPERT_BLOCK = 256
LN_EPS = 1e-5
DEPTH = 1
DN_ALPHA = (2 * DEPTH) ** 0.25
VMEM_LIMIT = 56 * 1024 * 1024
HALF = D_MODEL // 2
SC_SEG = 256
SC_WINDOW = 128
N_SEG = HALF // SC_SEG
U32 = jnp.uint32


def _dot(a, b):
    return jnp.dot(a, b, preferred_element_type=F32)


def _dot_nt(a, b):
    return lax.dot_general(a, b, (((1,), (1,)), ((), ())), preferred_element_type=F32)


def _dot_tn(a, b):
    return lax.dot_general(a, b, (((0,), (0,)), ((), ())), preferred_element_type=F32)


def _layer_norm(x, g, b):
    mu = jnp.mean(x, axis=-1, keepdims=True)
    xc = x - mu
    var = jnp.mean(xc * xc, axis=-1, keepdims=True)
    return xc * lax.rsqrt(var + LN_EPS) * g + b


def _sigmoid(x):
    return 1.0 / (1.0 + jnp.exp(-x))


def _split_bf16(x):
    hi = x.astype(BF16)
    lo = (x - hi.astype(F32)).astype(BF16)
    return hi, lo


def _pack_segments(x):
    bits = pltpu.bitcast(x.astype(BF16).astype(F32), U32)
    word = (bits[:, :HALF] >> 16) | bits[:, HALF:]
    return [word[:, j * SC_SEG:(j + 1) * SC_SEG] for j in range(N_SEG)]


def _unpack_segments(segs):
    word = jnp.concatenate(segs, axis=1)
    lo = pltpu.bitcast(word << 16, F32)
    hi = pltpu.bitcast(word & jnp.uint32(0xFFFF0000), F32)
    return lo, hi


def _mixer_kernel(x_ref, cs_ref, gs_ref, w_in_ref, w_dw_ref, b_dw_ref, lncg_ref, lncb_ref,
                  wg_ref, bg_ref, gng_ref, w_o_ref, ln1g_ref, ln1b_ref, wr_ref, br_ref, *rest,
                  tm, chunk, n_alias):
    (x1p_ref, x1_ref, logit_ref, conv_out_ref, gla_out_ref,
     ubuf, qbuf, kbuf, labuf, vbuf, gbuf, mixbuf, st_ref) = rest[n_alias:]
    t = pl.program_id(1)
    nt = pl.num_programs(1)
    n_chunks = tm // chunk
    n_sub = chunk // SUB

    @pl.when(t == 0)
    def _init():
        ubuf[0:HALO - (CONV_W - 1), :] = jnp.zeros((HALO - (CONV_W - 1), CONV_CH), F32)
        ubuf[HALO - (CONV_W - 1):HALO, :] = cs_ref[0]
        for h in range(GLA_HEADS):
            st_ref[:, h * GLA_DK:(h + 1) * GLA_DK] = gs_ref[0, h].T

    x = x_ref[0]
    xb = x.astype(BF16)

    hv = _dot(xb, w_in_ref[:, OFF_UV:OFF_UG])
    hg = _dot(xb, w_in_ref[:, OFF_UG:OFF_Q])
    u = hv * _sigmoid(hg)
    ubuf[HALO:HALO + tm, :] = u
    lead = HALO - (CONV_W - 1)

    def proj_q():
        qbuf[...] = _dot(xb, w_in_ref[:, OFF_Q:OFF_K]) * (GLA_DK ** -0.5)

    def proj_k():
        kbuf[...] = _dot(xb, w_in_ref[:, OFF_K:OFF_V])

    def proj_v(half):
        lo = half * (GLA_V // 2)
        vbuf[:, lo:lo + GLA_V // 2] = _dot(xb, w_in_ref[:, OFF_V + lo:OFF_V + lo + GLA_V // 2])

    def proj_g(half):
        lo = half * (GLA_V // 2)
        gbuf[:, lo:lo + GLA_V // 2] = _dot(xb, w_in_ref[:, OFF_G + lo:OFF_G + lo + GLA_V // 2])

    def proj_gate():
        ha = _dot(xb, w_in_ref[:, OFF_A:IN_COLS_PAD])
        a = _dot(ha.astype(BF16), wg_ref[...]) + bg_ref[...]
        labuf[...] = ((jnp.minimum(a, 0.0) - jnp.log(1.0 + jnp.exp(-jnp.abs(a))))
                      * (1.0 / GATE_NORM))

    proj_jobs = [proj_q, proj_k, functools.partial(proj_v, 0), functools.partial(proj_v, 1),
                 functools.partial(proj_g, 0), functools.partial(proj_g, 1), proj_gate]

    cb = min(CONV_BLOCK, tm)

    def conv_block(r0):
        acc = jnp.zeros((cb, CONV_CH), F32) + b_dw_ref[...]
        for r in range(8):
            rows = cb if r == 0 else cb + 8
            part = None
            for s in range(r, lead + CONV_W, 8):
                if s < lead:
                    continue
                term = w_dw_ref[s - lead:s - lead + 1, :] * ubuf[pl.ds(r0 + (s - r), rows), :]
                part = term if part is None else part + term
            acc = acc + part[r:r + cb, :]
        cact = _layer_norm(acc, lncg_ref[...], lncb_ref[...])
        cact = cact * _sigmoid(cact)
        mixbuf[pl.ds(r0, cb), 0:CONV_CH] = cact.astype(BF16)

    per_trip = 4 if n_chunks % 4 == 0 else 1
    n_trips = n_chunks // per_trip
    loop_blocks = (tm // 2) // (cb * n_trips) if n_chunks >= 4 else 0
    pre_blocks = tm // cb - loop_blocks * n_trips
    for blk in range(max(pre_blocks, len(proj_jobs))):
        if blk < len(proj_jobs):
            proj_jobs[blk]()
        if blk < pre_blocks:
            conv_block(blk * cb)

    lane_k = lax.broadcasted_iota(jnp.int32, (1, GLA_K), 1) // GLA_DK
    row_c = lax.broadcasted_iota(jnp.int32, (chunk, 1), 0)
    tri = (lax.broadcasted_iota(jnp.int32, (chunk, chunk), 0)
           >= lax.broadcasted_iota(jnp.int32, (chunk, chunk), 1)).astype(BF16)
    e2 = (lax.broadcasted_iota(jnp.int32, (GLA_K, GLA_V), 0) // GLA_DK
          == lax.broadcasted_iota(jnp.int32, (GLA_K, GLA_V), 1) // GLA_DV).astype(BF16)
    row_s = lax.broadcasted_iota(jnp.int32, (GLA_HEADS * chunk, 1), 0)
    blk_s = (row_s % chunk) // SUB
    same_blk = blk_s == lax.broadcasted_iota(jnp.int32, (1, 128), 1) // SUB
    second_half = row_c % SUB >= DIAG
    row_g = lax.broadcasted_iota(jnp.int32, (1, DIAG, 1), 1)
    lane_v = lax.broadcasted_iota(jnp.int32, (1, GLA_V), 1) % GLA_DV

    def head_stack(m):
        return jnp.concatenate(
            [jnp.where(lane_k == h, m, 0.0) for h in range(GLA_HEADS)], axis=0).astype(BF16)

    def unstack(r, width):
        return jnp.concatenate(
            [r[h * chunk:(h + 1) * chunk, h * width:(h + 1) * width] for h in range(GLA_HEADS)],
            axis=1)

    def chunk_step(c, carry):
        base = pl.multiple_of(c * chunk, chunk)
        q_c = qbuf[pl.ds(base, chunk), :]
        k_c = kbuf[pl.ds(base, chunk), :]
        v_c = vbuf[pl.ds(base, chunk), :]
        v_b = v_c.astype(BF16)
        la_c = labuf[pl.ds(base, chunk), :]
        la_hi, la_lo = _split_bf16(la_c)
        b = _dot(tri, la_hi) + _dot(tri, la_lo)
        b_last = b[chunk - 1:chunk, :]
        st = st_ref[...]

        qe = q_c * jnp.exp(b)
        o2 = _dot_nt(head_stack(qe), st.astype(BF16))
        o = jnp.concatenate([o2[h * chunk:(h + 1) * chunk, :] for h in range(GLA_HEADS)], axis=1)

        a_off = None
        if n_sub > 1:
            r_rows = [b[0:SUB, :]] + [jnp.broadcast_to(b[SUB * i - 1:SUB * i, :], (SUB, GLA_K))
                                      for i in range(1, n_sub)]
            r_q = jnp.concatenate(r_rows, axis=0)
            q_t = jnp.where(row_c >= SUB, q_c * jnp.exp(jnp.minimum(b - r_q, 0.0)), 0.0)
            k_parts = []
            for i in range(1, n_sub):
                r_i = b[SUB * i - 1:SUB * i, :]
                k_t = jnp.where(row_c < SUB * i, k_c * jnp.exp(jnp.minimum(r_i - b, 0.0)), 0.0)
                k_parts.append(k_t)
                k_parts.append(jnp.zeros((128 - chunk, GLA_K), F32))
            k_cat = jnp.concatenate(k_parts, axis=0).astype(BF16)
            r = _dot_nt(head_stack(q_t), k_cat)
            a_off = r[:, 0:128]
            for i in range(2, n_sub):
                a_off = jnp.where(blk_s == i, r[:, (i - 1) * 128:i * 128], a_off)

        r_b = jnp.concatenate(
            [jnp.broadcast_to(b[SUB * i + DIAG - 1:SUB * i + DIAG, :], (SUB, GLA_K))
             for i in range(n_sub)], axis=0)
        q_b = jnp.where(second_half, q_c * jnp.exp(jnp.minimum(b - r_b, 0.0)), 0.0)
        k_b = jnp.where(second_half, 0.0, k_c * jnp.exp(jnp.minimum(r_b - b, 0.0)))
        k_b = jnp.concatenate([k_b, jnp.zeros((128 - chunk, GLA_K), F32)], axis=0).astype(BF16)
        a_half = jnp.where(same_blk, _dot_nt(head_stack(q_b), k_b), 0.0)
        a_off = a_half if a_off is None else a_off + a_half

        q3 = q_c.reshape(chunk // DIAG, DIAG, GLA_K)
        k3 = k_c.reshape(chunk // DIAG, DIAG, GLA_K)
        b3 = b.reshape(chunk // DIAG, DIAG, GLA_K)
        ps = []
        for d in range(DIAG):
            k_s = k3 if d == 0 else pltpu.roll(k3, d, axis=1)
            b_s = b3 if d == 0 else pltpu.roll(b3, d, axis=1)
            p = jnp.where(row_g >= d, q3 * k_s * jnp.exp(jnp.minimum(b3 - b_s, 0.0)), 0.0)
            ps.append(p.reshape(chunk, GLA_K).astype(BF16))
        w_all = _dot(jnp.concatenate(ps, axis=0), e2)
        a_d = jnp.zeros((chunk, GLA_V), F32)
        for d in range(DIAG):
            a_d = jnp.where(lane_v == row_c - d, w_all[d * chunk:(d + 1) * chunk, :], a_d)
        a_off = a_off + jnp.concatenate(
            [a_d[:, h * GLA_DV:(h + 1) * GLA_DV] for h in range(GLA_HEADS)], axis=0)
        o1 = _dot(a_off[:, 0:chunk].astype(BF16), v_b)
        o = o + unstack(o1, GLA_DV)

        ke = k_c * jnp.exp(b_last - b)
        r2 = _dot_tn(v_b, ke.astype(BF16))
        upd = jnp.zeros((GLA_DV, GLA_K), F32)
        for h in range(GLA_HEADS):
            upd = upd + jnp.where(lane_k == h, r2[h * GLA_DV:(h + 1) * GLA_DV, :], 0.0)
        st_ref[...] = st * jnp.exp(b_last) + upd

        g_c = gbuf[pl.ds(base, chunk), :]
        outs = []
        for h in range(GLA_HEADS):
            oh = o[:, h * GLA_DV:(h + 1) * GLA_DV]
            ms = jnp.mean(oh * oh, axis=-1, keepdims=True)
            outs.append(oh * lax.rsqrt(ms + LN_EPS))
        on = jnp.concatenate(outs, axis=1) * gng_ref[...]
        on = on * (g_c * _sigmoid(g_c))
        mixbuf[pl.ds(base, chunk), CONV_CH:MIX_W] = on.astype(BF16)
        return carry

    def trip(i, carry):
        for j in range(max(per_trip, loop_blocks)):
            if j < loop_blocks:
                conv_block(pl.multiple_of((pre_blocks + i * loop_blocks + j) * cb, cb))
            if j < per_trip:
                carry = chunk_step(i * per_trip + j, carry)
        return carry

    lax.fori_loop(0, n_trips, trip, 0)

    tail = ubuf[tm:tm + HALO, :]
    ubuf[0:HALO, :] = tail

    @pl.when(t == nt - 1)
    def _conv_out():
        conv_out_ref[0] = tail[HALO - (CONV_W - 1):, :]

    @pl.when(t == nt - 1)
    def _gla_out():
        st = st_ref[...]
        for h in range(GLA_HEADS):
            gla_out_ref[0, h] = st[:, h * GLA_DK:(h + 1) * GLA_DK].T

    y = _dot(mixbuf[...], w_o_ref[...])
    x1 = _layer_norm(DN_ALPHA * x + y, ln1g_ref[...], ln1b_ref[...])
    x1_ref[...] = x1
    for j, seg in enumerate(_pack_segments(x1)):
        x1p_ref[j] = seg
    x_hi, x_lo = _split_bf16(x1)
    lg = _dot(x_hi, wr_ref[...])
    logit_ref[...] = lg[:, 0:128] + lg[:, 128:256] + _dot(x_lo, wr_ref[:, 0:128]) + br_ref[...]


def _mixer(x, conv_state, gla_state, p, *, tm, n_tok_total, tok_offset, token_bufs=None):
    bsz, seq, _ = x.shape
    chunk = min(CHUNK, seq)
    assert seq % tm == 0 and tm % chunk == 0 and chunk % SUB == 0 and tok_offset % tm == 0
    nt = seq // tm
    blk0 = tok_offset // tm
    full = lambda shape: pl.BlockSpec(shape, lambda b, t: (0,) * len(shape))
    alias = () if token_bufs is None else tuple(token_bufs)
    kern = functools.partial(_mixer_kernel, tm=tm, chunk=chunk, n_alias=len(alias))
    n_in = 16
    return pl.pallas_call(
        kern,
        grid=(bsz, nt),
        input_output_aliases={n_in + i: i for i in range(len(alias))},
        in_specs=[
            pl.BlockSpec((1, tm, D_MODEL), lambda b, t: (b, t, 0)),
            pl.BlockSpec((1, CONV_W - 1, CONV_CH), lambda b, t: (b, 0, 0)),
            pl.BlockSpec((1, GLA_HEADS, GLA_DK, GLA_DV), lambda b, t: (b, 0, 0, 0)),
            full((D_MODEL, IN_COLS_PAD)),
            full((CONV_W, CONV_CH)),
            full((1, CONV_CH)),
            full((1, CONV_CH)),
            full((1, CONV_CH)),
            full((GATE_PAD, GLA_K)),
            full((1, GLA_K)),
            full((1, GLA_V)),
            full((MIX_W, D_MODEL)),
            full((1, D_MODEL)),
            full((1, D_MODEL)),
            full((D_MODEL, 256)),
            full((1, 128)),
        ] + [pl.BlockSpec(memory_space=pl.ANY)] * len(alias),
        out_specs=[
            pl.BlockSpec((N_SEG, tm, SC_SEG), lambda b, t: (0, blk0 + b * nt + t, 0)),
            pl.BlockSpec((tm, D_MODEL), lambda b, t: (blk0 + b * nt + t, 0)),
            pl.BlockSpec((tm, 128), lambda b, t: (blk0 + b * nt + t, 0)),
            pl.BlockSpec((1, CONV_W - 1, CONV_CH), lambda b, t: (b, 0, 0)),
            pl.BlockSpec((1, GLA_HEADS, GLA_DK, GLA_DV), lambda b, t: (b, 0, 0, 0)),
        ],
        out_shape=[
            jax.ShapeDtypeStruct((N_SEG, n_tok_total, SC_SEG), U32),
            jax.ShapeDtypeStruct((n_tok_total, D_MODEL), F32),
            jax.ShapeDtypeStruct((n_tok_total, 128), F32),
            jax.ShapeDtypeStruct((bsz, CONV_W - 1, CONV_CH), F32),
            jax.ShapeDtypeStruct((bsz, GLA_HEADS, GLA_DK, GLA_DV), F32),
        ],
        scratch_shapes=[
            pltpu.VMEM((HALO + tm, CONV_CH), F32),
            pltpu.VMEM((tm, GLA_K), F32),
            pltpu.VMEM((tm, GLA_K), F32),
            pltpu.VMEM((tm, GLA_K), F32),
            pltpu.VMEM((tm, GLA_V), F32),
            pltpu.VMEM((tm, GLA_V), F32),
            pltpu.VMEM((tm, MIX_W), BF16),
            pltpu.VMEM((GLA_DV, GLA_K), F32),
        ],
        compiler_params=pltpu.CompilerParams(
            dimension_semantics=("arbitrary", "arbitrary"),
            vmem_limit_bytes=VMEM_LIMIT),
        name="mixer",
    )(x, conv_state, gla_state, p["w_in"], p["w_dw"], p["b_dw"], p["ln_conv_g"], p["ln_conv_b"],
      p["w_gate"], p["b_gate"], p["gla_norm_g"], p["w_o"], p["ln1_g"], p["ln1_b"],
      p["w_router"], p["b_router"], *alias)


def _route_kernel(lg_ref, ri_ref, rg_ref, cnt_ref, carry_ref, *, tr):
    i = pl.program_id(0)

    @pl.when(i == 0)
    def _init():
        carry_ref[...] = jnp.zeros((N_EXPERTS, 128), F32)

    l = lg_ref[...].T[0:N_EXPERTS, :]
    row = lax.broadcasted_iota(jnp.int32, (N_EXPERTS, tr), 0)
    hots, vals, idxs = [], [], []
    for _ in range(TOP_K):
        m = jnp.max(l, axis=0, keepdims=True)
        ik = jnp.min(jnp.where(l == m, row, N_EXPERTS), axis=0, keepdims=True)
        hot = row == ik
        hots.append(hot)
        vals.append(m)
        idxs.append(ik)
        l = jnp.where(hot, -jnp.inf, l)
    es = [jnp.exp(v - vals[0]) for v in vals]
    denom = es[0] + es[1] + es[2] + es[3]
    member = jnp.zeros((N_EXPERTS, tr), F32)
    for hot in hots:
        member = member + hot.astype(F32)
    before = (lax.broadcasted_iota(jnp.int32, (tr, tr), 0)
              < lax.broadcasted_iota(jnp.int32, (tr, tr), 1)).astype(BF16)
    cum = _dot(member.astype(BF16), before) + carry_ref[:, 0:1]
    ranks = [jnp.sum(jnp.where(hot, cum, 0.0), axis=0, keepdims=True).astype(jnp.int32)
             for hot in hots]
    ri_ref[...] = jnp.concatenate(idxs + ranks, axis=0)
    gates = jnp.concatenate([e / denom for e in es] + [jnp.zeros((128 - TOP_K, tr), F32)], axis=0)
    rg_ref[...] = gates.T
    carry_ref[...] = carry_ref[...] + jnp.sum(member, axis=1, keepdims=True)

    @pl.when(i == pl.num_programs(0) - 1)
    def _fin():
        cnt_ref[...] = carry_ref[...]


def _route(logits, *, tr):
    n_tok = logits.shape[0]
    assert n_tok % tr == 0
    return pl.pallas_call(
        functools.partial(_route_kernel, tr=tr),
        grid=(n_tok // tr,),
        in_specs=[pl.BlockSpec((tr, 128), lambda i: (i, 0))],
        out_specs=[pl.BlockSpec((2 * TOP_K, tr), lambda i: (0, i)),
                   pl.BlockSpec((tr, 128), lambda i: (i, 0)),
                   pl.BlockSpec((N_EXPERTS, 128), lambda i: (0, 0))],
        out_shape=[jax.ShapeDtypeStruct((2 * TOP_K, n_tok), jnp.int32),
                   jax.ShapeDtypeStruct((n_tok, 128), F32),
                   jax.ShapeDtypeStruct((N_EXPERTS, 128), F32)],
        scratch_shapes=[pltpu.VMEM((N_EXPERTS, 128), F32)],
        compiler_params=pltpu.CompilerParams(dimension_semantics=("arbitrary",)),
        name="route",
    )(logits)


def _ffn_kernel(gs_ref, nu_ref, xs_hbm, wgu_ref, bgu_ref, wdn_ref, bdn_ref, yb_hbm,
                wgu_bf, wdn_bf, xin, yout, in_sem, out_sem):
    e = pl.program_id(0)
    n_used = nu_ref[0]
    g_first = gs_ref[e]
    g_end = gs_ref[e + 1]

    def in_copy(g, slot):
        rows = pl.ds(pl.multiple_of(g * EXPERT_BLOCK, EXPERT_BLOCK), EXPERT_BLOCK)
        return pltpu.make_async_copy(xs_hbm.at[:, rows, :], xin.at[slot], in_sem.at[slot])

    def out_copy(g, slot):
        rows = pl.ds(pl.multiple_of(g * EXPERT_BLOCK, EXPERT_BLOCK), EXPERT_BLOCK)
        return pltpu.make_async_copy(yout.at[slot], yb_hbm.at[:, rows, :], out_sem.at[slot])

    @pl.when(e == 0)
    def _prime():
        for g in range(IN_AHEAD):
            @pl.when(g < n_used)
            def _():
                in_copy(g, g).start()

    @pl.when(g_end > g_first)
    def _cast():
        wgu_bf[...] = wgu_ref[0].astype(BF16)
        wdn_bf[...] = wdn_ref[0].astype(BF16)

    def blocks(g0, n):
        for j in range(n):
            g = g0 + j
            in_copy(g, g % IN_RING).wait()
        for j in range(n):
            g = g0 + IN_AHEAD + j

            @pl.when(g < n_used)
            def _prefetch():
                in_copy(g, g % IN_RING).start()
        for j in range(n):
            g = g0 + j

            @pl.when(g >= OUT_RING)
            def _free_out():
                out_copy(g - OUT_RING, g % OUT_RING).wait()
        for j in range(n):
            g = g0 + j
            x_lo, x_hi = _unpack_segments([xin[g % IN_RING, s] for s in range(N_SEG)])
            hgu = (_dot(x_lo.astype(BF16), wgu_bf[0:HALF, :])
                   + _dot(x_hi.astype(BF16), wgu_bf[HALF:, :]) + bgu_ref[0])
            gate = jnp.minimum(hgu[:, :D_FF], SWIGLU_LIMIT)
            up = jnp.clip(hgu[:, D_FF:], -SWIGLU_LIMIT, SWIGLU_LIMIT)
            act = (up + 1.0) * gate * _sigmoid(SWIGLU_ALPHA * gate)
            yb = _dot(act.astype(BF16), wdn_bf[...]) + bdn_ref[0]
            for s, seg in enumerate(_pack_segments(yb)):
                yout[g % OUT_RING, s] = seg
        for j in range(n):
            g = g0 + j
            out_copy(g, g % OUT_RING).start()

    n_pairs = (g_end - g_first) // 2

    def pair(i, carry):
        blocks(g_first + 2 * i, 2)
        return carry

    lax.fori_loop(0, n_pairs, pair, 0)

    @pl.when((g_end - g_first) % 2 == 1)
    def _odd():
        blocks(g_end - 1, 1)

    @pl.when(e == pl.num_programs(0) - 1)
    def _drain():
        for back in range(OUT_RING, 0, -1):
            @pl.when(n_used >= back)
            def _():
                out_copy(n_used - back, (n_used - back) % OUT_RING).wait()


def _ffn(block_start, n_used, xs, w_gu, b_gu, w_down, b_down):
    n_rows = xs.shape[1]
    grid_spec = pltpu.PrefetchScalarGridSpec(
        num_scalar_prefetch=2,
        grid=(N_EXPERTS,),
        in_specs=[
            pl.BlockSpec(memory_space=pl.ANY),
            pl.BlockSpec((1, D_MODEL, 2 * D_FF), lambda e, gs, nu: (e, 0, 0)),
            pl.BlockSpec((1, 1, 2 * D_FF), lambda e, gs, nu: (e, 0, 0)),
            pl.BlockSpec((1, D_FF, D_MODEL), lambda e, gs, nu: (e, 0, 0)),
            pl.BlockSpec((1, 1, D_MODEL), lambda e, gs, nu: (e, 0, 0)),
        ],
        out_specs=pl.BlockSpec(memory_space=pl.ANY),
        scratch_shapes=[pltpu.VMEM((D_MODEL, 2 * D_FF), BF16),
                        pltpu.VMEM((D_FF, D_MODEL), BF16),
                        pltpu.VMEM((IN_RING, N_SEG, EXPERT_BLOCK, SC_SEG), U32),
                        pltpu.VMEM((OUT_RING, N_SEG, EXPERT_BLOCK, SC_SEG), U32),
                        pltpu.SemaphoreType.DMA((IN_RING,)),
                        pltpu.SemaphoreType.DMA((OUT_RING,))],
    )
    return pl.pallas_call(
        _ffn_kernel,
        grid_spec=grid_spec,
        out_shape=jax.ShapeDtypeStruct((N_SEG, n_rows, SC_SEG), U32),
        compiler_params=pltpu.CompilerParams(
            dimension_semantics=("arbitrary",), vmem_limit_bytes=VMEM_LIMIT),
        name="expert_ffn",
    )(block_start, n_used, xs, w_gu, b_gu.reshape(N_EXPERTS, 1, 2 * D_FF), w_down,
      b_down.reshape(N_EXPERTS, 1, D_MODEL))


def _combine_kernel(x1_ref, g_ref, rg_ref, ln2g_ref, ln2b_ref, *rest, blk0, n_prompt_blocks,
                    n_alias, with_sample):
    yp_ref = rest[n_alias]
    i = blk0 + pl.program_id(0)
    m_lo = jnp.zeros((x1_ref.shape[0], HALF), F32)
    m_hi = jnp.zeros((x1_ref.shape[0], HALF), F32)
    for k in range(TOP_K):
        g_lo, g_hi = _unpack_segments([g_ref[k, j] for j in range(N_SEG)])
        gate = rg_ref[:, k:k + 1]
        m_lo = m_lo + gate * g_lo
        m_hi = m_hi + gate * g_hi
    z = DN_ALPHA * x1_ref[...] + jnp.concatenate([m_lo, m_hi], axis=1)
    y = _layer_norm(z, ln2g_ref[...], ln2b_ref[...])

    if with_sample:
        ys_ref = rest[n_alias + 1]

        @pl.when(i < n_prompt_blocks)
        def _p():
            yp_ref[...] = y

        @pl.when(i >= n_prompt_blocks)
        def _s():
            ys_ref[...] = y
    else:
        yp_ref[...] = y


def _combine(x1, g, rg, ln2_g, ln2_b, *, tok0, n_prompt, tf, yp_buf=None):
    n_tok = x1.shape[0]
    piece = g.shape[2]
    n_sample = n_tok - n_prompt
    assert n_prompt % tf == 0 and n_sample % tf == 0 and tok0 % tf == 0 and piece % tf == 0
    npb = n_prompt // tf
    blk0 = tok0 // tf
    with_sample = tok0 + piece > n_prompt
    assert not with_sample or tok0 + piece == n_tok
    alias = () if yp_buf is None else (yp_buf,)
    out_specs = [pl.BlockSpec((tf, D_MODEL), lambda i: (jnp.minimum(blk0 + i, npb - 1), 0))]
    out_shape = [jax.ShapeDtypeStruct((n_prompt, D_MODEL), F32)]
    if with_sample:
        out_specs.append(pl.BlockSpec((tf, D_MODEL), lambda i: (jnp.maximum(blk0 + i - npb, 0), 0)))
        out_shape.append(jax.ShapeDtypeStruct((n_sample, D_MODEL), F32))
    n_in = 5
    return pl.pallas_call(
        functools.partial(_combine_kernel, blk0=blk0, n_prompt_blocks=npb, n_alias=len(alias),
                          with_sample=with_sample),
        grid=(piece // tf,),
        input_output_aliases={n_in + i: i for i in range(len(alias))},
        in_specs=[pl.BlockSpec((tf, D_MODEL), lambda i: (blk0 + i, 0)),
                  pl.BlockSpec((TOP_K, N_SEG, tf, SC_SEG), lambda i: (0, 0, i, 0)),
                  pl.BlockSpec((tf, 128), lambda i: (blk0 + i, 0)),
                  pl.BlockSpec((1, D_MODEL), lambda i: (0, 0)),
                  pl.BlockSpec((1, D_MODEL), lambda i: (0, 0))]
        + [pl.BlockSpec(memory_space=pl.ANY)] * len(alias),
        out_specs=out_specs,
        out_shape=out_shape,
        compiler_params=pltpu.CompilerParams(
            dimension_semantics=("arbitrary",), vmem_limit_bytes=VMEM_LIMIT),
        name="combine",
    )(x1, g, rg, ln2_g, ln2_b, *alias)


def _sc_mesh():
    return plsc.VectorSubcoreMesh(core_axis_name="c", subcore_axis_name="s")


def _sc_scatter_rows(src, dest, n_out):
    n_src, width = src.shape
    n_k = dest.shape[0]
    assert width == SC_SEG and n_src % SC_WINDOW == 0 and dest.shape[1] == n_src

    @pl.kernel(out_type=jax.ShapeDtypeStruct((n_out, width), src.dtype), mesh=_sc_mesh(),
               scratch_types=[])
    def scatter_kernel(src_hbm, idx_hbm, out_hbm):
        def body(src_vmem, idx_vmem):
            for k in range(n_k):
                pltpu.sync_copy(src_vmem, out_hbm.at[idx_vmem.at[k]])

        pltpu.emit_pipeline(
            body, grid=(n_src // SC_WINDOW,),
            in_specs=[pl.BlockSpec((SC_WINDOW, width), lambda i: (i, 0)),
                      pl.BlockSpec((n_k, SC_WINDOW), lambda i: (0, i))],
            out_specs=[],
            core_axis_name=("c", "s"),
            dimension_semantics=(pltpu.PARALLEL,),
        )(src_hbm, idx_hbm)

    return scatter_kernel(src, dest)


def _sc_gather_rows(src, idx):
    n_out = idx.shape[1]
    width = src.shape[1]
    assert width == SC_SEG and n_out % SC_WINDOW == 0

    @pl.kernel(out_type=jax.ShapeDtypeStruct((n_out, width), src.dtype), mesh=_sc_mesh(),
               scratch_types=[])
    def gather_kernel(src_hbm, idx_hbm, out_hbm):
        def body(idx_vmem, out_vmem):
            pltpu.sync_copy(src_hbm.at[idx_vmem.at[0]], out_vmem)

        pltpu.emit_pipeline(
            body, grid=(n_out // SC_WINDOW,),
            in_specs=[pl.BlockSpec((1, SC_WINDOW), lambda i: (0, i))],
            out_specs=[pl.BlockSpec((SC_WINDOW, width), lambda i: (i, 0))],
            core_axis_name=("c", "s"),
            dimension_semantics=(pltpu.PARALLEL,),
        )(idx_hbm, out_hbm)

    return gather_kernel(src, idx)


def _prep_params(w_in, w_dw, b_dw, ln_conv_g, ln_conv_b, w_gate_lr, b_gate, gla_norm_g, w_o,
                 ln1_g, ln1_b, w_router, b_router):
    row = lambda v: v.reshape(1, -1).astype(F32)
    w_in_p = jnp.pad(w_in, ((0, 0), (0, IN_COLS_PAD - w_in.shape[1]))).astype(BF16)
    w_gate = jnp.pad(w_gate_lr, ((0, GATE_PAD - GATE_RANK), (0, 0))).astype(BF16)
    wr_hi = w_router.astype(BF16)
    wr_lo = (w_router - wr_hi.astype(F32)).astype(BF16)
    padr = lambda m: jnp.pad(m, ((0, 0), (0, 128 - N_EXPERTS)))
    return {
        "w_in": w_in_p, "w_dw": w_dw, "b_dw": row(b_dw), "ln_conv_g": row(ln_conv_g),
        "ln_conv_b": row(ln_conv_b), "w_gate": w_gate, "b_gate": row(b_gate),
        "gla_norm_g": row(gla_norm_g), "w_o": w_o.astype(BF16), "ln1_g": row(ln1_g),
        "ln1_b": row(ln1_b), "w_router": jnp.concatenate([padr(wr_hi), padr(wr_lo)], axis=1),
        "b_router": jnp.pad(row(b_router), ((0, 0), (0, 128 - N_EXPERTS))),
    }


def _moe_layout(ri, counts):
    n_tok = ri.shape[1]
    idx = ri[0:TOP_K, :]
    rank = ri[TOP_K:2 * TOP_K, :]
    cnt = counts[:, 0].astype(jnp.int32)
    padded = (cnt + EXPERT_BLOCK - 1) // EXPERT_BLOCK * EXPERT_BLOCK
    pad_end = jnp.cumsum(padded)
    pad_start = pad_end - padded
    nb = -(-(n_tok * TOP_K) // EXPERT_BLOCK) + N_EXPERTS
    dest = rank
    for e in range(N_EXPERTS):
        dest = dest + jnp.where(idx == e, pad_start[e], 0)
    block_start = (jnp.concatenate([pad_start, pad_end[-1:]]) // EXPERT_BLOCK).astype(jnp.int32)
    n_used = block_start[-1:]
    return dest, block_start, n_used, nb


def kernel(x_prompt, x_sample, state_conv, state_gla, w_in, w_dw, b_dw, ln_conv_g, ln_conv_b,
           w_gate_lr, b_gate, gla_norm_g, w_o, ln1_g, ln1_b, w_router, b_router, w_gu, b_gu,
           w_down, b_down, ln2_g, ln2_b):
    assert w_in.shape[0] == DEPTH
    l = 0
    p = _prep_params(w_in[l], w_dw[l], b_dw[l], ln_conv_g[l], ln_conv_b[l], w_gate_lr[l],
                     b_gate[l], gla_norm_g[l], w_o[l], ln1_g[l], ln1_b[l], w_router[l],
                     b_router[l])
    bp, tp, _ = x_prompt.shape
    bs, ts, _ = x_sample.shape
    zc = jnp.zeros((bp, CONV_W - 1, CONV_CH), F32)
    zs = jnp.zeros((bp, GLA_HEADS, GLA_DK, GLA_DV), F32)
    n_prompt = bp * tp
    n_tok = n_prompt + bs * ts
    x1p, x1, lg, conv_p, gla_p = _mixer(x_prompt, zc, zs, p, tm=512, n_tok_total=n_tok,
                                        tok_offset=0)
    x1p, x1, lg, conv_s, gla_s = _mixer(x_sample, state_conv[l], state_gla[l], p, tm=ts,
                                        n_tok_total=n_tok, tok_offset=n_prompt,
                                        token_bufs=(x1p, x1, lg))

    ri, rg, counts = _route(lg, tr=768 if n_tok % 768 == 0 else 256)
    dest, block_start, n_used, nb = _moe_layout(ri, counts)
    n_slot = nb * EXPERT_BLOCK
    seg0 = jnp.arange(N_SEG, dtype=jnp.int32) * n_slot
    dest_seg = (dest[:, None, :] + seg0[None, :, None]).reshape(TOP_K, N_SEG * n_tok)
    xs = _sc_scatter_rows(x1p.reshape(N_SEG * n_tok, SC_SEG), dest_seg, N_SEG * n_slot)
    yb = _ffn(block_start, n_used, xs.reshape(N_SEG, n_slot, SC_SEG), w_gu[l], b_gu[l], w_down[l],
              b_down[l])
    tf = 256
    piece = -(-n_prompt // (COMBINE_PIECES * tf)) * tf
    bounds = [min(p * piece, n_prompt) for p in range(COMBINE_PIECES)] + [n_tok]
    bounds = sorted(set(bounds))
    dest_kjt = dest_seg.reshape(TOP_K, N_SEG, n_tok)
    yp = None
    for t0, t1 in zip(bounds[:-1], bounds[1:]):
        g = _sc_gather_rows(yb.reshape(N_SEG * n_slot, SC_SEG),
                            dest_kjt[:, :, t0:t1].reshape(1, -1))
        outs = _combine(x1, g.reshape(TOP_K, N_SEG, t1 - t0, SC_SEG), rg,
                        ln2_g[l].reshape(1, -1), ln2_b[l].reshape(1, -1), tok0=t0,
                        n_prompt=n_prompt, tf=tf, yp_buf=yp)
        yp = outs[0]
    ys = outs[1]
    return (yp.reshape(bp, tp, D_MODEL), ys.reshape(bs, ts, D_MODEL),
            conv_p[None], gla_p[None], conv_s[None], gla_s[None])
```

```python
import functools

import jax
import jax.numpy as jnp
from jax import lax
from jax.experimental import pallas as pl
from jax.experimental.pallas import tpu as pltpu
from jax.experimental.pallas import tpu_sc as plsc

F32 = jnp.float32
BF16 = jnp.bfloat16

D_MODEL = 1024
CHUNK = 64
SUB = 16
DIAG = 8
CONV_CH = 512
CONV_W = 31
HALO = 32
CONV_BLOCK = 64
COMBINE_PIECES = 4
IN_AHEAD = 2
IN_RING = IN_AHEAD + 2
OUT_RING = 4
GLA_HEADS = 4
GLA_DK = 64
GLA_DV = 128
GLA_K = GLA_HEADS * GLA_DK
GLA_V = GLA_HEADS * GLA_DV
GATE_RANK = 16
GATE_PAD = 128
GATE_NORM = 16.0
MIX_W = CONV_CH + GLA_V
OFF_UV = 0
OFF_UG = OFF_UV + CONV_CH
OFF_Q = OFF_UG + CONV_CH
OFF_K = OFF_Q + GLA_K
OFF_V = OFF_K + GLA_K
OFF_G = OFF_V + GLA_V
OFF_A = OFF_G + GLA_V
IN_COLS_PAD = OFF_A + GATE_PAD
N_EXPERTS = 32
TOP_K = 4
D_FF = 1024
SWIGLU_LIMIT = 7.0
SWIGLU_ALPHA = 1.702
EXPERT_BLOCK = 256
LN_EPS = 1e-5
DEPTH = 1
DN_ALPHA = (2 * DEPTH) ** 0.25
VMEM_LIMIT = 56 * 1024 * 1024
HALF = D_MODEL // 2
SC_SEG = 256
SC_WINDOW = 128
N_SEG = HALF // SC_SEG
U32 = jnp.uint32


def _dot(a, b):
    return jnp.dot(a, b, preferred_element_type=F32)


def _dot_nt(a, b):
    return lax.dot_general(a, b, (((1,), (1,)), ((), ())), preferred_element_type=F32)


def _dot_tn(a, b):
    return lax.dot_general(a, b, (((0,), (0,)), ((), ())), preferred_element_type=F32)


def _layer_norm(x, g, b):
    mu = jnp.mean(x, axis=-1, keepdims=True)
    xc = x - mu
    var = jnp.mean(xc * xc, axis=-1, keepdims=True)
    return xc * lax.rsqrt(var + LN_EPS) * g + b


def _sigmoid(x):
    return 1.0 / (1.0 + jnp.exp(-x))


def _split_bf16(x):
    hi = x.astype(BF16)
    lo = (x - hi.astype(F32)).astype(BF16)
    return hi, lo


def _pack_segments(x):
    bits = pltpu.bitcast(x.astype(BF16).astype(F32), U32)
    word = (bits[:, :HALF] >> 16) | bits[:, HALF:]
    return [word[:, j * SC_SEG:(j + 1) * SC_SEG] for j in range(N_SEG)]


def _unpack_segments(segs):
    word = jnp.concatenate(segs, axis=1)
    lo = pltpu.bitcast(word << 16, F32)
    hi = pltpu.bitcast(word & jnp.uint32(0xFFFF0000), F32)
    return lo, hi


def _mixer_kernel(x_ref, cs_ref, gs_ref, w_in_ref, w_dw_ref, b_dw_ref, lncg_ref, lncb_ref,
                  wg_ref, bg_ref, gng_ref, w_o_ref, ln1g_ref, ln1b_ref, wr_ref, br_ref, *rest,
                  tm, chunk, n_alias):
    (x1p_ref, x1_ref, logit_ref, conv_out_ref, gla_out_ref,
     ubuf, qbuf, kbuf, labuf, vbuf, gbuf, mixbuf, st_ref) = rest[n_alias:]
    t = pl.program_id(1)
    nt = pl.num_programs(1)
    n_chunks = tm // chunk
    n_sub = chunk // SUB

    @pl.when(t == 0)
    def _init():
        ubuf[0:HALO - (CONV_W - 1), :] = jnp.zeros((HALO - (CONV_W - 1), CONV_CH), F32)
        ubuf[HALO - (CONV_W - 1):HALO, :] = cs_ref[0]
        for h in range(GLA_HEADS):
            st_ref[:, h * GLA_DK:(h + 1) * GLA_DK] = gs_ref[0, h].T

    x = x_ref[0]
    xb = x.astype(BF16)

    hv = _dot(xb, w_in_ref[:, OFF_UV:OFF_UG])
    hg = _dot(xb, w_in_ref[:, OFF_UG:OFF_Q])
    u = hv * _sigmoid(hg)
    ubuf[HALO:HALO + tm, :] = u
    lead = HALO - (CONV_W - 1)

    def proj_q(r0, n):
        qbuf[r0:r0 + n, :] = _dot(xb[r0:r0 + n], w_in_ref[:, OFF_Q:OFF_K]) * (GLA_DK ** -0.5)

    def proj_k(r0, n):
        kbuf[r0:r0 + n, :] = _dot(xb[r0:r0 + n], w_in_ref[:, OFF_K:OFF_V])

    def proj_v(half, r0, n):
        lo = half * (GLA_V // 2)
        vbuf[r0:r0 + n, lo:lo + GLA_V // 2] = _dot(
            xb[r0:r0 + n], w_in_ref[:, OFF_V + lo:OFF_V + lo + GLA_V // 2])

    def proj_g(half, r0, n):
        lo = half * (GLA_V // 2)
        gbuf[r0:r0 + n, lo:lo + GLA_V // 2] = _dot(
            xb[r0:r0 + n], w_in_ref[:, OFF_G + lo:OFF_G + lo + GLA_V // 2])

    def proj_gate(r0, n):
        ha = _dot(xb[r0:r0 + n], w_in_ref[:, OFF_A:IN_COLS_PAD])
        a = _dot(ha.astype(BF16), wg_ref[...]) + bg_ref[...]
        labuf[r0:r0 + n, :] = ((jnp.minimum(a, 0.0) - jnp.log(1.0 + jnp.exp(-jnp.abs(a))))
                               * (1.0 / GATE_NORM))

    proj_jobs = [proj_q, proj_k, functools.partial(proj_v, 0), functools.partial(proj_v, 1),
                 functools.partial(proj_g, 0), functools.partial(proj_g, 1), proj_gate]

    cb = min(CONV_BLOCK, tm)

    def conv_block(r0):
        acc = jnp.zeros((cb, CONV_CH), F32) + b_dw_ref[...]
        for r in range(8):
            rows = cb if r == 0 else cb + 8
            part = None
            for s in range(r, lead + CONV_W, 8):
                if s < lead:
                    continue
                w8 = w_dw_ref[8 * (s - lead):8 * (s - lead) + 8, :]
                u3 = ubuf[pl.ds(r0 + (s - r), rows), :].reshape(rows // 8, 8, CONV_CH)
                term = u3 * w8
                part = term if part is None else part + term
            acc = acc + part.reshape(rows, CONV_CH)[r:r + cb, :]
        cact = _layer_norm(acc, lncg_ref[...], lncb_ref[...])
        cact = cact * _sigmoid(cact)
        mixbuf[pl.ds(r0, cb), 0:CONV_CH] = cact.astype(BF16)

    per_trip = 4 if n_chunks % 4 == 0 else 1
    n_trips = n_chunks // per_trip
    loop_blocks = (tm // 2) // (cb * n_trips) if n_chunks >= 4 else 0
    pre_blocks = tm // cb - loop_blocks * n_trips
    static_trips = n_trips == 2
    rows_first = tm // 2 if static_trips else tm
    for blk in range(max(pre_blocks, len(proj_jobs))):
        if blk < len(proj_jobs):
            proj_jobs[blk](0, rows_first)
        if blk < pre_blocks:
            conv_block(blk * cb)

    lane_k = lax.broadcasted_iota(jnp.int32, (1, GLA_K), 1) // GLA_DK
    row_c = lax.broadcasted_iota(jnp.int32, (chunk, 1), 0)
    tri = (lax.broadcasted_iota(jnp.int32, (chunk, chunk), 0)
           >= lax.broadcasted_iota(jnp.int32, (chunk, chunk), 1)).astype(BF16)
    e2 = (lax.broadcasted_iota(jnp.int32, (GLA_K, GLA_V), 0) // GLA_DK
          == lax.broadcasted_iota(jnp.int32, (GLA_K, GLA_V), 1) // GLA_DV).astype(BF16)
    row_s = lax.broadcasted_iota(jnp.int32, (GLA_HEADS * chunk, 1), 0)
    blk_s = (row_s % chunk) // SUB
    same_blk = blk_s == lax.broadcasted_iota(jnp.int32, (1, 128), 1) // SUB
    second_half = row_c % SUB >= DIAG
    row_g = lax.broadcasted_iota(jnp.int32, (1, DIAG, 1), 1)
    lane_v = lax.broadcasted_iota(jnp.int32, (1, GLA_V), 1) % GLA_DV

    def head_stack(m):
        return jnp.concatenate(
            [jnp.where(lane_k == h, m, 0.0) for h in range(GLA_HEADS)], axis=0).astype(BF16)

    def unstack(r, width):
        return jnp.concatenate(
            [r[h * chunk:(h + 1) * chunk, h * width:(h + 1) * width] for h in range(GLA_HEADS)],
            axis=1)

    def chunk_step(c, carry):
        base = c * chunk if isinstance(c, int) else pl.multiple_of(c * chunk, chunk)
        q_c = qbuf[pl.ds(base, chunk), :]
        k_c = kbuf[pl.ds(base, chunk), :]
        v_c = vbuf[pl.ds(base, chunk), :]
        v_b = v_c.astype(BF16)
        la_c = labuf[pl.ds(base, chunk), :]
        la_hi, la_lo = _split_bf16(la_c)
        b = _dot(tri, la_hi) + _dot(tri, la_lo)
        b_last = b[chunk - 1:chunk, :]
        st = st_ref[...]

        qe = q_c * jnp.exp(b)
        o2 = _dot_nt(head_stack(qe), st.astype(BF16))
        o = jnp.concatenate([o2[h * chunk:(h + 1) * chunk, :] for h in range(GLA_HEADS)], axis=1)

        a_off = None
        if n_sub > 1:
            r_rows = [b[0:SUB, :]] + [jnp.broadcast_to(b[SUB * i - 1:SUB * i, :], (SUB, GLA_K))
                                      for i in range(1, n_sub)]
            r_q = jnp.concatenate(r_rows, axis=0)
            q_t = jnp.where(row_c >= SUB, q_c * jnp.exp(jnp.minimum(b - r_q, 0.0)), 0.0)
            k_parts = []
            for i in range(1, n_sub):
                r_i = b[SUB * i - 1:SUB * i, :]
                k_t = jnp.where(row_c < SUB * i, k_c * jnp.exp(jnp.minimum(r_i - b, 0.0)), 0.0)
                k_parts.append(k_t)
                k_parts.append(jnp.zeros((128 - chunk, GLA_K), F32))
            k_cat = jnp.concatenate(k_parts, axis=0).astype(BF16)
            r = _dot_nt(head_stack(q_t), k_cat)
            a_off = r[:, 0:128]
            for i in range(2, n_sub):
                a_off = jnp.where(blk_s == i, r[:, (i - 1) * 128:i * 128], a_off)

        r_b = jnp.concatenate(
            [jnp.broadcast_to(b[SUB * i + DIAG - 1:SUB * i + DIAG, :], (SUB, GLA_K))
             for i in range(n_sub)], axis=0)
        q_b = jnp.where(second_half, q_c * jnp.exp(jnp.minimum(b - r_b, 0.0)), 0.0)
        k_b = jnp.where(second_half, 0.0, k_c * jnp.exp(jnp.minimum(r_b - b, 0.0)))
        k_b = jnp.concatenate([k_b, jnp.zeros((128 - chunk, GLA_K), F32)], axis=0).astype(BF16)
        a_half = jnp.where(same_blk, _dot_nt(head_stack(q_b), k_b), 0.0)
        a_off = a_half if a_off is None else a_off + a_half

        q3 = q_c.reshape(chunk // DIAG, DIAG, GLA_K)
        k3 = k_c.reshape(chunk // DIAG, DIAG, GLA_K)
        b3 = b.reshape(chunk // DIAG, DIAG, GLA_K)
        ps = []
        for d in range(DIAG):
            k_s = k3 if d == 0 else pltpu.roll(k3, d, axis=1)
            b_s = b3 if d == 0 else pltpu.roll(b3, d, axis=1)
            p = jnp.where(row_g >= d, q3 * k_s * jnp.exp(jnp.minimum(b3 - b_s, 0.0)), 0.0)
            ps.append(p.reshape(chunk, GLA_K).astype(BF16))
        w_all = _dot(jnp.concatenate(ps, axis=0), e2)
        a_d = jnp.zeros((chunk, GLA_V), F32)
        for d in range(DIAG):
            a_d = jnp.where(lane_v == row_c - d, w_all[d * chunk:(d + 1) * chunk, :], a_d)
        a_off = a_off + jnp.concatenate(
            [a_d[:, h * GLA_DV:(h + 1) * GLA_DV] for h in range(GLA_HEADS)], axis=0)
        o1 = _dot(a_off[:, 0:chunk].astype(BF16), v_b)
        o = o + unstack(o1, GLA_DV)

        ke = k_c * jnp.exp(b_last - b)
        r2 = _dot_tn(v_b, ke.astype(BF16))
        upd = jnp.zeros((GLA_DV, GLA_K), F32)
        for h in range(GLA_HEADS):
            upd = upd + jnp.where(lane_k == h, r2[h * GLA_DV:(h + 1) * GLA_DV, :], 0.0)
        st_ref[...] = st * jnp.exp(b_last) + upd

        g_c = gbuf[pl.ds(base, chunk), :]
        outs = []
        for h in range(GLA_HEADS):
            oh = o[:, h * GLA_DV:(h + 1) * GLA_DV]
            ms = jnp.mean(oh * oh, axis=-1, keepdims=True)
            outs.append(oh * lax.rsqrt(ms + LN_EPS))
        on = jnp.concatenate(outs, axis=1) * gng_ref[...]
        on = on * (g_c * _sigmoid(g_c))
        mixbuf[pl.ds(base, chunk), CONV_CH:MIX_W] = on.astype(BF16)
        return carry

    def trip(i, carry, extra_jobs=()):
        for j in range(max(per_trip, loop_blocks, len(extra_jobs))):
            if j < len(extra_jobs):
                extra_jobs[j](rows_first, tm - rows_first)
            if j < loop_blocks:
                r0 = (pre_blocks + i * loop_blocks + j) * cb
                conv_block(r0 if isinstance(r0, int) else pl.multiple_of(r0, cb))
            if j < per_trip:
                carry = chunk_step(i * per_trip + j, carry)
        return carry

    def merge_rows(r0, n):
        y = _dot(mixbuf[r0:r0 + n, :], w_o_ref[...])
        x1 = _layer_norm(DN_ALPHA * x_ref[0, r0:r0 + n, :] + y, ln1g_ref[...], ln1b_ref[...])
        x1_ref[r0:r0 + n, :] = x1
        for j, seg in enumerate(_pack_segments(x1)):
            x1p_ref[j, r0:r0 + n, :] = seg
        x_hi, x_lo = _split_bf16(x1)
        lg = _dot(x_hi, wr_ref[...])
        logit_ref[r0:r0 + n, :] = (lg[:, 0:128] + lg[:, 128:256] + _dot(x_lo, wr_ref[:, 0:128])
                                   + br_ref[...])

    if static_trips:
        trip(0, 0, tuple(proj_jobs))
        merge_rows(0, rows_first)
        trip(1, 0)
        merge_rows(rows_first, tm - rows_first)
    else:
        lax.fori_loop(0, n_trips, trip, 0)
        merge_rows(0, tm)

    tail = ubuf[tm:tm + HALO, :]
    ubuf[0:HALO, :] = tail

    @pl.when(t == nt - 1)
    def _conv_out():
        conv_out_ref[0] = tail[HALO - (CONV_W - 1):, :]

    @pl.when(t == nt - 1)
    def _gla_out():
        st = st_ref[...]
        for h in range(GLA_HEADS):
            gla_out_ref[0, h] = st[:, h * GLA_DK:(h + 1) * GLA_DK].T


def _mixer(x, conv_state, gla_state, p, *, tm, n_tok_total, tok_offset, token_bufs=None):
    bsz, seq, _ = x.shape
    chunk = min(CHUNK, seq)
    assert seq % tm == 0 and tm % chunk == 0 and chunk % SUB == 0 and tok_offset % tm == 0
    nt = seq // tm
    blk0 = tok_offset // tm
    full = lambda shape: pl.BlockSpec(shape, lambda b, t: (0,) * len(shape))
    alias = () if token_bufs is None else tuple(token_bufs)
    kern = functools.partial(_mixer_kernel, tm=tm, chunk=chunk, n_alias=len(alias))
    n_in = 16
    return pl.pallas_call(
        kern,
        grid=(bsz, nt),
        input_output_aliases={n_in + i: i for i in range(len(alias))},
        in_specs=[
            pl.BlockSpec((1, tm, D_MODEL), lambda b, t: (b, t, 0)),
            pl.BlockSpec((1, CONV_W - 1, CONV_CH), lambda b, t: (b, 0, 0)),
            pl.BlockSpec((1, GLA_HEADS, GLA_DK, GLA_DV), lambda b, t: (b, 0, 0, 0)),
            full((D_MODEL, IN_COLS_PAD)),
            full((8 * CONV_W, CONV_CH)),
            full((1, CONV_CH)),
            full((1, CONV_CH)),
            full((1, CONV_CH)),
            full((GATE_PAD, GLA_K)),
            full((1, GLA_K)),
            full((1, GLA_V)),
            full((MIX_W, D_MODEL)),
            full((1, D_MODEL)),
            full((1, D_MODEL)),
            full((D_MODEL, 256)),
            full((1, 128)),
        ] + [pl.BlockSpec(memory_space=pl.ANY)] * len(alias),
        out_specs=[
            pl.BlockSpec((N_SEG, tm, SC_SEG), lambda b, t: (0, blk0 + b * nt + t, 0)),
            pl.BlockSpec((tm, D_MODEL), lambda b, t: (blk0 + b * nt + t, 0)),
            pl.BlockSpec((tm, 128), lambda b, t: (blk0 + b * nt + t, 0)),
            pl.BlockSpec((1, CONV_W - 1, CONV_CH), lambda b, t: (b, 0, 0)),
            pl.BlockSpec((1, GLA_HEADS, GLA_DK, GLA_DV), lambda b, t: (b, 0, 0, 0)),
        ],
        out_shape=[
            jax.ShapeDtypeStruct((N_SEG, n_tok_total, SC_SEG), U32),
            jax.ShapeDtypeStruct((n_tok_total, D_MODEL), F32),
            jax.ShapeDtypeStruct((n_tok_total, 128), F32),
            jax.ShapeDtypeStruct((bsz, CONV_W - 1, CONV_CH), F32),
            jax.ShapeDtypeStruct((bsz, GLA_HEADS, GLA_DK, GLA_DV), F32),
        ],
        scratch_shapes=[
            pltpu.VMEM((HALO + tm, CONV_CH), F32),
            pltpu.VMEM((tm, GLA_K), F32),
            pltpu.VMEM((tm, GLA_K), F32),
            pltpu.VMEM((tm, GLA_K), F32),
            pltpu.VMEM((tm, GLA_V), F32),
            pltpu.VMEM((tm, GLA_V), F32),
            pltpu.VMEM((tm, MIX_W), BF16),
            pltpu.VMEM((GLA_DV, GLA_K), F32),
        ],
        compiler_params=pltpu.CompilerParams(
            dimension_semantics=("arbitrary", "arbitrary"),
            vmem_limit_bytes=VMEM_LIMIT),
        name="mixer",
    )(x, conv_state, gla_state, p["w_in"], p["w_dw"], p["b_dw"], p["ln_conv_g"], p["ln_conv_b"],
      p["w_gate"], p["b_gate"], p["gla_norm_g"], p["w_o"], p["ln1_g"], p["ln1_b"],
      p["w_router"], p["b_router"], *alias)


def _route_kernel(lg_ref, ri_ref, rg_ref, cnt_ref, carry_ref, *, tr):
    i = pl.program_id(0)

    @pl.when(i == 0)
    def _init():
        carry_ref[...] = jnp.zeros((N_EXPERTS, 128), F32)

    l = lg_ref[...].T[0:N_EXPERTS, :]
    row = lax.broadcasted_iota(jnp.int32, (N_EXPERTS, tr), 0)
    hots, vals, idxs = [], [], []
    for _ in range(TOP_K):
        m = jnp.max(l, axis=0, keepdims=True)
        ik = jnp.min(jnp.where(l == m, row, N_EXPERTS), axis=0, keepdims=True)
        hot = row == ik
        hots.append(hot)
        vals.append(m)
        idxs.append(ik)
        l = jnp.where(hot, -jnp.inf, l)
    es = [jnp.exp(v - vals[0]) for v in vals]
    denom = es[0] + es[1] + es[2] + es[3]
    member = jnp.zeros((N_EXPERTS, tr), F32)
    for hot in hots:
        member = member + hot.astype(F32)
    before = (lax.broadcasted_iota(jnp.int32, (tr, tr), 0)
              < lax.broadcasted_iota(jnp.int32, (tr, tr), 1)).astype(BF16)
    cum = _dot(member.astype(BF16), before) + carry_ref[:, 0:1]
    ranks = [jnp.sum(jnp.where(hot, cum, 0.0), axis=0, keepdims=True).astype(jnp.int32)
             for hot in hots]
    ri_ref[...] = jnp.concatenate(idxs + ranks, axis=0)
    gates = jnp.concatenate([e / denom for e in es] + [jnp.zeros((128 - TOP_K, tr), F32)], axis=0)
    rg_ref[...] = gates.T
    carry_ref[...] = carry_ref[...] + jnp.sum(member, axis=1, keepdims=True)

    @pl.when(i == pl.num_programs(0) - 1)
    def _fin():
        cnt_ref[...] = carry_ref[...]


def _route(logits, *, tr):
    n_tok = logits.shape[0]
    assert n_tok % tr == 0
    return pl.pallas_call(
        functools.partial(_route_kernel, tr=tr),
        grid=(n_tok // tr,),
        in_specs=[pl.BlockSpec((tr, 128), lambda i: (i, 0))],
        out_specs=[pl.BlockSpec((2 * TOP_K, tr), lambda i: (0, i)),
                   pl.BlockSpec((tr, 128), lambda i: (i, 0)),
                   pl.BlockSpec((N_EXPERTS, 128), lambda i: (0, 0))],
        out_shape=[jax.ShapeDtypeStruct((2 * TOP_K, n_tok), jnp.int32),
                   jax.ShapeDtypeStruct((n_tok, 128), F32),
                   jax.ShapeDtypeStruct((N_EXPERTS, 128), F32)],
        scratch_shapes=[pltpu.VMEM((N_EXPERTS, 128), F32)],
        compiler_params=pltpu.CompilerParams(dimension_semantics=("arbitrary",)),
        name="route",
    )(logits)


def _ffn_kernel(gs_ref, nu_ref, xs_hbm, wgu_ref, bgu_ref, wdn_ref, bdn_ref, yb_hbm,
                wgu_bf, wdn_bf, xin, yout, in_sem, out_sem):
    e = pl.program_id(0)
    n_used = nu_ref[0]
    g_first = gs_ref[e]
    g_end = gs_ref[e + 1]

    def in_copy(g, slot):
        rows = pl.ds(pl.multiple_of(g * EXPERT_BLOCK, EXPERT_BLOCK), EXPERT_BLOCK)
        return pltpu.make_async_copy(xs_hbm.at[:, rows, :], xin.at[slot], in_sem.at[slot])

    def out_copy(g, slot):
        rows = pl.ds(pl.multiple_of(g * EXPERT_BLOCK, EXPERT_BLOCK), EXPERT_BLOCK)
        return pltpu.make_async_copy(yout.at[slot], yb_hbm.at[:, rows, :], out_sem.at[slot])

    @pl.when(e == 0)
    def _prime():
        for g in range(IN_AHEAD):
            @pl.when(g < n_used)
            def _():
                in_copy(g, g).start()

    @pl.when(g_end > g_first)
    def _cast():
        wgu_bf[...] = wgu_ref[0].astype(BF16)
        wdn_bf[...] = wdn_ref[0].astype(BF16)

    def blocks(g0, n):
        for j in range(n):
            g = g0 + j
            in_copy(g, g % IN_RING).wait()
        for j in range(n):
            g = g0 + IN_AHEAD + j

            @pl.when(g < n_used)
            def _prefetch():
                in_copy(g, g % IN_RING).start()
        for j in range(n):
            g = g0 + j

            @pl.when(g >= OUT_RING)
            def _free_out():
                out_copy(g - OUT_RING, g % OUT_RING).wait()
        for j in range(n):
            g = g0 + j
            x_lo, x_hi = _unpack_segments([xin[g % IN_RING, s] for s in range(N_SEG)])
            hgu = (_dot(x_lo.astype(BF16), wgu_bf[0:HALF, :])
                   + _dot(x_hi.astype(BF16), wgu_bf[HALF:, :]) + bgu_ref[0])
            gate = jnp.minimum(hgu[:, :D_FF], SWIGLU_LIMIT)
            up = jnp.clip(hgu[:, D_FF:], -SWIGLU_LIMIT, SWIGLU_LIMIT)
            act = (up + 1.0) * gate * _sigmoid(SWIGLU_ALPHA * gate)
            yb = _dot(act.astype(BF16), wdn_bf[...]) + bdn_ref[0]
            for s, seg in enumerate(_pack_segments(yb)):
                yout[g % OUT_RING, s] = seg
        for j in range(n):
            g = g0 + j
            out_copy(g, g % OUT_RING).start()

    n_pairs = (g_end - g_first) // 2

    def pair(i, carry):
        blocks(g_first + 2 * i, 2)
        return carry

    lax.fori_loop(0, n_pairs, pair, 0)

    @pl.when((g_end - g_first) % 2 == 1)
    def _odd():
        blocks(g_end - 1, 1)

    @pl.when(e == pl.num_programs(0) - 1)
    def _drain():
        for back in range(OUT_RING, 0, -1):
            @pl.when(n_used >= back)
            def _():
                out_copy(n_used - back, (n_used - back) % OUT_RING).wait()


def _ffn(block_start, n_used, xs, w_gu, b_gu, w_down, b_down):
    n_rows = xs.shape[1]
    grid_spec = pltpu.PrefetchScalarGridSpec(
        num_scalar_prefetch=2,
        grid=(N_EXPERTS,),
        in_specs=[
            pl.BlockSpec(memory_space=pl.ANY),
            pl.BlockSpec((1, D_MODEL, 2 * D_FF), lambda e, gs, nu: (e, 0, 0)),
            pl.BlockSpec((1, 1, 2 * D_FF), lambda e, gs, nu: (e, 0, 0)),
            pl.BlockSpec((1, D_FF, D_MODEL), lambda e, gs, nu: (e, 0, 0)),
            pl.BlockSpec((1, 1, D_MODEL), lambda e, gs, nu: (e, 0, 0)),
        ],
        out_specs=pl.BlockSpec(memory_space=pl.ANY),
        scratch_shapes=[pltpu.VMEM((D_MODEL, 2 * D_FF), BF16),
                        pltpu.VMEM((D_FF, D_MODEL), BF16),
                        pltpu.VMEM((IN_RING, N_SEG, EXPERT_BLOCK, SC_SEG), U32),
                        pltpu.VMEM((OUT_RING, N_SEG, EXPERT_BLOCK, SC_SEG), U32),
                        pltpu.SemaphoreType.DMA((IN_RING,)),
                        pltpu.SemaphoreType.DMA((OUT_RING,))],
    )
    return pl.pallas_call(
        _ffn_kernel,
        grid_spec=grid_spec,
        out_shape=jax.ShapeDtypeStruct((N_SEG, n_rows, SC_SEG), U32),
        compiler_params=pltpu.CompilerParams(
            dimension_semantics=("arbitrary",), vmem_limit_bytes=VMEM_LIMIT),
        name="expert_ffn",
    )(block_start, n_used, xs, w_gu, b_gu.reshape(N_EXPERTS, 1, 2 * D_FF), w_down,
      b_down.reshape(N_EXPERTS, 1, D_MODEL))


def _combine_kernel(x1_ref, g_ref, rg_ref, ln2g_ref, ln2b_ref, *rest, blk0, n_prompt_blocks,
                    n_alias, with_sample):
    yp_ref = rest[n_alias]
    i = blk0 + pl.program_id(0)
    m_lo = jnp.zeros((x1_ref.shape[0], HALF), F32)
    m_hi = jnp.zeros((x1_ref.shape[0], HALF), F32)
    for k in range(TOP_K):
        g_lo, g_hi = _unpack_segments([g_ref[k, j] for j in range(N_SEG)])
        gate = rg_ref[:, k:k + 1]
        m_lo = m_lo + gate * g_lo
        m_hi = m_hi + gate * g_hi
    z = DN_ALPHA * x1_ref[...] + jnp.concatenate([m_lo, m_hi], axis=1)
    y = _layer_norm(z, ln2g_ref[...], ln2b_ref[...])

    if with_sample:
        ys_ref = rest[n_alias + 1]

        @pl.when(i < n_prompt_blocks)
        def _p():
            yp_ref[...] = y

        @pl.when(i >= n_prompt_blocks)
        def _s():
            ys_ref[...] = y
    else:
        yp_ref[...] = y


def _combine(x1, g, rg, ln2_g, ln2_b, *, tok0, n_prompt, tf, yp_buf=None):
    n_tok = x1.shape[0]
    piece = g.shape[2]
    n_sample = n_tok - n_prompt
    assert n_prompt % tf == 0 and n_sample % tf == 0 and tok0 % tf == 0 and piece % tf == 0
    npb = n_prompt // tf
    blk0 = tok0 // tf
    with_sample = tok0 + piece > n_prompt
    assert not with_sample or tok0 + piece == n_tok
    alias = () if yp_buf is None else (yp_buf,)
    out_specs = [pl.BlockSpec((tf, D_MODEL), lambda i: (jnp.minimum(blk0 + i, npb - 1), 0))]
    out_shape = [jax.ShapeDtypeStruct((n_prompt, D_MODEL), F32)]
    if with_sample:
        out_specs.append(pl.BlockSpec((tf, D_MODEL), lambda i: (jnp.maximum(blk0 + i - npb, 0), 0)))
        out_shape.append(jax.ShapeDtypeStruct((n_sample, D_MODEL), F32))
    n_in = 5
    return pl.pallas_call(
        functools.partial(_combine_kernel, blk0=blk0, n_prompt_blocks=npb, n_alias=len(alias),
                          with_sample=with_sample),
        grid=(piece // tf,),
        input_output_aliases={n_in + i: i for i in range(len(alias))},
        in_specs=[pl.BlockSpec((tf, D_MODEL), lambda i: (blk0 + i, 0)),
                  pl.BlockSpec((TOP_K, N_SEG, tf, SC_SEG), lambda i: (0, 0, i, 0)),
                  pl.BlockSpec((tf, 128), lambda i: (blk0 + i, 0)),
                  pl.BlockSpec((1, D_MODEL), lambda i: (0, 0)),
                  pl.BlockSpec((1, D_MODEL), lambda i: (0, 0))]
        + [pl.BlockSpec(memory_space=pl.ANY)] * len(alias),
        out_specs=out_specs,
        out_shape=out_shape,
        compiler_params=pltpu.CompilerParams(
            dimension_semantics=("arbitrary",), vmem_limit_bytes=VMEM_LIMIT),
        name="combine",
    )(x1, g, rg, ln2_g, ln2_b, *alias)


def _sc_mesh():
    return plsc.VectorSubcoreMesh(core_axis_name="c", subcore_axis_name="s")


def _sc_scatter_rows(src, dest, n_out):
    n_src, width = src.shape
    n_k = dest.shape[0]
    assert width == SC_SEG and n_src % SC_WINDOW == 0 and dest.shape[1] == n_src

    @pl.kernel(out_type=jax.ShapeDtypeStruct((n_out, width), src.dtype), mesh=_sc_mesh(),
               scratch_types=[])
    def scatter_kernel(src_hbm, idx_hbm, out_hbm):
        def body(src_vmem, idx_vmem):
            for k in range(n_k):
                pltpu.sync_copy(src_vmem, out_hbm.at[idx_vmem.at[k]])

        pltpu.emit_pipeline(
            body, grid=(n_src // SC_WINDOW,),
            in_specs=[pl.BlockSpec((SC_WINDOW, width), lambda i: (i, 0)),
                      pl.BlockSpec((n_k, SC_WINDOW), lambda i: (0, i))],
            out_specs=[],
            core_axis_name=("c", "s"),
            dimension_semantics=(pltpu.PARALLEL,),
        )(src_hbm, idx_hbm)

    return scatter_kernel(src, dest)


def _sc_gather_rows(src, idx):
    n_out = idx.shape[1]
    width = src.shape[1]
    assert width == SC_SEG and n_out % SC_WINDOW == 0

    @pl.kernel(out_type=jax.ShapeDtypeStruct((n_out, width), src.dtype), mesh=_sc_mesh(),
               scratch_types=[])
    def gather_kernel(src_hbm, idx_hbm, out_hbm):
        def body(idx_vmem, out_vmem):
            pltpu.sync_copy(src_hbm.at[idx_vmem.at[0]], out_vmem)

        pltpu.emit_pipeline(
            body, grid=(n_out // SC_WINDOW,),
            in_specs=[pl.BlockSpec((1, SC_WINDOW), lambda i: (0, i))],
            out_specs=[pl.BlockSpec((SC_WINDOW, width), lambda i: (i, 0))],
            core_axis_name=("c", "s"),
            dimension_semantics=(pltpu.PARALLEL,),
        )(idx_hbm, out_hbm)

    return gather_kernel(src, idx)


def _prep_params(w_in, w_dw, b_dw, ln_conv_g, ln_conv_b, w_gate_lr, b_gate, gla_norm_g, w_o,
                 ln1_g, ln1_b, w_router, b_router):
    row = lambda v: v.reshape(1, -1).astype(F32)
    w_in_p = jnp.pad(w_in, ((0, 0), (0, IN_COLS_PAD - w_in.shape[1]))).astype(BF16)
    w_gate = jnp.pad(w_gate_lr, ((0, GATE_PAD - GATE_RANK), (0, 0))).astype(BF16)
    wr_hi = w_router.astype(BF16)
    wr_lo = (w_router - wr_hi.astype(F32)).astype(BF16)
    padr = lambda m: jnp.pad(m, ((0, 0), (0, 128 - N_EXPERTS)))
    return {
        "w_in": w_in_p, "w_dw": jnp.repeat(w_dw, 8, axis=0), "b_dw": row(b_dw), "ln_conv_g": row(ln_conv_g),
        "ln_conv_b": row(ln_conv_b), "w_gate": w_gate, "b_gate": row(b_gate),
        "gla_norm_g": row(gla_norm_g), "w_o": w_o.astype(BF16), "ln1_g": row(ln1_g),
        "ln1_b": row(ln1_b), "w_router": jnp.concatenate([padr(wr_hi), padr(wr_lo)], axis=1),
        "b_router": jnp.pad(row(b_router), ((0, 0), (0, 128 - N_EXPERTS))),
    }


def _moe_layout(ri, counts):
    n_tok = ri.shape[1]
    idx = ri[0:TOP_K, :]
    rank = ri[TOP_K:2 * TOP_K, :]
    cnt = counts[:, 0].astype(jnp.int32)
    padded = (cnt + EXPERT_BLOCK - 1) // EXPERT_BLOCK * EXPERT_BLOCK
    pad_end = jnp.cumsum(padded)
    pad_start = pad_end - padded
    nb = -(-(n_tok * TOP_K) // EXPERT_BLOCK) + N_EXPERTS
    dest = rank
    for e in range(N_EXPERTS):
        dest = dest + jnp.where(idx == e, pad_start[e], 0)
    block_start = (jnp.concatenate([pad_start, pad_end[-1:]]) // EXPERT_BLOCK).astype(jnp.int32)
    n_used = block_start[-1:]
    return dest, block_start, n_used, nb


def kernel(x_prompt, x_sample, state_conv, state_gla, w_in, w_dw, b_dw, ln_conv_g, ln_conv_b,
           w_gate_lr, b_gate, gla_norm_g, w_o, ln1_g, ln1_b, w_router, b_router, w_gu, b_gu,
           w_down, b_down, ln2_g, ln2_b):
    assert w_in.shape[0] == DEPTH
    l = 0
    p = _prep_params(w_in[l], w_dw[l], b_dw[l], ln_conv_g[l], ln_conv_b[l], w_gate_lr[l],
                     b_gate[l], gla_norm_g[l], w_o[l], ln1_g[l], ln1_b[l], w_router[l],
                     b_router[l])
    bp, tp, _ = x_prompt.shape
    bs, ts, _ = x_sample.shape
    zc = jnp.zeros((bp, CONV_W - 1, CONV_CH), F32)
    zs = jnp.zeros((bp, GLA_HEADS, GLA_DK, GLA_DV), F32)
    n_prompt = bp * tp
    n_tok = n_prompt + bs * ts
    x1p, x1, lg, conv_p, gla_p = _mixer(x_prompt, zc, zs, p, tm=512, n_tok_total=n_tok,
                                        tok_offset=0)
    x1p, x1, lg, conv_s, gla_s = _mixer(x_sample, state_conv[l], state_gla[l], p, tm=ts,
                                        n_tok_total=n_tok, tok_offset=n_prompt,
                                        token_bufs=(x1p, x1, lg))

    ri, rg, counts = _route(lg, tr=768 if n_tok % 768 == 0 else 256)
    dest, block_start, n_used, nb = _moe_layout(ri, counts)
    n_slot = nb * EXPERT_BLOCK
    seg0 = jnp.arange(N_SEG, dtype=jnp.int32) * n_slot
    dest_seg = (dest[:, None, :] + seg0[None, :, None]).reshape(TOP_K, N_SEG * n_tok)
    xs = _sc_scatter_rows(x1p.reshape(N_SEG * n_tok, SC_SEG), dest_seg, N_SEG * n_slot)
    yb = _ffn(block_start, n_used, xs.reshape(N_SEG, n_slot, SC_SEG), w_gu[l], b_gu[l], w_down[l],
              b_down[l])
    tf = 256
    piece = -(-n_prompt // (COMBINE_PIECES * tf)) * tf
    bounds = [min(p * piece, n_prompt) for p in range(COMBINE_PIECES)] + [n_tok]
    bounds = sorted(set(bounds))
    dest_kjt = dest_seg.reshape(TOP_K, N_SEG, n_tok)
    yp = None
    for t0, t1 in zip(bounds[:-1], bounds[1:]):
        g = _sc_gather_rows(yb.reshape(N_SEG * n_slot, SC_SEG),
                            dest_kjt[:, :, t0:t1].reshape(1, -1))
        outs = _combine(x1, g.reshape(TOP_K, N_SEG, t1 - t0, SC_SEG), rg,
                        ln2_g[l].reshape(1, -1), ln2_b[l].reshape(1, -1), tok0=t0,
                        n_prompt=n_prompt, tf=tf, yp_buf=yp)
        yp = outs[0]
    ys = outs[1]
    return (yp.reshape(bp, tp, D_MODEL), ys.reshape(bs, ts, D_MODEL),
            conv_p[None], gla_p[None], conv_s[None], gla_s[None])
```

```python
import functools

import jax
import jax.numpy as jnp
from jax import lax
from jax.experimental import pallas as pl
from jax.experimental.pallas import tpu as pltpu
from jax.experimental.pallas import tpu_sc as plsc

F32 = jnp.float32
BF16 = jnp.bfloat16

D_MODEL = 1024
CHUNK = 64
SUB = 16
DIAG = 8
CONV_CH = 512
CONV_W = 31
HALO = 32
CONV_BLOCK = 64
COMBINE_PIECES = 8
IN_AHEAD = 2
IN_RING = IN_AHEAD + 2
OUT_RING = 4
GLA_HEADS = 4
GLA_DK = 64
GLA_DV = 128
GLA_K = GLA_HEADS * GLA_DK
GLA_V = GLA_HEADS * GLA_DV
GATE_RANK = 16
GATE_PAD = 128
GATE_NORM = 16.0
MIX_W = CONV_CH + GLA_V
OFF_UV = 0
OFF_UG = OFF_UV + CONV_CH
OFF_Q = OFF_UG + CONV_CH
OFF_K = OFF_Q + GLA_K
OFF_V = OFF_K + GLA_K
OFF_G = OFF_V + GLA_V
OFF_A = OFF_G + GLA_V
IN_COLS_PAD = OFF_A + GATE_PAD
N_EXPERTS = 32
TOP_K = 4
D_FF = 1024
SWIGLU_LIMIT = 7.0
SWIGLU_ALPHA = 1.702
EXPERT_BLOCK = 256
LN_EPS = 1e-5
DEPTH = 1
DN_ALPHA = (2 * DEPTH) ** 0.25
VMEM_LIMIT = 56 * 1024 * 1024
HALF = D_MODEL // 2
SC_SEG = 256
SC_WINDOW = 128
N_SEG = HALF // SC_SEG
U32 = jnp.uint32


def _dot(a, b):
    return jnp.dot(a, b, preferred_element_type=F32)


def _dot_nt(a, b):
    return lax.dot_general(a, b, (((1,), (1,)), ((), ())), preferred_element_type=F32)


def _dot_tn(a, b):
    return lax.dot_general(a, b, (((0,), (0,)), ((), ())), preferred_element_type=F32)


def _layer_norm(x, g, b):
    mu = jnp.mean(x, axis=-1, keepdims=True)
    xc = x - mu
    var = jnp.mean(xc * xc, axis=-1, keepdims=True)
    return xc * lax.rsqrt(var + LN_EPS) * g + b


def _sigmoid(x):
    return 1.0 / (1.0 + jnp.exp(-x))


def _split_bf16(x):
    hi = x.astype(BF16)
    lo = (x - hi.astype(F32)).astype(BF16)
    return hi, lo


def _pack_segments(x):
    bits = pltpu.bitcast(x.astype(BF16).astype(F32), U32)
    word = (bits[:, :HALF] >> 16) | bits[:, HALF:]
    return [word[:, j * SC_SEG:(j + 1) * SC_SEG] for j in range(N_SEG)]


def _unpack_segments(segs):
    word = jnp.concatenate(segs, axis=1)
    lo = pltpu.bitcast(word << 16, F32)
    hi = pltpu.bitcast(word & jnp.uint32(0xFFFF0000), F32)
    return lo, hi


def _mixer_kernel(x_ref, cs_ref, gs_ref, w_in_ref, w_dw_ref, b_dw_ref, lncg_ref, lncb_ref,
                  wg_ref, bg_ref, gng_ref, w_o_ref, ln1g_ref, ln1b_ref, wr_ref, br_ref, *rest,
                  tm, chunk, n_alias):
    (x1p_ref, x1_ref, logit_ref, conv_out_ref, gla_out_ref,
     ubuf, qbuf, kbuf, labuf, vbuf, gbuf, mixbuf, st_ref) = rest[n_alias:]
    t = pl.program_id(1)
    nt = pl.num_programs(1)
    n_chunks = tm // chunk
    n_sub = chunk // SUB

    @pl.when(t == 0)
    def _init():
        ubuf[0:HALO - (CONV_W - 1), :] = jnp.zeros((HALO - (CONV_W - 1), CONV_CH), F32)
        ubuf[HALO - (CONV_W - 1):HALO, :] = cs_ref[0]
        for h in range(GLA_HEADS):
            st_ref[:, h * GLA_DK:(h + 1) * GLA_DK] = gs_ref[0, h].T

    x = x_ref[0]
    xb = x.astype(BF16)

    hv = _dot(xb, w_in_ref[:, OFF_UV:OFF_UG])
    hg = _dot(xb, w_in_ref[:, OFF_UG:OFF_Q])
    u = hv * _sigmoid(hg)
    ubuf[HALO:HALO + tm, :] = u
    lead = HALO - (CONV_W - 1)

    def proj_q(r0, n):
        qbuf[r0:r0 + n, :] = _dot(xb[r0:r0 + n], w_in_ref[:, OFF_Q:OFF_K]) * (GLA_DK ** -0.5)

    def proj_k(r0, n):
        kbuf[r0:r0 + n, :] = _dot(xb[r0:r0 + n], w_in_ref[:, OFF_K:OFF_V])

    def proj_v(half, r0, n):
        lo = half * (GLA_V // 2)
        vbuf[r0:r0 + n, lo:lo + GLA_V // 2] = _dot(
            xb[r0:r0 + n], w_in_ref[:, OFF_V + lo:OFF_V + lo + GLA_V // 2])

    def proj_g(half, r0, n):
        lo = half * (GLA_V // 2)
        gbuf[r0:r0 + n, lo:lo + GLA_V // 2] = _dot(
            xb[r0:r0 + n], w_in_ref[:, OFF_G + lo:OFF_G + lo + GLA_V // 2])

    def proj_gate(r0, n):
        ha = _dot(xb[r0:r0 + n], w_in_ref[:, OFF_A:IN_COLS_PAD])
        a = _dot(ha.astype(BF16), wg_ref[...]) + bg_ref[...]
        labuf[r0:r0 + n, :] = ((jnp.minimum(a, 0.0) - jnp.log(1.0 + jnp.exp(-jnp.abs(a))))
                               * (1.0 / GATE_NORM))

    proj_jobs = [proj_q, proj_k, functools.partial(proj_v, 0), functools.partial(proj_v, 1),
                 functools.partial(proj_g, 0), functools.partial(proj_g, 1), proj_gate]

    cb = min(CONV_BLOCK, tm)

    def conv_block(r0):
        acc = jnp.zeros((cb, CONV_CH), F32) + b_dw_ref[...]
        for r in range(8):
            rows = cb if r == 0 else cb + 8
            part = None
            for s in range(r, lead + CONV_W, 8):
                if s < lead:
                    continue
                w8 = w_dw_ref[8 * (s - lead):8 * (s - lead) + 8, :]
                u3 = ubuf[pl.ds(r0 + (s - r), rows), :].reshape(rows // 8, 8, CONV_CH)
                term = u3 * w8
                part = term if part is None else part + term
            acc = acc + part.reshape(rows, CONV_CH)[r:r + cb, :]
        cact = _layer_norm(acc, lncg_ref[...], lncb_ref[...])
        cact = cact * _sigmoid(cact)
        mixbuf[pl.ds(r0, cb), 0:CONV_CH] = cact.astype(BF16)

    per_trip = 4 if n_chunks % 4 == 0 else 1
    n_trips = n_chunks // per_trip
    loop_blocks = (tm // 2) // (cb * n_trips) if n_chunks >= 4 else 0
    pre_blocks = tm // cb - loop_blocks * n_trips
    static_trips = n_trips == 2
    rows_first = tm // 2 if static_trips else tm
    for blk in range(max(pre_blocks, len(proj_jobs))):
        if blk < len(proj_jobs):
            proj_jobs[blk](0, rows_first)
        if blk < pre_blocks:
            conv_block(blk * cb)

    lane_k = lax.broadcasted_iota(jnp.int32, (1, GLA_K), 1) // GLA_DK
    row_c = lax.broadcasted_iota(jnp.int32, (chunk, 1), 0)
    tri = (lax.broadcasted_iota(jnp.int32, (chunk, chunk), 0)
           >= lax.broadcasted_iota(jnp.int32, (chunk, chunk), 1)).astype(BF16)
    e2 = (lax.broadcasted_iota(jnp.int32, (GLA_K, GLA_V), 0) // GLA_DK
          == lax.broadcasted_iota(jnp.int32, (GLA_K, GLA_V), 1) // GLA_DV).astype(BF16)
    row_s = lax.broadcasted_iota(jnp.int32, (GLA_HEADS * chunk, 1), 0)
    blk_s = (row_s % chunk) // SUB
    same_blk = blk_s == lax.broadcasted_iota(jnp.int32, (1, 128), 1) // SUB
    second_half = row_c % SUB >= DIAG
    row_g = lax.broadcasted_iota(jnp.int32, (1, DIAG, 1), 1)
    lane_v = lax.broadcasted_iota(jnp.int32, (1, GLA_V), 1) % GLA_DV

    def head_stack(m):
        return jnp.concatenate(
            [jnp.where(lane_k == h, m, 0.0) for h in range(GLA_HEADS)], axis=0).astype(BF16)

    def unstack(r, width):
        return jnp.concatenate(
            [r[h * chunk:(h + 1) * chunk, h * width:(h + 1) * width] for h in range(GLA_HEADS)],
            axis=1)

    def chunk_step(c, carry):
        base = c * chunk if isinstance(c, int) else pl.multiple_of(c * chunk, chunk)
        q_c = qbuf[pl.ds(base, chunk), :]
        k_c = kbuf[pl.ds(base, chunk), :]
        v_c = vbuf[pl.ds(base, chunk), :]
        v_b = v_c.astype(BF16)
        la_c = labuf[pl.ds(base, chunk), :]
        la_hi, la_lo = _split_bf16(la_c)
        b = _dot(tri, la_hi) + _dot(tri, la_lo)
        b_last = b[chunk - 1:chunk, :]
        st = st_ref[...]

        qe = q_c * jnp.exp(b)
        o2 = _dot_nt(head_stack(qe), st.astype(BF16))
        o = jnp.concatenate([o2[h * chunk:(h + 1) * chunk, :] for h in range(GLA_HEADS)], axis=1)

        a_off = None
        if n_sub > 1:
            r_rows = [b[0:SUB, :]] + [jnp.broadcast_to(b[SUB * i - 1:SUB * i, :], (SUB, GLA_K))
                                      for i in range(1, n_sub)]
            r_q = jnp.concatenate(r_rows, axis=0)
            q_t = jnp.where(row_c >= SUB, q_c * jnp.exp(jnp.minimum(b - r_q, 0.0)), 0.0)
            k_parts = []
            for i in range(1, n_sub):
                r_i = b[SUB * i - 1:SUB * i, :]
                k_t = jnp.where(row_c < SUB * i, k_c * jnp.exp(jnp.minimum(r_i - b, 0.0)), 0.0)
                k_parts.append(k_t)
                k_parts.append(jnp.zeros((128 - chunk, GLA_K), F32))
            k_cat = jnp.concatenate(k_parts, axis=0).astype(BF16)
            r = _dot_nt(head_stack(q_t), k_cat)
            a_off = r[:, 0:128]
            for i in range(2, n_sub):
                a_off = jnp.where(blk_s == i, r[:, (i - 1) * 128:i * 128], a_off)

        r_b = jnp.concatenate(
            [jnp.broadcast_to(b[SUB * i + DIAG - 1:SUB * i + DIAG, :], (SUB, GLA_K))
             for i in range(n_sub)], axis=0)
        q_b = jnp.where(second_half, q_c * jnp.exp(jnp.minimum(b - r_b, 0.0)), 0.0)
        k_b = jnp.where(second_half, 0.0, k_c * jnp.exp(jnp.minimum(r_b - b, 0.0)))
        k_b = jnp.concatenate([k_b, jnp.zeros((128 - chunk, GLA_K), F32)], axis=0).astype(BF16)
        a_half = jnp.where(same_blk, _dot_nt(head_stack(q_b), k_b), 0.0)
        a_off = a_half if a_off is None else a_off + a_half

        q3 = q_c.reshape(chunk // DIAG, DIAG, GLA_K)
        k3 = k_c.reshape(chunk // DIAG, DIAG, GLA_K)
        b3 = b.reshape(chunk // DIAG, DIAG, GLA_K)
        ps = []
        for d in range(DIAG):
            k_s = k3 if d == 0 else pltpu.roll(k3, d, axis=1)
            b_s = b3 if d == 0 else pltpu.roll(b3, d, axis=1)
            p = jnp.where(row_g >= d, q3 * k_s * jnp.exp(jnp.minimum(b3 - b_s, 0.0)), 0.0)
            ps.append(p.reshape(chunk, GLA_K).astype(BF16))
        w_all = _dot(jnp.concatenate(ps, axis=0), e2)
        a_d = jnp.zeros((chunk, GLA_V), F32)
        for d in range(DIAG):
            a_d = jnp.where(lane_v == row_c - d, w_all[d * chunk:(d + 1) * chunk, :], a_d)
        a_off = a_off + jnp.concatenate(
            [a_d[:, h * GLA_DV:(h + 1) * GLA_DV] for h in range(GLA_HEADS)], axis=0)
        o1 = _dot(a_off[:, 0:chunk].astype(BF16), v_b)
        o = o + unstack(o1, GLA_DV)

        ke = k_c * jnp.exp(b_last - b)
        r2 = _dot_tn(v_b, ke.astype(BF16))
        upd = jnp.zeros((GLA_DV, GLA_K), F32)
        for h in range(GLA_HEADS):
            upd = upd + jnp.where(lane_k == h, r2[h * GLA_DV:(h + 1) * GLA_DV, :], 0.0)
        st_ref[...] = st * jnp.exp(b_last) + upd

        g_c = gbuf[pl.ds(base, chunk), :]
        outs = []
        for h in range(GLA_HEADS):
            oh = o[:, h * GLA_DV:(h + 1) * GLA_DV]
            ms = jnp.mean(oh * oh, axis=-1, keepdims=True)
            outs.append(oh * lax.rsqrt(ms + LN_EPS))
        on = jnp.concatenate(outs, axis=1) * gng_ref[...]
        on = on * (g_c * _sigmoid(g_c))
        mixbuf[pl.ds(base, chunk), CONV_CH:MIX_W] = on.astype(BF16)
        return carry

    def trip(i, carry, extra_jobs=()):
        for j in range(max(per_trip, loop_blocks, len(extra_jobs))):
            if j < len(extra_jobs):
                extra_jobs[j](rows_first, tm - rows_first)
            if j < loop_blocks:
                r0 = (pre_blocks + i * loop_blocks + j) * cb
                conv_block(r0 if isinstance(r0, int) else pl.multiple_of(r0, cb))
            if j < per_trip:
                carry = chunk_step(i * per_trip + j, carry)
        return carry

    def merge_rows(r0, n):
        y = _dot(mixbuf[r0:r0 + n, :], w_o_ref[...])
        x1 = _layer_norm(DN_ALPHA * x_ref[0, r0:r0 + n, :] + y, ln1g_ref[...], ln1b_ref[...])
        x1_ref[r0:r0 + n, :] = x1
        for j, seg in enumerate(_pack_segments(x1)):
            x1p_ref[j, r0:r0 + n, :] = seg
        x_hi, x_lo = _split_bf16(x1)
        lg = _dot(x_hi, wr_ref[...])
        logit_ref[r0:r0 + n, :] = (lg[:, 0:128] + lg[:, 128:256] + _dot(x_lo, wr_ref[:, 0:128])
                                   + br_ref[...])

    if static_trips:
        trip(0, 0, tuple(proj_jobs))
        merge_rows(0, rows_first)
        trip(1, 0)
        merge_rows(rows_first, tm - rows_first)
    else:
        lax.fori_loop(0, n_trips, trip, 0)
        merge_rows(0, tm)

    tail = ubuf[tm:tm + HALO, :]
    ubuf[0:HALO, :] = tail

    @pl.when(t == nt - 1)
    def _conv_out():
        conv_out_ref[0] = tail[HALO - (CONV_W - 1):, :]

    @pl.when(t == nt - 1)
    def _gla_out():
        st = st_ref[...]
        for h in range(GLA_HEADS):
            gla_out_ref[0, h] = st[:, h * GLA_DK:(h + 1) * GLA_DK].T


def _mixer(x, conv_state, gla_state, p, *, tm, n_tok_total, tok_offset, token_bufs=None):
    bsz, seq, _ = x.shape
    chunk = min(CHUNK, seq)
    assert seq % tm == 0 and tm % chunk == 0 and chunk % SUB == 0 and tok_offset % tm == 0
    nt = seq // tm
    blk0 = tok_offset // tm
    full = lambda shape: pl.BlockSpec(shape, lambda b, t: (0,) * len(shape))
    alias = () if token_bufs is None else tuple(token_bufs)
    kern = functools.partial(_mixer_kernel, tm=tm, chunk=chunk, n_alias=len(alias))
    n_in = 16
    return pl.pallas_call(
        kern,
        grid=(bsz, nt),
        input_output_aliases={n_in + i: i for i in range(len(alias))},
        in_specs=[
            pl.BlockSpec((1, tm, D_MODEL), lambda b, t: (b, t, 0)),
            pl.BlockSpec((1, CONV_W - 1, CONV_CH), lambda b, t: (b, 0, 0)),
            pl.BlockSpec((1, GLA_HEADS, GLA_DK, GLA_DV), lambda b, t: (b, 0, 0, 0)),
            full((D_MODEL, IN_COLS_PAD)),
            full((8 * CONV_W, CONV_CH)),
            full((1, CONV_CH)),
            full((1, CONV_CH)),
            full((1, CONV_CH)),
            full((GATE_PAD, GLA_K)),
            full((1, GLA_K)),
            full((1, GLA_V)),
            full((MIX_W, D_MODEL)),
            full((1, D_MODEL)),
            full((1, D_MODEL)),
            full((D_MODEL, 256)),
            full((1, 128)),
        ] + [pl.BlockSpec(memory_space=pl.ANY)] * len(alias),
        out_specs=[
            pl.BlockSpec((N_SEG, tm, SC_SEG), lambda b, t: (0, blk0 + b * nt + t, 0)),
            pl.BlockSpec((tm, D_MODEL), lambda b, t: (blk0 + b * nt + t, 0)),
            pl.BlockSpec((tm, 128), lambda b, t: (blk0 + b * nt + t, 0)),
            pl.BlockSpec((1, CONV_W - 1, CONV_CH), lambda b, t: (b, 0, 0)),
            pl.BlockSpec((1, GLA_HEADS, GLA_DK, GLA_DV), lambda b, t: (b, 0, 0, 0)),
        ],
        out_shape=[
            jax.ShapeDtypeStruct((N_SEG, n_tok_total, SC_SEG), U32),
            jax.ShapeDtypeStruct((n_tok_total, D_MODEL), F32),
            jax.ShapeDtypeStruct((n_tok_total, 128), F32),
            jax.ShapeDtypeStruct((bsz, CONV_W - 1, CONV_CH), F32),
            jax.ShapeDtypeStruct((bsz, GLA_HEADS, GLA_DK, GLA_DV), F32),
        ],
        scratch_shapes=[
            pltpu.VMEM((HALO + tm, CONV_CH), F32),
            pltpu.VMEM((tm, GLA_K), F32),
            pltpu.VMEM((tm, GLA_K), F32),
            pltpu.VMEM((tm, GLA_K), F32),
            pltpu.VMEM((tm, GLA_V), F32),
            pltpu.VMEM((tm, GLA_V), F32),
            pltpu.VMEM((tm, MIX_W), BF16),
            pltpu.VMEM((GLA_DV, GLA_K), F32),
        ],
        compiler_params=pltpu.CompilerParams(
            dimension_semantics=("arbitrary", "arbitrary"),
            vmem_limit_bytes=VMEM_LIMIT),
        name="mixer",
    )(x, conv_state, gla_state, p["w_in"], p["w_dw"], p["b_dw"], p["ln_conv_g"], p["ln_conv_b"],
      p["w_gate"], p["b_gate"], p["gla_norm_g"], p["w_o"], p["ln1_g"], p["ln1_b"],
      p["w_router"], p["b_router"], *alias)


def _route_kernel(lg_ref, ri_ref, rg_ref, cnt_ref, carry_ref, *, tr):
    i = pl.program_id(0)

    @pl.when(i == 0)
    def _init():
        carry_ref[...] = jnp.zeros((N_EXPERTS, 128), F32)

    l = lg_ref[...].T[0:N_EXPERTS, :]
    row = lax.broadcasted_iota(jnp.int32, (N_EXPERTS, tr), 0)
    hots, vals, idxs = [], [], []
    for _ in range(TOP_K):
        m = jnp.max(l, axis=0, keepdims=True)
        ik = jnp.min(jnp.where(l == m, row, N_EXPERTS), axis=0, keepdims=True)
        hot = row == ik
        hots.append(hot)
        vals.append(m)
        idxs.append(ik)
        l = jnp.where(hot, -jnp.inf, l)
    es = [jnp.exp(v - vals[0]) for v in vals]
    denom = es[0] + es[1] + es[2] + es[3]
    member = jnp.zeros((N_EXPERTS, tr), F32)
    for hot in hots:
        member = member + hot.astype(F32)
    before = (lax.broadcasted_iota(jnp.int32, (tr, tr), 0)
              < lax.broadcasted_iota(jnp.int32, (tr, tr), 1)).astype(BF16)
    cum = _dot(member.astype(BF16), before) + carry_ref[:, 0:1]
    ranks = [jnp.sum(jnp.where(hot, cum, 0.0), axis=0, keepdims=True).astype(jnp.int32)
             for hot in hots]
    ri_ref[...] = jnp.concatenate(idxs + ranks, axis=0)
    gates = jnp.concatenate([e / denom for e in es] + [jnp.zeros((128 - TOP_K, tr), F32)], axis=0)
    rg_ref[...] = gates.T
    carry_ref[...] = carry_ref[...] + jnp.sum(member, axis=1, keepdims=True)

    @pl.when(i == pl.num_programs(0) - 1)
    def _fin():
        cnt_ref[...] = carry_ref[...]


def _route(logits, *, tr):
    n_tok = logits.shape[0]
    assert n_tok % tr == 0
    return pl.pallas_call(
        functools.partial(_route_kernel, tr=tr),
        grid=(n_tok // tr,),
        in_specs=[pl.BlockSpec((tr, 128), lambda i: (i, 0))],
        out_specs=[pl.BlockSpec((2 * TOP_K, tr), lambda i: (0, i)),
                   pl.BlockSpec((tr, 128), lambda i: (i, 0)),
                   pl.BlockSpec((N_EXPERTS, 128), lambda i: (0, 0))],
        out_shape=[jax.ShapeDtypeStruct((2 * TOP_K, n_tok), jnp.int32),
                   jax.ShapeDtypeStruct((n_tok, 128), F32),
                   jax.ShapeDtypeStruct((N_EXPERTS, 128), F32)],
        scratch_shapes=[pltpu.VMEM((N_EXPERTS, 128), F32)],
        compiler_params=pltpu.CompilerParams(dimension_semantics=("arbitrary",)),
        name="route",
    )(logits)


def _ffn_kernel(gs_ref, nu_ref, xs_hbm, wgu_ref, bgu_ref, wdn_ref, bdn_ref, yb_hbm,
                wgu_bf, wdn_bf, xin, yout, in_sem, out_sem):
    e = pl.program_id(0)
    n_used = nu_ref[0]
    g_first = gs_ref[e]
    g_end = gs_ref[e + 1]

    def in_copy(g, slot):
        rows = pl.ds(pl.multiple_of(g * EXPERT_BLOCK, EXPERT_BLOCK), EXPERT_BLOCK)
        return pltpu.make_async_copy(xs_hbm.at[:, rows, :], xin.at[slot], in_sem.at[slot])

    def out_copy(g, slot):
        rows = pl.ds(pl.multiple_of(g * EXPERT_BLOCK, EXPERT_BLOCK), EXPERT_BLOCK)
        return pltpu.make_async_copy(yout.at[slot], yb_hbm.at[:, rows, :], out_sem.at[slot])

    @pl.when(e == 0)
    def _prime():
        for g in range(IN_AHEAD):
            @pl.when(g < n_used)
            def _():
                in_copy(g, g).start()

    @pl.when(g_end > g_first)
    def _cast():
        wgu_bf[...] = wgu_ref[0].astype(BF16)
        wdn_bf[...] = wdn_ref[0].astype(BF16)

    def blocks(g0, n):
        for j in range(n):
            g = g0 + j
            in_copy(g, g % IN_RING).wait()
        for j in range(n):
            g = g0 + IN_AHEAD + j

            @pl.when(g < n_used)
            def _prefetch():
                in_copy(g, g % IN_RING).start()
        for j in range(n):
            g = g0 + j

            @pl.when(g >= OUT_RING)
            def _free_out():
                out_copy(g - OUT_RING, g % OUT_RING).wait()
        for j in range(n):
            g = g0 + j
            x_lo, x_hi = _unpack_segments([xin[g % IN_RING, s] for s in range(N_SEG)])
            hgu = (_dot(x_lo.astype(BF16), wgu_bf[0:HALF, :])
                   + _dot(x_hi.astype(BF16), wgu_bf[HALF:, :]) + bgu_ref[0])
            gate = jnp.minimum(hgu[:, :D_FF], SWIGLU_LIMIT)
            up = jnp.clip(hgu[:, D_FF:], -SWIGLU_LIMIT, SWIGLU_LIMIT)
            act = (up + 1.0) * gate * _sigmoid(SWIGLU_ALPHA * gate)
            yb = _dot(act.astype(BF16), wdn_bf[...]) + bdn_ref[0]
            for s, seg in enumerate(_pack_segments(yb)):
                yout[g % OUT_RING, s] = seg
        for j in range(n):
            g = g0 + j
            out_copy(g, g % OUT_RING).start()

    n_pairs = (g_end - g_first) // 2

    def pair(i, carry):
        blocks(g_first + 2 * i, 2)
        return carry

    lax.fori_loop(0, n_pairs, pair, 0)

    @pl.when((g_end - g_first) % 2 == 1)
    def _odd():
        blocks(g_end - 1, 1)

    @pl.when(e == pl.num_programs(0) - 1)
    def _drain():
        for back in range(OUT_RING, 0, -1):
            @pl.when(n_used >= back)
            def _():
                out_copy(n_used - back, (n_used - back) % OUT_RING).wait()


def _ffn(block_start, n_used, xs, w_gu, b_gu, w_down, b_down):
    n_rows = xs.shape[1]
    grid_spec = pltpu.PrefetchScalarGridSpec(
        num_scalar_prefetch=2,
        grid=(N_EXPERTS,),
        in_specs=[
            pl.BlockSpec(memory_space=pl.ANY),
            pl.BlockSpec((1, D_MODEL, 2 * D_FF), lambda e, gs, nu: (e, 0, 0)),
            pl.BlockSpec((1, 1, 2 * D_FF), lambda e, gs, nu: (e, 0, 0)),
            pl.BlockSpec((1, D_FF, D_MODEL), lambda e, gs, nu: (e, 0, 0)),
            pl.BlockSpec((1, 1, D_MODEL), lambda e, gs, nu: (e, 0, 0)),
        ],
        out_specs=pl.BlockSpec(memory_space=pl.ANY),
        scratch_shapes=[pltpu.VMEM((D_MODEL, 2 * D_FF), BF16),
                        pltpu.VMEM((D_FF, D_MODEL), BF16),
                        pltpu.VMEM((IN_RING, N_SEG, EXPERT_BLOCK, SC_SEG), U32),
                        pltpu.VMEM((OUT_RING, N_SEG, EXPERT_BLOCK, SC_SEG), U32),
                        pltpu.SemaphoreType.DMA((IN_RING,)),
                        pltpu.SemaphoreType.DMA((OUT_RING,))],
    )
    return pl.pallas_call(
        _ffn_kernel,
        grid_spec=grid_spec,
        out_shape=jax.ShapeDtypeStruct((N_SEG, n_rows, SC_SEG), U32),
        compiler_params=pltpu.CompilerParams(
            dimension_semantics=("arbitrary",), vmem_limit_bytes=VMEM_LIMIT),
        name="expert_ffn",
    )(block_start, n_used, xs, w_gu, b_gu.reshape(N_EXPERTS, 1, 2 * D_FF), w_down,
      b_down.reshape(N_EXPERTS, 1, D_MODEL))


def _combine_kernel(x1_ref, g_ref, rg_ref, ln2g_ref, ln2b_ref, *rest, blk0, n_prompt_blocks,
                    n_alias, with_sample):
    yp_ref = rest[n_alias]
    i = blk0 + pl.program_id(0)
    m_lo = jnp.zeros((x1_ref.shape[0], HALF), F32)
    m_hi = jnp.zeros((x1_ref.shape[0], HALF), F32)
    for k in range(TOP_K):
        g_lo, g_hi = _unpack_segments([g_ref[k, j] for j in range(N_SEG)])
        gate = rg_ref[:, k:k + 1]
        m_lo = m_lo + gate * g_lo
        m_hi = m_hi + gate * g_hi
    z = DN_ALPHA * x1_ref[...] + jnp.concatenate([m_lo, m_hi], axis=1)
    y = _layer_norm(z, ln2g_ref[...], ln2b_ref[...])

    if with_sample:
        ys_ref = rest[n_alias + 1]

        @pl.when(i < n_prompt_blocks)
        def _p():
            yp_ref[...] = y

        @pl.when(i >= n_prompt_blocks)
        def _s():
            ys_ref[...] = y
    else:
        yp_ref[...] = y


def _combine(x1, g, rg, ln2_g, ln2_b, *, tok0, n_prompt, tf, yp_buf=None):
    n_tok = x1.shape[0]
    piece = g.shape[2]
    n_sample = n_tok - n_prompt
    assert n_prompt % tf == 0 and n_sample % tf == 0 and tok0 % tf == 0 and piece % tf == 0
    npb = n_prompt // tf
    blk0 = tok0 // tf
    with_sample = tok0 + piece > n_prompt
    assert not with_sample or tok0 + piece == n_tok
    alias = () if yp_buf is None else (yp_buf,)
    out_specs = [pl.BlockSpec((tf, D_MODEL), lambda i: (jnp.minimum(blk0 + i, npb - 1), 0))]
    out_shape = [jax.ShapeDtypeStruct((n_prompt, D_MODEL), F32)]
    if with_sample:
        out_specs.append(pl.BlockSpec((tf, D_MODEL), lambda i: (jnp.maximum(blk0 + i - npb, 0), 0)))
        out_shape.append(jax.ShapeDtypeStruct((n_sample, D_MODEL), F32))
    n_in = 5
    return pl.pallas_call(
        functools.partial(_combine_kernel, blk0=blk0, n_prompt_blocks=npb, n_alias=len(alias),
                          with_sample=with_sample),
        grid=(piece // tf,),
        input_output_aliases={n_in + i: i for i in range(len(alias))},
        in_specs=[pl.BlockSpec((tf, D_MODEL), lambda i: (blk0 + i, 0)),
                  pl.BlockSpec((TOP_K, N_SEG, tf, SC_SEG), lambda i: (0, 0, i, 0)),
                  pl.BlockSpec((tf, 128), lambda i: (blk0 + i, 0)),
                  pl.BlockSpec((1, D_MODEL), lambda i: (0, 0)),
                  pl.BlockSpec((1, D_MODEL), lambda i: (0, 0))]
        + [pl.BlockSpec(memory_space=pl.ANY)] * len(alias),
        out_specs=out_specs,
        out_shape=out_shape,
        compiler_params=pltpu.CompilerParams(
            dimension_semantics=("arbitrary",), vmem_limit_bytes=VMEM_LIMIT),
        name="combine",
    )(x1, g, rg, ln2_g, ln2_b, *alias)


def _sc_mesh():
    return plsc.VectorSubcoreMesh(core_axis_name="c", subcore_axis_name="s")


def _sc_scatter_rows(src, dest, n_out):
    n_src, width = src.shape
    n_k = dest.shape[0]
    assert width == SC_SEG and n_src % SC_WINDOW == 0 and dest.shape[1] == n_src

    @pl.kernel(out_type=jax.ShapeDtypeStruct((n_out, width), src.dtype), mesh=_sc_mesh(),
               scratch_types=[])
    def scatter_kernel(src_hbm, idx_hbm, out_hbm):
        def body(src_vmem, idx_vmem):
            for k in range(n_k):
                pltpu.sync_copy(src_vmem, out_hbm.at[idx_vmem.at[k]])

        pltpu.emit_pipeline(
            body, grid=(n_src // SC_WINDOW,),
            in_specs=[pl.BlockSpec((SC_WINDOW, width), lambda i: (i, 0)),
                      pl.BlockSpec((n_k, SC_WINDOW), lambda i: (0, i))],
            out_specs=[],
            core_axis_name=("c", "s"),
            dimension_semantics=(pltpu.PARALLEL,),
        )(src_hbm, idx_hbm)

    return scatter_kernel(src, dest)


def _sc_gather_rows(src, idx):
    n_out = idx.shape[1]
    width = src.shape[1]
    assert width == SC_SEG and n_out % SC_WINDOW == 0

    @pl.kernel(out_type=jax.ShapeDtypeStruct((n_out, width), src.dtype), mesh=_sc_mesh(),
               scratch_types=[])
    def gather_kernel(src_hbm, idx_hbm, out_hbm):
        def body(idx_vmem, out_vmem):
            pltpu.sync_copy(src_hbm.at[idx_vmem.at[0]], out_vmem)

        pltpu.emit_pipeline(
            body, grid=(n_out // SC_WINDOW,),
            in_specs=[pl.BlockSpec((1, SC_WINDOW), lambda i: (0, i))],
            out_specs=[pl.BlockSpec((SC_WINDOW, width), lambda i: (i, 0))],
            core_axis_name=("c", "s"),
            dimension_semantics=(pltpu.PARALLEL,),
        )(idx_hbm, out_hbm)

    return gather_kernel(src, idx)


def _prep_params(w_in, w_dw, b_dw, ln_conv_g, ln_conv_b, w_gate_lr, b_gate, gla_norm_g, w_o,
                 ln1_g, ln1_b, w_router, b_router):
    row = lambda v: v.reshape(1, -1).astype(F32)
    w_in_p = jnp.pad(w_in, ((0, 0), (0, IN_COLS_PAD - w_in.shape[1]))).astype(BF16)
    w_gate = jnp.pad(w_gate_lr, ((0, GATE_PAD - GATE_RANK), (0, 0))).astype(BF16)
    wr_hi = w_router.astype(BF16)
    wr_lo = (w_router - wr_hi.astype(F32)).astype(BF16)
    padr = lambda m: jnp.pad(m, ((0, 0), (0, 128 - N_EXPERTS)))
    return {
        "w_in": w_in_p, "w_dw": jnp.repeat(w_dw, 8, axis=0), "b_dw": row(b_dw), "ln_conv_g": row(ln_conv_g),
        "ln_conv_b": row(ln_conv_b), "w_gate": w_gate, "b_gate": row(b_gate),
        "gla_norm_g": row(gla_norm_g), "w_o": w_o.astype(BF16), "ln1_g": row(ln1_g),
        "ln1_b": row(ln1_b), "w_router": jnp.concatenate([padr(wr_hi), padr(wr_lo)], axis=1),
        "b_router": jnp.pad(row(b_router), ((0, 0), (0, 128 - N_EXPERTS))),
    }


def _dest_kernel(ps_ref, ri_ref, out_ref, *, n_slot):
    idx = ri_ref[0:TOP_K, :]
    dest = ri_ref[TOP_K:2 * TOP_K, :] + pl.program_id(0) * n_slot
    for e in range(N_EXPERTS):
        dest = dest + jnp.where(idx == e, ps_ref[e], 0)
    out_ref[...] = dest


def _moe_layout(ri, counts, *, tr):
    n_tok = ri.shape[1]
    cnt = counts[:, 0].astype(jnp.int32)
    padded = (cnt + EXPERT_BLOCK - 1) // EXPERT_BLOCK * EXPERT_BLOCK
    pad_end = jnp.cumsum(padded)
    pad_start = pad_end - padded
    nb = -(-(n_tok * TOP_K) // EXPERT_BLOCK) + N_EXPERTS
    n_t = n_tok // tr
    dest_seg = pl.pallas_call(
        functools.partial(_dest_kernel, n_slot=nb * EXPERT_BLOCK),
        grid_spec=pltpu.PrefetchScalarGridSpec(
            num_scalar_prefetch=1,
            grid=(N_SEG, n_t),
            in_specs=[pl.BlockSpec((2 * TOP_K, tr), lambda j, i, ps: (0, i))],
            out_specs=pl.BlockSpec((TOP_K, tr), lambda j, i, ps: (0, j * n_t + i)),
        ),
        out_shape=jax.ShapeDtypeStruct((TOP_K, N_SEG * n_tok), jnp.int32),
        name="slot_index",
    )(pad_start.astype(jnp.int32), ri)
    block_start = (jnp.concatenate([pad_start, pad_end[-1:]]) // EXPERT_BLOCK).astype(jnp.int32)
    n_used = block_start[-1:]
    return dest_seg, block_start, n_used, nb


def kernel(x_prompt, x_sample, state_conv, state_gla, w_in, w_dw, b_dw, ln_conv_g, ln_conv_b,
           w_gate_lr, b_gate, gla_norm_g, w_o, ln1_g, ln1_b, w_router, b_router, w_gu, b_gu,
           w_down, b_down, ln2_g, ln2_b):
    assert w_in.shape[0] == DEPTH
    l = 0
    p = _prep_params(w_in[l], w_dw[l], b_dw[l], ln_conv_g[l], ln_conv_b[l], w_gate_lr[l],
                     b_gate[l], gla_norm_g[l], w_o[l], ln1_g[l], ln1_b[l], w_router[l],
                     b_router[l])
    bp, tp, _ = x_prompt.shape
    bs, ts, _ = x_sample.shape
    zc = jnp.zeros((bp, CONV_W - 1, CONV_CH), F32)
    zs = jnp.zeros((bp, GLA_HEADS, GLA_DK, GLA_DV), F32)
    n_prompt = bp * tp
    n_tok = n_prompt + bs * ts
    x1p, x1, lg, conv_p, gla_p = _mixer(x_prompt, zc, zs, p, tm=512, n_tok_total=n_tok,
                                        tok_offset=0)
    x1p, x1, lg, conv_s, gla_s = _mixer(x_sample, state_conv[l], state_gla[l], p, tm=ts,
                                        n_tok_total=n_tok, tok_offset=n_prompt,
                                        token_bufs=(x1p, x1, lg))

    tr = 768 if n_tok % 768 == 0 else 256
    ri, rg, counts = _route(lg, tr=tr)
    dest_seg, block_start, n_used, nb = _moe_layout(ri, counts, tr=tr)
    n_slot = nb * EXPERT_BLOCK
    xs = _sc_scatter_rows(x1p.reshape(N_SEG * n_tok, SC_SEG), dest_seg, N_SEG * n_slot)
    yb = _ffn(block_start, n_used, xs.reshape(N_SEG, n_slot, SC_SEG), w_gu[l], b_gu[l], w_down[l],
              b_down[l])
    tf = 256
    piece = -(-n_prompt // (COMBINE_PIECES * tf)) * tf
    bounds = [min(p * piece, n_prompt) for p in range(COMBINE_PIECES)] + [n_tok]
    bounds = sorted(set(bounds))
    dest_kjt = dest_seg.reshape(TOP_K, N_SEG, n_tok)
    yp = None
    for t0, t1 in zip(bounds[:-1], bounds[1:]):
        g = _sc_gather_rows(yb.reshape(N_SEG * n_slot, SC_SEG),
                            dest_kjt[:, :, t0:t1].reshape(1, -1))
        outs = _combine(x1, g.reshape(TOP_K, N_SEG, t1 - t0, SC_SEG), rg,
                        ln2_g[l].reshape(1, -1), ln2_b[l].reshape(1, -1), tok0=t0,
                        n_prompt=n_prompt, tf=tf, yp_buf=yp)
        yp = outs[0]
    ys = outs[1]
    return (yp.reshape(bp, tp, D_MODEL), ys.reshape(bs, ts, D_MODEL),
            conv_p[None], gla_p[None], conv_s[None], gla_s[None])
```

```python
import functools

import jax
import jax.numpy as jnp
from jax import lax
from jax.experimental import pallas as pl
from jax.experimental.pallas import tpu as pltpu
from jax.experimental.pallas import tpu_sc as plsc

F32 = jnp.float32
BF16 = jnp.bfloat16

D_MODEL = 1024
CHUNK = 64
SUB = 16
DIAG = 8
CONV_CH = 512
CONV_W = 31
HALO = 32
CONV_BLOCK = 64
COMBINE_PIECES = 8
TRIP_BLOCKS = 4
IN_AHEAD = TRIP_BLOCKS
IN_RING = 2 * TRIP_BLOCKS
OUT_RING = 2 * TRIP_BLOCKS
GLA_HEADS = 4
GLA_DK = 64
GLA_DV = 128
GLA_K = GLA_HEADS * GLA_DK
GLA_V = GLA_HEADS * GLA_DV
GATE_RANK = 16
GATE_PAD = 128
GATE_NORM = 16.0
MIX_W = CONV_CH + GLA_V
OFF_UV = 0
OFF_UG = OFF_UV + CONV_CH
OFF_Q = OFF_UG + CONV_CH
OFF_K = OFF_Q + GLA_K
OFF_V = OFF_K + GLA_K
OFF_G = OFF_V + GLA_V
OFF_A = OFF_G + GLA_V
IN_COLS_PAD = OFF_A + GATE_PAD
N_EXPERTS = 32
TOP_K = 4
D_FF = 1024
SWIGLU_LIMIT = 7.0
SWIGLU_ALPHA = 1.702
EXPERT_BLOCK = 256
LN_EPS = 1e-5
DEPTH = 1
DN_ALPHA = (2 * DEPTH) ** 0.25
VMEM_LIMIT = 56 * 1024 * 1024
HALF = D_MODEL // 2
SC_SEG = 256
SC_WINDOW = 128
N_SEG = HALF // SC_SEG
U32 = jnp.uint32


def _dot(a, b):
    return jnp.dot(a, b, preferred_element_type=F32)


def _dot_nt(a, b):
    return lax.dot_general(a, b, (((1,), (1,)), ((), ())), preferred_element_type=F32)


def _dot_tn(a, b):
    return lax.dot_general(a, b, (((0,), (0,)), ((), ())), preferred_element_type=F32)


def _layer_norm(x, g, b):
    mu = jnp.mean(x, axis=-1, keepdims=True)
    xc = x - mu
    var = jnp.mean(xc * xc, axis=-1, keepdims=True)
    return xc * lax.rsqrt(var + LN_EPS) * g + b


def _sigmoid(x):
    return 1.0 / (1.0 + jnp.exp(-x))


def _split_bf16(x):
    hi = x.astype(BF16)
    lo = (x - hi.astype(F32)).astype(BF16)
    return hi, lo


def _pack_segments(x):
    bits = pltpu.bitcast(x.astype(BF16).astype(F32), U32)
    word = (bits[:, :HALF] >> 16) | bits[:, HALF:]
    return [word[:, j * SC_SEG:(j + 1) * SC_SEG] for j in range(N_SEG)]


def _unpack_segments(segs):
    word = jnp.concatenate(segs, axis=1)
    lo = pltpu.bitcast(word << 16, F32)
    hi = pltpu.bitcast(word & jnp.uint32(0xFFFF0000), F32)
    return lo, hi


def _mixer_kernel(x_ref, cs_ref, gs_ref, w_in_ref, w_dw_ref, b_dw_ref, lncg_ref, lncb_ref,
                  wg_ref, bg_ref, gng_ref, w_o_ref, ln1g_ref, ln1b_ref, wr_ref, br_ref, *rest,
                  tm, chunk, n_alias):
    (x1p_ref, x1_ref, logit_ref, conv_out_ref, gla_out_ref,
     ubuf, qbuf, kbuf, labuf, vbuf, gbuf, mixbuf, st_ref) = rest[n_alias:]
    t = pl.program_id(1)
    nt = pl.num_programs(1)
    n_chunks = tm // chunk
    n_sub = chunk // SUB

    @pl.when(t == 0)
    def _init():
        ubuf[0:HALO - (CONV_W - 1), :] = jnp.zeros((HALO - (CONV_W - 1), CONV_CH), F32)
        ubuf[HALO - (CONV_W - 1):HALO, :] = cs_ref[0]
        for h in range(GLA_HEADS):
            st_ref[:, h * GLA_DK:(h + 1) * GLA_DK] = gs_ref[0, h].T

    x = x_ref[0]
    xb = x.astype(BF16)

    hv = _dot(xb, w_in_ref[:, OFF_UV:OFF_UG])
    hg = _dot(xb, w_in_ref[:, OFF_UG:OFF_Q])
    u = hv * _sigmoid(hg)
    ubuf[HALO:HALO + tm, :] = u
    lead = HALO - (CONV_W - 1)

    def proj_q(r0, n):
        qbuf[r0:r0 + n, :] = _dot(xb[r0:r0 + n], w_in_ref[:, OFF_Q:OFF_K]) * (GLA_DK ** -0.5)

    def proj_k(r0, n):
        kbuf[r0:r0 + n, :] = _dot(xb[r0:r0 + n], w_in_ref[:, OFF_K:OFF_V])

    def proj_v(half, r0, n):
        lo = half * (GLA_V // 2)
        vbuf[r0:r0 + n, lo:lo + GLA_V // 2] = _dot(
            xb[r0:r0 + n], w_in_ref[:, OFF_V + lo:OFF_V + lo + GLA_V // 2])

    def proj_g(half, r0, n):
        lo = half * (GLA_V // 2)
        gbuf[r0:r0 + n, lo:lo + GLA_V // 2] = _dot(
            xb[r0:r0 + n], w_in_ref[:, OFF_G + lo:OFF_G + lo + GLA_V // 2])

    def proj_gate(r0, n):
        ha = _dot(xb[r0:r0 + n], w_in_ref[:, OFF_A:IN_COLS_PAD])
        a = _dot(ha.astype(BF16), wg_ref[...]) + bg_ref[...]
        labuf[r0:r0 + n, :] = ((jnp.minimum(a, 0.0) - jnp.log(1.0 + jnp.exp(-jnp.abs(a))))
                               * (1.0 / GATE_NORM))

    proj_jobs = [proj_q, proj_k, functools.partial(proj_v, 0), functools.partial(proj_v, 1),
                 functools.partial(proj_g, 0), functools.partial(proj_g, 1), proj_gate]

    cb = min(CONV_BLOCK, tm)

    def conv_block(r0):
        acc = jnp.zeros((cb, CONV_CH), F32) + b_dw_ref[...]
        for r in range(8):
            rows = cb if r == 0 else cb + 8
            part = None
            for s in range(r, lead + CONV_W, 8):
                if s < lead:
                    continue
                w8 = w_dw_ref[8 * (s - lead):8 * (s - lead) + 8, :]
                u3 = ubuf[pl.ds(r0 + (s - r), rows), :].reshape(rows // 8, 8, CONV_CH)
                term = u3 * w8
                part = term if part is None else part + term
            acc = acc + part.reshape(rows, CONV_CH)[r:r + cb, :]
        cact = _layer_norm(acc, lncg_ref[...], lncb_ref[...])
        cact = cact * _sigmoid(cact)
        mixbuf[pl.ds(r0, cb), 0:CONV_CH] = cact.astype(BF16)

    per_trip = 4 if n_chunks % 4 == 0 else 1
    n_trips = n_chunks // per_trip
    loop_blocks = (tm // 2) // (cb * n_trips) if n_chunks >= 4 else 0
    pre_blocks = tm // cb - loop_blocks * n_trips
    static_trips = n_trips == 2
    rows_first = tm // 2 if static_trips else tm
    for blk in range(max(pre_blocks, len(proj_jobs))):
        if blk < len(proj_jobs):
            proj_jobs[blk](0, rows_first)
        if blk < pre_blocks:
            conv_block(blk * cb)

    lane_k = lax.broadcasted_iota(jnp.int32, (1, GLA_K), 1) // GLA_DK
    row_c = lax.broadcasted_iota(jnp.int32, (chunk, 1), 0)
    tri = (lax.broadcasted_iota(jnp.int32, (chunk, chunk), 0)
           >= lax.broadcasted_iota(jnp.int32, (chunk, chunk), 1)).astype(BF16)
    e2 = (lax.broadcasted_iota(jnp.int32, (GLA_K, GLA_V), 0) // GLA_DK
          == lax.broadcasted_iota(jnp.int32, (GLA_K, GLA_V), 1) // GLA_DV).astype(BF16)
    row_s = lax.broadcasted_iota(jnp.int32, (GLA_HEADS * chunk, 1), 0)
    blk_s = (row_s % chunk) // SUB
    same_blk = blk_s == lax.broadcasted_iota(jnp.int32, (1, 128), 1) // SUB
    second_half = row_c % SUB >= DIAG
    row_g = lax.broadcasted_iota(jnp.int32, (1, DIAG, 1), 1)
    lane_v = lax.broadcasted_iota(jnp.int32, (1, GLA_V), 1) % GLA_DV

    def head_stack(m):
        return jnp.concatenate(
            [jnp.where(lane_k == h, m, 0.0) for h in range(GLA_HEADS)], axis=0).astype(BF16)

    def unstack(r, width):
        return jnp.concatenate(
            [r[h * chunk:(h + 1) * chunk, h * width:(h + 1) * width] for h in range(GLA_HEADS)],
            axis=1)

    def chunk_step(c, carry):
        base = c * chunk if isinstance(c, int) else pl.multiple_of(c * chunk, chunk)
        q_c = qbuf[pl.ds(base, chunk), :]
        k_c = kbuf[pl.ds(base, chunk), :]
        v_c = vbuf[pl.ds(base, chunk), :]
        v_b = v_c.astype(BF16)
        la_c = labuf[pl.ds(base, chunk), :]
        la_hi, la_lo = _split_bf16(la_c)
        b = _dot(tri, la_hi) + _dot(tri, la_lo)
        b_last = b[chunk - 1:chunk, :]
        st = st_ref[...]

        qe = q_c * jnp.exp(b)
        o2 = _dot_nt(head_stack(qe), st.astype(BF16))
        o = jnp.concatenate([o2[h * chunk:(h + 1) * chunk, :] for h in range(GLA_HEADS)], axis=1)

        a_off = None
        if n_sub > 1:
            r_rows = [b[0:SUB, :]] + [jnp.broadcast_to(b[SUB * i - 1:SUB * i, :], (SUB, GLA_K))
                                      for i in range(1, n_sub)]
            r_q = jnp.concatenate(r_rows, axis=0)
            q_t = jnp.where(row_c >= SUB, q_c * jnp.exp(jnp.minimum(b - r_q, 0.0)), 0.0)
            k_parts = []
            for i in range(1, n_sub):
                r_i = b[SUB * i - 1:SUB * i, :]
                k_t = jnp.where(row_c < SUB * i, k_c * jnp.exp(jnp.minimum(r_i - b, 0.0)), 0.0)
                k_parts.append(k_t)
                k_parts.append(jnp.zeros((128 - chunk, GLA_K), F32))
            k_cat = jnp.concatenate(k_parts, axis=0).astype(BF16)
            r = _dot_nt(head_stack(q_t), k_cat)
            a_off = r[:, 0:128]
            for i in range(2, n_sub):
                a_off = jnp.where(blk_s == i, r[:, (i - 1) * 128:i * 128], a_off)

        r_b = jnp.concatenate(
            [jnp.broadcast_to(b[SUB * i + DIAG - 1:SUB * i + DIAG, :], (SUB, GLA_K))
             for i in range(n_sub)], axis=0)
        q_b = jnp.where(second_half, q_c * jnp.exp(jnp.minimum(b - r_b, 0.0)), 0.0)
        k_b = jnp.where(second_half, 0.0, k_c * jnp.exp(jnp.minimum(r_b - b, 0.0)))
        k_b = jnp.concatenate([k_b, jnp.zeros((128 - chunk, GLA_K), F32)], axis=0).astype(BF16)
        a_half = jnp.where(same_blk, _dot_nt(head_stack(q_b), k_b), 0.0)
        a_off = a_half if a_off is None else a_off + a_half

        q3 = q_c.reshape(chunk // DIAG, DIAG, GLA_K)
        k3 = k_c.reshape(chunk // DIAG, DIAG, GLA_K)
        b3 = b.reshape(chunk // DIAG, DIAG, GLA_K)
        ps = []
        for d in range(DIAG):
            k_s = k3 if d == 0 else pltpu.roll(k3, d, axis=1)
            b_s = b3 if d == 0 else pltpu.roll(b3, d, axis=1)
            p = jnp.where(row_g >= d, q3 * k_s * jnp.exp(jnp.minimum(b3 - b_s, 0.0)), 0.0)
            ps.append(p.reshape(chunk, GLA_K).astype(BF16))
        w_all = _dot(jnp.concatenate(ps, axis=0), e2)
        a_d = jnp.zeros((chunk, GLA_V), F32)
        for d in range(DIAG):
            a_d = jnp.where(lane_v == row_c - d, w_all[d * chunk:(d + 1) * chunk, :], a_d)
        a_off = a_off + jnp.concatenate(
            [a_d[:, h * GLA_DV:(h + 1) * GLA_DV] for h in range(GLA_HEADS)], axis=0)
        o1 = _dot(a_off[:, 0:chunk].astype(BF16), v_b)
        o = o + unstack(o1, GLA_DV)

        ke = k_c * jnp.exp(b_last - b)
        r2 = _dot_tn(v_b, ke.astype(BF16))
        upd = jnp.zeros((GLA_DV, GLA_K), F32)
        for h in range(GLA_HEADS):
            upd = upd + jnp.where(lane_k == h, r2[h * GLA_DV:(h + 1) * GLA_DV, :], 0.0)
        st_ref[...] = st * jnp.exp(b_last) + upd

        g_c = gbuf[pl.ds(base, chunk), :]
        outs = []
        for h in range(GLA_HEADS):
            oh = o[:, h * GLA_DV:(h + 1) * GLA_DV]
            ms = jnp.mean(oh * oh, axis=-1, keepdims=True)
            outs.append(oh * lax.rsqrt(ms + LN_EPS))
        on = jnp.concatenate(outs, axis=1) * gng_ref[...]
        on = on * (g_c * _sigmoid(g_c))
        mixbuf[pl.ds(base, chunk), CONV_CH:MIX_W] = on.astype(BF16)
        return carry

    def trip(i, carry, extra_jobs=()):
        for j in range(max(per_trip, loop_blocks, len(extra_jobs))):
            if j < len(extra_jobs):
                extra_jobs[j](rows_first, tm - rows_first)
            if j < loop_blocks:
                r0 = (pre_blocks + i * loop_blocks + j) * cb
                conv_block(r0 if isinstance(r0, int) else pl.multiple_of(r0, cb))
            if j < per_trip:
                carry = chunk_step(i * per_trip + j, carry)
        return carry

    def merge_rows(r0, n):
        y = _dot(mixbuf[r0:r0 + n, :], w_o_ref[...])
        x1 = _layer_norm(DN_ALPHA * x_ref[0, r0:r0 + n, :] + y, ln1g_ref[...], ln1b_ref[...])
        x1_ref[r0:r0 + n, :] = x1
        for j, seg in enumerate(_pack_segments(x1)):
            x1p_ref[j, r0:r0 + n, :] = seg
        x_hi, x_lo = _split_bf16(x1)
        lg = _dot(x_hi, wr_ref[...])
        logit_ref[r0:r0 + n, :] = (lg[:, 0:128] + lg[:, 128:256] + _dot(x_lo, wr_ref[:, 0:128])
                                   + br_ref[...])

    if static_trips:
        trip(0, 0, tuple(proj_jobs))
        merge_rows(0, rows_first)
        trip(1, 0)
        merge_rows(rows_first, tm - rows_first)
    else:
        lax.fori_loop(0, n_trips, trip, 0)
        merge_rows(0, tm)

    tail = ubuf[tm:tm + HALO, :]
    ubuf[0:HALO, :] = tail

    @pl.when(t == nt - 1)
    def _conv_out():
        conv_out_ref[0] = tail[HALO - (CONV_W - 1):, :]

    @pl.when(t == nt - 1)
    def _gla_out():
        st = st_ref[...]
        for h in range(GLA_HEADS):
            gla_out_ref[0, h] = st[:, h * GLA_DK:(h + 1) * GLA_DK].T


def _mixer(x, conv_state, gla_state, p, *, tm, n_tok_total, tok_offset, token_bufs=None):
    bsz, seq, _ = x.shape
    chunk = min(CHUNK, seq)
    assert seq % tm == 0 and tm % chunk == 0 and chunk % SUB == 0 and tok_offset % tm == 0
    nt = seq // tm
    blk0 = tok_offset // tm
    full = lambda shape: pl.BlockSpec(shape, lambda b, t: (0,) * len(shape))
    alias = () if token_bufs is None else tuple(token_bufs)
    kern = functools.partial(_mixer_kernel, tm=tm, chunk=chunk, n_alias=len(alias))
    n_in = 16
    return pl.pallas_call(
        kern,
        grid=(bsz, nt),
        input_output_aliases={n_in + i: i for i in range(len(alias))},
        in_specs=[
            pl.BlockSpec((1, tm, D_MODEL), lambda b, t: (b, t, 0)),
            pl.BlockSpec((1, CONV_W - 1, CONV_CH), lambda b, t: (b, 0, 0)),
            pl.BlockSpec((1, GLA_HEADS, GLA_DK, GLA_DV), lambda b, t: (b, 0, 0, 0)),
            full((D_MODEL, IN_COLS_PAD)),
            full((8 * CONV_W, CONV_CH)),
            full((1, CONV_CH)),
            full((1, CONV_CH)),
            full((1, CONV_CH)),
            full((GATE_PAD, GLA_K)),
            full((1, GLA_K)),
            full((1, GLA_V)),
            full((MIX_W, D_MODEL)),
            full((1, D_MODEL)),
            full((1, D_MODEL)),
            full((D_MODEL, 256)),
            full((1, 128)),
        ] + [pl.BlockSpec(memory_space=pl.ANY)] * len(alias),
        out_specs=[
            pl.BlockSpec((N_SEG, tm, SC_SEG), lambda b, t: (0, blk0 + b * nt + t, 0)),
            pl.BlockSpec((tm, D_MODEL), lambda b, t: (blk0 + b * nt + t, 0)),
            pl.BlockSpec((tm, 128), lambda b, t: (blk0 + b * nt + t, 0)),
            pl.BlockSpec((1, CONV_W - 1, CONV_CH), lambda b, t: (b, 0, 0)),
            pl.BlockSpec((1, GLA_HEADS, GLA_DK, GLA_DV), lambda b, t: (b, 0, 0, 0)),
        ],
        out_shape=[
            jax.ShapeDtypeStruct((N_SEG, n_tok_total, SC_SEG), U32),
            jax.ShapeDtypeStruct((n_tok_total, D_MODEL), F32),
            jax.ShapeDtypeStruct((n_tok_total, 128), F32),
            jax.ShapeDtypeStruct((bsz, CONV_W - 1, CONV_CH), F32),
            jax.ShapeDtypeStruct((bsz, GLA_HEADS, GLA_DK, GLA_DV), F32),
        ],
        scratch_shapes=[
            pltpu.VMEM((HALO + tm, CONV_CH), F32),
            pltpu.VMEM((tm, GLA_K), F32),
            pltpu.VMEM((tm, GLA_K), F32),
            pltpu.VMEM((tm, GLA_K), F32),
            pltpu.VMEM((tm, GLA_V), F32),
            pltpu.VMEM((tm, GLA_V), F32),
            pltpu.VMEM((tm, MIX_W), BF16),
            pltpu.VMEM((GLA_DV, GLA_K), F32),
        ],
        compiler_params=pltpu.CompilerParams(
            dimension_semantics=("arbitrary", "arbitrary"),
            vmem_limit_bytes=VMEM_LIMIT),
        name="mixer",
    )(x, conv_state, gla_state, p["w_in"], p["w_dw"], p["b_dw"], p["ln_conv_g"], p["ln_conv_b"],
      p["w_gate"], p["b_gate"], p["gla_norm_g"], p["w_o"], p["ln1_g"], p["ln1_b"],
      p["w_router"], p["b_router"], *alias)


def _route_kernel(lg_ref, ri_ref, rg_ref, cnt_ref, carry_ref, *, tr):
    i = pl.program_id(0)

    @pl.when(i == 0)
    def _init():
        carry_ref[...] = jnp.zeros((N_EXPERTS, 128), F32)

    l = lg_ref[...].T[0:N_EXPERTS, :]
    row = lax.broadcasted_iota(jnp.int32, (N_EXPERTS, tr), 0)
    hots, vals, idxs = [], [], []
    for _ in range(TOP_K):
        m = jnp.max(l, axis=0, keepdims=True)
        ik = jnp.min(jnp.where(l == m, row, N_EXPERTS), axis=0, keepdims=True)
        hot = row == ik
        hots.append(hot)
        vals.append(m)
        idxs.append(ik)
        l = jnp.where(hot, -jnp.inf, l)
    es = [jnp.exp(v - vals[0]) for v in vals]
    denom = es[0] + es[1] + es[2] + es[3]
    member = jnp.zeros((N_EXPERTS, tr), F32)
    for hot in hots:
        member = member + hot.astype(F32)
    before = (lax.broadcasted_iota(jnp.int32, (tr, tr), 0)
              < lax.broadcasted_iota(jnp.int32, (tr, tr), 1)).astype(BF16)
    cum = _dot(member.astype(BF16), before) + carry_ref[:, 0:1]
    ranks = [jnp.sum(jnp.where(hot, cum, 0.0), axis=0, keepdims=True).astype(jnp.int32)
             for hot in hots]
    ri_ref[...] = jnp.concatenate(idxs + ranks, axis=0)
    gates = jnp.concatenate([e / denom for e in es] + [jnp.zeros((128 - TOP_K, tr), F32)], axis=0)
    rg_ref[...] = gates.T
    carry_ref[...] = carry_ref[...] + jnp.sum(member, axis=1, keepdims=True)

    @pl.when(i == pl.num_programs(0) - 1)
    def _fin():
        cnt_ref[...] = carry_ref[...]


def _route(logits, *, tr):
    n_tok = logits.shape[0]
    assert n_tok % tr == 0
    return pl.pallas_call(
        functools.partial(_route_kernel, tr=tr),
        grid=(n_tok // tr,),
        in_specs=[pl.BlockSpec((tr, 128), lambda i: (i, 0))],
        out_specs=[pl.BlockSpec((2 * TOP_K, tr), lambda i: (0, i)),
                   pl.BlockSpec((tr, 128), lambda i: (i, 0)),
                   pl.BlockSpec((N_EXPERTS, 128), lambda i: (0, 0))],
        out_shape=[jax.ShapeDtypeStruct((2 * TOP_K, n_tok), jnp.int32),
                   jax.ShapeDtypeStruct((n_tok, 128), F32),
                   jax.ShapeDtypeStruct((N_EXPERTS, 128), F32)],
        scratch_shapes=[pltpu.VMEM((N_EXPERTS, 128), F32)],
        compiler_params=pltpu.CompilerParams(dimension_semantics=("arbitrary",)),
        name="route",
    )(logits)


def _ffn_kernel(gs_ref, nu_ref, xs_hbm, wgu_ref, bgu_ref, wdn_ref, bdn_ref, yb_hbm,
                wgu_bf, wdn_bf, xin, yout, in_sem, out_sem):
    e = pl.program_id(0)
    n_used = nu_ref[0]
    g_first = gs_ref[e]
    g_end = gs_ref[e + 1]

    def in_copy(g, slot):
        rows = pl.ds(pl.multiple_of(g * EXPERT_BLOCK, EXPERT_BLOCK), EXPERT_BLOCK)
        return pltpu.make_async_copy(xs_hbm.at[:, rows, :], xin.at[slot], in_sem.at[slot])

    def out_copy(g, slot):
        rows = pl.ds(pl.multiple_of(g * EXPERT_BLOCK, EXPERT_BLOCK), EXPERT_BLOCK)
        return pltpu.make_async_copy(yout.at[slot], yb_hbm.at[:, rows, :], out_sem.at[slot])

    @pl.when(e == 0)
    def _prime():
        for g in range(IN_AHEAD):
            @pl.when(g < n_used)
            def _():
                in_copy(g, g).start()

    @pl.when(g_end > g_first)
    def _cast():
        wgu_bf[...] = wgu_ref[0].astype(BF16)
        wdn_bf[...] = wdn_ref[0].astype(BF16)

    def blocks(g0, n):
        for j in range(n):
            g = g0 + j
            in_copy(g, g % IN_RING).wait()
        for j in range(n):
            g = g0 + IN_AHEAD + j

            @pl.when(g < n_used)
            def _prefetch():
                in_copy(g, g % IN_RING).start()
        for j in range(n):
            g = g0 + j

            @pl.when(g >= OUT_RING)
            def _free_out():
                out_copy(g - OUT_RING, g % OUT_RING).wait()
        for j in range(n):
            g = g0 + j
            x_lo, x_hi = _unpack_segments([xin[g % IN_RING, s] for s in range(N_SEG)])
            hgu = (_dot(x_lo.astype(BF16), wgu_bf[0:HALF, :])
                   + _dot(x_hi.astype(BF16), wgu_bf[HALF:, :]) + bgu_ref[0])
            gate = jnp.minimum(hgu[:, :D_FF], SWIGLU_LIMIT)
            up = jnp.clip(hgu[:, D_FF:], -SWIGLU_LIMIT, SWIGLU_LIMIT)
            act = (up + 1.0) * gate * _sigmoid(SWIGLU_ALPHA * gate)
            yb = _dot(act.astype(BF16), wdn_bf[...]) + bdn_ref[0]
            for s, seg in enumerate(_pack_segments(yb)):
                yout[g % OUT_RING, s] = seg
        for j in range(n):
            g = g0 + j
            out_copy(g, g % OUT_RING).start()

    n_blocks = g_end - g_first
    n_full = n_blocks // TRIP_BLOCKS

    def full_trip(i, carry):
        blocks(g_first + TRIP_BLOCKS * i, TRIP_BLOCKS)
        return carry

    lax.fori_loop(0, n_full, full_trip, 0)
    done = n_full * TRIP_BLOCKS
    size = TRIP_BLOCKS // 2
    while size >= 1:
        @pl.when((n_blocks // size) % 2 == 1)
        def _rest(done=done, size=size):
            blocks(g_first + done, size)
        done = done + jnp.where((n_blocks // size) % 2 == 1, size, 0)
        size //= 2

    @pl.when(e == pl.num_programs(0) - 1)
    def _drain():
        for back in range(OUT_RING, 0, -1):
            @pl.when(n_used >= back)
            def _():
                out_copy(n_used - back, (n_used - back) % OUT_RING).wait()


def _ffn(block_start, n_used, xs, w_gu, b_gu, w_down, b_down):
    n_rows = xs.shape[1]
    grid_spec = pltpu.PrefetchScalarGridSpec(
        num_scalar_prefetch=2,
        grid=(N_EXPERTS,),
        in_specs=[
            pl.BlockSpec(memory_space=pl.ANY),
            pl.BlockSpec((1, D_MODEL, 2 * D_FF), lambda e, gs, nu: (e, 0, 0)),
            pl.BlockSpec((1, 1, 2 * D_FF), lambda e, gs, nu: (e, 0, 0)),
            pl.BlockSpec((1, D_FF, D_MODEL), lambda e, gs, nu: (e, 0, 0)),
            pl.BlockSpec((1, 1, D_MODEL), lambda e, gs, nu: (e, 0, 0)),
        ],
        out_specs=pl.BlockSpec(memory_space=pl.ANY),
        scratch_shapes=[pltpu.VMEM((D_MODEL, 2 * D_FF), BF16),
                        pltpu.VMEM((D_FF, D_MODEL), BF16),
                        pltpu.VMEM((IN_RING, N_SEG, EXPERT_BLOCK, SC_SEG), U32),
                        pltpu.VMEM((OUT_RING, N_SEG, EXPERT_BLOCK, SC_SEG), U32),
                        pltpu.SemaphoreType.DMA((IN_RING,)),
                        pltpu.SemaphoreType.DMA((OUT_RING,))],
    )
    return pl.pallas_call(
        _ffn_kernel,
        grid_spec=grid_spec,
        out_shape=jax.ShapeDtypeStruct((N_SEG, n_rows, SC_SEG), U32),
        compiler_params=pltpu.CompilerParams(
            dimension_semantics=("arbitrary",), vmem_limit_bytes=VMEM_LIMIT),
        name="expert_ffn",
    )(block_start, n_used, xs, w_gu, b_gu.reshape(N_EXPERTS, 1, 2 * D_FF), w_down,
      b_down.reshape(N_EXPERTS, 1, D_MODEL))


def _combine_kernel(x1_ref, g_ref, rg_ref, ln2g_ref, ln2b_ref, *rest, blk0, n_prompt_blocks,
                    n_alias, with_sample):
    yp_ref = rest[n_alias]
    i = blk0 + pl.program_id(0)
    m_lo = jnp.zeros((x1_ref.shape[0], HALF), F32)
    m_hi = jnp.zeros((x1_ref.shape[0], HALF), F32)
    for k in range(TOP_K):
        g_lo, g_hi = _unpack_segments([g_ref[k, j] for j in range(N_SEG)])
        gate = rg_ref[:, k:k + 1]
        m_lo = m_lo + gate * g_lo
        m_hi = m_hi + gate * g_hi
    z = DN_ALPHA * x1_ref[...] + jnp.concatenate([m_lo, m_hi], axis=1)
    y = _layer_norm(z, ln2g_ref[...], ln2b_ref[...])

    if with_sample:
        ys_ref = rest[n_alias + 1]

        @pl.when(i < n_prompt_blocks)
        def _p():
            yp_ref[...] = y

        @pl.when(i >= n_prompt_blocks)
        def _s():
            ys_ref[...] = y
    else:
        yp_ref[...] = y


def _combine(x1, g, rg, ln2_g, ln2_b, *, tok0, n_prompt, tf, yp_buf=None):
    n_tok = x1.shape[0]
    piece = g.shape[2]
    n_sample = n_tok - n_prompt
    assert n_prompt % tf == 0 and n_sample % tf == 0 and tok0 % tf == 0 and piece % tf == 0
    npb = n_prompt // tf
    blk0 = tok0 // tf
    with_sample = tok0 + piece > n_prompt
    assert not with_sample or tok0 + piece == n_tok
    alias = () if yp_buf is None else (yp_buf,)
    out_specs = [pl.BlockSpec((tf, D_MODEL), lambda i: (jnp.minimum(blk0 + i, npb - 1), 0))]
    out_shape = [jax.ShapeDtypeStruct((n_prompt, D_MODEL), F32)]
    if with_sample:
        out_specs.append(pl.BlockSpec((tf, D_MODEL), lambda i: (jnp.maximum(blk0 + i - npb, 0), 0)))
        out_shape.append(jax.ShapeDtypeStruct((n_sample, D_MODEL), F32))
    n_in = 5
    return pl.pallas_call(
        functools.partial(_combine_kernel, blk0=blk0, n_prompt_blocks=npb, n_alias=len(alias),
                          with_sample=with_sample),
        grid=(piece // tf,),
        input_output_aliases={n_in + i: i for i in range(len(alias))},
        in_specs=[pl.BlockSpec((tf, D_MODEL), lambda i: (blk0 + i, 0)),
                  pl.BlockSpec((TOP_K, N_SEG, tf, SC_SEG), lambda i: (0, 0, i, 0)),
                  pl.BlockSpec((tf, 128), lambda i: (blk0 + i, 0)),
                  pl.BlockSpec((1, D_MODEL), lambda i: (0, 0)),
                  pl.BlockSpec((1, D_MODEL), lambda i: (0, 0))]
        + [pl.BlockSpec(memory_space=pl.ANY)] * len(alias),
        out_specs=out_specs,
        out_shape=out_shape,
        compiler_params=pltpu.CompilerParams(
            dimension_semantics=("arbitrary",), vmem_limit_bytes=VMEM_LIMIT),
        name="combine",
    )(x1, g, rg, ln2_g, ln2_b, *alias)


def _sc_mesh():
    return plsc.VectorSubcoreMesh(core_axis_name="c", subcore_axis_name="s")


def _sc_scatter_rows(src, dest, n_out):
    n_src, width = src.shape
    n_k = dest.shape[0]
    assert width == SC_SEG and n_src % SC_WINDOW == 0 and dest.shape[1] == n_src

    @pl.kernel(out_type=jax.ShapeDtypeStruct((n_out, width), src.dtype), mesh=_sc_mesh(),
               scratch_types=[])
    def scatter_kernel(src_hbm, idx_hbm, out_hbm):
        def body(src_vmem, idx_vmem):
            for k in range(n_k):
                pltpu.sync_copy(src_vmem, out_hbm.at[idx_vmem.at[k]])

        pltpu.emit_pipeline(
            body, grid=(n_src // SC_WINDOW,),
            in_specs=[pl.BlockSpec((SC_WINDOW, width), lambda i: (i, 0)),
                      pl.BlockSpec((n_k, SC_WINDOW), lambda i: (0, i))],
            out_specs=[],
            core_axis_name=("c", "s"),
            dimension_semantics=(pltpu.PARALLEL,),
        )(src_hbm, idx_hbm)

    return scatter_kernel(src, dest)


def _sc_gather_rows(src, idx):
    n_out = idx.shape[1]
    width = src.shape[1]
    assert width == SC_SEG and n_out % SC_WINDOW == 0

    @pl.kernel(out_type=jax.ShapeDtypeStruct((n_out, width), src.dtype), mesh=_sc_mesh(),
               scratch_types=[])
    def gather_kernel(src_hbm, idx_hbm, out_hbm):
        def body(idx_vmem, out_vmem):
            pltpu.sync_copy(src_hbm.at[idx_vmem.at[0]], out_vmem)

        pltpu.emit_pipeline(
            body, grid=(n_out // SC_WINDOW,),
            in_specs=[pl.BlockSpec((1, SC_WINDOW), lambda i: (0, i))],
            out_specs=[pl.BlockSpec((SC_WINDOW, width), lambda i: (i, 0))],
            core_axis_name=("c", "s"),
            dimension_semantics=(pltpu.PARALLEL,),
        )(idx_hbm, out_hbm)

    return gather_kernel(src, idx)


def _prep_params(w_in, w_dw, b_dw, ln_conv_g, ln_conv_b, w_gate_lr, b_gate, gla_norm_g, w_o,
                 ln1_g, ln1_b, w_router, b_router):
    row = lambda v: v.reshape(1, -1).astype(F32)
    w_in_p = jnp.pad(w_in, ((0, 0), (0, IN_COLS_PAD - w_in.shape[1]))).astype(BF16)
    w_gate = jnp.pad(w_gate_lr, ((0, GATE_PAD - GATE_RANK), (0, 0))).astype(BF16)
    wr_hi = w_router.astype(BF16)
    wr_lo = (w_router - wr_hi.astype(F32)).astype(BF16)
    padr = lambda m: jnp.pad(m, ((0, 0), (0, 128 - N_EXPERTS)))
    return {
        "w_in": w_in_p, "w_dw": jnp.repeat(w_dw, 8, axis=0), "b_dw": row(b_dw), "ln_conv_g": row(ln_conv_g),
        "ln_conv_b": row(ln_conv_b), "w_gate": w_gate, "b_gate": row(b_gate),
        "gla_norm_g": row(gla_norm_g), "w_o": w_o.astype(BF16), "ln1_g": row(ln1_g),
        "ln1_b": row(ln1_b), "w_router": jnp.concatenate([padr(wr_hi), padr(wr_lo)], axis=1),
        "b_router": jnp.pad(row(b_router), ((0, 0), (0, 128 - N_EXPERTS))),
    }


def _dest_kernel(ps_ref, ri_ref, out_ref, *, n_slot):
    idx = ri_ref[0:TOP_K, :]
    dest = ri_ref[TOP_K:2 * TOP_K, :] + pl.program_id(0) * n_slot
    for e in range(N_EXPERTS):
        dest = dest + jnp.where(idx == e, ps_ref[e], 0)
    out_ref[...] = dest


def _moe_layout(ri, counts, *, tr):
    n_tok = ri.shape[1]
    cnt = counts[:, 0].astype(jnp.int32)
    padded = (cnt + EXPERT_BLOCK - 1) // EXPERT_BLOCK * EXPERT_BLOCK
    pad_end = jnp.cumsum(padded)
    pad_start = pad_end - padded
    nb = -(-(n_tok * TOP_K) // EXPERT_BLOCK) + N_EXPERTS
    n_t = n_tok // tr
    dest_seg = pl.pallas_call(
        functools.partial(_dest_kernel, n_slot=nb * EXPERT_BLOCK),
        grid_spec=pltpu.PrefetchScalarGridSpec(
            num_scalar_prefetch=1,
            grid=(N_SEG, n_t),
            in_specs=[pl.BlockSpec((2 * TOP_K, tr), lambda j, i, ps: (0, i))],
            out_specs=pl.BlockSpec((TOP_K, tr), lambda j, i, ps: (0, j * n_t + i)),
        ),
        out_shape=jax.ShapeDtypeStruct((TOP_K, N_SEG * n_tok), jnp.int32),
        name="slot_index",
    )(pad_start.astype(jnp.int32), ri)
    block_start = (jnp.concatenate([pad_start, pad_end[-1:]]) // EXPERT_BLOCK).astype(jnp.int32)
    n_used = block_start[-1:]
    return dest_seg, block_start, n_used, nb


def kernel(x_prompt, x_sample, state_conv, state_gla, w_in, w_dw, b_dw, ln_conv_g, ln_conv_b,
           w_gate_lr, b_gate, gla_norm_g, w_o, ln1_g, ln1_b, w_router, b_router, w_gu, b_gu,
           w_down, b_down, ln2_g, ln2_b):
    assert w_in.shape[0] == DEPTH
    l = 0
    p = _prep_params(w_in[l], w_dw[l], b_dw[l], ln_conv_g[l], ln_conv_b[l], w_gate_lr[l],
                     b_gate[l], gla_norm_g[l], w_o[l], ln1_g[l], ln1_b[l], w_router[l],
                     b_router[l])
    bp, tp, _ = x_prompt.shape
    bs, ts, _ = x_sample.shape
    zc = jnp.zeros((bp, CONV_W - 1, CONV_CH), F32)
    zs = jnp.zeros((bp, GLA_HEADS, GLA_DK, GLA_DV), F32)
    n_prompt = bp * tp
    n_tok = n_prompt + bs * ts
    x1p, x1, lg, conv_p, gla_p = _mixer(x_prompt, zc, zs, p, tm=512, n_tok_total=n_tok,
                                        tok_offset=0)
    x1p, x1, lg, conv_s, gla_s = _mixer(x_sample, state_conv[l], state_gla[l], p, tm=ts,
                                        n_tok_total=n_tok, tok_offset=n_prompt,
                                        token_bufs=(x1p, x1, lg))

    tr = 768 if n_tok % 768 == 0 else 256
    ri, rg, counts = _route(lg, tr=tr)
    dest_seg, block_start, n_used, nb = _moe_layout(ri, counts, tr=n_tok)
    n_slot = nb * EXPERT_BLOCK
    xs = _sc_scatter_rows(x1p.reshape(N_SEG * n_tok, SC_SEG), dest_seg, N_SEG * n_slot)
    yb = _ffn(block_start, n_used, xs.reshape(N_SEG, n_slot, SC_SEG), w_gu[l], b_gu[l], w_down[l],
              b_down[l])
    tf = 256
    piece = -(-n_prompt // (COMBINE_PIECES * tf)) * tf
    bounds = [min(p * piece, n_prompt) for p in range(COMBINE_PIECES)] + [n_tok]
    bounds = sorted(set(bounds))
    dest_kjt = dest_seg.reshape(TOP_K, N_SEG, n_tok)
    yp = None
    for t0, t1 in zip(bounds[:-1], bounds[1:]):
        g = _sc_gather_rows(yb.reshape(N_SEG * n_slot, SC_SEG),
                            dest_kjt[:, :, t0:t1].reshape(1, -1))
        outs = _combine(x1, g.reshape(TOP_K, N_SEG, t1 - t0, SC_SEG), rg,
                        ln2_g[l].reshape(1, -1), ln2_b[l].reshape(1, -1), tok0=t0,
                        n_prompt=n_prompt, tf=tf, yp_buf=yp)
        yp = outs[0]
    ys = outs[1]
    return (yp.reshape(bp, tp, D_MODEL), ys.reshape(bs, ts, D_MODEL),
            conv_p[None], gla_p[None], conv_s[None], gla_s[None])
```

```python
import functools

import jax
import jax.numpy as jnp
from jax import lax
from jax.experimental import pallas as pl
from jax.experimental.pallas import tpu as pltpu
from jax.experimental.pallas import tpu_sc as plsc

F32 = jnp.float32
BF16 = jnp.bfloat16

D_MODEL = 1024
CHUNK = 64
SUB = 16
DIAG = 8
CONV_CH = 512
CONV_W = 31
HALO = 32
CONV_BLOCK = 64
LAST_PIECE_BLOCKS = 4
COMBINE_PIECES = 8
TRIP_BLOCKS = 4
IN_AHEAD = TRIP_BLOCKS
IN_RING = 2 * TRIP_BLOCKS
OUT_RING = 2 * TRIP_BLOCKS
GLA_HEADS = 4
GLA_DK = 64
GLA_DV = 128
GLA_K = GLA_HEADS * GLA_DK
GLA_V = GLA_HEADS * GLA_DV
GATE_RANK = 16
GATE_PAD = 128
GATE_NORM = 16.0
MIX_W = CONV_CH + GLA_V
OFF_UV = 0
OFF_UG = OFF_UV + CONV_CH
OFF_Q = OFF_UG + CONV_CH
OFF_K = OFF_Q + GLA_K
OFF_V = OFF_K + GLA_K
OFF_G = OFF_V + GLA_V
OFF_A = OFF_G + GLA_V
IN_COLS_PAD = OFF_A + GATE_PAD
N_EXPERTS = 32
TOP_K = 4
D_FF = 1024
SWIGLU_LIMIT = 7.0
SWIGLU_ALPHA = 1.702
EXPERT_BLOCK = 256
LN_EPS = 1e-5
DEPTH = 1
DN_ALPHA = (2 * DEPTH) ** 0.25
VMEM_LIMIT = 56 * 1024 * 1024
HALF = D_MODEL // 2
SC_SEG = 256
SC_WINDOW = 128
N_SEG = HALF // SC_SEG
U32 = jnp.uint32


def _dot(a, b):
    return jnp.dot(a, b, preferred_element_type=F32)


def _dot_nt(a, b):
    return lax.dot_general(a, b, (((1,), (1,)), ((), ())), preferred_element_type=F32)


def _dot_tn(a, b):
    return lax.dot_general(a, b, (((0,), (0,)), ((), ())), preferred_element_type=F32)


def _layer_norm(x, g, b):
    mu = jnp.mean(x, axis=-1, keepdims=True)
    xc = x - mu
    var = jnp.mean(xc * xc, axis=-1, keepdims=True)
    return xc * lax.rsqrt(var + LN_EPS) * g + b


def _sigmoid(x):
    return 1.0 / (1.0 + jnp.exp(-x))


def _split_bf16(x):
    hi = x.astype(BF16)
    lo = (x - hi.astype(F32)).astype(BF16)
    return hi, lo


def _pack_segments(x):
    bits = pltpu.bitcast(x.astype(BF16).astype(F32), U32)
    word = (bits[:, :HALF] >> 16) | bits[:, HALF:]
    return [word[:, j * SC_SEG:(j + 1) * SC_SEG] for j in range(N_SEG)]


def _unpack_segments(segs):
    word = jnp.concatenate(segs, axis=1)
    lo = pltpu.bitcast(word << 16, F32)
    hi = pltpu.bitcast(word & jnp.uint32(0xFFFF0000), F32)
    return lo, hi


def _mixer_kernel(x_ref, cs_ref, gs_ref, w_in_ref, w_dw_ref, b_dw_ref, lncg_ref, lncb_ref,
                  wg_ref, bg_ref, gng_ref, w_o_ref, ln1g_ref, ln1b_ref, wr_ref, br_ref, *rest,
                  tm, chunk, n_alias):
    (x1p_ref, x1_ref, logit_ref, conv_out_ref, gla_out_ref,
     ubuf, qbuf, kbuf, labuf, vbuf, gbuf, mixbuf, st_ref) = rest[n_alias:]
    t = pl.program_id(1)
    nt = pl.num_programs(1)
    n_chunks = tm // chunk
    n_sub = chunk // SUB

    @pl.when(t == 0)
    def _init():
        ubuf[0:HALO - (CONV_W - 1), :] = jnp.zeros((HALO - (CONV_W - 1), CONV_CH), F32)
        ubuf[HALO - (CONV_W - 1):HALO, :] = cs_ref[0]
        for h in range(GLA_HEADS):
            st_ref[:, h * GLA_DK:(h + 1) * GLA_DK] = gs_ref[0, h].T

    x = x_ref[0]
    xb = x.astype(BF16)

    hv = _dot(xb, w_in_ref[:, OFF_UV:OFF_UG])
    hg = _dot(xb, w_in_ref[:, OFF_UG:OFF_Q])
    u = hv * _sigmoid(hg)
    ubuf[HALO:HALO + tm, :] = u
    lead = HALO - (CONV_W - 1)

    def proj_q(r0, n):
        qbuf[r0:r0 + n, :] = _dot(xb[r0:r0 + n], w_in_ref[:, OFF_Q:OFF_K]) * (GLA_DK ** -0.5)

    def proj_k(r0, n):
        kbuf[r0:r0 + n, :] = _dot(xb[r0:r0 + n], w_in_ref[:, OFF_K:OFF_V])

    def proj_v(half, r0, n):
        lo = half * (GLA_V // 2)
        vbuf[r0:r0 + n, lo:lo + GLA_V // 2] = _dot(
            xb[r0:r0 + n], w_in_ref[:, OFF_V + lo:OFF_V + lo + GLA_V // 2])

    def proj_g(half, r0, n):
        lo = half * (GLA_V // 2)
        gbuf[r0:r0 + n, lo:lo + GLA_V // 2] = _dot(
            xb[r0:r0 + n], w_in_ref[:, OFF_G + lo:OFF_G + lo + GLA_V // 2])

    def proj_gate(r0, n):
        ha = _dot(xb[r0:r0 + n], w_in_ref[:, OFF_A:IN_COLS_PAD])
        a = _dot(ha.astype(BF16), wg_ref[...]) + bg_ref[...]
        labuf[r0:r0 + n, :] = ((jnp.minimum(a, 0.0) - jnp.log(1.0 + jnp.exp(-jnp.abs(a))))
                               * (1.0 / GATE_NORM))

    proj_jobs = [proj_q, proj_k, functools.partial(proj_v, 0), functools.partial(proj_v, 1),
                 functools.partial(proj_g, 0), functools.partial(proj_g, 1), proj_gate]

    cb = min(CONV_BLOCK, tm)

    def conv_block(r0):
        acc = jnp.zeros((cb, CONV_CH), F32) + b_dw_ref[...]
        for r in range(8):
            rows = cb if r == 0 else cb + 8
            part = None
            for s in range(r, lead + CONV_W, 8):
                if s < lead:
                    continue
                w8 = w_dw_ref[8 * (s - lead):8 * (s - lead) + 8, :]
                u3 = ubuf[pl.ds(r0 + (s - r), rows), :].reshape(rows // 8, 8, CONV_CH)
                term = u3 * w8
                part = term if part is None else part + term
            acc = acc + part.reshape(rows, CONV_CH)[r:r + cb, :]
        cact = _layer_norm(acc, lncg_ref[...], lncb_ref[...])
        cact = cact * _sigmoid(cact)
        mixbuf[pl.ds(r0, cb), 0:CONV_CH] = cact.astype(BF16)

    per_trip = 4 if n_chunks % 4 == 0 else 1
    n_trips = n_chunks // per_trip
    loop_blocks = (tm // 2) // (cb * n_trips) if n_chunks >= 4 else 0
    pre_blocks = tm // cb - loop_blocks * n_trips
    static_trips = n_trips == 2
    rows_first = tm // 2 if static_trips else tm
    for blk in range(max(pre_blocks, len(proj_jobs))):
        if blk < len(proj_jobs):
            proj_jobs[blk](0, rows_first)
        if blk < pre_blocks:
            conv_block(blk * cb)

    lane_k = lax.broadcasted_iota(jnp.int32, (1, GLA_K), 1) // GLA_DK
    row_c = lax.broadcasted_iota(jnp.int32, (chunk, 1), 0)
    tri = (lax.broadcasted_iota(jnp.int32, (chunk, chunk), 0)
           >= lax.broadcasted_iota(jnp.int32, (chunk, chunk), 1)).astype(BF16)
    e2 = (lax.broadcasted_iota(jnp.int32, (GLA_K, GLA_V), 0) // GLA_DK
          == lax.broadcasted_iota(jnp.int32, (GLA_K, GLA_V), 1) // GLA_DV).astype(BF16)
    row_s = lax.broadcasted_iota(jnp.int32, (GLA_HEADS * chunk, 1), 0)
    blk_s = (row_s % chunk) // SUB
    same_blk = blk_s == lax.broadcasted_iota(jnp.int32, (1, 128), 1) // SUB
    second_half = row_c % SUB >= DIAG
    row_g = lax.broadcasted_iota(jnp.int32, (1, DIAG, 1), 1)
    lane_v = lax.broadcasted_iota(jnp.int32, (1, GLA_V), 1) % GLA_DV

    def head_stack(m):
        return jnp.concatenate(
            [jnp.where(lane_k == h, m, 0.0) for h in range(GLA_HEADS)], axis=0).astype(BF16)

    def unstack(r, width):
        return jnp.concatenate(
            [r[h * chunk:(h + 1) * chunk, h * width:(h + 1) * width] for h in range(GLA_HEADS)],
            axis=1)

    def chunk_step(c, carry):
        base = c * chunk if isinstance(c, int) else pl.multiple_of(c * chunk, chunk)
        q_c = qbuf[pl.ds(base, chunk), :]
        k_c = kbuf[pl.ds(base, chunk), :]
        v_c = vbuf[pl.ds(base, chunk), :]
        v_b = v_c.astype(BF16)
        la_c = labuf[pl.ds(base, chunk), :]
        la_hi, la_lo = _split_bf16(la_c)
        b = _dot(tri, la_hi) + _dot(tri, la_lo)
        b_last = b[chunk - 1:chunk, :]
        st = st_ref[...]

        qe = q_c * jnp.exp(b)
        o2 = _dot_nt(head_stack(qe), st.astype(BF16))
        o = jnp.concatenate([o2[h * chunk:(h + 1) * chunk, :] for h in range(GLA_HEADS)], axis=1)

        a_off = None
        if n_sub > 1:
            r_rows = [b[0:SUB, :]] + [jnp.broadcast_to(b[SUB * i - 1:SUB * i, :], (SUB, GLA_K))
                                      for i in range(1, n_sub)]
            r_q = jnp.concatenate(r_rows, axis=0)
            q_t = jnp.where(row_c >= SUB, q_c * jnp.exp(jnp.minimum(b - r_q, 0.0)), 0.0)
            k_parts = []
            for i in range(1, n_sub):
                r_i = b[SUB * i - 1:SUB * i, :]
                k_t = jnp.where(row_c < SUB * i, k_c * jnp.exp(jnp.minimum(r_i - b, 0.0)), 0.0)
                k_parts.append(k_t)
                k_parts.append(jnp.zeros((128 - chunk, GLA_K), F32))
            k_cat = jnp.concatenate(k_parts, axis=0).astype(BF16)
            r = _dot_nt(head_stack(q_t), k_cat)
            a_off = r[:, 0:128]
            for i in range(2, n_sub):
                a_off = jnp.where(blk_s == i, r[:, (i - 1) * 128:i * 128], a_off)

        r_b = jnp.concatenate(
            [jnp.broadcast_to(b[SUB * i + DIAG - 1:SUB * i + DIAG, :], (SUB, GLA_K))
             for i in range(n_sub)], axis=0)
        q_b = jnp.where(second_half, q_c * jnp.exp(jnp.minimum(b - r_b, 0.0)), 0.0)
        k_b = jnp.where(second_half, 0.0, k_c * jnp.exp(jnp.minimum(r_b - b, 0.0)))
        k_b = jnp.concatenate([k_b, jnp.zeros((128 - chunk, GLA_K), F32)], axis=0).astype(BF16)
        a_half = jnp.where(same_blk, _dot_nt(head_stack(q_b), k_b), 0.0)
        a_off = a_half if a_off is None else a_off + a_half

        q3 = q_c.reshape(chunk // DIAG, DIAG, GLA_K)
        k3 = k_c.reshape(chunk // DIAG, DIAG, GLA_K)
        b3 = b.reshape(chunk // DIAG, DIAG, GLA_K)
        ps = []
        for d in range(DIAG):
            k_s = k3 if d == 0 else pltpu.roll(k3, d, axis=1)
            b_s = b3 if d == 0 else pltpu.roll(b3, d, axis=1)
            p = jnp.where(row_g >= d, q3 * k_s * jnp.exp(jnp.minimum(b3 - b_s, 0.0)), 0.0)
            ps.append(p.reshape(chunk, GLA_K).astype(BF16))
        w_all = _dot(jnp.concatenate(ps, axis=0), e2)
        a_d = jnp.zeros((chunk, GLA_V), F32)
        for d in range(DIAG):
            a_d = jnp.where(lane_v == row_c - d, w_all[d * chunk:(d + 1) * chunk, :], a_d)
        a_off = a_off + jnp.concatenate(
            [a_d[:, h * GLA_DV:(h + 1) * GLA_DV] for h in range(GLA_HEADS)], axis=0)
        o1 = _dot(a_off[:, 0:chunk].astype(BF16), v_b)
        o = o + unstack(o1, GLA_DV)

        ke = k_c * jnp.exp(b_last - b)
        r2 = _dot_tn(v_b, ke.astype(BF16))
        upd = jnp.zeros((GLA_DV, GLA_K), F32)
        for h in range(GLA_HEADS):
            upd = upd + jnp.where(lane_k == h, r2[h * GLA_DV:(h + 1) * GLA_DV, :], 0.0)
        st_ref[...] = st * jnp.exp(b_last) + upd

        g_c = gbuf[pl.ds(base, chunk), :]
        outs = []
        for h in range(GLA_HEADS):
            oh = o[:, h * GLA_DV:(h + 1) * GLA_DV]
            ms = jnp.mean(oh * oh, axis=-1, keepdims=True)
            outs.append(oh * lax.rsqrt(ms + LN_EPS))
        on = jnp.concatenate(outs, axis=1) * gng_ref[...]
        on = on * (g_c * _sigmoid(g_c))
        mixbuf[pl.ds(base, chunk), CONV_CH:MIX_W] = on.astype(BF16)
        return carry

    def trip(i, carry, extra_jobs=()):
        for j in range(max(per_trip, loop_blocks, len(extra_jobs))):
            if j < len(extra_jobs):
                extra_jobs[j](rows_first, tm - rows_first)
            if j < loop_blocks:
                r0 = (pre_blocks + i * loop_blocks + j) * cb
                conv_block(r0 if isinstance(r0, int) else pl.multiple_of(r0, cb))
            if j < per_trip:
                carry = chunk_step(i * per_trip + j, carry)
        return carry

    def merge_rows(r0, n):
        y = _dot(mixbuf[r0:r0 + n, :], w_o_ref[...])
        x1 = _layer_norm(DN_ALPHA * x_ref[0, r0:r0 + n, :] + y, ln1g_ref[...], ln1b_ref[...])
        x1_ref[r0:r0 + n, :] = x1
        for j, seg in enumerate(_pack_segments(x1)):
            x1p_ref[j, r0:r0 + n, :] = seg
        x_hi, x_lo = _split_bf16(x1)
        lg = _dot(x_hi, wr_ref[...])
        logit_ref[r0:r0 + n, :] = (lg[:, 0:128] + lg[:, 128:256] + _dot(x_lo, wr_ref[:, 0:128])
                                   + br_ref[...])

    if static_trips:
        trip(0, 0, tuple(proj_jobs))
        merge_rows(0, rows_first)
        trip(1, 0)
        merge_rows(rows_first, tm - rows_first)
    else:
        lax.fori_loop(0, n_trips, trip, 0)
        merge_rows(0, tm)

    tail = ubuf[tm:tm + HALO, :]
    ubuf[0:HALO, :] = tail

    @pl.when(t == nt - 1)
    def _conv_out():
        conv_out_ref[0] = tail[HALO - (CONV_W - 1):, :]

    @pl.when(t == nt - 1)
    def _gla_out():
        st = st_ref[...]
        for h in range(GLA_HEADS):
            gla_out_ref[0, h] = st[:, h * GLA_DK:(h + 1) * GLA_DK].T


def _mixer(x, conv_state, gla_state, p, *, tm, n_tok_total, tok_offset, token_bufs=None):
    bsz, seq, _ = x.shape
    chunk = min(CHUNK, seq)
    assert seq % tm == 0 and tm % chunk == 0 and chunk % SUB == 0 and tok_offset % tm == 0
    nt = seq // tm
    blk0 = tok_offset // tm
    full = lambda shape: pl.BlockSpec(shape, lambda b, t: (0,) * len(shape))
    alias = () if token_bufs is None else tuple(token_bufs)
    kern = functools.partial(_mixer_kernel, tm=tm, chunk=chunk, n_alias=len(alias))
    n_in = 16
    return pl.pallas_call(
        kern,
        grid=(bsz, nt),
        input_output_aliases={n_in + i: i for i in range(len(alias))},
        in_specs=[
            pl.BlockSpec((1, tm, D_MODEL), lambda b, t: (b, t, 0)),
            pl.BlockSpec((1, CONV_W - 1, CONV_CH), lambda b, t: (b, 0, 0)),
            pl.BlockSpec((1, GLA_HEADS, GLA_DK, GLA_DV), lambda b, t: (b, 0, 0, 0)),
            full((D_MODEL, IN_COLS_PAD)),
            full((8 * CONV_W, CONV_CH)),
            full((1, CONV_CH)),
            full((1, CONV_CH)),
            full((1, CONV_CH)),
            full((GATE_PAD, GLA_K)),
            full((1, GLA_K)),
            full((1, GLA_V)),
            full((MIX_W, D_MODEL)),
            full((1, D_MODEL)),
            full((1, D_MODEL)),
            full((D_MODEL, 256)),
            full((1, 128)),
        ] + [pl.BlockSpec(memory_space=pl.ANY)] * len(alias),
        out_specs=[
            pl.BlockSpec((N_SEG, tm, SC_SEG), lambda b, t: (0, blk0 + b * nt + t, 0)),
            pl.BlockSpec((tm, D_MODEL), lambda b, t: (blk0 + b * nt + t, 0)),
            pl.BlockSpec((tm, 128), lambda b, t: (blk0 + b * nt + t, 0)),
            pl.BlockSpec((1, CONV_W - 1, CONV_CH), lambda b, t: (b, 0, 0)),
            pl.BlockSpec((1, GLA_HEADS, GLA_DK, GLA_DV), lambda b, t: (b, 0, 0, 0)),
        ],
        out_shape=[
            jax.ShapeDtypeStruct((N_SEG, n_tok_total, SC_SEG), U32),
            jax.ShapeDtypeStruct((n_tok_total, D_MODEL), F32),
            jax.ShapeDtypeStruct((n_tok_total, 128), F32),
            jax.ShapeDtypeStruct((bsz, CONV_W - 1, CONV_CH), F32),
            jax.ShapeDtypeStruct((bsz, GLA_HEADS, GLA_DK, GLA_DV), F32),
        ],
        scratch_shapes=[
            pltpu.VMEM((HALO + tm, CONV_CH), F32),
            pltpu.VMEM((tm, GLA_K), F32),
            pltpu.VMEM((tm, GLA_K), F32),
            pltpu.VMEM((tm, GLA_K), F32),
            pltpu.VMEM((tm, GLA_V), F32),
            pltpu.VMEM((tm, GLA_V), F32),
            pltpu.VMEM((tm, MIX_W), BF16),
            pltpu.VMEM((GLA_DV, GLA_K), F32),
        ],
        compiler_params=pltpu.CompilerParams(
            dimension_semantics=("arbitrary", "arbitrary"),
            vmem_limit_bytes=VMEM_LIMIT),
        name="mixer",
    )(x, conv_state, gla_state, p["w_in"], p["w_dw"], p["b_dw"], p["ln_conv_g"], p["ln_conv_b"],
      p["w_gate"], p["b_gate"], p["gla_norm_g"], p["w_o"], p["ln1_g"], p["ln1_b"],
      p["w_router"], p["b_router"], *alias)


def _route_kernel(lg_ref, ri_ref, rg_ref, cnt_ref, carry_ref, *, tr):
    i = pl.program_id(0)

    @pl.when(i == 0)
    def _init():
        carry_ref[...] = jnp.zeros((N_EXPERTS, 128), F32)

    l = lg_ref[...].T[0:N_EXPERTS, :]
    row = lax.broadcasted_iota(jnp.int32, (N_EXPERTS, tr), 0)
    hots, vals, idxs = [], [], []
    for _ in range(TOP_K):
        m = jnp.max(l, axis=0, keepdims=True)
        ik = jnp.min(jnp.where(l == m, row, N_EXPERTS), axis=0, keepdims=True)
        hot = row == ik
        hots.append(hot)
        vals.append(m)
        idxs.append(ik)
        l = jnp.where(hot, -jnp.inf, l)
    es = [jnp.exp(v - vals[0]) for v in vals]
    denom = es[0] + es[1] + es[2] + es[3]
    member = jnp.zeros((N_EXPERTS, tr), F32)
    for hot in hots:
        member = member + hot.astype(F32)
    before = (lax.broadcasted_iota(jnp.int32, (tr, tr), 0)
              < lax.broadcasted_iota(jnp.int32, (tr, tr), 1)).astype(BF16)
    cum = _dot(member.astype(BF16), before) + carry_ref[:, 0:1]
    ranks = [jnp.sum(jnp.where(hot, cum, 0.0), axis=0, keepdims=True).astype(jnp.int32)
             for hot in hots]
    ri_ref[...] = jnp.concatenate(idxs + ranks, axis=0)
    gates = jnp.concatenate([e / denom for e in es] + [jnp.zeros((128 - TOP_K, tr), F32)], axis=0)
    rg_ref[...] = gates.T
    carry_ref[...] = carry_ref[...] + jnp.sum(member, axis=1, keepdims=True)

    @pl.when(i == pl.num_programs(0) - 1)
    def _fin():
        cnt_ref[...] = carry_ref[...]


def _route(logits, *, tr):
    n_tok = logits.shape[0]
    assert n_tok % tr == 0
    return pl.pallas_call(
        functools.partial(_route_kernel, tr=tr),
        grid=(n_tok // tr,),
        in_specs=[pl.BlockSpec((tr, 128), lambda i: (i, 0))],
        out_specs=[pl.BlockSpec((2 * TOP_K, tr), lambda i: (0, i)),
                   pl.BlockSpec((tr, 128), lambda i: (i, 0)),
                   pl.BlockSpec((N_EXPERTS, 128), lambda i: (0, 0))],
        out_shape=[jax.ShapeDtypeStruct((2 * TOP_K, n_tok), jnp.int32),
                   jax.ShapeDtypeStruct((n_tok, 128), F32),
                   jax.ShapeDtypeStruct((N_EXPERTS, 128), F32)],
        scratch_shapes=[pltpu.VMEM((N_EXPERTS, 128), F32)],
        compiler_params=pltpu.CompilerParams(dimension_semantics=("arbitrary",)),
        name="route",
    )(logits)


def _ffn_kernel(gs_ref, nu_ref, xs_hbm, wgu_ref, bgu_ref, wdn_ref, bdn_ref, yb_hbm,
                wgu_bf, wdn_bf, xin, yout, in_sem, out_sem):
    e = pl.program_id(0)
    n_used = nu_ref[0]
    g_first = gs_ref[e]
    g_end = gs_ref[e + 1]

    def in_copy(g, slot):
        rows = pl.ds(pl.multiple_of(g * EXPERT_BLOCK, EXPERT_BLOCK), EXPERT_BLOCK)
        return pltpu.make_async_copy(xs_hbm.at[:, rows, :], xin.at[slot], in_sem.at[slot])

    def out_copy(g, slot):
        rows = pl.ds(pl.multiple_of(g * EXPERT_BLOCK, EXPERT_BLOCK), EXPERT_BLOCK)
        return pltpu.make_async_copy(yout.at[slot], yb_hbm.at[:, rows, :], out_sem.at[slot])

    @pl.when(e == 0)
    def _prime():
        for g in range(IN_AHEAD):
            @pl.when(g < n_used)
            def _():
                in_copy(g, g).start()

    @pl.when(g_end > g_first)
    def _cast():
        wgu_bf[...] = wgu_ref[0].astype(BF16)
        wdn_bf[...] = wdn_ref[0].astype(BF16)

    def blocks(g0, n):
        for j in range(n):
            g = g0 + j
            in_copy(g, g % IN_RING).wait()
        for j in range(n):
            g = g0 + IN_AHEAD + j

            @pl.when(g < n_used)
            def _prefetch():
                in_copy(g, g % IN_RING).start()
        for j in range(n):
            g = g0 + j

            @pl.when(g >= OUT_RING)
            def _free_out():
                out_copy(g - OUT_RING, g % OUT_RING).wait()
        for j in range(n):
            g = g0 + j
            x_lo, x_hi = _unpack_segments([xin[g % IN_RING, s] for s in range(N_SEG)])
            hgu = (_dot(x_lo.astype(BF16), wgu_bf[0:HALF, :])
                   + _dot(x_hi.astype(BF16), wgu_bf[HALF:, :]) + bgu_ref[0])
            gate = jnp.minimum(hgu[:, :D_FF], SWIGLU_LIMIT)
            up = jnp.clip(hgu[:, D_FF:], -SWIGLU_LIMIT, SWIGLU_LIMIT)
            act = (up + 1.0) * gate * _sigmoid(SWIGLU_ALPHA * gate)
            yb = _dot(act.astype(BF16), wdn_bf[...]) + bdn_ref[0]
            for s, seg in enumerate(_pack_segments(yb)):
                yout[g % OUT_RING, s] = seg
        for j in range(n):
            g = g0 + j
            out_copy(g, g % OUT_RING).start()

    n_blocks = g_end - g_first
    n_full = n_blocks // TRIP_BLOCKS

    def full_trip(i, carry):
        blocks(g_first + TRIP_BLOCKS * i, TRIP_BLOCKS)
        return carry

    lax.fori_loop(0, n_full, full_trip, 0)
    done = n_full * TRIP_BLOCKS
    size = TRIP_BLOCKS // 2
    while size >= 1:
        @pl.when((n_blocks // size) % 2 == 1)
        def _rest(done=done, size=size):
            blocks(g_first + done, size)
        done = done + jnp.where((n_blocks // size) % 2 == 1, size, 0)
        size //= 2

    @pl.when(e == pl.num_programs(0) - 1)
    def _drain():
        for back in range(OUT_RING, 0, -1):
            @pl.when(n_used >= back)
            def _():
                out_copy(n_used - back, (n_used - back) % OUT_RING).wait()


def _ffn(block_start, n_used, xs, w_gu, b_gu, w_down, b_down):
    n_rows = xs.shape[1]
    grid_spec = pltpu.PrefetchScalarGridSpec(
        num_scalar_prefetch=2,
        grid=(N_EXPERTS,),
        in_specs=[
            pl.BlockSpec(memory_space=pl.ANY),
            pl.BlockSpec((1, D_MODEL, 2 * D_FF), lambda e, gs, nu: (e, 0, 0)),
            pl.BlockSpec((1, 1, 2 * D_FF), lambda e, gs, nu: (e, 0, 0)),
            pl.BlockSpec((1, D_FF, D_MODEL), lambda e, gs, nu: (e, 0, 0)),
            pl.BlockSpec((1, 1, D_MODEL), lambda e, gs, nu: (e, 0, 0)),
        ],
        out_specs=pl.BlockSpec(memory_space=pl.ANY),
        scratch_shapes=[pltpu.VMEM((D_MODEL, 2 * D_FF), BF16),
                        pltpu.VMEM((D_FF, D_MODEL), BF16),
                        pltpu.VMEM((IN_RING, N_SEG, EXPERT_BLOCK, SC_SEG), U32),
                        pltpu.VMEM((OUT_RING, N_SEG, EXPERT_BLOCK, SC_SEG), U32),
                        pltpu.SemaphoreType.DMA((IN_RING,)),
                        pltpu.SemaphoreType.DMA((OUT_RING,))],
    )
    return pl.pallas_call(
        _ffn_kernel,
        grid_spec=grid_spec,
        out_shape=jax.ShapeDtypeStruct((N_SEG, n_rows, SC_SEG), U32),
        compiler_params=pltpu.CompilerParams(
            dimension_semantics=("arbitrary",), vmem_limit_bytes=VMEM_LIMIT),
        name="expert_ffn",
    )(block_start, n_used, xs, w_gu, b_gu.reshape(N_EXPERTS, 1, 2 * D_FF), w_down,
      b_down.reshape(N_EXPERTS, 1, D_MODEL))


def _combine_kernel(x1_ref, g_ref, rg_ref, ln2g_ref, ln2b_ref, *rest, blk0, n_prompt_blocks,
                    n_alias, with_sample):
    yp_ref = rest[n_alias]
    i = blk0 + pl.program_id(0)
    m_lo = jnp.zeros((x1_ref.shape[0], HALF), F32)
    m_hi = jnp.zeros((x1_ref.shape[0], HALF), F32)
    for k in range(TOP_K):
        g_lo, g_hi = _unpack_segments([g_ref[k, j] for j in range(N_SEG)])
        gate = rg_ref[:, k:k + 1]
        m_lo = m_lo + gate * g_lo
        m_hi = m_hi + gate * g_hi
    z = DN_ALPHA * x1_ref[...] + jnp.concatenate([m_lo, m_hi], axis=1)
    y = _layer_norm(z, ln2g_ref[...], ln2b_ref[...])

    if with_sample:
        ys_ref = rest[n_alias + 1]

        @pl.when(i < n_prompt_blocks)
        def _p():
            yp_ref[...] = y

        @pl.when(i >= n_prompt_blocks)
        def _s():
            ys_ref[...] = y
    else:
        yp_ref[...] = y


def _combine(x1, g, rg, ln2_g, ln2_b, *, tok0, n_prompt, tf, yp_buf=None):
    n_tok = x1.shape[0]
    piece = g.shape[2]
    n_sample = n_tok - n_prompt
    assert n_prompt % tf == 0 and n_sample % tf == 0 and tok0 % tf == 0 and piece % tf == 0
    npb = n_prompt // tf
    blk0 = tok0 // tf
    with_sample = tok0 + piece > n_prompt
    assert not with_sample or tok0 + piece == n_tok
    alias = () if yp_buf is None else (yp_buf,)
    out_specs = [pl.BlockSpec((tf, D_MODEL), lambda i: (jnp.minimum(blk0 + i, npb - 1), 0))]
    out_shape = [jax.ShapeDtypeStruct((n_prompt, D_MODEL), F32)]
    if with_sample:
        out_specs.append(pl.BlockSpec((tf, D_MODEL), lambda i: (jnp.maximum(blk0 + i - npb, 0), 0)))
        out_shape.append(jax.ShapeDtypeStruct((n_sample, D_MODEL), F32))
    n_in = 5
    return pl.pallas_call(
        functools.partial(_combine_kernel, blk0=blk0, n_prompt_blocks=npb, n_alias=len(alias),
                          with_sample=with_sample),
        grid=(piece // tf,),
        input_output_aliases={n_in + i: i for i in range(len(alias))},
        in_specs=[pl.BlockSpec((tf, D_MODEL), lambda i: (blk0 + i, 0)),
                  pl.BlockSpec((TOP_K, N_SEG, tf, SC_SEG), lambda i: (0, 0, i, 0)),
                  pl.BlockSpec((tf, 128), lambda i: (blk0 + i, 0)),
                  pl.BlockSpec((1, D_MODEL), lambda i: (0, 0)),
                  pl.BlockSpec((1, D_MODEL), lambda i: (0, 0))]
        + [pl.BlockSpec(memory_space=pl.ANY)] * len(alias),
        out_specs=out_specs,
        out_shape=out_shape,
        compiler_params=pltpu.CompilerParams(
            dimension_semantics=("arbitrary",), vmem_limit_bytes=VMEM_LIMIT),
        name="combine",
    )(x1, g, rg, ln2_g, ln2_b, *alias)


def _sc_mesh():
    return plsc.VectorSubcoreMesh(core_axis_name="c", subcore_axis_name="s")


def _sc_scatter_rows(src, dest, n_out):
    n_src, width = src.shape
    n_k = dest.shape[0]
    assert width == SC_SEG and n_src % SC_WINDOW == 0 and dest.shape[1] == n_src

    @pl.kernel(out_type=jax.ShapeDtypeStruct((n_out, width), src.dtype), mesh=_sc_mesh(),
               scratch_types=[])
    def scatter_kernel(src_hbm, idx_hbm, out_hbm):
        def body(src_vmem, idx_vmem):
            for k in range(n_k):
                pltpu.sync_copy(src_vmem, out_hbm.at[idx_vmem.at[k]])

        pltpu.emit_pipeline(
            body, grid=(n_src // SC_WINDOW,),
            in_specs=[pl.BlockSpec((SC_WINDOW, width), lambda i: (i, 0)),
                      pl.BlockSpec((n_k, SC_WINDOW), lambda i: (0, i))],
            out_specs=[],
            core_axis_name=("c", "s"),
            dimension_semantics=(pltpu.PARALLEL,),
        )(src_hbm, idx_hbm)

    return scatter_kernel(src, dest)


def _sc_gather_rows(src, idx):
    n_out = idx.shape[1]
    width = src.shape[1]
    assert width == SC_SEG and n_out % SC_WINDOW == 0

    @pl.kernel(out_type=jax.ShapeDtypeStruct((n_out, width), src.dtype), mesh=_sc_mesh(),
               scratch_types=[])
    def gather_kernel(src_hbm, idx_hbm, out_hbm):
        def body(idx_vmem, out_vmem):
            pltpu.sync_copy(src_hbm.at[idx_vmem.at[0]], out_vmem)

        pltpu.emit_pipeline(
            body, grid=(n_out // SC_WINDOW,),
            in_specs=[pl.BlockSpec((1, SC_WINDOW), lambda i: (0, i))],
            out_specs=[pl.BlockSpec((SC_WINDOW, width), lambda i: (i, 0))],
            core_axis_name=("c", "s"),
            dimension_semantics=(pltpu.PARALLEL,),
        )(idx_hbm, out_hbm)

    return gather_kernel(src, idx)


def _prep_params(w_in, w_dw, b_dw, ln_conv_g, ln_conv_b, w_gate_lr, b_gate, gla_norm_g, w_o,
                 ln1_g, ln1_b, w_router, b_router):
    row = lambda v: v.reshape(1, -1).astype(F32)
    w_in_p = jnp.pad(w_in, ((0, 0), (0, IN_COLS_PAD - w_in.shape[1]))).astype(BF16)
    w_gate = jnp.pad(w_gate_lr, ((0, GATE_PAD - GATE_RANK), (0, 0))).astype(BF16)
    wr_hi = w_router.astype(BF16)
    wr_lo = (w_router - wr_hi.astype(F32)).astype(BF16)
    padr = lambda m: jnp.pad(m, ((0, 0), (0, 128 - N_EXPERTS)))
    return {
        "w_in": w_in_p, "w_dw": jnp.repeat(w_dw, 8, axis=0), "b_dw": row(b_dw), "ln_conv_g": row(ln_conv_g),
        "ln_conv_b": row(ln_conv_b), "w_gate": w_gate, "b_gate": row(b_gate),
        "gla_norm_g": row(gla_norm_g), "w_o": w_o.astype(BF16), "ln1_g": row(ln1_g),
        "ln1_b": row(ln1_b), "w_router": jnp.concatenate([padr(wr_hi), padr(wr_lo)], axis=1),
        "b_router": jnp.pad(row(b_router), ((0, 0), (0, 128 - N_EXPERTS))),
    }


def _dest_kernel(ps_ref, ri_ref, out_ref, *, n_slot):
    idx = ri_ref[0:TOP_K, :]
    dest = ri_ref[TOP_K:2 * TOP_K, :] + pl.program_id(0) * n_slot
    for e in range(N_EXPERTS):
        dest = dest + jnp.where(idx == e, ps_ref[e], 0)
    out_ref[...] = dest


def _moe_layout(ri, counts, *, tr):
    n_tok = ri.shape[1]
    cnt = counts[:, 0].astype(jnp.int32)
    padded = (cnt + EXPERT_BLOCK - 1) // EXPERT_BLOCK * EXPERT_BLOCK
    pad_end = jnp.cumsum(padded)
    pad_start = pad_end - padded
    nb = -(-(n_tok * TOP_K) // EXPERT_BLOCK) + N_EXPERTS
    n_t = n_tok // tr
    dest_seg = pl.pallas_call(
        functools.partial(_dest_kernel, n_slot=nb * EXPERT_BLOCK),
        grid_spec=pltpu.PrefetchScalarGridSpec(
            num_scalar_prefetch=1,
            grid=(N_SEG, n_t),
            in_specs=[pl.BlockSpec((2 * TOP_K, tr), lambda j, i, ps: (0, i))],
            out_specs=pl.BlockSpec((TOP_K, tr), lambda j, i, ps: (0, j * n_t + i)),
        ),
        out_shape=jax.ShapeDtypeStruct((TOP_K, N_SEG * n_tok), jnp.int32),
        name="slot_index",
    )(pad_start.astype(jnp.int32), ri)
    block_start = (jnp.concatenate([pad_start, pad_end[-1:]]) // EXPERT_BLOCK).astype(jnp.int32)
    n_used = block_start[-1:]
    return dest_seg, block_start, n_used, nb


def kernel(x_prompt, x_sample, state_conv, state_gla, w_in, w_dw, b_dw, ln_conv_g, ln_conv_b,
           w_gate_lr, b_gate, gla_norm_g, w_o, ln1_g, ln1_b, w_router, b_router, w_gu, b_gu,
           w_down, b_down, ln2_g, ln2_b):
    assert w_in.shape[0] == DEPTH
    l = 0
    p = _prep_params(w_in[l], w_dw[l], b_dw[l], ln_conv_g[l], ln_conv_b[l], w_gate_lr[l],
                     b_gate[l], gla_norm_g[l], w_o[l], ln1_g[l], ln1_b[l], w_router[l],
                     b_router[l])
    bp, tp, _ = x_prompt.shape
    bs, ts, _ = x_sample.shape
    zc = jnp.zeros((bp, CONV_W - 1, CONV_CH), F32)
    zs = jnp.zeros((bp, GLA_HEADS, GLA_DK, GLA_DV), F32)
    n_prompt = bp * tp
    n_tok = n_prompt + bs * ts
    x1p, x1, lg, conv_p, gla_p = _mixer(x_prompt, zc, zs, p, tm=512, n_tok_total=n_tok,
                                        tok_offset=0)
    x1p, x1, lg, conv_s, gla_s = _mixer(x_sample, state_conv[l], state_gla[l], p, tm=ts,
                                        n_tok_total=n_tok, tok_offset=n_prompt,
                                        token_bufs=(x1p, x1, lg))

    tr = 768 if n_tok % 768 == 0 else 256
    ri, rg, counts = _route(lg, tr=tr)
    dest_seg, block_start, n_used, nb = _moe_layout(ri, counts, tr=n_tok)
    n_slot = nb * EXPERT_BLOCK
    xs = _sc_scatter_rows(x1p.reshape(N_SEG * n_tok, SC_SEG), dest_seg, N_SEG * n_slot)
    yb = _ffn(block_start, n_used, xs.reshape(N_SEG, n_slot, SC_SEG), w_gu[l], b_gu[l], w_down[l],
              b_down[l])
    tf = 256
    n_blk = n_prompt // tf
    last_blk = min(LAST_PIECE_BLOCKS, n_blk // 2)
    body_blk = n_blk - last_blk
    n_body = min(COMBINE_PIECES - 1, body_blk)
    bounds = [(p * body_blk // n_body) * tf for p in range(n_body)] + [body_blk * tf, n_tok]
    dest_kjt = dest_seg.reshape(TOP_K, N_SEG, n_tok)
    yp = None
    for t0, t1 in zip(bounds[:-1], bounds[1:]):
        g = _sc_gather_rows(yb.reshape(N_SEG * n_slot, SC_SEG),
                            dest_kjt[:, :, t0:t1].reshape(1, -1))
        outs = _combine(x1, g.reshape(TOP_K, N_SEG, t1 - t0, SC_SEG), rg,
                        ln2_g[l].reshape(1, -1), ln2_b[l].reshape(1, -1), tok0=t0,
                        n_prompt=n_prompt, tf=tf, yp_buf=yp)
        yp = outs[0]
    ys = outs[1]
    return (yp.reshape(bp, tp, D_MODEL), ys.reshape(bs, ts, D_MODEL),
            conv_p[None], gla_p[None], conv_s[None], gla_s[None])
```

```python
import functools

import jax
import jax.numpy as jnp
from jax import lax
from jax.experimental import pallas as pl
from jax.experimental.pallas import tpu as pltpu
from jax.experimental.pallas import tpu_sc as plsc

F32 = jnp.float32
BF16 = jnp.bfloat16

D_MODEL = 1024
CHUNK = 64
SUB = 16
DIAG = 8
CONV_CH = 512
CONV_W = 31
HALO = 32
GLU_ROWS = 128
CONV_BLOCK = 64
SAMPLE_STREAMS = 8
LAST_PIECE_BLOCKS = 4
COMBINE_PIECES = 8
TRIP_BLOCKS = 4
IN_AHEAD = TRIP_BLOCKS
IN_RING = 2 * TRIP_BLOCKS
OUT_RING = 2 * TRIP_BLOCKS
GLA_HEADS = 4
GLA_DK = 64
GLA_DV = 128
GLA_K = GLA_HEADS * GLA_DK
GLA_V = GLA_HEADS * GLA_DV
GATE_RANK = 16
GATE_PAD = 128
GATE_NORM = 16.0
MIX_W = CONV_CH + GLA_V
OFF_UV = 0
OFF_UG = OFF_UV + CONV_CH
OFF_Q = OFF_UG + CONV_CH
OFF_K = OFF_Q + GLA_K
OFF_V = OFF_K + GLA_K
OFF_G = OFF_V + GLA_V
OFF_A = OFF_G + GLA_V
IN_COLS_PAD = OFF_A + GATE_PAD
N_EXPERTS = 32
TOP_K = 4
D_FF = 1024
SWIGLU_LIMIT = 7.0
SWIGLU_ALPHA = 1.702
EXPERT_BLOCK = 256
LN_EPS = 1e-5
DEPTH = 1
DN_ALPHA = (2 * DEPTH) ** 0.25
VMEM_LIMIT = 56 * 1024 * 1024
HALF = D_MODEL // 2
SC_SEG = 256
SC_WINDOW = 128
N_SEG = HALF // SC_SEG
U32 = jnp.uint32


def _dot(a, b):
    return jnp.dot(a, b, preferred_element_type=F32)


def _dot_nt(a, b):
    return lax.dot_general(a, b, (((1,), (1,)), ((), ())), preferred_element_type=F32)


def _dot_tn(a, b):
    return lax.dot_general(a, b, (((0,), (0,)), ((), ())), preferred_element_type=F32)


def _layer_norm(x, g, b):
    mu = jnp.mean(x, axis=-1, keepdims=True)
    xc = x - mu
    var = jnp.mean(xc * xc, axis=-1, keepdims=True)
    return xc * lax.rsqrt(var + LN_EPS) * g + b


def _sigmoid(x):
    return 1.0 / (1.0 + jnp.exp(-x))


def _split_bf16(x):
    hi = x.astype(BF16)
    lo = (x - hi.astype(F32)).astype(BF16)
    return hi, lo


def _pack_segments(x):
    bits = pltpu.bitcast(x.astype(BF16).astype(F32), U32)
    word = (bits[:, :HALF] >> 16) | bits[:, HALF:]
    return [word[:, j * SC_SEG:(j + 1) * SC_SEG] for j in range(N_SEG)]


def _unpack_segments(segs):
    word = jnp.concatenate(segs, axis=1)
    lo = pltpu.bitcast(word << 16, F32)
    hi = pltpu.bitcast(word & jnp.uint32(0xFFFF0000), F32)
    return lo, hi


def _mixer_kernel(x_ref, cs_ref, gs_ref, w_in_ref, w_dw_ref, b_dw_ref, lncg_ref, lncb_ref,
                  wg_ref, bg_ref, gng_ref, w_o_ref, ln1g_ref, ln1b_ref, wr_ref, br_ref, *rest,
                  tm, chunk, n_alias, sb):
    (x1p_ref, x1_ref, logit_ref, conv_out_ref, gla_out_ref,
     ubuf, qbuf, kbuf, labuf, vbuf, gbuf, mixbuf, st_ref) = rest[n_alias:]
    t = pl.program_id(1)
    nt = pl.num_programs(1)
    n_chunks = tm // chunk
    n_sub = chunk // SUB
    n_rows = sb * tm
    us = HALO + tm

    @pl.when(t == 0)
    def _init():
        for s in range(sb):
            ubuf[s * us:s * us + HALO - (CONV_W - 1), :] = jnp.zeros(
                (HALO - (CONV_W - 1), CONV_CH), F32)
            ubuf[s * us + HALO - (CONV_W - 1):s * us + HALO, :] = cs_ref[s]
            for h in range(GLA_HEADS):
                st_ref[s * GLA_DV:(s + 1) * GLA_DV, h * GLA_DK:(h + 1) * GLA_DK] = gs_ref[s, h].T

    x = x_ref[...].reshape(n_rows, D_MODEL)
    xb = x.astype(BF16)

    gp = min(GLU_ROWS, tm) if sb == 1 else n_rows
    for r0 in range(0, n_rows, gp):
        hv = _dot(xb[r0:r0 + gp], w_in_ref[:, OFF_UV:OFF_UG])
        hg = _dot(xb[r0:r0 + gp], w_in_ref[:, OFF_UG:OFF_Q])
        u = hv * _sigmoid(hg)
        for s in range(sb):
            lo = max(r0, s * tm)
            hi = min(r0 + gp, (s + 1) * tm)
            if lo < hi:
                ubuf[s * us + HALO + lo - s * tm:s * us + HALO + hi - s * tm, :] = u[lo - r0:hi - r0]
    lead = HALO - (CONV_W - 1)

    def proj_q(r0, n):
        qbuf[r0:r0 + n, :] = _dot(xb[r0:r0 + n], w_in_ref[:, OFF_Q:OFF_K]) * (GLA_DK ** -0.5)

    def proj_k(r0, n):
        kbuf[r0:r0 + n, :] = _dot(xb[r0:r0 + n], w_in_ref[:, OFF_K:OFF_V])

    def proj_v(half, r0, n):
        lo = half * (GLA_V // 2)
        vbuf[r0:r0 + n, lo:lo + GLA_V // 2] = _dot(
            xb[r0:r0 + n], w_in_ref[:, OFF_V + lo:OFF_V + lo + GLA_V // 2])

    def proj_g(half, r0, n):
        lo = half * (GLA_V // 2)
        gbuf[r0:r0 + n, lo:lo + GLA_V // 2] = _dot(
            xb[r0:r0 + n], w_in_ref[:, OFF_G + lo:OFF_G + lo + GLA_V // 2])

    def proj_gate(r0, n):
        ha = _dot(xb[r0:r0 + n], w_in_ref[:, OFF_A:IN_COLS_PAD])
        a = _dot(ha.astype(BF16), wg_ref[...]) + bg_ref[...]
        labuf[r0:r0 + n, :] = ((jnp.minimum(a, 0.0) - jnp.log(1.0 + jnp.exp(-jnp.abs(a))))
                               * (1.0 / GATE_NORM))

    proj_jobs = [proj_q, proj_k, functools.partial(proj_v, 0), functools.partial(proj_v, 1),
                 functools.partial(proj_g, 0), functools.partial(proj_g, 1), proj_gate]

    cb = min(CONV_BLOCK, tm)

    def conv_block(stream, r0):
        acc = jnp.zeros((cb, CONV_CH), F32) + b_dw_ref[...]
        for r in range(8):
            rows = cb if r == 0 else cb + 8
            part = None
            for s in range(r, lead + CONV_W, 8):
                if s < lead:
                    continue
                w8 = w_dw_ref[8 * (s - lead):8 * (s - lead) + 8, :]
                u3 = ubuf[pl.ds(stream * us + r0 + (s - r), rows), :].reshape(
                    rows // 8, 8, CONV_CH)
                term = u3 * w8
                part = term if part is None else part + term
            acc = acc + part.reshape(rows, CONV_CH)[r:r + cb, :]
        cact = _layer_norm(acc, lncg_ref[...], lncb_ref[...])
        cact = cact * _sigmoid(cact)
        mixbuf[pl.ds(stream * tm + r0, cb), 0:CONV_CH] = cact.astype(BF16)

    per_trip = 4 if n_chunks % 4 == 0 else 1
    n_trips = n_chunks // per_trip
    loop_blocks = (tm // 2) // (cb * n_trips) if n_chunks >= 4 else 0
    pre_blocks = tm // cb - loop_blocks * n_trips
    static_trips = n_trips == 2 and sb == 1
    rows_first = tm // 2 if static_trips else n_rows
    pre = [(s, blk * cb) for s in range(sb) for blk in range(pre_blocks)]
    for i in range(max(len(pre), len(proj_jobs))):
        if i < len(proj_jobs):
            proj_jobs[i](0, rows_first)
        if i < len(pre):
            conv_block(*pre[i])

    lane_k = lax.broadcasted_iota(jnp.int32, (1, GLA_K), 1) // GLA_DK
    row_c = lax.broadcasted_iota(jnp.int32, (chunk, 1), 0)
    tri = (lax.broadcasted_iota(jnp.int32, (chunk, chunk), 0)
           >= lax.broadcasted_iota(jnp.int32, (chunk, chunk), 1)).astype(BF16)
    e2 = (lax.broadcasted_iota(jnp.int32, (GLA_K, GLA_V), 0) // GLA_DK
          == lax.broadcasted_iota(jnp.int32, (GLA_K, GLA_V), 1) // GLA_DV).astype(BF16)
    row_s = lax.broadcasted_iota(jnp.int32, (GLA_HEADS * chunk, 1), 0)
    blk_s = (row_s % chunk) // SUB
    same_blk = blk_s == lax.broadcasted_iota(jnp.int32, (1, 128), 1) // SUB
    second_half = row_c % SUB >= DIAG
    row_g = lax.broadcasted_iota(jnp.int32, (1, DIAG, 1), 1)
    lane_v = lax.broadcasted_iota(jnp.int32, (1, GLA_V), 1) % GLA_DV

    def head_stack(m):
        return jnp.concatenate(
            [jnp.where(lane_k == h, m, 0.0) for h in range(GLA_HEADS)], axis=0).astype(BF16)

    def unstack(r, width):
        return jnp.concatenate(
            [r[h * chunk:(h + 1) * chunk, h * width:(h + 1) * width] for h in range(GLA_HEADS)],
            axis=1)

    def chunk_step(stream, c, carry):
        base = stream * tm + (c * chunk if isinstance(c, int) else pl.multiple_of(c * chunk, chunk))
        st_rows = slice(stream * GLA_DV, (stream + 1) * GLA_DV)
        q_c = qbuf[pl.ds(base, chunk), :]
        k_c = kbuf[pl.ds(base, chunk), :]
        v_c = vbuf[pl.ds(base, chunk), :]
        v_b = v_c.astype(BF16)
        la_c = labuf[pl.ds(base, chunk), :]
        la_hi, la_lo = _split_bf16(la_c)
        b = _dot(tri, la_hi) + _dot(tri, la_lo)
        b_last = b[chunk - 1:chunk, :]
        st = st_ref[st_rows, :]

        qe = q_c * jnp.exp(b)
        o2 = _dot_nt(head_stack(qe), st.astype(BF16))
        o = jnp.concatenate([o2[h * chunk:(h + 1) * chunk, :] for h in range(GLA_HEADS)], axis=1)

        a_off = None
        if n_sub > 1:
            r_rows = [b[0:SUB, :]] + [jnp.broadcast_to(b[SUB * i - 1:SUB * i, :], (SUB, GLA_K))
                                      for i in range(1, n_sub)]
            r_q = jnp.concatenate(r_rows, axis=0)
            q_t = jnp.where(row_c >= SUB, q_c * jnp.exp(jnp.minimum(b - r_q, 0.0)), 0.0)
            k_parts = []
            for i in range(1, n_sub):
                r_i = b[SUB * i - 1:SUB * i, :]
                k_t = jnp.where(row_c < SUB * i, k_c * jnp.exp(jnp.minimum(r_i - b, 0.0)), 0.0)
                k_parts.append(k_t)
                k_parts.append(jnp.zeros((128 - chunk, GLA_K), F32))
            k_cat = jnp.concatenate(k_parts, axis=0).astype(BF16)
            r = _dot_nt(head_stack(q_t), k_cat)
            a_off = r[:, 0:128]
            for i in range(2, n_sub):
                a_off = jnp.where(blk_s == i, r[:, (i - 1) * 128:i * 128], a_off)

        r_b = jnp.concatenate(
            [jnp.broadcast_to(b[SUB * i + DIAG - 1:SUB * i + DIAG, :], (SUB, GLA_K))
             for i in range(n_sub)], axis=0)
        q_b = jnp.where(second_half, q_c * jnp.exp(jnp.minimum(b - r_b, 0.0)), 0.0)
        k_b = jnp.where(second_half, 0.0, k_c * jnp.exp(jnp.minimum(r_b - b, 0.0)))
        k_b = jnp.concatenate([k_b, jnp.zeros((128 - chunk, GLA_K), F32)], axis=0).astype(BF16)
        a_half = jnp.where(same_blk, _dot_nt(head_stack(q_b), k_b), 0.0)
        a_off = a_half if a_off is None else a_off + a_half

        q3 = q_c.reshape(chunk // DIAG, DIAG, GLA_K)
        k3 = k_c.reshape(chunk // DIAG, DIAG, GLA_K)
        b3 = b.reshape(chunk // DIAG, DIAG, GLA_K)
        ps = []
        for d in range(DIAG):
            k_s = k3 if d == 0 else pltpu.roll(k3, d, axis=1)
            b_s = b3 if d == 0 else pltpu.roll(b3, d, axis=1)
            p = jnp.where(row_g >= d, q3 * k_s * jnp.exp(jnp.minimum(b3 - b_s, 0.0)), 0.0)
            ps.append(p.reshape(chunk, GLA_K).astype(BF16))
        w_all = _dot(jnp.concatenate(ps, axis=0), e2)
        a_d = jnp.zeros((chunk, GLA_V), F32)
        for d in range(DIAG):
            a_d = jnp.where(lane_v == row_c - d, w_all[d * chunk:(d + 1) * chunk, :], a_d)
        a_off = a_off + jnp.concatenate(
            [a_d[:, h * GLA_DV:(h + 1) * GLA_DV] for h in range(GLA_HEADS)], axis=0)
        o1 = _dot(a_off[:, 0:chunk].astype(BF16), v_b)
        o = o + unstack(o1, GLA_DV)

        ke = k_c * jnp.exp(b_last - b)
        r2 = _dot_tn(v_b, ke.astype(BF16))
        upd = jnp.zeros((GLA_DV, GLA_K), F32)
        for h in range(GLA_HEADS):
            upd = upd + jnp.where(lane_k == h, r2[h * GLA_DV:(h + 1) * GLA_DV, :], 0.0)
        st_ref[st_rows, :] = st * jnp.exp(b_last) + upd

        g_c = gbuf[pl.ds(base, chunk), :]
        outs = []
        for h in range(GLA_HEADS):
            oh = o[:, h * GLA_DV:(h + 1) * GLA_DV]
            ms = jnp.mean(oh * oh, axis=-1, keepdims=True)
            outs.append(oh * lax.rsqrt(ms + LN_EPS))
        on = jnp.concatenate(outs, axis=1) * gng_ref[...]
        on = on * (g_c * _sigmoid(g_c))
        mixbuf[pl.ds(base, chunk), CONV_CH:MIX_W] = on.astype(BF16)
        return carry

    def trip(stream, i, carry, extra_jobs=()):
        for j in range(max(per_trip, loop_blocks, len(extra_jobs))):
            if j < len(extra_jobs):
                extra_jobs[j](rows_first, tm - rows_first)
            if j < loop_blocks:
                r0 = (pre_blocks + i * loop_blocks + j) * cb
                conv_block(stream, r0 if isinstance(r0, int) else pl.multiple_of(r0, cb))
            if j < per_trip:
                carry = chunk_step(stream, i * per_trip + j, carry)
        return carry

    def merge_rows(r0, n):
        y = _dot(mixbuf[r0:r0 + n, :], w_o_ref[...])
        x1 = _layer_norm(DN_ALPHA * x[r0:r0 + n, :] + y, ln1g_ref[...], ln1b_ref[...])
        x1_ref[r0:r0 + n, :] = x1
        for j, seg in enumerate(_pack_segments(x1)):
            x1p_ref[j, r0:r0 + n, :] = seg
        x_hi, x_lo = _split_bf16(x1)
        lg = _dot(x_hi, wr_ref[...])
        logit_ref[r0:r0 + n, :] = (lg[:, 0:128] + lg[:, 128:256] + _dot(x_lo, wr_ref[:, 0:128])
                                   + br_ref[...])

    if static_trips:
        trip(0, 0, 0, tuple(proj_jobs))
        merge_rows(0, rows_first)
        trip(0, 1, 0)
        merge_rows(rows_first, tm - rows_first)
    else:
        for s in range(sb):
            if n_trips == 1:
                trip(s, 0, 0)
            else:
                lax.fori_loop(0, n_trips, functools.partial(trip, s), 0)
        merge_rows(0, n_rows)

    tails = []
    for s in range(sb):
        tail = ubuf[s * us + tm:s * us + tm + HALO, :]
        ubuf[s * us:s * us + HALO, :] = tail
        tails.append(tail)

    @pl.when(t == nt - 1)
    def _state_out():
        for s in range(sb):
            conv_out_ref[s] = tails[s][HALO - (CONV_W - 1):, :]
            st = st_ref[s * GLA_DV:(s + 1) * GLA_DV, :]
            for h in range(GLA_HEADS):
                gla_out_ref[s, h] = st[:, h * GLA_DK:(h + 1) * GLA_DK].T


def _mixer(x, conv_state, gla_state, p, *, tm, n_tok_total, tok_offset, token_bufs=None, sb=1):
    bsz, seq, _ = x.shape
    chunk = min(CHUNK, seq)
    rows = sb * tm
    assert seq % tm == 0 and tm % chunk == 0 and chunk % SUB == 0 and tok_offset % rows == 0
    assert bsz % sb == 0 and (sb == 1 or seq == tm)
    nt = seq // tm
    blk0 = tok_offset // rows
    full = lambda shape: pl.BlockSpec(shape, lambda b, t: (0,) * len(shape))
    alias = () if token_bufs is None else tuple(token_bufs)
    kern = functools.partial(_mixer_kernel, tm=tm, chunk=chunk, n_alias=len(alias), sb=sb)
    n_in = 16
    return pl.pallas_call(
        kern,
        grid=(bsz // sb, nt),
        input_output_aliases={n_in + i: i for i in range(len(alias))},
        in_specs=[
            pl.BlockSpec((sb, tm, D_MODEL), lambda b, t: (b, t, 0)),
            pl.BlockSpec((sb, CONV_W - 1, CONV_CH), lambda b, t: (b, 0, 0)),
            pl.BlockSpec((sb, GLA_HEADS, GLA_DK, GLA_DV), lambda b, t: (b, 0, 0, 0)),
            full((D_MODEL, IN_COLS_PAD)),
            full((8 * CONV_W, CONV_CH)),
            full((1, CONV_CH)),
            full((1, CONV_CH)),
            full((1, CONV_CH)),
            full((GATE_PAD, GLA_K)),
            full((1, GLA_K)),
            full((1, GLA_V)),
            full((MIX_W, D_MODEL)),
            full((1, D_MODEL)),
            full((1, D_MODEL)),
            full((D_MODEL, 256)),
            full((1, 128)),
        ] + [pl.BlockSpec(memory_space=pl.ANY)] * len(alias),
        out_specs=[
            pl.BlockSpec((N_SEG, rows, SC_SEG), lambda b, t: (0, blk0 + b * nt + t, 0)),
            pl.BlockSpec((rows, D_MODEL), lambda b, t: (blk0 + b * nt + t, 0)),
            pl.BlockSpec((rows, 128), lambda b, t: (blk0 + b * nt + t, 0)),
            pl.BlockSpec((sb, CONV_W - 1, CONV_CH), lambda b, t: (b, 0, 0)),
            pl.BlockSpec((sb, GLA_HEADS, GLA_DK, GLA_DV), lambda b, t: (b, 0, 0, 0)),
        ],
        out_shape=[
            jax.ShapeDtypeStruct((N_SEG, n_tok_total, SC_SEG), U32),
            jax.ShapeDtypeStruct((n_tok_total, D_MODEL), F32),
            jax.ShapeDtypeStruct((n_tok_total, 128), F32),
            jax.ShapeDtypeStruct((bsz, CONV_W - 1, CONV_CH), F32),
            jax.ShapeDtypeStruct((bsz, GLA_HEADS, GLA_DK, GLA_DV), F32),
        ],
        scratch_shapes=[
            pltpu.VMEM((sb * (HALO + tm), CONV_CH), F32),
            pltpu.VMEM((rows, GLA_K), F32),
            pltpu.VMEM((rows, GLA_K), F32),
            pltpu.VMEM((rows, GLA_K), F32),
            pltpu.VMEM((rows, GLA_V), F32),
            pltpu.VMEM((rows, GLA_V), F32),
            pltpu.VMEM((rows, MIX_W), BF16),
            pltpu.VMEM((sb * GLA_DV, GLA_K), F32),
        ],
        compiler_params=pltpu.CompilerParams(
            dimension_semantics=("arbitrary", "arbitrary"),
            vmem_limit_bytes=VMEM_LIMIT),
        name="mixer",
    )(x, conv_state, gla_state, p["w_in"], p["w_dw"], p["b_dw"], p["ln_conv_g"], p["ln_conv_b"],
      p["w_gate"], p["b_gate"], p["gla_norm_g"], p["w_o"], p["ln1_g"], p["ln1_b"],
      p["w_router"], p["b_router"], *alias)


def _route_kernel(lg_ref, ri_ref, rg_ref, cnt_ref, carry_ref, *, tr):
    i = pl.program_id(0)

    @pl.when(i == 0)
    def _init():
        carry_ref[...] = jnp.zeros((N_EXPERTS, 128), F32)

    l = lg_ref[...].T[0:N_EXPERTS, :]
    row = lax.broadcasted_iota(jnp.int32, (N_EXPERTS, tr), 0)
    hots, vals, idxs = [], [], []
    for _ in range(TOP_K):
        m = jnp.max(l, axis=0, keepdims=True)
        ik = jnp.min(jnp.where(l == m, row, N_EXPERTS), axis=0, keepdims=True)
        hot = row == ik
        hots.append(hot)
        vals.append(m)
        idxs.append(ik)
        l = jnp.where(hot, -jnp.inf, l)
    es = [jnp.exp(v - vals[0]) for v in vals]
    denom = es[0] + es[1] + es[2] + es[3]
    member = jnp.zeros((N_EXPERTS, tr), F32)
    for hot in hots:
        member = member + hot.astype(F32)
    before = (lax.broadcasted_iota(jnp.int32, (tr, tr), 0)
              < lax.broadcasted_iota(jnp.int32, (tr, tr), 1)).astype(BF16)
    cum = _dot(member.astype(BF16), before) + carry_ref[:, 0:1]
    ranks = [jnp.sum(jnp.where(hot, cum, 0.0), axis=0, keepdims=True).astype(jnp.int32)
             for hot in hots]
    ri_ref[...] = jnp.concatenate(idxs + ranks, axis=0)
    gates = jnp.concatenate([e / denom for e in es] + [jnp.zeros((128 - TOP_K, tr), F32)], axis=0)
    rg_ref[...] = gates.T
    carry_ref[...] = carry_ref[...] + jnp.sum(member, axis=1, keepdims=True)

    @pl.when(i == pl.num_programs(0) - 1)
    def _fin():
        cnt_ref[...] = carry_ref[...]


def _route(logits, *, tr):
    n_tok = logits.shape[0]
    assert n_tok % tr == 0
    return pl.pallas_call(
        functools.partial(_route_kernel, tr=tr),
        grid=(n_tok // tr,),
        in_specs=[pl.BlockSpec((tr, 128), lambda i: (i, 0))],
        out_specs=[pl.BlockSpec((2 * TOP_K, tr), lambda i: (0, i)),
                   pl.BlockSpec((tr, 128), lambda i: (i, 0)),
                   pl.BlockSpec((N_EXPERTS, 128), lambda i: (0, 0))],
        out_shape=[jax.ShapeDtypeStruct((2 * TOP_K, n_tok), jnp.int32),
                   jax.ShapeDtypeStruct((n_tok, 128), F32),
                   jax.ShapeDtypeStruct((N_EXPERTS, 128), F32)],
        scratch_shapes=[pltpu.VMEM((N_EXPERTS, 128), F32)],
        compiler_params=pltpu.CompilerParams(dimension_semantics=("arbitrary",)),
        name="route",
    )(logits)


def _ffn_kernel(gs_ref, nu_ref, xs_hbm, wgu_ref, bgu_ref, wdn_ref, bdn_ref, yb_hbm,
                wgu_bf, wdn_bf, xin, yout, in_sem, out_sem):
    e = pl.program_id(0)
    n_used = nu_ref[0]
    g_first = gs_ref[e]
    g_end = gs_ref[e + 1]

    def in_copy(g, slot):
        rows = pl.ds(pl.multiple_of(g * EXPERT_BLOCK, EXPERT_BLOCK), EXPERT_BLOCK)
        return pltpu.make_async_copy(xs_hbm.at[:, rows, :], xin.at[slot], in_sem.at[slot])

    def out_copy(g, slot):
        rows = pl.ds(pl.multiple_of(g * EXPERT_BLOCK, EXPERT_BLOCK), EXPERT_BLOCK)
        return pltpu.make_async_copy(yout.at[slot], yb_hbm.at[:, rows, :], out_sem.at[slot])

    @pl.when(e == 0)
    def _prime():
        for g in range(IN_AHEAD):
            @pl.when(g < n_used)
            def _():
                in_copy(g, g).start()

    @pl.when(g_end > g_first)
    def _cast():
        wgu_bf[...] = wgu_ref[0].astype(BF16)
        wdn_bf[...] = wdn_ref[0].astype(BF16)

    def blocks(g0, n):
        for j in range(n):
            g = g0 + j
            in_copy(g, g % IN_RING).wait()
        for j in range(n):
            g = g0 + IN_AHEAD + j

            @pl.when(g < n_used)
            def _prefetch():
                in_copy(g, g % IN_RING).start()
        for j in range(n):
            g = g0 + j

            @pl.when(g >= OUT_RING)
            def _free_out():
                out_copy(g - OUT_RING, g % OUT_RING).wait()
        for j in range(n):
            g = g0 + j
            x_lo, x_hi = _unpack_segments([xin[g % IN_RING, s] for s in range(N_SEG)])
            hgu = (_dot(x_lo.astype(BF16), wgu_bf[0:HALF, :])
                   + _dot(x_hi.astype(BF16), wgu_bf[HALF:, :]) + bgu_ref[0])
            gate = jnp.minimum(hgu[:, :D_FF], SWIGLU_LIMIT)
            up = jnp.clip(hgu[:, D_FF:], -SWIGLU_LIMIT, SWIGLU_LIMIT)
            act = (up + 1.0) * gate * _sigmoid(SWIGLU_ALPHA * gate)
            yb = _dot(act.astype(BF16), wdn_bf[...]) + bdn_ref[0]
            for s, seg in enumerate(_pack_segments(yb)):
                yout[g % OUT_RING, s] = seg
        for j in range(n):
            g = g0 + j
            out_copy(g, g % OUT_RING).start()

    n_blocks = g_end - g_first
    n_full = n_blocks // TRIP_BLOCKS

    def full_trip(i, carry):
        blocks(g_first + TRIP_BLOCKS * i, TRIP_BLOCKS)
        return carry

    lax.fori_loop(0, n_full, full_trip, 0)
    done = n_full * TRIP_BLOCKS
    size = TRIP_BLOCKS // 2
    while size >= 1:
        @pl.when((n_blocks // size) % 2 == 1)
        def _rest(done=done, size=size):
            blocks(g_first + done, size)
        done = done + jnp.where((n_blocks // size) % 2 == 1, size, 0)
        size //= 2

    @pl.when(e == pl.num_programs(0) - 1)
    def _drain():
        for back in range(OUT_RING, 0, -1):
            @pl.when(n_used >= back)
            def _():
                out_copy(n_used - back, (n_used - back) % OUT_RING).wait()


def _ffn(block_start, n_used, xs, w_gu, b_gu, w_down, b_down):
    n_rows = xs.shape[1]
    grid_spec = pltpu.PrefetchScalarGridSpec(
        num_scalar_prefetch=2,
        grid=(N_EXPERTS,),
        in_specs=[
            pl.BlockSpec(memory_space=pl.ANY),
            pl.BlockSpec((1, D_MODEL, 2 * D_FF), lambda e, gs, nu: (e, 0, 0)),
            pl.BlockSpec((1, 1, 2 * D_FF), lambda e, gs, nu: (e, 0, 0)),
            pl.BlockSpec((1, D_FF, D_MODEL), lambda e, gs, nu: (e, 0, 0)),
            pl.BlockSpec((1, 1, D_MODEL), lambda e, gs, nu: (e, 0, 0)),
        ],
        out_specs=pl.BlockSpec(memory_space=pl.ANY),
        scratch_shapes=[pltpu.VMEM((D_MODEL, 2 * D_FF), BF16),
                        pltpu.VMEM((D_FF, D_MODEL), BF16),
                        pltpu.VMEM((IN_RING, N_SEG, EXPERT_BLOCK, SC_SEG), U32),
                        pltpu.VMEM((OUT_RING, N_SEG, EXPERT_BLOCK, SC_SEG), U32),
                        pltpu.SemaphoreType.DMA((IN_RING,)),
                        pltpu.SemaphoreType.DMA((OUT_RING,))],
    )
    return pl.pallas_call(
        _ffn_kernel,
        grid_spec=grid_spec,
        out_shape=jax.ShapeDtypeStruct((N_SEG, n_rows, SC_SEG), U32),
        compiler_params=pltpu.CompilerParams(
            dimension_semantics=("arbitrary",), vmem_limit_bytes=VMEM_LIMIT),
        name="expert_ffn",
    )(block_start, n_used, xs, w_gu, b_gu.reshape(N_EXPERTS, 1, 2 * D_FF), w_down,
      b_down.reshape(N_EXPERTS, 1, D_MODEL))


def _combine_kernel(x1_ref, g_ref, rg_ref, ln2g_ref, ln2b_ref, *rest, blk0, n_prompt_blocks,
                    n_alias, with_sample):
    yp_ref = rest[n_alias]
    i = blk0 + pl.program_id(0)
    m_lo = jnp.zeros((x1_ref.shape[0], HALF), F32)
    m_hi = jnp.zeros((x1_ref.shape[0], HALF), F32)
    for k in range(TOP_K):
        g_lo, g_hi = _unpack_segments([g_ref[k, j] for j in range(N_SEG)])
        gate = rg_ref[:, k:k + 1]
        m_lo = m_lo + gate * g_lo
        m_hi = m_hi + gate * g_hi
    z = DN_ALPHA * x1_ref[...] + jnp.concatenate([m_lo, m_hi], axis=1)
    y = _layer_norm(z, ln2g_ref[...], ln2b_ref[...])

    if with_sample:
        ys_ref = rest[n_alias + 1]

        @pl.when(i < n_prompt_blocks)
        def _p():
            yp_ref[...] = y

        @pl.when(i >= n_prompt_blocks)
        def _s():
            ys_ref[...] = y
    else:
        yp_ref[...] = y


def _combine(x1, g, rg, ln2_g, ln2_b, *, tok0, n_prompt, tf, yp_buf=None):
    n_tok = x1.shape[0]
    piece = g.shape[2]
    n_sample = n_tok - n_prompt
    assert n_prompt % tf == 0 and n_sample % tf == 0 and tok0 % tf == 0 and piece % tf == 0
    npb = n_prompt // tf
    blk0 = tok0 // tf
    with_sample = tok0 + piece > n_prompt
    assert not with_sample or tok0 + piece == n_tok
    alias = () if yp_buf is None else (yp_buf,)
    out_specs = [pl.BlockSpec((tf, D_MODEL), lambda i: (jnp.minimum(blk0 + i, npb - 1), 0))]
    out_shape = [jax.ShapeDtypeStruct((n_prompt, D_MODEL), F32)]
    if with_sample:
        out_specs.append(pl.BlockSpec((tf, D_MODEL), lambda i: (jnp.maximum(blk0 + i - npb, 0), 0)))
        out_shape.append(jax.ShapeDtypeStruct((n_sample, D_MODEL), F32))
    n_in = 5
    return pl.pallas_call(
        functools.partial(_combine_kernel, blk0=blk0, n_prompt_blocks=npb, n_alias=len(alias),
                          with_sample=with_sample),
        grid=(piece // tf,),
        input_output_aliases={n_in + i: i for i in range(len(alias))},
        in_specs=[pl.BlockSpec((tf, D_MODEL), lambda i: (blk0 + i, 0)),
                  pl.BlockSpec((TOP_K, N_SEG, tf, SC_SEG), lambda i: (0, 0, i, 0)),
                  pl.BlockSpec((tf, 128), lambda i: (blk0 + i, 0)),
                  pl.BlockSpec((1, D_MODEL), lambda i: (0, 0)),
                  pl.BlockSpec((1, D_MODEL), lambda i: (0, 0))]
        + [pl.BlockSpec(memory_space=pl.ANY)] * len(alias),
        out_specs=out_specs,
        out_shape=out_shape,
        compiler_params=pltpu.CompilerParams(
            dimension_semantics=("arbitrary",), vmem_limit_bytes=VMEM_LIMIT),
        name="combine",
    )(x1, g, rg, ln2_g, ln2_b, *alias)


def _sc_mesh():
    return plsc.VectorSubcoreMesh(core_axis_name="c", subcore_axis_name="s")


def _sc_scatter_rows(src, dest, n_out):
    n_src, width = src.shape
    n_k = dest.shape[0]
    assert width == SC_SEG and n_src % SC_WINDOW == 0 and dest.shape[1] == n_src

    @pl.kernel(out_type=jax.ShapeDtypeStruct((n_out, width), src.dtype), mesh=_sc_mesh(),
               scratch_types=[])
    def scatter_kernel(src_hbm, idx_hbm, out_hbm):
        def body(src_vmem, idx_vmem):
            for k in range(n_k):
                pltpu.sync_copy(src_vmem, out_hbm.at[idx_vmem.at[k]])

        pltpu.emit_pipeline(
            body, grid=(n_src // SC_WINDOW,),
            in_specs=[pl.BlockSpec((SC_WINDOW, width), lambda i: (i, 0)),
                      pl.BlockSpec((n_k, SC_WINDOW), lambda i: (0, i))],
            out_specs=[],
            core_axis_name=("c", "s"),
            dimension_semantics=(pltpu.PARALLEL,),
        )(src_hbm, idx_hbm)

    return scatter_kernel(src, dest)


def _sc_gather_rows(src, idx):
    n_out = idx.shape[1]
    width = src.shape[1]
    assert width == SC_SEG and n_out % SC_WINDOW == 0

    @pl.kernel(out_type=jax.ShapeDtypeStruct((n_out, width), src.dtype), mesh=_sc_mesh(),
               scratch_types=[])
    def gather_kernel(src_hbm, idx_hbm, out_hbm):
        def body(idx_vmem, out_vmem):
            pltpu.sync_copy(src_hbm.at[idx_vmem.at[0]], out_vmem)

        pltpu.emit_pipeline(
            body, grid=(n_out // SC_WINDOW,),
            in_specs=[pl.BlockSpec((1, SC_WINDOW), lambda i: (0, i))],
            out_specs=[pl.BlockSpec((SC_WINDOW, width), lambda i: (i, 0))],
            core_axis_name=("c", "s"),
            dimension_semantics=(pltpu.PARALLEL,),
        )(idx_hbm, out_hbm)

    return gather_kernel(src, idx)


def _prep_params(w_in, w_dw, b_dw, ln_conv_g, ln_conv_b, w_gate_lr, b_gate, gla_norm_g, w_o,
                 ln1_g, ln1_b, w_router, b_router):
    row = lambda v: v.reshape(1, -1).astype(F32)
    w_in_p = jnp.pad(w_in, ((0, 0), (0, IN_COLS_PAD - w_in.shape[1]))).astype(BF16)
    w_gate = jnp.pad(w_gate_lr, ((0, GATE_PAD - GATE_RANK), (0, 0))).astype(BF16)
    wr_hi = w_router.astype(BF16)
    wr_lo = (w_router - wr_hi.astype(F32)).astype(BF16)
    padr = lambda m: jnp.pad(m, ((0, 0), (0, 128 - N_EXPERTS)))
    return {
        "w_in": w_in_p, "w_dw": jnp.repeat(w_dw, 8, axis=0), "b_dw": row(b_dw), "ln_conv_g": row(ln_conv_g),
        "ln_conv_b": row(ln_conv_b), "w_gate": w_gate, "b_gate": row(b_gate),
        "gla_norm_g": row(gla_norm_g), "w_o": w_o.astype(BF16), "ln1_g": row(ln1_g),
        "ln1_b": row(ln1_b), "w_router": jnp.concatenate([padr(wr_hi), padr(wr_lo)], axis=1),
        "b_router": jnp.pad(row(b_router), ((0, 0), (0, 128 - N_EXPERTS))),
    }


def _dest_kernel(ps_ref, ri_ref, out_ref, *, n_slot):
    idx = ri_ref[0:TOP_K, :]
    dest = ri_ref[TOP_K:2 * TOP_K, :] + pl.program_id(0) * n_slot
    for e in range(N_EXPERTS):
        dest = dest + jnp.where(idx == e, ps_ref[e], 0)
    out_ref[...] = dest


def _moe_layout(ri, counts, *, tr):
    n_tok = ri.shape[1]
    cnt = counts[:, 0].astype(jnp.int32)
    padded = (cnt + EXPERT_BLOCK - 1) // EXPERT_BLOCK * EXPERT_BLOCK
    pad_end = jnp.cumsum(padded)
    pad_start = pad_end - padded
    nb = -(-(n_tok * TOP_K) // EXPERT_BLOCK) + N_EXPERTS
    n_t = n_tok // tr
    dest_seg = pl.pallas_call(
        functools.partial(_dest_kernel, n_slot=nb * EXPERT_BLOCK),
        grid_spec=pltpu.PrefetchScalarGridSpec(
            num_scalar_prefetch=1,
            grid=(N_SEG, n_t),
            in_specs=[pl.BlockSpec((2 * TOP_K, tr), lambda j, i, ps: (0, i))],
            out_specs=pl.BlockSpec((TOP_K, tr), lambda j, i, ps: (0, j * n_t + i)),
        ),
        out_shape=jax.ShapeDtypeStruct((TOP_K, N_SEG * n_tok), jnp.int32),
        name="slot_index",
    )(pad_start.astype(jnp.int32), ri)
    block_start = (jnp.concatenate([pad_start, pad_end[-1:]]) // EXPERT_BLOCK).astype(jnp.int32)
    n_used = block_start[-1:]
    return dest_seg, block_start, n_used, nb


def kernel(x_prompt, x_sample, state_conv, state_gla, w_in, w_dw, b_dw, ln_conv_g, ln_conv_b,
           w_gate_lr, b_gate, gla_norm_g, w_o, ln1_g, ln1_b, w_router, b_router, w_gu, b_gu,
           w_down, b_down, ln2_g, ln2_b):
    assert w_in.shape[0] == DEPTH
    l = 0
    p = _prep_params(w_in[l], w_dw[l], b_dw[l], ln_conv_g[l], ln_conv_b[l], w_gate_lr[l],
                     b_gate[l], gla_norm_g[l], w_o[l], ln1_g[l], ln1_b[l], w_router[l],
                     b_router[l])
    bp, tp, _ = x_prompt.shape
    bs, ts, _ = x_sample.shape
    zc = jnp.zeros((bp, CONV_W - 1, CONV_CH), F32)
    zs = jnp.zeros((bp, GLA_HEADS, GLA_DK, GLA_DV), F32)
    n_prompt = bp * tp
    n_tok = n_prompt + bs * ts
    x1p, x1, lg, conv_p, gla_p = _mixer(x_prompt, zc, zs, p, tm=512, n_tok_total=n_tok,
                                        tok_offset=0)
    x1p, x1, lg, conv_s, gla_s = _mixer(x_sample, state_conv[l], state_gla[l], p, tm=ts,
                                        n_tok_total=n_tok, tok_offset=n_prompt,
                                        token_bufs=(x1p, x1, lg),
                                        sb=SAMPLE_STREAMS if bs % SAMPLE_STREAMS == 0 else 1)

    tr = 768 if n_tok % 768 == 0 else 256
    ri, rg, counts = _route(lg, tr=tr)
    dest_seg, block_start, n_used, nb = _moe_layout(ri, counts, tr=n_tok)
    n_slot = nb * EXPERT_BLOCK
    xs = _sc_scatter_rows(x1p.reshape(N_SEG * n_tok, SC_SEG), dest_seg, N_SEG * n_slot)
    yb = _ffn(block_start, n_used, xs.reshape(N_SEG, n_slot, SC_SEG), w_gu[l], b_gu[l], w_down[l],
              b_down[l])
    tf = 256
    n_blk = n_prompt // tf
    last_blk = min(LAST_PIECE_BLOCKS, n_blk // 2)
    body_blk = n_blk - last_blk
    n_body = min(COMBINE_PIECES - 1, body_blk)
    bounds = [(p * body_blk // n_body) * tf for p in range(n_body)] + [body_blk * tf, n_tok]
    dest_kjt = dest_seg.reshape(TOP_K, N_SEG, n_tok)
    yp = None
    for t0, t1 in zip(bounds[:-1], bounds[1:]):
        g = _sc_gather_rows(yb.reshape(N_SEG * n_slot, SC_SEG),
                            dest_kjt[:, :, t0:t1].reshape(1, -1))
        outs = _combine(x1, g.reshape(TOP_K, N_SEG, t1 - t0, SC_SEG), rg,
                        ln2_g[l].reshape(1, -1), ln2_b[l].reshape(1, -1), tok0=t0,
                        n_prompt=n_prompt, tf=tf, yp_buf=yp)
        yp = outs[0]
    ys = outs[1]
    return (yp.reshape(bp, tp, D_MODEL), ys.reshape(bs, ts, D_MODEL),
            conv_p[None], gla_p[None], conv_s[None], gla_s[None])
```

```python
import functools

import jax
import jax.numpy as jnp
from jax import lax
from jax.experimental import pallas as pl
from jax.experimental.pallas import tpu as pltpu
from jax.experimental.pallas import tpu_sc as plsc

F32 = jnp.float32
BF16 = jnp.bfloat16

D_MODEL = 1024
CHUNK = 64
SUB = 16
DIAG = 8
CONV_CH = 512
CONV_W = 31
HALO = 32
PROMPT_TILE = 1024
MAX_STATIC_TRIPS = 4
GLU_ROWS = 128
CONV_BLOCK = 64
SAMPLE_STREAMS = 8
LAST_PIECE_BLOCKS = 4
COMBINE_PIECES = 8
TRIP_BLOCKS = 4
IN_AHEAD = TRIP_BLOCKS
IN_RING = 2 * TRIP_BLOCKS
OUT_RING = 2 * TRIP_BLOCKS
GLA_HEADS = 4
GLA_DK = 64
GLA_DV = 128
GLA_K = GLA_HEADS * GLA_DK
GLA_V = GLA_HEADS * GLA_DV
GATE_RANK = 16
GATE_PAD = 128
GATE_NORM = 16.0
MIX_W = CONV_CH + GLA_V
OFF_UV = 0
OFF_UG = OFF_UV + CONV_CH
OFF_Q = OFF_UG + CONV_CH
OFF_K = OFF_Q + GLA_K
OFF_V = OFF_K + GLA_K
OFF_G = OFF_V + GLA_V
OFF_A = OFF_G + GLA_V
IN_COLS_PAD = OFF_A + GATE_PAD
N_EXPERTS = 32
TOP_K = 4
D_FF = 1024
SWIGLU_LIMIT = 7.0
SWIGLU_ALPHA = 1.702
EXPERT_BLOCK = 256
LN_EPS = 1e-5
DEPTH = 1
DN_ALPHA = (2 * DEPTH) ** 0.25
VMEM_LIMIT = 56 * 1024 * 1024
HALF = D_MODEL // 2
SC_SEG = 256
SC_WINDOW = 128
N_SEG = HALF // SC_SEG
U32 = jnp.uint32


def _dot(a, b):
    return jnp.dot(a, b, preferred_element_type=F32)


def _dot_nt(a, b):
    return lax.dot_general(a, b, (((1,), (1,)), ((), ())), preferred_element_type=F32)


def _dot_tn(a, b):
    return lax.dot_general(a, b, (((0,), (0,)), ((), ())), preferred_element_type=F32)


def _layer_norm(x, g, b):
    mu = jnp.mean(x, axis=-1, keepdims=True)
    xc = x - mu
    var = jnp.mean(xc * xc, axis=-1, keepdims=True)
    return xc * lax.rsqrt(var + LN_EPS) * g + b


def _sigmoid(x):
    return 1.0 / (1.0 + jnp.exp(-x))


def _split_bf16(x):
    hi = x.astype(BF16)
    lo = (x - hi.astype(F32)).astype(BF16)
    return hi, lo


def _pack_segments(x):
    bits = pltpu.bitcast(x.astype(BF16).astype(F32), U32)
    word = (bits[:, :HALF] >> 16) | bits[:, HALF:]
    return [word[:, j * SC_SEG:(j + 1) * SC_SEG] for j in range(N_SEG)]


def _unpack_segments(segs):
    word = jnp.concatenate(segs, axis=1)
    lo = pltpu.bitcast(word << 16, F32)
    hi = pltpu.bitcast(word & jnp.uint32(0xFFFF0000), F32)
    return lo, hi


def _mixer_kernel(x_ref, cs_ref, gs_ref, w_in_ref, w_dw_ref, b_dw_ref, lncg_ref, lncb_ref,
                  wg_ref, bg_ref, gng_ref, w_o_ref, ln1g_ref, ln1b_ref, wr_ref, br_ref, *rest,
                  tm, chunk, n_alias, sb):
    (x1p_ref, x1_ref, logit_ref, conv_out_ref, gla_out_ref,
     ubuf, qbuf, kbuf, labuf, vbuf, gbuf, mixbuf, st_ref) = rest[n_alias:]
    t = pl.program_id(1)
    nt = pl.num_programs(1)
    n_chunks = tm // chunk
    n_sub = chunk // SUB
    n_rows = sb * tm
    us = HALO + tm

    @pl.when(t == 0)
    def _init():
        for s in range(sb):
            ubuf[s * us:s * us + HALO - (CONV_W - 1), :] = jnp.zeros(
                (HALO - (CONV_W - 1), CONV_CH), F32)
            ubuf[s * us + HALO - (CONV_W - 1):s * us + HALO, :] = cs_ref[s]
            for h in range(GLA_HEADS):
                st_ref[s * GLA_DV:(s + 1) * GLA_DV, h * GLA_DK:(h + 1) * GLA_DK] = gs_ref[s, h].T

    x = x_ref[...].reshape(n_rows, D_MODEL)
    xb = x.astype(BF16)

    gp = min(GLU_ROWS, tm) if sb == 1 else n_rows
    for r0 in range(0, n_rows, gp):
        hv = _dot(xb[r0:r0 + gp], w_in_ref[:, OFF_UV:OFF_UG])
        hg = _dot(xb[r0:r0 + gp], w_in_ref[:, OFF_UG:OFF_Q])
        u = hv * _sigmoid(hg)
        for s in range(sb):
            lo = max(r0, s * tm)
            hi = min(r0 + gp, (s + 1) * tm)
            if lo < hi:
                ubuf[s * us + HALO + lo - s * tm:s * us + HALO + hi - s * tm, :] = u[lo - r0:hi - r0]
    lead = HALO - (CONV_W - 1)

    def proj_q(r0, n):
        qbuf[r0:r0 + n, :] = _dot(xb[r0:r0 + n], w_in_ref[:, OFF_Q:OFF_K]) * (GLA_DK ** -0.5)

    def proj_k(r0, n):
        kbuf[r0:r0 + n, :] = _dot(xb[r0:r0 + n], w_in_ref[:, OFF_K:OFF_V])

    def proj_v(half, r0, n):
        lo = half * (GLA_V // 2)
        vbuf[r0:r0 + n, lo:lo + GLA_V // 2] = _dot(
            xb[r0:r0 + n], w_in_ref[:, OFF_V + lo:OFF_V + lo + GLA_V // 2])

    def proj_g(half, r0, n):
        lo = half * (GLA_V // 2)
        gbuf[r0:r0 + n, lo:lo + GLA_V // 2] = _dot(
            xb[r0:r0 + n], w_in_ref[:, OFF_G + lo:OFF_G + lo + GLA_V // 2])

    def proj_gate(r0, n):
        ha = _dot(xb[r0:r0 + n], w_in_ref[:, OFF_A:IN_COLS_PAD])
        a = _dot(ha.astype(BF16), wg_ref[...]) + bg_ref[...]
        labuf[r0:r0 + n, :] = ((jnp.minimum(a, 0.0) - jnp.log(1.0 + jnp.exp(-jnp.abs(a))))
                               * (1.0 / GATE_NORM))

    proj_jobs = [proj_q, proj_k, functools.partial(proj_v, 0), functools.partial(proj_v, 1),
                 functools.partial(proj_g, 0), functools.partial(proj_g, 1), proj_gate]

    cb = min(CONV_BLOCK, tm)

    def conv_block(stream, r0):
        acc = jnp.zeros((cb, CONV_CH), F32) + b_dw_ref[...]
        for r in range(8):
            rows = cb if r == 0 else cb + 8
            part = None
            for s in range(r, lead + CONV_W, 8):
                if s < lead:
                    continue
                w8 = w_dw_ref[8 * (s - lead):8 * (s - lead) + 8, :]
                u3 = ubuf[pl.ds(stream * us + r0 + (s - r), rows), :].reshape(
                    rows // 8, 8, CONV_CH)
                term = u3 * w8
                part = term if part is None else part + term
            acc = acc + part.reshape(rows, CONV_CH)[r:r + cb, :]
        cact = _layer_norm(acc, lncg_ref[...], lncb_ref[...])
        cact = cact * _sigmoid(cact)
        mixbuf[pl.ds(stream * tm + r0, cb), 0:CONV_CH] = cact.astype(BF16)

    per_trip = 4 if n_chunks % 4 == 0 else 1
    n_trips = n_chunks // per_trip
    static_trips = 2 <= n_trips <= MAX_STATIC_TRIPS and sb == 1
    trip_rows = per_trip * chunk
    loop_blocks = (tm // 2) // (cb * n_trips) if n_chunks >= 4 else 0
    pre_blocks = tm // cb - loop_blocks * n_trips
    rows_first = trip_rows if static_trips else n_rows
    pre = [(s, blk * cb) for s in range(sb) for blk in range(pre_blocks)]
    for i in range(max(len(pre), len(proj_jobs))):
        if i < len(proj_jobs):
            proj_jobs[i](0, rows_first)
        if i < len(pre):
            conv_block(*pre[i])

    lane_k = lax.broadcasted_iota(jnp.int32, (1, GLA_K), 1) // GLA_DK
    row_c = lax.broadcasted_iota(jnp.int32, (chunk, 1), 0)
    tri = (lax.broadcasted_iota(jnp.int32, (chunk, chunk), 0)
           >= lax.broadcasted_iota(jnp.int32, (chunk, chunk), 1)).astype(BF16)
    e2 = (lax.broadcasted_iota(jnp.int32, (GLA_K, GLA_V), 0) // GLA_DK
          == lax.broadcasted_iota(jnp.int32, (GLA_K, GLA_V), 1) // GLA_DV).astype(BF16)
    row_s = lax.broadcasted_iota(jnp.int32, (GLA_HEADS * chunk, 1), 0)
    blk_s = (row_s % chunk) // SUB
    same_blk = blk_s == lax.broadcasted_iota(jnp.int32, (1, 128), 1) // SUB
    second_half = row_c % SUB >= DIAG
    row_g = lax.broadcasted_iota(jnp.int32, (1, DIAG, 1), 1)
    lane_v = lax.broadcasted_iota(jnp.int32, (1, GLA_V), 1) % GLA_DV

    def head_stack(m):
        return jnp.concatenate(
            [jnp.where(lane_k == h, m, 0.0) for h in range(GLA_HEADS)], axis=0).astype(BF16)

    def unstack(r, width):
        return jnp.concatenate(
            [r[h * chunk:(h + 1) * chunk, h * width:(h + 1) * width] for h in range(GLA_HEADS)],
            axis=1)

    def chunk_step(stream, c, carry):
        base = stream * tm + (c * chunk if isinstance(c, int) else pl.multiple_of(c * chunk, chunk))
        st_rows = slice(stream * GLA_DV, (stream + 1) * GLA_DV)
        q_c = qbuf[pl.ds(base, chunk), :]
        k_c = kbuf[pl.ds(base, chunk), :]
        v_c = vbuf[pl.ds(base, chunk), :]
        v_b = v_c.astype(BF16)
        la_c = labuf[pl.ds(base, chunk), :]
        la_hi, la_lo = _split_bf16(la_c)
        b = _dot(tri, la_hi) + _dot(tri, la_lo)
        b_last = b[chunk - 1:chunk, :]
        st = st_ref[st_rows, :]

        qe = q_c * jnp.exp(b)
        o2 = _dot_nt(head_stack(qe), st.astype(BF16))
        o = jnp.concatenate([o2[h * chunk:(h + 1) * chunk, :] for h in range(GLA_HEADS)], axis=1)

        a_off = None
        if n_sub > 1:
            r_rows = [b[0:SUB, :]] + [jnp.broadcast_to(b[SUB * i - 1:SUB * i, :], (SUB, GLA_K))
                                      for i in range(1, n_sub)]
            r_q = jnp.concatenate(r_rows, axis=0)
            q_t = jnp.where(row_c >= SUB, q_c * jnp.exp(jnp.minimum(b - r_q, 0.0)), 0.0)
            k_parts = []
            for i in range(1, n_sub):
                r_i = b[SUB * i - 1:SUB * i, :]
                k_t = jnp.where(row_c < SUB * i, k_c * jnp.exp(jnp.minimum(r_i - b, 0.0)), 0.0)
                k_parts.append(k_t)
                k_parts.append(jnp.zeros((128 - chunk, GLA_K), F32))
            k_cat = jnp.concatenate(k_parts, axis=0).astype(BF16)
            r = _dot_nt(head_stack(q_t), k_cat)
            a_off = r[:, 0:128]
            for i in range(2, n_sub):
                a_off = jnp.where(blk_s == i, r[:, (i - 1) * 128:i * 128], a_off)

        r_b = jnp.concatenate(
            [jnp.broadcast_to(b[SUB * i + DIAG - 1:SUB * i + DIAG, :], (SUB, GLA_K))
             for i in range(n_sub)], axis=0)
        q_b = jnp.where(second_half, q_c * jnp.exp(jnp.minimum(b - r_b, 0.0)), 0.0)
        k_b = jnp.where(second_half, 0.0, k_c * jnp.exp(jnp.minimum(r_b - b, 0.0)))
        k_b = jnp.concatenate([k_b, jnp.zeros((128 - chunk, GLA_K), F32)], axis=0).astype(BF16)
        a_half = jnp.where(same_blk, _dot_nt(head_stack(q_b), k_b), 0.0)
        a_off = a_half if a_off is None else a_off + a_half

        q3 = q_c.reshape(chunk // DIAG, DIAG, GLA_K)
        k3 = k_c.reshape(chunk // DIAG, DIAG, GLA_K)
        b3 = b.reshape(chunk // DIAG, DIAG, GLA_K)
        ps = []
        for d in range(DIAG):
            k_s = k3 if d == 0 else pltpu.roll(k3, d, axis=1)
            b_s = b3 if d == 0 else pltpu.roll(b3, d, axis=1)
            p = jnp.where(row_g >= d, q3 * k_s * jnp.exp(jnp.minimum(b3 - b_s, 0.0)), 0.0)
            ps.append(p.reshape(chunk, GLA_K).astype(BF16))
        w_all = _dot(jnp.concatenate(ps, axis=0), e2)
        a_d = jnp.zeros((chunk, GLA_V), F32)
        for d in range(DIAG):
            a_d = jnp.where(lane_v == row_c - d, w_all[d * chunk:(d + 1) * chunk, :], a_d)
        a_off = a_off + jnp.concatenate(
            [a_d[:, h * GLA_DV:(h + 1) * GLA_DV] for h in range(GLA_HEADS)], axis=0)
        o1 = _dot(a_off[:, 0:chunk].astype(BF16), v_b)
        o = o + unstack(o1, GLA_DV)

        ke = k_c * jnp.exp(b_last - b)
        r2 = _dot_tn(v_b, ke.astype(BF16))
        upd = jnp.zeros((GLA_DV, GLA_K), F32)
        for h in range(GLA_HEADS):
            upd = upd + jnp.where(lane_k == h, r2[h * GLA_DV:(h + 1) * GLA_DV, :], 0.0)
        st_ref[st_rows, :] = st * jnp.exp(b_last) + upd

        g_c = gbuf[pl.ds(base, chunk), :]
        outs = []
        for h in range(GLA_HEADS):
            oh = o[:, h * GLA_DV:(h + 1) * GLA_DV]
            ms = jnp.mean(oh * oh, axis=-1, keepdims=True)
            outs.append(oh * lax.rsqrt(ms + LN_EPS))
        on = jnp.concatenate(outs, axis=1) * gng_ref[...]
        on = on * (g_c * _sigmoid(g_c))
        mixbuf[pl.ds(base, chunk), CONV_CH:MIX_W] = on.astype(BF16)
        return carry

    def trip(stream, i, carry, extra_jobs=()):
        for j in range(max(per_trip, loop_blocks, len(extra_jobs))):
            if j < len(extra_jobs):
                extra_jobs[j]((i + 1) * trip_rows, trip_rows)
            if j < loop_blocks:
                r0 = (pre_blocks + i * loop_blocks + j) * cb
                conv_block(stream, r0 if isinstance(r0, int) else pl.multiple_of(r0, cb))
            if j < per_trip:
                carry = chunk_step(stream, i * per_trip + j, carry)
        return carry

    def merge_rows(r0, n):
        y = _dot(mixbuf[r0:r0 + n, :], w_o_ref[...])
        x1 = _layer_norm(DN_ALPHA * x[r0:r0 + n, :] + y, ln1g_ref[...], ln1b_ref[...])
        x1_ref[r0:r0 + n, :] = x1
        for j, seg in enumerate(_pack_segments(x1)):
            x1p_ref[j, r0:r0 + n, :] = seg
        x_hi, x_lo = _split_bf16(x1)
        lg = _dot(x_hi, wr_ref[...])
        logit_ref[r0:r0 + n, :] = (lg[:, 0:128] + lg[:, 128:256] + _dot(x_lo, wr_ref[:, 0:128])
                                   + br_ref[...])

    if static_trips:
        for i in range(n_trips):
            trip(0, i, 0, tuple(proj_jobs) if i + 1 < n_trips else ())
            merge_rows(i * trip_rows, trip_rows)
    else:
        for s in range(sb):
            if n_trips == 1:
                trip(s, 0, 0)
            else:
                lax.fori_loop(0, n_trips, functools.partial(trip, s), 0)
        merge_rows(0, n_rows)

    tails = []
    for s in range(sb):
        tail = ubuf[s * us + tm:s * us + tm + HALO, :]
        ubuf[s * us:s * us + HALO, :] = tail
        tails.append(tail)

    @pl.when(t == nt - 1)
    def _state_out():
        for s in range(sb):
            conv_out_ref[s] = tails[s][HALO - (CONV_W - 1):, :]
            st = st_ref[s * GLA_DV:(s + 1) * GLA_DV, :]
            for h in range(GLA_HEADS):
                gla_out_ref[s, h] = st[:, h * GLA_DK:(h + 1) * GLA_DK].T


def _mixer(x, conv_state, gla_state, p, *, tm, n_tok_total, tok_offset, token_bufs=None, sb=1):
    bsz, seq, _ = x.shape
    chunk = min(CHUNK, seq)
    rows = sb * tm
    assert seq % tm == 0 and tm % chunk == 0 and chunk % SUB == 0 and tok_offset % rows == 0
    assert bsz % sb == 0 and (sb == 1 or seq == tm)
    nt = seq // tm
    blk0 = tok_offset // rows
    full = lambda shape: pl.BlockSpec(shape, lambda b, t: (0,) * len(shape))
    alias = () if token_bufs is None else tuple(token_bufs)
    kern = functools.partial(_mixer_kernel, tm=tm, chunk=chunk, n_alias=len(alias), sb=sb)
    n_in = 16
    return pl.pallas_call(
        kern,
        grid=(bsz // sb, nt),
        input_output_aliases={n_in + i: i for i in range(len(alias))},
        in_specs=[
            pl.BlockSpec((sb, tm, D_MODEL), lambda b, t: (b, t, 0)),
            pl.BlockSpec((sb, CONV_W - 1, CONV_CH), lambda b, t: (b, 0, 0)),
            pl.BlockSpec((sb, GLA_HEADS, GLA_DK, GLA_DV), lambda b, t: (b, 0, 0, 0)),
            full((D_MODEL, IN_COLS_PAD)),
            full((8 * CONV_W, CONV_CH)),
            full((1, CONV_CH)),
            full((1, CONV_CH)),
            full((1, CONV_CH)),
            full((GATE_PAD, GLA_K)),
            full((1, GLA_K)),
            full((1, GLA_V)),
            full((MIX_W, D_MODEL)),
            full((1, D_MODEL)),
            full((1, D_MODEL)),
            full((D_MODEL, 256)),
            full((1, 128)),
        ] + [pl.BlockSpec(memory_space=pl.ANY)] * len(alias),
        out_specs=[
            pl.BlockSpec((N_SEG, rows, SC_SEG), lambda b, t: (0, blk0 + b * nt + t, 0)),
            pl.BlockSpec((rows, D_MODEL), lambda b, t: (blk0 + b * nt + t, 0)),
            pl.BlockSpec((rows, 128), lambda b, t: (blk0 + b * nt + t, 0)),
            pl.BlockSpec((sb, CONV_W - 1, CONV_CH), lambda b, t: (b, 0, 0)),
            pl.BlockSpec((sb, GLA_HEADS, GLA_DK, GLA_DV), lambda b, t: (b, 0, 0, 0)),
        ],
        out_shape=[
            jax.ShapeDtypeStruct((N_SEG, n_tok_total, SC_SEG), U32),
            jax.ShapeDtypeStruct((n_tok_total, D_MODEL), F32),
            jax.ShapeDtypeStruct((n_tok_total, 128), F32),
            jax.ShapeDtypeStruct((bsz, CONV_W - 1, CONV_CH), F32),
            jax.ShapeDtypeStruct((bsz, GLA_HEADS, GLA_DK, GLA_DV), F32),
        ],
        scratch_shapes=[
            pltpu.VMEM((sb * (HALO + tm), CONV_CH), F32),
            pltpu.VMEM((rows, GLA_K), F32),
            pltpu.VMEM((rows, GLA_K), F32),
            pltpu.VMEM((rows, GLA_K), F32),
            pltpu.VMEM((rows, GLA_V), F32),
            pltpu.VMEM((rows, GLA_V), F32),
            pltpu.VMEM((rows, MIX_W), BF16),
            pltpu.VMEM((sb * GLA_DV, GLA_K), F32),
        ],
        compiler_params=pltpu.CompilerParams(
            dimension_semantics=("arbitrary", "arbitrary"),
            vmem_limit_bytes=VMEM_LIMIT),
        name="mixer",
    )(x, conv_state, gla_state, p["w_in"], p["w_dw"], p["b_dw"], p["ln_conv_g"], p["ln_conv_b"],
      p["w_gate"], p["b_gate"], p["gla_norm_g"], p["w_o"], p["ln1_g"], p["ln1_b"],
      p["w_router"], p["b_router"], *alias)


def _route_kernel(lg_ref, ri_ref, rg_ref, cnt_ref, carry_ref, *, tr):
    i = pl.program_id(0)

    @pl.when(i == 0)
    def _init():
        carry_ref[...] = jnp.zeros((N_EXPERTS, 128), F32)

    l = lg_ref[...].T[0:N_EXPERTS, :]
    row = lax.broadcasted_iota(jnp.int32, (N_EXPERTS, tr), 0)
    hots, vals, idxs = [], [], []
    for _ in range(TOP_K):
        m = jnp.max(l, axis=0, keepdims=True)
        ik = jnp.min(jnp.where(l == m, row, N_EXPERTS), axis=0, keepdims=True)
        hot = row == ik
        hots.append(hot)
        vals.append(m)
        idxs.append(ik)
        l = jnp.where(hot, -jnp.inf, l)
    es = [jnp.exp(v - vals[0]) for v in vals]
    denom = es[0] + es[1] + es[2] + es[3]
    member = jnp.zeros((N_EXPERTS, tr), F32)
    for hot in hots:
        member = member + hot.astype(F32)
    before = (lax.broadcasted_iota(jnp.int32, (tr, tr), 0)
              < lax.broadcasted_iota(jnp.int32, (tr, tr), 1)).astype(BF16)
    cum = _dot(member.astype(BF16), before) + carry_ref[:, 0:1]
    ranks = [jnp.sum(jnp.where(hot, cum, 0.0), axis=0, keepdims=True).astype(jnp.int32)
             for hot in hots]
    ri_ref[...] = jnp.concatenate(idxs + ranks, axis=0)
    gates = jnp.concatenate([e / denom for e in es] + [jnp.zeros((128 - TOP_K, tr), F32)], axis=0)
    rg_ref[...] = gates.T
    carry_ref[...] = carry_ref[...] + jnp.sum(member, axis=1, keepdims=True)

    @pl.when(i == pl.num_programs(0) - 1)
    def _fin():
        cnt_ref[...] = carry_ref[...]


def _route(logits, *, tr):
    n_tok = logits.shape[0]
    assert n_tok % tr == 0
    return pl.pallas_call(
        functools.partial(_route_kernel, tr=tr),
        grid=(n_tok // tr,),
        in_specs=[pl.BlockSpec((tr, 128), lambda i: (i, 0))],
        out_specs=[pl.BlockSpec((2 * TOP_K, tr), lambda i: (0, i)),
                   pl.BlockSpec((tr, 128), lambda i: (i, 0)),
                   pl.BlockSpec((N_EXPERTS, 128), lambda i: (0, 0))],
        out_shape=[jax.ShapeDtypeStruct((2 * TOP_K, n_tok), jnp.int32),
                   jax.ShapeDtypeStruct((n_tok, 128), F32),
                   jax.ShapeDtypeStruct((N_EXPERTS, 128), F32)],
        scratch_shapes=[pltpu.VMEM((N_EXPERTS, 128), F32)],
        compiler_params=pltpu.CompilerParams(dimension_semantics=("arbitrary",)),
        name="route",
    )(logits)


def _ffn_kernel(gs_ref, nu_ref, xs_hbm, wgu_ref, bgu_ref, wdn_ref, bdn_ref, yb_hbm,
                wgu_bf, wdn_bf, xin, yout, in_sem, out_sem):
    e = pl.program_id(0)
    n_used = nu_ref[0]
    g_first = gs_ref[e]
    g_end = gs_ref[e + 1]

    def in_copy(g, slot):
        rows = pl.ds(pl.multiple_of(g * EXPERT_BLOCK, EXPERT_BLOCK), EXPERT_BLOCK)
        return pltpu.make_async_copy(xs_hbm.at[:, rows, :], xin.at[slot], in_sem.at[slot])

    def out_copy(g, slot):
        rows = pl.ds(pl.multiple_of(g * EXPERT_BLOCK, EXPERT_BLOCK), EXPERT_BLOCK)
        return pltpu.make_async_copy(yout.at[slot], yb_hbm.at[:, rows, :], out_sem.at[slot])

    @pl.when(e == 0)
    def _prime():
        for g in range(IN_AHEAD):
            @pl.when(g < n_used)
            def _():
                in_copy(g, g).start()

    @pl.when(g_end > g_first)
    def _cast():
        wgu_bf[...] = wgu_ref[0].astype(BF16)
        wdn_bf[...] = wdn_ref[0].astype(BF16)

    def blocks(g0, n):
        for j in range(n):
            g = g0 + j
            in_copy(g, g % IN_RING).wait()
        for j in range(n):
            g = g0 + IN_AHEAD + j

            @pl.when(g < n_used)
            def _prefetch():
                in_copy(g, g % IN_RING).start()
        for j in range(n):
            g = g0 + j

            @pl.when(g >= OUT_RING)
            def _free_out():
                out_copy(g - OUT_RING, g % OUT_RING).wait()
        for j in range(n):
            g = g0 + j
            x_lo, x_hi = _unpack_segments([xin[g % IN_RING, s] for s in range(N_SEG)])
            hgu = (_dot(x_lo.astype(BF16), wgu_bf[0:HALF, :])
                   + _dot(x_hi.astype(BF16), wgu_bf[HALF:, :]) + bgu_ref[0])
            gate = jnp.minimum(hgu[:, :D_FF], SWIGLU_LIMIT)
            up = jnp.clip(hgu[:, D_FF:], -SWIGLU_LIMIT, SWIGLU_LIMIT)
            act = (up + 1.0) * gate * _sigmoid(SWIGLU_ALPHA * gate)
            yb = _dot(act.astype(BF16), wdn_bf[...]) + bdn_ref[0]
            for s, seg in enumerate(_pack_segments(yb)):
                yout[g % OUT_RING, s] = seg
        for j in range(n):
            g = g0 + j
            out_copy(g, g % OUT_RING).start()

    n_blocks = g_end - g_first
    n_full = n_blocks // TRIP_BLOCKS

    def full_trip(i, carry):
        blocks(g_first + TRIP_BLOCKS * i, TRIP_BLOCKS)
        return carry

    lax.fori_loop(0, n_full, full_trip, 0)
    done = n_full * TRIP_BLOCKS
    size = TRIP_BLOCKS // 2
    while size >= 1:
        @pl.when((n_blocks // size) % 2 == 1)
        def _rest(done=done, size=size):
            blocks(g_first + done, size)
        done = done + jnp.where((n_blocks // size) % 2 == 1, size, 0)
        size //= 2

    @pl.when(e == pl.num_programs(0) - 1)
    def _drain():
        for back in range(OUT_RING, 0, -1):
            @pl.when(n_used >= back)
            def _():
                out_copy(n_used - back, (n_used - back) % OUT_RING).wait()


def _ffn(block_start, n_used, xs, w_gu, b_gu, w_down, b_down):
    n_rows = xs.shape[1]
    grid_spec = pltpu.PrefetchScalarGridSpec(
        num_scalar_prefetch=2,
        grid=(N_EXPERTS,),
        in_specs=[
            pl.BlockSpec(memory_space=pl.ANY),
            pl.BlockSpec((1, D_MODEL, 2 * D_FF), lambda e, gs, nu: (e, 0, 0)),
            pl.BlockSpec((1, 1, 2 * D_FF), lambda e, gs, nu: (e, 0, 0)),
            pl.BlockSpec((1, D_FF, D_MODEL), lambda e, gs, nu: (e, 0, 0)),
            pl.BlockSpec((1, 1, D_MODEL), lambda e, gs, nu: (e, 0, 0)),
        ],
        out_specs=pl.BlockSpec(memory_space=pl.ANY),
        scratch_shapes=[pltpu.VMEM((D_MODEL, 2 * D_FF), BF16),
                        pltpu.VMEM((D_FF, D_MODEL), BF16),
                        pltpu.VMEM((IN_RING, N_SEG, EXPERT_BLOCK, SC_SEG), U32),
                        pltpu.VMEM((OUT_RING, N_SEG, EXPERT_BLOCK, SC_SEG), U32),
                        pltpu.SemaphoreType.DMA((IN_RING,)),
                        pltpu.SemaphoreType.DMA((OUT_RING,))],
    )
    return pl.pallas_call(
        _ffn_kernel,
        grid_spec=grid_spec,
        out_shape=jax.ShapeDtypeStruct((N_SEG, n_rows, SC_SEG), U32),
        compiler_params=pltpu.CompilerParams(
            dimension_semantics=("arbitrary",), vmem_limit_bytes=VMEM_LIMIT),
        name="expert_ffn",
    )(block_start, n_used, xs, w_gu, b_gu.reshape(N_EXPERTS, 1, 2 * D_FF), w_down,
      b_down.reshape(N_EXPERTS, 1, D_MODEL))


def _combine_kernel(x1_ref, g_ref, rg_ref, ln2g_ref, ln2b_ref, *rest, blk0, n_prompt_blocks,
                    n_alias, with_sample):
    yp_ref = rest[n_alias]
    i = blk0 + pl.program_id(0)
    m_lo = jnp.zeros((x1_ref.shape[0], HALF), F32)
    m_hi = jnp.zeros((x1_ref.shape[0], HALF), F32)
    for k in range(TOP_K):
        g_lo, g_hi = _unpack_segments([g_ref[k, j] for j in range(N_SEG)])
        gate = rg_ref[:, k:k + 1]
        m_lo = m_lo + gate * g_lo
        m_hi = m_hi + gate * g_hi
    z = DN_ALPHA * x1_ref[...] + jnp.concatenate([m_lo, m_hi], axis=1)
    y = _layer_norm(z, ln2g_ref[...], ln2b_ref[...])

    if with_sample:
        ys_ref = rest[n_alias + 1]

        @pl.when(i < n_prompt_blocks)
        def _p():
            yp_ref[...] = y

        @pl.when(i >= n_prompt_blocks)
        def _s():
            ys_ref[...] = y
    else:
        yp_ref[...] = y


def _combine(x1, g, rg, ln2_g, ln2_b, *, tok0, n_prompt, tf, yp_buf=None):
    n_tok = x1.shape[0]
    piece = g.shape[2]
    n_sample = n_tok - n_prompt
    assert n_prompt % tf == 0 and n_sample % tf == 0 and tok0 % tf == 0 and piece % tf == 0
    npb = n_prompt // tf
    blk0 = tok0 // tf
    with_sample = tok0 + piece > n_prompt
    assert not with_sample or tok0 + piece == n_tok
    alias = () if yp_buf is None else (yp_buf,)
    out_specs = [pl.BlockSpec((tf, D_MODEL), lambda i: (jnp.minimum(blk0 + i, npb - 1), 0))]
    out_shape = [jax.ShapeDtypeStruct((n_prompt, D_MODEL), F32)]
    if with_sample:
        out_specs.append(pl.BlockSpec((tf, D_MODEL), lambda i: (jnp.maximum(blk0 + i - npb, 0), 0)))
        out_shape.append(jax.ShapeDtypeStruct((n_sample, D_MODEL), F32))
    n_in = 5
    return pl.pallas_call(
        functools.partial(_combine_kernel, blk0=blk0, n_prompt_blocks=npb, n_alias=len(alias),
                          with_sample=with_sample),
        grid=(piece // tf,),
        input_output_aliases={n_in + i: i for i in range(len(alias))},
        in_specs=[pl.BlockSpec((tf, D_MODEL), lambda i: (blk0 + i, 0)),
                  pl.BlockSpec((TOP_K, N_SEG, tf, SC_SEG), lambda i: (0, 0, i, 0)),
                  pl.BlockSpec((tf, 128), lambda i: (blk0 + i, 0)),
                  pl.BlockSpec((1, D_MODEL), lambda i: (0, 0)),
                  pl.BlockSpec((1, D_MODEL), lambda i: (0, 0))]
        + [pl.BlockSpec(memory_space=pl.ANY)] * len(alias),
        out_specs=out_specs,
        out_shape=out_shape,
        compiler_params=pltpu.CompilerParams(
            dimension_semantics=("arbitrary",), vmem_limit_bytes=VMEM_LIMIT),
        name="combine",
    )(x1, g, rg, ln2_g, ln2_b, *alias)


def _sc_mesh():
    return plsc.VectorSubcoreMesh(core_axis_name="c", subcore_axis_name="s")


def _sc_scatter_rows(src, dest, n_out):
    n_src, width = src.shape
    n_k = dest.shape[0]
    assert width == SC_SEG and n_src % SC_WINDOW == 0 and dest.shape[1] == n_src

    @pl.kernel(out_type=jax.ShapeDtypeStruct((n_out, width), src.dtype), mesh=_sc_mesh(),
               scratch_types=[])
    def scatter_kernel(src_hbm, idx_hbm, out_hbm):
        def body(src_vmem, idx_vmem):
            for k in range(n_k):
                pltpu.sync_copy(src_vmem, out_hbm.at[idx_vmem.at[k]])

        pltpu.emit_pipeline(
            body, grid=(n_src // SC_WINDOW,),
            in_specs=[pl.BlockSpec((SC_WINDOW, width), lambda i: (i, 0)),
                      pl.BlockSpec((n_k, SC_WINDOW), lambda i: (0, i))],
            out_specs=[],
            core_axis_name=("c", "s"),
            dimension_semantics=(pltpu.PARALLEL,),
        )(src_hbm, idx_hbm)

    return scatter_kernel(src, dest)


def _sc_gather_rows(src, idx):
    n_out = idx.shape[1]
    width = src.shape[1]
    assert width == SC_SEG and n_out % SC_WINDOW == 0

    @pl.kernel(out_type=jax.ShapeDtypeStruct((n_out, width), src.dtype), mesh=_sc_mesh(),
               scratch_types=[])
    def gather_kernel(src_hbm, idx_hbm, out_hbm):
        def body(idx_vmem, out_vmem):
            pltpu.sync_copy(src_hbm.at[idx_vmem.at[0]], out_vmem)

        pltpu.emit_pipeline(
            body, grid=(n_out // SC_WINDOW,),
            in_specs=[pl.BlockSpec((1, SC_WINDOW), lambda i: (0, i))],
            out_specs=[pl.BlockSpec((SC_WINDOW, width), lambda i: (i, 0))],
            core_axis_name=("c", "s"),
            dimension_semantics=(pltpu.PARALLEL,),
        )(idx_hbm, out_hbm)

    return gather_kernel(src, idx)


def _prep_params(w_in, w_dw, b_dw, ln_conv_g, ln_conv_b, w_gate_lr, b_gate, gla_norm_g, w_o,
                 ln1_g, ln1_b, w_router, b_router):
    row = lambda v: v.reshape(1, -1).astype(F32)
    w_in_p = jnp.pad(w_in, ((0, 0), (0, IN_COLS_PAD - w_in.shape[1]))).astype(BF16)
    w_gate = jnp.pad(w_gate_lr, ((0, GATE_PAD - GATE_RANK), (0, 0))).astype(BF16)
    wr_hi = w_router.astype(BF16)
    wr_lo = (w_router - wr_hi.astype(F32)).astype(BF16)
    padr = lambda m: jnp.pad(m, ((0, 0), (0, 128 - N_EXPERTS)))
    return {
        "w_in": w_in_p, "w_dw": jnp.repeat(w_dw, 8, axis=0), "b_dw": row(b_dw), "ln_conv_g": row(ln_conv_g),
        "ln_conv_b": row(ln_conv_b), "w_gate": w_gate, "b_gate": row(b_gate),
        "gla_norm_g": row(gla_norm_g), "w_o": w_o.astype(BF16), "ln1_g": row(ln1_g),
        "ln1_b": row(ln1_b), "w_router": jnp.concatenate([padr(wr_hi), padr(wr_lo)], axis=1),
        "b_router": jnp.pad(row(b_router), ((0, 0), (0, 128 - N_EXPERTS))),
    }


def _dest_kernel(ps_ref, ri_ref, out_ref, *, n_slot):
    idx = ri_ref[0:TOP_K, :]
    dest = ri_ref[TOP_K:2 * TOP_K, :] + pl.program_id(0) * n_slot
    for e in range(N_EXPERTS):
        dest = dest + jnp.where(idx == e, ps_ref[e], 0)
    out_ref[...] = dest


def _moe_layout(ri, counts, *, tr):
    n_tok = ri.shape[1]
    cnt = counts[:, 0].astype(jnp.int32)
    padded = (cnt + EXPERT_BLOCK - 1) // EXPERT_BLOCK * EXPERT_BLOCK
    pad_end = jnp.cumsum(padded)
    pad_start = pad_end - padded
    nb = -(-(n_tok * TOP_K) // EXPERT_BLOCK) + N_EXPERTS
    n_t = n_tok // tr
    dest_seg = pl.pallas_call(
        functools.partial(_dest_kernel, n_slot=nb * EXPERT_BLOCK),
        grid_spec=pltpu.PrefetchScalarGridSpec(
            num_scalar_prefetch=1,
            grid=(N_SEG, n_t),
            in_specs=[pl.BlockSpec((2 * TOP_K, tr), lambda j, i, ps: (0, i))],
            out_specs=pl.BlockSpec((TOP_K, tr), lambda j, i, ps: (0, j * n_t + i)),
        ),
        out_shape=jax.ShapeDtypeStruct((TOP_K, N_SEG * n_tok), jnp.int32),
        name="slot_index",
    )(pad_start.astype(jnp.int32), ri)
    block_start = (jnp.concatenate([pad_start, pad_end[-1:]]) // EXPERT_BLOCK).astype(jnp.int32)
    n_used = block_start[-1:]
    return dest_seg, block_start, n_used, nb


def kernel(x_prompt, x_sample, state_conv, state_gla, w_in, w_dw, b_dw, ln_conv_g, ln_conv_b,
           w_gate_lr, b_gate, gla_norm_g, w_o, ln1_g, ln1_b, w_router, b_router, w_gu, b_gu,
           w_down, b_down, ln2_g, ln2_b):
    assert w_in.shape[0] == DEPTH
    l = 0
    p = _prep_params(w_in[l], w_dw[l], b_dw[l], ln_conv_g[l], ln_conv_b[l], w_gate_lr[l],
                     b_gate[l], gla_norm_g[l], w_o[l], ln1_g[l], ln1_b[l], w_router[l],
                     b_router[l])
    bp, tp, _ = x_prompt.shape
    bs, ts, _ = x_sample.shape
    zc = jnp.zeros((bp, CONV_W - 1, CONV_CH), F32)
    zs = jnp.zeros((bp, GLA_HEADS, GLA_DK, GLA_DV), F32)
    n_prompt = bp * tp
    n_tok = n_prompt + bs * ts
    x1p, x1, lg, conv_p, gla_p = _mixer(x_prompt, zc, zs, p, tm=PROMPT_TILE, n_tok_total=n_tok,
                                        tok_offset=0)
    x1p, x1, lg, conv_s, gla_s = _mixer(x_sample, state_conv[l], state_gla[l], p, tm=ts,
                                        n_tok_total=n_tok, tok_offset=n_prompt,
                                        token_bufs=(x1p, x1, lg),
                                        sb=SAMPLE_STREAMS if bs % SAMPLE_STREAMS == 0 else 1)

    tr = 768 if n_tok % 768 == 0 else 256
    ri, rg, counts = _route(lg, tr=tr)
    dest_seg, block_start, n_used, nb = _moe_layout(ri, counts, tr=n_tok)
    n_slot = nb * EXPERT_BLOCK
    xs = _sc_scatter_rows(x1p.reshape(N_SEG * n_tok, SC_SEG), dest_seg, N_SEG * n_slot)
    yb = _ffn(block_start, n_used, xs.reshape(N_SEG, n_slot, SC_SEG), w_gu[l], b_gu[l], w_down[l],
              b_down[l])
    tf = 256
    n_blk = n_prompt // tf
    last_blk = min(LAST_PIECE_BLOCKS, n_blk // 2)
    body_blk = n_blk - last_blk
    n_body = min(COMBINE_PIECES - 1, body_blk)
    bounds = [(p * body_blk // n_body) * tf for p in range(n_body)] + [body_blk * tf, n_tok]
    dest_kjt = dest_seg.reshape(TOP_K, N_SEG, n_tok)
    yp = None
    for t0, t1 in zip(bounds[:-1], bounds[1:]):
        g = _sc_gather_rows(yb.reshape(N_SEG * n_slot, SC_SEG),
                            dest_kjt[:, :, t0:t1].reshape(1, -1))
        outs = _combine(x1, g.reshape(TOP_K, N_SEG, t1 - t0, SC_SEG), rg,
                        ln2_g[l].reshape(1, -1), ln2_b[l].reshape(1, -1), tok0=t0,
                        n_prompt=n_prompt, tf=tf, yp_buf=yp)
        yp = outs[0]
    ys = outs[1]
    return (yp.reshape(bp, tp, D_MODEL), ys.reshape(bs, ts, D_MODEL),
            conv_p[None], gla_p[None], conv_s[None], gla_s[None])
```

```python
import functools

import jax
import jax.numpy as jnp
from jax import lax
from jax.experimental import pallas as pl
from jax.experimental.pallas import tpu as pltpu
from jax.experimental.pallas import tpu_sc as plsc

F32 = jnp.float32
BF16 = jnp.bfloat16

D_MODEL = 1024
CHUNK = 64
SUB = 16
DIAG = 8
CONV_CH = 512
CONV_W = 31
HALO = 32
PROMPT_TILE = 1024
MAX_STATIC_TRIPS = 4
GLU_ROWS = 128
CONV_BLOCK = 64
SAMPLE_STREAMS = 8
LAST_PIECE_BLOCKS = 4
COMBINE_PIECES = 8
TRIP_BLOCKS = 4
IN_AHEAD = TRIP_BLOCKS
IN_RING = 2 * TRIP_BLOCKS
OUT_RING = 2 * TRIP_BLOCKS
GLA_HEADS = 4
GLA_DK = 64
GLA_DV = 128
GLA_K = GLA_HEADS * GLA_DK
GLA_V = GLA_HEADS * GLA_DV
GATE_RANK = 16
GATE_PAD = 128
GATE_NORM = 16.0
MIX_W = CONV_CH + GLA_V
OFF_UV = 0
OFF_UG = OFF_UV + CONV_CH
OFF_Q = OFF_UG + CONV_CH
OFF_K = OFF_Q + GLA_K
OFF_V = OFF_K + GLA_K
OFF_G = OFF_V + GLA_V
OFF_A = OFF_G + GLA_V
IN_COLS_PAD = OFF_A + GATE_PAD
N_EXPERTS = 32
TOP_K = 4
D_FF = 1024
SWIGLU_LIMIT = 7.0
SWIGLU_ALPHA = 1.702
EXPERT_BLOCK = 256
LN_EPS = 1e-5
DEPTH = 1
DN_ALPHA = (2 * DEPTH) ** 0.25
VMEM_LIMIT = 56 * 1024 * 1024
HALF = D_MODEL // 2
SC_SEG = 256
SC_WINDOW = 128
N_SEG = HALF // SC_SEG
U32 = jnp.uint32


def _dot(a, b):
    return jnp.dot(a, b, preferred_element_type=F32)


def _dot_nt(a, b):
    return lax.dot_general(a, b, (((1,), (1,)), ((), ())), preferred_element_type=F32)


def _dot_tn(a, b):
    return lax.dot_general(a, b, (((0,), (0,)), ((), ())), preferred_element_type=F32)


def _layer_norm(x, g, b):
    mu = jnp.mean(x, axis=-1, keepdims=True)
    xc = x - mu
    var = jnp.mean(xc * xc, axis=-1, keepdims=True)
    return xc * lax.rsqrt(var + LN_EPS) * g + b


def _sigmoid(x):
    return 1.0 / (1.0 + jnp.exp(-x))


def _split_bf16(x):
    hi = x.astype(BF16)
    lo = (x - hi.astype(F32)).astype(BF16)
    return hi, lo


def _pack_segments(x):
    bits = pltpu.bitcast(x.astype(BF16).astype(F32), U32)
    word = (bits[:, :HALF] >> 16) | bits[:, HALF:]
    return [word[:, j * SC_SEG:(j + 1) * SC_SEG] for j in range(N_SEG)]


def _unpack_segments(segs):
    word = jnp.concatenate(segs, axis=1)
    lo = pltpu.bitcast(word << 16, F32)
    hi = pltpu.bitcast(word & jnp.uint32(0xFFFF0000), F32)
    return lo, hi


def _mixer_kernel(x_ref, cs_ref, gs_ref, w_in_ref, w_dw_ref, b_dw_ref, lncg_ref, lncb_ref,
                  wg_ref, bg_ref, gng_ref, w_o_ref, ln1g_ref, ln1b_ref, wr_ref, br_ref, *rest,
                  tm, chunk, n_alias, sb):
    (x1p_ref, x1_ref, logit_ref, conv_out_ref, gla_out_ref,
     ubuf, qbuf, kbuf, labuf, vbuf, gbuf, mixbuf, st_ref) = rest[n_alias:]
    t = pl.program_id(1)
    nt = pl.num_programs(1)
    n_chunks = tm // chunk
    n_sub = chunk // SUB
    n_rows = sb * tm
    us = HALO + tm

    @pl.when(t == 0)
    def _init():
        for s in range(sb):
            ubuf[s * us:s * us + HALO - (CONV_W - 1), :] = jnp.zeros(
                (HALO - (CONV_W - 1), CONV_CH), F32)
            ubuf[s * us + HALO - (CONV_W - 1):s * us + HALO, :] = cs_ref[s]
            for h in range(GLA_HEADS):
                st_ref[s * GLA_DV:(s + 1) * GLA_DV, h * GLA_DK:(h + 1) * GLA_DK] = gs_ref[s, h].T

    x = x_ref[...].reshape(n_rows, D_MODEL)
    xb = x.astype(BF16)

    gp = min(GLU_ROWS, tm) if sb == 1 else n_rows
    for r0 in range(0, n_rows, gp):
        hv = _dot(xb[r0:r0 + gp], w_in_ref[:, OFF_UV:OFF_UG])
        hg = _dot(xb[r0:r0 + gp], w_in_ref[:, OFF_UG:OFF_Q])
        u = hv * _sigmoid(hg)
        for s in range(sb):
            lo = max(r0, s * tm)
            hi = min(r0 + gp, (s + 1) * tm)
            if lo < hi:
                ubuf[s * us + HALO + lo - s * tm:s * us + HALO + hi - s * tm, :] = u[lo - r0:hi - r0]
    lead = HALO - (CONV_W - 1)

    def proj_q(r0, n):
        qbuf[r0:r0 + n, :] = _dot(xb[r0:r0 + n], w_in_ref[:, OFF_Q:OFF_K]) * (GLA_DK ** -0.5)

    def proj_k(r0, n):
        kbuf[r0:r0 + n, :] = _dot(xb[r0:r0 + n], w_in_ref[:, OFF_K:OFF_V])

    def proj_v(half, r0, n):
        lo = half * (GLA_V // 2)
        vbuf[r0:r0 + n, lo:lo + GLA_V // 2] = _dot(
            xb[r0:r0 + n], w_in_ref[:, OFF_V + lo:OFF_V + lo + GLA_V // 2])

    def proj_g(half, r0, n):
        lo = half * (GLA_V // 2)
        gbuf[r0:r0 + n, lo:lo + GLA_V // 2] = _dot(
            xb[r0:r0 + n], w_in_ref[:, OFF_G + lo:OFF_G + lo + GLA_V // 2])

    def proj_gate(r0, n):
        ha = _dot(xb[r0:r0 + n], w_in_ref[:, OFF_A:IN_COLS_PAD])
        a = _dot(ha.astype(BF16), wg_ref[...]) + bg_ref[...]
        labuf[r0:r0 + n, :] = ((jnp.minimum(a, 0.0) - jnp.log(1.0 + jnp.exp(-jnp.abs(a))))
                               * (1.0 / GATE_NORM))

    proj_jobs = [proj_q, proj_k, functools.partial(proj_v, 0), functools.partial(proj_v, 1),
                 functools.partial(proj_g, 0), functools.partial(proj_g, 1), proj_gate]

    cb = min(CONV_BLOCK, tm)

    def conv_block(stream, r0):
        acc = jnp.zeros((cb, CONV_CH), F32) + b_dw_ref[...]
        for r in range(8):
            rows = cb if r == 0 else cb + 8
            part = None
            for s in range(r, lead + CONV_W, 8):
                if s < lead:
                    continue
                w8 = w_dw_ref[8 * (s - lead):8 * (s - lead) + 8, :]
                u3 = ubuf[pl.ds(stream * us + r0 + (s - r), rows), :].reshape(
                    rows // 8, 8, CONV_CH)
                term = u3 * w8
                part = term if part is None else part + term
            acc = acc + part.reshape(rows, CONV_CH)[r:r + cb, :]
        cact = _layer_norm(acc, lncg_ref[...], lncb_ref[...])
        cact = cact * _sigmoid(cact)
        mixbuf[pl.ds(stream * tm + r0, cb), 0:CONV_CH] = cact.astype(BF16)

    per_trip = 4 if n_chunks % 4 == 0 else 1
    n_trips = n_chunks // per_trip
    static_trips = 2 <= n_trips <= MAX_STATIC_TRIPS and sb == 1
    trip_rows = per_trip * chunk
    loop_blocks = (tm // 2) // (cb * n_trips) if n_chunks >= 4 else 0
    pre_blocks = tm // cb - loop_blocks * n_trips
    rows_first = trip_rows if static_trips else n_rows
    pre = [(s, blk * cb) for s in range(sb) for blk in range(pre_blocks)]
    for i in range(max(len(pre), len(proj_jobs))):
        if i < len(proj_jobs):
            proj_jobs[i](0, rows_first)
        if i < len(pre):
            conv_block(*pre[i])

    lane_k = lax.broadcasted_iota(jnp.int32, (1, GLA_K), 1) // GLA_DK
    row_c = lax.broadcasted_iota(jnp.int32, (chunk, 1), 0)
    tri = (lax.broadcasted_iota(jnp.int32, (chunk, chunk), 0)
           >= lax.broadcasted_iota(jnp.int32, (chunk, chunk), 1)).astype(BF16)
    e2 = (lax.broadcasted_iota(jnp.int32, (GLA_K, GLA_V), 0) // GLA_DK
          == lax.broadcasted_iota(jnp.int32, (GLA_K, GLA_V), 1) // GLA_DV).astype(BF16)
    row_s = lax.broadcasted_iota(jnp.int32, (GLA_HEADS * chunk, 1), 0)
    blk_s = (row_s % chunk) // SUB
    same_blk = blk_s == lax.broadcasted_iota(jnp.int32, (1, 128), 1) // SUB
    second_half = row_c % SUB >= DIAG
    row_g = lax.broadcasted_iota(jnp.int32, (1, DIAG, 1), 1)
    lane_v = lax.broadcasted_iota(jnp.int32, (1, GLA_V), 1) % GLA_DV

    def head_stack(m):
        return jnp.concatenate(
            [jnp.where(lane_k == h, m, 0.0) for h in range(GLA_HEADS)], axis=0).astype(BF16)

    def unstack(r, width):
        return jnp.concatenate(
            [r[h * chunk:(h + 1) * chunk, h * width:(h + 1) * width] for h in range(GLA_HEADS)],
            axis=1)

    def chunk_step(stream, c, carry):
        base = stream * tm + (c * chunk if isinstance(c, int) else pl.multiple_of(c * chunk, chunk))
        st_rows = slice(stream * GLA_DV, (stream + 1) * GLA_DV)
        q_c = qbuf[pl.ds(base, chunk), :]
        k_c = kbuf[pl.ds(base, chunk), :]
        v_c = vbuf[pl.ds(base, chunk), :]
        v_b = v_c.astype(BF16)
        la_c = labuf[pl.ds(base, chunk), :]
        la_hi, la_lo = _split_bf16(la_c)
        b = _dot(tri, la_hi) + _dot(tri, la_lo)
        b_last = b[chunk - 1:chunk, :]
        st = st_ref[st_rows, :]

        qe = q_c * jnp.exp(b)
        o2 = _dot_nt(head_stack(qe), st.astype(BF16))
        o = jnp.concatenate([o2[h * chunk:(h + 1) * chunk, :] for h in range(GLA_HEADS)], axis=1)

        a_off = None
        if n_sub > 1:
            r_rows = [b[0:SUB, :]] + [jnp.broadcast_to(b[SUB * i - 1:SUB * i, :], (SUB, GLA_K))
                                      for i in range(1, n_sub)]
            r_q = jnp.concatenate(r_rows, axis=0)
            q_t = jnp.where(row_c >= SUB, q_c * jnp.exp(jnp.minimum(b - r_q, 0.0)), 0.0)
            k_parts = []
            for i in range(1, n_sub):
                r_i = b[SUB * i - 1:SUB * i, :]
                k_t = jnp.where(row_c < SUB * i, k_c * jnp.exp(jnp.minimum(r_i - b, 0.0)), 0.0)
                k_parts.append(k_t)
                k_parts.append(jnp.zeros((128 - chunk, GLA_K), F32))
            k_cat = jnp.concatenate(k_parts, axis=0).astype(BF16)
            r = _dot_nt(head_stack(q_t), k_cat)
            a_off = r[:, 0:128]
            for i in range(2, n_sub):
                a_off = jnp.where(blk_s == i, r[:, (i - 1) * 128:i * 128], a_off)

        r_b = jnp.concatenate(
            [jnp.broadcast_to(b[SUB * i + DIAG - 1:SUB * i + DIAG, :], (SUB, GLA_K))
             for i in range(n_sub)], axis=0)
        q_b = jnp.where(second_half, q_c * jnp.exp(jnp.minimum(b - r_b, 0.0)), 0.0)
        k_b = jnp.where(second_half, 0.0, k_c * jnp.exp(jnp.minimum(r_b - b, 0.0)))
        k_b = jnp.concatenate([k_b, jnp.zeros((128 - chunk, GLA_K), F32)], axis=0).astype(BF16)
        a_half = jnp.where(same_blk, _dot_nt(head_stack(q_b), k_b), 0.0)
        a_off = a_half if a_off is None else a_off + a_half

        q3 = q_c.reshape(chunk // DIAG, DIAG, GLA_K)
        k3 = k_c.reshape(chunk // DIAG, DIAG, GLA_K)
        b3 = b.reshape(chunk // DIAG, DIAG, GLA_K)
        ps = []
        for d in range(DIAG):
            k_s = k3 if d == 0 else pltpu.roll(k3, d, axis=1)
            b_s = b3 if d == 0 else pltpu.roll(b3, d, axis=1)
            p = jnp.where(row_g >= d, q3 * k_s * jnp.exp(jnp.minimum(b3 - b_s, 0.0)), 0.0)
            ps.append(p.reshape(chunk, GLA_K).astype(BF16))
        w_all = _dot(jnp.concatenate(ps, axis=0), e2)
        a_d = jnp.zeros((chunk, GLA_V), F32)
        for d in range(DIAG):
            a_d = jnp.where(lane_v == row_c - d, w_all[d * chunk:(d + 1) * chunk, :], a_d)
        a_off = a_off + jnp.concatenate(
            [a_d[:, h * GLA_DV:(h + 1) * GLA_DV] for h in range(GLA_HEADS)], axis=0)
        o1 = _dot(a_off[:, 0:chunk].astype(BF16), v_b)
        o = o + unstack(o1, GLA_DV)

        ke = k_c * jnp.exp(b_last - b)
        r2 = _dot_tn(v_b, ke.astype(BF16))
        upd = jnp.zeros((GLA_DV, GLA_K), F32)
        for h in range(GLA_HEADS):
            upd = upd + jnp.where(lane_k == h, r2[h * GLA_DV:(h + 1) * GLA_DV, :], 0.0)
        st_ref[st_rows, :] = st * jnp.exp(b_last) + upd

        g_c = gbuf[pl.ds(base, chunk), :]
        outs = []
        for h in range(GLA_HEADS):
            oh = o[:, h * GLA_DV:(h + 1) * GLA_DV]
            ms = jnp.mean(oh * oh, axis=-1, keepdims=True)
            outs.append(oh * lax.rsqrt(ms + LN_EPS))
        on = jnp.concatenate(outs, axis=1) * gng_ref[...]
        on = on * (g_c * _sigmoid(g_c))
        mixbuf[pl.ds(base, chunk), CONV_CH:MIX_W] = on.astype(BF16)
        return carry

    def trip(stream, i, carry, extra_jobs=()):
        for j in range(max(per_trip, loop_blocks, len(extra_jobs))):
            if j < len(extra_jobs):
                extra_jobs[j]((i + 1) * trip_rows, trip_rows)
            if j < loop_blocks:
                r0 = (pre_blocks + i * loop_blocks + j) * cb
                conv_block(stream, r0 if isinstance(r0, int) else pl.multiple_of(r0, cb))
            if j < per_trip:
                carry = chunk_step(stream, i * per_trip + j, carry)
        return carry

    def merge_rows(r0, n):
        y = _dot(mixbuf[r0:r0 + n, :], w_o_ref[...])
        x1 = _layer_norm(DN_ALPHA * x[r0:r0 + n, :] + y, ln1g_ref[...], ln1b_ref[...])
        x1_ref[r0:r0 + n, :] = x1
        for j, seg in enumerate(_pack_segments(x1)):
            x1p_ref[j, r0:r0 + n, :] = seg
        x_hi, x_lo = _split_bf16(x1)
        lg = _dot(x_hi, wr_ref[...])
        logit_ref[r0:r0 + n, :] = (lg[:, 0:128] + lg[:, 128:256] + _dot(x_lo, wr_ref[:, 0:128])
                                   + br_ref[...])

    if static_trips:
        for i in range(n_trips):
            trip(0, i, 0, tuple(proj_jobs) if i + 1 < n_trips else ())
            merge_rows(i * trip_rows, trip_rows)
    else:
        for s in range(sb):
            if n_trips == 1:
                trip(s, 0, 0)
            else:
                lax.fori_loop(0, n_trips, functools.partial(trip, s), 0)
        merge_rows(0, n_rows)

    tails = []
    for s in range(sb):
        tail = ubuf[s * us + tm:s * us + tm + HALO, :]
        ubuf[s * us:s * us + HALO, :] = tail
        tails.append(tail)

    @pl.when(t == nt - 1)
    def _state_out():
        for s in range(sb):
            conv_out_ref[s] = tails[s][HALO - (CONV_W - 1):, :]
            st = st_ref[s * GLA_DV:(s + 1) * GLA_DV, :]
            for h in range(GLA_HEADS):
                gla_out_ref[s, h] = st[:, h * GLA_DK:(h + 1) * GLA_DK].T


def _mixer(x, conv_state, gla_state, p, *, tm, n_tok_total, tok_offset, token_bufs=None, sb=1):
    bsz, seq, _ = x.shape
    chunk = min(CHUNK, seq)
    rows = sb * tm
    assert seq % tm == 0 and tm % chunk == 0 and chunk % SUB == 0 and tok_offset % rows == 0
    assert bsz % sb == 0 and (sb == 1 or seq == tm)
    nt = seq // tm
    blk0 = tok_offset // rows
    full = lambda shape: pl.BlockSpec(shape, lambda b, t: (0,) * len(shape))
    alias = () if token_bufs is None else tuple(token_bufs)
    kern = functools.partial(_mixer_kernel, tm=tm, chunk=chunk, n_alias=len(alias), sb=sb)
    n_in = 16
    return pl.pallas_call(
        kern,
        grid=(bsz // sb, nt),
        input_output_aliases={n_in + i: i for i in range(len(alias))},
        in_specs=[
            pl.BlockSpec((sb, tm, D_MODEL), lambda b, t: (b, t, 0)),
            pl.BlockSpec((sb, CONV_W - 1, CONV_CH), lambda b, t: (b, 0, 0)),
            pl.BlockSpec((sb, GLA_HEADS, GLA_DK, GLA_DV), lambda b, t: (b, 0, 0, 0)),
            full((D_MODEL, IN_COLS_PAD)),
            full((8 * CONV_W, CONV_CH)),
            full((1, CONV_CH)),
            full((1, CONV_CH)),
            full((1, CONV_CH)),
            full((GATE_PAD, GLA_K)),
            full((1, GLA_K)),
            full((1, GLA_V)),
            full((MIX_W, D_MODEL)),
            full((1, D_MODEL)),
            full((1, D_MODEL)),
            full((D_MODEL, 256)),
            full((1, 128)),
        ] + [pl.BlockSpec(memory_space=pl.ANY)] * len(alias),
        out_specs=[
            pl.BlockSpec((N_SEG, rows, SC_SEG), lambda b, t: (0, blk0 + b * nt + t, 0)),
            pl.BlockSpec((rows, D_MODEL), lambda b, t: (blk0 + b * nt + t, 0)),
            pl.BlockSpec((rows, 128), lambda b, t: (blk0 + b * nt + t, 0)),
            pl.BlockSpec((sb, CONV_W - 1, CONV_CH), lambda b, t: (b, 0, 0)),
            pl.BlockSpec((sb, GLA_HEADS, GLA_DK, GLA_DV), lambda b, t: (b, 0, 0, 0)),
        ],
        out_shape=[
            jax.ShapeDtypeStruct((N_SEG, n_tok_total, SC_SEG), U32),
            jax.ShapeDtypeStruct((n_tok_total, D_MODEL), F32),
            jax.ShapeDtypeStruct((n_tok_total, 128), F32),
            jax.ShapeDtypeStruct((bsz, CONV_W - 1, CONV_CH), F32),
            jax.ShapeDtypeStruct((bsz, GLA_HEADS, GLA_DK, GLA_DV), F32),
        ],
        scratch_shapes=[
            pltpu.VMEM((sb * (HALO + tm), CONV_CH), F32),
            pltpu.VMEM((rows, GLA_K), F32),
            pltpu.VMEM((rows, GLA_K), F32),
            pltpu.VMEM((rows, GLA_K), F32),
            pltpu.VMEM((rows, GLA_V), F32),
            pltpu.VMEM((rows, GLA_V), F32),
            pltpu.VMEM((rows, MIX_W), BF16),
            pltpu.VMEM((sb * GLA_DV, GLA_K), F32),
        ],
        compiler_params=pltpu.CompilerParams(
            dimension_semantics=("arbitrary", "arbitrary"),
            vmem_limit_bytes=VMEM_LIMIT),
        name="mixer",
    )(x, conv_state, gla_state, p["w_in"], p["w_dw"], p["b_dw"], p["ln_conv_g"], p["ln_conv_b"],
      p["w_gate"], p["b_gate"], p["gla_norm_g"], p["w_o"], p["ln1_g"], p["ln1_b"],
      p["w_router"], p["b_router"], *alias)


def _route_kernel(lg_ref, ri_ref, rg_ref, cnt_ref, carry_ref, *, tr):
    i = pl.program_id(0)

    @pl.when(i == 0)
    def _init():
        carry_ref[...] = jnp.zeros((N_EXPERTS, 128), F32)

    l = lg_ref[...].T[0:N_EXPERTS, :]
    row = lax.broadcasted_iota(jnp.int32, (N_EXPERTS, tr), 0)
    hots, vals, idxs = [], [], []
    for _ in range(TOP_K):
        m = jnp.max(l, axis=0, keepdims=True)
        ik = jnp.min(jnp.where(l == m, row, N_EXPERTS), axis=0, keepdims=True)
        hot = row == ik
        hots.append(hot)
        vals.append(m)
        idxs.append(ik)
        l = jnp.where(hot, -jnp.inf, l)
    es = [jnp.exp(v - vals[0]) for v in vals]
    denom = es[0] + es[1] + es[2] + es[3]
    member = jnp.zeros((N_EXPERTS, tr), F32)
    for hot in hots:
        member = member + hot.astype(F32)
    before = (lax.broadcasted_iota(jnp.int32, (tr, tr), 0)
              < lax.broadcasted_iota(jnp.int32, (tr, tr), 1)).astype(BF16)
    cum = _dot(member.astype(BF16), before) + carry_ref[:, 0:1]
    ranks = [jnp.sum(jnp.where(hot, cum, 0.0), axis=0, keepdims=True).astype(jnp.int32)
             for hot in hots]
    ri_ref[...] = jnp.concatenate(idxs + ranks, axis=0)
    gates = jnp.concatenate([e / denom for e in es] + [jnp.zeros((128 - TOP_K, tr), F32)], axis=0)
    rg_ref[...] = gates.T
    carry_ref[...] = carry_ref[...] + jnp.sum(member, axis=1, keepdims=True)

    @pl.when(i == pl.num_programs(0) - 1)
    def _fin():
        cnt_ref[...] = carry_ref[...]


def _route(logits, *, tr):
    n_tok = logits.shape[0]
    assert n_tok % tr == 0
    return pl.pallas_call(
        functools.partial(_route_kernel, tr=tr),
        grid=(n_tok // tr,),
        in_specs=[pl.BlockSpec((tr, 128), lambda i: (i, 0))],
        out_specs=[pl.BlockSpec((2 * TOP_K, tr), lambda i: (0, i)),
                   pl.BlockSpec((tr, 128), lambda i: (i, 0)),
                   pl.BlockSpec((N_EXPERTS, 128), lambda i: (0, 0))],
        out_shape=[jax.ShapeDtypeStruct((2 * TOP_K, n_tok), jnp.int32),
                   jax.ShapeDtypeStruct((n_tok, 128), F32),
                   jax.ShapeDtypeStruct((N_EXPERTS, 128), F32)],
        scratch_shapes=[pltpu.VMEM((N_EXPERTS, 128), F32)],
        compiler_params=pltpu.CompilerParams(dimension_semantics=("arbitrary",)),
        name="route",
    )(logits)


def _ffn_kernel(gs_ref, nu_ref, xs_hbm, wgu_ref, bgu_ref, wdn_ref, bdn_ref, yb_hbm,
                wgu_bf, wdn_bf, xin, yout, in_sem, out_sem):
    e = pl.program_id(0)
    n_used = nu_ref[0]
    g_first = gs_ref[e]
    g_end = gs_ref[e + 1]

    def in_copy(g, slot):
        rows = pl.ds(pl.multiple_of(g * EXPERT_BLOCK, EXPERT_BLOCK), EXPERT_BLOCK)
        return pltpu.make_async_copy(xs_hbm.at[:, rows, :], xin.at[slot], in_sem.at[slot])

    def out_copy(g, slot):
        rows = pl.ds(pl.multiple_of(g * EXPERT_BLOCK, EXPERT_BLOCK), EXPERT_BLOCK)
        return pltpu.make_async_copy(yout.at[slot], yb_hbm.at[:, rows, :], out_sem.at[slot])

    @pl.when(e == 0)
    def _prime():
        for g in range(IN_AHEAD):
            @pl.when(g < n_used)
            def _():
                in_copy(g, g).start()

    @pl.when(g_end > g_first)
    def _cast():
        wgu_bf[...] = wgu_ref[0].astype(BF16)
        wdn_bf[...] = wdn_ref[0].astype(BF16)

    def blocks(g0, n):
        for j in range(n):
            g = g0 + j
            in_copy(g, g % IN_RING).wait()
        for j in range(n):
            g = g0 + IN_AHEAD + j

            @pl.when(g < n_used)
            def _prefetch():
                in_copy(g, g % IN_RING).start()
        for j in range(n):
            g = g0 + j

            @pl.when(g >= OUT_RING)
            def _free_out():
                out_copy(g - OUT_RING, g % OUT_RING).wait()
        for j in range(n):
            g = g0 + j
            x_lo, x_hi = _unpack_segments([xin[g % IN_RING, s] for s in range(N_SEG)])
            hgu = (_dot(x_lo.astype(BF16), wgu_bf[0:HALF, :])
                   + _dot(x_hi.astype(BF16), wgu_bf[HALF:, :]) + bgu_ref[0])
            gate = jnp.minimum(hgu[:, :D_FF], SWIGLU_LIMIT)
            up = jnp.clip(hgu[:, D_FF:], -SWIGLU_LIMIT, SWIGLU_LIMIT)
            act = (up + 1.0) * gate * _sigmoid(SWIGLU_ALPHA * gate)
            yb = _dot(act.astype(BF16), wdn_bf[...]) + bdn_ref[0]
            for s, seg in enumerate(_pack_segments(yb)):
                yout[g % OUT_RING, s] = seg
        for j in range(n):
            g = g0 + j
            out_copy(g, g % OUT_RING).start()

    n_blocks = g_end - g_first
    n_full = n_blocks // TRIP_BLOCKS

    def full_trip(i, carry):
        blocks(g_first + TRIP_BLOCKS * i, TRIP_BLOCKS)
        return carry

    lax.fori_loop(0, n_full, full_trip, 0)
    done = n_full * TRIP_BLOCKS
    size = TRIP_BLOCKS // 2
    while size >= 1:
        @pl.when((n_blocks // size) % 2 == 1)
        def _rest(done=done, size=size):
            blocks(g_first + done, size)
        done = done + jnp.where((n_blocks // size) % 2 == 1, size, 0)
        size //= 2

    @pl.when(e == pl.num_programs(0) - 1)
    def _drain():
        for back in range(OUT_RING, 0, -1):
            @pl.when(n_used >= back)
            def _():
                out_copy(n_used - back, (n_used - back) % OUT_RING).wait()


def _ffn(block_start, n_used, xs, w_gu, b_gu, w_down, b_down):
    n_rows = xs.shape[1]
    grid_spec = pltpu.PrefetchScalarGridSpec(
        num_scalar_prefetch=2,
        grid=(N_EXPERTS,),
        in_specs=[
            pl.BlockSpec(memory_space=pl.ANY),
            pl.BlockSpec((1, D_MODEL, 2 * D_FF), lambda e, gs, nu: (e, 0, 0)),
            pl.BlockSpec((1, 1, 2 * D_FF), lambda e, gs, nu: (e, 0, 0)),
            pl.BlockSpec((1, D_FF, D_MODEL), lambda e, gs, nu: (e, 0, 0)),
            pl.BlockSpec((1, 1, D_MODEL), lambda e, gs, nu: (e, 0, 0)),
        ],
        out_specs=pl.BlockSpec(memory_space=pl.ANY),
        scratch_shapes=[pltpu.VMEM((D_MODEL, 2 * D_FF), BF16),
                        pltpu.VMEM((D_FF, D_MODEL), BF16),
                        pltpu.VMEM((IN_RING, N_SEG, EXPERT_BLOCK, SC_SEG), U32),
                        pltpu.VMEM((OUT_RING, N_SEG, EXPERT_BLOCK, SC_SEG), U32),
                        pltpu.SemaphoreType.DMA((IN_RING,)),
                        pltpu.SemaphoreType.DMA((OUT_RING,))],
    )
    return pl.pallas_call(
        _ffn_kernel,
        grid_spec=grid_spec,
        out_shape=jax.ShapeDtypeStruct((N_SEG, n_rows, SC_SEG), U32),
        compiler_params=pltpu.CompilerParams(
            dimension_semantics=("arbitrary",), vmem_limit_bytes=VMEM_LIMIT),
        name="expert_ffn",
    )(block_start, n_used, xs, w_gu, b_gu.reshape(N_EXPERTS, 1, 2 * D_FF), w_down,
      b_down.reshape(N_EXPERTS, 1, D_MODEL))


def _combine_kernel(x1_ref, g_ref, rg_ref, ln2g_ref, ln2b_ref, *rest, blk0, n_prompt_blocks,
                    n_alias, with_sample):
    yp_ref = rest[n_alias]
    i = blk0 + pl.program_id(0)
    m_lo = jnp.zeros((x1_ref.shape[0], HALF), F32)
    m_hi = jnp.zeros((x1_ref.shape[0], HALF), F32)
    for k in range(TOP_K):
        g_lo, g_hi = _unpack_segments([g_ref[k, j] for j in range(N_SEG)])
        gate = rg_ref[:, k:k + 1]
        m_lo = m_lo + gate * g_lo
        m_hi = m_hi + gate * g_hi
    z = DN_ALPHA * x1_ref[...] + jnp.concatenate([m_lo, m_hi], axis=1)
    y = _layer_norm(z, ln2g_ref[...], ln2b_ref[...])

    if with_sample:
        ys_ref = rest[n_alias + 1]

        @pl.when(i < n_prompt_blocks)
        def _p():
            yp_ref[...] = y

        @pl.when(i >= n_prompt_blocks)
        def _s():
            ys_ref[...] = y
    else:
        yp_ref[...] = y


def _combine(x1, g, rg, ln2_g, ln2_b, *, tok0, n_prompt, tf, yp_buf=None):
    n_tok = x1.shape[0]
    piece = g.shape[2]
    n_sample = n_tok - n_prompt
    assert n_prompt % tf == 0 and tok0 % tf == 0 and piece % tf == 0
    npb = n_prompt // tf
    blk0 = tok0 // tf
    with_sample = tok0 + piece > n_prompt
    assert not with_sample or (tok0 + piece == n_tok and n_sample % tf == 0)
    alias = () if yp_buf is None else (yp_buf,)
    out_specs = [pl.BlockSpec((tf, D_MODEL), lambda i: (jnp.minimum(blk0 + i, npb - 1), 0))]
    out_shape = [jax.ShapeDtypeStruct((n_prompt, D_MODEL), F32)]
    if with_sample:
        out_specs.append(pl.BlockSpec((tf, D_MODEL), lambda i: (jnp.maximum(blk0 + i - npb, 0), 0)))
        out_shape.append(jax.ShapeDtypeStruct((n_sample, D_MODEL), F32))
    n_in = 5
    return pl.pallas_call(
        functools.partial(_combine_kernel, blk0=blk0, n_prompt_blocks=npb, n_alias=len(alias),
                          with_sample=with_sample),
        grid=(piece // tf,),
        input_output_aliases={n_in + i: i for i in range(len(alias))},
        in_specs=[pl.BlockSpec((tf, D_MODEL), lambda i: (blk0 + i, 0)),
                  pl.BlockSpec((TOP_K, N_SEG, tf, SC_SEG), lambda i: (0, 0, i, 0)),
                  pl.BlockSpec((tf, 128), lambda i: (blk0 + i, 0)),
                  pl.BlockSpec((1, D_MODEL), lambda i: (0, 0)),
                  pl.BlockSpec((1, D_MODEL), lambda i: (0, 0))]
        + [pl.BlockSpec(memory_space=pl.ANY)] * len(alias),
        out_specs=out_specs,
        out_shape=out_shape,
        compiler_params=pltpu.CompilerParams(
            dimension_semantics=("arbitrary",), vmem_limit_bytes=VMEM_LIMIT),
        name="combine",
    )(x1, g, rg, ln2_g, ln2_b, *alias)


def _sc_mesh():
    return plsc.VectorSubcoreMesh(core_axis_name="c", subcore_axis_name="s")


def _sc_scatter_rows(src, dest, n_out):
    n_src, width = src.shape
    n_k = dest.shape[0]
    assert width == SC_SEG and n_src % SC_WINDOW == 0 and dest.shape[1] == n_src

    @pl.kernel(out_type=jax.ShapeDtypeStruct((n_out, width), src.dtype), mesh=_sc_mesh(),
               scratch_types=[])
    def scatter_kernel(src_hbm, idx_hbm, out_hbm):
        def body(src_vmem, idx_vmem):
            for k in range(n_k):
                pltpu.sync_copy(src_vmem, out_hbm.at[idx_vmem.at[k]])

        pltpu.emit_pipeline(
            body, grid=(n_src // SC_WINDOW,),
            in_specs=[pl.BlockSpec((SC_WINDOW, width), lambda i: (i, 0)),
                      pl.BlockSpec((n_k, SC_WINDOW), lambda i: (0, i))],
            out_specs=[],
            core_axis_name=("c", "s"),
            dimension_semantics=(pltpu.PARALLEL,),
        )(src_hbm, idx_hbm)

    return scatter_kernel(src, dest)


def _sc_gather_rows(src, idx):
    n_out = idx.shape[1]
    width = src.shape[1]
    assert width == SC_SEG and n_out % SC_WINDOW == 0

    @pl.kernel(out_type=jax.ShapeDtypeStruct((n_out, width), src.dtype), mesh=_sc_mesh(),
               scratch_types=[])
    def gather_kernel(src_hbm, idx_hbm, out_hbm):
        def body(idx_vmem, out_vmem):
            pltpu.sync_copy(src_hbm.at[idx_vmem.at[0]], out_vmem)

        pltpu.emit_pipeline(
            body, grid=(n_out // SC_WINDOW,),
            in_specs=[pl.BlockSpec((1, SC_WINDOW), lambda i: (0, i))],
            out_specs=[pl.BlockSpec((SC_WINDOW, width), lambda i: (i, 0))],
            core_axis_name=("c", "s"),
            dimension_semantics=(pltpu.PARALLEL,),
        )(idx_hbm, out_hbm)

    return gather_kernel(src, idx)


def _prep_params(w_in, w_dw, b_dw, ln_conv_g, ln_conv_b, w_gate_lr, b_gate, gla_norm_g, w_o,
                 ln1_g, ln1_b, w_router, b_router):
    row = lambda v: v.reshape(1, -1).astype(F32)
    w_in_p = jnp.pad(w_in, ((0, 0), (0, IN_COLS_PAD - w_in.shape[1]))).astype(BF16)
    w_gate = jnp.pad(w_gate_lr, ((0, GATE_PAD - GATE_RANK), (0, 0))).astype(BF16)
    wr_hi = w_router.astype(BF16)
    wr_lo = (w_router - wr_hi.astype(F32)).astype(BF16)
    padr = lambda m: jnp.pad(m, ((0, 0), (0, 128 - N_EXPERTS)))
    return {
        "w_in": w_in_p, "w_dw": jnp.repeat(w_dw, 8, axis=0), "b_dw": row(b_dw), "ln_conv_g": row(ln_conv_g),
        "ln_conv_b": row(ln_conv_b), "w_gate": w_gate, "b_gate": row(b_gate),
        "gla_norm_g": row(gla_norm_g), "w_o": w_o.astype(BF16), "ln1_g": row(ln1_g),
        "ln1_b": row(ln1_b), "w_router": jnp.concatenate([padr(wr_hi), padr(wr_lo)], axis=1),
        "b_router": jnp.pad(row(b_router), ((0, 0), (0, 128 - N_EXPERTS))),
    }


def _dest_kernel(ps_ref, ri_ref, out_ref, *, n_slot):
    idx = ri_ref[0:TOP_K, :]
    dest = ri_ref[TOP_K:2 * TOP_K, :] + pl.program_id(0) * n_slot
    for e in range(N_EXPERTS):
        dest = dest + jnp.where(idx == e, ps_ref[e], 0)
    out_ref[...] = dest


def _moe_layout(ri, counts, *, tr):
    n_tok = ri.shape[1]
    cnt = counts[:, 0].astype(jnp.int32)
    padded = (cnt + EXPERT_BLOCK - 1) // EXPERT_BLOCK * EXPERT_BLOCK
    pad_end = jnp.cumsum(padded)
    pad_start = pad_end - padded
    nb = -(-(n_tok * TOP_K) // EXPERT_BLOCK) + N_EXPERTS
    n_t = n_tok // tr
    dest_seg = pl.pallas_call(
        functools.partial(_dest_kernel, n_slot=nb * EXPERT_BLOCK),
        grid_spec=pltpu.PrefetchScalarGridSpec(
            num_scalar_prefetch=1,
            grid=(N_SEG, n_t),
            in_specs=[pl.BlockSpec((2 * TOP_K, tr), lambda j, i, ps: (0, i))],
            out_specs=pl.BlockSpec((TOP_K, tr), lambda j, i, ps: (0, j * n_t + i)),
        ),
        out_shape=jax.ShapeDtypeStruct((TOP_K, N_SEG * n_tok), jnp.int32),
        name="slot_index",
    )(pad_start.astype(jnp.int32), ri)
    block_start = (jnp.concatenate([pad_start, pad_end[-1:]]) // EXPERT_BLOCK).astype(jnp.int32)
    n_used = block_start[-1:]
    return dest_seg, block_start, n_used, nb


def kernel(x_prompt, x_sample, state_conv, state_gla, w_in, w_dw, b_dw, ln_conv_g, ln_conv_b,
           w_gate_lr, b_gate, gla_norm_g, w_o, ln1_g, ln1_b, w_router, b_router, w_gu, b_gu,
           w_down, b_down, ln2_g, ln2_b):
    assert w_in.shape[0] == DEPTH
    l = 0
    p = _prep_params(w_in[l], w_dw[l], b_dw[l], ln_conv_g[l], ln_conv_b[l], w_gate_lr[l],
                     b_gate[l], gla_norm_g[l], w_o[l], ln1_g[l], ln1_b[l], w_router[l],
                     b_router[l])
    bp, tp, _ = x_prompt.shape
    bs, ts, _ = x_sample.shape
    zc = jnp.zeros((bp, CONV_W - 1, CONV_CH), F32)
    zs = jnp.zeros((bp, GLA_HEADS, GLA_DK, GLA_DV), F32)
    n_prompt = bp * tp
    n_tok = n_prompt + bs * ts
    x1p, x1, lg, conv_p, gla_p = _mixer(x_prompt, zc, zs, p, tm=PROMPT_TILE, n_tok_total=n_tok,
                                        tok_offset=0)
    x1p, x1, lg, conv_s, gla_s = _mixer(x_sample, state_conv[l], state_gla[l], p, tm=ts,
                                        n_tok_total=n_tok, tok_offset=n_prompt,
                                        token_bufs=(x1p, x1, lg),
                                        sb=SAMPLE_STREAMS if bs % SAMPLE_STREAMS == 0 else 1)

    tr = 768 if n_tok % 768 == 0 else 256
    ri, rg, counts = _route(lg, tr=tr)
    dest_seg, block_start, n_used, nb = _moe_layout(ri, counts, tr=n_tok)
    n_slot = nb * EXPERT_BLOCK
    xs = _sc_scatter_rows(x1p.reshape(N_SEG * n_tok, SC_SEG), dest_seg, N_SEG * n_slot)
    yb = _ffn(block_start, n_used, xs.reshape(N_SEG, n_slot, SC_SEG), w_gu[l], b_gu[l], w_down[l],
              b_down[l])
    tf_last = bs * ts
    n_blk = n_prompt // tf_last
    last_blk = min(LAST_PIECE_BLOCKS, n_blk // 2)
    body_tok = (n_blk - last_blk) * tf_last
    tf_body = 2 * tf_last if body_tok % (2 * tf_last) == 0 else tf_last
    body_blk = body_tok // tf_body
    n_body = min(COMBINE_PIECES - 1, body_blk)
    bounds = [(p * body_blk // n_body) * tf_body for p in range(n_body)] + [body_tok, n_tok]
    dest_kjt = dest_seg.reshape(TOP_K, N_SEG, n_tok)
    yp = None
    for t0, t1 in zip(bounds[:-1], bounds[1:]):
        g = _sc_gather_rows(yb.reshape(N_SEG * n_slot, SC_SEG),
                            dest_kjt[:, :, t0:t1].reshape(1, -1))
        outs = _combine(x1, g.reshape(TOP_K, N_SEG, t1 - t0, SC_SEG), rg,
                        ln2_g[l].reshape(1, -1), ln2_b[l].reshape(1, -1), tok0=t0,
                        n_prompt=n_prompt, tf=tf_last if t1 == n_tok else tf_body, yp_buf=yp)
        yp = outs[0]
    ys = outs[1]
    return (yp.reshape(bp, tp, D_MODEL), ys.reshape(bs, ts, D_MODEL),
            conv_p[None], gla_p[None], conv_s[None], gla_s[None])
```

```python
import functools

import jax
import jax.numpy as jnp
from jax import lax
from jax.experimental import pallas as pl
from jax.experimental.pallas import tpu as pltpu
from jax.experimental.pallas import tpu_sc as plsc

F32 = jnp.float32
BF16 = jnp.bfloat16

LANES = 128
D_MODEL = 1024
CHUNK = 64
SUB = 16
DIAG = 8
CONV_CH = 512
CONV_W = 31
HALO = 32
PROMPT_TILE = 1024
MAX_STATIC_TRIPS = 4
GLU_ROWS = 128
CONV_BLOCK = 64
ROUTE_TILE = 768
SAMPLE_STREAMS = 8
LAST_PIECE_BLOCKS = 4
COMBINE_PIECES = 8
TRIP_BLOCKS = 4
IN_AHEAD = TRIP_BLOCKS
IN_RING = 2 * TRIP_BLOCKS
OUT_RING = 2 * TRIP_BLOCKS
GLA_HEADS = 4
GLA_DK = 64
GLA_DV = 128
GLA_K = GLA_HEADS * GLA_DK
GLA_V = GLA_HEADS * GLA_DV
GATE_RANK = 16
GATE_PAD = 128
GATE_NORM = 16.0
MIX_W = CONV_CH + GLA_V
OFF_UV = 0
OFF_UG = OFF_UV + CONV_CH
OFF_Q = OFF_UG + CONV_CH
OFF_K = OFF_Q + GLA_K
OFF_V = OFF_K + GLA_K
OFF_G = OFF_V + GLA_V
OFF_A = OFF_G + GLA_V
IN_COLS_PAD = OFF_A + GATE_PAD
N_EXPERTS = 32
TOP_K = 4
D_FF = 1024
SWIGLU_LIMIT = 7.0
SWIGLU_ALPHA = 1.702
EXPERT_BLOCK = 256
LN_EPS = 1e-5
DEPTH = 1
DN_ALPHA = (2 * DEPTH) ** 0.25
VMEM_LIMIT = 56 * 1024 * 1024
HALF = D_MODEL // 2
SC_SEG = 256
SC_WINDOW = 128
N_SEG = HALF // SC_SEG
U32 = jnp.uint32


def _dot(a, b):
    return jnp.dot(a, b, preferred_element_type=F32)


def _dot_nt(a, b):
    return lax.dot_general(a, b, (((1,), (1,)), ((), ())), preferred_element_type=F32)


def _dot_tn(a, b):
    return lax.dot_general(a, b, (((0,), (0,)), ((), ())), preferred_element_type=F32)


def _layer_norm(x, g, b):
    mu = jnp.mean(x, axis=-1, keepdims=True)
    xc = x - mu
    var = jnp.mean(xc * xc, axis=-1, keepdims=True)
    return xc * lax.rsqrt(var + LN_EPS) * g + b


def _sigmoid(x):
    return 1.0 / (1.0 + jnp.exp(-x))


def _split_bf16(x):
    hi = x.astype(BF16)
    lo = (x - hi.astype(F32)).astype(BF16)
    return hi, lo


def _pack_segments(x):
    bits = pltpu.bitcast(x.astype(BF16).astype(F32), U32)
    word = (bits[:, :HALF] >> 16) | bits[:, HALF:]
    return [word[:, j * SC_SEG:(j + 1) * SC_SEG] for j in range(N_SEG)]


def _unpack_segments(segs):
    word = jnp.concatenate(segs, axis=1)
    lo = pltpu.bitcast(word << 16, F32)
    hi = pltpu.bitcast(word & jnp.uint32(0xFFFF0000), F32)
    return lo, hi


def _mixer_kernel(x_ref, cs_ref, gs_ref, w_in_ref, w_dw_ref, b_dw_ref, lncg_ref, lncb_ref,
                  wg_ref, bg_ref, gng_ref, w_o_ref, ln1g_ref, ln1b_ref, wr_ref, br_ref, *rest,
                  tm, chunk, n_alias, sb):
    (x1p_ref, x1_ref, logit_ref, conv_out_ref, gla_out_ref,
     ubuf, qbuf, kbuf, labuf, vbuf, gbuf, mixbuf, st_ref) = rest[n_alias:]
    t = pl.program_id(1)
    nt = pl.num_programs(1)
    n_chunks = tm // chunk
    n_sub = chunk // SUB
    n_rows = sb * tm
    us = HALO + tm

    @pl.when(t == 0)
    def _init():
        for s in range(sb):
            ubuf[s * us:s * us + HALO - (CONV_W - 1), :] = jnp.zeros(
                (HALO - (CONV_W - 1), CONV_CH), F32)
            ubuf[s * us + HALO - (CONV_W - 1):s * us + HALO, :] = cs_ref[s]
            for h in range(GLA_HEADS):
                st_ref[s * GLA_DV:(s + 1) * GLA_DV, h * GLA_DK:(h + 1) * GLA_DK] = gs_ref[s, h].T

    x = x_ref[...].reshape(n_rows, D_MODEL)
    xb = x.astype(BF16)

    gp = min(GLU_ROWS, tm) if sb == 1 else n_rows
    for r0 in range(0, n_rows, gp):
        hv = _dot(xb[r0:r0 + gp], w_in_ref[:, OFF_UV:OFF_UG])
        hg = _dot(xb[r0:r0 + gp], w_in_ref[:, OFF_UG:OFF_Q])
        u = hv * _sigmoid(hg)
        for s in range(sb):
            lo = max(r0, s * tm)
            hi = min(r0 + gp, (s + 1) * tm)
            if lo < hi:
                ubuf[s * us + HALO + lo - s * tm:s * us + HALO + hi - s * tm, :] = u[lo - r0:hi - r0]
    lead = HALO - (CONV_W - 1)

    def proj_q(r0, n):
        qbuf[r0:r0 + n, :] = _dot(xb[r0:r0 + n], w_in_ref[:, OFF_Q:OFF_K]) * (GLA_DK ** -0.5)

    def proj_k(r0, n):
        kbuf[r0:r0 + n, :] = _dot(xb[r0:r0 + n], w_in_ref[:, OFF_K:OFF_V])

    def proj_v(half, r0, n):
        lo = half * (GLA_V // 2)
        vbuf[r0:r0 + n, lo:lo + GLA_V // 2] = _dot(
            xb[r0:r0 + n], w_in_ref[:, OFF_V + lo:OFF_V + lo + GLA_V // 2])

    def proj_g(half, r0, n):
        lo = half * (GLA_V // 2)
        gbuf[r0:r0 + n, lo:lo + GLA_V // 2] = _dot(
            xb[r0:r0 + n], w_in_ref[:, OFF_G + lo:OFF_G + lo + GLA_V // 2])

    def proj_gate(r0, n):
        ha = _dot(xb[r0:r0 + n], w_in_ref[:, OFF_A:IN_COLS_PAD])
        a = _dot(ha.astype(BF16), wg_ref[...]) + bg_ref[...]
        labuf[r0:r0 + n, :] = ((jnp.minimum(a, 0.0) - jnp.log(1.0 + jnp.exp(-jnp.abs(a))))
                               * (1.0 / GATE_NORM))

    proj_jobs = [proj_q, proj_k, functools.partial(proj_v, 0), functools.partial(proj_v, 1),
                 functools.partial(proj_g, 0), functools.partial(proj_g, 1), proj_gate]

    cb = min(CONV_BLOCK, tm)

    def conv_block(stream, r0):
        acc = jnp.zeros((cb, CONV_CH), F32) + b_dw_ref[...]
        for r in range(8):
            rows = cb if r == 0 else cb + 8
            part = None
            for s in range(r, lead + CONV_W, 8):
                if s < lead:
                    continue
                w8 = w_dw_ref[8 * (s - lead):8 * (s - lead) + 8, :]
                u3 = ubuf[pl.ds(stream * us + r0 + (s - r), rows), :].reshape(
                    rows // 8, 8, CONV_CH)
                term = u3 * w8
                part = term if part is None else part + term
            acc = acc + part.reshape(rows, CONV_CH)[r:r + cb, :]
        cact = _layer_norm(acc, lncg_ref[...], lncb_ref[...])
        cact = cact * _sigmoid(cact)
        mixbuf[pl.ds(stream * tm + r0, cb), 0:CONV_CH] = cact.astype(BF16)

    per_trip = 4 if n_chunks % 4 == 0 else 1
    n_trips = n_chunks // per_trip
    static_trips = 2 <= n_trips <= MAX_STATIC_TRIPS and sb == 1
    trip_rows = per_trip * chunk
    loop_blocks = (tm // 2) // (cb * n_trips) if n_chunks >= 4 else 0
    pre_blocks = tm // cb - loop_blocks * n_trips
    rows_first = trip_rows if static_trips else n_rows
    pre = [(s, blk * cb) for s in range(sb) for blk in range(pre_blocks)]
    for i in range(max(len(pre), len(proj_jobs))):
        if i < len(proj_jobs):
            proj_jobs[i](0, rows_first)
        if i < len(pre):
            conv_block(*pre[i])

    lane_k = lax.broadcasted_iota(jnp.int32, (1, GLA_K), 1) // GLA_DK
    row_c = lax.broadcasted_iota(jnp.int32, (chunk, 1), 0)
    tri = (lax.broadcasted_iota(jnp.int32, (chunk, chunk), 0)
           >= lax.broadcasted_iota(jnp.int32, (chunk, chunk), 1)).astype(BF16)
    e2 = (lax.broadcasted_iota(jnp.int32, (GLA_K, GLA_V), 0) // GLA_DK
          == lax.broadcasted_iota(jnp.int32, (GLA_K, GLA_V), 1) // GLA_DV).astype(BF16)
    row_s = lax.broadcasted_iota(jnp.int32, (GLA_HEADS * chunk, 1), 0)
    blk_s = (row_s % chunk) // SUB
    same_blk = blk_s == lax.broadcasted_iota(jnp.int32, (1, LANES), 1) // SUB
    second_half = row_c % SUB >= DIAG
    row_g = lax.broadcasted_iota(jnp.int32, (1, DIAG, 1), 1)
    lane_v = lax.broadcasted_iota(jnp.int32, (1, GLA_V), 1) % GLA_DV

    def head_stack(m):
        return jnp.concatenate(
            [jnp.where(lane_k == h, m, 0.0) for h in range(GLA_HEADS)], axis=0).astype(BF16)

    def unstack(r, width):
        return jnp.concatenate(
            [r[h * chunk:(h + 1) * chunk, h * width:(h + 1) * width] for h in range(GLA_HEADS)],
            axis=1)

    def chunk_step(stream, c, carry):
        base = stream * tm + (c * chunk if isinstance(c, int) else pl.multiple_of(c * chunk, chunk))
        st_rows = slice(stream * GLA_DV, (stream + 1) * GLA_DV)
        q_c = qbuf[pl.ds(base, chunk), :]
        k_c = kbuf[pl.ds(base, chunk), :]
        v_c = vbuf[pl.ds(base, chunk), :]
        v_b = v_c.astype(BF16)
        la_c = labuf[pl.ds(base, chunk), :]
        la_hi, la_lo = _split_bf16(la_c)
        b = _dot(tri, la_hi) + _dot(tri, la_lo)
        b_last = b[chunk - 1:chunk, :]
        st = st_ref[st_rows, :]

        qe = q_c * jnp.exp(b)
        o2 = _dot_nt(head_stack(qe), st.astype(BF16))
        o = jnp.concatenate([o2[h * chunk:(h + 1) * chunk, :] for h in range(GLA_HEADS)], axis=1)

        a_off = None
        if n_sub > 1:
            r_rows = [b[0:SUB, :]] + [jnp.broadcast_to(b[SUB * i - 1:SUB * i, :], (SUB, GLA_K))
                                      for i in range(1, n_sub)]
            r_q = jnp.concatenate(r_rows, axis=0)
            q_t = jnp.where(row_c >= SUB, q_c * jnp.exp(jnp.minimum(b - r_q, 0.0)), 0.0)
            k_parts = []
            for i in range(1, n_sub):
                r_i = b[SUB * i - 1:SUB * i, :]
                k_t = jnp.where(row_c < SUB * i, k_c * jnp.exp(jnp.minimum(r_i - b, 0.0)), 0.0)
                k_parts.append(k_t)
                k_parts.append(jnp.zeros((LANES - chunk, GLA_K), F32))
            k_cat = jnp.concatenate(k_parts, axis=0).astype(BF16)
            r = _dot_nt(head_stack(q_t), k_cat)
            a_off = r[:, 0:LANES]
            for i in range(2, n_sub):
                a_off = jnp.where(blk_s == i, r[:, (i - 1) * LANES:i * LANES], a_off)

        r_b = jnp.concatenate(
            [jnp.broadcast_to(b[SUB * i + DIAG - 1:SUB * i + DIAG, :], (SUB, GLA_K))
             for i in range(n_sub)], axis=0)
        q_b = jnp.where(second_half, q_c * jnp.exp(jnp.minimum(b - r_b, 0.0)), 0.0)
        k_b = jnp.where(second_half, 0.0, k_c * jnp.exp(jnp.minimum(r_b - b, 0.0)))
        k_b = jnp.concatenate([k_b, jnp.zeros((LANES - chunk, GLA_K), F32)], axis=0).astype(BF16)
        a_half = jnp.where(same_blk, _dot_nt(head_stack(q_b), k_b), 0.0)
        a_off = a_half if a_off is None else a_off + a_half

        q3 = q_c.reshape(chunk // DIAG, DIAG, GLA_K)
        k3 = k_c.reshape(chunk // DIAG, DIAG, GLA_K)
        b3 = b.reshape(chunk // DIAG, DIAG, GLA_K)
        ps = []
        for d in range(DIAG):
            k_s = k3 if d == 0 else pltpu.roll(k3, d, axis=1)
            b_s = b3 if d == 0 else pltpu.roll(b3, d, axis=1)
            p = jnp.where(row_g >= d, q3 * k_s * jnp.exp(jnp.minimum(b3 - b_s, 0.0)), 0.0)
            ps.append(p.reshape(chunk, GLA_K).astype(BF16))
        w_all = _dot(jnp.concatenate(ps, axis=0), e2)
        a_d = jnp.zeros((chunk, GLA_V), F32)
        for d in range(DIAG):
            a_d = jnp.where(lane_v == row_c - d, w_all[d * chunk:(d + 1) * chunk, :], a_d)
        a_off = a_off + jnp.concatenate(
            [a_d[:, h * GLA_DV:(h + 1) * GLA_DV] for h in range(GLA_HEADS)], axis=0)
        o1 = _dot(a_off[:, 0:chunk].astype(BF16), v_b)
        o = o + unstack(o1, GLA_DV)

        ke = k_c * jnp.exp(b_last - b)
        r2 = _dot_tn(v_b, ke.astype(BF16))
        upd = jnp.zeros((GLA_DV, GLA_K), F32)
        for h in range(GLA_HEADS):
            upd = upd + jnp.where(lane_k == h, r2[h * GLA_DV:(h + 1) * GLA_DV, :], 0.0)
        st_ref[st_rows, :] = st * jnp.exp(b_last) + upd

        g_c = gbuf[pl.ds(base, chunk), :]
        outs = []
        for h in range(GLA_HEADS):
            oh = o[:, h * GLA_DV:(h + 1) * GLA_DV]
            ms = jnp.mean(oh * oh, axis=-1, keepdims=True)
            outs.append(oh * lax.rsqrt(ms + LN_EPS))
        on = jnp.concatenate(outs, axis=1) * gng_ref[...]
        on = on * (g_c * _sigmoid(g_c))
        mixbuf[pl.ds(base, chunk), CONV_CH:MIX_W] = on.astype(BF16)
        return carry

    def trip(stream, i, carry, extra_jobs=()):
        for j in range(max(per_trip, loop_blocks, len(extra_jobs))):
            if j < len(extra_jobs):
                extra_jobs[j]((i + 1) * trip_rows, trip_rows)
            if j < loop_blocks:
                r0 = (pre_blocks + i * loop_blocks + j) * cb
                conv_block(stream, r0 if isinstance(r0, int) else pl.multiple_of(r0, cb))
            if j < per_trip:
                carry = chunk_step(stream, i * per_trip + j, carry)
        return carry

    def merge_rows(r0, n):
        y = _dot(mixbuf[r0:r0 + n, :], w_o_ref[...])
        x1 = _layer_norm(DN_ALPHA * x[r0:r0 + n, :] + y, ln1g_ref[...], ln1b_ref[...])
        x1_ref[r0:r0 + n, :] = x1
        for j, seg in enumerate(_pack_segments(x1)):
            x1p_ref[j, r0:r0 + n, :] = seg
        x_hi, x_lo = _split_bf16(x1)
        lg = _dot(x_hi, wr_ref[...])
        logit_ref[r0:r0 + n, :] = (lg[:, 0:LANES] + lg[:, LANES:2 * LANES]
                                   + _dot(x_lo, wr_ref[:, 0:LANES]) + br_ref[...])

    if static_trips:
        for i in range(n_trips):
            trip(0, i, 0, tuple(proj_jobs) if i + 1 < n_trips else ())
            merge_rows(i * trip_rows, trip_rows)
    else:
        for s in range(sb):
            if n_trips == 1:
                trip(s, 0, 0)
            else:
                lax.fori_loop(0, n_trips, functools.partial(trip, s), 0)
        merge_rows(0, n_rows)

    tails = []
    for s in range(sb):
        tail = ubuf[s * us + tm:s * us + tm + HALO, :]
        ubuf[s * us:s * us + HALO, :] = tail
        tails.append(tail)

    @pl.when(t == nt - 1)
    def _state_out():
        for s in range(sb):
            conv_out_ref[s] = tails[s][HALO - (CONV_W - 1):, :]
            st = st_ref[s * GLA_DV:(s + 1) * GLA_DV, :]
            for h in range(GLA_HEADS):
                gla_out_ref[s, h] = st[:, h * GLA_DK:(h + 1) * GLA_DK].T


def _mixer(x, conv_state, gla_state, p, *, tm, n_tok_total, tok_offset, token_bufs=None, sb=1):
    bsz, seq, _ = x.shape
    chunk = min(CHUNK, seq)
    rows = sb * tm
    assert seq % tm == 0 and tm % chunk == 0 and chunk % SUB == 0 and tok_offset % rows == 0
    assert bsz % sb == 0 and (sb == 1 or seq == tm)
    nt = seq // tm
    blk0 = tok_offset // rows
    full = lambda shape: pl.BlockSpec(shape, lambda b, t: (0,) * len(shape))
    alias = () if token_bufs is None else tuple(token_bufs)
    kern = functools.partial(_mixer_kernel, tm=tm, chunk=chunk, n_alias=len(alias), sb=sb)
    n_in = 16
    return pl.pallas_call(
        kern,
        grid=(bsz // sb, nt),
        input_output_aliases={n_in + i: i for i in range(len(alias))},
        in_specs=[
            pl.BlockSpec((sb, tm, D_MODEL), lambda b, t: (b, t, 0)),
            pl.BlockSpec((sb, CONV_W - 1, CONV_CH), lambda b, t: (b, 0, 0)),
            pl.BlockSpec((sb, GLA_HEADS, GLA_DK, GLA_DV), lambda b, t: (b, 0, 0, 0)),
            full((D_MODEL, IN_COLS_PAD)),
            full((8 * CONV_W, CONV_CH)),
            full((1, CONV_CH)),
            full((1, CONV_CH)),
            full((1, CONV_CH)),
            full((GATE_PAD, GLA_K)),
            full((1, GLA_K)),
            full((1, GLA_V)),
            full((MIX_W, D_MODEL)),
            full((1, D_MODEL)),
            full((1, D_MODEL)),
            full((D_MODEL, 2 * LANES)),
            full((1, LANES)),
        ] + [pl.BlockSpec(memory_space=pl.ANY)] * len(alias),
        out_specs=[
            pl.BlockSpec((N_SEG, rows, SC_SEG), lambda b, t: (0, blk0 + b * nt + t, 0)),
            pl.BlockSpec((rows, D_MODEL), lambda b, t: (blk0 + b * nt + t, 0)),
            pl.BlockSpec((rows, LANES), lambda b, t: (blk0 + b * nt + t, 0)),
            pl.BlockSpec((sb, CONV_W - 1, CONV_CH), lambda b, t: (b, 0, 0)),
            pl.BlockSpec((sb, GLA_HEADS, GLA_DK, GLA_DV), lambda b, t: (b, 0, 0, 0)),
        ],
        out_shape=[
            jax.ShapeDtypeStruct((N_SEG, n_tok_total, SC_SEG), U32),
            jax.ShapeDtypeStruct((n_tok_total, D_MODEL), F32),
            jax.ShapeDtypeStruct((n_tok_total, LANES), F32),
            jax.ShapeDtypeStruct((bsz, CONV_W - 1, CONV_CH), F32),
            jax.ShapeDtypeStruct((bsz, GLA_HEADS, GLA_DK, GLA_DV), F32),
        ],
        scratch_shapes=[
            pltpu.VMEM((sb * (HALO + tm), CONV_CH), F32),
            pltpu.VMEM((rows, GLA_K), F32),
            pltpu.VMEM((rows, GLA_K), F32),
            pltpu.VMEM((rows, GLA_K), F32),
            pltpu.VMEM((rows, GLA_V), F32),
            pltpu.VMEM((rows, GLA_V), F32),
            pltpu.VMEM((rows, MIX_W), BF16),
            pltpu.VMEM((sb * GLA_DV, GLA_K), F32),
        ],
        compiler_params=pltpu.CompilerParams(
            dimension_semantics=("arbitrary", "arbitrary"),
            vmem_limit_bytes=VMEM_LIMIT),
        name="mixer",
    )(x, conv_state, gla_state, p["w_in"], p["w_dw"], p["b_dw"], p["ln_conv_g"], p["ln_conv_b"],
      p["w_gate"], p["b_gate"], p["gla_norm_g"], p["w_o"], p["ln1_g"], p["ln1_b"],
      p["w_router"], p["b_router"], *alias)


def _route_kernel(lg_ref, ri_ref, rg_ref, cnt_ref, carry_ref, *, tr):
    i = pl.program_id(0)

    @pl.when(i == 0)
    def _init():
        carry_ref[...] = jnp.zeros((N_EXPERTS, LANES), F32)

    l = lg_ref[...].T[0:N_EXPERTS, :]
    row = lax.broadcasted_iota(jnp.int32, (N_EXPERTS, tr), 0)
    hots, vals, idxs = [], [], []
    for _ in range(TOP_K):
        m = jnp.max(l, axis=0, keepdims=True)
        ik = jnp.min(jnp.where(l == m, row, N_EXPERTS), axis=0, keepdims=True)
        hot = row == ik
        hots.append(hot)
        vals.append(m)
        idxs.append(ik)
        l = jnp.where(hot, -jnp.inf, l)
    es = [jnp.exp(v - vals[0]) for v in vals]
    denom = es[0] + es[1] + es[2] + es[3]
    member = jnp.zeros((N_EXPERTS, tr), F32)
    for hot in hots:
        member = member + hot.astype(F32)
    before = (lax.broadcasted_iota(jnp.int32, (tr, tr), 0)
              < lax.broadcasted_iota(jnp.int32, (tr, tr), 1)).astype(BF16)
    cum = _dot(member.astype(BF16), before) + carry_ref[:, 0:1]
    ranks = [jnp.sum(jnp.where(hot, cum, 0.0), axis=0, keepdims=True).astype(jnp.int32)
             for hot in hots]
    ri_ref[...] = jnp.concatenate(idxs + ranks, axis=0)
    gates = jnp.concatenate([e / denom for e in es] + [jnp.zeros((LANES - TOP_K, tr), F32)], axis=0)
    rg_ref[...] = gates.T
    carry_ref[...] = carry_ref[...] + jnp.sum(member, axis=1, keepdims=True)

    @pl.when(i == pl.num_programs(0) - 1)
    def _fin():
        cnt_ref[...] = carry_ref[...]


def _route(logits, *, tr):
    n_tok = logits.shape[0]
    assert n_tok % tr == 0
    return pl.pallas_call(
        functools.partial(_route_kernel, tr=tr),
        grid=(n_tok // tr,),
        in_specs=[pl.BlockSpec((tr, LANES), lambda i: (i, 0))],
        out_specs=[pl.BlockSpec((2 * TOP_K, tr), lambda i: (0, i)),
                   pl.BlockSpec((tr, LANES), lambda i: (i, 0)),
                   pl.BlockSpec((N_EXPERTS, LANES), lambda i: (0, 0))],
        out_shape=[jax.ShapeDtypeStruct((2 * TOP_K, n_tok), jnp.int32),
                   jax.ShapeDtypeStruct((n_tok, LANES), F32),
                   jax.ShapeDtypeStruct((N_EXPERTS, LANES), F32)],
        scratch_shapes=[pltpu.VMEM((N_EXPERTS, LANES), F32)],
        compiler_params=pltpu.CompilerParams(dimension_semantics=("arbitrary",)),
        name="route",
    )(logits)


def _ffn_kernel(gs_ref, nu_ref, xs_hbm, wgu_ref, bgu_ref, wdn_ref, bdn_ref, yb_hbm,
                wgu_bf, wdn_bf, xin, yout, in_sem, out_sem):
    e = pl.program_id(0)
    n_used = nu_ref[0]
    g_first = gs_ref[e]
    g_end = gs_ref[e + 1]

    def in_copy(g, slot):
        rows = pl.ds(pl.multiple_of(g * EXPERT_BLOCK, EXPERT_BLOCK), EXPERT_BLOCK)
        return pltpu.make_async_copy(xs_hbm.at[:, rows, :], xin.at[slot], in_sem.at[slot])

    def out_copy(g, slot):
        rows = pl.ds(pl.multiple_of(g * EXPERT_BLOCK, EXPERT_BLOCK), EXPERT_BLOCK)
        return pltpu.make_async_copy(yout.at[slot], yb_hbm.at[:, rows, :], out_sem.at[slot])

    @pl.when(e == 0)
    def _prime():
        for g in range(IN_AHEAD):
            @pl.when(g < n_used)
            def _():
                in_copy(g, g).start()

    @pl.when(g_end > g_first)
    def _cast():
        wgu_bf[...] = wgu_ref[0].astype(BF16)
        wdn_bf[...] = wdn_ref[0].astype(BF16)

    def blocks(g0, n):
        for j in range(n):
            g = g0 + j
            in_copy(g, g % IN_RING).wait()
        for j in range(n):
            g = g0 + IN_AHEAD + j

            @pl.when(g < n_used)
            def _prefetch():
                in_copy(g, g % IN_RING).start()
        for j in range(n):
            g = g0 + j

            @pl.when(g >= OUT_RING)
            def _free_out():
                out_copy(g - OUT_RING, g % OUT_RING).wait()
        for j in range(n):
            g = g0 + j
            x_lo, x_hi = _unpack_segments([xin[g % IN_RING, s] for s in range(N_SEG)])
            hgu = (_dot(x_lo.astype(BF16), wgu_bf[0:HALF, :])
                   + _dot(x_hi.astype(BF16), wgu_bf[HALF:, :]) + bgu_ref[0])
            gate = jnp.minimum(hgu[:, :D_FF], SWIGLU_LIMIT)
            up = jnp.clip(hgu[:, D_FF:], -SWIGLU_LIMIT, SWIGLU_LIMIT)
            act = (up + 1.0) * gate * _sigmoid(SWIGLU_ALPHA * gate)
            yb = _dot(act.astype(BF16), wdn_bf[...]) + bdn_ref[0]
            for s, seg in enumerate(_pack_segments(yb)):
                yout[g % OUT_RING, s] = seg
        for j in range(n):
            g = g0 + j
            out_copy(g, g % OUT_RING).start()

    n_blocks = g_end - g_first
    n_full = n_blocks // TRIP_BLOCKS

    def full_trip(i, carry):
        blocks(g_first + TRIP_BLOCKS * i, TRIP_BLOCKS)
        return carry

    lax.fori_loop(0, n_full, full_trip, 0)
    done = n_full * TRIP_BLOCKS
    size = TRIP_BLOCKS // 2
    while size >= 1:
        @pl.when((n_blocks // size) % 2 == 1)
        def _rest(done=done, size=size):
            blocks(g_first + done, size)
        done = done + jnp.where((n_blocks // size) % 2 == 1, size, 0)
        size //= 2

    @pl.when(e == pl.num_programs(0) - 1)
    def _drain():
        for back in range(OUT_RING, 0, -1):
            @pl.when(n_used >= back)
            def _():
                out_copy(n_used - back, (n_used - back) % OUT_RING).wait()


def _ffn(block_start, n_used, xs, w_gu, b_gu, w_down, b_down):
    n_rows = xs.shape[1]
    grid_spec = pltpu.PrefetchScalarGridSpec(
        num_scalar_prefetch=2,
        grid=(N_EXPERTS,),
        in_specs=[
            pl.BlockSpec(memory_space=pl.ANY),
            pl.BlockSpec((1, D_MODEL, 2 * D_FF), lambda e, gs, nu: (e, 0, 0)),
            pl.BlockSpec((1, 1, 2 * D_FF), lambda e, gs, nu: (e, 0, 0)),
            pl.BlockSpec((1, D_FF, D_MODEL), lambda e, gs, nu: (e, 0, 0)),
            pl.BlockSpec((1, 1, D_MODEL), lambda e, gs, nu: (e, 0, 0)),
        ],
        out_specs=pl.BlockSpec(memory_space=pl.ANY),
        scratch_shapes=[pltpu.VMEM((D_MODEL, 2 * D_FF), BF16),
                        pltpu.VMEM((D_FF, D_MODEL), BF16),
                        pltpu.VMEM((IN_RING, N_SEG, EXPERT_BLOCK, SC_SEG), U32),
                        pltpu.VMEM((OUT_RING, N_SEG, EXPERT_BLOCK, SC_SEG), U32),
                        pltpu.SemaphoreType.DMA((IN_RING,)),
                        pltpu.SemaphoreType.DMA((OUT_RING,))],
    )
    return pl.pallas_call(
        _ffn_kernel,
        grid_spec=grid_spec,
        out_shape=jax.ShapeDtypeStruct((N_SEG, n_rows, SC_SEG), U32),
        compiler_params=pltpu.CompilerParams(
            dimension_semantics=("arbitrary",), vmem_limit_bytes=VMEM_LIMIT),
        name="expert_ffn",
    )(block_start, n_used, xs, w_gu, b_gu.reshape(N_EXPERTS, 1, 2 * D_FF), w_down,
      b_down.reshape(N_EXPERTS, 1, D_MODEL))


def _combine_kernel(x1_ref, g_ref, rg_ref, ln2g_ref, ln2b_ref, *rest, blk0, n_prompt_blocks,
                    n_alias, with_sample):
    yp_ref = rest[n_alias]
    i = blk0 + pl.program_id(0)
    m_lo = jnp.zeros((x1_ref.shape[0], HALF), F32)
    m_hi = jnp.zeros((x1_ref.shape[0], HALF), F32)
    for k in range(TOP_K):
        g_lo, g_hi = _unpack_segments([g_ref[k, j] for j in range(N_SEG)])
        gate = rg_ref[:, k:k + 1]
        m_lo = m_lo + gate * g_lo
        m_hi = m_hi + gate * g_hi
    z = DN_ALPHA * x1_ref[...] + jnp.concatenate([m_lo, m_hi], axis=1)
    y = _layer_norm(z, ln2g_ref[...], ln2b_ref[...])

    if with_sample:
        ys_ref = rest[n_alias + 1]

        @pl.when(i < n_prompt_blocks)
        def _p():
            yp_ref[...] = y

        @pl.when(i >= n_prompt_blocks)
        def _s():
            ys_ref[...] = y
    else:
        yp_ref[...] = y


def _combine(x1, g, rg, ln2_g, ln2_b, *, tok0, n_prompt, tf, yp_buf=None):
    n_tok = x1.shape[0]
    piece = g.shape[2]
    n_sample = n_tok - n_prompt
    assert n_prompt % tf == 0 and tok0 % tf == 0 and piece % tf == 0
    npb = n_prompt // tf
    blk0 = tok0 // tf
    with_sample = tok0 + piece > n_prompt
    assert not with_sample or (tok0 + piece == n_tok and n_sample % tf == 0)
    alias = () if yp_buf is None else (yp_buf,)
    out_specs = [pl.BlockSpec((tf, D_MODEL), lambda i: (jnp.minimum(blk0 + i, npb - 1), 0))]
    out_shape = [jax.ShapeDtypeStruct((n_prompt, D_MODEL), F32)]
    if with_sample:
        out_specs.append(pl.BlockSpec((tf, D_MODEL), lambda i: (jnp.maximum(blk0 + i - npb, 0), 0)))
        out_shape.append(jax.ShapeDtypeStruct((n_sample, D_MODEL), F32))
    n_in = 5
    return pl.pallas_call(
        functools.partial(_combine_kernel, blk0=blk0, n_prompt_blocks=npb, n_alias=len(alias),
                          with_sample=with_sample),
        grid=(piece // tf,),
        input_output_aliases={n_in + i: i for i in range(len(alias))},
        in_specs=[pl.BlockSpec((tf, D_MODEL), lambda i: (blk0 + i, 0)),
                  pl.BlockSpec((TOP_K, N_SEG, tf, SC_SEG), lambda i: (0, 0, i, 0)),
                  pl.BlockSpec((tf, LANES), lambda i: (blk0 + i, 0)),
                  pl.BlockSpec((1, D_MODEL), lambda i: (0, 0)),
                  pl.BlockSpec((1, D_MODEL), lambda i: (0, 0))]
        + [pl.BlockSpec(memory_space=pl.ANY)] * len(alias),
        out_specs=out_specs,
        out_shape=out_shape,
        compiler_params=pltpu.CompilerParams(
            dimension_semantics=("arbitrary",), vmem_limit_bytes=VMEM_LIMIT),
        name="combine",
    )(x1, g, rg, ln2_g, ln2_b, *alias)


def _sc_mesh():
    return plsc.VectorSubcoreMesh(core_axis_name="c", subcore_axis_name="s")


def _sc_scatter_rows(src, dest, n_out):
    n_src, width = src.shape
    n_k = dest.shape[0]
    assert width == SC_SEG and n_src % SC_WINDOW == 0 and dest.shape[1] == n_src

    @pl.kernel(out_type=jax.ShapeDtypeStruct((n_out, width), src.dtype), mesh=_sc_mesh(),
               scratch_types=[])
    def scatter_kernel(src_hbm, idx_hbm, out_hbm):
        def body(src_vmem, idx_vmem):
            for k in range(n_k):
                pltpu.sync_copy(src_vmem, out_hbm.at[idx_vmem.at[k]])

        pltpu.emit_pipeline(
            body, grid=(n_src // SC_WINDOW,),
            in_specs=[pl.BlockSpec((SC_WINDOW, width), lambda i: (i, 0)),
                      pl.BlockSpec((n_k, SC_WINDOW), lambda i: (0, i))],
            out_specs=[],
            core_axis_name=("c", "s"),
            dimension_semantics=(pltpu.PARALLEL,),
        )(src_hbm, idx_hbm)

    return scatter_kernel(src, dest)


def _sc_gather_rows(src, idx):
    n_out = idx.shape[1]
    width = src.shape[1]
    assert width == SC_SEG and n_out % SC_WINDOW == 0

    @pl.kernel(out_type=jax.ShapeDtypeStruct((n_out, width), src.dtype), mesh=_sc_mesh(),
               scratch_types=[])
    def gather_kernel(src_hbm, idx_hbm, out_hbm):
        def body(idx_vmem, out_vmem):
            pltpu.sync_copy(src_hbm.at[idx_vmem.at[0]], out_vmem)

        pltpu.emit_pipeline(
            body, grid=(n_out // SC_WINDOW,),
            in_specs=[pl.BlockSpec((1, SC_WINDOW), lambda i: (0, i))],
            out_specs=[pl.BlockSpec((SC_WINDOW, width), lambda i: (i, 0))],
            core_axis_name=("c", "s"),
            dimension_semantics=(pltpu.PARALLEL,),
        )(idx_hbm, out_hbm)

    return gather_kernel(src, idx)


def _prep_params(w_in, w_dw, b_dw, ln_conv_g, ln_conv_b, w_gate_lr, b_gate, gla_norm_g, w_o,
                 ln1_g, ln1_b, w_router, b_router):
    row = lambda v: v.reshape(1, -1).astype(F32)
    w_in_p = jnp.pad(w_in, ((0, 0), (0, IN_COLS_PAD - w_in.shape[1]))).astype(BF16)
    w_gate = jnp.pad(w_gate_lr, ((0, GATE_PAD - GATE_RANK), (0, 0))).astype(BF16)
    wr_hi = w_router.astype(BF16)
    wr_lo = (w_router - wr_hi.astype(F32)).astype(BF16)
    padr = lambda m: jnp.pad(m, ((0, 0), (0, LANES - N_EXPERTS)))
    return {
        "w_in": w_in_p, "w_dw": jnp.repeat(w_dw, 8, axis=0), "b_dw": row(b_dw), "ln_conv_g": row(ln_conv_g),
        "ln_conv_b": row(ln_conv_b), "w_gate": w_gate, "b_gate": row(b_gate),
        "gla_norm_g": row(gla_norm_g), "w_o": w_o.astype(BF16), "ln1_g": row(ln1_g),
        "ln1_b": row(ln1_b), "w_router": jnp.concatenate([padr(wr_hi), padr(wr_lo)], axis=1),
        "b_router": jnp.pad(row(b_router), ((0, 0), (0, LANES - N_EXPERTS))),
    }


def _dest_kernel(ps_ref, ri_ref, out_ref, *, n_slot):
    idx = ri_ref[0:TOP_K, :]
    dest = ri_ref[TOP_K:2 * TOP_K, :] + pl.program_id(0) * n_slot
    for e in range(N_EXPERTS):
        dest = dest + jnp.where(idx == e, ps_ref[e], 0)
    out_ref[...] = dest


def _moe_layout(ri, counts, *, tr):
    n_tok = ri.shape[1]
    cnt = counts[:, 0].astype(jnp.int32)
    padded = (cnt + EXPERT_BLOCK - 1) // EXPERT_BLOCK * EXPERT_BLOCK
    pad_end = jnp.cumsum(padded)
    pad_start = pad_end - padded
    nb = -(-(n_tok * TOP_K) // EXPERT_BLOCK) + N_EXPERTS
    n_t = n_tok // tr
    dest_seg = pl.pallas_call(
        functools.partial(_dest_kernel, n_slot=nb * EXPERT_BLOCK),
        grid_spec=pltpu.PrefetchScalarGridSpec(
            num_scalar_prefetch=1,
            grid=(N_SEG, n_t),
            in_specs=[pl.BlockSpec((2 * TOP_K, tr), lambda j, i, ps: (0, i))],
            out_specs=pl.BlockSpec((TOP_K, tr), lambda j, i, ps: (0, j * n_t + i)),
        ),
        out_shape=jax.ShapeDtypeStruct((TOP_K, N_SEG * n_tok), jnp.int32),
        name="slot_index",
    )(pad_start.astype(jnp.int32), ri)
    block_start = (jnp.concatenate([pad_start, pad_end[-1:]]) // EXPERT_BLOCK).astype(jnp.int32)
    n_used = block_start[-1:]
    return dest_seg, block_start, n_used, nb


def kernel(x_prompt, x_sample, state_conv, state_gla, w_in, w_dw, b_dw, ln_conv_g, ln_conv_b,
           w_gate_lr, b_gate, gla_norm_g, w_o, ln1_g, ln1_b, w_router, b_router, w_gu, b_gu,
           w_down, b_down, ln2_g, ln2_b):
    assert w_in.shape[0] == DEPTH
    l = 0
    p = _prep_params(w_in[l], w_dw[l], b_dw[l], ln_conv_g[l], ln_conv_b[l], w_gate_lr[l],
                     b_gate[l], gla_norm_g[l], w_o[l], ln1_g[l], ln1_b[l], w_router[l],
                     b_router[l])
    bp, tp, _ = x_prompt.shape
    bs, ts, _ = x_sample.shape
    zc = jnp.zeros((bp, CONV_W - 1, CONV_CH), F32)
    zs = jnp.zeros((bp, GLA_HEADS, GLA_DK, GLA_DV), F32)
    n_prompt = bp * tp
    n_tok = n_prompt + bs * ts
    x1p, x1, lg, conv_p, gla_p = _mixer(x_prompt, zc, zs, p, tm=PROMPT_TILE, n_tok_total=n_tok,
                                        tok_offset=0)
    x1p, x1, lg, conv_s, gla_s = _mixer(x_sample, state_conv[l], state_gla[l], p, tm=ts,
                                        n_tok_total=n_tok, tok_offset=n_prompt,
                                        token_bufs=(x1p, x1, lg),
                                        sb=SAMPLE_STREAMS if bs % SAMPLE_STREAMS == 0 else 1)

    tr = ROUTE_TILE if n_tok % ROUTE_TILE == 0 else bs * ts
    ri, rg, counts = _route(lg, tr=tr)
    dest_seg, block_start, n_used, nb = _moe_layout(ri, counts, tr=n_tok)
    n_slot = nb * EXPERT_BLOCK
    xs = _sc_scatter_rows(x1p.reshape(N_SEG * n_tok, SC_SEG), dest_seg, N_SEG * n_slot)
    yb = _ffn(block_start, n_used, xs.reshape(N_SEG, n_slot, SC_SEG), w_gu[l], b_gu[l], w_down[l],
              b_down[l])
    tf_last = bs * ts
    n_blk = n_prompt // tf_last
    last_blk = min(LAST_PIECE_BLOCKS, n_blk // 2)
    body_tok = (n_blk - last_blk) * tf_last
    tf_body = 2 * tf_last if body_tok % (2 * tf_last) == 0 else tf_last
    body_blk = body_tok // tf_body
    n_body = min(COMBINE_PIECES - 1, body_blk)
    bounds = [(p * body_blk // n_body) * tf_body for p in range(n_body)] + [body_tok, n_tok]
    dest_kjt = dest_seg.reshape(TOP_K, N_SEG, n_tok)
    yp = None
    for t0, t1 in zip(bounds[:-1], bounds[1:]):
        g = _sc_gather_rows(yb.reshape(N_SEG * n_slot, SC_SEG),
                            dest_kjt[:, :, t0:t1].reshape(1, -1))
        outs = _combine(x1, g.reshape(TOP_K, N_SEG, t1 - t0, SC_SEG), rg,
                        ln2_g[l].reshape(1, -1), ln2_b[l].reshape(1, -1), tok0=t0,
                        n_prompt=n_prompt, tf=tf_last if t1 == n_tok else tf_body, yp_buf=yp)
        yp = outs[0]
    ys = outs[1]
    return (yp.reshape(bp, tp, D_MODEL), ys.reshape(bs, ts, D_MODEL),
            conv_p[None], gla_p[None], conv_s[None], gla_s[None])
```

```python
import functools

import jax
import jax.numpy as jnp
from jax import lax
from jax.experimental import pallas as pl
from jax.experimental.pallas import tpu as pltpu
from jax.experimental.pallas import tpu_sc as plsc

F32 = jnp.float32
BF16 = jnp.bfloat16

LANES = 128
D_MODEL = 1024
CHUNK = 64
SUB = 16
DIAG = 8
CONV_CH = 512
CONV_W = 31
HALO = 32
PROMPT_TILE = 1024
MAX_STATIC_TRIPS = 4
GLU_ROWS = 128
CONV_BLOCK = 64
ROUTE_TILE = 768
SAMPLE_STREAMS = 8
LAST_PIECE_BLOCKS = 4
COMBINE_PIECES = 8
TRIP_BLOCKS = 8
IN_AHEAD = TRIP_BLOCKS
IN_RING = 2 * TRIP_BLOCKS
OUT_RING = 2 * TRIP_BLOCKS
GLA_HEADS = 4
GLA_DK = 64
GLA_DV = 128
GLA_K = GLA_HEADS * GLA_DK
GLA_V = GLA_HEADS * GLA_DV
GATE_RANK = 16
GATE_PAD = 128
GATE_NORM = 16.0
MIX_W = CONV_CH + GLA_V
OFF_UV = 0
OFF_UG = OFF_UV + CONV_CH
OFF_Q = OFF_UG + CONV_CH
OFF_K = OFF_Q + GLA_K
OFF_V = OFF_K + GLA_K
OFF_G = OFF_V + GLA_V
OFF_A = OFF_G + GLA_V
IN_COLS_PAD = OFF_A + GATE_PAD
N_EXPERTS = 32
TOP_K = 4
D_FF = 1024
SWIGLU_LIMIT = 7.0
SWIGLU_ALPHA = 1.702
EXPERT_BLOCK = 256
LN_EPS = 1e-5
DEPTH = 1
DN_ALPHA = (2 * DEPTH) ** 0.25
VMEM_LIMIT = 56 * 1024 * 1024
HALF = D_MODEL // 2
SC_SEG = 256
SC_WINDOW = 128
N_SEG = HALF // SC_SEG
U32 = jnp.uint32


def _dot(a, b):
    return jnp.dot(a, b, preferred_element_type=F32)


def _dot_nt(a, b):
    return lax.dot_general(a, b, (((1,), (1,)), ((), ())), preferred_element_type=F32)


def _dot_tn(a, b):
    return lax.dot_general(a, b, (((0,), (0,)), ((), ())), preferred_element_type=F32)


def _layer_norm(x, g, b):
    mu = jnp.mean(x, axis=-1, keepdims=True)
    xc = x - mu
    var = jnp.mean(xc * xc, axis=-1, keepdims=True)
    return xc * lax.rsqrt(var + LN_EPS) * g + b


def _sigmoid(x):
    return 1.0 / (1.0 + jnp.exp(-x))


def _split_bf16(x):
    hi = x.astype(BF16)
    lo = (x - hi.astype(F32)).astype(BF16)
    return hi, lo


def _pack_segments(x):
    bits = pltpu.bitcast(x.astype(BF16).astype(F32), U32)
    word = (bits[:, :HALF] >> 16) | bits[:, HALF:]
    return [word[:, j * SC_SEG:(j + 1) * SC_SEG] for j in range(N_SEG)]


def _unpack_segments(segs):
    word = jnp.concatenate(segs, axis=1)
    lo = pltpu.bitcast(word << 16, F32)
    hi = pltpu.bitcast(word & jnp.uint32(0xFFFF0000), F32)
    return lo, hi


def _mixer_kernel(x_ref, cs_ref, gs_ref, w_in_ref, w_dw_ref, b_dw_ref, lncg_ref, lncb_ref,
                  wg_ref, bg_ref, gng_ref, w_o_ref, ln1g_ref, ln1b_ref, wr_ref, br_ref, *rest,
                  tm, chunk, n_alias, sb):
    (x1p_ref, x1_ref, logit_ref, conv_out_ref, gla_out_ref,
     ubuf, qbuf, kbuf, labuf, vbuf, gbuf, mixbuf, st_ref) = rest[n_alias:]
    t = pl.program_id(1)
    nt = pl.num_programs(1)
    n_chunks = tm // chunk
    n_sub = chunk // SUB
    n_rows = sb * tm
    us = HALO + tm

    @pl.when(t == 0)
    def _init():
        for s in range(sb):
            ubuf[s * us:s * us + HALO - (CONV_W - 1), :] = jnp.zeros(
                (HALO - (CONV_W - 1), CONV_CH), F32)
            ubuf[s * us + HALO - (CONV_W - 1):s * us + HALO, :] = cs_ref[s]
            for h in range(GLA_HEADS):
                st_ref[s * GLA_DV:(s + 1) * GLA_DV, h * GLA_DK:(h + 1) * GLA_DK] = gs_ref[s, h].T

    x = x_ref[...].reshape(n_rows, D_MODEL)
    xb = x.astype(BF16)

    gp = min(GLU_ROWS, tm) if sb == 1 else n_rows
    for r0 in range(0, n_rows, gp):
        hv = _dot(xb[r0:r0 + gp], w_in_ref[:, OFF_UV:OFF_UG])
        hg = _dot(xb[r0:r0 + gp], w_in_ref[:, OFF_UG:OFF_Q])
        u = hv * _sigmoid(hg)
        for s in range(sb):
            lo = max(r0, s * tm)
            hi = min(r0 + gp, (s + 1) * tm)
            if lo < hi:
                ubuf[s * us + HALO + lo - s * tm:s * us + HALO + hi - s * tm, :] = u[lo - r0:hi - r0]
    lead = HALO - (CONV_W - 1)

    def proj_q(r0, n):
        qbuf[r0:r0 + n, :] = _dot(xb[r0:r0 + n], w_in_ref[:, OFF_Q:OFF_K]) * (GLA_DK ** -0.5)

    def proj_k(r0, n):
        kbuf[r0:r0 + n, :] = _dot(xb[r0:r0 + n], w_in_ref[:, OFF_K:OFF_V])

    def proj_v(half, r0, n):
        lo = half * (GLA_V // 2)
        vbuf[r0:r0 + n, lo:lo + GLA_V // 2] = _dot(
            xb[r0:r0 + n], w_in_ref[:, OFF_V + lo:OFF_V + lo + GLA_V // 2])

    def proj_g(half, r0, n):
        lo = half * (GLA_V // 2)
        gbuf[r0:r0 + n, lo:lo + GLA_V // 2] = _dot(
            xb[r0:r0 + n], w_in_ref[:, OFF_G + lo:OFF_G + lo + GLA_V // 2])

    def proj_gate(r0, n):
        ha = _dot(xb[r0:r0 + n], w_in_ref[:, OFF_A:IN_COLS_PAD])
        a = _dot(ha.astype(BF16), wg_ref[...]) + bg_ref[...]
        labuf[r0:r0 + n, :] = ((jnp.minimum(a, 0.0) - jnp.log(1.0 + jnp.exp(-jnp.abs(a))))
                               * (1.0 / GATE_NORM))

    proj_jobs = [proj_q, proj_k, functools.partial(proj_v, 0), functools.partial(proj_v, 1),
                 functools.partial(proj_g, 0), functools.partial(proj_g, 1), proj_gate]

    cb = min(CONV_BLOCK, tm)

    def conv_block(stream, r0):
        acc = jnp.zeros((cb, CONV_CH), F32) + b_dw_ref[...]
        for r in range(8):
            rows = cb if r == 0 else cb + 8
            part = None
            for s in range(r, lead + CONV_W, 8):
                if s < lead:
                    continue
                w8 = w_dw_ref[8 * (s - lead):8 * (s - lead) + 8, :]
                u3 = ubuf[pl.ds(stream * us + r0 + (s - r), rows), :].reshape(
                    rows // 8, 8, CONV_CH)
                term = u3 * w8
                part = term if part is None else part + term
            acc = acc + part.reshape(rows, CONV_CH)[r:r + cb, :]
        cact = _layer_norm(acc, lncg_ref[...], lncb_ref[...])
        cact = cact * _sigmoid(cact)
        mixbuf[pl.ds(stream * tm + r0, cb), 0:CONV_CH] = cact.astype(BF16)

    per_trip = 4 if n_chunks % 4 == 0 else 1
    n_trips = n_chunks // per_trip
    static_trips = 2 <= n_trips <= MAX_STATIC_TRIPS and sb == 1
    trip_rows = per_trip * chunk
    loop_blocks = (tm // 2) // (cb * n_trips) if n_chunks >= 4 else 0
    pre_blocks = tm // cb - loop_blocks * n_trips
    rows_first = trip_rows if static_trips else n_rows
    pre = [(s, blk * cb) for s in range(sb) for blk in range(pre_blocks)]
    for i in range(max(len(pre), len(proj_jobs))):
        if i < len(proj_jobs):
            proj_jobs[i](0, rows_first)
        if i < len(pre):
            conv_block(*pre[i])

    lane_k = lax.broadcasted_iota(jnp.int32, (1, GLA_K), 1) // GLA_DK
    row_c = lax.broadcasted_iota(jnp.int32, (chunk, 1), 0)
    tri = (lax.broadcasted_iota(jnp.int32, (chunk, chunk), 0)
           >= lax.broadcasted_iota(jnp.int32, (chunk, chunk), 1)).astype(BF16)
    e2 = (lax.broadcasted_iota(jnp.int32, (GLA_K, GLA_V), 0) // GLA_DK
          == lax.broadcasted_iota(jnp.int32, (GLA_K, GLA_V), 1) // GLA_DV).astype(BF16)
    row_s = lax.broadcasted_iota(jnp.int32, (GLA_HEADS * chunk, 1), 0)
    blk_s = (row_s % chunk) // SUB
    same_blk = blk_s == lax.broadcasted_iota(jnp.int32, (1, LANES), 1) // SUB
    second_half = row_c % SUB >= DIAG
    row_g = lax.broadcasted_iota(jnp.int32, (1, DIAG, 1), 1)
    lane_v = lax.broadcasted_iota(jnp.int32, (1, GLA_V), 1) % GLA_DV

    def head_stack(m):
        return jnp.concatenate(
            [jnp.where(lane_k == h, m, 0.0) for h in range(GLA_HEADS)], axis=0).astype(BF16)

    def unstack(r, width):
        return jnp.concatenate(
            [r[h * chunk:(h + 1) * chunk, h * width:(h + 1) * width] for h in range(GLA_HEADS)],
            axis=1)

    def chunk_step(stream, c, carry):
        base = stream * tm + (c * chunk if isinstance(c, int) else pl.multiple_of(c * chunk, chunk))
        st_rows = slice(stream * GLA_DV, (stream + 1) * GLA_DV)
        q_c = qbuf[pl.ds(base, chunk), :]
        k_c = kbuf[pl.ds(base, chunk), :]
        v_c = vbuf[pl.ds(base, chunk), :]
        v_b = v_c.astype(BF16)
        la_c = labuf[pl.ds(base, chunk), :]
        la_hi, la_lo = _split_bf16(la_c)
        b = _dot(tri, la_hi) + _dot(tri, la_lo)
        b_last = b[chunk - 1:chunk, :]
        st = st_ref[st_rows, :]

        qe = q_c * jnp.exp(b)
        o2 = _dot_nt(head_stack(qe), st.astype(BF16))
        o = jnp.concatenate([o2[h * chunk:(h + 1) * chunk, :] for h in range(GLA_HEADS)], axis=1)

        a_off = None
        if n_sub > 1:
            r_rows = [b[0:SUB, :]] + [jnp.broadcast_to(b[SUB * i - 1:SUB * i, :], (SUB, GLA_K))
                                      for i in range(1, n_sub)]
            r_q = jnp.concatenate(r_rows, axis=0)
            q_t = jnp.where(row_c >= SUB, q_c * jnp.exp(jnp.minimum(b - r_q, 0.0)), 0.0)
            k_parts = []
            for i in range(1, n_sub):
                r_i = b[SUB * i - 1:SUB * i, :]
                k_t = jnp.where(row_c < SUB * i, k_c * jnp.exp(jnp.minimum(r_i - b, 0.0)), 0.0)
                k_parts.append(k_t)
                k_parts.append(jnp.zeros((LANES - chunk, GLA_K), F32))
            k_cat = jnp.concatenate(k_parts, axis=0).astype(BF16)
            r = _dot_nt(head_stack(q_t), k_cat)
            a_off = r[:, 0:LANES]
            for i in range(2, n_sub):
                a_off = jnp.where(blk_s == i, r[:, (i - 1) * LANES:i * LANES], a_off)

        r_b = jnp.concatenate(
            [jnp.broadcast_to(b[SUB * i + DIAG - 1:SUB * i + DIAG, :], (SUB, GLA_K))
             for i in range(n_sub)], axis=0)
        q_b = jnp.where(second_half, q_c * jnp.exp(jnp.minimum(b - r_b, 0.0)), 0.0)
        k_b = jnp.where(second_half, 0.0, k_c * jnp.exp(jnp.minimum(r_b - b, 0.0)))
        k_b = jnp.concatenate([k_b, jnp.zeros((LANES - chunk, GLA_K), F32)], axis=0).astype(BF16)
        a_half = jnp.where(same_blk, _dot_nt(head_stack(q_b), k_b), 0.0)
        a_off = a_half if a_off is None else a_off + a_half

        q3 = q_c.reshape(chunk // DIAG, DIAG, GLA_K)
        k3 = k_c.reshape(chunk // DIAG, DIAG, GLA_K)
        b3 = b.reshape(chunk // DIAG, DIAG, GLA_K)
        ps = []
        for d in range(DIAG):
            k_s = k3 if d == 0 else pltpu.roll(k3, d, axis=1)
            b_s = b3 if d == 0 else pltpu.roll(b3, d, axis=1)
            p = jnp.where(row_g >= d, q3 * k_s * jnp.exp(jnp.minimum(b3 - b_s, 0.0)), 0.0)
            ps.append(p.reshape(chunk, GLA_K).astype(BF16))
        w_all = _dot(jnp.concatenate(ps, axis=0), e2)
        a_d = jnp.zeros((chunk, GLA_V), F32)
        for d in range(DIAG):
            a_d = jnp.where(lane_v == row_c - d, w_all[d * chunk:(d + 1) * chunk, :], a_d)
        a_off = a_off + jnp.concatenate(
            [a_d[:, h * GLA_DV:(h + 1) * GLA_DV] for h in range(GLA_HEADS)], axis=0)
        o1 = _dot(a_off[:, 0:chunk].astype(BF16), v_b)
        o = o + unstack(o1, GLA_DV)

        ke = k_c * jnp.exp(b_last - b)
        r2 = _dot_tn(v_b, ke.astype(BF16))
        upd = jnp.zeros((GLA_DV, GLA_K), F32)
        for h in range(GLA_HEADS):
            upd = upd + jnp.where(lane_k == h, r2[h * GLA_DV:(h + 1) * GLA_DV, :], 0.0)
        st_ref[st_rows, :] = st * jnp.exp(b_last) + upd

        g_c = gbuf[pl.ds(base, chunk), :]
        outs = []
        for h in range(GLA_HEADS):
            oh = o[:, h * GLA_DV:(h + 1) * GLA_DV]
            ms = jnp.mean(oh * oh, axis=-1, keepdims=True)
            outs.append(oh * lax.rsqrt(ms + LN_EPS))
        on = jnp.concatenate(outs, axis=1) * gng_ref[...]
        on = on * (g_c * _sigmoid(g_c))
        mixbuf[pl.ds(base, chunk), CONV_CH:MIX_W] = on.astype(BF16)
        return carry

    def trip(stream, i, carry, extra_jobs=()):
        for j in range(max(per_trip, loop_blocks, len(extra_jobs))):
            if j < len(extra_jobs):
                extra_jobs[j]((i + 1) * trip_rows, trip_rows)
            if j < loop_blocks:
                r0 = (pre_blocks + i * loop_blocks + j) * cb
                conv_block(stream, r0 if isinstance(r0, int) else pl.multiple_of(r0, cb))
            if j < per_trip:
                carry = chunk_step(stream, i * per_trip + j, carry)
        return carry

    def merge_rows(r0, n):
        y = _dot(mixbuf[r0:r0 + n, :], w_o_ref[...])
        x1 = _layer_norm(DN_ALPHA * x[r0:r0 + n, :] + y, ln1g_ref[...], ln1b_ref[...])
        x1_ref[r0:r0 + n, :] = x1
        for j, seg in enumerate(_pack_segments(x1)):
            x1p_ref[j, r0:r0 + n, :] = seg
        x_hi, x_lo = _split_bf16(x1)
        lg = _dot(x_hi, wr_ref[...])
        logit_ref[r0:r0 + n, :] = (lg[:, 0:LANES] + lg[:, LANES:2 * LANES]
                                   + _dot(x_lo, wr_ref[:, 0:LANES]) + br_ref[...])

    if static_trips:
        for i in range(n_trips):
            trip(0, i, 0, tuple(proj_jobs) if i + 1 < n_trips else ())
            merge_rows(i * trip_rows, trip_rows)
    else:
        for s in range(sb):
            if n_trips == 1:
                trip(s, 0, 0)
            else:
                lax.fori_loop(0, n_trips, functools.partial(trip, s), 0)
        merge_rows(0, n_rows)

    tails = []
    for s in range(sb):
        tail = ubuf[s * us + tm:s * us + tm + HALO, :]
        ubuf[s * us:s * us + HALO, :] = tail
        tails.append(tail)

    @pl.when(t == nt - 1)
    def _state_out():
        for s in range(sb):
            conv_out_ref[s] = tails[s][HALO - (CONV_W - 1):, :]
            st = st_ref[s * GLA_DV:(s + 1) * GLA_DV, :]
            for h in range(GLA_HEADS):
                gla_out_ref[s, h] = st[:, h * GLA_DK:(h + 1) * GLA_DK].T


def _mixer(x, conv_state, gla_state, p, *, tm, n_tok_total, tok_offset, token_bufs=None, sb=1):
    bsz, seq, _ = x.shape
    chunk = min(CHUNK, seq)
    rows = sb * tm
    assert seq % tm == 0 and tm % chunk == 0 and chunk % SUB == 0 and tok_offset % rows == 0
    assert bsz % sb == 0 and (sb == 1 or seq == tm)
    nt = seq // tm
    blk0 = tok_offset // rows
    full = lambda shape: pl.BlockSpec(shape, lambda b, t: (0,) * len(shape))
    alias = () if token_bufs is None else tuple(token_bufs)
    kern = functools.partial(_mixer_kernel, tm=tm, chunk=chunk, n_alias=len(alias), sb=sb)
    n_in = 16
    return pl.pallas_call(
        kern,
        grid=(bsz // sb, nt),
        input_output_aliases={n_in + i: i for i in range(len(alias))},
        in_specs=[
            pl.BlockSpec((sb, tm, D_MODEL), lambda b, t: (b, t, 0)),
            pl.BlockSpec((sb, CONV_W - 1, CONV_CH), lambda b, t: (b, 0, 0)),
            pl.BlockSpec((sb, GLA_HEADS, GLA_DK, GLA_DV), lambda b, t: (b, 0, 0, 0)),
            full((D_MODEL, IN_COLS_PAD)),
            full((8 * CONV_W, CONV_CH)),
            full((1, CONV_CH)),
            full((1, CONV_CH)),
            full((1, CONV_CH)),
            full((GATE_PAD, GLA_K)),
            full((1, GLA_K)),
            full((1, GLA_V)),
            full((MIX_W, D_MODEL)),
            full((1, D_MODEL)),
            full((1, D_MODEL)),
            full((D_MODEL, 2 * LANES)),
            full((1, LANES)),
        ] + [pl.BlockSpec(memory_space=pl.ANY)] * len(alias),
        out_specs=[
            pl.BlockSpec((N_SEG, rows, SC_SEG), lambda b, t: (0, blk0 + b * nt + t, 0)),
            pl.BlockSpec((rows, D_MODEL), lambda b, t: (blk0 + b * nt + t, 0)),
            pl.BlockSpec((rows, LANES), lambda b, t: (blk0 + b * nt + t, 0)),
            pl.BlockSpec((sb, CONV_W - 1, CONV_CH), lambda b, t: (b, 0, 0)),
            pl.BlockSpec((sb, GLA_HEADS, GLA_DK, GLA_DV), lambda b, t: (b, 0, 0, 0)),
        ],
        out_shape=[
            jax.ShapeDtypeStruct((N_SEG, n_tok_total, SC_SEG), U32),
            jax.ShapeDtypeStruct((n_tok_total, D_MODEL), F32),
            jax.ShapeDtypeStruct((n_tok_total, LANES), F32),
            jax.ShapeDtypeStruct((bsz, CONV_W - 1, CONV_CH), F32),
            jax.ShapeDtypeStruct((bsz, GLA_HEADS, GLA_DK, GLA_DV), F32),
        ],
        scratch_shapes=[
            pltpu.VMEM((sb * (HALO + tm), CONV_CH), F32),
            pltpu.VMEM((rows, GLA_K), F32),
            pltpu.VMEM((rows, GLA_K), F32),
            pltpu.VMEM((rows, GLA_K), F32),
            pltpu.VMEM((rows, GLA_V), F32),
            pltpu.VMEM((rows, GLA_V), F32),
            pltpu.VMEM((rows, MIX_W), BF16),
            pltpu.VMEM((sb * GLA_DV, GLA_K), F32),
        ],
        compiler_params=pltpu.CompilerParams(
            dimension_semantics=("arbitrary", "arbitrary"),
            vmem_limit_bytes=VMEM_LIMIT),
        name="mixer",
    )(x, conv_state, gla_state, p["w_in"], p["w_dw"], p["b_dw"], p["ln_conv_g"], p["ln_conv_b"],
      p["w_gate"], p["b_gate"], p["gla_norm_g"], p["w_o"], p["ln1_g"], p["ln1_b"],
      p["w_router"], p["b_router"], *alias)


def _route_kernel(lg_ref, ri_ref, rg_ref, cnt_ref, carry_ref, *, tr):
    i = pl.program_id(0)

    @pl.when(i == 0)
    def _init():
        carry_ref[...] = jnp.zeros((N_EXPERTS, LANES), F32)

    l = lg_ref[...].T[0:N_EXPERTS, :]
    row = lax.broadcasted_iota(jnp.int32, (N_EXPERTS, tr), 0)
    hots, vals, idxs = [], [], []
    for _ in range(TOP_K):
        m = jnp.max(l, axis=0, keepdims=True)
        ik = jnp.min(jnp.where(l == m, row, N_EXPERTS), axis=0, keepdims=True)
        hot = row == ik
        hots.append(hot)
        vals.append(m)
        idxs.append(ik)
        l = jnp.where(hot, -jnp.inf, l)
    es = [jnp.exp(v - vals[0]) for v in vals]
    denom = es[0] + es[1] + es[2] + es[3]
    member = jnp.zeros((N_EXPERTS, tr), F32)
    for hot in hots:
        member = member + hot.astype(F32)
    before = (lax.broadcasted_iota(jnp.int32, (tr, tr), 0)
              < lax.broadcasted_iota(jnp.int32, (tr, tr), 1)).astype(BF16)
    cum = _dot(member.astype(BF16), before) + carry_ref[:, 0:1]
    ranks = [jnp.sum(jnp.where(hot, cum, 0.0), axis=0, keepdims=True).astype(jnp.int32)
             for hot in hots]
    ri_ref[...] = jnp.concatenate(idxs + ranks, axis=0)
    gates = jnp.concatenate([e / denom for e in es] + [jnp.zeros((LANES - TOP_K, tr), F32)], axis=0)
    rg_ref[...] = gates.T
    carry_ref[...] = carry_ref[...] + jnp.sum(member, axis=1, keepdims=True)

    @pl.when(i == pl.num_programs(0) - 1)
    def _fin():
        cnt_ref[...] = carry_ref[...]


def _route(logits, *, tr):
    n_tok = logits.shape[0]
    assert n_tok % tr == 0
    return pl.pallas_call(
        functools.partial(_route_kernel, tr=tr),
        grid=(n_tok // tr,),
        in_specs=[pl.BlockSpec((tr, LANES), lambda i: (i, 0))],
        out_specs=[pl.BlockSpec((2 * TOP_K, tr), lambda i: (0, i)),
                   pl.BlockSpec((tr, LANES), lambda i: (i, 0)),
                   pl.BlockSpec((N_EXPERTS, LANES), lambda i: (0, 0))],
        out_shape=[jax.ShapeDtypeStruct((2 * TOP_K, n_tok), jnp.int32),
                   jax.ShapeDtypeStruct((n_tok, LANES), F32),
                   jax.ShapeDtypeStruct((N_EXPERTS, LANES), F32)],
        scratch_shapes=[pltpu.VMEM((N_EXPERTS, LANES), F32)],
        compiler_params=pltpu.CompilerParams(dimension_semantics=("arbitrary",)),
        name="route",
    )(logits)


def _ffn_kernel(gs_ref, nu_ref, xs_hbm, wgu_ref, bgu_ref, wdn_ref, bdn_ref, yb_hbm,
                wgu_bf, wdn_bf, xin, yout, in_sem, out_sem):
    e = pl.program_id(0)
    n_used = nu_ref[0]
    g_first = gs_ref[e]
    g_end = gs_ref[e + 1]

    def in_copy(g, slot):
        rows = pl.ds(pl.multiple_of(g * EXPERT_BLOCK, EXPERT_BLOCK), EXPERT_BLOCK)
        return pltpu.make_async_copy(xs_hbm.at[:, rows, :], xin.at[slot], in_sem.at[slot])

    def out_copy(g, slot):
        rows = pl.ds(pl.multiple_of(g * EXPERT_BLOCK, EXPERT_BLOCK), EXPERT_BLOCK)
        return pltpu.make_async_copy(yout.at[slot], yb_hbm.at[:, rows, :], out_sem.at[slot])

    @pl.when(e == 0)
    def _prime():
        for g in range(IN_AHEAD):
            @pl.when(g < n_used)
            def _():
                in_copy(g, g).start()

    @pl.when(g_end > g_first)
    def _cast():
        wgu_bf[...] = wgu_ref[0].astype(BF16)
        wdn_bf[...] = wdn_ref[0].astype(BF16)

    def blocks(g0, n):
        for j in range(n):
            g = g0 + j
            in_copy(g, g % IN_RING).wait()
        for j in range(n):
            g = g0 + IN_AHEAD + j

            @pl.when(g < n_used)
            def _prefetch():
                in_copy(g, g % IN_RING).start()
        for j in range(n):
            g = g0 + j

            @pl.when(g >= OUT_RING)
            def _free_out():
                out_copy(g - OUT_RING, g % OUT_RING).wait()
        for j in range(n):
            g = g0 + j
            x_lo, x_hi = _unpack_segments([xin[g % IN_RING, s] for s in range(N_SEG)])
            hgu = (_dot(x_lo.astype(BF16), wgu_bf[0:HALF, :])
                   + _dot(x_hi.astype(BF16), wgu_bf[HALF:, :]) + bgu_ref[0])
            gate = jnp.minimum(hgu[:, :D_FF], SWIGLU_LIMIT)
            up = jnp.clip(hgu[:, D_FF:], -SWIGLU_LIMIT, SWIGLU_LIMIT)
            act = (up + 1.0) * gate * _sigmoid(SWIGLU_ALPHA * gate)
            yb = _dot(act.astype(BF16), wdn_bf[...]) + bdn_ref[0]
            for s, seg in enumerate(_pack_segments(yb)):
                yout[g % OUT_RING, s] = seg
        for j in range(n):
            g = g0 + j
            out_copy(g, g % OUT_RING).start()

    n_blocks = g_end - g_first
    n_full = n_blocks // TRIP_BLOCKS

    def full_trip(i, carry):
        blocks(g_first + TRIP_BLOCKS * i, TRIP_BLOCKS)
        return carry

    lax.fori_loop(0, n_full, full_trip, 0)
    done = n_full * TRIP_BLOCKS
    size = TRIP_BLOCKS // 2
    while size >= 1:
        @pl.when((n_blocks // size) % 2 == 1)
        def _rest(done=done, size=size):
            blocks(g_first + done, size)
        done = done + jnp.where((n_blocks // size) % 2 == 1, size, 0)
        size //= 2

    @pl.when(e == pl.num_programs(0) - 1)
    def _drain():
        for back in range(OUT_RING, 0, -1):
            @pl.when(n_used >= back)
            def _():
                out_copy(n_used - back, (n_used - back) % OUT_RING).wait()


def _ffn(block_start, n_used, xs, w_gu, b_gu, w_down, b_down):
    n_rows = xs.shape[1]
    grid_spec = pltpu.PrefetchScalarGridSpec(
        num_scalar_prefetch=2,
        grid=(N_EXPERTS,),
        in_specs=[
            pl.BlockSpec(memory_space=pl.ANY),
            pl.BlockSpec((1, D_MODEL, 2 * D_FF), lambda e, gs, nu: (e, 0, 0)),
            pl.BlockSpec((1, 1, 2 * D_FF), lambda e, gs, nu: (e, 0, 0)),
            pl.BlockSpec((1, D_FF, D_MODEL), lambda e, gs, nu: (e, 0, 0)),
            pl.BlockSpec((1, 1, D_MODEL), lambda e, gs, nu: (e, 0, 0)),
        ],
        out_specs=pl.BlockSpec(memory_space=pl.ANY),
        scratch_shapes=[pltpu.VMEM((D_MODEL, 2 * D_FF), BF16),
                        pltpu.VMEM((D_FF, D_MODEL), BF16),
                        pltpu.VMEM((IN_RING, N_SEG, EXPERT_BLOCK, SC_SEG), U32),
                        pltpu.VMEM((OUT_RING, N_SEG, EXPERT_BLOCK, SC_SEG), U32),
                        pltpu.SemaphoreType.DMA((IN_RING,)),
                        pltpu.SemaphoreType.DMA((OUT_RING,))],
    )
    return pl.pallas_call(
        _ffn_kernel,
        grid_spec=grid_spec,
        out_shape=jax.ShapeDtypeStruct((N_SEG, n_rows, SC_SEG), U32),
        compiler_params=pltpu.CompilerParams(
            dimension_semantics=("arbitrary",), vmem_limit_bytes=VMEM_LIMIT),
        name="expert_ffn",
    )(block_start, n_used, xs, w_gu, b_gu.reshape(N_EXPERTS, 1, 2 * D_FF), w_down,
      b_down.reshape(N_EXPERTS, 1, D_MODEL))


def _combine_kernel(x1_ref, g_ref, rg_ref, ln2g_ref, ln2b_ref, *rest, blk0, n_prompt_blocks,
                    n_alias, with_sample):
    yp_ref = rest[n_alias]
    i = blk0 + pl.program_id(0)
    m_lo = jnp.zeros((x1_ref.shape[0], HALF), F32)
    m_hi = jnp.zeros((x1_ref.shape[0], HALF), F32)
    for k in range(TOP_K):
        g_lo, g_hi = _unpack_segments([g_ref[k, j] for j in range(N_SEG)])
        gate = rg_ref[:, k:k + 1]
        m_lo = m_lo + gate * g_lo
        m_hi = m_hi + gate * g_hi
    z = DN_ALPHA * x1_ref[...] + jnp.concatenate([m_lo, m_hi], axis=1)
    y = _layer_norm(z, ln2g_ref[...], ln2b_ref[...])

    if with_sample:
        ys_ref = rest[n_alias + 1]

        @pl.when(i < n_prompt_blocks)
        def _p():
            yp_ref[...] = y

        @pl.when(i >= n_prompt_blocks)
        def _s():
            ys_ref[...] = y
    else:
        yp_ref[...] = y


def _combine(x1, g, rg, ln2_g, ln2_b, *, tok0, n_prompt, tf, yp_buf=None):
    n_tok = x1.shape[0]
    piece = g.shape[2]
    n_sample = n_tok - n_prompt
    assert n_prompt % tf == 0 and tok0 % tf == 0 and piece % tf == 0
    npb = n_prompt // tf
    blk0 = tok0 // tf
    with_sample = tok0 + piece > n_prompt
    assert not with_sample or (tok0 + piece == n_tok and n_sample % tf == 0)
    alias = () if yp_buf is None else (yp_buf,)
    out_specs = [pl.BlockSpec((tf, D_MODEL), lambda i: (jnp.minimum(blk0 + i, npb - 1), 0))]
    out_shape = [jax.ShapeDtypeStruct((n_prompt, D_MODEL), F32)]
    if with_sample:
        out_specs.append(pl.BlockSpec((tf, D_MODEL), lambda i: (jnp.maximum(blk0 + i - npb, 0), 0)))
        out_shape.append(jax.ShapeDtypeStruct((n_sample, D_MODEL), F32))
    n_in = 5
    return pl.pallas_call(
        functools.partial(_combine_kernel, blk0=blk0, n_prompt_blocks=npb, n_alias=len(alias),
                          with_sample=with_sample),
        grid=(piece // tf,),
        input_output_aliases={n_in + i: i for i in range(len(alias))},
        in_specs=[pl.BlockSpec((tf, D_MODEL), lambda i: (blk0 + i, 0)),
                  pl.BlockSpec((TOP_K, N_SEG, tf, SC_SEG), lambda i: (0, 0, i, 0)),
                  pl.BlockSpec((tf, LANES), lambda i: (blk0 + i, 0)),
                  pl.BlockSpec((1, D_MODEL), lambda i: (0, 0)),
                  pl.BlockSpec((1, D_MODEL), lambda i: (0, 0))]
        + [pl.BlockSpec(memory_space=pl.ANY)] * len(alias),
        out_specs=out_specs,
        out_shape=out_shape,
        compiler_params=pltpu.CompilerParams(
            dimension_semantics=("arbitrary",), vmem_limit_bytes=VMEM_LIMIT),
        name="combine",
    )(x1, g, rg, ln2_g, ln2_b, *alias)


def _sc_mesh():
    return plsc.VectorSubcoreMesh(core_axis_name="c", subcore_axis_name="s")


def _sc_scatter_rows(src, dest, n_out):
    n_src, width = src.shape
    n_k = dest.shape[0]
    assert width == SC_SEG and n_src % SC_WINDOW == 0 and dest.shape[1] == n_src

    @pl.kernel(out_type=jax.ShapeDtypeStruct((n_out, width), src.dtype), mesh=_sc_mesh(),
               scratch_types=[])
    def scatter_kernel(src_hbm, idx_hbm, out_hbm):
        def body(src_vmem, idx_vmem):
            for k in range(n_k):
                pltpu.sync_copy(src_vmem, out_hbm.at[idx_vmem.at[k]])

        pltpu.emit_pipeline(
            body, grid=(n_src // SC_WINDOW,),
            in_specs=[pl.BlockSpec((SC_WINDOW, width), lambda i: (i, 0)),
                      pl.BlockSpec((n_k, SC_WINDOW), lambda i: (0, i))],
            out_specs=[],
            core_axis_name=("c", "s"),
            dimension_semantics=(pltpu.PARALLEL,),
        )(src_hbm, idx_hbm)

    return scatter_kernel(src, dest)


def _sc_gather_rows(src, idx):
    n_out = idx.shape[1]
    width = src.shape[1]
    assert width == SC_SEG and n_out % SC_WINDOW == 0

    @pl.kernel(out_type=jax.ShapeDtypeStruct((n_out, width), src.dtype), mesh=_sc_mesh(),
               scratch_types=[])
    def gather_kernel(src_hbm, idx_hbm, out_hbm):
        def body(idx_vmem, out_vmem):
            pltpu.sync_copy(src_hbm.at[idx_vmem.at[0]], out_vmem)

        pltpu.emit_pipeline(
            body, grid=(n_out // SC_WINDOW,),
            in_specs=[pl.BlockSpec((1, SC_WINDOW), lambda i: (0, i))],
            out_specs=[pl.BlockSpec((SC_WINDOW, width), lambda i: (i, 0))],
            core_axis_name=("c", "s"),
            dimension_semantics=(pltpu.PARALLEL,),
        )(idx_hbm, out_hbm)

    return gather_kernel(src, idx)


def _prep_params(w_in, w_dw, b_dw, ln_conv_g, ln_conv_b, w_gate_lr, b_gate, gla_norm_g, w_o,
                 ln1_g, ln1_b, w_router, b_router):
    row = lambda v: v.reshape(1, -1).astype(F32)
    w_in_p = jnp.pad(w_in, ((0, 0), (0, IN_COLS_PAD - w_in.shape[1]))).astype(BF16)
    w_gate = jnp.pad(w_gate_lr, ((0, GATE_PAD - GATE_RANK), (0, 0))).astype(BF16)
    wr_hi = w_router.astype(BF16)
    wr_lo = (w_router - wr_hi.astype(F32)).astype(BF16)
    padr = lambda m: jnp.pad(m, ((0, 0), (0, LANES - N_EXPERTS)))
    return {
        "w_in": w_in_p, "w_dw": jnp.repeat(w_dw, 8, axis=0), "b_dw": row(b_dw), "ln_conv_g": row(ln_conv_g),
        "ln_conv_b": row(ln_conv_b), "w_gate": w_gate, "b_gate": row(b_gate),
        "gla_norm_g": row(gla_norm_g), "w_o": w_o.astype(BF16), "ln1_g": row(ln1_g),
        "ln1_b": row(ln1_b), "w_router": jnp.concatenate([padr(wr_hi), padr(wr_lo)], axis=1),
        "b_router": jnp.pad(row(b_router), ((0, 0), (0, LANES - N_EXPERTS))),
    }


def _dest_kernel(ps_ref, ri_ref, out_ref, *, n_slot):
    idx = ri_ref[0:TOP_K, :]
    dest = ri_ref[TOP_K:2 * TOP_K, :] + pl.program_id(0) * n_slot
    for e in range(N_EXPERTS):
        dest = dest + jnp.where(idx == e, ps_ref[e], 0)
    out_ref[...] = dest


def _moe_layout(ri, counts, *, tr):
    n_tok = ri.shape[1]
    cnt = counts[:, 0].astype(jnp.int32)
    padded = (cnt + EXPERT_BLOCK - 1) // EXPERT_BLOCK * EXPERT_BLOCK
    pad_end = jnp.cumsum(padded)
    pad_start = pad_end - padded
    nb = -(-(n_tok * TOP_K) // EXPERT_BLOCK) + N_EXPERTS
    n_t = n_tok // tr
    dest_seg = pl.pallas_call(
        functools.partial(_dest_kernel, n_slot=nb * EXPERT_BLOCK),
        grid_spec=pltpu.PrefetchScalarGridSpec(
            num_scalar_prefetch=1,
            grid=(N_SEG, n_t),
            in_specs=[pl.BlockSpec((2 * TOP_K, tr), lambda j, i, ps: (0, i))],
            out_specs=pl.BlockSpec((TOP_K, tr), lambda j, i, ps: (0, j * n_t + i)),
        ),
        out_shape=jax.ShapeDtypeStruct((TOP_K, N_SEG * n_tok), jnp.int32),
        name="slot_index",
    )(pad_start.astype(jnp.int32), ri)
    block_start = (jnp.concatenate([pad_start, pad_end[-1:]]) // EXPERT_BLOCK).astype(jnp.int32)
    n_used = block_start[-1:]
    return dest_seg, block_start, n_used, nb


def kernel(x_prompt, x_sample, state_conv, state_gla, w_in, w_dw, b_dw, ln_conv_g, ln_conv_b,
           w_gate_lr, b_gate, gla_norm_g, w_o, ln1_g, ln1_b, w_router, b_router, w_gu, b_gu,
           w_down, b_down, ln2_g, ln2_b):
    assert w_in.shape[0] == DEPTH
    l = 0
    p = _prep_params(w_in[l], w_dw[l], b_dw[l], ln_conv_g[l], ln_conv_b[l], w_gate_lr[l],
                     b_gate[l], gla_norm_g[l], w_o[l], ln1_g[l], ln1_b[l], w_router[l],
                     b_router[l])
    bp, tp, _ = x_prompt.shape
    bs, ts, _ = x_sample.shape
    zc = jnp.zeros((bp, CONV_W - 1, CONV_CH), F32)
    zs = jnp.zeros((bp, GLA_HEADS, GLA_DK, GLA_DV), F32)
    n_prompt = bp * tp
    n_tok = n_prompt + bs * ts
    x1p, x1, lg, conv_p, gla_p = _mixer(x_prompt, zc, zs, p, tm=PROMPT_TILE, n_tok_total=n_tok,
                                        tok_offset=0)
    x1p, x1, lg, conv_s, gla_s = _mixer(x_sample, state_conv[l], state_gla[l], p, tm=ts,
                                        n_tok_total=n_tok, tok_offset=n_prompt,
                                        token_bufs=(x1p, x1, lg),
                                        sb=SAMPLE_STREAMS if bs % SAMPLE_STREAMS == 0 else 1)

    tr = ROUTE_TILE if n_tok % ROUTE_TILE == 0 else bs * ts
    ri, rg, counts = _route(lg, tr=tr)
    dest_seg, block_start, n_used, nb = _moe_layout(ri, counts, tr=n_tok)
    n_slot = nb * EXPERT_BLOCK
    xs = _sc_scatter_rows(x1p.reshape(N_SEG * n_tok, SC_SEG), dest_seg, N_SEG * n_slot)
    yb = _ffn(block_start, n_used, xs.reshape(N_SEG, n_slot, SC_SEG), w_gu[l], b_gu[l], w_down[l],
              b_down[l])
    tf_last = bs * ts
    n_blk = n_prompt // tf_last
    last_blk = min(LAST_PIECE_BLOCKS, n_blk // 2)
    body_tok = (n_blk - last_blk) * tf_last
    tf_body = 2 * tf_last if body_tok % (2 * tf_last) == 0 else tf_last
    body_blk = body_tok // tf_body
    n_body = min(COMBINE_PIECES - 1, body_blk)
    bounds = [(p * body_blk // n_body) * tf_body for p in range(n_body)] + [body_tok, n_tok]
    dest_kjt = dest_seg.reshape(TOP_K, N_SEG, n_tok)
    yp = None
    for t0, t1 in zip(bounds[:-1], bounds[1:]):
        g = _sc_gather_rows(yb.reshape(N_SEG * n_slot, SC_SEG),
                            dest_kjt[:, :, t0:t1].reshape(1, -1))
        outs = _combine(x1, g.reshape(TOP_K, N_SEG, t1 - t0, SC_SEG), rg,
                        ln2_g[l].reshape(1, -1), ln2_b[l].reshape(1, -1), tok0=t0,
                        n_prompt=n_prompt, tf=tf_last if t1 == n_tok else tf_body, yp_buf=yp)
        yp = outs[0]
    ys = outs[1]
    return (yp.reshape(bp, tp, D_MODEL), ys.reshape(bs, ts, D_MODEL),
            conv_p[None], gla_p[None], conv_s[None], gla_s[None])
```

```python
import functools

import jax
import jax.numpy as jnp
from jax import lax
from jax.experimental import pallas as pl
from jax.experimental.pallas import tpu as pltpu
from jax.experimental.pallas import tpu_sc as plsc

F32 = jnp.float32
BF16 = jnp.bfloat16

LANES = 128
D_MODEL = 1024
CHUNK = 64
SUB = 16
DIAG = 8
CONV_CH = 512
CONV_W = 31
HALO = 32
PROMPT_TILE = 1024
MAX_STATIC_TRIPS = 4
GLU_ROWS = 128
CONV_BLOCK = 64
ROUTE_TILE = 768
SAMPLE_STREAMS = 8
LAST_PIECE_BLOCKS = 4
COMBINE_PIECES = 8
TRIP_BLOCKS = 4
IN_AHEAD = TRIP_BLOCKS
IN_RING = 2 * TRIP_BLOCKS
OUT_RING = 2 * TRIP_BLOCKS
GLA_HEADS = 4
GLA_DK = 64
GLA_DV = 128
GLA_K = GLA_HEADS * GLA_DK
GLA_V = GLA_HEADS * GLA_DV
GATE_RANK = 16
GATE_PAD = 128
GATE_NORM = 16.0
MIX_W = CONV_CH + GLA_V
OFF_UV = 0
OFF_UG = OFF_UV + CONV_CH
OFF_Q = OFF_UG + CONV_CH
OFF_K = OFF_Q + GLA_K
OFF_V = OFF_K + GLA_K
OFF_G = OFF_V + GLA_V
OFF_A = OFF_G + GLA_V
IN_COLS_PAD = OFF_A + GATE_PAD
N_EXPERTS = 32
TOP_K = 4
D_FF = 1024
SWIGLU_LIMIT = 7.0
SWIGLU_ALPHA = 1.702
EXPERT_BLOCK = 256
LN_EPS = 1e-5
DEPTH = 1
DN_ALPHA = (2 * DEPTH) ** 0.25
VMEM_LIMIT = 56 * 1024 * 1024
HALF = D_MODEL // 2
SC_SEG = 256
SC_WINDOW = 128
N_SEG = HALF // SC_SEG
U32 = jnp.uint32


def _dot(a, b):
    return jnp.dot(a, b, preferred_element_type=F32)


def _dot_nt(a, b):
    return lax.dot_general(a, b, (((1,), (1,)), ((), ())), preferred_element_type=F32)


def _dot_tn(a, b):
    return lax.dot_general(a, b, (((0,), (0,)), ((), ())), preferred_element_type=F32)


def _layer_norm(x, g, b):
    mu = jnp.mean(x, axis=-1, keepdims=True)
    xc = x - mu
    var = jnp.mean(xc * xc, axis=-1, keepdims=True)
    return xc * lax.rsqrt(var + LN_EPS) * g + b


def _sigmoid(x):
    return 1.0 / (1.0 + jnp.exp(-x))


def _split_bf16(x):
    hi = x.astype(BF16)
    lo = (x - hi.astype(F32)).astype(BF16)
    return hi, lo


def _pack_segments(x):
    bits = pltpu.bitcast(x.astype(BF16).astype(F32), U32)
    word = (bits[:, :HALF] >> 16) | bits[:, HALF:]
    return [word[:, j * SC_SEG:(j + 1) * SC_SEG] for j in range(N_SEG)]


def _unpack_segments(segs):
    word = jnp.concatenate(segs, axis=1)
    lo = pltpu.bitcast(word << 16, F32)
    hi = pltpu.bitcast(word & jnp.uint32(0xFFFF0000), F32)
    return lo, hi


def _mixer_kernel(x_ref, cs_ref, gs_ref, w_in_ref, w_dw_ref, b_dw_ref, lncg_ref, lncb_ref,
                  wg_ref, bg_ref, gng_ref, w_o_ref, ln1g_ref, ln1b_ref, wr_ref, br_ref, *rest,
                  tm, chunk, n_alias, sb):
    (x1p_ref, x1_ref, logit_ref, conv_out_ref, gla_out_ref,
     ubuf, qbuf, kbuf, labuf, vbuf, gbuf, mixbuf, st_ref) = rest[n_alias:]
    t = pl.program_id(1)
    nt = pl.num_programs(1)
    n_chunks = tm // chunk
    n_sub = chunk // SUB
    n_rows = sb * tm
    us = HALO + tm

    @pl.when(t == 0)
    def _init():
        for s in range(sb):
            ubuf[s * us:s * us + HALO - (CONV_W - 1), :] = jnp.zeros(
                (HALO - (CONV_W - 1), CONV_CH), F32)
            ubuf[s * us + HALO - (CONV_W - 1):s * us + HALO, :] = cs_ref[s]
            for h in range(GLA_HEADS):
                st_ref[s * GLA_DV:(s + 1) * GLA_DV, h * GLA_DK:(h + 1) * GLA_DK] = gs_ref[s, h].T

    x = x_ref[...].reshape(n_rows, D_MODEL)
    xb = x.astype(BF16)

    gp = min(GLU_ROWS, tm) if sb == 1 else n_rows
    for r0 in range(0, n_rows, gp):
        hv = _dot(xb[r0:r0 + gp], w_in_ref[:, OFF_UV:OFF_UG])
        hg = _dot(xb[r0:r0 + gp], w_in_ref[:, OFF_UG:OFF_Q])
        u = hv * _sigmoid(hg)
        for s in range(sb):
            lo = max(r0, s * tm)
            hi = min(r0 + gp, (s + 1) * tm)
            if lo < hi:
                ubuf[s * us + HALO + lo - s * tm:s * us + HALO + hi - s * tm, :] = u[lo - r0:hi - r0]
    lead = HALO - (CONV_W - 1)

    def proj_q(r0, n):
        qbuf[r0:r0 + n, :] = _dot(xb[r0:r0 + n], w_in_ref[:, OFF_Q:OFF_K]) * (GLA_DK ** -0.5)

    def proj_k(r0, n):
        kbuf[r0:r0 + n, :] = _dot(xb[r0:r0 + n], w_in_ref[:, OFF_K:OFF_V])

    def proj_v(half, r0, n):
        lo = half * (GLA_V // 2)
        vbuf[r0:r0 + n, lo:lo + GLA_V // 2] = _dot(
            xb[r0:r0 + n], w_in_ref[:, OFF_V + lo:OFF_V + lo + GLA_V // 2])

    def proj_g(half, r0, n):
        lo = half * (GLA_V // 2)
        gbuf[r0:r0 + n, lo:lo + GLA_V // 2] = _dot(
            xb[r0:r0 + n], w_in_ref[:, OFF_G + lo:OFF_G + lo + GLA_V // 2])

    def proj_gate(r0, n):
        ha = _dot(xb[r0:r0 + n], w_in_ref[:, OFF_A:IN_COLS_PAD])
        a = _dot(ha.astype(BF16), wg_ref[...]) + bg_ref[...]
        labuf[r0:r0 + n, :] = ((jnp.minimum(a, 0.0) - jnp.log(1.0 + jnp.exp(-jnp.abs(a))))
                               * (1.0 / GATE_NORM))

    proj_jobs = [proj_q, proj_k, functools.partial(proj_v, 0), functools.partial(proj_v, 1),
                 functools.partial(proj_g, 0), functools.partial(proj_g, 1), proj_gate]

    cb = min(CONV_BLOCK, tm)

    def conv_block(stream, r0):
        acc = jnp.zeros((cb, CONV_CH), F32) + b_dw_ref[...]
        for r in range(8):
            rows = cb if r == 0 else cb + 8
            part = None
            for s in range(r, lead + CONV_W, 8):
                if s < lead:
                    continue
                w8 = w_dw_ref[8 * (s - lead):8 * (s - lead) + 8, :]
                u3 = ubuf[pl.ds(stream * us + r0 + (s - r), rows), :].reshape(
                    rows // 8, 8, CONV_CH)
                term = u3 * w8
                part = term if part is None else part + term
            acc = acc + part.reshape(rows, CONV_CH)[r:r + cb, :]
        cact = _layer_norm(acc, lncg_ref[...], lncb_ref[...])
        cact = cact * _sigmoid(cact)
        mixbuf[pl.ds(stream * tm + r0, cb), 0:CONV_CH] = cact.astype(BF16)

    per_trip = 4 if n_chunks % 4 == 0 else 1
    n_trips = n_chunks // per_trip
    static_trips = 2 <= n_trips <= MAX_STATIC_TRIPS and sb == 1
    trip_rows = per_trip * chunk
    loop_blocks = (tm // 2) // (cb * n_trips) if n_chunks >= 4 else 0
    pre_blocks = tm // cb - loop_blocks * n_trips
    rows_first = trip_rows if static_trips else n_rows
    pre = [(s, blk * cb) for s in range(sb) for blk in range(pre_blocks)]
    for i in range(max(len(pre), len(proj_jobs))):
        if i < len(proj_jobs):
            proj_jobs[i](0, rows_first)
        if i < len(pre):
            conv_block(*pre[i])

    lane_k = lax.broadcasted_iota(jnp.int32, (1, GLA_K), 1) // GLA_DK
    row_c = lax.broadcasted_iota(jnp.int32, (chunk, 1), 0)
    tri = (lax.broadcasted_iota(jnp.int32, (chunk, chunk), 0)
           >= lax.broadcasted_iota(jnp.int32, (chunk, chunk), 1)).astype(BF16)
    e2 = (lax.broadcasted_iota(jnp.int32, (GLA_K, GLA_V), 0) // GLA_DK
          == lax.broadcasted_iota(jnp.int32, (GLA_K, GLA_V), 1) // GLA_DV).astype(BF16)
    row_s = lax.broadcasted_iota(jnp.int32, (GLA_HEADS * chunk, 1), 0)
    blk_s = (row_s % chunk) // SUB
    same_blk = blk_s == lax.broadcasted_iota(jnp.int32, (1, LANES), 1) // SUB
    second_half = row_c % SUB >= DIAG
    row_g = lax.broadcasted_iota(jnp.int32, (1, DIAG, 1), 1)
    lane_v = lax.broadcasted_iota(jnp.int32, (1, GLA_V), 1) % GLA_DV

    def head_stack(m):
        return jnp.concatenate(
            [jnp.where(lane_k == h, m, 0.0) for h in range(GLA_HEADS)], axis=0).astype(BF16)

    def unstack(r, width):
        return jnp.concatenate(
            [r[h * chunk:(h + 1) * chunk, h * width:(h + 1) * width] for h in range(GLA_HEADS)],
            axis=1)

    def chunk_step(stream, c, carry):
        base = stream * tm + (c * chunk if isinstance(c, int) else pl.multiple_of(c * chunk, chunk))
        st_rows = slice(stream * GLA_DV, (stream + 1) * GLA_DV)
        q_c = qbuf[pl.ds(base, chunk), :]
        k_c = kbuf[pl.ds(base, chunk), :]
        v_c = vbuf[pl.ds(base, chunk), :]
        v_b = v_c.astype(BF16)
        la_c = labuf[pl.ds(base, chunk), :]
        la_hi, la_lo = _split_bf16(la_c)
        b = _dot(tri, la_hi) + _dot(tri, la_lo)
        b_last = b[chunk - 1:chunk, :]
        st = st_ref[st_rows, :]

        qe = q_c * jnp.exp(b)
        o2 = _dot_nt(head_stack(qe), st.astype(BF16))
        o = jnp.concatenate([o2[h * chunk:(h + 1) * chunk, :] for h in range(GLA_HEADS)], axis=1)

        a_off = None
        if n_sub > 1:
            r_rows = [b[0:SUB, :]] + [jnp.broadcast_to(b[SUB * i - 1:SUB * i, :], (SUB, GLA_K))
                                      for i in range(1, n_sub)]
            r_q = jnp.concatenate(r_rows, axis=0)
            q_t = jnp.where(row_c >= SUB, q_c * jnp.exp(jnp.minimum(b - r_q, 0.0)), 0.0)
            k_parts = []
            for i in range(1, n_sub):
                r_i = b[SUB * i - 1:SUB * i, :]
                k_t = jnp.where(row_c < SUB * i, k_c * jnp.exp(jnp.minimum(r_i - b, 0.0)), 0.0)
                k_parts.append(k_t)
                k_parts.append(jnp.zeros((LANES - chunk, GLA_K), F32))
            k_cat = jnp.concatenate(k_parts, axis=0).astype(BF16)
            r = _dot_nt(head_stack(q_t), k_cat)
            a_off = r[:, 0:LANES]
            for i in range(2, n_sub):
                a_off = jnp.where(blk_s == i, r[:, (i - 1) * LANES:i * LANES], a_off)

        r_b = jnp.concatenate(
            [jnp.broadcast_to(b[SUB * i + DIAG - 1:SUB * i + DIAG, :], (SUB, GLA_K))
             for i in range(n_sub)], axis=0)
        q_b = jnp.where(second_half, q_c * jnp.exp(jnp.minimum(b - r_b, 0.0)), 0.0)
        k_b = jnp.where(second_half, 0.0, k_c * jnp.exp(jnp.minimum(r_b - b, 0.0)))
        k_b = jnp.concatenate([k_b, jnp.zeros((LANES - chunk, GLA_K), F32)], axis=0).astype(BF16)
        a_half = jnp.where(same_blk, _dot_nt(head_stack(q_b), k_b), 0.0)
        a_off = a_half if a_off is None else a_off + a_half

        q3 = q_c.reshape(chunk // DIAG, DIAG, GLA_K)
        k3 = k_c.reshape(chunk // DIAG, DIAG, GLA_K)
        b3 = b.reshape(chunk // DIAG, DIAG, GLA_K)
        ps = []
        for d in range(DIAG):
            k_s = k3 if d == 0 else pltpu.roll(k3, d, axis=1)
            b_s = b3 if d == 0 else pltpu.roll(b3, d, axis=1)
            p = jnp.where(row_g >= d, q3 * k_s * jnp.exp(jnp.minimum(b3 - b_s, 0.0)), 0.0)
            ps.append(p.reshape(chunk, GLA_K).astype(BF16))
        w_all = _dot(jnp.concatenate(ps, axis=0), e2)
        a_d = jnp.zeros((chunk, GLA_V), F32)
        for d in range(DIAG):
            a_d = jnp.where(lane_v == row_c - d, w_all[d * chunk:(d + 1) * chunk, :], a_d)
        a_off = a_off + jnp.concatenate(
            [a_d[:, h * GLA_DV:(h + 1) * GLA_DV] for h in range(GLA_HEADS)], axis=0)
        o1 = _dot(a_off[:, 0:chunk].astype(BF16), v_b)
        o = o + unstack(o1, GLA_DV)

        ke = k_c * jnp.exp(b_last - b)
        r2 = _dot_tn(v_b, ke.astype(BF16))
        upd = jnp.zeros((GLA_DV, GLA_K), F32)
        for h in range(GLA_HEADS):
            upd = upd + jnp.where(lane_k == h, r2[h * GLA_DV:(h + 1) * GLA_DV, :], 0.0)
        st_ref[st_rows, :] = st * jnp.exp(b_last) + upd

        g_c = gbuf[pl.ds(base, chunk), :]
        outs = []
        for h in range(GLA_HEADS):
            oh = o[:, h * GLA_DV:(h + 1) * GLA_DV]
            ms = jnp.mean(oh * oh, axis=-1, keepdims=True)
            outs.append(oh * lax.rsqrt(ms + LN_EPS))
        on = jnp.concatenate(outs, axis=1) * gng_ref[...]
        on = on * (g_c * _sigmoid(g_c))
        mixbuf[pl.ds(base, chunk), CONV_CH:MIX_W] = on.astype(BF16)
        return carry

    def trip(stream, i, carry, extra_jobs=()):
        for j in range(max(per_trip, loop_blocks, len(extra_jobs))):
            if j < len(extra_jobs):
                extra_jobs[j]((i + 1) * trip_rows, trip_rows)
            if j < loop_blocks:
                r0 = (pre_blocks + i * loop_blocks + j) * cb
                conv_block(stream, r0 if isinstance(r0, int) else pl.multiple_of(r0, cb))
            if j < per_trip:
                carry = chunk_step(stream, i * per_trip + j, carry)
        return carry

    def merge_rows(r0, n):
        y = _dot(mixbuf[r0:r0 + n, :], w_o_ref[...])
        x1 = _layer_norm(DN_ALPHA * x[r0:r0 + n, :] + y, ln1g_ref[...], ln1b_ref[...])
        x1_ref[r0:r0 + n, :] = x1
        for j, seg in enumerate(_pack_segments(x1)):
            x1p_ref[j, r0:r0 + n, :] = seg
        x_hi, x_lo = _split_bf16(x1)
        lg = _dot(x_hi, wr_ref[...])
        logit_ref[r0:r0 + n, :] = (lg[:, 0:LANES] + lg[:, LANES:2 * LANES]
                                   + _dot(x_lo, wr_ref[:, 0:LANES]) + br_ref[...])

    if static_trips:
        for i in range(n_trips):
            trip(0, i, 0, tuple(proj_jobs) if i + 1 < n_trips else ())
            merge_rows(i * trip_rows, trip_rows)
    else:
        for s in range(sb):
            if n_trips == 1:
                trip(s, 0, 0)
            else:
                lax.fori_loop(0, n_trips, functools.partial(trip, s), 0)
        merge_rows(0, n_rows)

    tails = []
    for s in range(sb):
        tail = ubuf[s * us + tm:s * us + tm + HALO, :]
        ubuf[s * us:s * us + HALO, :] = tail
        tails.append(tail)

    @pl.when(t == nt - 1)
    def _state_out():
        for s in range(sb):
            conv_out_ref[s] = tails[s][HALO - (CONV_W - 1):, :]
            st = st_ref[s * GLA_DV:(s + 1) * GLA_DV, :]
            for h in range(GLA_HEADS):
                gla_out_ref[s, h] = st[:, h * GLA_DK:(h + 1) * GLA_DK].T


def _mixer(x, conv_state, gla_state, p, *, tm, n_tok_total, tok_offset, token_bufs=None, sb=1):
    bsz, seq, _ = x.shape
    chunk = min(CHUNK, seq)
    rows = sb * tm
    assert seq % tm == 0 and tm % chunk == 0 and chunk % SUB == 0 and tok_offset % rows == 0
    assert bsz % sb == 0 and (sb == 1 or seq == tm)
    nt = seq // tm
    blk0 = tok_offset // rows
    full = lambda shape: pl.BlockSpec(shape, lambda b, t: (0,) * len(shape))
    alias = () if token_bufs is None else tuple(token_bufs)
    kern = functools.partial(_mixer_kernel, tm=tm, chunk=chunk, n_alias=len(alias), sb=sb)
    n_in = 16
    return pl.pallas_call(
        kern,
        grid=(bsz // sb, nt),
        input_output_aliases={n_in + i: i for i in range(len(alias))},
        in_specs=[
            pl.BlockSpec((sb, tm, D_MODEL), lambda b, t: (b, t, 0)),
            pl.BlockSpec((sb, CONV_W - 1, CONV_CH), lambda b, t: (b, 0, 0)),
            pl.BlockSpec((sb, GLA_HEADS, GLA_DK, GLA_DV), lambda b, t: (b, 0, 0, 0)),
            full((D_MODEL, IN_COLS_PAD)),
            full((8 * CONV_W, CONV_CH)),
            full((1, CONV_CH)),
            full((1, CONV_CH)),
            full((1, CONV_CH)),
            full((GATE_PAD, GLA_K)),
            full((1, GLA_K)),
            full((1, GLA_V)),
            full((MIX_W, D_MODEL)),
            full((1, D_MODEL)),
            full((1, D_MODEL)),
            full((D_MODEL, 2 * LANES)),
            full((1, LANES)),
        ] + [pl.BlockSpec(memory_space=pl.ANY)] * len(alias),
        out_specs=[
            pl.BlockSpec((N_SEG, rows, SC_SEG), lambda b, t: (0, blk0 + b * nt + t, 0)),
            pl.BlockSpec((rows, D_MODEL), lambda b, t: (blk0 + b * nt + t, 0)),
            pl.BlockSpec((rows, LANES), lambda b, t: (blk0 + b * nt + t, 0)),
            pl.BlockSpec((sb, CONV_W - 1, CONV_CH), lambda b, t: (b, 0, 0)),
            pl.BlockSpec((sb, GLA_HEADS, GLA_DK, GLA_DV), lambda b, t: (b, 0, 0, 0)),
        ],
        out_shape=[
            jax.ShapeDtypeStruct((N_SEG, n_tok_total, SC_SEG), U32),
            jax.ShapeDtypeStruct((n_tok_total, D_MODEL), F32),
            jax.ShapeDtypeStruct((n_tok_total, LANES), F32),
            jax.ShapeDtypeStruct((bsz, CONV_W - 1, CONV_CH), F32),
            jax.ShapeDtypeStruct((bsz, GLA_HEADS, GLA_DK, GLA_DV), F32),
        ],
        scratch_shapes=[
            pltpu.VMEM((sb * (HALO + tm), CONV_CH), F32),
            pltpu.VMEM((rows, GLA_K), F32),
            pltpu.VMEM((rows, GLA_K), F32),
            pltpu.VMEM((rows, GLA_K), F32),
            pltpu.VMEM((rows, GLA_V), F32),
            pltpu.VMEM((rows, GLA_V), F32),
            pltpu.VMEM((rows, MIX_W), BF16),
            pltpu.VMEM((sb * GLA_DV, GLA_K), F32),
        ],
        compiler_params=pltpu.CompilerParams(
            dimension_semantics=("arbitrary", "arbitrary"),
            vmem_limit_bytes=VMEM_LIMIT),
        name="mixer",
    )(x, conv_state, gla_state, p["w_in"], p["w_dw"], p["b_dw"], p["ln_conv_g"], p["ln_conv_b"],
      p["w_gate"], p["b_gate"], p["gla_norm_g"], p["w_o"], p["ln1_g"], p["ln1_b"],
      p["w_router"], p["b_router"], *alias)


def _route_kernel(lg_ref, ri_ref, rg_ref, cnt_ref, carry_ref, *, tr):
    i = pl.program_id(0)

    @pl.when(i == 0)
    def _init():
        carry_ref[...] = jnp.zeros((N_EXPERTS, LANES), F32)

    l = lg_ref[...].T[0:N_EXPERTS, :]
    row = lax.broadcasted_iota(jnp.int32, (N_EXPERTS, tr), 0)
    hots, vals, idxs = [], [], []
    for _ in range(TOP_K):
        m = jnp.max(l, axis=0, keepdims=True)
        ik = jnp.min(jnp.where(l == m, row, N_EXPERTS), axis=0, keepdims=True)
        hot = row == ik
        hots.append(hot)
        vals.append(m)
        idxs.append(ik)
        l = jnp.where(hot, -jnp.inf, l)
    es = [jnp.exp(v - vals[0]) for v in vals]
    denom = es[0] + es[1] + es[2] + es[3]
    member = jnp.zeros((N_EXPERTS, tr), F32)
    for hot in hots:
        member = member + hot.astype(F32)
    before = (lax.broadcasted_iota(jnp.int32, (tr, tr), 0)
              < lax.broadcasted_iota(jnp.int32, (tr, tr), 1)).astype(BF16)
    cum = _dot(member.astype(BF16), before) + carry_ref[:, 0:1]
    ranks = [jnp.sum(jnp.where(hot, cum, 0.0), axis=0, keepdims=True).astype(jnp.int32)
             for hot in hots]
    ri_ref[...] = jnp.concatenate(idxs + ranks, axis=0)
    gates = jnp.concatenate([e / denom for e in es] + [jnp.zeros((LANES - TOP_K, tr), F32)], axis=0)
    rg_ref[...] = gates.T
    carry_ref[...] = carry_ref[...] + jnp.sum(member, axis=1, keepdims=True)

    @pl.when(i == pl.num_programs(0) - 1)
    def _fin():
        cnt_ref[...] = carry_ref[...]


def _route(logits, *, tr):
    n_tok = logits.shape[0]
    assert n_tok % tr == 0
    return pl.pallas_call(
        functools.partial(_route_kernel, tr=tr),
        grid=(n_tok // tr,),
        in_specs=[pl.BlockSpec((tr, LANES), lambda i: (i, 0))],
        out_specs=[pl.BlockSpec((2 * TOP_K, tr), lambda i: (0, i)),
                   pl.BlockSpec((tr, LANES), lambda i: (i, 0)),
                   pl.BlockSpec((N_EXPERTS, LANES), lambda i: (0, 0))],
        out_shape=[jax.ShapeDtypeStruct((2 * TOP_K, n_tok), jnp.int32),
                   jax.ShapeDtypeStruct((n_tok, LANES), F32),
                   jax.ShapeDtypeStruct((N_EXPERTS, LANES), F32)],
        scratch_shapes=[pltpu.VMEM((N_EXPERTS, LANES), F32)],
        compiler_params=pltpu.CompilerParams(dimension_semantics=("arbitrary",)),
        name="route",
    )(logits)


def _ffn_kernel(gs_ref, nu_ref, xs_hbm, wgu_ref, bgu_ref, wdn_ref, bdn_ref, yb_hbm,
                wgu_bf, wdn_bf, xin, yout, in_sem, out_sem):
    e = pl.program_id(0)
    n_used = nu_ref[0]
    g_first = gs_ref[e]
    g_end = gs_ref[e + 1]

    def in_copy(g, slot):
        rows = pl.ds(pl.multiple_of(g * EXPERT_BLOCK, EXPERT_BLOCK), EXPERT_BLOCK)
        return pltpu.make_async_copy(xs_hbm.at[:, rows, :], xin.at[slot], in_sem.at[slot])

    def out_copy(g, slot):
        rows = pl.ds(pl.multiple_of(g * EXPERT_BLOCK, EXPERT_BLOCK), EXPERT_BLOCK)
        return pltpu.make_async_copy(yout.at[slot], yb_hbm.at[:, rows, :], out_sem.at[slot])

    @pl.when(e == 0)
    def _prime():
        for g in range(IN_AHEAD):
            @pl.when(g < n_used)
            def _():
                in_copy(g, g).start()

    def cast_weights():
        wgu_bf[...] = wgu_ref[0].astype(BF16)
        wdn_bf[...] = wdn_ref[0].astype(BF16)

    def blocks(g0, n, with_cast=False):
        for j in range(n):
            g = g0 + j
            in_copy(g, g % IN_RING).wait()
        for j in range(n):
            g = g0 + IN_AHEAD + j

            @pl.when(g < n_used)
            def _prefetch():
                in_copy(g, g % IN_RING).start()
        for j in range(n):
            g = g0 + j

            @pl.when(g >= OUT_RING)
            def _free_out():
                out_copy(g - OUT_RING, g % OUT_RING).wait()
        if with_cast:
            cast_weights()
        for j in range(n):
            g = g0 + j
            x_lo, x_hi = _unpack_segments([xin[g % IN_RING, s] for s in range(N_SEG)])
            hgu = (_dot(x_lo.astype(BF16), wgu_bf[0:HALF, :])
                   + _dot(x_hi.astype(BF16), wgu_bf[HALF:, :]) + bgu_ref[0])
            gate = jnp.minimum(hgu[:, :D_FF], SWIGLU_LIMIT)
            up = jnp.clip(hgu[:, D_FF:], -SWIGLU_LIMIT, SWIGLU_LIMIT)
            act = (up + 1.0) * gate * _sigmoid(SWIGLU_ALPHA * gate)
            yb = _dot(act.astype(BF16), wdn_bf[...]) + bdn_ref[0]
            for s, seg in enumerate(_pack_segments(yb)):
                yout[g % OUT_RING, s] = seg
        for j in range(n):
            g = g0 + j
            out_copy(g, g % OUT_RING).start()

    n_blocks = g_end - g_first
    n_full = n_blocks // TRIP_BLOCKS

    @pl.when((n_full == 0) & (n_blocks > 0))
    def _cast_only():
        cast_weights()

    @pl.when(n_full >= 1)
    def _first_trip():
        blocks(g_first, TRIP_BLOCKS, with_cast=True)

    def full_trip(i, carry):
        blocks(g_first + TRIP_BLOCKS * i, TRIP_BLOCKS)
        return carry

    lax.fori_loop(1, n_full, full_trip, 0)
    done = n_full * TRIP_BLOCKS
    size = TRIP_BLOCKS // 2
    while size >= 1:
        @pl.when((n_blocks // size) % 2 == 1)
        def _rest(done=done, size=size):
            blocks(g_first + done, size)
        done = done + jnp.where((n_blocks // size) % 2 == 1, size, 0)
        size //= 2

    @pl.when(e == pl.num_programs(0) - 1)
    def _drain():
        for back in range(OUT_RING, 0, -1):
            @pl.when(n_used >= back)
            def _():
                out_copy(n_used - back, (n_used - back) % OUT_RING).wait()


def _ffn(block_start, n_used, xs, w_gu, b_gu, w_down, b_down):
    n_rows = xs.shape[1]
    grid_spec = pltpu.PrefetchScalarGridSpec(
        num_scalar_prefetch=2,
        grid=(N_EXPERTS,),
        in_specs=[
            pl.BlockSpec(memory_space=pl.ANY),
            pl.BlockSpec((1, D_MODEL, 2 * D_FF), lambda e, gs, nu: (e, 0, 0)),
            pl.BlockSpec((1, 1, 2 * D_FF), lambda e, gs, nu: (e, 0, 0)),
            pl.BlockSpec((1, D_FF, D_MODEL), lambda e, gs, nu: (e, 0, 0)),
            pl.BlockSpec((1, 1, D_MODEL), lambda e, gs, nu: (e, 0, 0)),
        ],
        out_specs=pl.BlockSpec(memory_space=pl.ANY),
        scratch_shapes=[pltpu.VMEM((D_MODEL, 2 * D_FF), BF16),
                        pltpu.VMEM((D_FF, D_MODEL), BF16),
                        pltpu.VMEM((IN_RING, N_SEG, EXPERT_BLOCK, SC_SEG), U32),
                        pltpu.VMEM((OUT_RING, N_SEG, EXPERT_BLOCK, SC_SEG), U32),
                        pltpu.SemaphoreType.DMA((IN_RING,)),
                        pltpu.SemaphoreType.DMA((OUT_RING,))],
    )
    return pl.pallas_call(
        _ffn_kernel,
        grid_spec=grid_spec,
        out_shape=jax.ShapeDtypeStruct((N_SEG, n_rows, SC_SEG), U32),
        compiler_params=pltpu.CompilerParams(
            dimension_semantics=("arbitrary",), vmem_limit_bytes=VMEM_LIMIT),
        name="expert_ffn",
    )(block_start, n_used, xs, w_gu, b_gu.reshape(N_EXPERTS, 1, 2 * D_FF), w_down,
      b_down.reshape(N_EXPERTS, 1, D_MODEL))


def _combine_kernel(x1_ref, g_ref, rg_ref, ln2g_ref, ln2b_ref, *rest, blk0, n_prompt_blocks,
                    n_alias, with_sample):
    yp_ref = rest[n_alias]
    i = blk0 + pl.program_id(0)
    m_lo = jnp.zeros((x1_ref.shape[0], HALF), F32)
    m_hi = jnp.zeros((x1_ref.shape[0], HALF), F32)
    for k in range(TOP_K):
        g_lo, g_hi = _unpack_segments([g_ref[k, j] for j in range(N_SEG)])
        gate = rg_ref[:, k:k + 1]
        m_lo = m_lo + gate * g_lo
        m_hi = m_hi + gate * g_hi
    z = DN_ALPHA * x1_ref[...] + jnp.concatenate([m_lo, m_hi], axis=1)
    y = _layer_norm(z, ln2g_ref[...], ln2b_ref[...])

    if with_sample:
        ys_ref = rest[n_alias + 1]

        @pl.when(i < n_prompt_blocks)
        def _p():
            yp_ref[...] = y

        @pl.when(i >= n_prompt_blocks)
        def _s():
            ys_ref[...] = y
    else:
        yp_ref[...] = y


def _combine(x1, g, rg, ln2_g, ln2_b, *, tok0, n_prompt, tf, yp_buf=None):
    n_tok = x1.shape[0]
    piece = g.shape[2]
    n_sample = n_tok - n_prompt
    assert n_prompt % tf == 0 and tok0 % tf == 0 and piece % tf == 0
    npb = n_prompt // tf
    blk0 = tok0 // tf
    with_sample = tok0 + piece > n_prompt
    assert not with_sample or (tok0 + piece == n_tok and n_sample % tf == 0)
    alias = () if yp_buf is None else (yp_buf,)
    out_specs = [pl.BlockSpec((tf, D_MODEL), lambda i: (jnp.minimum(blk0 + i, npb - 1), 0))]
    out_shape = [jax.ShapeDtypeStruct((n_prompt, D_MODEL), F32)]
    if with_sample:
        out_specs.append(pl.BlockSpec((tf, D_MODEL), lambda i: (jnp.maximum(blk0 + i - npb, 0), 0)))
        out_shape.append(jax.ShapeDtypeStruct((n_sample, D_MODEL), F32))
    n_in = 5
    return pl.pallas_call(
        functools.partial(_combine_kernel, blk0=blk0, n_prompt_blocks=npb, n_alias=len(alias),
                          with_sample=with_sample),
        grid=(piece // tf,),
        input_output_aliases={n_in + i: i for i in range(len(alias))},
        in_specs=[pl.BlockSpec((tf, D_MODEL), lambda i: (blk0 + i, 0)),
                  pl.BlockSpec((TOP_K, N_SEG, tf, SC_SEG), lambda i: (0, 0, i, 0)),
                  pl.BlockSpec((tf, LANES), lambda i: (blk0 + i, 0)),
                  pl.BlockSpec((1, D_MODEL), lambda i: (0, 0)),
                  pl.BlockSpec((1, D_MODEL), lambda i: (0, 0))]
        + [pl.BlockSpec(memory_space=pl.ANY)] * len(alias),
        out_specs=out_specs,
        out_shape=out_shape,
        compiler_params=pltpu.CompilerParams(
            dimension_semantics=("arbitrary",), vmem_limit_bytes=VMEM_LIMIT),
        name="combine",
    )(x1, g, rg, ln2_g, ln2_b, *alias)


def _sc_mesh():
    return plsc.VectorSubcoreMesh(core_axis_name="c", subcore_axis_name="s")


def _sc_scatter_rows(src, dest, n_out):
    n_src, width = src.shape
    n_k = dest.shape[0]
    assert width == SC_SEG and n_src % SC_WINDOW == 0 and dest.shape[1] == n_src

    @pl.kernel(out_type=jax.ShapeDtypeStruct((n_out, width), src.dtype), mesh=_sc_mesh(),
               scratch_types=[])
    def scatter_kernel(src_hbm, idx_hbm, out_hbm):
        def body(src_vmem, idx_vmem):
            for k in range(n_k):
                pltpu.sync_copy(src_vmem, out_hbm.at[idx_vmem.at[k]])

        pltpu.emit_pipeline(
            body, grid=(n_src // SC_WINDOW,),
            in_specs=[pl.BlockSpec((SC_WINDOW, width), lambda i: (i, 0)),
                      pl.BlockSpec((n_k, SC_WINDOW), lambda i: (0, i))],
            out_specs=[],
            core_axis_name=("c", "s"),
            dimension_semantics=(pltpu.PARALLEL,),
        )(src_hbm, idx_hbm)

    return scatter_kernel(src, dest)


def _sc_gather_rows(src, idx):
    n_out = idx.shape[1]
    width = src.shape[1]
    assert width == SC_SEG and n_out % SC_WINDOW == 0

    @pl.kernel(out_type=jax.ShapeDtypeStruct((n_out, width), src.dtype), mesh=_sc_mesh(),
               scratch_types=[])
    def gather_kernel(src_hbm, idx_hbm, out_hbm):
        def body(idx_vmem, out_vmem):
            pltpu.sync_copy(src_hbm.at[idx_vmem.at[0]], out_vmem)

        pltpu.emit_pipeline(
            body, grid=(n_out // SC_WINDOW,),
            in_specs=[pl.BlockSpec((1, SC_WINDOW), lambda i: (0, i))],
            out_specs=[pl.BlockSpec((SC_WINDOW, width), lambda i: (i, 0))],
            core_axis_name=("c", "s"),
            dimension_semantics=(pltpu.PARALLEL,),
        )(idx_hbm, out_hbm)

    return gather_kernel(src, idx)


def _prep_params(w_in, w_dw, b_dw, ln_conv_g, ln_conv_b, w_gate_lr, b_gate, gla_norm_g, w_o,
                 ln1_g, ln1_b, w_router, b_router):
    row = lambda v: v.reshape(1, -1).astype(F32)
    w_in_p = jnp.pad(w_in, ((0, 0), (0, IN_COLS_PAD - w_in.shape[1]))).astype(BF16)
    w_gate = jnp.pad(w_gate_lr, ((0, GATE_PAD - GATE_RANK), (0, 0))).astype(BF16)
    wr_hi = w_router.astype(BF16)
    wr_lo = (w_router - wr_hi.astype(F32)).astype(BF16)
    padr = lambda m: jnp.pad(m, ((0, 0), (0, LANES - N_EXPERTS)))
    return {
        "w_in": w_in_p, "w_dw": jnp.repeat(w_dw, 8, axis=0), "b_dw": row(b_dw), "ln_conv_g": row(ln_conv_g),
        "ln_conv_b": row(ln_conv_b), "w_gate": w_gate, "b_gate": row(b_gate),
        "gla_norm_g": row(gla_norm_g), "w_o": w_o.astype(BF16), "ln1_g": row(ln1_g),
        "ln1_b": row(ln1_b), "w_router": jnp.concatenate([padr(wr_hi), padr(wr_lo)], axis=1),
        "b_router": jnp.pad(row(b_router), ((0, 0), (0, LANES - N_EXPERTS))),
    }


def _dest_kernel(ps_ref, ri_ref, out_ref, *, n_slot):
    idx = ri_ref[0:TOP_K, :]
    dest = ri_ref[TOP_K:2 * TOP_K, :] + pl.program_id(0) * n_slot
    for e in range(N_EXPERTS):
        dest = dest + jnp.where(idx == e, ps_ref[e], 0)
    out_ref[...] = dest


def _moe_layout(ri, counts, *, tr):
    n_tok = ri.shape[1]
    cnt = counts[:, 0].astype(jnp.int32)
    padded = (cnt + EXPERT_BLOCK - 1) // EXPERT_BLOCK * EXPERT_BLOCK
    pad_end = jnp.cumsum(padded)
    pad_start = pad_end - padded
    nb = -(-(n_tok * TOP_K) // EXPERT_BLOCK) + N_EXPERTS
    n_t = n_tok // tr
    dest_seg = pl.pallas_call(
        functools.partial(_dest_kernel, n_slot=nb * EXPERT_BLOCK),
        grid_spec=pltpu.PrefetchScalarGridSpec(
            num_scalar_prefetch=1,
            grid=(N_SEG, n_t),
            in_specs=[pl.BlockSpec((2 * TOP_K, tr), lambda j, i, ps: (0, i))],
            out_specs=pl.BlockSpec((TOP_K, tr), lambda j, i, ps: (0, j * n_t + i)),
        ),
        out_shape=jax.ShapeDtypeStruct((TOP_K, N_SEG * n_tok), jnp.int32),
        name="slot_index",
    )(pad_start.astype(jnp.int32), ri)
    block_start = (jnp.concatenate([pad_start, pad_end[-1:]]) // EXPERT_BLOCK).astype(jnp.int32)
    n_used = block_start[-1:]
    return dest_seg, block_start, n_used, nb


def kernel(x_prompt, x_sample, state_conv, state_gla, w_in, w_dw, b_dw, ln_conv_g, ln_conv_b,
           w_gate_lr, b_gate, gla_norm_g, w_o, ln1_g, ln1_b, w_router, b_router, w_gu, b_gu,
           w_down, b_down, ln2_g, ln2_b):
    assert w_in.shape[0] == DEPTH
    l = 0
    p = _prep_params(w_in[l], w_dw[l], b_dw[l], ln_conv_g[l], ln_conv_b[l], w_gate_lr[l],
                     b_gate[l], gla_norm_g[l], w_o[l], ln1_g[l], ln1_b[l], w_router[l],
                     b_router[l])
    bp, tp, _ = x_prompt.shape
    bs, ts, _ = x_sample.shape
    zc = jnp.zeros((bp, CONV_W - 1, CONV_CH), F32)
    zs = jnp.zeros((bp, GLA_HEADS, GLA_DK, GLA_DV), F32)
    n_prompt = bp * tp
    n_tok = n_prompt + bs * ts
    x1p, x1, lg, conv_p, gla_p = _mixer(x_prompt, zc, zs, p, tm=PROMPT_TILE, n_tok_total=n_tok,
                                        tok_offset=0)
    x1p, x1, lg, conv_s, gla_s = _mixer(x_sample, state_conv[l], state_gla[l], p, tm=ts,
                                        n_tok_total=n_tok, tok_offset=n_prompt,
                                        token_bufs=(x1p, x1, lg),
                                        sb=SAMPLE_STREAMS if bs % SAMPLE_STREAMS == 0 else 1)

    tr = ROUTE_TILE if n_tok % ROUTE_TILE == 0 else bs * ts
    ri, rg, counts = _route(lg, tr=tr)
    dest_seg, block_start, n_used, nb = _moe_layout(ri, counts, tr=n_tok)
    n_slot = nb * EXPERT_BLOCK
    xs = _sc_scatter_rows(x1p.reshape(N_SEG * n_tok, SC_SEG), dest_seg, N_SEG * n_slot)
    yb = _ffn(block_start, n_used, xs.reshape(N_SEG, n_slot, SC_SEG), w_gu[l], b_gu[l], w_down[l],
              b_down[l])
    tf_last = bs * ts
    n_blk = n_prompt // tf_last
    last_blk = min(LAST_PIECE_BLOCKS, n_blk // 2)
    body_tok = (n_blk - last_blk) * tf_last
    tf_body = 2 * tf_last if body_tok % (2 * tf_last) == 0 else tf_last
    body_blk = body_tok // tf_body
    n_body = min(COMBINE_PIECES - 1, body_blk)
    bounds = [(p * body_blk // n_body) * tf_body for p in range(n_body)] + [body_tok, n_tok]
    dest_kjt = dest_seg.reshape(TOP_K, N_SEG, n_tok)
    yp = None
    for t0, t1 in zip(bounds[:-1], bounds[1:]):
        g = _sc_gather_rows(yb.reshape(N_SEG * n_slot, SC_SEG),
                            dest_kjt[:, :, t0:t1].reshape(1, -1))
        outs = _combine(x1, g.reshape(TOP_K, N_SEG, t1 - t0, SC_SEG), rg,
                        ln2_g[l].reshape(1, -1), ln2_b[l].reshape(1, -1), tok0=t0,
                        n_prompt=n_prompt, tf=tf_last if t1 == n_tok else tf_body, yp_buf=yp)
        yp = outs[0]
    ys = outs[1]
    return (yp.reshape(bp, tp, D_MODEL), ys.reshape(bs, ts, D_MODEL),
            conv_p[None], gla_p[None], conv_s[None], gla_s[None])
```

```python
import functools

import jax
import jax.numpy as jnp
from jax import lax
from jax.experimental import pallas as pl
from jax.experimental.pallas import tpu as pltpu
from jax.experimental.pallas import tpu_sc as plsc

F32 = jnp.float32
BF16 = jnp.bfloat16

LANES = 128
D_MODEL = 1024
CHUNK = 64
SUB = 16
DIAG = 8
CONV_CH = 512
CONV_W = 31
HALO = 32
PROMPT_TILE = 1024
MAX_STATIC_TRIPS = 4
GLU_ROWS = 128
CONV_BLOCK = 64
ROUTE_TILE = 768
SAMPLE_STREAMS = 16
LAST_PIECE_BLOCKS = 4
COMBINE_PIECES = 8
TRIP_BLOCKS = 4
IN_AHEAD = TRIP_BLOCKS
IN_RING = 2 * TRIP_BLOCKS
OUT_RING = 2 * TRIP_BLOCKS
GLA_HEADS = 4
GLA_DK = 64
GLA_DV = 128
GLA_K = GLA_HEADS * GLA_DK
GLA_V = GLA_HEADS * GLA_DV
GATE_RANK = 16
GATE_PAD = 128
GATE_NORM = 16.0
MIX_W = CONV_CH + GLA_V
OFF_UV = 0
OFF_UG = OFF_UV + CONV_CH
OFF_Q = OFF_UG + CONV_CH
OFF_K = OFF_Q + GLA_K
OFF_V = OFF_K + GLA_K
OFF_G = OFF_V + GLA_V
OFF_A = OFF_G + GLA_V
IN_COLS_PAD = OFF_A + GATE_PAD
N_EXPERTS = 32
TOP_K = 4
D_FF = 1024
SWIGLU_LIMIT = 7.0
SWIGLU_ALPHA = 1.702
EXPERT_BLOCK = 256
LN_EPS = 1e-5
DEPTH = 1
DN_ALPHA = (2 * DEPTH) ** 0.25
VMEM_LIMIT = 56 * 1024 * 1024
HALF = D_MODEL // 2
SC_SEG = 256
SC_WINDOW = 128
N_SEG = HALF // SC_SEG
U32 = jnp.uint32


def _dot(a, b):
    return jnp.dot(a, b, preferred_element_type=F32)


def _dot_nt(a, b):
    return lax.dot_general(a, b, (((1,), (1,)), ((), ())), preferred_element_type=F32)


def _dot_tn(a, b):
    return lax.dot_general(a, b, (((0,), (0,)), ((), ())), preferred_element_type=F32)


def _layer_norm(x, g, b):
    mu = jnp.mean(x, axis=-1, keepdims=True)
    xc = x - mu
    var = jnp.mean(xc * xc, axis=-1, keepdims=True)
    return xc * lax.rsqrt(var + LN_EPS) * g + b


def _sigmoid(x):
    return 1.0 / (1.0 + jnp.exp(-x))


def _split_bf16(x):
    hi = x.astype(BF16)
    lo = (x - hi.astype(F32)).astype(BF16)
    return hi, lo


def _pack_segments(x):
    bits = pltpu.bitcast(x.astype(BF16).astype(F32), U32)
    word = (bits[:, :HALF] >> 16) | bits[:, HALF:]
    return [word[:, j * SC_SEG:(j + 1) * SC_SEG] for j in range(N_SEG)]


def _unpack_segments(segs):
    word = jnp.concatenate(segs, axis=1)
    lo = pltpu.bitcast(word << 16, F32)
    hi = pltpu.bitcast(word & jnp.uint32(0xFFFF0000), F32)
    return lo, hi


def _mixer_kernel(x_ref, cs_ref, gs_ref, w_in_ref, w_dw_ref, b_dw_ref, lncg_ref, lncb_ref,
                  wg_ref, bg_ref, gng_ref, w_o_ref, ln1g_ref, ln1b_ref, wr_ref, br_ref, *rest,
                  tm, chunk, n_alias, sb):
    (x1p_ref, x1_ref, logit_ref, conv_out_ref, gla_out_ref,
     ubuf, qbuf, kbuf, labuf, vbuf, gbuf, mixbuf, st_ref) = rest[n_alias:]
    t = pl.program_id(1)
    nt = pl.num_programs(1)
    n_chunks = tm // chunk
    n_sub = chunk // SUB
    n_rows = sb * tm
    us = HALO + tm

    @pl.when(t == 0)
    def _init():
        for s in range(sb):
            ubuf[s * us:s * us + HALO - (CONV_W - 1), :] = jnp.zeros(
                (HALO - (CONV_W - 1), CONV_CH), F32)
            ubuf[s * us + HALO - (CONV_W - 1):s * us + HALO, :] = cs_ref[s]
            for h in range(GLA_HEADS):
                st_ref[s * GLA_DV:(s + 1) * GLA_DV, h * GLA_DK:(h + 1) * GLA_DK] = gs_ref[s, h].T

    x = x_ref[...].reshape(n_rows, D_MODEL)
    xb = x.astype(BF16)

    gp = min(GLU_ROWS, tm) if sb == 1 else n_rows
    for r0 in range(0, n_rows, gp):
        hv = _dot(xb[r0:r0 + gp], w_in_ref[:, OFF_UV:OFF_UG])
        hg = _dot(xb[r0:r0 + gp], w_in_ref[:, OFF_UG:OFF_Q])
        u = hv * _sigmoid(hg)
        for s in range(sb):
            lo = max(r0, s * tm)
            hi = min(r0 + gp, (s + 1) * tm)
            if lo < hi:
                ubuf[s * us + HALO + lo - s * tm:s * us + HALO + hi - s * tm, :] = u[lo - r0:hi - r0]
    lead = HALO - (CONV_W - 1)

    def proj_q(r0, n):
        qbuf[r0:r0 + n, :] = _dot(xb[r0:r0 + n], w_in_ref[:, OFF_Q:OFF_K]) * (GLA_DK ** -0.5)

    def proj_k(r0, n):
        kbuf[r0:r0 + n, :] = _dot(xb[r0:r0 + n], w_in_ref[:, OFF_K:OFF_V])

    def proj_v(half, r0, n):
        lo = half * (GLA_V // 2)
        vbuf[r0:r0 + n, lo:lo + GLA_V // 2] = _dot(
            xb[r0:r0 + n], w_in_ref[:, OFF_V + lo:OFF_V + lo + GLA_V // 2])

    def proj_g(half, r0, n):
        lo = half * (GLA_V // 2)
        gbuf[r0:r0 + n, lo:lo + GLA_V // 2] = _dot(
            xb[r0:r0 + n], w_in_ref[:, OFF_G + lo:OFF_G + lo + GLA_V // 2])

    def proj_gate(r0, n):
        ha = _dot(xb[r0:r0 + n], w_in_ref[:, OFF_A:IN_COLS_PAD])
        a = _dot(ha.astype(BF16), wg_ref[...]) + bg_ref[...]
        labuf[r0:r0 + n, :] = ((jnp.minimum(a, 0.0) - jnp.log(1.0 + jnp.exp(-jnp.abs(a))))
                               * (1.0 / GATE_NORM))

    proj_jobs = [proj_q, proj_k, functools.partial(proj_v, 0), functools.partial(proj_v, 1),
                 functools.partial(proj_g, 0), functools.partial(proj_g, 1), proj_gate]

    cb = min(CONV_BLOCK, tm)

    def conv_block(stream, r0):
        acc = jnp.zeros((cb, CONV_CH), F32) + b_dw_ref[...]
        for r in range(8):
            rows = cb if r == 0 else cb + 8
            part = None
            for s in range(r, lead + CONV_W, 8):
                if s < lead:
                    continue
                w8 = w_dw_ref[8 * (s - lead):8 * (s - lead) + 8, :]
                u3 = ubuf[pl.ds(stream * us + r0 + (s - r), rows), :].reshape(
                    rows // 8, 8, CONV_CH)
                term = u3 * w8
                part = term if part is None else part + term
            acc = acc + part.reshape(rows, CONV_CH)[r:r + cb, :]
        cact = _layer_norm(acc, lncg_ref[...], lncb_ref[...])
        cact = cact * _sigmoid(cact)
        mixbuf[pl.ds(stream * tm + r0, cb), 0:CONV_CH] = cact.astype(BF16)

    per_trip = 4 if n_chunks % 4 == 0 else 1
    n_trips = n_chunks // per_trip
    static_trips = 2 <= n_trips <= MAX_STATIC_TRIPS and sb == 1
    trip_rows = per_trip * chunk
    loop_blocks = (tm // 2) // (cb * n_trips) if n_chunks >= 4 else 0
    pre_blocks = tm // cb - loop_blocks * n_trips
    rows_first = trip_rows if static_trips else n_rows
    pre = [(s, blk * cb) for s in range(sb) for blk in range(pre_blocks)]
    for i in range(max(len(pre), len(proj_jobs))):
        if i < len(proj_jobs):
            proj_jobs[i](0, rows_first)
        if i < len(pre):
            conv_block(*pre[i])

    lane_k = lax.broadcasted_iota(jnp.int32, (1, GLA_K), 1) // GLA_DK
    row_c = lax.broadcasted_iota(jnp.int32, (chunk, 1), 0)
    tri = (lax.broadcasted_iota(jnp.int32, (chunk, chunk), 0)
           >= lax.broadcasted_iota(jnp.int32, (chunk, chunk), 1)).astype(BF16)
    e2 = (lax.broadcasted_iota(jnp.int32, (GLA_K, GLA_V), 0) // GLA_DK
          == lax.broadcasted_iota(jnp.int32, (GLA_K, GLA_V), 1) // GLA_DV).astype(BF16)
    row_s = lax.broadcasted_iota(jnp.int32, (GLA_HEADS * chunk, 1), 0)
    blk_s = (row_s % chunk) // SUB
    same_blk = blk_s == lax.broadcasted_iota(jnp.int32, (1, LANES), 1) // SUB
    second_half = row_c % SUB >= DIAG
    row_g = lax.broadcasted_iota(jnp.int32, (1, DIAG, 1), 1)
    lane_v = lax.broadcasted_iota(jnp.int32, (1, GLA_V), 1) % GLA_DV

    def head_stack(m):
        return jnp.concatenate(
            [jnp.where(lane_k == h, m, 0.0) for h in range(GLA_HEADS)], axis=0).astype(BF16)

    def unstack(r, width):
        return jnp.concatenate(
            [r[h * chunk:(h + 1) * chunk, h * width:(h + 1) * width] for h in range(GLA_HEADS)],
            axis=1)

    def chunk_step(stream, c, carry):
        base = stream * tm + (c * chunk if isinstance(c, int) else pl.multiple_of(c * chunk, chunk))
        st_rows = slice(stream * GLA_DV, (stream + 1) * GLA_DV)
        q_c = qbuf[pl.ds(base, chunk), :]
        k_c = kbuf[pl.ds(base, chunk), :]
        v_c = vbuf[pl.ds(base, chunk), :]
        v_b = v_c.astype(BF16)
        la_c = labuf[pl.ds(base, chunk), :]
        la_hi, la_lo = _split_bf16(la_c)
        b = _dot(tri, la_hi) + _dot(tri, la_lo)
        b_last = b[chunk - 1:chunk, :]
        st = st_ref[st_rows, :]

        qe = q_c * jnp.exp(b)
        o2 = _dot_nt(head_stack(qe), st.astype(BF16))
        o = jnp.concatenate([o2[h * chunk:(h + 1) * chunk, :] for h in range(GLA_HEADS)], axis=1)

        a_off = None
        if n_sub > 1:
            r_rows = [b[0:SUB, :]] + [jnp.broadcast_to(b[SUB * i - 1:SUB * i, :], (SUB, GLA_K))
                                      for i in range(1, n_sub)]
            r_q = jnp.concatenate(r_rows, axis=0)
            q_t = jnp.where(row_c >= SUB, q_c * jnp.exp(jnp.minimum(b - r_q, 0.0)), 0.0)
            k_parts = []
            for i in range(1, n_sub):
                r_i = b[SUB * i - 1:SUB * i, :]
                k_t = jnp.where(row_c < SUB * i, k_c * jnp.exp(jnp.minimum(r_i - b, 0.0)), 0.0)
                k_parts.append(k_t)
                k_parts.append(jnp.zeros((LANES - chunk, GLA_K), F32))
            k_cat = jnp.concatenate(k_parts, axis=0).astype(BF16)
            r = _dot_nt(head_stack(q_t), k_cat)
            a_off = r[:, 0:LANES]
            for i in range(2, n_sub):
                a_off = jnp.where(blk_s == i, r[:, (i - 1) * LANES:i * LANES], a_off)

        r_b = jnp.concatenate(
            [jnp.broadcast_to(b[SUB * i + DIAG - 1:SUB * i + DIAG, :], (SUB, GLA_K))
             for i in range(n_sub)], axis=0)
        q_b = jnp.where(second_half, q_c * jnp.exp(jnp.minimum(b - r_b, 0.0)), 0.0)
        k_b = jnp.where(second_half, 0.0, k_c * jnp.exp(jnp.minimum(r_b - b, 0.0)))
        k_b = jnp.concatenate([k_b, jnp.zeros((LANES - chunk, GLA_K), F32)], axis=0).astype(BF16)
        a_half = jnp.where(same_blk, _dot_nt(head_stack(q_b), k_b), 0.0)
        a_off = a_half if a_off is None else a_off + a_half

        q3 = q_c.reshape(chunk // DIAG, DIAG, GLA_K)
        k3 = k_c.reshape(chunk // DIAG, DIAG, GLA_K)
        b3 = b.reshape(chunk // DIAG, DIAG, GLA_K)
        ps = []
        for d in range(DIAG):
            k_s = k3 if d == 0 else pltpu.roll(k3, d, axis=1)
            b_s = b3 if d == 0 else pltpu.roll(b3, d, axis=1)
            p = jnp.where(row_g >= d, q3 * k_s * jnp.exp(jnp.minimum(b3 - b_s, 0.0)), 0.0)
            ps.append(p.reshape(chunk, GLA_K).astype(BF16))
        w_all = _dot(jnp.concatenate(ps, axis=0), e2)
        a_d = jnp.zeros((chunk, GLA_V), F32)
        for d in range(DIAG):
            a_d = jnp.where(lane_v == row_c - d, w_all[d * chunk:(d + 1) * chunk, :], a_d)
        a_off = a_off + jnp.concatenate(
            [a_d[:, h * GLA_DV:(h + 1) * GLA_DV] for h in range(GLA_HEADS)], axis=0)
        o1 = _dot(a_off[:, 0:chunk].astype(BF16), v_b)
        o = o + unstack(o1, GLA_DV)

        ke = k_c * jnp.exp(b_last - b)
        r2 = _dot_tn(v_b, ke.astype(BF16))
        upd = jnp.zeros((GLA_DV, GLA_K), F32)
        for h in range(GLA_HEADS):
            upd = upd + jnp.where(lane_k == h, r2[h * GLA_DV:(h + 1) * GLA_DV, :], 0.0)
        st_ref[st_rows, :] = st * jnp.exp(b_last) + upd

        g_c = gbuf[pl.ds(base, chunk), :]
        outs = []
        for h in range(GLA_HEADS):
            oh = o[:, h * GLA_DV:(h + 1) * GLA_DV]
            ms = jnp.mean(oh * oh, axis=-1, keepdims=True)
            outs.append(oh * lax.rsqrt(ms + LN_EPS))
        on = jnp.concatenate(outs, axis=1) * gng_ref[...]
        on = on * (g_c * _sigmoid(g_c))
        mixbuf[pl.ds(base, chunk), CONV_CH:MIX_W] = on.astype(BF16)
        return carry

    def trip(stream, i, carry, extra_jobs=()):
        for j in range(max(per_trip, loop_blocks, len(extra_jobs))):
            if j < len(extra_jobs):
                extra_jobs[j]((i + 1) * trip_rows, trip_rows)
            if j < loop_blocks:
                r0 = (pre_blocks + i * loop_blocks + j) * cb
                conv_block(stream, r0 if isinstance(r0, int) else pl.multiple_of(r0, cb))
            if j < per_trip:
                carry = chunk_step(stream, i * per_trip + j, carry)
        return carry

    def merge_rows(r0, n):
        y = _dot(mixbuf[r0:r0 + n, :], w_o_ref[...])
        x1 = _layer_norm(DN_ALPHA * x[r0:r0 + n, :] + y, ln1g_ref[...], ln1b_ref[...])
        x1_ref[r0:r0 + n, :] = x1
        for j, seg in enumerate(_pack_segments(x1)):
            x1p_ref[j, r0:r0 + n, :] = seg
        x_hi, x_lo = _split_bf16(x1)
        lg = _dot(x_hi, wr_ref[...])
        logit_ref[r0:r0 + n, :] = (lg[:, 0:LANES] + lg[:, LANES:2 * LANES]
                                   + _dot(x_lo, wr_ref[:, 0:LANES]) + br_ref[...])

    if static_trips:
        for i in range(n_trips):
            trip(0, i, 0, tuple(proj_jobs) if i + 1 < n_trips else ())
            merge_rows(i * trip_rows, trip_rows)
    else:
        for s in range(sb):
            if n_trips == 1:
                trip(s, 0, 0)
            else:
                lax.fori_loop(0, n_trips, functools.partial(trip, s), 0)
        merge_rows(0, n_rows)

    tails = []
    for s in range(sb):
        tail = ubuf[s * us + tm:s * us + tm + HALO, :]
        ubuf[s * us:s * us + HALO, :] = tail
        tails.append(tail)

    @pl.when(t == nt - 1)
    def _state_out():
        for s in range(sb):
            conv_out_ref[s] = tails[s][HALO - (CONV_W - 1):, :]
            st = st_ref[s * GLA_DV:(s + 1) * GLA_DV, :]
            for h in range(GLA_HEADS):
                gla_out_ref[s, h] = st[:, h * GLA_DK:(h + 1) * GLA_DK].T


def _mixer(x, conv_state, gla_state, p, *, tm, n_tok_total, tok_offset, token_bufs=None, sb=1):
    bsz, seq, _ = x.shape
    chunk = min(CHUNK, seq)
    rows = sb * tm
    assert seq % tm == 0 and tm % chunk == 0 and chunk % SUB == 0 and tok_offset % rows == 0
    assert bsz % sb == 0 and (sb == 1 or seq == tm)
    nt = seq // tm
    blk0 = tok_offset // rows
    full = lambda shape: pl.BlockSpec(shape, lambda b, t: (0,) * len(shape))
    alias = () if token_bufs is None else tuple(token_bufs)
    kern = functools.partial(_mixer_kernel, tm=tm, chunk=chunk, n_alias=len(alias), sb=sb)
    n_in = 16
    return pl.pallas_call(
        kern,
        grid=(bsz // sb, nt),
        input_output_aliases={n_in + i: i for i in range(len(alias))},
        in_specs=[
            pl.BlockSpec((sb, tm, D_MODEL), lambda b, t: (b, t, 0)),
            pl.BlockSpec((sb, CONV_W - 1, CONV_CH), lambda b, t: (b, 0, 0)),
            pl.BlockSpec((sb, GLA_HEADS, GLA_DK, GLA_DV), lambda b, t: (b, 0, 0, 0)),
            full((D_MODEL, IN_COLS_PAD)),
            full((8 * CONV_W, CONV_CH)),
            full((1, CONV_CH)),
            full((1, CONV_CH)),
            full((1, CONV_CH)),
            full((GATE_PAD, GLA_K)),
            full((1, GLA_K)),
            full((1, GLA_V)),
            full((MIX_W, D_MODEL)),
            full((1, D_MODEL)),
            full((1, D_MODEL)),
            full((D_MODEL, 2 * LANES)),
            full((1, LANES)),
        ] + [pl.BlockSpec(memory_space=pl.ANY)] * len(alias),
        out_specs=[
            pl.BlockSpec((N_SEG, rows, SC_SEG), lambda b, t: (0, blk0 + b * nt + t, 0)),
            pl.BlockSpec((rows, D_MODEL), lambda b, t: (blk0 + b * nt + t, 0)),
            pl.BlockSpec((rows, LANES), lambda b, t: (blk0 + b * nt + t, 0)),
            pl.BlockSpec((sb, CONV_W - 1, CONV_CH), lambda b, t: (b, 0, 0)),
            pl.BlockSpec((sb, GLA_HEADS, GLA_DK, GLA_DV), lambda b, t: (b, 0, 0, 0)),
        ],
        out_shape=[
            jax.ShapeDtypeStruct((N_SEG, n_tok_total, SC_SEG), U32),
            jax.ShapeDtypeStruct((n_tok_total, D_MODEL), F32),
            jax.ShapeDtypeStruct((n_tok_total, LANES), F32),
            jax.ShapeDtypeStruct((bsz, CONV_W - 1, CONV_CH), F32),
            jax.ShapeDtypeStruct((bsz, GLA_HEADS, GLA_DK, GLA_DV), F32),
        ],
        scratch_shapes=[
            pltpu.VMEM((sb * (HALO + tm), CONV_CH), F32),
            pltpu.VMEM((rows, GLA_K), F32),
            pltpu.VMEM((rows, GLA_K), F32),
            pltpu.VMEM((rows, GLA_K), F32),
            pltpu.VMEM((rows, GLA_V), F32),
            pltpu.VMEM((rows, GLA_V), F32),
            pltpu.VMEM((rows, MIX_W), BF16),
            pltpu.VMEM((sb * GLA_DV, GLA_K), F32),
        ],
        compiler_params=pltpu.CompilerParams(
            dimension_semantics=("arbitrary", "arbitrary"),
            vmem_limit_bytes=VMEM_LIMIT),
        name="mixer",
    )(x, conv_state, gla_state, p["w_in"], p["w_dw"], p["b_dw"], p["ln_conv_g"], p["ln_conv_b"],
      p["w_gate"], p["b_gate"], p["gla_norm_g"], p["w_o"], p["ln1_g"], p["ln1_b"],
      p["w_router"], p["b_router"], *alias)


def _route_kernel(lg_ref, ri_ref, rg_ref, cnt_ref, carry_ref, *, tr):
    i = pl.program_id(0)

    @pl.when(i == 0)
    def _init():
        carry_ref[...] = jnp.zeros((N_EXPERTS, LANES), F32)

    l = lg_ref[...].T[0:N_EXPERTS, :]
    row = lax.broadcasted_iota(jnp.int32, (N_EXPERTS, tr), 0)
    hots, vals, idxs = [], [], []
    for _ in range(TOP_K):
        m = jnp.max(l, axis=0, keepdims=True)
        ik = jnp.min(jnp.where(l == m, row, N_EXPERTS), axis=0, keepdims=True)
        hot = row == ik
        hots.append(hot)
        vals.append(m)
        idxs.append(ik)
        l = jnp.where(hot, -jnp.inf, l)
    es = [jnp.exp(v - vals[0]) for v in vals]
    denom = es[0] + es[1] + es[2] + es[3]
    member = jnp.zeros((N_EXPERTS, tr), F32)
    for hot in hots:
        member = member + hot.astype(F32)
    before = (lax.broadcasted_iota(jnp.int32, (tr, tr), 0)
              < lax.broadcasted_iota(jnp.int32, (tr, tr), 1)).astype(BF16)
    cum = _dot(member.astype(BF16), before) + carry_ref[:, 0:1]
    ranks = [jnp.sum(jnp.where(hot, cum, 0.0), axis=0, keepdims=True).astype(jnp.int32)
             for hot in hots]
    ri_ref[...] = jnp.concatenate(idxs + ranks, axis=0)
    gates = jnp.concatenate([e / denom for e in es] + [jnp.zeros((LANES - TOP_K, tr), F32)], axis=0)
    rg_ref[...] = gates.T
    carry_ref[...] = carry_ref[...] + jnp.sum(member, axis=1, keepdims=True)

    @pl.when(i == pl.num_programs(0) - 1)
    def _fin():
        cnt_ref[...] = carry_ref[...]


def _route(logits, *, tr):
    n_tok = logits.shape[0]
    assert n_tok % tr == 0
    return pl.pallas_call(
        functools.partial(_route_kernel, tr=tr),
        grid=(n_tok // tr,),
        in_specs=[pl.BlockSpec((tr, LANES), lambda i: (i, 0))],
        out_specs=[pl.BlockSpec((2 * TOP_K, tr), lambda i: (0, i)),
                   pl.BlockSpec((tr, LANES), lambda i: (i, 0)),
                   pl.BlockSpec((N_EXPERTS, LANES), lambda i: (0, 0))],
        out_shape=[jax.ShapeDtypeStruct((2 * TOP_K, n_tok), jnp.int32),
                   jax.ShapeDtypeStruct((n_tok, LANES), F32),
                   jax.ShapeDtypeStruct((N_EXPERTS, LANES), F32)],
        scratch_shapes=[pltpu.VMEM((N_EXPERTS, LANES), F32)],
        compiler_params=pltpu.CompilerParams(dimension_semantics=("arbitrary",)),
        name="route",
    )(logits)


def _ffn_kernel(gs_ref, nu_ref, xs_hbm, wgu_ref, bgu_ref, wdn_ref, bdn_ref, yb_hbm,
                wgu_bf, wdn_bf, xin, yout, in_sem, out_sem):
    e = pl.program_id(0)
    n_used = nu_ref[0]
    g_first = gs_ref[e]
    g_end = gs_ref[e + 1]

    def in_copy(g, slot):
        rows = pl.ds(pl.multiple_of(g * EXPERT_BLOCK, EXPERT_BLOCK), EXPERT_BLOCK)
        return pltpu.make_async_copy(xs_hbm.at[:, rows, :], xin.at[slot], in_sem.at[slot])

    def out_copy(g, slot):
        rows = pl.ds(pl.multiple_of(g * EXPERT_BLOCK, EXPERT_BLOCK), EXPERT_BLOCK)
        return pltpu.make_async_copy(yout.at[slot], yb_hbm.at[:, rows, :], out_sem.at[slot])

    @pl.when(e == 0)
    def _prime():
        for g in range(IN_AHEAD):
            @pl.when(g < n_used)
            def _():
                in_copy(g, g).start()

    def cast_weights():
        wgu_bf[...] = wgu_ref[0].astype(BF16)
        wdn_bf[...] = wdn_ref[0].astype(BF16)

    def blocks(g0, n, with_cast=False):
        for j in range(n):
            g = g0 + j
            in_copy(g, g % IN_RING).wait()
        for j in range(n):
            g = g0 + IN_AHEAD + j

            @pl.when(g < n_used)
            def _prefetch():
                in_copy(g, g % IN_RING).start()
        for j in range(n):
            g = g0 + j

            @pl.when(g >= OUT_RING)
            def _free_out():
                out_copy(g - OUT_RING, g % OUT_RING).wait()
        if with_cast:
            cast_weights()
        for j in range(n):
            g = g0 + j
            x_lo, x_hi = _unpack_segments([xin[g % IN_RING, s] for s in range(N_SEG)])
            hgu = (_dot(x_lo.astype(BF16), wgu_bf[0:HALF, :])
                   + _dot(x_hi.astype(BF16), wgu_bf[HALF:, :]) + bgu_ref[0])
            gate = jnp.minimum(hgu[:, :D_FF], SWIGLU_LIMIT)
            up = jnp.clip(hgu[:, D_FF:], -SWIGLU_LIMIT, SWIGLU_LIMIT)
            act = (up + 1.0) * gate * _sigmoid(SWIGLU_ALPHA * gate)
            yb = _dot(act.astype(BF16), wdn_bf[...]) + bdn_ref[0]
            for s, seg in enumerate(_pack_segments(yb)):
                yout[g % OUT_RING, s] = seg
        for j in range(n):
            g = g0 + j
            out_copy(g, g % OUT_RING).start()

    n_blocks = g_end - g_first
    n_full = n_blocks // TRIP_BLOCKS

    @pl.when((n_full == 0) & (n_blocks > 0))
    def _cast_only():
        cast_weights()

    @pl.when(n_full >= 1)
    def _first_trip():
        blocks(g_first, TRIP_BLOCKS, with_cast=True)

    def full_trip(i, carry):
        blocks(g_first + TRIP_BLOCKS * i, TRIP_BLOCKS)
        return carry

    lax.fori_loop(1, n_full, full_trip, 0)
    done = n_full * TRIP_BLOCKS
    size = TRIP_BLOCKS // 2
    while size >= 1:
        @pl.when((n_blocks // size) % 2 == 1)
        def _rest(done=done, size=size):
            blocks(g_first + done, size)
        done = done + jnp.where((n_blocks // size) % 2 == 1, size, 0)
        size //= 2

    @pl.when(e == pl.num_programs(0) - 1)
    def _drain():
        for back in range(OUT_RING, 0, -1):
            @pl.when(n_used >= back)
            def _():
                out_copy(n_used - back, (n_used - back) % OUT_RING).wait()


def _ffn(block_start, n_used, xs, w_gu, b_gu, w_down, b_down):
    n_rows = xs.shape[1]
    grid_spec = pltpu.PrefetchScalarGridSpec(
        num_scalar_prefetch=2,
        grid=(N_EXPERTS,),
        in_specs=[
            pl.BlockSpec(memory_space=pl.ANY),
            pl.BlockSpec((1, D_MODEL, 2 * D_FF), lambda e, gs, nu: (e, 0, 0)),
            pl.BlockSpec((1, 1, 2 * D_FF), lambda e, gs, nu: (e, 0, 0)),
            pl.BlockSpec((1, D_FF, D_MODEL), lambda e, gs, nu: (e, 0, 0)),
            pl.BlockSpec((1, 1, D_MODEL), lambda e, gs, nu: (e, 0, 0)),
        ],
        out_specs=pl.BlockSpec(memory_space=pl.ANY),
        scratch_shapes=[pltpu.VMEM((D_MODEL, 2 * D_FF), BF16),
                        pltpu.VMEM((D_FF, D_MODEL), BF16),
                        pltpu.VMEM((IN_RING, N_SEG, EXPERT_BLOCK, SC_SEG), U32),
                        pltpu.VMEM((OUT_RING, N_SEG, EXPERT_BLOCK, SC_SEG), U32),
                        pltpu.SemaphoreType.DMA((IN_RING,)),
                        pltpu.SemaphoreType.DMA((OUT_RING,))],
    )
    return pl.pallas_call(
        _ffn_kernel,
        grid_spec=grid_spec,
        out_shape=jax.ShapeDtypeStruct((N_SEG, n_rows, SC_SEG), U32),
        compiler_params=pltpu.CompilerParams(
            dimension_semantics=("arbitrary",), vmem_limit_bytes=VMEM_LIMIT),
        name="expert_ffn",
    )(block_start, n_used, xs, w_gu, b_gu.reshape(N_EXPERTS, 1, 2 * D_FF), w_down,
      b_down.reshape(N_EXPERTS, 1, D_MODEL))


def _combine_kernel(x1_ref, g_ref, rg_ref, ln2g_ref, ln2b_ref, *rest, blk0, n_prompt_blocks,
                    n_alias, with_sample):
    yp_ref = rest[n_alias]
    i = blk0 + pl.program_id(0)
    m_lo = jnp.zeros((x1_ref.shape[0], HALF), F32)
    m_hi = jnp.zeros((x1_ref.shape[0], HALF), F32)
    for k in range(TOP_K):
        g_lo, g_hi = _unpack_segments([g_ref[k, j] for j in range(N_SEG)])
        gate = rg_ref[:, k:k + 1]
        m_lo = m_lo + gate * g_lo
        m_hi = m_hi + gate * g_hi
    z = DN_ALPHA * x1_ref[...] + jnp.concatenate([m_lo, m_hi], axis=1)
    y = _layer_norm(z, ln2g_ref[...], ln2b_ref[...])

    if with_sample:
        ys_ref = rest[n_alias + 1]

        @pl.when(i < n_prompt_blocks)
        def _p():
            yp_ref[...] = y

        @pl.when(i >= n_prompt_blocks)
        def _s():
            ys_ref[...] = y
    else:
        yp_ref[...] = y


def _combine(x1, g, rg, ln2_g, ln2_b, *, tok0, n_prompt, tf, yp_buf=None):
    n_tok = x1.shape[0]
    piece = g.shape[2]
    n_sample = n_tok - n_prompt
    assert n_prompt % tf == 0 and tok0 % tf == 0 and piece % tf == 0
    npb = n_prompt // tf
    blk0 = tok0 // tf
    with_sample = tok0 + piece > n_prompt
    assert not with_sample or (tok0 + piece == n_tok and n_sample % tf == 0)
    alias = () if yp_buf is None else (yp_buf,)
    out_specs = [pl.BlockSpec((tf, D_MODEL), lambda i: (jnp.minimum(blk0 + i, npb - 1), 0))]
    out_shape = [jax.ShapeDtypeStruct((n_prompt, D_MODEL), F32)]
    if with_sample:
        out_specs.append(pl.BlockSpec((tf, D_MODEL), lambda i: (jnp.maximum(blk0 + i - npb, 0), 0)))
        out_shape.append(jax.ShapeDtypeStruct((n_sample, D_MODEL), F32))
    n_in = 5
    return pl.pallas_call(
        functools.partial(_combine_kernel, blk0=blk0, n_prompt_blocks=npb, n_alias=len(alias),
                          with_sample=with_sample),
        grid=(piece // tf,),
        input_output_aliases={n_in + i: i for i in range(len(alias))},
        in_specs=[pl.BlockSpec((tf, D_MODEL), lambda i: (blk0 + i, 0)),
                  pl.BlockSpec((TOP_K, N_SEG, tf, SC_SEG), lambda i: (0, 0, i, 0)),
                  pl.BlockSpec((tf, LANES), lambda i: (blk0 + i, 0)),
                  pl.BlockSpec((1, D_MODEL), lambda i: (0, 0)),
                  pl.BlockSpec((1, D_MODEL), lambda i: (0, 0))]
        + [pl.BlockSpec(memory_space=pl.ANY)] * len(alias),
        out_specs=out_specs,
        out_shape=out_shape,
        compiler_params=pltpu.CompilerParams(
            dimension_semantics=("arbitrary",), vmem_limit_bytes=VMEM_LIMIT),
        name="combine",
    )(x1, g, rg, ln2_g, ln2_b, *alias)


def _sc_mesh():
    return plsc.VectorSubcoreMesh(core_axis_name="c", subcore_axis_name="s")


def _sc_scatter_rows(src, dest, n_out):
    n_src, width = src.shape
    n_k = dest.shape[0]
    assert width == SC_SEG and n_src % SC_WINDOW == 0 and dest.shape[1] == n_src

    @pl.kernel(out_type=jax.ShapeDtypeStruct((n_out, width), src.dtype), mesh=_sc_mesh(),
               scratch_types=[])
    def scatter_kernel(src_hbm, idx_hbm, out_hbm):
        def body(src_vmem, idx_vmem):
            for k in range(n_k):
                pltpu.sync_copy(src_vmem, out_hbm.at[idx_vmem.at[k]])

        pltpu.emit_pipeline(
            body, grid=(n_src // SC_WINDOW,),
            in_specs=[pl.BlockSpec((SC_WINDOW, width), lambda i: (i, 0)),
                      pl.BlockSpec((n_k, SC_WINDOW), lambda i: (0, i))],
            out_specs=[],
            core_axis_name=("c", "s"),
            dimension_semantics=(pltpu.PARALLEL,),
        )(src_hbm, idx_hbm)

    return scatter_kernel(src, dest)


def _sc_gather_rows(src, idx):
    n_out = idx.shape[1]
    width = src.shape[1]
    assert width == SC_SEG and n_out % SC_WINDOW == 0

    @pl.kernel(out_type=jax.ShapeDtypeStruct((n_out, width), src.dtype), mesh=_sc_mesh(),
               scratch_types=[])
    def gather_kernel(src_hbm, idx_hbm, out_hbm):
        def body(idx_vmem, out_vmem):
            pltpu.sync_copy(src_hbm.at[idx_vmem.at[0]], out_vmem)

        pltpu.emit_pipeline(
            body, grid=(n_out // SC_WINDOW,),
            in_specs=[pl.BlockSpec((1, SC_WINDOW), lambda i: (0, i))],
            out_specs=[pl.BlockSpec((SC_WINDOW, width), lambda i: (i, 0))],
            core_axis_name=("c", "s"),
            dimension_semantics=(pltpu.PARALLEL,),
        )(idx_hbm, out_hbm)

    return gather_kernel(src, idx)


def _prep_params(w_in, w_dw, b_dw, ln_conv_g, ln_conv_b, w_gate_lr, b_gate, gla_norm_g, w_o,
                 ln1_g, ln1_b, w_router, b_router):
    row = lambda v: v.reshape(1, -1).astype(F32)
    w_in_p = jnp.pad(w_in, ((0, 0), (0, IN_COLS_PAD - w_in.shape[1]))).astype(BF16)
    w_gate = jnp.pad(w_gate_lr, ((0, GATE_PAD - GATE_RANK), (0, 0))).astype(BF16)
    wr_hi = w_router.astype(BF16)
    wr_lo = (w_router - wr_hi.astype(F32)).astype(BF16)
    padr = lambda m: jnp.pad(m, ((0, 0), (0, LANES - N_EXPERTS)))
    return {
        "w_in": w_in_p, "w_dw": jnp.repeat(w_dw, 8, axis=0), "b_dw": row(b_dw), "ln_conv_g": row(ln_conv_g),
        "ln_conv_b": row(ln_conv_b), "w_gate": w_gate, "b_gate": row(b_gate),
        "gla_norm_g": row(gla_norm_g), "w_o": w_o.astype(BF16), "ln1_g": row(ln1_g),
        "ln1_b": row(ln1_b), "w_router": jnp.concatenate([padr(wr_hi), padr(wr_lo)], axis=1),
        "b_router": jnp.pad(row(b_router), ((0, 0), (0, LANES - N_EXPERTS))),
    }


def _dest_kernel(ps_ref, ri_ref, out_ref, *, n_slot):
    idx = ri_ref[0:TOP_K, :]
    dest = ri_ref[TOP_K:2 * TOP_K, :] + pl.program_id(0) * n_slot
    for e in range(N_EXPERTS):
        dest = dest + jnp.where(idx == e, ps_ref[e], 0)
    out_ref[...] = dest


def _moe_layout(ri, counts, *, tr):
    n_tok = ri.shape[1]
    cnt = counts[:, 0].astype(jnp.int32)
    padded = (cnt + EXPERT_BLOCK - 1) // EXPERT_BLOCK * EXPERT_BLOCK
    pad_end = jnp.cumsum(padded)
    pad_start = pad_end - padded
    nb = -(-(n_tok * TOP_K) // EXPERT_BLOCK) + N_EXPERTS
    n_t = n_tok // tr
    dest_seg = pl.pallas_call(
        functools.partial(_dest_kernel, n_slot=nb * EXPERT_BLOCK),
        grid_spec=pltpu.PrefetchScalarGridSpec(
            num_scalar_prefetch=1,
            grid=(N_SEG, n_t),
            in_specs=[pl.BlockSpec((2 * TOP_K, tr), lambda j, i, ps: (0, i))],
            out_specs=pl.BlockSpec((TOP_K, tr), lambda j, i, ps: (0, j * n_t + i)),
        ),
        out_shape=jax.ShapeDtypeStruct((TOP_K, N_SEG * n_tok), jnp.int32),
        name="slot_index",
    )(pad_start.astype(jnp.int32), ri)
    block_start = (jnp.concatenate([pad_start, pad_end[-1:]]) // EXPERT_BLOCK).astype(jnp.int32)
    n_used = block_start[-1:]
    return dest_seg, block_start, n_used, nb


def kernel(x_prompt, x_sample, state_conv, state_gla, w_in, w_dw, b_dw, ln_conv_g, ln_conv_b,
           w_gate_lr, b_gate, gla_norm_g, w_o, ln1_g, ln1_b, w_router, b_router, w_gu, b_gu,
           w_down, b_down, ln2_g, ln2_b):
    assert w_in.shape[0] == DEPTH
    l = 0
    p = _prep_params(w_in[l], w_dw[l], b_dw[l], ln_conv_g[l], ln_conv_b[l], w_gate_lr[l],
                     b_gate[l], gla_norm_g[l], w_o[l], ln1_g[l], ln1_b[l], w_router[l],
                     b_router[l])
    bp, tp, _ = x_prompt.shape
    bs, ts, _ = x_sample.shape
    zc = jnp.zeros((bp, CONV_W - 1, CONV_CH), F32)
    zs = jnp.zeros((bp, GLA_HEADS, GLA_DK, GLA_DV), F32)
    n_prompt = bp * tp
    n_tok = n_prompt + bs * ts
    x1p, x1, lg, conv_p, gla_p = _mixer(x_prompt, zc, zs, p, tm=PROMPT_TILE, n_tok_total=n_tok,
                                        tok_offset=0)
    x1p, x1, lg, conv_s, gla_s = _mixer(x_sample, state_conv[l], state_gla[l], p, tm=ts,
                                        n_tok_total=n_tok, tok_offset=n_prompt,
                                        token_bufs=(x1p, x1, lg),
                                        sb=SAMPLE_STREAMS if bs % SAMPLE_STREAMS == 0 else 1)

    tr = ROUTE_TILE if n_tok % ROUTE_TILE == 0 else bs * ts
    ri, rg, counts = _route(lg, tr=tr)
    dest_seg, block_start, n_used, nb = _moe_layout(ri, counts, tr=n_tok)
    n_slot = nb * EXPERT_BLOCK
    xs = _sc_scatter_rows(x1p.reshape(N_SEG * n_tok, SC_SEG), dest_seg, N_SEG * n_slot)
    yb = _ffn(block_start, n_used, xs.reshape(N_SEG, n_slot, SC_SEG), w_gu[l], b_gu[l], w_down[l],
              b_down[l])
    tf_last = bs * ts
    n_blk = n_prompt // tf_last
    last_blk = min(LAST_PIECE_BLOCKS, n_blk // 2)
    body_tok = (n_blk - last_blk) * tf_last
    tf_body = 2 * tf_last if body_tok % (2 * tf_last) == 0 else tf_last
    body_blk = body_tok // tf_body
    n_body = min(COMBINE_PIECES - 1, body_blk)
    bounds = [(p * body_blk // n_body) * tf_body for p in range(n_body)] + [body_tok, n_tok]
    dest_kjt = dest_seg.reshape(TOP_K, N_SEG, n_tok)
    yp = None
    for t0, t1 in zip(bounds[:-1], bounds[1:]):
        g = _sc_gather_rows(yb.reshape(N_SEG * n_slot, SC_SEG),
                            dest_kjt[:, :, t0:t1].reshape(1, -1))
        outs = _combine(x1, g.reshape(TOP_K, N_SEG, t1 - t0, SC_SEG), rg,
                        ln2_g[l].reshape(1, -1), ln2_b[l].reshape(1, -1), tok0=t0,
                        n_prompt=n_prompt, tf=tf_last if t1 == n_tok else tf_body, yp_buf=yp)
        yp = outs[0]
    ys = outs[1]
    return (yp.reshape(bp, tp, D_MODEL), ys.reshape(bs, ts, D_MODEL),
            conv_p[None], gla_p[None], conv_s[None], gla_s[None])
```

```python
import functools

import jax
import jax.numpy as jnp
from jax import lax
from jax.experimental import pallas as pl
from jax.experimental.pallas import tpu as pltpu
from jax.experimental.pallas import tpu_sc as plsc

F32 = jnp.float32
BF16 = jnp.bfloat16

LANES = 128
D_MODEL = 1024
CHUNK = 64
SUB = 16
DIAG = 8
CONV_CH = 512
CONV_W = 31
HALO = 32
PROMPT_TILE = 1024
MAX_STATIC_TRIPS = 4
GLU_ROWS = 128
CONV_BLOCK = 64
ROUTE_TILE = 768
SAMPLE_STREAMS = 16
LAST_PIECE_BLOCKS = 4
COMBINE_PIECES = 8
TRIP_BLOCKS = 4
IN_AHEAD = TRIP_BLOCKS
IN_RING = 2 * TRIP_BLOCKS
OUT_RING = 2 * TRIP_BLOCKS
GLA_HEADS = 4
GLA_DK = 64
GLA_DV = 128
GLA_K = GLA_HEADS * GLA_DK
GLA_V = GLA_HEADS * GLA_DV
GATE_RANK = 16
GATE_PAD = 128
GATE_NORM = 16.0
MIX_W = CONV_CH + GLA_V
OFF_UV = 0
OFF_UG = OFF_UV + CONV_CH
OFF_Q = OFF_UG + CONV_CH
OFF_K = OFF_Q + GLA_K
OFF_V = OFF_K + GLA_K
OFF_G = OFF_V + GLA_V
OFF_A = OFF_G + GLA_V
IN_COLS_PAD = OFF_A + GATE_PAD
N_EXPERTS = 32
TOP_K = 4
D_FF = 1024
SWIGLU_LIMIT = 7.0
SWIGLU_ALPHA = 1.702
EXPERT_BLOCK = 256
LN_EPS = 1e-5
DEPTH = 1
DN_ALPHA = (2 * DEPTH) ** 0.25
VMEM_LIMIT = 56 * 1024 * 1024
HALF = D_MODEL // 2
SC_SEG = 256
SC_WINDOW = 128
N_SEG = HALF // SC_SEG
U32 = jnp.uint32


def _dot(a, b):
    return jnp.dot(a, b, preferred_element_type=F32)


def _dot_nt(a, b):
    return lax.dot_general(a, b, (((1,), (1,)), ((), ())), preferred_element_type=F32)


def _dot_tn(a, b):
    return lax.dot_general(a, b, (((0,), (0,)), ((), ())), preferred_element_type=F32)


def _layer_norm(x, g, b):
    mu = jnp.mean(x, axis=-1, keepdims=True)
    xc = x - mu
    var = jnp.mean(xc * xc, axis=-1, keepdims=True)
    return xc * lax.rsqrt(var + LN_EPS) * g + b


def _sigmoid(x):
    return 1.0 / (1.0 + jnp.exp(-x))


def _split_bf16(x):
    hi = x.astype(BF16)
    lo = (x - hi.astype(F32)).astype(BF16)
    return hi, lo


def _pack_segments(x):
    bits = pltpu.bitcast(x.astype(BF16).astype(F32), U32)
    word = (bits[:, :HALF] >> 16) | bits[:, HALF:]
    return [word[:, j * SC_SEG:(j + 1) * SC_SEG] for j in range(N_SEG)]


def _unpack_segments(segs):
    word = jnp.concatenate(segs, axis=1)
    lo = pltpu.bitcast(word << 16, F32)
    hi = pltpu.bitcast(word & jnp.uint32(0xFFFF0000), F32)
    return lo, hi


def _mixer_kernel(x_ref, cs_ref, gs_ref, w_in_ref, w_dw_ref, b_dw_ref, lncg_ref, lncb_ref,
                  wg_ref, bg_ref, gng_ref, w_o_ref, ln1g_ref, ln1b_ref, wr_ref, br_ref, *rest,
                  tm, chunk, n_alias, sb):
    (x1p_ref, x1_ref, logit_ref, conv_out_ref, gla_out_ref,
     ubuf, qbuf, kbuf, labuf, vbuf, gbuf, mixbuf, st_ref) = rest[n_alias:]
    t = pl.program_id(1)
    nt = pl.num_programs(1)
    n_chunks = tm // chunk
    n_sub = chunk // SUB
    n_rows = sb * tm
    us = HALO + tm

    @pl.when(t == 0)
    def _init():
        for s in range(sb):
            ubuf[s * us:s * us + HALO - (CONV_W - 1), :] = jnp.zeros(
                (HALO - (CONV_W - 1), CONV_CH), F32)
            ubuf[s * us + HALO - (CONV_W - 1):s * us + HALO, :] = cs_ref[s]
            for h in range(GLA_HEADS):
                st_ref[s * GLA_DV:(s + 1) * GLA_DV, h * GLA_DK:(h + 1) * GLA_DK] = gs_ref[s, h].T

    x = x_ref[...].reshape(n_rows, D_MODEL)
    xb = x.astype(BF16)

    gp = min(GLU_ROWS, tm) if sb == 1 else n_rows
    for r0 in range(0, n_rows, gp):
        hv = _dot(xb[r0:r0 + gp], w_in_ref[:, OFF_UV:OFF_UG])
        hg = _dot(xb[r0:r0 + gp], w_in_ref[:, OFF_UG:OFF_Q])
        u = hv * _sigmoid(hg)
        for s in range(sb):
            lo = max(r0, s * tm)
            hi = min(r0 + gp, (s + 1) * tm)
            if lo < hi:
                ubuf[s * us + HALO + lo - s * tm:s * us + HALO + hi - s * tm, :] = u[lo - r0:hi - r0]
    lead = HALO - (CONV_W - 1)

    def proj_q(r0, n):
        qbuf[r0:r0 + n, :] = _dot(xb[r0:r0 + n], w_in_ref[:, OFF_Q:OFF_K]) * (GLA_DK ** -0.5)

    def proj_k(r0, n):
        kbuf[r0:r0 + n, :] = _dot(xb[r0:r0 + n], w_in_ref[:, OFF_K:OFF_V])

    def proj_v(half, r0, n):
        lo = half * (GLA_V // 2)
        vbuf[r0:r0 + n, lo:lo + GLA_V // 2] = _dot(
            xb[r0:r0 + n], w_in_ref[:, OFF_V + lo:OFF_V + lo + GLA_V // 2])

    def proj_g(half, r0, n):
        lo = half * (GLA_V // 2)
        gbuf[r0:r0 + n, lo:lo + GLA_V // 2] = _dot(
            xb[r0:r0 + n], w_in_ref[:, OFF_G + lo:OFF_G + lo + GLA_V // 2])

    def proj_gate(r0, n):
        ha = _dot(xb[r0:r0 + n], w_in_ref[:, OFF_A:IN_COLS_PAD])
        a = _dot(ha.astype(BF16), wg_ref[...]) + bg_ref[...]
        labuf[r0:r0 + n, :] = ((jnp.minimum(a, 0.0) - jnp.log(1.0 + jnp.exp(-jnp.abs(a))))
                               * (1.0 / GATE_NORM))

    proj_jobs = [proj_q, proj_k, functools.partial(proj_v, 0), functools.partial(proj_v, 1),
                 functools.partial(proj_g, 0), functools.partial(proj_g, 1), proj_gate]

    cb = min(CONV_BLOCK, tm)

    def conv_block(stream, r0):
        acc = jnp.zeros((cb, CONV_CH), F32) + b_dw_ref[...]
        for r in range(8):
            rows = cb if r == 0 else cb + 8
            part = None
            for s in range(r, lead + CONV_W, 8):
                if s < lead:
                    continue
                w8 = w_dw_ref[8 * (s - lead):8 * (s - lead) + 8, :]
                u3 = ubuf[pl.ds(stream * us + r0 + (s - r), rows), :].reshape(
                    rows // 8, 8, CONV_CH)
                term = u3 * w8
                part = term if part is None else part + term
            acc = acc + part.reshape(rows, CONV_CH)[r:r + cb, :]
        cact = _layer_norm(acc, lncg_ref[...], lncb_ref[...])
        cact = cact * _sigmoid(cact)
        mixbuf[pl.ds(stream * tm + r0, cb), 0:CONV_CH] = cact.astype(BF16)

    per_trip = 4 if n_chunks % 4 == 0 else 1
    n_trips = n_chunks // per_trip
    static_trips = 2 <= n_trips <= MAX_STATIC_TRIPS and sb == 1
    trip_rows = per_trip * chunk
    loop_blocks = (tm // 2) // (cb * n_trips) if n_chunks >= 4 else 0
    pre_blocks = tm // cb - loop_blocks * n_trips
    rows_first = trip_rows if static_trips else n_rows
    pre = [(s, blk * cb) for s in range(sb) for blk in range(pre_blocks)]
    for i in range(max(len(pre), len(proj_jobs))):
        if i < len(proj_jobs):
            proj_jobs[i](0, rows_first)
        if i < len(pre):
            conv_block(*pre[i])

    lane_k = lax.broadcasted_iota(jnp.int32, (1, GLA_K), 1) // GLA_DK
    row_c = lax.broadcasted_iota(jnp.int32, (chunk, 1), 0)
    tri = (lax.broadcasted_iota(jnp.int32, (chunk, chunk), 0)
           >= lax.broadcasted_iota(jnp.int32, (chunk, chunk), 1)).astype(BF16)
    e2 = (lax.broadcasted_iota(jnp.int32, (GLA_K, GLA_V), 0) // GLA_DK
          == lax.broadcasted_iota(jnp.int32, (GLA_K, GLA_V), 1) // GLA_DV).astype(BF16)
    row_s = lax.broadcasted_iota(jnp.int32, (GLA_HEADS * chunk, 1), 0)
    blk_s = (row_s % chunk) // SUB
    same_blk = blk_s == lax.broadcasted_iota(jnp.int32, (1, LANES), 1) // SUB
    second_half = row_c % SUB >= DIAG
    row_g = lax.broadcasted_iota(jnp.int32, (1, DIAG, 1), 1)
    lane_v = lax.broadcasted_iota(jnp.int32, (1, GLA_V), 1) % GLA_DV

    def head_stack(m):
        return jnp.concatenate(
            [jnp.where(lane_k == h, m, 0.0) for h in range(GLA_HEADS)], axis=0).astype(BF16)

    def unstack(r, width):
        return jnp.concatenate(
            [r[h * chunk:(h + 1) * chunk, h * width:(h + 1) * width] for h in range(GLA_HEADS)],
            axis=1)

    def chunk_step(stream, c, carry):
        base = stream * tm + (c * chunk if isinstance(c, int) else pl.multiple_of(c * chunk, chunk))
        st_rows = slice(stream * GLA_DV, (stream + 1) * GLA_DV)
        q_c = qbuf[pl.ds(base, chunk), :]
        k_c = kbuf[pl.ds(base, chunk), :]
        v_c = vbuf[pl.ds(base, chunk), :]
        v_b = v_c.astype(BF16)
        la_c = labuf[pl.ds(base, chunk), :]
        la_hi, la_lo = _split_bf16(la_c)
        b = _dot(tri, la_hi) + _dot(tri, la_lo)
        b_last = b[chunk - 1:chunk, :]
        st = st_ref[st_rows, :]

        qe = q_c * jnp.exp(b)
        o2 = _dot_nt(head_stack(qe), st.astype(BF16))
        o = jnp.concatenate([o2[h * chunk:(h + 1) * chunk, :] for h in range(GLA_HEADS)], axis=1)

        a_off = None
        if n_sub > 1:
            r_rows = [b[0:SUB, :]] + [jnp.broadcast_to(b[SUB * i - 1:SUB * i, :], (SUB, GLA_K))
                                      for i in range(1, n_sub)]
            r_q = jnp.concatenate(r_rows, axis=0)
            q_t = jnp.where(row_c >= SUB, q_c * jnp.exp(jnp.minimum(b - r_q, 0.0)), 0.0)
            k_parts = []
            for i in range(1, n_sub):
                r_i = b[SUB * i - 1:SUB * i, :]
                k_t = jnp.where(row_c < SUB * i, k_c * jnp.exp(jnp.minimum(r_i - b, 0.0)), 0.0)
                k_parts.append(k_t)
                k_parts.append(jnp.zeros((LANES - chunk, GLA_K), F32))
            k_cat = jnp.concatenate(k_parts, axis=0).astype(BF16)
            r = _dot_nt(head_stack(q_t), k_cat)
            a_off = r[:, 0:LANES]
            for i in range(2, n_sub):
                a_off = jnp.where(blk_s == i, r[:, (i - 1) * LANES:i * LANES], a_off)

        r_b = jnp.concatenate(
            [jnp.broadcast_to(b[SUB * i + DIAG - 1:SUB * i + DIAG, :], (SUB, GLA_K))
             for i in range(n_sub)], axis=0)
        q_b = jnp.where(second_half, q_c * jnp.exp(jnp.minimum(b - r_b, 0.0)), 0.0)
        k_b = jnp.where(second_half, 0.0, k_c * jnp.exp(jnp.minimum(r_b - b, 0.0)))
        k_b = jnp.concatenate([k_b, jnp.zeros((LANES - chunk, GLA_K), F32)], axis=0).astype(BF16)
        a_half = jnp.where(same_blk, _dot_nt(head_stack(q_b), k_b), 0.0)
        a_off = a_half if a_off is None else a_off + a_half

        q3 = q_c.reshape(chunk // DIAG, DIAG, GLA_K)
        k3 = k_c.reshape(chunk // DIAG, DIAG, GLA_K)
        b3 = b.reshape(chunk // DIAG, DIAG, GLA_K)
        ps = []
        for d in range(DIAG):
            k_s = k3 if d == 0 else pltpu.roll(k3, d, axis=1)
            b_s = b3 if d == 0 else pltpu.roll(b3, d, axis=1)
            p = jnp.where(row_g >= d, q3 * k_s * jnp.exp(jnp.minimum(b3 - b_s, 0.0)), 0.0)
            ps.append(p.reshape(chunk, GLA_K).astype(BF16))
        w_all = _dot(jnp.concatenate(ps, axis=0), e2)
        a_d = jnp.zeros((chunk, GLA_V), F32)
        for d in range(DIAG):
            a_d = jnp.where(lane_v == row_c - d, w_all[d * chunk:(d + 1) * chunk, :], a_d)
        a_off = a_off + jnp.concatenate(
            [a_d[:, h * GLA_DV:(h + 1) * GLA_DV] for h in range(GLA_HEADS)], axis=0)
        o1 = _dot(a_off[:, 0:chunk].astype(BF16), v_b)
        o = o + unstack(o1, GLA_DV)

        ke = k_c * jnp.exp(b_last - b)
        r2 = _dot_tn(v_b, ke.astype(BF16))
        upd = jnp.zeros((GLA_DV, GLA_K), F32)
        for h in range(GLA_HEADS):
            upd = upd + jnp.where(lane_k == h, r2[h * GLA_DV:(h + 1) * GLA_DV, :], 0.0)
        st_ref[st_rows, :] = st * jnp.exp(b_last) + upd

        g_c = gbuf[pl.ds(base, chunk), :]
        outs = []
        for h in range(GLA_HEADS):
            oh = o[:, h * GLA_DV:(h + 1) * GLA_DV]
            ms = jnp.mean(oh * oh, axis=-1, keepdims=True)
            outs.append(oh * lax.rsqrt(ms + LN_EPS))
        on = jnp.concatenate(outs, axis=1) * gng_ref[...]
        on = on * (g_c * _sigmoid(g_c))
        mixbuf[pl.ds(base, chunk), CONV_CH:MIX_W] = on.astype(BF16)
        return carry

    def trip(stream, i, carry, extra_jobs=()):
        for j in range(max(per_trip, loop_blocks, len(extra_jobs))):
            if j < len(extra_jobs):
                extra_jobs[j]((i + 1) * trip_rows, trip_rows)
            if j < loop_blocks:
                r0 = (pre_blocks + i * loop_blocks + j) * cb
                conv_block(stream, r0 if isinstance(r0, int) else pl.multiple_of(r0, cb))
            if j < per_trip:
                carry = chunk_step(stream, i * per_trip + j, carry)
        return carry

    def merge_rows(r0, n):
        y = _dot(mixbuf[r0:r0 + n, :], w_o_ref[...])
        x1 = _layer_norm(DN_ALPHA * x[r0:r0 + n, :] + y, ln1g_ref[...], ln1b_ref[...])
        x1_ref[r0:r0 + n, :] = x1
        for j, seg in enumerate(_pack_segments(x1)):
            x1p_ref[j, r0:r0 + n, :] = seg
        x_hi, x_lo = _split_bf16(x1)
        lg = _dot(x_hi, wr_ref[...])
        logit_ref[r0:r0 + n, :] = (lg[:, 0:LANES] + lg[:, LANES:2 * LANES]
                                   + _dot(x_lo, wr_ref[:, 0:LANES]) + br_ref[...])

    if static_trips:
        for i in range(n_trips):
            trip(0, i, 0, tuple(proj_jobs) if i + 1 < n_trips else ())
            merge_rows(i * trip_rows, trip_rows)
    else:
        for s in range(sb):
            if n_trips == 1:
                trip(s, 0, 0)
            else:
                lax.fori_loop(0, n_trips, functools.partial(trip, s), 0)
        merge_rows(0, n_rows)

    tails = []
    for s in range(sb):
        tail = ubuf[s * us + tm:s * us + tm + HALO, :]
        ubuf[s * us:s * us + HALO, :] = tail
        tails.append(tail)

    @pl.when(t == nt - 1)
    def _state_out():
        for s in range(sb):
            conv_out_ref[s] = tails[s][HALO - (CONV_W - 1):, :]
            st = st_ref[s * GLA_DV:(s + 1) * GLA_DV, :]
            for h in range(GLA_HEADS):
                gla_out_ref[s, h] = st[:, h * GLA_DK:(h + 1) * GLA_DK].T


def _mixer(x, conv_state, gla_state, p, *, tm, n_tok_total, tok_offset, token_bufs=None, sb=1):
    bsz, seq, _ = x.shape
    chunk = min(CHUNK, seq)
    rows = sb * tm
    assert seq % tm == 0 and tm % chunk == 0 and chunk % SUB == 0 and tok_offset % rows == 0
    assert bsz % sb == 0 and (sb == 1 or seq == tm)
    nt = seq // tm
    blk0 = tok_offset // rows
    full = lambda shape: pl.BlockSpec(shape, lambda b, t: (0,) * len(shape))
    alias = () if token_bufs is None else tuple(token_bufs)
    kern = functools.partial(_mixer_kernel, tm=tm, chunk=chunk, n_alias=len(alias), sb=sb)
    n_in = 16
    return pl.pallas_call(
        kern,
        grid=(bsz // sb, nt),
        input_output_aliases={n_in + i: i for i in range(len(alias))},
        in_specs=[
            pl.BlockSpec((sb, tm, D_MODEL), lambda b, t: (b, t, 0)),
            pl.BlockSpec((sb, CONV_W - 1, CONV_CH), lambda b, t: (b, 0, 0)),
            pl.BlockSpec((sb, GLA_HEADS, GLA_DK, GLA_DV), lambda b, t: (b, 0, 0, 0)),
            full((D_MODEL, IN_COLS_PAD)),
            full((8 * CONV_W, CONV_CH)),
            full((1, CONV_CH)),
            full((1, CONV_CH)),
            full((1, CONV_CH)),
            full((GATE_PAD, GLA_K)),
            full((1, GLA_K)),
            full((1, GLA_V)),
            full((MIX_W, D_MODEL)),
            full((1, D_MODEL)),
            full((1, D_MODEL)),
            full((D_MODEL, 2 * LANES)),
            full((1, LANES)),
        ] + [pl.BlockSpec(memory_space=pl.ANY)] * len(alias),
        out_specs=[
            pl.BlockSpec((N_SEG, rows, SC_SEG), lambda b, t: (0, blk0 + b * nt + t, 0)),
            pl.BlockSpec((rows, D_MODEL), lambda b, t: (blk0 + b * nt + t, 0)),
            pl.BlockSpec((rows, LANES), lambda b, t: (blk0 + b * nt + t, 0)),
            pl.BlockSpec((sb, CONV_W - 1, CONV_CH), lambda b, t: (b, 0, 0)),
            pl.BlockSpec((sb, GLA_HEADS, GLA_DK, GLA_DV), lambda b, t: (b, 0, 0, 0)),
        ],
        out_shape=[
            jax.ShapeDtypeStruct((N_SEG, n_tok_total, SC_SEG), U32),
            jax.ShapeDtypeStruct((n_tok_total, D_MODEL), F32),
            jax.ShapeDtypeStruct((n_tok_total, LANES), F32),
            jax.ShapeDtypeStruct((bsz, CONV_W - 1, CONV_CH), F32),
            jax.ShapeDtypeStruct((bsz, GLA_HEADS, GLA_DK, GLA_DV), F32),
        ],
        scratch_shapes=[
            pltpu.VMEM((sb * (HALO + tm), CONV_CH), F32),
            pltpu.VMEM((rows, GLA_K), F32),
            pltpu.VMEM((rows, GLA_K), F32),
            pltpu.VMEM((rows, GLA_K), F32),
            pltpu.VMEM((rows, GLA_V), F32),
            pltpu.VMEM((rows, GLA_V), F32),
            pltpu.VMEM((rows, MIX_W), BF16),
            pltpu.VMEM((sb * GLA_DV, GLA_K), F32),
        ],
        compiler_params=pltpu.CompilerParams(
            dimension_semantics=("arbitrary", "arbitrary"),
            vmem_limit_bytes=VMEM_LIMIT),
        name="mixer",
    )(x, conv_state, gla_state, p["w_in"], p["w_dw"], p["b_dw"], p["ln_conv_g"], p["ln_conv_b"],
      p["w_gate"], p["b_gate"], p["gla_norm_g"], p["w_o"], p["ln1_g"], p["ln1_b"],
      p["w_router"], p["b_router"], *alias)


def _route_kernel(lg_ref, ri_ref, rg_ref, cnt_ref, carry_ref, *, tr):
    i = pl.program_id(0)

    @pl.when(i == 0)
    def _init():
        carry_ref[...] = jnp.zeros((N_EXPERTS, LANES), F32)

    l = lg_ref[...].T[0:N_EXPERTS, :]
    row = lax.broadcasted_iota(jnp.int32, (N_EXPERTS, tr), 0)
    hots, vals, idxs = [], [], []
    for _ in range(TOP_K):
        m = jnp.max(l, axis=0, keepdims=True)
        ik = jnp.min(jnp.where(l == m, row, N_EXPERTS), axis=0, keepdims=True)
        hot = row == ik
        hots.append(hot)
        vals.append(m)
        idxs.append(ik)
        l = jnp.where(hot, -jnp.inf, l)
    es = [jnp.exp(v - vals[0]) for v in vals]
    denom = es[0] + es[1] + es[2] + es[3]
    member = jnp.zeros((N_EXPERTS, tr), F32)
    for hot in hots:
        member = member + hot.astype(F32)
    before = (lax.broadcasted_iota(jnp.int32, (tr, tr), 0)
              < lax.broadcasted_iota(jnp.int32, (tr, tr), 1)).astype(BF16)
    cum = _dot(member.astype(BF16), before) + carry_ref[:, 0:1]
    ranks = [jnp.sum(jnp.where(hot, cum, 0.0), axis=0, keepdims=True).astype(jnp.int32)
             for hot in hots]
    ri_ref[...] = jnp.concatenate(idxs + ranks, axis=0)
    gates = jnp.concatenate([e / denom for e in es] + [jnp.zeros((LANES - TOP_K, tr), F32)], axis=0)
    rg_ref[...] = gates.T
    carry_ref[...] = carry_ref[...] + jnp.sum(member, axis=1, keepdims=True)

    @pl.when(i == pl.num_programs(0) - 1)
    def _fin():
        cnt_ref[...] = carry_ref[...]


def _route(logits, *, tr):
    n_tok = logits.shape[0]
    assert n_tok % tr == 0
    return pl.pallas_call(
        functools.partial(_route_kernel, tr=tr),
        grid=(n_tok // tr,),
        in_specs=[pl.BlockSpec((tr, LANES), lambda i: (i, 0))],
        out_specs=[pl.BlockSpec((2 * TOP_K, tr), lambda i: (0, i)),
                   pl.BlockSpec((tr, LANES), lambda i: (i, 0)),
                   pl.BlockSpec((N_EXPERTS, LANES), lambda i: (0, 0))],
        out_shape=[jax.ShapeDtypeStruct((2 * TOP_K, n_tok), jnp.int32),
                   jax.ShapeDtypeStruct((n_tok, LANES), F32),
                   jax.ShapeDtypeStruct((N_EXPERTS, LANES), F32)],
        scratch_shapes=[pltpu.VMEM((N_EXPERTS, LANES), F32)],
        compiler_params=pltpu.CompilerParams(dimension_semantics=("arbitrary",)),
        name="route",
    )(logits)


def _ffn_kernel(gs_ref, nu_ref, xs_hbm, wgu_ref, bgu_ref, wdn_ref, bdn_ref, yb_hbm,
                wgu_bf, wdn_bf, xin, yout, in_sem, out_sem):
    e = pl.program_id(0)
    n_used = nu_ref[0]
    g_first = gs_ref[e]
    g_end = gs_ref[e + 1]

    def in_copy(g, slot):
        rows = pl.ds(pl.multiple_of(g * EXPERT_BLOCK, EXPERT_BLOCK), EXPERT_BLOCK)
        return pltpu.make_async_copy(xs_hbm.at[:, rows, :], xin.at[slot], in_sem.at[slot])

    def out_copy(g, slot):
        rows = pl.ds(pl.multiple_of(g * EXPERT_BLOCK, EXPERT_BLOCK), EXPERT_BLOCK)
        return pltpu.make_async_copy(yout.at[slot], yb_hbm.at[:, rows, :], out_sem.at[slot])

    @pl.when(e == 0)
    def _prime():
        for g in range(IN_AHEAD):
            @pl.when(g < n_used)
            def _():
                in_copy(g, g).start()

    def cast_weights():
        wgu_bf[...] = wgu_ref[0].astype(BF16)
        wdn_bf[...] = wdn_ref[0].astype(BF16)

    def blocks(g0, n, with_cast=False):
        for j in range(n):
            g = g0 + j
            in_copy(g, g % IN_RING).wait()
        for j in range(n):
            g = g0 + IN_AHEAD + j

            @pl.when(g < n_used)
            def _prefetch():
                in_copy(g, g % IN_RING).start()
        for j in range(n):
            g = g0 + j

            @pl.when(g >= OUT_RING)
            def _free_out():
                out_copy(g - OUT_RING, g % OUT_RING).wait()
        if with_cast:
            cast_weights()
        for j in range(n):
            g = g0 + j
            x_lo, x_hi = _unpack_segments([xin[g % IN_RING, s] for s in range(N_SEG)])
            x_bf = jnp.concatenate([x_lo.astype(BF16), x_hi.astype(BF16)], axis=1)
            hgu = _dot(x_bf, wgu_bf[...]) + bgu_ref[0]
            gate = jnp.minimum(hgu[:, :D_FF], SWIGLU_LIMIT)
            up = jnp.clip(hgu[:, D_FF:], -SWIGLU_LIMIT, SWIGLU_LIMIT)
            act = (up + 1.0) * gate * _sigmoid(SWIGLU_ALPHA * gate)
            yb = _dot(act.astype(BF16), wdn_bf[...]) + bdn_ref[0]
            for s, seg in enumerate(_pack_segments(yb)):
                yout[g % OUT_RING, s] = seg
        for j in range(n):
            g = g0 + j
            out_copy(g, g % OUT_RING).start()

    n_blocks = g_end - g_first
    n_full = n_blocks // TRIP_BLOCKS

    @pl.when((n_full == 0) & (n_blocks > 0))
    def _cast_only():
        cast_weights()

    @pl.when(n_full >= 1)
    def _first_trip():
        blocks(g_first, TRIP_BLOCKS, with_cast=True)

    def full_trip(i, carry):
        blocks(g_first + TRIP_BLOCKS * i, TRIP_BLOCKS)
        return carry

    lax.fori_loop(1, n_full, full_trip, 0)
    done = n_full * TRIP_BLOCKS
    size = TRIP_BLOCKS // 2
    while size >= 1:
        @pl.when((n_blocks // size) % 2 == 1)
        def _rest(done=done, size=size):
            blocks(g_first + done, size)
        done = done + jnp.where((n_blocks // size) % 2 == 1, size, 0)
        size //= 2

    @pl.when(e == pl.num_programs(0) - 1)
    def _drain():
        for back in range(OUT_RING, 0, -1):
            @pl.when(n_used >= back)
            def _():
                out_copy(n_used - back, (n_used - back) % OUT_RING).wait()


def _ffn(block_start, n_used, xs, w_gu, b_gu, w_down, b_down):
    n_rows = xs.shape[1]
    grid_spec = pltpu.PrefetchScalarGridSpec(
        num_scalar_prefetch=2,
        grid=(N_EXPERTS,),
        in_specs=[
            pl.BlockSpec(memory_space=pl.ANY),
            pl.BlockSpec((1, D_MODEL, 2 * D_FF), lambda e, gs, nu: (e, 0, 0)),
            pl.BlockSpec((1, 1, 2 * D_FF), lambda e, gs, nu: (e, 0, 0)),
            pl.BlockSpec((1, D_FF, D_MODEL), lambda e, gs, nu: (e, 0, 0)),
            pl.BlockSpec((1, 1, D_MODEL), lambda e, gs, nu: (e, 0, 0)),
        ],
        out_specs=pl.BlockSpec(memory_space=pl.ANY),
        scratch_shapes=[pltpu.VMEM((D_MODEL, 2 * D_FF), BF16),
                        pltpu.VMEM((D_FF, D_MODEL), BF16),
                        pltpu.VMEM((IN_RING, N_SEG, EXPERT_BLOCK, SC_SEG), U32),
                        pltpu.VMEM((OUT_RING, N_SEG, EXPERT_BLOCK, SC_SEG), U32),
                        pltpu.SemaphoreType.DMA((IN_RING,)),
                        pltpu.SemaphoreType.DMA((OUT_RING,))],
    )
    return pl.pallas_call(
        _ffn_kernel,
        grid_spec=grid_spec,
        out_shape=jax.ShapeDtypeStruct((N_SEG, n_rows, SC_SEG), U32),
        compiler_params=pltpu.CompilerParams(
            dimension_semantics=("arbitrary",), vmem_limit_bytes=VMEM_LIMIT),
        name="expert_ffn",
    )(block_start, n_used, xs, w_gu, b_gu.reshape(N_EXPERTS, 1, 2 * D_FF), w_down,
      b_down.reshape(N_EXPERTS, 1, D_MODEL))


def _combine_kernel(x1_ref, g_ref, rg_ref, ln2g_ref, ln2b_ref, *rest, blk0, n_prompt_blocks,
                    n_alias, with_sample):
    yp_ref = rest[n_alias]
    i = blk0 + pl.program_id(0)
    m_lo = jnp.zeros((x1_ref.shape[0], HALF), F32)
    m_hi = jnp.zeros((x1_ref.shape[0], HALF), F32)
    for k in range(TOP_K):
        g_lo, g_hi = _unpack_segments([g_ref[k, j] for j in range(N_SEG)])
        gate = rg_ref[:, k:k + 1]
        m_lo = m_lo + gate * g_lo
        m_hi = m_hi + gate * g_hi
    z = DN_ALPHA * x1_ref[...] + jnp.concatenate([m_lo, m_hi], axis=1)
    y = _layer_norm(z, ln2g_ref[...], ln2b_ref[...])

    if with_sample:
        ys_ref = rest[n_alias + 1]

        @pl.when(i < n_prompt_blocks)
        def _p():
            yp_ref[...] = y

        @pl.when(i >= n_prompt_blocks)
        def _s():
            ys_ref[...] = y
    else:
        yp_ref[...] = y


def _combine(x1, g, rg, ln2_g, ln2_b, *, tok0, n_prompt, tf, yp_buf=None):
    n_tok = x1.shape[0]
    piece = g.shape[2]
    n_sample = n_tok - n_prompt
    assert n_prompt % tf == 0 and tok0 % tf == 0 and piece % tf == 0
    npb = n_prompt // tf
    blk0 = tok0 // tf
    with_sample = tok0 + piece > n_prompt
    assert not with_sample or (tok0 + piece == n_tok and n_sample % tf == 0)
    alias = () if yp_buf is None else (yp_buf,)
    out_specs = [pl.BlockSpec((tf, D_MODEL), lambda i: (jnp.minimum(blk0 + i, npb - 1), 0))]
    out_shape = [jax.ShapeDtypeStruct((n_prompt, D_MODEL), F32)]
    if with_sample:
        out_specs.append(pl.BlockSpec((tf, D_MODEL), lambda i: (jnp.maximum(blk0 + i - npb, 0), 0)))
        out_shape.append(jax.ShapeDtypeStruct((n_sample, D_MODEL), F32))
    n_in = 5
    return pl.pallas_call(
        functools.partial(_combine_kernel, blk0=blk0, n_prompt_blocks=npb, n_alias=len(alias),
                          with_sample=with_sample),
        grid=(piece // tf,),
        input_output_aliases={n_in + i: i for i in range(len(alias))},
        in_specs=[pl.BlockSpec((tf, D_MODEL), lambda i: (blk0 + i, 0)),
                  pl.BlockSpec((TOP_K, N_SEG, tf, SC_SEG), lambda i: (0, 0, i, 0)),
                  pl.BlockSpec((tf, LANES), lambda i: (blk0 + i, 0)),
                  pl.BlockSpec((1, D_MODEL), lambda i: (0, 0)),
                  pl.BlockSpec((1, D_MODEL), lambda i: (0, 0))]
        + [pl.BlockSpec(memory_space=pl.ANY)] * len(alias),
        out_specs=out_specs,
        out_shape=out_shape,
        compiler_params=pltpu.CompilerParams(
            dimension_semantics=("arbitrary",), vmem_limit_bytes=VMEM_LIMIT),
        name="combine",
    )(x1, g, rg, ln2_g, ln2_b, *alias)


def _sc_mesh():
    return plsc.VectorSubcoreMesh(core_axis_name="c", subcore_axis_name="s")


def _sc_scatter_rows(src, dest, n_out):
    n_src, width = src.shape
    n_k = dest.shape[0]
    assert width == SC_SEG and n_src % SC_WINDOW == 0 and dest.shape[1] == n_src

    @pl.kernel(out_type=jax.ShapeDtypeStruct((n_out, width), src.dtype), mesh=_sc_mesh(),
               scratch_types=[])
    def scatter_kernel(src_hbm, idx_hbm, out_hbm):
        def body(src_vmem, idx_vmem):
            for k in range(n_k):
                pltpu.sync_copy(src_vmem, out_hbm.at[idx_vmem.at[k]])

        pltpu.emit_pipeline(
            body, grid=(n_src // SC_WINDOW,),
            in_specs=[pl.BlockSpec((SC_WINDOW, width), lambda i: (i, 0)),
                      pl.BlockSpec((n_k, SC_WINDOW), lambda i: (0, i))],
            out_specs=[],
            core_axis_name=("c", "s"),
            dimension_semantics=(pltpu.PARALLEL,),
        )(src_hbm, idx_hbm)

    return scatter_kernel(src, dest)


def _sc_gather_rows(src, idx):
    n_out = idx.shape[1]
    width = src.shape[1]
    assert width == SC_SEG and n_out % SC_WINDOW == 0

    @pl.kernel(out_type=jax.ShapeDtypeStruct((n_out, width), src.dtype), mesh=_sc_mesh(),
               scratch_types=[])
    def gather_kernel(src_hbm, idx_hbm, out_hbm):
        def body(idx_vmem, out_vmem):
            pltpu.sync_copy(src_hbm.at[idx_vmem.at[0]], out_vmem)

        pltpu.emit_pipeline(
            body, grid=(n_out // SC_WINDOW,),
            in_specs=[pl.BlockSpec((1, SC_WINDOW), lambda i: (0, i))],
            out_specs=[pl.BlockSpec((SC_WINDOW, width), lambda i: (i, 0))],
            core_axis_name=("c", "s"),
            dimension_semantics=(pltpu.PARALLEL,),
        )(idx_hbm, out_hbm)

    return gather_kernel(src, idx)


def _prep_params(w_in, w_dw, b_dw, ln_conv_g, ln_conv_b, w_gate_lr, b_gate, gla_norm_g, w_o,
                 ln1_g, ln1_b, w_router, b_router):
    row = lambda v: v.reshape(1, -1).astype(F32)
    w_in_p = jnp.pad(w_in, ((0, 0), (0, IN_COLS_PAD - w_in.shape[1]))).astype(BF16)
    w_gate = jnp.pad(w_gate_lr, ((0, GATE_PAD - GATE_RANK), (0, 0))).astype(BF16)
    wr_hi = w_router.astype(BF16)
    wr_lo = (w_router - wr_hi.astype(F32)).astype(BF16)
    padr = lambda m: jnp.pad(m, ((0, 0), (0, LANES - N_EXPERTS)))
    return {
        "w_in": w_in_p, "w_dw": jnp.repeat(w_dw, 8, axis=0), "b_dw": row(b_dw), "ln_conv_g": row(ln_conv_g),
        "ln_conv_b": row(ln_conv_b), "w_gate": w_gate, "b_gate": row(b_gate),
        "gla_norm_g": row(gla_norm_g), "w_o": w_o.astype(BF16), "ln1_g": row(ln1_g),
        "ln1_b": row(ln1_b), "w_router": jnp.concatenate([padr(wr_hi), padr(wr_lo)], axis=1),
        "b_router": jnp.pad(row(b_router), ((0, 0), (0, LANES - N_EXPERTS))),
    }


def _dest_kernel(ps_ref, ri_ref, out_ref, *, n_slot):
    idx = ri_ref[0:TOP_K, :]
    dest = ri_ref[TOP_K:2 * TOP_K, :] + pl.program_id(0) * n_slot
    for e in range(N_EXPERTS):
        dest = dest + jnp.where(idx == e, ps_ref[e], 0)
    out_ref[...] = dest


def _moe_layout(ri, counts, *, tr):
    n_tok = ri.shape[1]
    cnt = counts[:, 0].astype(jnp.int32)
    padded = (cnt + EXPERT_BLOCK - 1) // EXPERT_BLOCK * EXPERT_BLOCK
    pad_end = jnp.cumsum(padded)
    pad_start = pad_end - padded
    nb = -(-(n_tok * TOP_K) // EXPERT_BLOCK) + N_EXPERTS
    n_t = n_tok // tr
    dest_seg = pl.pallas_call(
        functools.partial(_dest_kernel, n_slot=nb * EXPERT_BLOCK),
        grid_spec=pltpu.PrefetchScalarGridSpec(
            num_scalar_prefetch=1,
            grid=(N_SEG, n_t),
            in_specs=[pl.BlockSpec((2 * TOP_K, tr), lambda j, i, ps: (0, i))],
            out_specs=pl.BlockSpec((TOP_K, tr), lambda j, i, ps: (0, j * n_t + i)),
        ),
        out_shape=jax.ShapeDtypeStruct((TOP_K, N_SEG * n_tok), jnp.int32),
        name="slot_index",
    )(pad_start.astype(jnp.int32), ri)
    block_start = (jnp.concatenate([pad_start, pad_end[-1:]]) // EXPERT_BLOCK).astype(jnp.int32)
    n_used = block_start[-1:]
    return dest_seg, block_start, n_used, nb


def kernel(x_prompt, x_sample, state_conv, state_gla, w_in, w_dw, b_dw, ln_conv_g, ln_conv_b,
           w_gate_lr, b_gate, gla_norm_g, w_o, ln1_g, ln1_b, w_router, b_router, w_gu, b_gu,
           w_down, b_down, ln2_g, ln2_b):
    assert w_in.shape[0] == DEPTH
    l = 0
    p = _prep_params(w_in[l], w_dw[l], b_dw[l], ln_conv_g[l], ln_conv_b[l], w_gate_lr[l],
                     b_gate[l], gla_norm_g[l], w_o[l], ln1_g[l], ln1_b[l], w_router[l],
                     b_router[l])
    bp, tp, _ = x_prompt.shape
    bs, ts, _ = x_sample.shape
    zc = jnp.zeros((bp, CONV_W - 1, CONV_CH), F32)
    zs = jnp.zeros((bp, GLA_HEADS, GLA_DK, GLA_DV), F32)
    n_prompt = bp * tp
    n_tok = n_prompt + bs * ts
    x1p, x1, lg, conv_p, gla_p = _mixer(x_prompt, zc, zs, p, tm=PROMPT_TILE, n_tok_total=n_tok,
                                        tok_offset=0)
    x1p, x1, lg, conv_s, gla_s = _mixer(x_sample, state_conv[l], state_gla[l], p, tm=ts,
                                        n_tok_total=n_tok, tok_offset=n_prompt,
                                        token_bufs=(x1p, x1, lg),
                                        sb=SAMPLE_STREAMS if bs % SAMPLE_STREAMS == 0 else 1)

    tr = ROUTE_TILE if n_tok % ROUTE_TILE == 0 else bs * ts
    ri, rg, counts = _route(lg, tr=tr)
    dest_seg, block_start, n_used, nb = _moe_layout(ri, counts, tr=n_tok)
    n_slot = nb * EXPERT_BLOCK
    xs = _sc_scatter_rows(x1p.reshape(N_SEG * n_tok, SC_SEG), dest_seg, N_SEG * n_slot)
    yb = _ffn(block_start, n_used, xs.reshape(N_SEG, n_slot, SC_SEG), w_gu[l], b_gu[l], w_down[l],
              b_down[l])
    tf_last = bs * ts
    n_blk = n_prompt // tf_last
    last_blk = min(LAST_PIECE_BLOCKS, n_blk // 2)
    body_tok = (n_blk - last_blk) * tf_last
    tf_body = 2 * tf_last if body_tok % (2 * tf_last) == 0 else tf_last
    body_blk = body_tok // tf_body
    n_body = min(COMBINE_PIECES - 1, body_blk)
    bounds = [(p * body_blk // n_body) * tf_body for p in range(n_body)] + [body_tok, n_tok]
    dest_kjt = dest_seg.reshape(TOP_K, N_SEG, n_tok)
    yp = None
    for t0, t1 in zip(bounds[:-1], bounds[1:]):
        g = _sc_gather_rows(yb.reshape(N_SEG * n_slot, SC_SEG),
                            dest_kjt[:, :, t0:t1].reshape(1, -1))
        outs = _combine(x1, g.reshape(TOP_K, N_SEG, t1 - t0, SC_SEG), rg,
                        ln2_g[l].reshape(1, -1), ln2_b[l].reshape(1, -1), tok0=t0,
                        n_prompt=n_prompt, tf=tf_last if t1 == n_tok else tf_body, yp_buf=yp)
        yp = outs[0]
    ys = outs[1]
    return (yp.reshape(bp, tp, D_MODEL), ys.reshape(bs, ts, D_MODEL),
            conv_p[None], gla_p[None], conv_s[None], gla_s[None])
```

```python
import functools

import jax
import jax.numpy as jnp
from jax import lax
from jax.experimental import pallas as pl
from jax.experimental.pallas import tpu as pltpu
from jax.experimental.pallas import tpu_sc as plsc

F32 = jnp.float32
BF16 = jnp.bfloat16

LANES = 128
D_MODEL = 1024
CHUNK = 64
SUB = 16
DIAG = 8
CONV_CH = 512
CONV_W = 31
HALO = 32
PROMPT_TILE = 1024
MAX_STATIC_TRIPS = 4
GLU_ROWS = 128
CONV_BLOCK = 64
ROUTE_TILE = 768
SAMPLE_STREAMS = 16
LAST_PIECE_BLOCKS = 4
COMBINE_PIECES = 6
TRIP_BLOCKS = 4
IN_AHEAD = TRIP_BLOCKS
IN_RING = 2 * TRIP_BLOCKS
OUT_RING = 2 * TRIP_BLOCKS
GLA_HEADS = 4
GLA_DK = 64
GLA_DV = 128
GLA_K = GLA_HEADS * GLA_DK
GLA_V = GLA_HEADS * GLA_DV
GATE_RANK = 16
GATE_PAD = 128
GATE_NORM = 16.0
MIX_W = CONV_CH + GLA_V
OFF_UV = 0
OFF_UG = OFF_UV + CONV_CH
OFF_Q = OFF_UG + CONV_CH
OFF_K = OFF_Q + GLA_K
OFF_V = OFF_K + GLA_K
OFF_G = OFF_V + GLA_V
OFF_A = OFF_G + GLA_V
IN_COLS_PAD = OFF_A + GATE_PAD
N_EXPERTS = 32
TOP_K = 4
D_FF = 1024
SWIGLU_LIMIT = 7.0
SWIGLU_ALPHA = 1.702
EXPERT_BLOCK = 256
LN_EPS = 1e-5
DEPTH = 1
DN_ALPHA = (2 * DEPTH) ** 0.25
VMEM_LIMIT = 56 * 1024 * 1024
HALF = D_MODEL // 2
SC_SEG = 256
SC_WINDOW = 128
N_SEG = HALF // SC_SEG
U32 = jnp.uint32


def _dot(a, b):
    return jnp.dot(a, b, preferred_element_type=F32)


def _dot_nt(a, b):
    return lax.dot_general(a, b, (((1,), (1,)), ((), ())), preferred_element_type=F32)


def _dot_tn(a, b):
    return lax.dot_general(a, b, (((0,), (0,)), ((), ())), preferred_element_type=F32)


def _layer_norm(x, g, b):
    mu = jnp.mean(x, axis=-1, keepdims=True)
    xc = x - mu
    var = jnp.mean(xc * xc, axis=-1, keepdims=True)
    return xc * lax.rsqrt(var + LN_EPS) * g + b


def _sigmoid(x):
    return 1.0 / (1.0 + jnp.exp(-x))


def _split_bf16(x):
    hi = x.astype(BF16)
    lo = (x - hi.astype(F32)).astype(BF16)
    return hi, lo


def _pack_segments(x):
    bits = pltpu.bitcast(x.astype(BF16).astype(F32), U32)
    word = (bits[:, :HALF] >> 16) | bits[:, HALF:]
    return [word[:, j * SC_SEG:(j + 1) * SC_SEG] for j in range(N_SEG)]


def _unpack_segments(segs):
    word = jnp.concatenate(segs, axis=1)
    lo = pltpu.bitcast(word << 16, F32)
    hi = pltpu.bitcast(word & jnp.uint32(0xFFFF0000), F32)
    return lo, hi


def _mixer_kernel(x_ref, cs_ref, gs_ref, w_in_ref, w_dw_ref, b_dw_ref, lncg_ref, lncb_ref,
                  wg_ref, bg_ref, gng_ref, w_o_ref, ln1g_ref, ln1b_ref, wr_ref, br_ref, *rest,
                  tm, chunk, n_alias, sb):
    (x1p_ref, x1_ref, logit_ref, conv_out_ref, gla_out_ref,
     ubuf, qbuf, kbuf, labuf, vbuf, gbuf, mixbuf, st_ref) = rest[n_alias:]
    t = pl.program_id(1)
    nt = pl.num_programs(1)
    n_chunks = tm // chunk
    n_sub = chunk // SUB
    n_rows = sb * tm
    us = HALO + tm

    @pl.when(t == 0)
    def _init():
        for s in range(sb):
            ubuf[s * us:s * us + HALO - (CONV_W - 1), :] = jnp.zeros(
                (HALO - (CONV_W - 1), CONV_CH), F32)
            ubuf[s * us + HALO - (CONV_W - 1):s * us + HALO, :] = cs_ref[s]
            for h in range(GLA_HEADS):
                st_ref[s * GLA_DV:(s + 1) * GLA_DV, h * GLA_DK:(h + 1) * GLA_DK] = gs_ref[s, h].T

    x = x_ref[...].reshape(n_rows, D_MODEL)
    xb = x.astype(BF16)

    gp = min(GLU_ROWS, tm) if sb == 1 else n_rows
    for r0 in range(0, n_rows, gp):
        hv = _dot(xb[r0:r0 + gp], w_in_ref[:, OFF_UV:OFF_UG])
        hg = _dot(xb[r0:r0 + gp], w_in_ref[:, OFF_UG:OFF_Q])
        u = hv * _sigmoid(hg)
        for s in range(sb):
            lo = max(r0, s * tm)
            hi = min(r0 + gp, (s + 1) * tm)
            if lo < hi:
                ubuf[s * us + HALO + lo - s * tm:s * us + HALO + hi - s * tm, :] = u[lo - r0:hi - r0]
    lead = HALO - (CONV_W - 1)

    def proj_q(r0, n):
        qbuf[r0:r0 + n, :] = _dot(xb[r0:r0 + n], w_in_ref[:, OFF_Q:OFF_K]) * (GLA_DK ** -0.5)

    def proj_k(r0, n):
        kbuf[r0:r0 + n, :] = _dot(xb[r0:r0 + n], w_in_ref[:, OFF_K:OFF_V])

    def proj_v(half, r0, n):
        lo = half * (GLA_V // 2)
        vbuf[r0:r0 + n, lo:lo + GLA_V // 2] = _dot(
            xb[r0:r0 + n], w_in_ref[:, OFF_V + lo:OFF_V + lo + GLA_V // 2])

    def proj_g(half, r0, n):
        lo = half * (GLA_V // 2)
        gbuf[r0:r0 + n, lo:lo + GLA_V // 2] = _dot(
            xb[r0:r0 + n], w_in_ref[:, OFF_G + lo:OFF_G + lo + GLA_V // 2])

    def proj_gate(r0, n):
        ha = _dot(xb[r0:r0 + n], w_in_ref[:, OFF_A:IN_COLS_PAD])
        a = _dot(ha.astype(BF16), wg_ref[...]) + bg_ref[...]
        labuf[r0:r0 + n, :] = ((jnp.minimum(a, 0.0) - jnp.log(1.0 + jnp.exp(-jnp.abs(a))))
                               * (1.0 / GATE_NORM))

    proj_jobs = [proj_q, proj_k, functools.partial(proj_v, 0), functools.partial(proj_v, 1),
                 functools.partial(proj_g, 0), functools.partial(proj_g, 1), proj_gate]

    cb = min(CONV_BLOCK, tm)

    def conv_block(stream, r0):
        acc = jnp.zeros((cb, CONV_CH), F32) + b_dw_ref[...]
        for r in range(8):
            rows = cb if r == 0 else cb + 8
            part = None
            for s in range(r, lead + CONV_W, 8):
                if s < lead:
                    continue
                w8 = w_dw_ref[8 * (s - lead):8 * (s - lead) + 8, :]
                u3 = ubuf[pl.ds(stream * us + r0 + (s - r), rows), :].reshape(
                    rows // 8, 8, CONV_CH)
                term = u3 * w8
                part = term if part is None else part + term
            acc = acc + part.reshape(rows, CONV_CH)[r:r + cb, :]
        cact = _layer_norm(acc, lncg_ref[...], lncb_ref[...])
        cact = cact * _sigmoid(cact)
        mixbuf[pl.ds(stream * tm + r0, cb), 0:CONV_CH] = cact.astype(BF16)

    per_trip = 4 if n_chunks % 4 == 0 else 1
    n_trips = n_chunks // per_trip
    static_trips = 2 <= n_trips <= MAX_STATIC_TRIPS and sb == 1
    trip_rows = per_trip * chunk
    loop_blocks = (tm // 2) // (cb * n_trips) if n_chunks >= 4 else 0
    pre_blocks = tm // cb - loop_blocks * n_trips
    rows_first = trip_rows if static_trips else n_rows
    pre = [(s, blk * cb) for s in range(sb) for blk in range(pre_blocks)]
    for i in range(max(len(pre), len(proj_jobs))):
        if i < len(proj_jobs):
            proj_jobs[i](0, rows_first)
        if i < len(pre):
            conv_block(*pre[i])

    lane_k = lax.broadcasted_iota(jnp.int32, (1, GLA_K), 1) // GLA_DK
    row_c = lax.broadcasted_iota(jnp.int32, (chunk, 1), 0)
    tri = (lax.broadcasted_iota(jnp.int32, (chunk, chunk), 0)
           >= lax.broadcasted_iota(jnp.int32, (chunk, chunk), 1)).astype(BF16)
    e2 = (lax.broadcasted_iota(jnp.int32, (GLA_K, GLA_V), 0) // GLA_DK
          == lax.broadcasted_iota(jnp.int32, (GLA_K, GLA_V), 1) // GLA_DV).astype(BF16)
    row_s = lax.broadcasted_iota(jnp.int32, (GLA_HEADS * chunk, 1), 0)
    blk_s = (row_s % chunk) // SUB
    same_blk = blk_s == lax.broadcasted_iota(jnp.int32, (1, LANES), 1) // SUB
    second_half = row_c % SUB >= DIAG
    row_g = lax.broadcasted_iota(jnp.int32, (1, DIAG, 1), 1)
    lane_v = lax.broadcasted_iota(jnp.int32, (1, GLA_V), 1) % GLA_DV

    def head_stack(m):
        return jnp.concatenate(
            [jnp.where(lane_k == h, m, 0.0) for h in range(GLA_HEADS)], axis=0).astype(BF16)

    def unstack(r, width):
        return jnp.concatenate(
            [r[h * chunk:(h + 1) * chunk, h * width:(h + 1) * width] for h in range(GLA_HEADS)],
            axis=1)

    def chunk_step(stream, c, carry):
        base = stream * tm + (c * chunk if isinstance(c, int) else pl.multiple_of(c * chunk, chunk))
        st_rows = slice(stream * GLA_DV, (stream + 1) * GLA_DV)
        q_c = qbuf[pl.ds(base, chunk), :]
        k_c = kbuf[pl.ds(base, chunk), :]
        v_c = vbuf[pl.ds(base, chunk), :]
        v_b = v_c.astype(BF16)
        la_c = labuf[pl.ds(base, chunk), :]
        la_hi, la_lo = _split_bf16(la_c)
        b = _dot(tri, la_hi) + _dot(tri, la_lo)
        b_last = b[chunk - 1:chunk, :]
        st = st_ref[st_rows, :]

        qe = q_c * jnp.exp(b)
        o2 = _dot_nt(head_stack(qe), st.astype(BF16))
        o = jnp.concatenate([o2[h * chunk:(h + 1) * chunk, :] for h in range(GLA_HEADS)], axis=1)

        a_off = None
        if n_sub > 1:
            r_rows = [b[0:SUB, :]] + [jnp.broadcast_to(b[SUB * i - 1:SUB * i, :], (SUB, GLA_K))
                                      for i in range(1, n_sub)]
            r_q = jnp.concatenate(r_rows, axis=0)
            q_t = jnp.where(row_c >= SUB, q_c * jnp.exp(jnp.minimum(b - r_q, 0.0)), 0.0)
            k_parts = []
            for i in range(1, n_sub):
                r_i = b[SUB * i - 1:SUB * i, :]
                k_t = jnp.where(row_c < SUB * i, k_c * jnp.exp(jnp.minimum(r_i - b, 0.0)), 0.0)
                k_parts.append(k_t)
                k_parts.append(jnp.zeros((LANES - chunk, GLA_K), F32))
            k_cat = jnp.concatenate(k_parts, axis=0).astype(BF16)
            r = _dot_nt(head_stack(q_t), k_cat)
            a_off = r[:, 0:LANES]
            for i in range(2, n_sub):
                a_off = jnp.where(blk_s == i, r[:, (i - 1) * LANES:i * LANES], a_off)

        r_b = jnp.concatenate(
            [jnp.broadcast_to(b[SUB * i + DIAG - 1:SUB * i + DIAG, :], (SUB, GLA_K))
             for i in range(n_sub)], axis=0)
        q_b = jnp.where(second_half, q_c * jnp.exp(jnp.minimum(b - r_b, 0.0)), 0.0)
        k_b = jnp.where(second_half, 0.0, k_c * jnp.exp(jnp.minimum(r_b - b, 0.0)))
        k_b = jnp.concatenate([k_b, jnp.zeros((LANES - chunk, GLA_K), F32)], axis=0).astype(BF16)
        a_half = jnp.where(same_blk, _dot_nt(head_stack(q_b), k_b), 0.0)
        a_off = a_half if a_off is None else a_off + a_half

        q3 = q_c.reshape(chunk // DIAG, DIAG, GLA_K)
        k3 = k_c.reshape(chunk // DIAG, DIAG, GLA_K)
        b3 = b.reshape(chunk // DIAG, DIAG, GLA_K)
        ps = []
        for d in range(DIAG):
            k_s = k3 if d == 0 else pltpu.roll(k3, d, axis=1)
            b_s = b3 if d == 0 else pltpu.roll(b3, d, axis=1)
            p = jnp.where(row_g >= d, q3 * k_s * jnp.exp(jnp.minimum(b3 - b_s, 0.0)), 0.0)
            ps.append(p.reshape(chunk, GLA_K).astype(BF16))
        w_all = _dot(jnp.concatenate(ps, axis=0), e2)
        a_d = jnp.zeros((chunk, GLA_V), F32)
        for d in range(DIAG):
            a_d = jnp.where(lane_v == row_c - d, w_all[d * chunk:(d + 1) * chunk, :], a_d)
        a_off = a_off + jnp.concatenate(
            [a_d[:, h * GLA_DV:(h + 1) * GLA_DV] for h in range(GLA_HEADS)], axis=0)
        o1 = _dot(a_off[:, 0:chunk].astype(BF16), v_b)
        o = o + unstack(o1, GLA_DV)

        ke = k_c * jnp.exp(b_last - b)
        r2 = _dot_tn(v_b, ke.astype(BF16))
        upd = jnp.zeros((GLA_DV, GLA_K), F32)
        for h in range(GLA_HEADS):
            upd = upd + jnp.where(lane_k == h, r2[h * GLA_DV:(h + 1) * GLA_DV, :], 0.0)
        st_ref[st_rows, :] = st * jnp.exp(b_last) + upd

        g_c = gbuf[pl.ds(base, chunk), :]
        outs = []
        for h in range(GLA_HEADS):
            oh = o[:, h * GLA_DV:(h + 1) * GLA_DV]
            ms = jnp.mean(oh * oh, axis=-1, keepdims=True)
            outs.append(oh * lax.rsqrt(ms + LN_EPS))
        on = jnp.concatenate(outs, axis=1) * gng_ref[...]
        on = on * (g_c * _sigmoid(g_c))
        mixbuf[pl.ds(base, chunk), CONV_CH:MIX_W] = on.astype(BF16)
        return carry

    def trip(stream, i, carry, extra_jobs=()):
        for j in range(max(per_trip, loop_blocks, len(extra_jobs))):
            if j < len(extra_jobs):
                extra_jobs[j]((i + 1) * trip_rows, trip_rows)
            if j < loop_blocks:
                r0 = (pre_blocks + i * loop_blocks + j) * cb
                conv_block(stream, r0 if isinstance(r0, int) else pl.multiple_of(r0, cb))
            if j < per_trip:
                carry = chunk_step(stream, i * per_trip + j, carry)
        return carry

    def merge_rows(r0, n):
        y = _dot(mixbuf[r0:r0 + n, :], w_o_ref[...])
        x1 = _layer_norm(DN_ALPHA * x[r0:r0 + n, :] + y, ln1g_ref[...], ln1b_ref[...])
        x1_ref[r0:r0 + n, :] = x1
        for j, seg in enumerate(_pack_segments(x1)):
            x1p_ref[j, r0:r0 + n, :] = seg
        x_hi, x_lo = _split_bf16(x1)
        lg = _dot(x_hi, wr_ref[...])
        logit_ref[r0:r0 + n, :] = (lg[:, 0:LANES] + lg[:, LANES:2 * LANES]
                                   + _dot(x_lo, wr_ref[:, 0:LANES]) + br_ref[...])

    if static_trips:
        for i in range(n_trips):
            trip(0, i, 0, tuple(proj_jobs) if i + 1 < n_trips else ())
            merge_rows(i * trip_rows, trip_rows)
    else:
        for s in range(sb):
            if n_trips == 1:
                trip(s, 0, 0)
            else:
                lax.fori_loop(0, n_trips, functools.partial(trip, s), 0)
        merge_rows(0, n_rows)

    tails = []
    for s in range(sb):
        tail = ubuf[s * us + tm:s * us + tm + HALO, :]
        ubuf[s * us:s * us + HALO, :] = tail
        tails.append(tail)

    @pl.when(t == nt - 1)
    def _state_out():
        for s in range(sb):
            conv_out_ref[s] = tails[s][HALO - (CONV_W - 1):, :]
            st = st_ref[s * GLA_DV:(s + 1) * GLA_DV, :]
            for h in range(GLA_HEADS):
                gla_out_ref[s, h] = st[:, h * GLA_DK:(h + 1) * GLA_DK].T


def _mixer(x, conv_state, gla_state, p, *, tm, n_tok_total, tok_offset, token_bufs=None, sb=1):
    bsz, seq, _ = x.shape
    chunk = min(CHUNK, seq)
    rows = sb * tm
    assert seq % tm == 0 and tm % chunk == 0 and chunk % SUB == 0 and tok_offset % rows == 0
    assert bsz % sb == 0 and (sb == 1 or seq == tm)
    nt = seq // tm
    blk0 = tok_offset // rows
    full = lambda shape: pl.BlockSpec(shape, lambda b, t: (0,) * len(shape))
    alias = () if token_bufs is None else tuple(token_bufs)
    kern = functools.partial(_mixer_kernel, tm=tm, chunk=chunk, n_alias=len(alias), sb=sb)
    n_in = 16
    return pl.pallas_call(
        kern,
        grid=(bsz // sb, nt),
        input_output_aliases={n_in + i: i for i in range(len(alias))},
        in_specs=[
            pl.BlockSpec((sb, tm, D_MODEL), lambda b, t: (b, t, 0)),
            pl.BlockSpec((sb, CONV_W - 1, CONV_CH), lambda b, t: (b, 0, 0)),
            pl.BlockSpec((sb, GLA_HEADS, GLA_DK, GLA_DV), lambda b, t: (b, 0, 0, 0)),
            full((D_MODEL, IN_COLS_PAD)),
            full((8 * CONV_W, CONV_CH)),
            full((1, CONV_CH)),
            full((1, CONV_CH)),
            full((1, CONV_CH)),
            full((GATE_PAD, GLA_K)),
            full((1, GLA_K)),
            full((1, GLA_V)),
            full((MIX_W, D_MODEL)),
            full((1, D_MODEL)),
            full((1, D_MODEL)),
            full((D_MODEL, 2 * LANES)),
            full((1, LANES)),
        ] + [pl.BlockSpec(memory_space=pl.ANY)] * len(alias),
        out_specs=[
            pl.BlockSpec((N_SEG, rows, SC_SEG), lambda b, t: (0, blk0 + b * nt + t, 0)),
            pl.BlockSpec((rows, D_MODEL), lambda b, t: (blk0 + b * nt + t, 0)),
            pl.BlockSpec((rows, LANES), lambda b, t: (blk0 + b * nt + t, 0)),
            pl.BlockSpec((sb, CONV_W - 1, CONV_CH), lambda b, t: (b, 0, 0)),
            pl.BlockSpec((sb, GLA_HEADS, GLA_DK, GLA_DV), lambda b, t: (b, 0, 0, 0)),
        ],
        out_shape=[
            jax.ShapeDtypeStruct((N_SEG, n_tok_total, SC_SEG), U32),
            jax.ShapeDtypeStruct((n_tok_total, D_MODEL), F32),
            jax.ShapeDtypeStruct((n_tok_total, LANES), F32),
            jax.ShapeDtypeStruct((bsz, CONV_W - 1, CONV_CH), F32),
            jax.ShapeDtypeStruct((bsz, GLA_HEADS, GLA_DK, GLA_DV), F32),
        ],
        scratch_shapes=[
            pltpu.VMEM((sb * (HALO + tm), CONV_CH), F32),
            pltpu.VMEM((rows, GLA_K), F32),
            pltpu.VMEM((rows, GLA_K), F32),
            pltpu.VMEM((rows, GLA_K), F32),
            pltpu.VMEM((rows, GLA_V), F32),
            pltpu.VMEM((rows, GLA_V), F32),
            pltpu.VMEM((rows, MIX_W), BF16),
            pltpu.VMEM((sb * GLA_DV, GLA_K), F32),
        ],
        compiler_params=pltpu.CompilerParams(
            dimension_semantics=("arbitrary", "arbitrary"),
            vmem_limit_bytes=VMEM_LIMIT),
        name="mixer",
    )(x, conv_state, gla_state, p["w_in"], p["w_dw"], p["b_dw"], p["ln_conv_g"], p["ln_conv_b"],
      p["w_gate"], p["b_gate"], p["gla_norm_g"], p["w_o"], p["ln1_g"], p["ln1_b"],
      p["w_router"], p["b_router"], *alias)


def _route_kernel(lg_ref, ri_ref, rg_ref, cnt_ref, carry_ref, *, tr):
    i = pl.program_id(0)

    @pl.when(i == 0)
    def _init():
        carry_ref[...] = jnp.zeros((N_EXPERTS, LANES), F32)

    l = lg_ref[...].T[0:N_EXPERTS, :]
    row = lax.broadcasted_iota(jnp.int32, (N_EXPERTS, tr), 0)
    hots, vals, idxs = [], [], []
    for _ in range(TOP_K):
        m = jnp.max(l, axis=0, keepdims=True)
        ik = jnp.min(jnp.where(l == m, row, N_EXPERTS), axis=0, keepdims=True)
        hot = row == ik
        hots.append(hot)
        vals.append(m)
        idxs.append(ik)
        l = jnp.where(hot, -jnp.inf, l)
    es = [jnp.exp(v - vals[0]) for v in vals]
    denom = es[0] + es[1] + es[2] + es[3]
    member = jnp.zeros((N_EXPERTS, tr), F32)
    for hot in hots:
        member = member + hot.astype(F32)
    before = (lax.broadcasted_iota(jnp.int32, (tr, tr), 0)
              < lax.broadcasted_iota(jnp.int32, (tr, tr), 1)).astype(BF16)
    cum = _dot(member.astype(BF16), before) + carry_ref[:, 0:1]
    ranks = [jnp.sum(jnp.where(hot, cum, 0.0), axis=0, keepdims=True).astype(jnp.int32)
             for hot in hots]
    ri_ref[...] = jnp.concatenate(idxs + ranks, axis=0)
    gates = jnp.concatenate([e / denom for e in es] + [jnp.zeros((LANES - TOP_K, tr), F32)], axis=0)
    rg_ref[...] = gates.T
    carry_ref[...] = carry_ref[...] + jnp.sum(member, axis=1, keepdims=True)

    @pl.when(i == pl.num_programs(0) - 1)
    def _fin():
        cnt_ref[...] = carry_ref[...]


def _route(logits, *, tr):
    n_tok = logits.shape[0]
    assert n_tok % tr == 0
    return pl.pallas_call(
        functools.partial(_route_kernel, tr=tr),
        grid=(n_tok // tr,),
        in_specs=[pl.BlockSpec((tr, LANES), lambda i: (i, 0))],
        out_specs=[pl.BlockSpec((2 * TOP_K, tr), lambda i: (0, i)),
                   pl.BlockSpec((tr, LANES), lambda i: (i, 0)),
                   pl.BlockSpec((N_EXPERTS, LANES), lambda i: (0, 0))],
        out_shape=[jax.ShapeDtypeStruct((2 * TOP_K, n_tok), jnp.int32),
                   jax.ShapeDtypeStruct((n_tok, LANES), F32),
                   jax.ShapeDtypeStruct((N_EXPERTS, LANES), F32)],
        scratch_shapes=[pltpu.VMEM((N_EXPERTS, LANES), F32)],
        compiler_params=pltpu.CompilerParams(dimension_semantics=("arbitrary",)),
        name="route",
    )(logits)


def _ffn_kernel(gs_ref, nu_ref, xs_hbm, wgu_ref, bgu_ref, wdn_ref, bdn_ref, yb_hbm,
                wgu_bf, wdn_bf, xin, yout, in_sem, out_sem):
    e = pl.program_id(0)
    n_used = nu_ref[0]
    g_first = gs_ref[e]
    g_end = gs_ref[e + 1]

    def in_copy(g, slot):
        rows = pl.ds(pl.multiple_of(g * EXPERT_BLOCK, EXPERT_BLOCK), EXPERT_BLOCK)
        return pltpu.make_async_copy(xs_hbm.at[:, rows, :], xin.at[slot], in_sem.at[slot])

    def out_copy(g, slot):
        rows = pl.ds(pl.multiple_of(g * EXPERT_BLOCK, EXPERT_BLOCK), EXPERT_BLOCK)
        return pltpu.make_async_copy(yout.at[slot], yb_hbm.at[:, rows, :], out_sem.at[slot])

    @pl.when(e == 0)
    def _prime():
        for g in range(IN_AHEAD):
            @pl.when(g < n_used)
            def _():
                in_copy(g, g).start()

    def cast_weights():
        wgu_bf[...] = wgu_ref[0].astype(BF16)
        wdn_bf[...] = wdn_ref[0].astype(BF16)

    def blocks(g0, n, with_cast=False):
        for j in range(n):
            g = g0 + j
            in_copy(g, g % IN_RING).wait()
        for j in range(n):
            g = g0 + IN_AHEAD + j

            @pl.when(g < n_used)
            def _prefetch():
                in_copy(g, g % IN_RING).start()
        for j in range(n):
            g = g0 + j

            @pl.when(g >= OUT_RING)
            def _free_out():
                out_copy(g - OUT_RING, g % OUT_RING).wait()
        if with_cast:
            cast_weights()
        for j in range(n):
            g = g0 + j
            x_lo, x_hi = _unpack_segments([xin[g % IN_RING, s] for s in range(N_SEG)])
            x_bf = jnp.concatenate([x_lo.astype(BF16), x_hi.astype(BF16)], axis=1)
            hgu = _dot(x_bf, wgu_bf[...]) + bgu_ref[0]
            gate = jnp.minimum(hgu[:, :D_FF], SWIGLU_LIMIT)
            up = jnp.clip(hgu[:, D_FF:], -SWIGLU_LIMIT, SWIGLU_LIMIT)
            act = (up + 1.0) * gate * _sigmoid(SWIGLU_ALPHA * gate)
            yb = _dot(act.astype(BF16), wdn_bf[...]) + bdn_ref[0]
            for s, seg in enumerate(_pack_segments(yb)):
                yout[g % OUT_RING, s] = seg
        for j in range(n):
            g = g0 + j
            out_copy(g, g % OUT_RING).start()

    n_blocks = g_end - g_first
    n_full = n_blocks // TRIP_BLOCKS

    @pl.when((n_full == 0) & (n_blocks > 0))
    def _cast_only():
        cast_weights()

    @pl.when(n_full >= 1)
    def _first_trip():
        blocks(g_first, TRIP_BLOCKS, with_cast=True)

    def full_trip(i, carry):
        blocks(g_first + TRIP_BLOCKS * i, TRIP_BLOCKS)
        return carry

    lax.fori_loop(1, n_full, full_trip, 0)
    done = n_full * TRIP_BLOCKS
    size = TRIP_BLOCKS // 2
    while size >= 1:
        @pl.when((n_blocks // size) % 2 == 1)
        def _rest(done=done, size=size):
            blocks(g_first + done, size)
        done = done + jnp.where((n_blocks // size) % 2 == 1, size, 0)
        size //= 2

    @pl.when(e == pl.num_programs(0) - 1)
    def _drain():
        for back in range(OUT_RING, 0, -1):
            @pl.when(n_used >= back)
            def _():
                out_copy(n_used - back, (n_used - back) % OUT_RING).wait()


def _ffn(block_start, n_used, xs, w_gu, b_gu, w_down, b_down):
    n_rows = xs.shape[1]
    grid_spec = pltpu.PrefetchScalarGridSpec(
        num_scalar_prefetch=2,
        grid=(N_EXPERTS,),
        in_specs=[
            pl.BlockSpec(memory_space=pl.ANY),
            pl.BlockSpec((1, D_MODEL, 2 * D_FF), lambda e, gs, nu: (e, 0, 0)),
            pl.BlockSpec((1, 1, 2 * D_FF), lambda e, gs, nu: (e, 0, 0)),
            pl.BlockSpec((1, D_FF, D_MODEL), lambda e, gs, nu: (e, 0, 0)),
            pl.BlockSpec((1, 1, D_MODEL), lambda e, gs, nu: (e, 0, 0)),
        ],
        out_specs=pl.BlockSpec(memory_space=pl.ANY),
        scratch_shapes=[pltpu.VMEM((D_MODEL, 2 * D_FF), BF16),
                        pltpu.VMEM((D_FF, D_MODEL), BF16),
                        pltpu.VMEM((IN_RING, N_SEG, EXPERT_BLOCK, SC_SEG), U32),
                        pltpu.VMEM((OUT_RING, N_SEG, EXPERT_BLOCK, SC_SEG), U32),
                        pltpu.SemaphoreType.DMA((IN_RING,)),
                        pltpu.SemaphoreType.DMA((OUT_RING,))],
    )
    return pl.pallas_call(
        _ffn_kernel,
        grid_spec=grid_spec,
        out_shape=jax.ShapeDtypeStruct((N_SEG, n_rows, SC_SEG), U32),
        compiler_params=pltpu.CompilerParams(
            dimension_semantics=("arbitrary",), vmem_limit_bytes=VMEM_LIMIT),
        name="expert_ffn",
    )(block_start, n_used, xs, w_gu, b_gu.reshape(N_EXPERTS, 1, 2 * D_FF), w_down,
      b_down.reshape(N_EXPERTS, 1, D_MODEL))


def _combine_kernel(x1_ref, g_ref, rg_ref, ln2g_ref, ln2b_ref, *rest, blk0, n_prompt_blocks,
                    n_alias, with_sample):
    yp_ref = rest[n_alias]
    i = blk0 + pl.program_id(0)
    m_lo = jnp.zeros((x1_ref.shape[0], HALF), F32)
    m_hi = jnp.zeros((x1_ref.shape[0], HALF), F32)
    for k in range(TOP_K):
        g_lo, g_hi = _unpack_segments([g_ref[k, j] for j in range(N_SEG)])
        gate = rg_ref[:, k:k + 1]
        m_lo = m_lo + gate * g_lo
        m_hi = m_hi + gate * g_hi
    z = DN_ALPHA * x1_ref[...] + jnp.concatenate([m_lo, m_hi], axis=1)
    y = _layer_norm(z, ln2g_ref[...], ln2b_ref[...])

    if with_sample:
        ys_ref = rest[n_alias + 1]

        @pl.when(i < n_prompt_blocks)
        def _p():
            yp_ref[...] = y

        @pl.when(i >= n_prompt_blocks)
        def _s():
            ys_ref[...] = y
    else:
        yp_ref[...] = y


def _combine(x1, g, rg, ln2_g, ln2_b, *, tok0, n_prompt, tf, yp_buf=None):
    n_tok = x1.shape[0]
    piece = g.shape[2]
    n_sample = n_tok - n_prompt
    assert n_prompt % tf == 0 and tok0 % tf == 0 and piece % tf == 0
    npb = n_prompt // tf
    blk0 = tok0 // tf
    with_sample = tok0 + piece > n_prompt
    assert not with_sample or (tok0 + piece == n_tok and n_sample % tf == 0)
    alias = () if yp_buf is None else (yp_buf,)
    out_specs = [pl.BlockSpec((tf, D_MODEL), lambda i: (jnp.minimum(blk0 + i, npb - 1), 0))]
    out_shape = [jax.ShapeDtypeStruct((n_prompt, D_MODEL), F32)]
    if with_sample:
        out_specs.append(pl.BlockSpec((tf, D_MODEL), lambda i: (jnp.maximum(blk0 + i - npb, 0), 0)))
        out_shape.append(jax.ShapeDtypeStruct((n_sample, D_MODEL), F32))
    n_in = 5
    return pl.pallas_call(
        functools.partial(_combine_kernel, blk0=blk0, n_prompt_blocks=npb, n_alias=len(alias),
                          with_sample=with_sample),
        grid=(piece // tf,),
        input_output_aliases={n_in + i: i for i in range(len(alias))},
        in_specs=[pl.BlockSpec((tf, D_MODEL), lambda i: (blk0 + i, 0)),
                  pl.BlockSpec((TOP_K, N_SEG, tf, SC_SEG), lambda i: (0, 0, i, 0)),
                  pl.BlockSpec((tf, LANES), lambda i: (blk0 + i, 0)),
                  pl.BlockSpec((1, D_MODEL), lambda i: (0, 0)),
                  pl.BlockSpec((1, D_MODEL), lambda i: (0, 0))]
        + [pl.BlockSpec(memory_space=pl.ANY)] * len(alias),
        out_specs=out_specs,
        out_shape=out_shape,
        compiler_params=pltpu.CompilerParams(
            dimension_semantics=("arbitrary",), vmem_limit_bytes=VMEM_LIMIT),
        name="combine",
    )(x1, g, rg, ln2_g, ln2_b, *alias)


def _sc_mesh():
    return plsc.VectorSubcoreMesh(core_axis_name="c", subcore_axis_name="s")


def _sc_scatter_rows(src, dest, n_out):
    n_src, width = src.shape
    n_k = dest.shape[0]
    assert width == SC_SEG and n_src % SC_WINDOW == 0 and dest.shape[1] == n_src

    @pl.kernel(out_type=jax.ShapeDtypeStruct((n_out, width), src.dtype), mesh=_sc_mesh(),
               scratch_types=[])
    def scatter_kernel(src_hbm, idx_hbm, out_hbm):
        def body(src_vmem, idx_vmem):
            for k in range(n_k):
                pltpu.sync_copy(src_vmem, out_hbm.at[idx_vmem.at[k]])

        pltpu.emit_pipeline(
            body, grid=(n_src // SC_WINDOW,),
            in_specs=[pl.BlockSpec((SC_WINDOW, width), lambda i: (i, 0)),
                      pl.BlockSpec((n_k, SC_WINDOW), lambda i: (0, i))],
            out_specs=[],
            core_axis_name=("c", "s"),
            dimension_semantics=(pltpu.PARALLEL,),
        )(src_hbm, idx_hbm)

    return scatter_kernel(src, dest)


def _sc_gather_rows(src, idx):
    n_out = idx.shape[1]
    width = src.shape[1]
    assert width == SC_SEG and n_out % SC_WINDOW == 0

    @pl.kernel(out_type=jax.ShapeDtypeStruct((n_out, width), src.dtype), mesh=_sc_mesh(),
               scratch_types=[])
    def gather_kernel(src_hbm, idx_hbm, out_hbm):
        def body(idx_vmem, out_vmem):
            pltpu.sync_copy(src_hbm.at[idx_vmem.at[0]], out_vmem)

        pltpu.emit_pipeline(
            body, grid=(n_out // SC_WINDOW,),
            in_specs=[pl.BlockSpec((1, SC_WINDOW), lambda i: (0, i))],
            out_specs=[pl.BlockSpec((SC_WINDOW, width), lambda i: (i, 0))],
            core_axis_name=("c", "s"),
            dimension_semantics=(pltpu.PARALLEL,),
        )(idx_hbm, out_hbm)

    return gather_kernel(src, idx)


def _prep_params(w_in, w_dw, b_dw, ln_conv_g, ln_conv_b, w_gate_lr, b_gate, gla_norm_g, w_o,
                 ln1_g, ln1_b, w_router, b_router):
    row = lambda v: v.reshape(1, -1).astype(F32)
    w_in_p = jnp.pad(w_in, ((0, 0), (0, IN_COLS_PAD - w_in.shape[1]))).astype(BF16)
    w_gate = jnp.pad(w_gate_lr, ((0, GATE_PAD - GATE_RANK), (0, 0))).astype(BF16)
    wr_hi = w_router.astype(BF16)
    wr_lo = (w_router - wr_hi.astype(F32)).astype(BF16)
    padr = lambda m: jnp.pad(m, ((0, 0), (0, LANES - N_EXPERTS)))
    return {
        "w_in": w_in_p, "w_dw": jnp.repeat(w_dw, 8, axis=0), "b_dw": row(b_dw), "ln_conv_g": row(ln_conv_g),
        "ln_conv_b": row(ln_conv_b), "w_gate": w_gate, "b_gate": row(b_gate),
        "gla_norm_g": row(gla_norm_g), "w_o": w_o.astype(BF16), "ln1_g": row(ln1_g),
        "ln1_b": row(ln1_b), "w_router": jnp.concatenate([padr(wr_hi), padr(wr_lo)], axis=1),
        "b_router": jnp.pad(row(b_router), ((0, 0), (0, LANES - N_EXPERTS))),
    }


def _dest_kernel(ps_ref, ri_ref, out_ref, *, n_slot):
    idx = ri_ref[0:TOP_K, :]
    dest = ri_ref[TOP_K:2 * TOP_K, :] + pl.program_id(0) * n_slot
    for e in range(N_EXPERTS):
        dest = dest + jnp.where(idx == e, ps_ref[e], 0)
    out_ref[...] = dest


def _moe_layout(ri, counts, *, tr):
    n_tok = ri.shape[1]
    cnt = counts[:, 0].astype(jnp.int32)
    padded = (cnt + EXPERT_BLOCK - 1) // EXPERT_BLOCK * EXPERT_BLOCK
    pad_end = jnp.cumsum(padded)
    pad_start = pad_end - padded
    nb = -(-(n_tok * TOP_K) // EXPERT_BLOCK) + N_EXPERTS
    n_t = n_tok // tr
    dest_seg = pl.pallas_call(
        functools.partial(_dest_kernel, n_slot=nb * EXPERT_BLOCK),
        grid_spec=pltpu.PrefetchScalarGridSpec(
            num_scalar_prefetch=1,
            grid=(N_SEG, n_t),
            in_specs=[pl.BlockSpec((2 * TOP_K, tr), lambda j, i, ps: (0, i))],
            out_specs=pl.BlockSpec((TOP_K, tr), lambda j, i, ps: (0, j * n_t + i)),
        ),
        out_shape=jax.ShapeDtypeStruct((TOP_K, N_SEG * n_tok), jnp.int32),
        name="slot_index",
    )(pad_start.astype(jnp.int32), ri)
    block_start = (jnp.concatenate([pad_start, pad_end[-1:]]) // EXPERT_BLOCK).astype(jnp.int32)
    n_used = block_start[-1:]
    return dest_seg, block_start, n_used, nb


def kernel(x_prompt, x_sample, state_conv, state_gla, w_in, w_dw, b_dw, ln_conv_g, ln_conv_b,
           w_gate_lr, b_gate, gla_norm_g, w_o, ln1_g, ln1_b, w_router, b_router, w_gu, b_gu,
           w_down, b_down, ln2_g, ln2_b):
    assert w_in.shape[0] == DEPTH
    l = 0
    p = _prep_params(w_in[l], w_dw[l], b_dw[l], ln_conv_g[l], ln_conv_b[l], w_gate_lr[l],
                     b_gate[l], gla_norm_g[l], w_o[l], ln1_g[l], ln1_b[l], w_router[l],
                     b_router[l])
    bp, tp, _ = x_prompt.shape
    bs, ts, _ = x_sample.shape
    zc = jnp.zeros((bp, CONV_W - 1, CONV_CH), F32)
    zs = jnp.zeros((bp, GLA_HEADS, GLA_DK, GLA_DV), F32)
    n_prompt = bp * tp
    n_tok = n_prompt + bs * ts
    x1p, x1, lg, conv_p, gla_p = _mixer(x_prompt, zc, zs, p, tm=PROMPT_TILE, n_tok_total=n_tok,
                                        tok_offset=0)
    x1p, x1, lg, conv_s, gla_s = _mixer(x_sample, state_conv[l], state_gla[l], p, tm=ts,
                                        n_tok_total=n_tok, tok_offset=n_prompt,
                                        token_bufs=(x1p, x1, lg),
                                        sb=SAMPLE_STREAMS if bs % SAMPLE_STREAMS == 0 else 1)

    tr = ROUTE_TILE if n_tok % ROUTE_TILE == 0 else bs * ts
    ri, rg, counts = _route(lg, tr=tr)
    dest_seg, block_start, n_used, nb = _moe_layout(ri, counts, tr=n_tok)
    n_slot = nb * EXPERT_BLOCK
    xs = _sc_scatter_rows(x1p.reshape(N_SEG * n_tok, SC_SEG), dest_seg, N_SEG * n_slot)
    yb = _ffn(block_start, n_used, xs.reshape(N_SEG, n_slot, SC_SEG), w_gu[l], b_gu[l], w_down[l],
              b_down[l])
    tf_last = bs * ts
    n_blk = n_prompt // tf_last
    last_blk = min(LAST_PIECE_BLOCKS, n_blk // 2)
    body_tok = (n_blk - last_blk) * tf_last
    tf_body = 2 * tf_last if body_tok % (2 * tf_last) == 0 else tf_last
    body_blk = body_tok // tf_body
    n_body = min(COMBINE_PIECES - 1, body_blk)
    bounds = [(p * body_blk // n_body) * tf_body for p in range(n_body)] + [body_tok, n_tok]
    dest_kjt = dest_seg.reshape(TOP_K, N_SEG, n_tok)
    yp = None
    for t0, t1 in zip(bounds[:-1], bounds[1:]):
        g = _sc_gather_rows(yb.reshape(N_SEG * n_slot, SC_SEG),
                            dest_kjt[:, :, t0:t1].reshape(1, -1))
        outs = _combine(x1, g.reshape(TOP_K, N_SEG, t1 - t0, SC_SEG), rg,
                        ln2_g[l].reshape(1, -1), ln2_b[l].reshape(1, -1), tok0=t0,
                        n_prompt=n_prompt, tf=tf_last if t1 == n_tok else tf_body, yp_buf=yp)
        yp = outs[0]
    ys = outs[1]
    return (yp.reshape(bp, tp, D_MODEL), ys.reshape(bs, ts, D_MODEL),
            conv_p[None], gla_p[None], conv_s[None], gla_s[None])
```

```python
import functools

import jax
import jax.numpy as jnp
from jax import lax
from jax.experimental import pallas as pl
from jax.experimental.pallas import tpu as pltpu
from jax.experimental.pallas import tpu_sc as plsc

F32 = jnp.float32
BF16 = jnp.bfloat16

LANES = 128
D_MODEL = 1024
CHUNK = 64
SUB = 16
DIAG = 8
CONV_CH = 512
CONV_W = 31
HALO = 32
PROMPT_TILE = 1024
MAX_STATIC_TRIPS = 4
GLU_ROWS = 128
CONV_BLOCK = 64
ROUTE_TILE = 768
SAMPLE_STREAMS = 16
LAST_PIECE_BLOCKS = 4
COMBINE_PIECES = 8
TRIP_BLOCKS = 4
IN_AHEAD = TRIP_BLOCKS
IN_RING = 2 * TRIP_BLOCKS
OUT_RING = 2 * TRIP_BLOCKS
GLA_HEADS = 4
GLA_DK = 64
GLA_DV = 128
GLA_K = GLA_HEADS * GLA_DK
GLA_V = GLA_HEADS * GLA_DV
GATE_RANK = 16
GATE_PAD = 128
GATE_NORM = 16.0
MIX_W = CONV_CH + GLA_V
OFF_UV = 0
OFF_UG = OFF_UV + CONV_CH
OFF_Q = OFF_UG + CONV_CH
OFF_K = OFF_Q + GLA_K
OFF_V = OFF_K + GLA_K
OFF_G = OFF_V + GLA_V
OFF_A = OFF_G + GLA_V
IN_COLS_PAD = OFF_A + GATE_PAD
N_EXPERTS = 32
TOP_K = 4
D_FF = 1024
SWIGLU_LIMIT = 7.0
SWIGLU_ALPHA = 1.702
EXPERT_BLOCK = 256
LN_EPS = 1e-5
DEPTH = 1
DN_ALPHA = (2 * DEPTH) ** 0.25
VMEM_LIMIT = 56 * 1024 * 1024
HALF = D_MODEL // 2
SC_SEG = 256
SC_WINDOW = 128
N_SEG = HALF // SC_SEG
U32 = jnp.uint32


def _dot(a, b):
    return jnp.dot(a, b, preferred_element_type=F32)


def _dot_nt(a, b):
    return lax.dot_general(a, b, (((1,), (1,)), ((), ())), preferred_element_type=F32)


def _dot_tn(a, b):
    return lax.dot_general(a, b, (((0,), (0,)), ((), ())), preferred_element_type=F32)


def _layer_norm(x, g, b):
    mu = jnp.mean(x, axis=-1, keepdims=True)
    xc = x - mu
    var = jnp.mean(xc * xc, axis=-1, keepdims=True)
    return xc * lax.rsqrt(var + LN_EPS) * g + b


def _sigmoid(x):
    return 1.0 / (1.0 + jnp.exp(-x))


def _split_bf16(x):
    hi = x.astype(BF16)
    lo = (x - hi.astype(F32)).astype(BF16)
    return hi, lo


def _pack_segments(x):
    bits = pltpu.bitcast(x.astype(BF16).astype(F32), U32)
    word = (bits[:, :HALF] >> 16) | bits[:, HALF:]
    return [word[:, j * SC_SEG:(j + 1) * SC_SEG] for j in range(N_SEG)]


def _unpack_segments(segs):
    word = jnp.concatenate(segs, axis=1)
    lo = pltpu.bitcast(word << 16, F32)
    hi = pltpu.bitcast(word & jnp.uint32(0xFFFF0000), F32)
    return lo, hi


def _mixer_kernel(x_ref, cs_ref, gs_ref, w_in_ref, w_dw_ref, b_dw_ref, lncg_ref, lncb_ref,
                  wg_ref, bg_ref, gng_ref, w_o_ref, ln1g_ref, ln1b_ref, wr_ref, br_ref, *rest,
                  tm, chunk, n_alias, sb):
    (x1p_ref, x1_ref, logit_ref, conv_out_ref, gla_out_ref,
     ubuf, qbuf, kbuf, labuf, vbuf, gbuf, mixbuf, st_ref) = rest[n_alias:]
    t = pl.program_id(1)
    nt = pl.num_programs(1)
    n_chunks = tm // chunk
    n_sub = chunk // SUB
    n_rows = sb * tm
    us = HALO + tm

    @pl.when(t == 0)
    def _init():
        for s in range(sb):
            ubuf[s * us:s * us + HALO - (CONV_W - 1), :] = jnp.zeros(
                (HALO - (CONV_W - 1), CONV_CH), F32)
            ubuf[s * us + HALO - (CONV_W - 1):s * us + HALO, :] = cs_ref[s]
            for h in range(GLA_HEADS):
                st_ref[s * GLA_DV:(s + 1) * GLA_DV, h * GLA_DK:(h + 1) * GLA_DK] = gs_ref[s, h].T

    x = x_ref[...].reshape(n_rows, D_MODEL)
    xb = x.astype(BF16)

    gp = min(GLU_ROWS, tm) if sb == 1 else n_rows
    for r0 in range(0, n_rows, gp):
        hv = _dot(xb[r0:r0 + gp], w_in_ref[:, OFF_UV:OFF_UG])
        hg = _dot(xb[r0:r0 + gp], w_in_ref[:, OFF_UG:OFF_Q])
        u = hv * _sigmoid(hg)
        for s in range(sb):
            lo = max(r0, s * tm)
            hi = min(r0 + gp, (s + 1) * tm)
            if lo < hi:
                ubuf[s * us + HALO + lo - s * tm:s * us + HALO + hi - s * tm, :] = u[lo - r0:hi - r0]
    lead = HALO - (CONV_W - 1)

    def proj_q(r0, n):
        qbuf[r0:r0 + n, :] = _dot(xb[r0:r0 + n], w_in_ref[:, OFF_Q:OFF_K]) * (GLA_DK ** -0.5)

    def proj_k(r0, n):
        kbuf[r0:r0 + n, :] = _dot(xb[r0:r0 + n], w_in_ref[:, OFF_K:OFF_V])

    def proj_v(half, r0, n):
        lo = half * (GLA_V // 2)
        vbuf[r0:r0 + n, lo:lo + GLA_V // 2] = _dot(
            xb[r0:r0 + n], w_in_ref[:, OFF_V + lo:OFF_V + lo + GLA_V // 2])

    def proj_g(half, r0, n):
        lo = half * (GLA_V // 2)
        gbuf[r0:r0 + n, lo:lo + GLA_V // 2] = _dot(
            xb[r0:r0 + n], w_in_ref[:, OFF_G + lo:OFF_G + lo + GLA_V // 2])

    def proj_gate(r0, n):
        ha = _dot(xb[r0:r0 + n], w_in_ref[:, OFF_A:IN_COLS_PAD])
        a = _dot(ha.astype(BF16), wg_ref[...]) + bg_ref[...]
        labuf[r0:r0 + n, :] = ((jnp.minimum(a, 0.0) - jnp.log(1.0 + jnp.exp(-jnp.abs(a))))
                               * (1.0 / GATE_NORM))

    proj_jobs = [proj_q, proj_k, functools.partial(proj_v, 0), functools.partial(proj_v, 1),
                 functools.partial(proj_g, 0), functools.partial(proj_g, 1), proj_gate]

    cb = min(CONV_BLOCK, tm)

    def conv_block(stream, r0):
        acc = jnp.zeros((cb, CONV_CH), F32) + b_dw_ref[...]
        for r in range(8):
            rows = cb if r == 0 else cb + 8
            part = None
            for s in range(r, lead + CONV_W, 8):
                if s < lead:
                    continue
                w8 = w_dw_ref[8 * (s - lead):8 * (s - lead) + 8, :]
                u3 = ubuf[pl.ds(stream * us + r0 + (s - r), rows), :].reshape(
                    rows // 8, 8, CONV_CH)
                term = u3 * w8
                part = term if part is None else part + term
            acc = acc + part.reshape(rows, CONV_CH)[r:r + cb, :]
        cact = _layer_norm(acc, lncg_ref[...], lncb_ref[...])
        cact = cact * _sigmoid(cact)
        mixbuf[pl.ds(stream * tm + r0, cb), 0:CONV_CH] = cact.astype(BF16)

    per_trip = 4 if n_chunks % 4 == 0 else 1
    n_trips = n_chunks // per_trip
    static_trips = 2 <= n_trips <= MAX_STATIC_TRIPS and sb == 1
    trip_rows = per_trip * chunk
    loop_blocks = (tm // 2) // (cb * n_trips) if n_chunks >= 4 else 0
    pre_blocks = tm // cb - loop_blocks * n_trips
    rows_first = trip_rows if static_trips else n_rows
    pre = [(s, blk * cb) for s in range(sb) for blk in range(pre_blocks)]
    for i in range(max(len(pre), len(proj_jobs))):
        if i < len(proj_jobs):
            proj_jobs[i](0, rows_first)
        if i < len(pre):
            conv_block(*pre[i])

    lane_k = lax.broadcasted_iota(jnp.int32, (1, GLA_K), 1) // GLA_DK
    row_c = lax.broadcasted_iota(jnp.int32, (chunk, 1), 0)
    tri = (lax.broadcasted_iota(jnp.int32, (chunk, chunk), 0)
           >= lax.broadcasted_iota(jnp.int32, (chunk, chunk), 1)).astype(BF16)
    e2 = (lax.broadcasted_iota(jnp.int32, (GLA_K, GLA_V), 0) // GLA_DK
          == lax.broadcasted_iota(jnp.int32, (GLA_K, GLA_V), 1) // GLA_DV).astype(BF16)
    row_s = lax.broadcasted_iota(jnp.int32, (GLA_HEADS * chunk, 1), 0)
    blk_s = (row_s % chunk) // SUB
    same_blk = blk_s == lax.broadcasted_iota(jnp.int32, (1, LANES), 1) // SUB
    second_half = row_c % SUB >= DIAG
    row_g = lax.broadcasted_iota(jnp.int32, (1, DIAG, 1), 1)
    lane_v = lax.broadcasted_iota(jnp.int32, (1, GLA_V), 1) % GLA_DV

    def head_stack(m):
        return jnp.concatenate(
            [jnp.where(lane_k == h, m, 0.0) for h in range(GLA_HEADS)], axis=0).astype(BF16)

    def unstack(r, width):
        return jnp.concatenate(
            [r[h * chunk:(h + 1) * chunk, h * width:(h + 1) * width] for h in range(GLA_HEADS)],
            axis=1)

    def chunk_step(stream, c, carry):
        base = stream * tm + (c * chunk if isinstance(c, int) else pl.multiple_of(c * chunk, chunk))
        st_rows = slice(stream * GLA_DV, (stream + 1) * GLA_DV)
        q_c = qbuf[pl.ds(base, chunk), :]
        k_c = kbuf[pl.ds(base, chunk), :]
        v_c = vbuf[pl.ds(base, chunk), :]
        v_b = v_c.astype(BF16)
        la_c = labuf[pl.ds(base, chunk), :]
        la_hi, la_lo = _split_bf16(la_c)
        b = _dot(tri, la_hi) + _dot(tri, la_lo)
        b_last = b[chunk - 1:chunk, :]
        st = st_ref[st_rows, :]

        qe = q_c * jnp.exp(b)
        o2 = _dot_nt(head_stack(qe), st.astype(BF16))
        o = jnp.concatenate([o2[h * chunk:(h + 1) * chunk, :] for h in range(GLA_HEADS)], axis=1)

        a_off = None
        if n_sub > 1:
            r_rows = [b[0:SUB, :]] + [jnp.broadcast_to(b[SUB * i - 1:SUB * i, :], (SUB, GLA_K))
                                      for i in range(1, n_sub)]
            r_q = jnp.concatenate(r_rows, axis=0)
            q_t = jnp.where(row_c >= SUB, q_c * jnp.exp(jnp.minimum(b - r_q, 0.0)), 0.0)
            k_parts = []
            for i in range(1, n_sub):
                r_i = b[SUB * i - 1:SUB * i, :]
                k_t = jnp.where(row_c < SUB * i, k_c * jnp.exp(jnp.minimum(r_i - b, 0.0)), 0.0)
                k_parts.append(k_t)
                k_parts.append(jnp.zeros((LANES - chunk, GLA_K), F32))
            k_cat = jnp.concatenate(k_parts, axis=0).astype(BF16)
            r = _dot_nt(head_stack(q_t), k_cat)
            a_off = r[:, 0:LANES]
            for i in range(2, n_sub):
                a_off = jnp.where(blk_s == i, r[:, (i - 1) * LANES:i * LANES], a_off)

        r_b = jnp.concatenate(
            [jnp.broadcast_to(b[SUB * i + DIAG - 1:SUB * i + DIAG, :], (SUB, GLA_K))
             for i in range(n_sub)], axis=0)
        q_b = jnp.where(second_half, q_c * jnp.exp(jnp.minimum(b - r_b, 0.0)), 0.0)
        k_b = jnp.where(second_half, 0.0, k_c * jnp.exp(jnp.minimum(r_b - b, 0.0)))
        k_b = jnp.concatenate([k_b, jnp.zeros((LANES - chunk, GLA_K), F32)], axis=0).astype(BF16)
        a_half = jnp.where(same_blk, _dot_nt(head_stack(q_b), k_b), 0.0)
        a_off = a_half if a_off is None else a_off + a_half

        q3 = q_c.reshape(chunk // DIAG, DIAG, GLA_K)
        k3 = k_c.reshape(chunk // DIAG, DIAG, GLA_K)
        b3 = b.reshape(chunk // DIAG, DIAG, GLA_K)
        ps = []
        for d in range(DIAG):
            k_s = k3 if d == 0 else pltpu.roll(k3, d, axis=1)
            b_s = b3 if d == 0 else pltpu.roll(b3, d, axis=1)
            p = jnp.where(row_g >= d, q3 * k_s * jnp.exp(jnp.minimum(b3 - b_s, 0.0)), 0.0)
            ps.append(p.reshape(chunk, GLA_K).astype(BF16))
        w_all = _dot(jnp.concatenate(ps, axis=0), e2)
        a_d = jnp.zeros((chunk, GLA_V), F32)
        for d in range(DIAG):
            a_d = jnp.where(lane_v == row_c - d, w_all[d * chunk:(d + 1) * chunk, :], a_d)
        a_off = a_off + jnp.concatenate(
            [a_d[:, h * GLA_DV:(h + 1) * GLA_DV] for h in range(GLA_HEADS)], axis=0)
        o1 = _dot(a_off[:, 0:chunk].astype(BF16), v_b)
        o = o + unstack(o1, GLA_DV)

        ke = k_c * jnp.exp(b_last - b)
        r2 = _dot_tn(v_b, ke.astype(BF16))
        upd = jnp.zeros((GLA_DV, GLA_K), F32)
        for h in range(GLA_HEADS):
            upd = upd + jnp.where(lane_k == h, r2[h * GLA_DV:(h + 1) * GLA_DV, :], 0.0)
        st_ref[st_rows, :] = st * jnp.exp(b_last) + upd

        g_c = gbuf[pl.ds(base, chunk), :]
        outs = []
        for h in range(GLA_HEADS):
            oh = o[:, h * GLA_DV:(h + 1) * GLA_DV]
            ms = jnp.mean(oh * oh, axis=-1, keepdims=True)
            outs.append(oh * lax.rsqrt(ms + LN_EPS))
        on = jnp.concatenate(outs, axis=1) * gng_ref[...]
        on = on * (g_c * _sigmoid(g_c))
        mixbuf[pl.ds(base, chunk), CONV_CH:MIX_W] = on.astype(BF16)
        return carry

    def trip(stream, i, carry, extra_jobs=()):
        for j in range(max(per_trip, loop_blocks, len(extra_jobs))):
            if j < len(extra_jobs):
                extra_jobs[j]((i + 1) * trip_rows, trip_rows)
            if j < loop_blocks:
                r0 = (pre_blocks + i * loop_blocks + j) * cb
                conv_block(stream, r0 if isinstance(r0, int) else pl.multiple_of(r0, cb))
            if j < per_trip:
                carry = chunk_step(stream, i * per_trip + j, carry)
        return carry

    def merge_rows(r0, n):
        y = _dot(mixbuf[r0:r0 + n, :], w_o_ref[...])
        x1 = _layer_norm(DN_ALPHA * x[r0:r0 + n, :] + y, ln1g_ref[...], ln1b_ref[...])
        x1_ref[r0:r0 + n, :] = x1
        for j, seg in enumerate(_pack_segments(x1)):
            x1p_ref[j, r0:r0 + n, :] = seg
        x_hi, x_lo = _split_bf16(x1)
        lg = _dot(x_hi, wr_ref[...])
        logit_ref[r0:r0 + n, :] = (lg[:, 0:LANES] + lg[:, LANES:2 * LANES]
                                   + _dot(x_lo, wr_ref[:, 0:LANES]) + br_ref[...])

    if static_trips:
        for i in range(n_trips):
            trip(0, i, 0, tuple(proj_jobs) if i + 1 < n_trips else ())
            merge_rows(i * trip_rows, trip_rows)
    else:
        for s in range(sb):
            if n_trips == 1:
                trip(s, 0, 0)
            else:
                lax.fori_loop(0, n_trips, functools.partial(trip, s), 0)
        merge_rows(0, n_rows)

    tails = []
    for s in range(sb):
        tail = ubuf[s * us + tm:s * us + tm + HALO, :]
        ubuf[s * us:s * us + HALO, :] = tail
        tails.append(tail)

    @pl.when(t == nt - 1)
    def _state_out():
        for s in range(sb):
            conv_out_ref[s] = tails[s][HALO - (CONV_W - 1):, :]
            st = st_ref[s * GLA_DV:(s + 1) * GLA_DV, :]
            for h in range(GLA_HEADS):
                gla_out_ref[s, h] = st[:, h * GLA_DK:(h + 1) * GLA_DK].T


def _mixer(x, conv_state, gla_state, p, *, tm, n_tok_total, tok_offset, token_bufs=None, sb=1):
    bsz, seq, _ = x.shape
    chunk = min(CHUNK, seq)
    rows = sb * tm
    assert seq % tm == 0 and tm % chunk == 0 and chunk % SUB == 0 and tok_offset % rows == 0
    assert bsz % sb == 0 and (sb == 1 or seq == tm)
    nt = seq // tm
    blk0 = tok_offset // rows
    full = lambda shape: pl.BlockSpec(shape, lambda b, t: (0,) * len(shape))
    alias = () if token_bufs is None else tuple(token_bufs)
    kern = functools.partial(_mixer_kernel, tm=tm, chunk=chunk, n_alias=len(alias), sb=sb)
    n_in = 16
    return pl.pallas_call(
        kern,
        grid=(bsz // sb, nt),
        input_output_aliases={n_in + i: i for i in range(len(alias))},
        in_specs=[
            pl.BlockSpec((sb, tm, D_MODEL), lambda b, t: (b, t, 0)),
            pl.BlockSpec((sb, CONV_W - 1, CONV_CH), lambda b, t: (b, 0, 0)),
            pl.BlockSpec((sb, GLA_HEADS, GLA_DK, GLA_DV), lambda b, t: (b, 0, 0, 0)),
            full((D_MODEL, IN_COLS_PAD)),
            full((8 * CONV_W, CONV_CH)),
            full((1, CONV_CH)),
            full((1, CONV_CH)),
            full((1, CONV_CH)),
            full((GATE_PAD, GLA_K)),
            full((1, GLA_K)),
            full((1, GLA_V)),
            full((MIX_W, D_MODEL)),
            full((1, D_MODEL)),
            full((1, D_MODEL)),
            full((D_MODEL, 2 * LANES)),
            full((1, LANES)),
        ] + [pl.BlockSpec(memory_space=pl.ANY)] * len(alias),
        out_specs=[
            pl.BlockSpec((N_SEG, rows, SC_SEG), lambda b, t: (0, blk0 + b * nt + t, 0)),
            pl.BlockSpec((rows, D_MODEL), lambda b, t: (blk0 + b * nt + t, 0)),
            pl.BlockSpec((rows, LANES), lambda b, t: (blk0 + b * nt + t, 0)),
            pl.BlockSpec((sb, CONV_W - 1, CONV_CH), lambda b, t: (b, 0, 0)),
            pl.BlockSpec((sb, GLA_HEADS, GLA_DK, GLA_DV), lambda b, t: (b, 0, 0, 0)),
        ],
        out_shape=[
            jax.ShapeDtypeStruct((N_SEG, n_tok_total, SC_SEG), U32),
            jax.ShapeDtypeStruct((n_tok_total, D_MODEL), F32),
            jax.ShapeDtypeStruct((n_tok_total, LANES), F32),
            jax.ShapeDtypeStruct((bsz, CONV_W - 1, CONV_CH), F32),
            jax.ShapeDtypeStruct((bsz, GLA_HEADS, GLA_DK, GLA_DV), F32),
        ],
        scratch_shapes=[
            pltpu.VMEM((sb * (HALO + tm), CONV_CH), F32),
            pltpu.VMEM((rows, GLA_K), F32),
            pltpu.VMEM((rows, GLA_K), F32),
            pltpu.VMEM((rows, GLA_K), F32),
            pltpu.VMEM((rows, GLA_V), F32),
            pltpu.VMEM((rows, GLA_V), F32),
            pltpu.VMEM((rows, MIX_W), BF16),
            pltpu.VMEM((sb * GLA_DV, GLA_K), F32),
        ],
        compiler_params=pltpu.CompilerParams(
            dimension_semantics=("arbitrary", "arbitrary"),
            vmem_limit_bytes=VMEM_LIMIT),
        name="mixer",
    )(x, conv_state, gla_state, p["w_in"], p["w_dw"], p["b_dw"], p["ln_conv_g"], p["ln_conv_b"],
      p["w_gate"], p["b_gate"], p["gla_norm_g"], p["w_o"], p["ln1_g"], p["ln1_b"],
      p["w_router"], p["b_router"], *alias)


def _route_kernel(lg_ref, ri_ref, rg_ref, cnt_ref, carry_ref, *, tr):
    i = pl.program_id(0)

    @pl.when(i == 0)
    def _init():
        carry_ref[...] = jnp.zeros((N_EXPERTS, LANES), F32)

    l = lg_ref[...].T[0:N_EXPERTS, :]
    row = lax.broadcasted_iota(jnp.int32, (N_EXPERTS, tr), 0)
    hots, vals, idxs = [], [], []
    for _ in range(TOP_K):
        m = jnp.max(l, axis=0, keepdims=True)
        ik = jnp.min(jnp.where(l == m, row, N_EXPERTS), axis=0, keepdims=True)
        hot = row == ik
        hots.append(hot)
        vals.append(m)
        idxs.append(ik)
        l = jnp.where(hot, -jnp.inf, l)
    es = [jnp.exp(v - vals[0]) for v in vals]
    denom = es[0] + es[1] + es[2] + es[3]
    member = jnp.zeros((N_EXPERTS, tr), F32)
    for hot in hots:
        member = member + hot.astype(F32)
    before = (lax.broadcasted_iota(jnp.int32, (tr, tr), 0)
              < lax.broadcasted_iota(jnp.int32, (tr, tr), 1)).astype(BF16)
    cum = _dot(member.astype(BF16), before) + carry_ref[:, 0:1]
    ranks = [jnp.sum(jnp.where(hot, cum, 0.0), axis=0, keepdims=True).astype(jnp.int32)
             for hot in hots]
    ri_ref[...] = jnp.concatenate(idxs + ranks, axis=0)
    gates = jnp.concatenate([e / denom for e in es] + [jnp.zeros((LANES - TOP_K, tr), F32)], axis=0)
    rg_ref[...] = gates.T
    carry_ref[...] = carry_ref[...] + jnp.sum(member, axis=1, keepdims=True)

    @pl.when(i == pl.num_programs(0) - 1)
    def _fin():
        cnt_ref[...] = carry_ref[...]


def _route(logits, *, tr):
    n_tok = logits.shape[0]
    assert n_tok % tr == 0
    return pl.pallas_call(
        functools.partial(_route_kernel, tr=tr),
        grid=(n_tok // tr,),
        in_specs=[pl.BlockSpec((tr, LANES), lambda i: (i, 0))],
        out_specs=[pl.BlockSpec((2 * TOP_K, tr), lambda i: (0, i)),
                   pl.BlockSpec((tr, LANES), lambda i: (i, 0)),
                   pl.BlockSpec((N_EXPERTS, LANES), lambda i: (0, 0))],
        out_shape=[jax.ShapeDtypeStruct((2 * TOP_K, n_tok), jnp.int32),
                   jax.ShapeDtypeStruct((n_tok, LANES), F32),
                   jax.ShapeDtypeStruct((N_EXPERTS, LANES), F32)],
        scratch_shapes=[pltpu.VMEM((N_EXPERTS, LANES), F32)],
        compiler_params=pltpu.CompilerParams(dimension_semantics=("arbitrary",)),
        name="route",
    )(logits)


def _ffn_kernel(gs_ref, nu_ref, xs_hbm, wgu_ref, bgu_ref, wdn_ref, bdn_ref, yb_hbm,
                wgu_bf, wdn_bf, xin, yout, in_sem, out_sem):
    e = pl.program_id(0)
    n_used = nu_ref[0]
    g_first = gs_ref[e]
    g_end = gs_ref[e + 1]

    def in_copy(g, slot):
        rows = pl.ds(pl.multiple_of(g * EXPERT_BLOCK, EXPERT_BLOCK), EXPERT_BLOCK)
        return pltpu.make_async_copy(xs_hbm.at[:, rows, :], xin.at[slot], in_sem.at[slot])

    def out_copy(g, slot):
        rows = pl.ds(pl.multiple_of(g * EXPERT_BLOCK, EXPERT_BLOCK), EXPERT_BLOCK)
        return pltpu.make_async_copy(yout.at[slot], yb_hbm.at[:, rows, :], out_sem.at[slot])

    @pl.when(e == 0)
    def _prime():
        for g in range(IN_AHEAD):
            @pl.when(g < n_used)
            def _():
                in_copy(g, g).start()

    def cast_weights():
        wgu_bf[...] = wgu_ref[0].astype(BF16)
        wdn_bf[...] = wdn_ref[0].astype(BF16)

    def blocks(g0, n, with_cast=False):
        for j in range(n):
            g = g0 + j
            in_copy(g, g % IN_RING).wait()
        for j in range(n):
            g = g0 + IN_AHEAD + j

            @pl.when(g < n_used)
            def _prefetch():
                in_copy(g, g % IN_RING).start()
        for j in range(n):
            g = g0 + j

            @pl.when(g >= OUT_RING)
            def _free_out():
                out_copy(g - OUT_RING, g % OUT_RING).wait()
        if with_cast:
            cast_weights()
        for j in range(n):
            g = g0 + j
            x_lo, x_hi = _unpack_segments([xin[g % IN_RING, s] for s in range(N_SEG)])
            x_bf = jnp.concatenate([x_lo.astype(BF16), x_hi.astype(BF16)], axis=1)
            hgu = _dot(x_bf, wgu_bf[...]) + bgu_ref[0]
            gate = jnp.minimum(hgu[:, :D_FF], SWIGLU_LIMIT)
            up = jnp.clip(hgu[:, D_FF:], -SWIGLU_LIMIT, SWIGLU_LIMIT)
            act = (up + 1.0) * gate * _sigmoid(SWIGLU_ALPHA * gate)
            yb = _dot(act.astype(BF16), wdn_bf[...]) + bdn_ref[0]
            for s, seg in enumerate(_pack_segments(yb)):
                yout[g % OUT_RING, s] = seg
        for j in range(n):
            g = g0 + j
            out_copy(g, g % OUT_RING).start()

    n_blocks = g_end - g_first
    n_full = n_blocks // TRIP_BLOCKS

    @pl.when((n_full == 0) & (n_blocks > 0))
    def _cast_only():
        cast_weights()

    @pl.when(n_full >= 1)
    def _first_trip():
        blocks(g_first, TRIP_BLOCKS, with_cast=True)

    def full_trip(i, carry):
        blocks(g_first + TRIP_BLOCKS * i, TRIP_BLOCKS)
        return carry

    lax.fori_loop(1, n_full, full_trip, 0)
    done = n_full * TRIP_BLOCKS
    size = TRIP_BLOCKS // 2
    while size >= 1:
        @pl.when((n_blocks // size) % 2 == 1)
        def _rest(done=done, size=size):
            blocks(g_first + done, size)
        done = done + jnp.where((n_blocks // size) % 2 == 1, size, 0)
        size //= 2

    @pl.when(e == pl.num_programs(0) - 1)
    def _drain():
        for back in range(OUT_RING, 0, -1):
            @pl.when(n_used >= back)
            def _():
                out_copy(n_used - back, (n_used - back) % OUT_RING).wait()


def _ffn(block_start, n_used, xs, w_gu, b_gu, w_down, b_down):
    n_rows = xs.shape[1]
    grid_spec = pltpu.PrefetchScalarGridSpec(
        num_scalar_prefetch=2,
        grid=(N_EXPERTS,),
        in_specs=[
            pl.BlockSpec(memory_space=pl.ANY),
            pl.BlockSpec((1, D_MODEL, 2 * D_FF), lambda e, gs, nu: (e, 0, 0)),
            pl.BlockSpec((1, 1, 2 * D_FF), lambda e, gs, nu: (e, 0, 0)),
            pl.BlockSpec((1, D_FF, D_MODEL), lambda e, gs, nu: (e, 0, 0)),
            pl.BlockSpec((1, 1, D_MODEL), lambda e, gs, nu: (e, 0, 0)),
        ],
        out_specs=pl.BlockSpec(memory_space=pl.ANY),
        scratch_shapes=[pltpu.VMEM((D_MODEL, 2 * D_FF), BF16),
                        pltpu.VMEM((D_FF, D_MODEL), BF16),
                        pltpu.VMEM((IN_RING, N_SEG, EXPERT_BLOCK, SC_SEG), U32),
                        pltpu.VMEM((OUT_RING, N_SEG, EXPERT_BLOCK, SC_SEG), U32),
                        pltpu.SemaphoreType.DMA((IN_RING,)),
                        pltpu.SemaphoreType.DMA((OUT_RING,))],
    )
    return pl.pallas_call(
        _ffn_kernel,
        grid_spec=grid_spec,
        out_shape=jax.ShapeDtypeStruct((N_SEG, n_rows, SC_SEG), U32),
        compiler_params=pltpu.CompilerParams(
            dimension_semantics=("arbitrary",), vmem_limit_bytes=VMEM_LIMIT),
        name="expert_ffn",
    )(block_start, n_used, xs, w_gu, b_gu.reshape(N_EXPERTS, 1, 2 * D_FF), w_down,
      b_down.reshape(N_EXPERTS, 1, D_MODEL))


def _combine_kernel(x1_ref, g_ref, rg_ref, ln2g_ref, ln2b_ref, *rest, blk0, n_prompt_blocks,
                    n_alias, with_sample):
    yp_ref = rest[n_alias]
    i = blk0 + pl.program_id(0)
    m_lo = jnp.zeros((x1_ref.shape[0], HALF), F32)
    m_hi = jnp.zeros((x1_ref.shape[0], HALF), F32)
    for k in range(TOP_K):
        g_lo, g_hi = _unpack_segments([g_ref[k, j] for j in range(N_SEG)])
        gate = rg_ref[:, k:k + 1]
        m_lo = m_lo + gate * g_lo
        m_hi = m_hi + gate * g_hi
    z = DN_ALPHA * x1_ref[...] + jnp.concatenate([m_lo, m_hi], axis=1)
    y = _layer_norm(z, ln2g_ref[...], ln2b_ref[...])

    if with_sample:
        ys_ref = rest[n_alias + 1]

        @pl.when(i < n_prompt_blocks)
        def _p():
            yp_ref[...] = y

        @pl.when(i >= n_prompt_blocks)
        def _s():
            ys_ref[...] = y
    else:
        yp_ref[...] = y


def _combine(x1, g, rg, ln2_g, ln2_b, *, tok0, n_prompt, tf, yp_buf=None):
    n_tok = x1.shape[0]
    piece = g.shape[2]
    n_sample = n_tok - n_prompt
    assert n_prompt % tf == 0 and tok0 % tf == 0 and piece % tf == 0
    npb = n_prompt // tf
    blk0 = tok0 // tf
    with_sample = tok0 + piece > n_prompt
    assert not with_sample or (tok0 + piece == n_tok and n_sample % tf == 0)
    alias = () if yp_buf is None else (yp_buf,)
    out_specs = [pl.BlockSpec((tf, D_MODEL), lambda i: (jnp.minimum(blk0 + i, npb - 1), 0))]
    out_shape = [jax.ShapeDtypeStruct((n_prompt, D_MODEL), F32)]
    if with_sample:
        out_specs.append(pl.BlockSpec((tf, D_MODEL), lambda i: (jnp.maximum(blk0 + i - npb, 0), 0)))
        out_shape.append(jax.ShapeDtypeStruct((n_sample, D_MODEL), F32))
    n_in = 5
    return pl.pallas_call(
        functools.partial(_combine_kernel, blk0=blk0, n_prompt_blocks=npb, n_alias=len(alias),
                          with_sample=with_sample),
        grid=(piece // tf,),
        input_output_aliases={n_in + i: i for i in range(len(alias))},
        in_specs=[pl.BlockSpec((tf, D_MODEL), lambda i: (blk0 + i, 0)),
                  pl.BlockSpec((TOP_K, N_SEG, tf, SC_SEG), lambda i: (0, 0, i, 0)),
                  pl.BlockSpec((tf, LANES), lambda i: (blk0 + i, 0)),
                  pl.BlockSpec((1, D_MODEL), lambda i: (0, 0)),
                  pl.BlockSpec((1, D_MODEL), lambda i: (0, 0))]
        + [pl.BlockSpec(memory_space=pl.ANY)] * len(alias),
        out_specs=out_specs,
        out_shape=out_shape,
        compiler_params=pltpu.CompilerParams(
            dimension_semantics=("arbitrary",), vmem_limit_bytes=VMEM_LIMIT),
        name="combine",
    )(x1, g, rg, ln2_g, ln2_b, *alias)


def _sc_mesh():
    return plsc.VectorSubcoreMesh(core_axis_name="c", subcore_axis_name="s")


def _sc_scatter_rows(src, dest, n_out):
    n_src, width = src.shape
    n_k = dest.shape[0]
    assert width == SC_SEG and n_src % SC_WINDOW == 0 and dest.shape[1] == n_src

    @pl.kernel(out_type=jax.ShapeDtypeStruct((n_out, width), src.dtype), mesh=_sc_mesh(),
               scratch_types=[pltpu.SemaphoreType.DMA((n_k,))])
    def scatter_kernel(src_hbm, idx_hbm, out_hbm, sems):
        def body(src_vmem, idx_vmem):
            copies = [pltpu.async_copy(src_vmem, out_hbm.at[idx_vmem.at[k]], sems.at[k])
                      for k in range(n_k)]
            for copy in copies:
                copy.wait()

        pltpu.emit_pipeline(
            body, grid=(n_src // SC_WINDOW,),
            in_specs=[pl.BlockSpec((SC_WINDOW, width), lambda i: (i, 0)),
                      pl.BlockSpec((n_k, SC_WINDOW), lambda i: (0, i))],
            out_specs=[],
            core_axis_name=("c", "s"),
            dimension_semantics=(pltpu.PARALLEL,),
        )(src_hbm, idx_hbm)

    return scatter_kernel(src, dest)


def _sc_gather_rows(src, idx):
    n_out = idx.shape[1]
    width = src.shape[1]
    assert width == SC_SEG and n_out % SC_WINDOW == 0

    @pl.kernel(out_type=jax.ShapeDtypeStruct((n_out, width), src.dtype), mesh=_sc_mesh(),
               scratch_types=[])
    def gather_kernel(src_hbm, idx_hbm, out_hbm):
        def body(idx_vmem, out_vmem):
            pltpu.sync_copy(src_hbm.at[idx_vmem.at[0]], out_vmem)

        pltpu.emit_pipeline(
            body, grid=(n_out // SC_WINDOW,),
            in_specs=[pl.BlockSpec((1, SC_WINDOW), lambda i: (0, i))],
            out_specs=[pl.BlockSpec((SC_WINDOW, width), lambda i: (i, 0))],
            core_axis_name=("c", "s"),
            dimension_semantics=(pltpu.PARALLEL,),
        )(idx_hbm, out_hbm)

    return gather_kernel(src, idx)


def _prep_params(w_in, w_dw, b_dw, ln_conv_g, ln_conv_b, w_gate_lr, b_gate, gla_norm_g, w_o,
                 ln1_g, ln1_b, w_router, b_router):
    row = lambda v: v.reshape(1, -1).astype(F32)
    w_in_p = jnp.pad(w_in, ((0, 0), (0, IN_COLS_PAD - w_in.shape[1]))).astype(BF16)
    w_gate = jnp.pad(w_gate_lr, ((0, GATE_PAD - GATE_RANK), (0, 0))).astype(BF16)
    wr_hi = w_router.astype(BF16)
    wr_lo = (w_router - wr_hi.astype(F32)).astype(BF16)
    padr = lambda m: jnp.pad(m, ((0, 0), (0, LANES - N_EXPERTS)))
    return {
        "w_in": w_in_p, "w_dw": jnp.repeat(w_dw, 8, axis=0), "b_dw": row(b_dw), "ln_conv_g": row(ln_conv_g),
        "ln_conv_b": row(ln_conv_b), "w_gate": w_gate, "b_gate": row(b_gate),
        "gla_norm_g": row(gla_norm_g), "w_o": w_o.astype(BF16), "ln1_g": row(ln1_g),
        "ln1_b": row(ln1_b), "w_router": jnp.concatenate([padr(wr_hi), padr(wr_lo)], axis=1),
        "b_router": jnp.pad(row(b_router), ((0, 0), (0, LANES - N_EXPERTS))),
    }


def _dest_kernel(ps_ref, ri_ref, out_ref, *, n_slot):
    idx = ri_ref[0:TOP_K, :]
    dest = ri_ref[TOP_K:2 * TOP_K, :] + pl.program_id(0) * n_slot
    for e in range(N_EXPERTS):
        dest = dest + jnp.where(idx == e, ps_ref[e], 0)
    out_ref[...] = dest


def _moe_layout(ri, counts, *, tr):
    n_tok = ri.shape[1]
    cnt = counts[:, 0].astype(jnp.int32)
    padded = (cnt + EXPERT_BLOCK - 1) // EXPERT_BLOCK * EXPERT_BLOCK
    pad_end = jnp.cumsum(padded)
    pad_start = pad_end - padded
    nb = -(-(n_tok * TOP_K) // EXPERT_BLOCK) + N_EXPERTS
    n_t = n_tok // tr
    dest_seg = pl.pallas_call(
        functools.partial(_dest_kernel, n_slot=nb * EXPERT_BLOCK),
        grid_spec=pltpu.PrefetchScalarGridSpec(
            num_scalar_prefetch=1,
            grid=(N_SEG, n_t),
            in_specs=[pl.BlockSpec((2 * TOP_K, tr), lambda j, i, ps: (0, i))],
            out_specs=pl.BlockSpec((TOP_K, tr), lambda j, i, ps: (0, j * n_t + i)),
        ),
        out_shape=jax.ShapeDtypeStruct((TOP_K, N_SEG * n_tok), jnp.int32),
        name="slot_index",
    )(pad_start.astype(jnp.int32), ri)
    block_start = (jnp.concatenate([pad_start, pad_end[-1:]]) // EXPERT_BLOCK).astype(jnp.int32)
    n_used = block_start[-1:]
    return dest_seg, block_start, n_used, nb


def kernel(x_prompt, x_sample, state_conv, state_gla, w_in, w_dw, b_dw, ln_conv_g, ln_conv_b,
           w_gate_lr, b_gate, gla_norm_g, w_o, ln1_g, ln1_b, w_router, b_router, w_gu, b_gu,
           w_down, b_down, ln2_g, ln2_b):
    assert w_in.shape[0] == DEPTH
    l = 0
    p = _prep_params(w_in[l], w_dw[l], b_dw[l], ln_conv_g[l], ln_conv_b[l], w_gate_lr[l],
                     b_gate[l], gla_norm_g[l], w_o[l], ln1_g[l], ln1_b[l], w_router[l],
                     b_router[l])
    bp, tp, _ = x_prompt.shape
    bs, ts, _ = x_sample.shape
    zc = jnp.zeros((bp, CONV_W - 1, CONV_CH), F32)
    zs = jnp.zeros((bp, GLA_HEADS, GLA_DK, GLA_DV), F32)
    n_prompt = bp * tp
    n_tok = n_prompt + bs * ts
    x1p, x1, lg, conv_p, gla_p = _mixer(x_prompt, zc, zs, p, tm=PROMPT_TILE, n_tok_total=n_tok,
                                        tok_offset=0)
    x1p, x1, lg, conv_s, gla_s = _mixer(x_sample, state_conv[l], state_gla[l], p, tm=ts,
                                        n_tok_total=n_tok, tok_offset=n_prompt,
                                        token_bufs=(x1p, x1, lg),
                                        sb=SAMPLE_STREAMS if bs % SAMPLE_STREAMS == 0 else 1)

    tr = ROUTE_TILE if n_tok % ROUTE_TILE == 0 else bs * ts
    ri, rg, counts = _route(lg, tr=tr)
    dest_seg, block_start, n_used, nb = _moe_layout(ri, counts, tr=n_tok)
    n_slot = nb * EXPERT_BLOCK
    xs = _sc_scatter_rows(x1p.reshape(N_SEG * n_tok, SC_SEG), dest_seg, N_SEG * n_slot)
    yb = _ffn(block_start, n_used, xs.reshape(N_SEG, n_slot, SC_SEG), w_gu[l], b_gu[l], w_down[l],
              b_down[l])
    tf_last = bs * ts
    n_blk = n_prompt // tf_last
    last_blk = min(LAST_PIECE_BLOCKS, n_blk // 2)
    body_tok = (n_blk - last_blk) * tf_last
    tf_body = 2 * tf_last if body_tok % (2 * tf_last) == 0 else tf_last
    body_blk = body_tok // tf_body
    n_body = min(COMBINE_PIECES - 1, body_blk)
    bounds = [(p * body_blk // n_body) * tf_body for p in range(n_body)] + [body_tok, n_tok]
    dest_kjt = dest_seg.reshape(TOP_K, N_SEG, n_tok)
    yp = None
    for t0, t1 in zip(bounds[:-1], bounds[1:]):
        g = _sc_gather_rows(yb.reshape(N_SEG * n_slot, SC_SEG),
                            dest_kjt[:, :, t0:t1].reshape(1, -1))
        outs = _combine(x1, g.reshape(TOP_K, N_SEG, t1 - t0, SC_SEG), rg,
                        ln2_g[l].reshape(1, -1), ln2_b[l].reshape(1, -1), tok0=t0,
                        n_prompt=n_prompt, tf=tf_last if t1 == n_tok else tf_body, yp_buf=yp)
        yp = outs[0]
    ys = outs[1]
    return (yp.reshape(bp, tp, D_MODEL), ys.reshape(bs, ts, D_MODEL),
            conv_p[None], gla_p[None], conv_s[None], gla_s[None])
```

```python
import functools

import jax
import jax.numpy as jnp
from jax import lax
from jax.experimental import pallas as pl
from jax.experimental.pallas import tpu as pltpu
from jax.experimental.pallas import tpu_sc as plsc

F32 = jnp.float32
BF16 = jnp.bfloat16

LANES = 128
D_MODEL = 1024
CHUNK = 64
SUB = 16
DIAG = 8
CONV_CH = 512
CONV_W = 31
HALO = 32
PROMPT_TILE = 1024
MAX_STATIC_TRIPS = 4
GLU_ROWS = 128
CONV_BLOCK = 64
ROUTE_TILE = 768
SAMPLE_STREAMS = 16
LAST_PIECE_BLOCKS = 4
COMBINE_PIECES = 8
TRIP_BLOCKS = 4
IN_AHEAD = TRIP_BLOCKS
IN_RING = 2 * TRIP_BLOCKS
OUT_RING = 2 * TRIP_BLOCKS
GLA_HEADS = 4
GLA_DK = 64
GLA_DV = 128
GLA_K = GLA_HEADS * GLA_DK
GLA_V = GLA_HEADS * GLA_DV
GATE_RANK = 16
GATE_PAD = 128
GATE_NORM = 16.0
MIX_W = CONV_CH + GLA_V
OFF_UV = 0
OFF_UG = OFF_UV + CONV_CH
OFF_Q = OFF_UG + CONV_CH
OFF_K = OFF_Q + GLA_K
OFF_V = OFF_K + GLA_K
OFF_G = OFF_V + GLA_V
OFF_A = OFF_G + GLA_V
IN_COLS_PAD = OFF_A + GATE_PAD
N_EXPERTS = 32
TOP_K = 4
D_FF = 1024
SWIGLU_LIMIT = 7.0
SWIGLU_ALPHA = 1.702
EXPERT_BLOCK = 256
LN_EPS = 1e-5
DEPTH = 1
DN_ALPHA = (2 * DEPTH) ** 0.25
VMEM_LIMIT = 56 * 1024 * 1024
HALF = D_MODEL // 2
SC_SEG = 256
SC_WINDOW = 128
N_SEG = HALF // SC_SEG
U32 = jnp.uint32


def _dot(a, b):
    return jnp.dot(a, b, preferred_element_type=F32)


def _dot_nt(a, b):
    return lax.dot_general(a, b, (((1,), (1,)), ((), ())), preferred_element_type=F32)


def _dot_tn(a, b):
    return lax.dot_general(a, b, (((0,), (0,)), ((), ())), preferred_element_type=F32)


def _layer_norm(x, g, b):
    mu = jnp.mean(x, axis=-1, keepdims=True)
    xc = x - mu
    var = jnp.mean(xc * xc, axis=-1, keepdims=True)
    return xc * lax.rsqrt(var + LN_EPS) * g + b


def _sigmoid(x):
    return 1.0 / (1.0 + jnp.exp(-x))


def _split_bf16(x):
    hi = x.astype(BF16)
    lo = (x - hi.astype(F32)).astype(BF16)
    return hi, lo


def _pack_segments(x):
    bits = pltpu.bitcast(x.astype(BF16).astype(F32), U32)
    word = (bits[:, :HALF] >> 16) | bits[:, HALF:]
    return [word[:, j * SC_SEG:(j + 1) * SC_SEG] for j in range(N_SEG)]


def _unpack_segments(segs):
    word = jnp.concatenate(segs, axis=1)
    lo = pltpu.bitcast(word << 16, F32)
    hi = pltpu.bitcast(word & jnp.uint32(0xFFFF0000), F32)
    return lo, hi


def _mixer_kernel(x_ref, cs_ref, gs_ref, w_in_ref, w_dw_ref, b_dw_ref, lncg_ref, lncb_ref,
                  wg_ref, bg_ref, gng_ref, w_o_ref, ln1g_ref, ln1b_ref, wr_ref, br_ref, *rest,
                  tm, chunk, n_alias, sb):
    (x1p_ref, x1_ref, logit_ref, conv_out_ref, gla_out_ref,
     ubuf, qbuf, kbuf, labuf, vbuf, gbuf, mixbuf, st_ref) = rest[n_alias:]
    t = pl.program_id(1)
    nt = pl.num_programs(1)
    n_chunks = tm // chunk
    n_sub = chunk // SUB
    n_rows = sb * tm
    us = HALO + tm

    @pl.when(t == 0)
    def _init():
        for s in range(sb):
            ubuf[s * us:s * us + HALO - (CONV_W - 1), :] = jnp.zeros(
                (HALO - (CONV_W - 1), CONV_CH), F32)
            ubuf[s * us + HALO - (CONV_W - 1):s * us + HALO, :] = cs_ref[s]
            for h in range(GLA_HEADS):
                st_ref[s * GLA_DV:(s + 1) * GLA_DV, h * GLA_DK:(h + 1) * GLA_DK] = gs_ref[s, h].T

    x = x_ref[...].reshape(n_rows, D_MODEL)
    xb = x.astype(BF16)

    gp = min(GLU_ROWS, tm) if sb == 1 else n_rows
    for r0 in range(0, n_rows, gp):
        hv = _dot(xb[r0:r0 + gp], w_in_ref[:, OFF_UV:OFF_UG])
        hg = _dot(xb[r0:r0 + gp], w_in_ref[:, OFF_UG:OFF_Q])
        u = hv * _sigmoid(hg)
        for s in range(sb):
            lo = max(r0, s * tm)
            hi = min(r0 + gp, (s + 1) * tm)
            if lo < hi:
                ubuf[s * us + HALO + lo - s * tm:s * us + HALO + hi - s * tm, :] = u[lo - r0:hi - r0]
    lead = HALO - (CONV_W - 1)

    def proj_q(r0, n):
        qbuf[r0:r0 + n, :] = _dot(xb[r0:r0 + n], w_in_ref[:, OFF_Q:OFF_K]) * (GLA_DK ** -0.5)

    def proj_k(r0, n):
        kbuf[r0:r0 + n, :] = _dot(xb[r0:r0 + n], w_in_ref[:, OFF_K:OFF_V])

    def proj_v(half, r0, n):
        lo = half * (GLA_V // 2)
        vbuf[r0:r0 + n, lo:lo + GLA_V // 2] = _dot(
            xb[r0:r0 + n], w_in_ref[:, OFF_V + lo:OFF_V + lo + GLA_V // 2])

    def proj_g(half, r0, n):
        lo = half * (GLA_V // 2)
        gbuf[r0:r0 + n, lo:lo + GLA_V // 2] = _dot(
            xb[r0:r0 + n], w_in_ref[:, OFF_G + lo:OFF_G + lo + GLA_V // 2])

    def proj_gate(r0, n):
        ha = _dot(xb[r0:r0 + n], w_in_ref[:, OFF_A:IN_COLS_PAD])
        a = _dot(ha.astype(BF16), wg_ref[...]) + bg_ref[...]
        labuf[r0:r0 + n, :] = ((jnp.minimum(a, 0.0) - jnp.log(1.0 + jnp.exp(-jnp.abs(a))))
                               * (1.0 / GATE_NORM))

    proj_jobs = [proj_q, proj_k, functools.partial(proj_v, 0), functools.partial(proj_v, 1),
                 functools.partial(proj_g, 0), functools.partial(proj_g, 1), proj_gate]

    cb = min(CONV_BLOCK, tm)

    def conv_block(stream, r0):
        acc = jnp.zeros((cb, CONV_CH), F32) + b_dw_ref[...]
        for r in range(8):
            rows = cb if r == 0 else cb + 8
            part = None
            for s in range(r, lead + CONV_W, 8):
                if s < lead:
                    continue
                w8 = w_dw_ref[8 * (s - lead):8 * (s - lead) + 8, :]
                u3 = ubuf[pl.ds(stream * us + r0 + (s - r), rows), :].reshape(
                    rows // 8, 8, CONV_CH)
                term = u3 * w8
                part = term if part is None else part + term
            acc = acc + part.reshape(rows, CONV_CH)[r:r + cb, :]
        cact = _layer_norm(acc, lncg_ref[...], lncb_ref[...])
        cact = cact * _sigmoid(cact)
        mixbuf[pl.ds(stream * tm + r0, cb), 0:CONV_CH] = cact.astype(BF16)

    per_trip = 4 if n_chunks % 4 == 0 else 1
    n_trips = n_chunks // per_trip
    static_trips = 2 <= n_trips <= MAX_STATIC_TRIPS and sb == 1
    trip_rows = per_trip * chunk
    loop_blocks = (tm // 2) // (cb * n_trips) if n_chunks >= 4 else 0
    pre_blocks = tm // cb - loop_blocks * n_trips
    rows_first = trip_rows if static_trips else n_rows
    pre = [(s, blk * cb) for s in range(sb) for blk in range(pre_blocks)]
    for i in range(max(len(pre), len(proj_jobs))):
        if i < len(proj_jobs):
            proj_jobs[i](0, rows_first)
        if i < len(pre):
            conv_block(*pre[i])

    lane_k = lax.broadcasted_iota(jnp.int32, (1, GLA_K), 1) // GLA_DK
    row_c = lax.broadcasted_iota(jnp.int32, (chunk, 1), 0)
    tri = (lax.broadcasted_iota(jnp.int32, (chunk, chunk), 0)
           >= lax.broadcasted_iota(jnp.int32, (chunk, chunk), 1)).astype(BF16)
    e2 = (lax.broadcasted_iota(jnp.int32, (GLA_K, GLA_V), 0) // GLA_DK
          == lax.broadcasted_iota(jnp.int32, (GLA_K, GLA_V), 1) // GLA_DV).astype(BF16)
    row_s = lax.broadcasted_iota(jnp.int32, (GLA_HEADS * chunk, 1), 0)
    blk_s = (row_s % chunk) // SUB
    same_blk = blk_s == lax.broadcasted_iota(jnp.int32, (1, LANES), 1) // SUB
    second_half = row_c % SUB >= DIAG
    row_g = lax.broadcasted_iota(jnp.int32, (1, DIAG, 1), 1)
    lane_v = lax.broadcasted_iota(jnp.int32, (1, GLA_V), 1) % GLA_DV

    def head_stack(m):
        return jnp.concatenate(
            [jnp.where(lane_k == h, m, 0.0) for h in range(GLA_HEADS)], axis=0).astype(BF16)

    def unstack(r, width):
        return jnp.concatenate(
            [r[h * chunk:(h + 1) * chunk, h * width:(h + 1) * width] for h in range(GLA_HEADS)],
            axis=1)

    def chunk_step(stream, c, carry):
        base = stream * tm + (c * chunk if isinstance(c, int) else pl.multiple_of(c * chunk, chunk))
        st_rows = slice(stream * GLA_DV, (stream + 1) * GLA_DV)
        q_c = qbuf[pl.ds(base, chunk), :]
        k_c = kbuf[pl.ds(base, chunk), :]
        v_c = vbuf[pl.ds(base, chunk), :]
        v_b = v_c.astype(BF16)
        la_c = labuf[pl.ds(base, chunk), :]
        la_hi, la_lo = _split_bf16(la_c)
        b = _dot(tri, la_hi) + _dot(tri, la_lo)
        b_last = b[chunk - 1:chunk, :]
        st = st_ref[st_rows, :]

        qe = q_c * jnp.exp(b)
        o2 = _dot_nt(head_stack(qe), st.astype(BF16))
        o = jnp.concatenate([o2[h * chunk:(h + 1) * chunk, :] for h in range(GLA_HEADS)], axis=1)

        a_off = None
        if n_sub > 1:
            r_rows = [b[0:SUB, :]] + [jnp.broadcast_to(b[SUB * i - 1:SUB * i, :], (SUB, GLA_K))
                                      for i in range(1, n_sub)]
            r_q = jnp.concatenate(r_rows, axis=0)
            q_t = jnp.where(row_c >= SUB, q_c * jnp.exp(jnp.minimum(b - r_q, 0.0)), 0.0)
            k_parts = []
            for i in range(1, n_sub):
                r_i = b[SUB * i - 1:SUB * i, :]
                k_t = jnp.where(row_c < SUB * i, k_c * jnp.exp(jnp.minimum(r_i - b, 0.0)), 0.0)
                k_parts.append(k_t)
                k_parts.append(jnp.zeros((LANES - chunk, GLA_K), F32))
            k_cat = jnp.concatenate(k_parts, axis=0).astype(BF16)
            r = _dot_nt(head_stack(q_t), k_cat)
            a_off = r[:, 0:LANES]
            for i in range(2, n_sub):
                a_off = jnp.where(blk_s == i, r[:, (i - 1) * LANES:i * LANES], a_off)

        r_b = jnp.concatenate(
            [jnp.broadcast_to(b[SUB * i + DIAG - 1:SUB * i + DIAG, :], (SUB, GLA_K))
             for i in range(n_sub)], axis=0)
        q_b = jnp.where(second_half, q_c * jnp.exp(jnp.minimum(b - r_b, 0.0)), 0.0)
        k_b = jnp.where(second_half, 0.0, k_c * jnp.exp(jnp.minimum(r_b - b, 0.0)))
        k_b = jnp.concatenate([k_b, jnp.zeros((LANES - chunk, GLA_K), F32)], axis=0).astype(BF16)
        a_half = jnp.where(same_blk, _dot_nt(head_stack(q_b), k_b), 0.0)
        a_off = a_half if a_off is None else a_off + a_half

        q3 = q_c.reshape(chunk // DIAG, DIAG, GLA_K)
        k3 = k_c.reshape(chunk // DIAG, DIAG, GLA_K)
        b3 = b.reshape(chunk // DIAG, DIAG, GLA_K)
        ps = []
        for d in range(DIAG):
            k_s = k3 if d == 0 else pltpu.roll(k3, d, axis=1)
            b_s = b3 if d == 0 else pltpu.roll(b3, d, axis=1)
            p = jnp.where(row_g >= d, q3 * k_s * jnp.exp(jnp.minimum(b3 - b_s, 0.0)), 0.0)
            ps.append(p.reshape(chunk, GLA_K).astype(BF16))
        w_all = _dot(jnp.concatenate(ps, axis=0), e2)
        a_d = jnp.zeros((chunk, GLA_V), F32)
        for d in range(DIAG):
            a_d = jnp.where(lane_v == row_c - d, w_all[d * chunk:(d + 1) * chunk, :], a_d)
        a_off = a_off + jnp.concatenate(
            [a_d[:, h * GLA_DV:(h + 1) * GLA_DV] for h in range(GLA_HEADS)], axis=0)
        o1 = _dot(a_off[:, 0:chunk].astype(BF16), v_b)
        o = o + unstack(o1, GLA_DV)

        ke = k_c * jnp.exp(b_last - b)
        r2 = _dot_tn(v_b, ke.astype(BF16))
        upd = jnp.zeros((GLA_DV, GLA_K), F32)
        for h in range(GLA_HEADS):
            upd = upd + jnp.where(lane_k == h, r2[h * GLA_DV:(h + 1) * GLA_DV, :], 0.0)
        st_ref[st_rows, :] = st * jnp.exp(b_last) + upd

        g_c = gbuf[pl.ds(base, chunk), :]
        outs = []
        for h in range(GLA_HEADS):
            oh = o[:, h * GLA_DV:(h + 1) * GLA_DV]
            ms = jnp.mean(oh * oh, axis=-1, keepdims=True)
            outs.append(oh * lax.rsqrt(ms + LN_EPS))
        on = jnp.concatenate(outs, axis=1) * gng_ref[...]
        on = on * (g_c * _sigmoid(g_c))
        mixbuf[pl.ds(base, chunk), CONV_CH:MIX_W] = on.astype(BF16)
        return carry

    def trip(stream, i, carry, extra_jobs=()):
        for j in range(max(per_trip, loop_blocks, len(extra_jobs))):
            if j < len(extra_jobs):
                extra_jobs[j]((i + 1) * trip_rows, trip_rows)
            if j < loop_blocks:
                r0 = (pre_blocks + i * loop_blocks + j) * cb
                conv_block(stream, r0 if isinstance(r0, int) else pl.multiple_of(r0, cb))
            if j < per_trip:
                carry = chunk_step(stream, i * per_trip + j, carry)
        return carry

    def merge_rows(r0, n):
        y = _dot(mixbuf[r0:r0 + n, :], w_o_ref[...])
        x1 = _layer_norm(DN_ALPHA * x[r0:r0 + n, :] + y, ln1g_ref[...], ln1b_ref[...])
        x1_ref[r0:r0 + n, :] = x1
        for j, seg in enumerate(_pack_segments(x1)):
            x1p_ref[j, r0:r0 + n, :] = seg
        x_hi, x_lo = _split_bf16(x1)
        lg = _dot(x_hi, wr_ref[...])
        logit_ref[r0:r0 + n, :] = (lg[:, 0:LANES] + lg[:, LANES:2 * LANES]
                                   + _dot(x_lo, wr_ref[:, 0:LANES]) + br_ref[...])

    if static_trips:
        for i in range(n_trips):
            trip(0, i, 0, tuple(proj_jobs) if i + 1 < n_trips else ())
            merge_rows(i * trip_rows, trip_rows)
    else:
        for s in range(sb):
            if n_trips == 1:
                trip(s, 0, 0)
            else:
                lax.fori_loop(0, n_trips, functools.partial(trip, s), 0)
        merge_rows(0, n_rows)

    tails = []
    for s in range(sb):
        tail = ubuf[s * us + tm:s * us + tm + HALO, :]
        ubuf[s * us:s * us + HALO, :] = tail
        tails.append(tail)

    @pl.when(t == nt - 1)
    def _state_out():
        for s in range(sb):
            conv_out_ref[s] = tails[s][HALO - (CONV_W - 1):, :]
            st = st_ref[s * GLA_DV:(s + 1) * GLA_DV, :]
            for h in range(GLA_HEADS):
                gla_out_ref[s, h] = st[:, h * GLA_DK:(h + 1) * GLA_DK].T


def _mixer(x, conv_state, gla_state, p, *, tm, n_tok_total, tok_offset, token_bufs=None, sb=1):
    bsz, seq, _ = x.shape
    chunk = min(CHUNK, seq)
    rows = sb * tm
    assert seq % tm == 0 and tm % chunk == 0 and chunk % SUB == 0 and tok_offset % rows == 0
    assert bsz % sb == 0 and (sb == 1 or seq == tm)
    nt = seq // tm
    blk0 = tok_offset // rows
    full = lambda shape: pl.BlockSpec(shape, lambda b, t: (0,) * len(shape))
    alias = () if token_bufs is None else tuple(token_bufs)
    kern = functools.partial(_mixer_kernel, tm=tm, chunk=chunk, n_alias=len(alias), sb=sb)
    n_in = 16
    return pl.pallas_call(
        kern,
        grid=(bsz // sb, nt),
        input_output_aliases={n_in + i: i for i in range(len(alias))},
        in_specs=[
            pl.BlockSpec((sb, tm, D_MODEL), lambda b, t: (b, t, 0)),
            pl.BlockSpec((sb, CONV_W - 1, CONV_CH), lambda b, t: (b, 0, 0)),
            pl.BlockSpec((sb, GLA_HEADS, GLA_DK, GLA_DV), lambda b, t: (b, 0, 0, 0)),
            full((D_MODEL, IN_COLS_PAD)),
            full((8 * CONV_W, CONV_CH)),
            full((1, CONV_CH)),
            full((1, CONV_CH)),
            full((1, CONV_CH)),
            full((GATE_PAD, GLA_K)),
            full((1, GLA_K)),
            full((1, GLA_V)),
            full((MIX_W, D_MODEL)),
            full((1, D_MODEL)),
            full((1, D_MODEL)),
            full((D_MODEL, 2 * LANES)),
            full((1, LANES)),
        ] + [pl.BlockSpec(memory_space=pl.ANY)] * len(alias),
        out_specs=[
            pl.BlockSpec((N_SEG, rows, SC_SEG), lambda b, t: (0, blk0 + b * nt + t, 0)),
            pl.BlockSpec((rows, D_MODEL), lambda b, t: (blk0 + b * nt + t, 0)),
            pl.BlockSpec((rows, LANES), lambda b, t: (blk0 + b * nt + t, 0)),
            pl.BlockSpec((sb, CONV_W - 1, CONV_CH), lambda b, t: (b, 0, 0)),
            pl.BlockSpec((sb, GLA_HEADS, GLA_DK, GLA_DV), lambda b, t: (b, 0, 0, 0)),
        ],
        out_shape=[
            jax.ShapeDtypeStruct((N_SEG, n_tok_total, SC_SEG), U32),
            jax.ShapeDtypeStruct((n_tok_total, D_MODEL), F32),
            jax.ShapeDtypeStruct((n_tok_total, LANES), F32),
            jax.ShapeDtypeStruct((bsz, CONV_W - 1, CONV_CH), F32),
            jax.ShapeDtypeStruct((bsz, GLA_HEADS, GLA_DK, GLA_DV), F32),
        ],
        scratch_shapes=[
            pltpu.VMEM((sb * (HALO + tm), CONV_CH), F32),
            pltpu.VMEM((rows, GLA_K), F32),
            pltpu.VMEM((rows, GLA_K), F32),
            pltpu.VMEM((rows, GLA_K), F32),
            pltpu.VMEM((rows, GLA_V), F32),
            pltpu.VMEM((rows, GLA_V), F32),
            pltpu.VMEM((rows, MIX_W), BF16),
            pltpu.VMEM((sb * GLA_DV, GLA_K), F32),
        ],
        compiler_params=pltpu.CompilerParams(
            dimension_semantics=("arbitrary", "arbitrary"),
            vmem_limit_bytes=VMEM_LIMIT),
        name="mixer",
    )(x, conv_state, gla_state, p["w_in"], p["w_dw"], p["b_dw"], p["ln_conv_g"], p["ln_conv_b"],
      p["w_gate"], p["b_gate"], p["gla_norm_g"], p["w_o"], p["ln1_g"], p["ln1_b"],
      p["w_router"], p["b_router"], *alias)


def _route_kernel(lg_ref, ri_ref, rg_ref, cnt_ref, carry_ref, *, tr):
    i = pl.program_id(0)

    @pl.when(i == 0)
    def _init():
        carry_ref[...] = jnp.zeros((N_EXPERTS, LANES), F32)

    l = lg_ref[...].T[0:N_EXPERTS, :]
    row = lax.broadcasted_iota(jnp.int32, (N_EXPERTS, tr), 0)
    hots, vals, idxs = [], [], []
    for _ in range(TOP_K):
        m = jnp.max(l, axis=0, keepdims=True)
        ik = jnp.min(jnp.where(l == m, row, N_EXPERTS), axis=0, keepdims=True)
        hot = row == ik
        hots.append(hot)
        vals.append(m)
        idxs.append(ik)
        l = jnp.where(hot, -jnp.inf, l)
    es = [jnp.exp(v - vals[0]) for v in vals]
    denom = es[0] + es[1] + es[2] + es[3]
    member = jnp.zeros((N_EXPERTS, tr), F32)
    for hot in hots:
        member = member + hot.astype(F32)
    before = (lax.broadcasted_iota(jnp.int32, (tr, tr), 0)
              < lax.broadcasted_iota(jnp.int32, (tr, tr), 1)).astype(BF16)
    cum = _dot(member.astype(BF16), before) + carry_ref[:, 0:1]
    ranks = [jnp.sum(jnp.where(hot, cum, 0.0), axis=0, keepdims=True).astype(jnp.int32)
             for hot in hots]
    ri_ref[...] = jnp.concatenate(idxs + ranks, axis=0)
    gates = jnp.concatenate([e / denom for e in es] + [jnp.zeros((LANES - TOP_K, tr), F32)], axis=0)
    rg_ref[...] = gates.T
    carry_ref[...] = carry_ref[...] + jnp.sum(member, axis=1, keepdims=True)

    @pl.when(i == pl.num_programs(0) - 1)
    def _fin():
        cnt_ref[...] = carry_ref[...]


def _route(logits, *, tr):
    n_tok = logits.shape[0]
    assert n_tok % tr == 0
    return pl.pallas_call(
        functools.partial(_route_kernel, tr=tr),
        grid=(n_tok // tr,),
        in_specs=[pl.BlockSpec((tr, LANES), lambda i: (i, 0))],
        out_specs=[pl.BlockSpec((2 * TOP_K, tr), lambda i: (0, i)),
                   pl.BlockSpec((tr, LANES), lambda i: (i, 0)),
                   pl.BlockSpec((N_EXPERTS, LANES), lambda i: (0, 0))],
        out_shape=[jax.ShapeDtypeStruct((2 * TOP_K, n_tok), jnp.int32),
                   jax.ShapeDtypeStruct((n_tok, LANES), F32),
                   jax.ShapeDtypeStruct((N_EXPERTS, LANES), F32)],
        scratch_shapes=[pltpu.VMEM((N_EXPERTS, LANES), F32)],
        compiler_params=pltpu.CompilerParams(dimension_semantics=("arbitrary",)),
        name="route",
    )(logits)


def _ffn_kernel(gs_ref, nu_ref, xs_hbm, wgu_ref, bgu_ref, wdn_ref, bdn_ref, yb_hbm,
                wgu_bf, wdn_bf, xin, yout, in_sem, out_sem):
    e = pl.program_id(0)
    n_used = nu_ref[0]
    g_first = gs_ref[e]
    g_end = gs_ref[e + 1]

    def in_copy(g, slot):
        rows = pl.ds(pl.multiple_of(g * EXPERT_BLOCK, EXPERT_BLOCK), EXPERT_BLOCK)
        return pltpu.make_async_copy(xs_hbm.at[:, rows, :], xin.at[slot], in_sem.at[slot])

    def out_copy(g, slot):
        rows = pl.ds(pl.multiple_of(g * EXPERT_BLOCK, EXPERT_BLOCK), EXPERT_BLOCK)
        return pltpu.make_async_copy(yout.at[slot], yb_hbm.at[:, rows, :], out_sem.at[slot])

    @pl.when(e == 0)
    def _prime():
        for g in range(IN_AHEAD):
            @pl.when(g < n_used)
            def _():
                in_copy(g, g).start()

    def cast_weights():
        wgu_bf[...] = wgu_ref[0].astype(BF16)
        wdn_bf[...] = wdn_ref[0].astype(BF16)

    def blocks(g0, n, with_cast=False):
        for j in range(n):
            g = g0 + j
            in_copy(g, g % IN_RING).wait()
        for j in range(n):
            g = g0 + IN_AHEAD + j

            @pl.when(g < n_used)
            def _prefetch():
                in_copy(g, g % IN_RING).start()
        for j in range(n):
            g = g0 + j

            @pl.when(g >= OUT_RING)
            def _free_out():
                out_copy(g - OUT_RING, g % OUT_RING).wait()
        if with_cast:
            cast_weights()
        for j in range(n):
            g = g0 + j
            x_lo, x_hi = _unpack_segments([xin[g % IN_RING, s] for s in range(N_SEG)])
            x_bf = jnp.concatenate([x_lo.astype(BF16), x_hi.astype(BF16)], axis=1)
            hgu = _dot(x_bf, wgu_bf[...]) + bgu_ref[0]
            gate = jnp.minimum(hgu[:, :D_FF], SWIGLU_LIMIT)
            up = jnp.clip(hgu[:, D_FF:], -SWIGLU_LIMIT, SWIGLU_LIMIT)
            act = (up + 1.0) * gate * _sigmoid(SWIGLU_ALPHA * gate)
            yb = _dot(act.astype(BF16), wdn_bf[...]) + bdn_ref[0]
            for s, seg in enumerate(_pack_segments(yb)):
                yout[g % OUT_RING, s] = seg
        for j in range(n):
            g = g0 + j
            rows = pl.ds(pl.multiple_of(g * EXPERT_BLOCK, EXPERT_BLOCK), EXPERT_BLOCK)
            pltpu.async_copy(yout.at[g % OUT_RING], yb_hbm.at[:, rows, :],
                             out_sem.at[g % OUT_RING], priority=1)

    n_blocks = g_end - g_first
    n_full = n_blocks // TRIP_BLOCKS

    @pl.when((n_full == 0) & (n_blocks > 0))
    def _cast_only():
        cast_weights()

    @pl.when(n_full >= 1)
    def _first_trip():
        blocks(g_first, TRIP_BLOCKS, with_cast=True)

    def full_trip(i, carry):
        blocks(g_first + TRIP_BLOCKS * i, TRIP_BLOCKS)
        return carry

    lax.fori_loop(1, n_full, full_trip, 0)
    done = n_full * TRIP_BLOCKS
    size = TRIP_BLOCKS // 2
    while size >= 1:
        @pl.when((n_blocks // size) % 2 == 1)
        def _rest(done=done, size=size):
            blocks(g_first + done, size)
        done = done + jnp.where((n_blocks // size) % 2 == 1, size, 0)
        size //= 2

    @pl.when(e == pl.num_programs(0) - 1)
    def _drain():
        for back in range(OUT_RING, 0, -1):
            @pl.when(n_used >= back)
            def _():
                out_copy(n_used - back, (n_used - back) % OUT_RING).wait()


def _ffn(block_start, n_used, xs, w_gu, b_gu, w_down, b_down):
    n_rows = xs.shape[1]
    grid_spec = pltpu.PrefetchScalarGridSpec(
        num_scalar_prefetch=2,
        grid=(N_EXPERTS,),
        in_specs=[
            pl.BlockSpec(memory_space=pl.ANY),
            pl.BlockSpec((1, D_MODEL, 2 * D_FF), lambda e, gs, nu: (e, 0, 0)),
            pl.BlockSpec((1, 1, 2 * D_FF), lambda e, gs, nu: (e, 0, 0)),
            pl.BlockSpec((1, D_FF, D_MODEL), lambda e, gs, nu: (e, 0, 0)),
            pl.BlockSpec((1, 1, D_MODEL), lambda e, gs, nu: (e, 0, 0)),
        ],
        out_specs=pl.BlockSpec(memory_space=pl.ANY),
        scratch_shapes=[pltpu.VMEM((D_MODEL, 2 * D_FF), BF16),
                        pltpu.VMEM((D_FF, D_MODEL), BF16),
                        pltpu.VMEM((IN_RING, N_SEG, EXPERT_BLOCK, SC_SEG), U32),
                        pltpu.VMEM((OUT_RING, N_SEG, EXPERT_BLOCK, SC_SEG), U32),
                        pltpu.SemaphoreType.DMA((IN_RING,)),
                        pltpu.SemaphoreType.DMA((OUT_RING,))],
    )
    return pl.pallas_call(
        _ffn_kernel,
        grid_spec=grid_spec,
        out_shape=jax.ShapeDtypeStruct((N_SEG, n_rows, SC_SEG), U32),
        compiler_params=pltpu.CompilerParams(
            dimension_semantics=("arbitrary",), vmem_limit_bytes=VMEM_LIMIT),
        name="expert_ffn",
    )(block_start, n_used, xs, w_gu, b_gu.reshape(N_EXPERTS, 1, 2 * D_FF), w_down,
      b_down.reshape(N_EXPERTS, 1, D_MODEL))


def _combine_kernel(x1_ref, g_ref, rg_ref, ln2g_ref, ln2b_ref, *rest, blk0, n_prompt_blocks,
                    n_alias, with_sample):
    yp_ref = rest[n_alias]
    i = blk0 + pl.program_id(0)
    m_lo = jnp.zeros((x1_ref.shape[0], HALF), F32)
    m_hi = jnp.zeros((x1_ref.shape[0], HALF), F32)
    for k in range(TOP_K):
        g_lo, g_hi = _unpack_segments([g_ref[k, j] for j in range(N_SEG)])
        gate = rg_ref[:, k:k + 1]
        m_lo = m_lo + gate * g_lo
        m_hi = m_hi + gate * g_hi
    z = DN_ALPHA * x1_ref[...] + jnp.concatenate([m_lo, m_hi], axis=1)
    y = _layer_norm(z, ln2g_ref[...], ln2b_ref[...])

    if with_sample:
        ys_ref = rest[n_alias + 1]

        @pl.when(i < n_prompt_blocks)
        def _p():
            yp_ref[...] = y

        @pl.when(i >= n_prompt_blocks)
        def _s():
            ys_ref[...] = y
    else:
        yp_ref[...] = y


def _combine(x1, g, rg, ln2_g, ln2_b, *, tok0, n_prompt, tf, yp_buf=None):
    n_tok = x1.shape[0]
    piece = g.shape[2]
    n_sample = n_tok - n_prompt
    assert n_prompt % tf == 0 and tok0 % tf == 0 and piece % tf == 0
    npb = n_prompt // tf
    blk0 = tok0 // tf
    with_sample = tok0 + piece > n_prompt
    assert not with_sample or (tok0 + piece == n_tok and n_sample % tf == 0)
    alias = () if yp_buf is None else (yp_buf,)
    out_specs = [pl.BlockSpec((tf, D_MODEL), lambda i: (jnp.minimum(blk0 + i, npb - 1), 0))]
    out_shape = [jax.ShapeDtypeStruct((n_prompt, D_MODEL), F32)]
    if with_sample:
        out_specs.append(pl.BlockSpec((tf, D_MODEL), lambda i: (jnp.maximum(blk0 + i - npb, 0), 0)))
        out_shape.append(jax.ShapeDtypeStruct((n_sample, D_MODEL), F32))
    n_in = 5
    return pl.pallas_call(
        functools.partial(_combine_kernel, blk0=blk0, n_prompt_blocks=npb, n_alias=len(alias),
                          with_sample=with_sample),
        grid=(piece // tf,),
        input_output_aliases={n_in + i: i for i in range(len(alias))},
        in_specs=[pl.BlockSpec((tf, D_MODEL), lambda i: (blk0 + i, 0)),
                  pl.BlockSpec((TOP_K, N_SEG, tf, SC_SEG), lambda i: (0, 0, i, 0)),
                  pl.BlockSpec((tf, LANES), lambda i: (blk0 + i, 0)),
                  pl.BlockSpec((1, D_MODEL), lambda i: (0, 0)),
                  pl.BlockSpec((1, D_MODEL), lambda i: (0, 0))]
        + [pl.BlockSpec(memory_space=pl.ANY)] * len(alias),
        out_specs=out_specs,
        out_shape=out_shape,
        compiler_params=pltpu.CompilerParams(
            dimension_semantics=("arbitrary",), vmem_limit_bytes=VMEM_LIMIT),
        name="combine",
    )(x1, g, rg, ln2_g, ln2_b, *alias)


def _sc_mesh():
    return plsc.VectorSubcoreMesh(core_axis_name="c", subcore_axis_name="s")


def _sc_scatter_rows(src, dest, n_out):
    n_src, width = src.shape
    n_k = dest.shape[0]
    assert width == SC_SEG and n_src % SC_WINDOW == 0 and dest.shape[1] == n_src

    @pl.kernel(out_type=jax.ShapeDtypeStruct((n_out, width), src.dtype), mesh=_sc_mesh(),
               scratch_types=[])
    def scatter_kernel(src_hbm, idx_hbm, out_hbm):
        def body(src_vmem, idx_vmem):
            for k in range(n_k):
                pltpu.sync_copy(src_vmem, out_hbm.at[idx_vmem.at[k]])

        pltpu.emit_pipeline(
            body, grid=(n_src // SC_WINDOW,),
            in_specs=[pl.BlockSpec((SC_WINDOW, width), lambda i: (i, 0)),
                      pl.BlockSpec((n_k, SC_WINDOW), lambda i: (0, i))],
            out_specs=[],
            core_axis_name=("c", "s"),
            dimension_semantics=(pltpu.PARALLEL,),
        )(src_hbm, idx_hbm)

    return scatter_kernel(src, dest)


def _sc_gather_rows(src, idx):
    n_out = idx.shape[1]
    width = src.shape[1]
    assert width == SC_SEG and n_out % SC_WINDOW == 0

    @pl.kernel(out_type=jax.ShapeDtypeStruct((n_out, width), src.dtype), mesh=_sc_mesh(),
               scratch_types=[])
    def gather_kernel(src_hbm, idx_hbm, out_hbm):
        def body(idx_vmem, out_vmem):
            pltpu.sync_copy(src_hbm.at[idx_vmem.at[0]], out_vmem)

        pltpu.emit_pipeline(
            body, grid=(n_out // SC_WINDOW,),
            in_specs=[pl.BlockSpec((1, SC_WINDOW), lambda i: (0, i))],
            out_specs=[pl.BlockSpec((SC_WINDOW, width), lambda i: (i, 0))],
            core_axis_name=("c", "s"),
            dimension_semantics=(pltpu.PARALLEL,),
        )(idx_hbm, out_hbm)

    return gather_kernel(src, idx)


def _prep_params(w_in, w_dw, b_dw, ln_conv_g, ln_conv_b, w_gate_lr, b_gate, gla_norm_g, w_o,
                 ln1_g, ln1_b, w_router, b_router):
    row = lambda v: v.reshape(1, -1).astype(F32)
    w_in_p = jnp.pad(w_in, ((0, 0), (0, IN_COLS_PAD - w_in.shape[1]))).astype(BF16)
    w_gate = jnp.pad(w_gate_lr, ((0, GATE_PAD - GATE_RANK), (0, 0))).astype(BF16)
    wr_hi = w_router.astype(BF16)
    wr_lo = (w_router - wr_hi.astype(F32)).astype(BF16)
    padr = lambda m: jnp.pad(m, ((0, 0), (0, LANES - N_EXPERTS)))
    return {
        "w_in": w_in_p, "w_dw": jnp.repeat(w_dw, 8, axis=0), "b_dw": row(b_dw), "ln_conv_g": row(ln_conv_g),
        "ln_conv_b": row(ln_conv_b), "w_gate": w_gate, "b_gate": row(b_gate),
        "gla_norm_g": row(gla_norm_g), "w_o": w_o.astype(BF16), "ln1_g": row(ln1_g),
        "ln1_b": row(ln1_b), "w_router": jnp.concatenate([padr(wr_hi), padr(wr_lo)], axis=1),
        "b_router": jnp.pad(row(b_router), ((0, 0), (0, LANES - N_EXPERTS))),
    }


def _dest_kernel(ps_ref, ri_ref, out_ref, *, n_slot):
    idx = ri_ref[0:TOP_K, :]
    dest = ri_ref[TOP_K:2 * TOP_K, :] + pl.program_id(0) * n_slot
    for e in range(N_EXPERTS):
        dest = dest + jnp.where(idx == e, ps_ref[e], 0)
    out_ref[...] = dest


def _moe_layout(ri, counts, *, tr):
    n_tok = ri.shape[1]
    cnt = counts[:, 0].astype(jnp.int32)
    padded = (cnt + EXPERT_BLOCK - 1) // EXPERT_BLOCK * EXPERT_BLOCK
    pad_end = jnp.cumsum(padded)
    pad_start = pad_end - padded
    nb = -(-(n_tok * TOP_K) // EXPERT_BLOCK) + N_EXPERTS
    n_t = n_tok // tr
    dest_seg = pl.pallas_call(
        functools.partial(_dest_kernel, n_slot=nb * EXPERT_BLOCK),
        grid_spec=pltpu.PrefetchScalarGridSpec(
            num_scalar_prefetch=1,
            grid=(N_SEG, n_t),
            in_specs=[pl.BlockSpec((2 * TOP_K, tr), lambda j, i, ps: (0, i))],
            out_specs=pl.BlockSpec((TOP_K, tr), lambda j, i, ps: (0, j * n_t + i)),
        ),
        out_shape=jax.ShapeDtypeStruct((TOP_K, N_SEG * n_tok), jnp.int32),
        name="slot_index",
    )(pad_start.astype(jnp.int32), ri)
    block_start = (jnp.concatenate([pad_start, pad_end[-1:]]) // EXPERT_BLOCK).astype(jnp.int32)
    n_used = block_start[-1:]
    return dest_seg, block_start, n_used, nb


def kernel(x_prompt, x_sample, state_conv, state_gla, w_in, w_dw, b_dw, ln_conv_g, ln_conv_b,
           w_gate_lr, b_gate, gla_norm_g, w_o, ln1_g, ln1_b, w_router, b_router, w_gu, b_gu,
           w_down, b_down, ln2_g, ln2_b):
    assert w_in.shape[0] == DEPTH
    l = 0
    p = _prep_params(w_in[l], w_dw[l], b_dw[l], ln_conv_g[l], ln_conv_b[l], w_gate_lr[l],
                     b_gate[l], gla_norm_g[l], w_o[l], ln1_g[l], ln1_b[l], w_router[l],
                     b_router[l])
    bp, tp, _ = x_prompt.shape
    bs, ts, _ = x_sample.shape
    zc = jnp.zeros((bp, CONV_W - 1, CONV_CH), F32)
    zs = jnp.zeros((bp, GLA_HEADS, GLA_DK, GLA_DV), F32)
    n_prompt = bp * tp
    n_tok = n_prompt + bs * ts
    x1p, x1, lg, conv_p, gla_p = _mixer(x_prompt, zc, zs, p, tm=PROMPT_TILE, n_tok_total=n_tok,
                                        tok_offset=0)
    x1p, x1, lg, conv_s, gla_s = _mixer(x_sample, state_conv[l], state_gla[l], p, tm=ts,
                                        n_tok_total=n_tok, tok_offset=n_prompt,
                                        token_bufs=(x1p, x1, lg),
                                        sb=SAMPLE_STREAMS if bs % SAMPLE_STREAMS == 0 else 1)

    tr = ROUTE_TILE if n_tok % ROUTE_TILE == 0 else bs * ts
    ri, rg, counts = _route(lg, tr=tr)
    dest_seg, block_start, n_used, nb = _moe_layout(ri, counts, tr=n_tok)
    n_slot = nb * EXPERT_BLOCK
    xs = _sc_scatter_rows(x1p.reshape(N_SEG * n_tok, SC_SEG), dest_seg, N_SEG * n_slot)
    yb = _ffn(block_start, n_used, xs.reshape(N_SEG, n_slot, SC_SEG), w_gu[l], b_gu[l], w_down[l],
              b_down[l])
    tf_last = bs * ts
    n_blk = n_prompt // tf_last
    last_blk = min(LAST_PIECE_BLOCKS, n_blk // 2)
    body_tok = (n_blk - last_blk) * tf_last
    tf_body = 2 * tf_last if body_tok % (2 * tf_last) == 0 else tf_last
    body_blk = body_tok // tf_body
    n_body = min(COMBINE_PIECES - 1, body_blk)
    bounds = [(p * body_blk // n_body) * tf_body for p in range(n_body)] + [body_tok, n_tok]
    dest_kjt = dest_seg.reshape(TOP_K, N_SEG, n_tok)
    yp = None
    for t0, t1 in zip(bounds[:-1], bounds[1:]):
        g = _sc_gather_rows(yb.reshape(N_SEG * n_slot, SC_SEG),
                            dest_kjt[:, :, t0:t1].reshape(1, -1))
        outs = _combine(x1, g.reshape(TOP_K, N_SEG, t1 - t0, SC_SEG), rg,
                        ln2_g[l].reshape(1, -1), ln2_b[l].reshape(1, -1), tok0=t0,
                        n_prompt=n_prompt, tf=tf_last if t1 == n_tok else tf_body, yp_buf=yp)
        yp = outs[0]
    ys = outs[1]
    return (yp.reshape(bp, tp, D_MODEL), ys.reshape(bs, ts, D_MODEL),
            conv_p[None], gla_p[None], conv_s[None], gla_s[None])
```
